```python
import jax, jax.numpy as jnp
from jax import lax
import numpy as np

D_MODEL = 1024
BATCH = 8
SEQ = 4096
DEPTH = 2

PLE_DIM = 256
MIX_WIDTH = D_MODEL
POOL_WIDTH = MIX_WIDTH // 2
POOL_WINDOWS = (2, 4, 8, 16)
N_POOL_GROUPS = len(POOL_WINDOWS)
POOL_GROUP_DIM = POOL_WIDTH // N_POOL_GROUPS
SB_WIDTH = MIX_WIDTH - POOL_WIDTH
SB_HEAD_DIM = 64
SB_HEADS = SB_WIDTH // SB_HEAD_DIM
SB_BLOCK = 128
GDN_HEAD_DIM = 128
GDN_HEADS = MIX_WIDTH // GDN_HEAD_DIM
GDN_CONV = 4
GDN_CHUNK = 64
FFN_DIM = 2816
FFN_CONV = 3
EPS = 1e-6
N_EVEN = (DEPTH + 1) // 2
N_ODD = DEPTH // 2
EVEN_IN = POOL_WIDTH + 3 * SB_WIDTH
ODD_IN = 4 * MIX_WIDTH + 2 * GDN_HEADS

kernel_name = 'hybrid_pool_stickbreak_gdn_convffn_ple'


def rmsnorm(x, gain):
    xf = x.astype(jnp.float32)
    y = xf * lax.rsqrt(jnp.mean(xf * xf, axis=-1, keepdims=True) + EPS)
    return (y * gain.astype(jnp.float32)).astype(x.dtype)


def l2norm(x):
    return x * lax.rsqrt(jnp.sum(x * x, axis=-1, keepdims=True) + EPS)


def causal_dwconv(x, w):
    K = w.shape[0]
    T = x.shape[1]
    xp = jnp.pad(x, ((0, 0), (K - 1, 0), (0, 0)))
    return sum(xp[:, i:i + T] * w[i] for i in range(K))


def pool_mixer(u, pool_w, pool_scale):
    B, T, _ = u.shape
    ug = u.reshape(B, T, N_POOL_GROUPS, POOL_GROUP_DIM).astype(jnp.float32)
    cs = jnp.pad(jnp.cumsum(ug, axis=1), ((0, 0), (1, 0), (0, 0), (0, 0)))
    t = jnp.arange(T)
    win = jnp.array(POOL_WINDOWS, dtype=jnp.int32)
    start = jnp.maximum(t[:, None] + 1 - win[None, :], 0)
    g_idx = jnp.arange(N_POOL_GROUPS)[None, :]
    window_sum = cs[:, 1:] - cs[:, start, g_idx]
    count = (t[:, None] + 1 - start).astype(jnp.float32)
    y = window_sum / count[None, :, :, None] - ug
    y = jnp.einsum('btgc,gcd->btgd', y, pool_w.astype(jnp.float32))
    return (y.reshape(B, T, POOL_WIDTH) * pool_scale.astype(jnp.float32)).astype(u.dtype)


def stick_breaking_attention(q, k, v):
    T = q.shape[2]
    scale = SB_HEAD_DIM ** -0.5
    vf = v.astype(jnp.float32)
    outs = []
    for blk in range(T // SB_BLOCK):
        q0 = blk * SB_BLOCK
        end = q0 + SB_BLOCK
        z = jnp.einsum('bhqd,bhkd->bhqk', q[:, :, q0:end], k[:, :, :end]).astype(jnp.float32) * scale
        q_pos = q0 + jnp.arange(SB_BLOCK)
        k_pos = jnp.arange(end)
        valid = k_pos[None, :] < q_pos[:, None]
        log_1m = jnp.where(valid, jax.nn.log_sigmoid(-z), 0.0)
        log_keep = lax.cumsum(log_1m, axis=3, reverse=True) - log_1m
        a = jnp.where(valid, jnp.exp(jax.nn.log_sigmoid(z) + log_keep), 0.0)
        outs.append(jnp.einsum('bhqk,bhkd->bhqd', a, vf[:, :, :end]))
    return jnp.concatenate(outs, axis=2).astype(v.dtype)


def gated_delta_rule_chunked(q, k, v, g, beta):
    B, H, T, dk = q.shape
    dv = v.shape[-1]
    n = T // GDN_CHUNK
    q = q * dk ** -0.5

    def chunks(a):
        return a.reshape(B, H, n, GDN_CHUNK, *a.shape[3:])

    q, k, v, g, beta = chunks(q), chunks(k), chunks(v), chunks(g), chunks(beta)
    gc = jnp.cumsum(g, axis=-1)
    idx = jnp.arange(GDN_CHUNK)
    incl = idx[:, None] >= idx[None, :]
    strict = idx[:, None] > idx[None, :]
    decay = jnp.where(incl, jnp.exp(jnp.where(incl, gc[..., :, None] - gc[..., None, :], 0.0)), 0.0)
    k_beta = k * beta[..., None]
    a_mat = jnp.where(strict, jnp.einsum('bhncd,bhnsd->bhncs', k_beta, k) * decay, 0.0)
    rhs = jnp.concatenate([v * beta[..., None], k_beta * jnp.exp(gc)[..., None]], axis=-1)
    sol = lax.linalg.triangular_solve(jnp.eye(GDN_CHUNK, dtype=a_mat.dtype) + a_mat, rhs,
                                      left_side=True, lower=True, unit_diagonal=True)
    u, w = sol[..., :dv], sol[..., dv:]
    qk = jnp.einsum('bhncd,bhnsd->bhncs', q, k) * decay
    q_dec = q * jnp.exp(gc)[..., None]
    k_dec = k * jnp.exp(gc[..., -1:] - gc)[..., None]
    g_last = jnp.exp(gc[..., -1])
    xs = (jnp.moveaxis(qk, 2, 0), jnp.moveaxis(u, 2, 0), jnp.moveaxis(w, 2, 0),
          jnp.moveaxis(q_dec, 2, 0), jnp.moveaxis(k_dec, 2, 0), jnp.moveaxis(g_last, 2, 0))

    def step(state, inp):
        qk_c, u_c, w_c, q_c, k_c, gl = inp
        v_new = u_c - jnp.einsum('bhcd,bhde->bhce', w_c, state)
        o = jnp.einsum('bhcd,bhde->bhce', q_c, state) + jnp.einsum('bhcs,bhse->bhce', qk_c, v_new)
        state = state * gl[..., None, None] + jnp.einsum('bhcd,bhce->bhde', k_c, v_new)
        return state, o

    s0 = jnp.zeros((B, H, dk, dv), jnp.float32)
    _, o = lax.scan(step, s0, xs)
    return jnp.moveaxis(o, 0, 2).reshape(B, H, T, dv)


def even_mixer(h, w_in, pool_w, pool_scale, w_out):
    B, T, _ = h.shape
    proj = h @ w_in
    u, q, k, v = jnp.split(proj, [POOL_WIDTH, POOL_WIDTH + SB_WIDTH, POOL_WIDTH + 2 * SB_WIDTH], axis=-1)
    pool_out = pool_mixer(u, pool_w, pool_scale)

    def heads(a):
        return a.reshape(B, T, SB_HEADS, SB_HEAD_DIM).transpose(0, 2, 1, 3)

    attn = stick_breaking_attention(heads(q), heads(k), heads(v))
    attn = attn.transpose(0, 2, 1, 3).reshape(B, T, SB_WIDTH)
    return jnp.concatenate([pool_out, attn], axis=-1) @ w_out


def odd_mixer(h, w_in, conv_w, a_log, dt_bias, norm_w, w_out):
    B, T, _ = h.shape
    proj = h @ w_in
    qkv, z, b, a = jnp.split(proj, [3 * MIX_WIDTH, 4 * MIX_WIDTH, 4 * MIX_WIDTH + GDN_HEADS], axis=-1)
    qkv = jax.nn.silu(causal_dwconv(qkv, conv_w))
    q, k, v = jnp.split(qkv, 3, axis=-1)

    def heads(x_):
        return x_.reshape(B, T, GDN_HEADS, GDN_HEAD_DIM).transpose(0, 2, 1, 3).astype(jnp.float32)

    q, k, v = l2norm(heads(q)), l2norm(heads(k)), heads(v)
    beta = jax.nn.sigmoid(b.astype(jnp.float32)).transpose(0, 2, 1)
    g = (-jnp.exp(a_log.astype(jnp.float32))
         * jax.nn.softplus(a.astype(jnp.float32) + dt_bias.astype(jnp.float32))).transpose(0, 2, 1)
    o = gated_delta_rule_chunked(q, k, v, g, beta).transpose(0, 2, 1, 3)
    o = o * lax.rsqrt(jnp.mean(o * o, axis=-1, keepdims=True) + EPS) * norm_w.astype(jnp.float32)
    o = o * jax.nn.silu(z.reshape(B, T, GDN_HEADS, GDN_HEAD_DIM).astype(jnp.float32))
    return o.reshape(B, T, MIX_WIDTH).astype(h.dtype) @ w_out


def conv_ffn(h, w_up, conv_w, w_down):
    up = causal_dwconv(h @ w_up, conv_w)
    gate, val = jnp.split(up, 2, axis=-1)
    return (jax.nn.silu(gate) * val) @ w_down


def _fwd_setup_inputs(seed: int = 0) -> dict:
    key = jax.random.key(seed)
    ks = jax.random.split(key, 24)
    f32 = jnp.float32

    def dense(k_, shape, fan_in):
        return jax.random.normal(k_, shape, f32) * fan_in ** -0.5

    def gain(k_, shape):
        return 1.0 + 0.02 * jax.random.normal(k_, shape, f32)

    dt = jnp.exp(jax.random.uniform(ks[11], (N_ODD, GDN_HEADS), f32, np.log(1e-3), np.log(1e-1)))
    return {
        'x': jax.random.normal(ks[0], (BATCH, SEQ, D_MODEL), f32),
        'p': jax.random.normal(ks[1], (DEPTH, BATCH, SEQ, PLE_DIM), f32),
        'mix_norm_e': gain(ks[2], (N_EVEN, D_MODEL)),
        'w_in_e': dense(ks[3], (N_EVEN, D_MODEL, EVEN_IN), D_MODEL),
        'pool_w': dense(ks[4], (N_EVEN, N_POOL_GROUPS, POOL_GROUP_DIM, POOL_GROUP_DIM), POOL_GROUP_DIM),
        'pool_scale': 1.0 + 0.1 * jax.random.normal(ks[5], (N_EVEN, POOL_WIDTH), f32),
        'w_out_e': dense(ks[6], (N_EVEN, MIX_WIDTH, D_MODEL), MIX_WIDTH),
        'mix_norm_o': gain(ks[7], (N_ODD, D_MODEL)),
        'w_in_o': dense(ks[8], (N_ODD, D_MODEL, ODD_IN), D_MODEL),
        'conv_qkv_o': dense(ks[9], (N_ODD, GDN_CONV, 3 * MIX_WIDTH), GDN_CONV),
        'a_log_o': jnp.log(jax.random.uniform(ks[10], (N_ODD, GDN_HEADS), f32, 1.0, 16.0)),
        'dt_bias_o': dt + jnp.log(-jnp.expm1(-dt)),
        'gdn_norm_o': gain(ks[12], (N_ODD, GDN_HEAD_DIM)),
        'w_out_o': dense(ks[13], (N_ODD, MIX_WIDTH, D_MODEL), MIX_WIDTH),
        'ffn_norm': gain(ks[14], (DEPTH, D_MODEL)),
        'w_up': dense(ks[15], (DEPTH, D_MODEL, 2 * FFN_DIM), D_MODEL),
        'ffn_conv': dense(ks[16], (DEPTH, FFN_CONV, 2 * FFN_DIM), FFN_CONV),
        'w_down': dense(ks[17], (DEPTH, FFN_DIM, D_MODEL), FFN_DIM),
        'ple_norm': gain(ks[18], (DEPTH, D_MODEL)),
        'w_ple_gate': dense(ks[19], (DEPTH, D_MODEL, D_MODEL), D_MODEL),
        'w_ple': dense(ks[20], (DEPTH, PLE_DIM, D_MODEL), PLE_DIM),
        'final_norm': gain(ks[21], (D_MODEL,)),
    }


def _fwd_reference(x, p, mix_norm_e, w_in_e, pool_w, pool_scale, w_out_e,
              mix_norm_o, w_in_o, conv_qkv_o, a_log_o, dt_bias_o, gdn_norm_o, w_out_o,
              ffn_norm, w_up, ffn_conv, w_down, ple_norm, w_ple_gate, w_ple, final_norm):
    for i in range(DEPTH):
        j = i // 2
        if i % 2 == 0:
            x = x + even_mixer(rmsnorm(x, mix_norm_e[j]), w_in_e[j], pool_w[j], pool_scale[j], w_out_e[j])
        else:
            x = x + odd_mixer(rmsnorm(x, mix_norm_o[j]), w_in_o[j], conv_qkv_o[j], a_log_o[j],
                              dt_bias_o[j], gdn_norm_o[j], w_out_o[j])
        x = x + conv_ffn(rmsnorm(x, ffn_norm[i]), w_up[i], ffn_conv[i], w_down[i])
        gate = jax.nn.sigmoid(rmsnorm(x, ple_norm[i]) @ w_ple_gate[i])
        x = x + (p[i] @ w_ple[i]) * gate
    return rmsnorm(x, final_norm)


import jax as _jax
import jax.numpy as _jnp

TWIN_FORMAT = 'train_step'
FWD_PARAMS = ['x', 'p', 'mix_norm_e', 'w_in_e', 'pool_w', 'pool_scale', 'w_out_e', 'mix_norm_o', 'w_in_o', 'conv_qkv_o', 'a_log_o', 'dt_bias_o', 'gdn_norm_o', 'w_out_o', 'ffn_norm', 'w_up', 'ffn_conv', 'w_down', 'ple_norm', 'w_ple_gate', 'w_ple', 'final_norm']
TWIN_WEIGHTS = ['mix_norm_e', 'w_in_e', 'pool_w', 'pool_scale', 'w_out_e', 'mix_norm_o', 'w_in_o', 'conv_qkv_o', 'a_log_o', 'dt_bias_o', 'gdn_norm_o', 'w_out_o', 'ffn_norm', 'w_up', 'ffn_conv', 'w_down', 'ple_norm', 'w_ple_gate', 'w_ple', 'final_norm']
TWIN_DIFF_INPUT = 'x'
TWIN_INPUTS = ['x', 'p', 'mix_norm_e', 'w_in_e', 'pool_w', 'pool_scale', 'w_out_e', 'mix_norm_o', 'w_in_o', 'conv_qkv_o', 'a_log_o', 'dt_bias_o', 'gdn_norm_o', 'w_out_o', 'ffn_norm', 'w_up', 'ffn_conv', 'w_down', 'ple_norm', 'w_ple_gate', 'w_ple', 'final_norm', 'loss_target', 'm_mix_norm_e', 'm_w_in_e', 'm_pool_w', 'm_pool_scale', 'm_w_out_e', 'm_mix_norm_o', 'm_w_in_o', 'm_conv_qkv_o', 'm_a_log_o', 'm_dt_bias_o', 'm_gdn_norm_o', 'm_w_out_o', 'm_ffn_norm', 'm_w_up', 'm_ffn_conv', 'm_w_down', 'm_ple_norm', 'm_w_ple_gate', 'm_w_ple', 'm_final_norm', 'v_mix_norm_e', 'v_w_in_e', 'v_pool_w', 'v_pool_scale', 'v_w_out_e', 'v_mix_norm_o', 'v_w_in_o', 'v_conv_qkv_o', 'v_a_log_o', 'v_dt_bias_o', 'v_gdn_norm_o', 'v_w_out_o', 'v_ffn_norm', 'v_w_up', 'v_ffn_conv', 'v_w_down', 'v_ple_norm', 'v_w_ple_gate', 'v_w_ple', 'v_final_norm']
TWIN_OUTPUTS = ['loss', 'grad_x', 'grad_mix_norm_e', 'grad_w_in_e', 'grad_pool_w', 'grad_pool_scale', 'grad_w_out_e', 'grad_mix_norm_o', 'grad_w_in_o', 'grad_conv_qkv_o', 'grad_a_log_o', 'grad_dt_bias_o', 'grad_gdn_norm_o', 'grad_w_out_o', 'grad_ffn_norm', 'grad_w_up', 'grad_ffn_conv', 'grad_w_down', 'grad_ple_norm', 'grad_w_ple_gate', 'grad_w_ple', 'grad_final_norm', 'delta_mix_norm_e', 'delta_w_in_e', 'delta_pool_w', 'delta_pool_scale', 'delta_w_out_e', 'delta_mix_norm_o', 'delta_w_in_o', 'delta_conv_qkv_o', 'delta_a_log_o', 'delta_dt_bias_o', 'delta_gdn_norm_o', 'delta_w_out_o', 'delta_ffn_norm', 'delta_w_up', 'delta_ffn_conv', 'delta_w_down', 'delta_ple_norm', 'delta_w_ple_gate', 'delta_w_ple', 'delta_final_norm', 'new_m_mix_norm_e', 'new_m_w_in_e', 'new_m_pool_w', 'new_m_pool_scale', 'new_m_w_out_e', 'new_m_mix_norm_o', 'new_m_w_in_o', 'new_m_conv_qkv_o', 'new_m_a_log_o', 'new_m_dt_bias_o', 'new_m_gdn_norm_o', 'new_m_w_out_o', 'new_m_ffn_norm', 'new_m_w_up', 'new_m_ffn_conv', 'new_m_w_down', 'new_m_ple_norm', 'new_m_w_ple_gate', 'new_m_w_ple', 'new_m_final_norm', 'new_v_mix_norm_e', 'new_v_w_in_e', 'new_v_pool_w', 'new_v_pool_scale', 'new_v_w_out_e', 'new_v_mix_norm_o', 'new_v_w_in_o', 'new_v_conv_qkv_o', 'new_v_a_log_o', 'new_v_dt_bias_o', 'new_v_gdn_norm_o', 'new_v_w_out_o', 'new_v_ffn_norm', 'new_v_w_up', 'new_v_ffn_conv', 'new_v_w_down', 'new_v_ple_norm', 'new_v_w_ple_gate', 'new_v_w_ple', 'new_v_final_norm']
TWIN_LEAF_KINDS = {'loss': 'loss', 'grad_x': 'grad_x', 'grad_mix_norm_e': 'grad_w', 'grad_w_in_e': 'grad_w', 'grad_pool_w': 'grad_w', 'grad_pool_scale': 'grad_w', 'grad_w_out_e': 'grad_w', 'grad_mix_norm_o': 'grad_w', 'grad_w_in_o': 'grad_w', 'grad_conv_qkv_o': 'grad_w', 'grad_a_log_o': 'grad_w', 'grad_dt_bias_o': 'grad_w', 'grad_gdn_norm_o': 'grad_w', 'grad_w_out_o': 'grad_w', 'grad_ffn_norm': 'grad_w', 'grad_w_up': 'grad_w', 'grad_ffn_conv': 'grad_w', 'grad_w_down': 'grad_w', 'grad_ple_norm': 'grad_w', 'grad_w_ple_gate': 'grad_w', 'grad_w_ple': 'grad_w', 'grad_final_norm': 'grad_w', 'delta_mix_norm_e': 'delta_w', 'delta_w_in_e': 'delta_w', 'delta_pool_w': 'delta_w', 'delta_pool_scale': 'delta_w', 'delta_w_out_e': 'delta_w', 'delta_mix_norm_o': 'delta_w', 'delta_w_in_o': 'delta_w', 'delta_conv_qkv_o': 'delta_w', 'delta_a_log_o': 'delta_w', 'delta_dt_bias_o': 'delta_w', 'delta_gdn_norm_o': 'delta_w', 'delta_w_out_o': 'delta_w', 'delta_ffn_norm': 'delta_w', 'delta_w_up': 'delta_w', 'delta_ffn_conv': 'delta_w', 'delta_w_down': 'delta_w', 'delta_ple_norm': 'delta_w', 'delta_w_ple_gate': 'delta_w', 'delta_w_ple': 'delta_w', 'delta_final_norm': 'delta_w', 'new_m_mix_norm_e': 'new_m', 'new_m_w_in_e': 'new_m', 'new_m_pool_w': 'new_m', 'new_m_pool_scale': 'new_m', 'new_m_w_out_e': 'new_m', 'new_m_mix_norm_o': 'new_m', 'new_m_w_in_o': 'new_m', 'new_m_conv_qkv_o': 'new_m', 'new_m_a_log_o': 'new_m', 'new_m_dt_bias_o': 'new_m', 'new_m_gdn_norm_o': 'new_m', 'new_m_w_out_o': 'new_m', 'new_m_ffn_norm': 'new_m', 'new_m_w_up': 'new_m', 'new_m_ffn_conv': 'new_m', 'new_m_w_down': 'new_m', 'new_m_ple_norm': 'new_m', 'new_m_w_ple_gate': 'new_m', 'new_m_w_ple': 'new_m', 'new_m_final_norm': 'new_m', 'new_v_mix_norm_e': 'new_v', 'new_v_w_in_e': 'new_v', 'new_v_pool_w': 'new_v', 'new_v_pool_scale': 'new_v', 'new_v_w_out_e': 'new_v', 'new_v_mix_norm_o': 'new_v', 'new_v_w_in_o': 'new_v', 'new_v_conv_qkv_o': 'new_v', 'new_v_a_log_o': 'new_v', 'new_v_dt_bias_o': 'new_v', 'new_v_gdn_norm_o': 'new_v', 'new_v_w_out_o': 'new_v', 'new_v_ffn_norm': 'new_v', 'new_v_w_up': 'new_v', 'new_v_ffn_conv': 'new_v', 'new_v_w_down': 'new_v', 'new_v_ple_norm': 'new_v', 'new_v_w_ple_gate': 'new_v', 'new_v_w_ple': 'new_v', 'new_v_final_norm': 'new_v'}


def _forward(args):
    return _fwd_reference(*[args[k] for k in FWD_PARAMS])


def _output_shape():
    out = _jax.eval_shape(lambda: _forward(_fwd_setup_inputs(0)))
    return out.shape, out.dtype

N_MICROBATCH = 1
ADAM_LR = 0.001
ADAM_B1 = 0.9
ADAM_B2 = 0.999
ADAM_EPS = 1e-08
ADAM_WD = 0.01
ADAM_STEP = 10
PER_EXAMPLE_BATCH_AXIS = {'x': 0, 'p': 1, 'loss_target': 0}
SHARED_INPUTS = []
_WEIGHT_DTYPES = {'mix_norm_e': _jnp.float32, 'w_in_e': _jnp.float32, 'pool_w': _jnp.float32, 'pool_scale': _jnp.float32, 'w_out_e': _jnp.float32, 'mix_norm_o': _jnp.float32, 'w_in_o': _jnp.float32, 'conv_qkv_o': _jnp.float32, 'a_log_o': _jnp.float32, 'dt_bias_o': _jnp.float32, 'gdn_norm_o': _jnp.float32, 'w_out_o': _jnp.float32, 'ffn_norm': _jnp.float32, 'w_up': _jnp.float32, 'ffn_conv': _jnp.float32, 'w_down': _jnp.float32, 'ple_norm': _jnp.float32, 'w_ple_gate': _jnp.float32, 'w_ple': _jnp.float32, 'final_norm': _jnp.float32}
MOMENT_SCALE = {'mix_norm_e': 1.399094e-01, 'w_in_e': 9.753894e-02, 'pool_w': 1.447575e-01, 'pool_scale': 1.521024e-01, 'w_out_e': 1.283160e-01, 'mix_norm_o': 1.178686e-01, 'w_in_o': 5.592711e-02, 'conv_qkv_o': 5.079137e-02, 'a_log_o': 2.737483e-01, 'dt_bias_o': 2.689245e-01, 'gdn_norm_o': 1.822312e-01, 'w_out_o': 6.597708e-02, 'ffn_norm': 1.054369e-01, 'w_up': 4.449913e-02, 'ffn_conv': 4.535229e-02, 'w_down': 7.249530e-02, 'ple_norm': 2.562148e-02, 'w_ple_gate': 2.491129e-02, 'w_ple': 6.383181e-02, 'final_norm': 3.202199e+01}


def _to_microbatches(a, axis):
    t = _jnp.moveaxis(a, axis, 0)
    t = t.reshape((N_MICROBATCH, t.shape[0] // N_MICROBATCH) + t.shape[1:])
    return _jnp.moveaxis(t, 1, axis + 1)


def setup_inputs(seed: int = 0) -> dict:
    inp = _fwd_setup_inputs(seed)
    key = _jax.random.fold_in(_jax.random.key(seed), 7919)
    shape, _ = _output_shape()
    out = dict(inp)
    out["loss_target"] = _jax.random.normal(_jax.random.fold_in(key, 0), shape, _jnp.float32)
    for i, name in enumerate(TWIN_WEIGHTS):
        w = inp[name].astype(_jnp.float32)
        if MOMENT_SCALE is None:
            s = _jnp.sqrt(_jnp.mean(_jnp.square(w)) + 1e-30)
        else:
            s = MOMENT_SCALE[name]
        km, kv = _jax.random.split(_jax.random.fold_in(key, i + 1))
        out[name] = w
        out["m_" + name] = s * _jax.random.normal(km, w.shape, _jnp.float32)
        out["v_" + name] = (s * s) * _jax.random.uniform(kv, w.shape, _jnp.float32, 0.5, 1.5)
    if N_MICROBATCH > 1:
        for name, axis in PER_EXAMPLE_BATCH_AXIS.items():
            out[name] = _to_microbatches(out[name], axis)
    return {'x': out['x'], 'p': out['p'], 'mix_norm_e': out['mix_norm_e'], 'w_in_e': out['w_in_e'], 'pool_w': out['pool_w'], 'pool_scale': out['pool_scale'], 'w_out_e': out['w_out_e'], 'mix_norm_o': out['mix_norm_o'], 'w_in_o': out['w_in_o'], 'conv_qkv_o': out['conv_qkv_o'], 'a_log_o': out['a_log_o'], 'dt_bias_o': out['dt_bias_o'], 'gdn_norm_o': out['gdn_norm_o'], 'w_out_o': out['w_out_o'], 'ffn_norm': out['ffn_norm'], 'w_up': out['w_up'], 'ffn_conv': out['ffn_conv'], 'w_down': out['w_down'], 'ple_norm': out['ple_norm'], 'w_ple_gate': out['w_ple_gate'], 'w_ple': out['w_ple'], 'final_norm': out['final_norm'], 'loss_target': out['loss_target'], 'm_mix_norm_e': out['m_mix_norm_e'], 'm_w_in_e': out['m_w_in_e'], 'm_pool_w': out['m_pool_w'], 'm_pool_scale': out['m_pool_scale'], 'm_w_out_e': out['m_w_out_e'], 'm_mix_norm_o': out['m_mix_norm_o'], 'm_w_in_o': out['m_w_in_o'], 'm_conv_qkv_o': out['m_conv_qkv_o'], 'm_a_log_o': out['m_a_log_o'], 'm_dt_bias_o': out['m_dt_bias_o'], 'm_gdn_norm_o': out['m_gdn_norm_o'], 'm_w_out_o': out['m_w_out_o'], 'm_ffn_norm': out['m_ffn_norm'], 'm_w_up': out['m_w_up'], 'm_ffn_conv': out['m_ffn_conv'], 'm_w_down': out['m_w_down'], 'm_ple_norm': out['m_ple_norm'], 'm_w_ple_gate': out['m_w_ple_gate'], 'm_w_ple': out['m_w_ple'], 'm_final_norm': out['m_final_norm'], 'v_mix_norm_e': out['v_mix_norm_e'], 'v_w_in_e': out['v_w_in_e'], 'v_pool_w': out['v_pool_w'], 'v_pool_scale': out['v_pool_scale'], 'v_w_out_e': out['v_w_out_e'], 'v_mix_norm_o': out['v_mix_norm_o'], 'v_w_in_o': out['v_w_in_o'], 'v_conv_qkv_o': out['v_conv_qkv_o'], 'v_a_log_o': out['v_a_log_o'], 'v_dt_bias_o': out['v_dt_bias_o'], 'v_gdn_norm_o': out['v_gdn_norm_o'], 'v_w_out_o': out['v_w_out_o'], 'v_ffn_norm': out['v_ffn_norm'], 'v_w_up': out['v_w_up'], 'v_ffn_conv': out['v_ffn_conv'], 'v_w_down': out['v_w_down'], 'v_ple_norm': out['v_ple_norm'], 'v_w_ple_gate': out['v_w_ple_gate'], 'v_w_ple': out['v_w_ple'], 'v_final_norm': out['v_final_norm']}


def _loss(weights, diff, rest, loss_target):
    with _jax.named_scope("forward"):
        args = {**rest, TWIN_DIFF_INPUT: diff, **{k: w.astype(_WEIGHT_DTYPES[k]) for k, w in weights.items()}}
        y = _forward(args)
    with _jax.named_scope("loss_head"):
        err = _jnp.square(y.astype(_jnp.float32) - loss_target)
        return 0.5 * _jnp.sum(_jnp.mean(err, axis=-1)) if err.ndim else 0.5 * err


def _adamw(w, g, m, v):
    m = ADAM_B1 * m + (1.0 - ADAM_B1) * g
    v = ADAM_B2 * v + (1.0 - ADAM_B2) * _jnp.square(g)
    m_hat = m / (1.0 - ADAM_B1 ** ADAM_STEP)
    v_hat = v / (1.0 - ADAM_B2 ** ADAM_STEP)
    delta = -ADAM_LR * (m_hat / (_jnp.sqrt(v_hat) + ADAM_EPS) + ADAM_WD * w)
    return delta, m, v


def reference(x, p, mix_norm_e, w_in_e, pool_w, pool_scale, w_out_e, mix_norm_o, w_in_o, conv_qkv_o, a_log_o, dt_bias_o, gdn_norm_o, w_out_o, ffn_norm, w_up, ffn_conv, w_down, ple_norm, w_ple_gate, w_ple, final_norm, loss_target, m_mix_norm_e, m_w_in_e, m_pool_w, m_pool_scale, m_w_out_e, m_mix_norm_o, m_w_in_o, m_conv_qkv_o, m_a_log_o, m_dt_bias_o, m_gdn_norm_o, m_w_out_o, m_ffn_norm, m_w_up, m_ffn_conv, m_w_down, m_ple_norm, m_w_ple_gate, m_w_ple, m_final_norm, v_mix_norm_e, v_w_in_e, v_pool_w, v_pool_scale, v_w_out_e, v_mix_norm_o, v_w_in_o, v_conv_qkv_o, v_a_log_o, v_dt_bias_o, v_gdn_norm_o, v_w_out_o, v_ffn_norm, v_w_up, v_ffn_conv, v_w_down, v_ple_norm, v_w_ple_gate, v_w_ple, v_final_norm):
    given = dict(x=x, p=p, mix_norm_e=mix_norm_e, w_in_e=w_in_e, pool_w=pool_w, pool_scale=pool_scale, w_out_e=w_out_e, mix_norm_o=mix_norm_o, w_in_o=w_in_o, conv_qkv_o=conv_qkv_o, a_log_o=a_log_o, dt_bias_o=dt_bias_o, gdn_norm_o=gdn_norm_o, w_out_o=w_out_o, ffn_norm=ffn_norm, w_up=w_up, ffn_conv=ffn_conv, w_down=w_down, ple_norm=ple_norm, w_ple_gate=w_ple_gate, w_ple=w_ple, final_norm=final_norm, loss_target=loss_target, m_mix_norm_e=m_mix_norm_e, m_w_in_e=m_w_in_e, m_pool_w=m_pool_w, m_pool_scale=m_pool_scale, m_w_out_e=m_w_out_e, m_mix_norm_o=m_mix_norm_o, m_w_in_o=m_w_in_o, m_conv_qkv_o=m_conv_qkv_o, m_a_log_o=m_a_log_o, m_dt_bias_o=m_dt_bias_o, m_gdn_norm_o=m_gdn_norm_o, m_w_out_o=m_w_out_o, m_ffn_norm=m_ffn_norm, m_w_up=m_w_up, m_ffn_conv=m_ffn_conv, m_w_down=m_w_down, m_ple_norm=m_ple_norm, m_w_ple_gate=m_w_ple_gate, m_w_ple=m_w_ple, m_final_norm=m_final_norm, v_mix_norm_e=v_mix_norm_e, v_w_in_e=v_w_in_e, v_pool_w=v_pool_w, v_pool_scale=v_pool_scale, v_w_out_e=v_w_out_e, v_mix_norm_o=v_mix_norm_o, v_w_in_o=v_w_in_o, v_conv_qkv_o=v_conv_qkv_o, v_a_log_o=v_a_log_o, v_dt_bias_o=v_dt_bias_o, v_gdn_norm_o=v_gdn_norm_o, v_w_out_o=v_w_out_o, v_ffn_norm=v_ffn_norm, v_w_up=v_w_up, v_ffn_conv=v_ffn_conv, v_w_down=v_w_down, v_ple_norm=v_ple_norm, v_w_ple_gate=v_w_ple_gate, v_w_ple=v_w_ple, v_final_norm=v_final_norm)
    weights = {n: given[n] for n in TWIN_WEIGHTS}
    shared = {n: given[n] for n in SHARED_INPUTS}
    per_example = {n: given[n] for n in ['x', 'p']}
    grad_fn = _jax.value_and_grad(_loss, argnums=(0, 1))

    def one_microbatch(ex, loss_target):
        ex = dict(ex)
        diff = ex.pop(TWIN_DIFF_INPUT)
        return grad_fn(weights, diff, {**shared, **ex}, loss_target)

    if N_MICROBATCH == 1:
        loss, (grad_w, grad_x) = one_microbatch(per_example, given["loss_target"])
    else:
        def body(carry, xs):
            loss_sum, grad_sum = carry
            l_k, (gw_k, gx_k) = one_microbatch(xs[0], xs[1])
            with _jax.named_scope("update"):
                return (loss_sum + l_k, _jax.tree.map(_jnp.add, grad_sum, gw_k)), gx_k

        init = (_jnp.zeros((), _jnp.float32), _jax.tree.map(_jnp.zeros_like, weights))
        (loss, grad_w), grad_x = _jax.lax.scan(body, init, (per_example, given["loss_target"]))
    with _jax.named_scope("update"):
        delta_w, new_m, new_v = {}, {}, {}
        for n in TWIN_WEIGHTS:
            delta_w[n], new_m[n], new_v[n] = _adamw(weights[n], grad_w[n], given["m_" + n], given["v_" + n])
    return (loss, grad_x, *[grad_w[n] for n in TWIN_WEIGHTS], *[delta_w[n] for n in TWIN_WEIGHTS],
            *[new_m[n] for n in TWIN_WEIGHTS], *[new_v[n] for n in TWIN_WEIGHTS])
```

```python
import functools

import jax
import jax.numpy as jnp
from jax import lax
from jax.experimental import pallas as pl
from jax.experimental.pallas import tpu as pltpu

F32 = jnp.float32
BF16 = jnp.bfloat16

D_MODEL = 1024
PLE_DIM = 256
POOL_WIDTH = 512
POOL_WINDOWS = (2, 4, 8, 16)
POOL_GROUP_DIM = 128
SB_HEADS = 8
SB_HEAD_DIM = 64
GDN_HEADS = 8
GDN_HEAD_DIM = 128
GDN_CONV = 4
GDN_CHUNK = 64
FFN_DIM = 2816
FFN_CONV = 3
EPS = 1e-6
ODD_IN = 4 * D_MODEL + 2 * GDN_HEADS
ODD_IN_PAD = 33 * 128
ADAM_LR, ADAM_B1, ADAM_B2, ADAM_EPS, ADAM_WD, ADAM_STEP = 0.001, 0.9, 0.999, 1e-08, 0.01, 10

LANE = 128
VMEM_LIMIT = 56 * 1024 * 1024

N_CHIPS = 4
N_DEV = 8


def _cp(sem=None):
    return pltpu.CompilerParams(dimension_semantics=sem, vmem_limit_bytes=VMEM_LIMIT)


def _tile(n, pref):
    if n <= pref:
        return n
    best = None
    for t in range(LANE, pref + 1, LANE):
        if n % t == 0:
            best = t
    assert best is not None, (n, pref)
    return best


_DIMS = {"nn": (((1,), (0,)), ((), ())), "nt": (((1,), (1,)), ((), ())), "tn": (((0,), (0,)), ((), ()))}
_BDIMS = {"nn": (((2,), (1,)), ((0,), (0,))), "nt": (((2,), (2,)), ((0,), (0,))), "tn": (((1,), (1,)), ((0,), (0,)))}


def _dot(a, b, mode="nn"):
    return lax.dot_general(a.astype(BF16), b.astype(BF16), _DIMS[mode], preferred_element_type=F32)


def _bdot(a, b, mode="nn"):
    return lax.dot_general(a.astype(BF16), b.astype(BF16), _BDIMS[mode], preferred_element_type=F32)


def _split2(x):
    hi = x.astype(BF16)
    lo = (x - hi.astype(F32)).astype(BF16)
    return hi, lo


def _split3(x):
    hi = x.astype(BF16)
    r = x - hi.astype(F32)
    mid = r.astype(BF16)
    lo = (r - mid.astype(F32)).astype(BF16)
    return hi, mid, lo


def _dot_x01(x, m01, mode="nn"):
    hi, lo = _split2(x)
    return (lax.dot_general(hi, m01, _DIMS[mode], preferred_element_type=F32)
            + lax.dot_general(lo, m01, _DIMS[mode], preferred_element_type=F32))


def _dot3_raw(a, b, mode):
    ah, al = _split2(a)
    bh, bl = _split2(b)
    d = _DIMS[mode]
    return (lax.dot_general(ah, bh, d, preferred_element_type=F32)
            + lax.dot_general(ah, bl, d, preferred_element_type=F32)
            + lax.dot_general(al, bh, d, preferred_element_type=F32))


@jax.custom_vjp
def _dot3(a, b):
    return _dot3_raw(a, b, "nn")


def _dot3_fwd(a, b):
    return _dot3_raw(a, b, "nn"), (a, b)


def _dot3_bwd(res, g):
    a, b = res
    return _dot3_raw(g, b, "nt"), _dot3_raw(a, g, "tn")


_dot3.defvjp(_dot3_fwd, _dot3_bwd)


@jax.custom_vjp
def _dot1_nt(a, b):
    return _dot(a, b, "nt")


def _dot1_nt_fwd(a, b):
    return _dot(a, b, "nt"), (a, b)


def _dot1_nt_bwd(res, g):
    a, b = res
    return _dot(g, b, "nn"), _dot(g, a, "tn")


_dot1_nt.defvjp(_dot1_nt_fwd, _dot1_nt_bwd)


def _m01_left_raw(m, x):
    d = _DIMS["nn"]
    p0, p1, p2 = _split3(x)
    return (lax.dot_general(m, p0, d, preferred_element_type=F32)
            + lax.dot_general(m, p1, d, preferred_element_type=F32)
            + lax.dot_general(m, p2, d, preferred_element_type=F32))


@jax.custom_vjp
def _m01_left(m, mt, x):
    return _m01_left_raw(m, x)


def _m01_left_fwd(m, mt, x):
    return _m01_left_raw(m, x), (m, mt)


def _m01_left_bwd(res, g):
    m, mt = res
    return jnp.zeros_like(m), jnp.zeros_like(mt), _m01_left_raw(mt, g)


_m01_left.defvjp(_m01_left_fwd, _m01_left_bwd)


def _softplus(x):
    return jnp.maximum(x, 0.0) + jnp.log(1.0 + jnp.exp(-jnp.abs(x)))


def _sigmoid(x):
    return 1.0 / (1.0 + jnp.exp(-x))


def _silu(x):
    return x * _sigmoid(x)


def _dsilu(x):
    s = _sigmoid(x)
    return s * (1.0 + x * (1.0 - s))


def _mm(a, b, mode, name, out_dtype=F32, res=None, tm=512, tn=512, tk=1024):
    if mode == "nn":
        (M, K), (K2, N) = a.shape, b.shape
    elif mode == "nt":
        (M, K), (N, K2) = a.shape, b.shape
    else:
        (K, M), (K2, N) = a.shape, b.shape
    assert K == K2, (name, a.shape, b.shape)
    tm, tn, tk = _tile(M, tm), _tile(N, tn), _tile(K, tk)
    nk = K // tk
    a_spec = {"nn": pl.BlockSpec((tm, tk), lambda i, j, k: (i, k)),
              "nt": pl.BlockSpec((tm, tk), lambda i, j, k: (i, k)),
              "tn": pl.BlockSpec((tk, tm), lambda i, j, k: (k, i))}[mode]
    b_spec = {"nn": pl.BlockSpec((tk, tn), lambda i, j, k: (k, j)),
              "nt": pl.BlockSpec((tn, tk), lambda i, j, k: (j, k)),
              "tn": pl.BlockSpec((tk, tn), lambda i, j, k: (k, j))}[mode]
    o_spec = pl.BlockSpec((tm, tn), lambda i, j, k: (i, j))
    has_res = res is not None

    def body(*refs):
        if has_res:
            a_ref, b_ref, r_ref, o_ref, acc = refs
        else:
            a_ref, b_ref, o_ref, acc = refs
        k = pl.program_id(2)

        @pl.when(k == 0)
        def _():
            acc[...] = jnp.zeros_like(acc)

        acc[...] += _dot(a_ref[...], b_ref[...], mode)

        @pl.when(k == nk - 1)
        def _():
            r = acc[...]
            if has_res:
                r = r + r_ref[...]
            o_ref[...] = r.astype(out_dtype)

    ins = [a, b] + ([res] if has_res else [])
    in_specs = [a_spec, b_spec] + ([o_spec] if has_res else [])
    return pl.pallas_call(
        body, name=name, grid=(M // tm, N // tn, nk),
        in_specs=in_specs, out_specs=o_spec,
        out_shape=jax.ShapeDtypeStruct((M, N), out_dtype),
        scratch_shapes=[pltpu.VMEM((tm, tn), F32)],
        compiler_params=_cp(("parallel", "parallel", "arbitrary")),
    )(*ins)


def _rms_fwd(x, gain, name):
    T, D = x.shape
    tt = _tile(T, 512)

    def body(x_ref, g_ref, o_ref):
        xv = x_ref[...]
        r = lax.rsqrt(jnp.mean(xv * xv, axis=-1, keepdims=True) + EPS)
        o_ref[...] = (xv * r * g_ref[...]).astype(BF16)

    return pl.pallas_call(
        body, name=name, grid=(T // tt,),
        in_specs=[pl.BlockSpec((tt, D), lambda i: (i, 0)), pl.BlockSpec((1, D), lambda i: (0, 0))],
        out_specs=pl.BlockSpec((tt, D), lambda i: (i, 0)),
        out_shape=jax.ShapeDtypeStruct((T, D), BF16),
        compiler_params=_cp(("parallel",)),
    )(x, gain)


def _rms_bwd(x, gain, dh, dres, name):
    T, D = x.shape
    tt = _tile(T, 512)

    def body(x_ref, g_ref, dh_ref, dr_ref, dx_ref, dg_ref):
        i = pl.program_id(0)
        xv = x_ref[...]
        dy = dh_ref[...].astype(F32)
        r = lax.rsqrt(jnp.mean(xv * xv, axis=-1, keepdims=True) + EPS)
        xn = xv * r
        gdy = dy * g_ref[...]
        dx = r * (gdy - xn * jnp.mean(gdy * xn, axis=-1, keepdims=True))
        dx_ref[...] = dr_ref[...] + dx

        @pl.when(i == 0)
        def _():
            dg_ref[...] = jnp.zeros_like(dg_ref)

        dg_ref[...] += jnp.sum(dy * xn, axis=0, keepdims=True)

    row = pl.BlockSpec((tt, D), lambda i: (i, 0))
    vec = pl.BlockSpec((1, D), lambda i: (0, 0))
    return pl.pallas_call(
        body, name=name, grid=(T // tt,),
        in_specs=[row, vec, row, row], out_specs=[row, vec],
        out_shape=[jax.ShapeDtypeStruct((T, D), F32), jax.ShapeDtypeStruct((1, D), F32)],
        compiler_params=_cp(("arbitrary",)),
    )(x, gain, dh, dres)


def _final_loss(x, gain, target, name):
    T, D = x.shape
    tt = _tile(T, 512)

    def body(x_ref, g_ref, t_ref, l_ref, dx_ref, dg_ref):
        i = pl.program_id(0)
        xv = x_ref[...]
        r = lax.rsqrt(jnp.mean(xv * xv, axis=-1, keepdims=True) + EPS)
        xn = xv * r
        err = xn * g_ref[...] - t_ref[...]
        dy = err * (1.0 / D)
        gdy = dy * g_ref[...]
        dx_ref[...] = r * (gdy - xn * jnp.mean(gdy * xn, axis=-1, keepdims=True))

        @pl.when(i == 0)
        def _():
            dg_ref[...] = jnp.zeros_like(dg_ref)
            l_ref[...] = jnp.zeros_like(l_ref)

        dg_ref[...] += jnp.sum(dy * xn, axis=0, keepdims=True)
        l_ref[...] += jnp.sum(jnp.sum(err * err, axis=1, keepdims=True), axis=0, keepdims=True)

    row = pl.BlockSpec((tt, D), lambda i: (i, 0))
    vec = pl.BlockSpec((1, D), lambda i: (0, 0))
    return pl.pallas_call(
        body, name=name, grid=(T // tt,),
        in_specs=[row, vec, row],
        out_specs=[pl.BlockSpec((8, LANE), lambda i: (0, 0)), row, vec],
        out_shape=[jax.ShapeDtypeStruct((8, LANE), F32), jax.ShapeDtypeStruct((T, D), F32),
                   jax.ShapeDtypeStruct((1, D), F32)],
        compiler_params=_cp(("arbitrary",)),
    )(x, gain, target)


def _shift_down(x, i, t_idx):
    if i == 0:
        return x
    return jnp.where(t_idx >= i, pltpu.roll(x, i, 0), 0.0)


def _shift_up(x, i, t_idx):
    if i == 0:
        return x
    n = x.shape[0]
    return jnp.where(t_idx < n - i, pltpu.roll(x, n - i, 0), 0.0)


def _pool_select(g, vals):
    out = vals[-1]
    for gi in range(len(vals) - 2, -1, -1):
        out = jnp.where(g == gi, vals[gi], out)
    return out


def _pool_y(u, g, t_idx):
    s1 = u + _shift_down(u, 1, t_idx)
    s2 = s1 + _shift_down(s1, 2, t_idx)
    s3 = s2 + _shift_down(s2, 4, t_idx)
    s4 = s3 + _shift_down(s3, 8, t_idx)
    ws = _pool_select(g, [s1, s2, s3, s4])
    win = _pool_select(g, [jnp.float32(w) for w in POOL_WINDOWS])
    cnt = jnp.minimum(t_idx.astype(F32) + 1.0, win)
    return ws / cnt - u, cnt


def _pool_fwd(proj, pool_w, pool_scale):
    T = proj.shape[0]
    G, C = len(POOL_WINDOWS), POOL_GROUP_DIM

    def body(u_ref, w_ref, s_ref, o_ref):
        g = pl.program_id(0)
        t_idx = lax.broadcasted_iota(jnp.int32, (T, C), 0)
        y, _ = _pool_y(u_ref[...], g, t_idx)
        o_ref[...] = _dot(y, w_ref[0]) * s_ref[...]

    return pl.pallas_call(
        body, name="pool_fwd", grid=(G,),
        in_specs=[pl.BlockSpec((T, C), lambda g: (0, g)), pl.BlockSpec((1, C, C), lambda g: (g, 0, 0)),
                  pl.BlockSpec((1, C), lambda g: (0, g))],
        out_specs=pl.BlockSpec((T, C), lambda g: (0, g)),
        out_shape=jax.ShapeDtypeStruct((T, G * C), F32),
        compiler_params=_cp(("parallel",)),
    )(proj, pool_w, pool_scale)


def _pool_bwd(proj, pool_w, pool_scale, dmix):
    T = proj.shape[0]
    G, C = len(POOL_WINDOWS), POOL_GROUP_DIM

    def body(u_ref, w_ref, s_ref, do_ref, du_ref, dw_ref, ds_ref):
        g = pl.program_id(0)
        t_idx = lax.broadcasted_iota(jnp.int32, (T, C), 0)
        y, cnt = _pool_y(u_ref[...], g, t_idx)
        w = w_ref[0]
        dout = do_ref[...]
        ds_ref[...] = jnp.sum(dout * _dot(y, w), axis=0, keepdims=True)
        dy2 = dout * s_ref[...]
        dw_ref[0] = _dot(y, dy2, "tn")
        dy = _dot(dy2, w, "nt")
        dz = dy / cnt
        r1 = dz + _shift_up(dz, 1, t_idx)
        r2 = r1 + _shift_up(r1, 2, t_idx)
        r3 = r2 + _shift_up(r2, 4, t_idx)
        r4 = r3 + _shift_up(r3, 8, t_idx)
        du_ref[...] = _pool_select(g, [r1, r2, r3, r4]) - dy

    col = pl.BlockSpec((T, C), lambda g: (0, g))
    return pl.pallas_call(
        body, name="pool_bwd", grid=(G,),
        in_specs=[col, pl.BlockSpec((1, C, C), lambda g: (g, 0, 0)), pl.BlockSpec((1, C), lambda g: (0, g)), col],
        out_specs=[col, pl.BlockSpec((1, C, C), lambda g: (g, 0, 0)), pl.BlockSpec((1, C), lambda g: (0, g))],
        out_shape=[jax.ShapeDtypeStruct((T, G * C), F32), jax.ShapeDtypeStruct((G, C, C), F32),
                   jax.ShapeDtypeStruct((1, G * C), F32)],
        compiler_params=_cp(("parallel",)),
    )(proj, pool_w, pool_scale, dmix)


SB_SCALE = SB_HEAD_DIM ** -0.5


def _sb_tile_logits(qb, kblk, valid):
    z = _dot(qb, kblk, "nt") * SB_SCALE
    sp = _softplus(z)
    l1m = -sp
    if valid is not None:
        l1m = jnp.where(valid, l1m, 0.0)
    return z, sp, l1m


def _sb_fwd(q, k, v):
    H, T, dh = q.shape
    B = _tile(T, 256)
    nq = T // B

    def body(q_ref, k_ref, v_ref, o_ref, l_ref):
        qi = pl.program_id(1)
        qb = q_ref[0]
        row = lax.broadcasted_iota(jnp.int32, (B, B), 0)
        col = lax.broadcasted_iota(jnp.int32, (B, B), 1)
        later = (row > col).astype(BF16)

        def tile(kb, carry, acc, valid):
            ks = pl.ds(pl.multiple_of(kb * B, B), B)
            z, sp, l1m = _sb_tile_logits(qb, k_ref[0, ks, :], valid)
            log_keep = _dot_x01(l1m, later) + carry
            a = jnp.exp(z - sp + log_keep)
            if valid is not None:
                a = jnp.where(valid, a, 0.0)
            acc = acc + _dot(a, v_ref[0, ks, :])
            carry = carry + jnp.sum(l1m, axis=1, keepdims=True)
            return carry, acc

        carry, acc = tile(qi, jnp.zeros((B, 1), F32), jnp.zeros((B, dh), F32), col < row)
        carry, acc = lax.fori_loop(0, qi, lambda i, c: tile(qi - 1 - i, c[0], c[1], None), (carry, acc))
        o_ref[0] = acc
        l_ref[0] = carry

    qspec = pl.BlockSpec((1, B, dh), lambda h, i: (h, i, 0))
    full = pl.BlockSpec((1, T, dh), lambda h, i: (h, 0, 0))
    return pl.pallas_call(
        body, name="sb_fwd", grid=(H, nq),
        in_specs=[qspec, full, full],
        out_specs=[qspec, pl.BlockSpec((1, B, 1), lambda h, i: (h, i, 0))],
        out_shape=[jax.ShapeDtypeStruct((H, T, dh), F32), jax.ShapeDtypeStruct((H, T, 1), F32)],
        compiler_params=_cp(("parallel", "parallel")),
    )(q, k, v)


def _sb_bwd(q, k, v, dout, ltot):
    H, T, dh = q.shape
    B = _tile(T, 256)
    nq = T // B

    def body(q_ref, k_ref, v_ref, do_ref, l_ref, dq_ref, dk_ref, dv_ref):
        qi = pl.program_id(1)

        @pl.when(qi == 0)
        def _():
            dk_ref[...] = jnp.zeros_like(dk_ref)
            dv_ref[...] = jnp.zeros_like(dv_ref)

        qb = q_ref[0]
        dob = do_ref[0].astype(BF16)
        ltot_q = l_ref[0]
        row = lax.broadcasted_iota(jnp.int32, (B, B), 0)
        col = lax.broadcasted_iota(jnp.int32, (B, B), 1)
        upto = (row <= col).astype(BF16)
        before = (row < col).astype(BF16)

        def tile(kb, P, E, dq, valid):
            ks = pl.ds(pl.multiple_of(kb * B, B), B)
            kblk = k_ref[0, ks, :]
            vblk = v_ref[0, ks, :]
            z, sp, l1m = _sb_tile_logits(qb, kblk, valid)
            log_keep = ltot_q - P - _dot_x01(l1m, upto)
            a = jnp.exp(z - sp + log_keep)
            if valid is not None:
                a = jnp.where(valid, a, 0.0)
            e = _dot(dob, vblk, "nt") * a
            e_before = _dot_x01(e, before) + E
            dz = (e * jnp.exp(-sp) - jnp.exp(z - sp) * e_before) * SB_SCALE
            if valid is not None:
                dz = jnp.where(valid, dz, 0.0)
            dzb = dz.astype(BF16)
            dq = dq + _dot(dzb, kblk)
            dk_ref[0, ks, :] += _dot(dzb, qb, "tn")
            dv_ref[0, ks, :] += _dot(a, dob, "tn")
            return P + jnp.sum(l1m, axis=1, keepdims=True), E + jnp.sum(e, axis=1, keepdims=True), dq

        zeros1 = jnp.zeros((B, 1), F32)
        P, E, dq = lax.fori_loop(0, qi, lambda kb, c: tile(kb, c[0], c[1], c[2], None),
                                 (zeros1, zeros1, jnp.zeros((B, dh), F32)))
        _, _, dq = tile(qi, P, E, dq, col < row)
        dq_ref[0] = dq

    qspec = pl.BlockSpec((1, B, dh), lambda h, i: (h, i, 0))
    full = pl.BlockSpec((1, T, dh), lambda h, i: (h, 0, 0))
    shp = jax.ShapeDtypeStruct((H, T, dh), F32)
    return pl.pallas_call(
        body, name="sb_bwd", grid=(H, nq),
        in_specs=[qspec, full, full, qspec, pl.BlockSpec((1, B, 1), lambda h, i: (h, i, 0))],
        out_specs=[qspec, full, full],
        out_shape=[shp, shp, shp],
        compiler_params=_cp(("parallel", "arbitrary")),
    )(q, k, v, dout, ltot)


def _rows(w_ref, K):
    return [w_ref[i:i + 1, :] for i in range(K)]


def _conv(x, ws, t_idx):
    K = len(ws)
    y = ws[K - 1] * x
    for i in range(K - 1):
        y = y + ws[i] * _shift_down(x, K - 1 - i, t_idx)
    return y


def _conv_bwd(x, ws, dy, t_idx):
    K = len(ws)
    dx = ws[K - 1] * dy
    dws = []
    for i in range(K - 1):
        dx = dx + ws[i] * _shift_up(dy, K - 1 - i, t_idx)
        dws.append(jnp.sum(dy * _shift_down(x, K - 1 - i, t_idx), axis=0, keepdims=True))
    dws.append(jnp.sum(dy * x, axis=0, keepdims=True))
    return dx, dws


def _store_rows(ref, rows):
    for i, r in enumerate(rows):
        ref[i:i + 1, :] = r


def _ffn_act_fwd(up, conv_w, name):
    T = up.shape[0]
    F = FFN_DIM
    nb = F // LANE

    def body(g_ref, v_ref, wg_ref, wv_ref, o_ref):
        t_idx = lax.broadcasted_iota(jnp.int32, (T, LANE), 0)
        cg = _conv(g_ref[...], _rows(wg_ref, FFN_CONV), t_idx)
        cv = _conv(v_ref[...], _rows(wv_ref, FFN_CONV), t_idx)
        o_ref[...] = (_silu(cg) * cv).astype(BF16)

    return pl.pallas_call(
        body, name=name, grid=(nb,),
        in_specs=[pl.BlockSpec((T, LANE), lambda j: (0, j)), pl.BlockSpec((T, LANE), lambda j: (0, j + nb)),
                  pl.BlockSpec((FFN_CONV, LANE), lambda j: (0, j)),
                  pl.BlockSpec((FFN_CONV, LANE), lambda j: (0, j + nb))],
        out_specs=pl.BlockSpec((T, LANE), lambda j: (0, j)),
        out_shape=jax.ShapeDtypeStruct((T, F), BF16),
        compiler_params=_cp(("parallel",)),
    )(up, up, conv_w, conv_w)


def _ffn_act_bwd(up, conv_w, dact, name):
    T = up.shape[0]
    F = FFN_DIM
    nb = F // LANE

    def body(g_ref, v_ref, wg_ref, wv_ref, da_ref, dg_ref, dv_ref, dwg_ref, dwv_ref):
        t_idx = lax.broadcasted_iota(jnp.int32, (T, LANE), 0)
        xg, xv, wg, wv = g_ref[...], v_ref[...], _rows(wg_ref, FFN_CONV), _rows(wv_ref, FFN_CONV)
        cg = _conv(xg, wg, t_idx)
        cv = _conv(xv, wv, t_idx)
        da = da_ref[...].astype(F32)
        dxg, dwg = _conv_bwd(xg, wg, da * cv * _dsilu(cg), t_idx)
        dxv, dwv = _conv_bwd(xv, wv, da * _silu(cg), t_idx)
        dg_ref[...] = dxg.astype(BF16)
        dv_ref[...] = dxv.astype(BF16)
        _store_rows(dwg_ref, dwg)
        _store_rows(dwv_ref, dwv)

    col = pl.BlockSpec((T, LANE), lambda j: (0, j))
    wcol = pl.BlockSpec((FFN_CONV, LANE), lambda j: (0, j))
    return pl.pallas_call(
        body, name=name, grid=(nb,),
        in_specs=[col, pl.BlockSpec((T, LANE), lambda j: (0, j + nb)), wcol,
                  pl.BlockSpec((FFN_CONV, LANE), lambda j: (0, j + nb)), col],
        out_specs=[col, col, wcol, wcol],
        out_shape=[jax.ShapeDtypeStruct((T, F), BF16), jax.ShapeDtypeStruct((T, F), BF16),
                   jax.ShapeDtypeStruct((FFN_CONV, F), F32), jax.ShapeDtypeStruct((FFN_CONV, F), F32)],
        compiler_params=_cp(("parallel",)),
    )(up, up, conv_w, conv_w, dact)


N_QK_BLOCKS = 2 * GDN_HEADS


def _gdn_pre_fwd(proj, conv_w):
    T = proj.shape[0]
    nb = 3 * GDN_HEADS

    def body(x_ref, w_ref, o_ref):
        j = pl.program_id(0)
        t_idx = lax.broadcasted_iota(jnp.int32, (T, LANE), 0)
        s = _silu(_conv(x_ref[...], _rows(w_ref, GDN_CONV), t_idx))
        rn = lax.rsqrt(jnp.sum(s * s, axis=-1, keepdims=True) + EPS)
        o_ref[...] = s * jnp.where(j < N_QK_BLOCKS, rn, 1.0)

    return pl.pallas_call(
        body, name="gdn_pre_fwd", grid=(nb,),
        in_specs=[pl.BlockSpec((T, LANE), lambda j: (0, j)), pl.BlockSpec((GDN_CONV, LANE), lambda j: (0, j))],
        out_specs=pl.BlockSpec((T, LANE), lambda j: (0, j)),
        out_shape=jax.ShapeDtypeStruct((T, nb * LANE), F32),
        compiler_params=_cp(("parallel",)),
    )(proj, conv_w)


def _gdn_pre_bwd(proj, conv_w, dout):
    T = proj.shape[0]
    nb = 3 * GDN_HEADS

    def body(x_ref, w_ref, do_ref, dx_ref, dw_ref):
        j = pl.program_id(0)
        t_idx = lax.broadcasted_iota(jnp.int32, (T, LANE), 0)
        x, w = x_ref[...], _rows(w_ref, GDN_CONV)
        c = _conv(x, w, t_idx)
        s = _silu(c)
        rn = lax.rsqrt(jnp.sum(s * s, axis=-1, keepdims=True) + EPS)
        do = do_ref[...]
        y = s * rn
        ds_normed = rn * (do - y * jnp.sum(do * y, axis=-1, keepdims=True))
        ds = jnp.where(j < N_QK_BLOCKS, ds_normed, do)
        dx, dw = _conv_bwd(x, w, ds * _dsilu(c), t_idx)
        dx_ref[...] = dx.astype(BF16)
        _store_rows(dw_ref, dw)

    col = pl.BlockSpec((T, LANE), lambda j: (0, j))
    wcol = pl.BlockSpec((GDN_CONV, LANE), lambda j: (0, j))
    return pl.pallas_call(
        body, name="gdn_pre_bwd", grid=(nb,),
        in_specs=[col, wcol, col], out_specs=[col, wcol],
        out_shape=[jax.ShapeDtypeStruct((T, nb * LANE), BF16), jax.ShapeDtypeStruct((GDN_CONV, nb * LANE), F32)],
        compiler_params=_cp(("parallel",)),
    )(proj, conv_w, dout)


def _gdn_consts():
    C = GDN_CHUNK
    r = lax.broadcasted_iota(jnp.int32, (C, C), 0)
    c = lax.broadcasted_iota(jnp.int32, (C, C), 1)
    return dict(incl=r >= c, strict=r > c, eye=(r == c).astype(F32),
                low=(r >= c).astype(BF16), up=(r <= c).astype(BF16), ones=jnp.ones((C, C), BF16))


def _gdn_prep_chunk(q, k, v, b, a, alog, dtb, cs):
    C, dk = q.shape
    beta = _sigmoid(b)
    g = -jnp.exp(alog) * _softplus(a + dtb)
    g_sq = jnp.broadcast_to(g, (C, C))
    g_wide = jnp.broadcast_to(g, (C, dk))
    gc_i = _m01_left(cs["low"], cs["up"], g_sq)
    gc_j = _m01_left(cs["ones"], cs["ones"], g_sq * cs["up"].astype(F32))
    gc_wide = _m01_left(cs["low"], cs["up"], g_wide)
    gl_wide = _m01_left(cs["ones"], cs["ones"], g_wide)
    decay = jnp.where(cs["incl"], jnp.exp(jnp.where(cs["incl"], gc_i - gc_j, 0.0)), 0.0)
    egc = jnp.exp(gc_wide)
    qs = q * (dk ** -0.5)
    k_beta = k * beta
    a_mat = jnp.where(cs["strict"], _dot1_nt(k_beta, k) * decay, 0.0)
    inv = cs["eye"] - a_mat
    pw = _dot3(a_mat, a_mat)
    n_factors = C.bit_length() - 2
    for f in range(n_factors):
        inv = inv + _dot3(inv, pw)
        if f < n_factors - 1:
            pw = _dot3(pw, pw)
    u = _dot3(inv, v * beta)
    w = _dot3(inv, k_beta * egc)
    qk = _dot1_nt(qs, k) * decay
    q_dec = qs * egc
    k_dec = k * jnp.exp(gl_wide - gc_wide)
    g_last = jnp.exp(gl_wide)[0:8, :]
    return qk, u, w, q_dec, k_dec, g_last


GDN_PREP_CHUNKS = 8


def _gdn_prep_specs(T):
    C, dk = GDN_CHUNK, GDN_HEAD_DIM
    npc = min(GDN_PREP_CHUNKS, T // C)
    tc = npc * C
    H = GDN_HEADS
    in_specs = [pl.BlockSpec((tc, dk), lambda h, i: (i, h)),
                pl.BlockSpec((tc, dk), lambda h, i: (i, H + h)),
                pl.BlockSpec((tc, dk), lambda h, i: (i, 2 * H + h)),
                pl.BlockSpec((1, tc, 1), lambda h, i: (h, i, 0)),
                pl.BlockSpec((1, tc, 1), lambda h, i: (h, i, 0)),
                pl.BlockSpec((1, 1, 1), lambda h, i: (h, 0, 0)),
                pl.BlockSpec((1, 1, 1), lambda h, i: (h, 0, 0))]
    xs_specs = [pl.BlockSpec((1, tc, C), lambda h, i: (h, i, 0)),
                pl.BlockSpec((1, tc, dk), lambda h, i: (h, i, 0)),
                pl.BlockSpec((1, tc, dk), lambda h, i: (h, i, 0)),
                pl.BlockSpec((1, tc, dk), lambda h, i: (h, i, 0)),
                pl.BlockSpec((1, tc, dk), lambda h, i: (h, i, 0)),
                pl.BlockSpec((1, npc * 8, dk), lambda h, i: (h, i, 0))]
    xs_shapes = [jax.ShapeDtypeStruct((H, T, C), F32)] + [jax.ShapeDtypeStruct((H, T, dk), F32)] * 4 + [
        jax.ShapeDtypeStruct((H, 8 * T // C, dk), F32)]
    return npc, tc, in_specs, xs_specs, xs_shapes


def _gdn_prep_fwd(qkv, b, a, alog, dtb):
    T = qkv.shape[0]
    C = GDN_CHUNK
    npc, tc, in_specs, xs_specs, xs_shapes = _gdn_prep_specs(T)

    def body(q_ref, k_ref, v_ref, b_ref, a_ref, al_ref, dt_ref, qk_ref, u_ref, w_ref, qd_ref, kd_ref, gl_ref):
        cs = _gdn_consts()

        def chunk(c, carry):
            rows = pl.ds(pl.multiple_of(c * C, C), C)
            outs = _gdn_prep_chunk(q_ref[rows, :], k_ref[rows, :], v_ref[rows, :], b_ref[0, rows, :],
                                   a_ref[0, rows, :], al_ref[0], dt_ref[0], cs)
            for ref, val in zip((qk_ref, u_ref, w_ref, qd_ref, kd_ref), outs[:5]):
                ref[0, rows, :] = val
            gl_ref[0, pl.ds(pl.multiple_of(c * 8, 8), 8), :] = outs[5]
            return carry

        lax.fori_loop(0, npc, chunk, 0)

    return pl.pallas_call(
        body, name="gdn_prep_fwd", grid=(GDN_HEADS, T // tc),
        in_specs=in_specs, out_specs=xs_specs, out_shape=xs_shapes,
        compiler_params=_cp(("parallel", "parallel")),
    )(qkv, qkv, qkv, b, a, alog, dtb)


def _gdn_prep_bwd(qkv, b, a, alog, dtb, dxs):
    T = qkv.shape[0]
    C, dk, H = GDN_CHUNK, GDN_HEAD_DIM, GDN_HEADS
    npc, tc, in_specs, xs_specs, _ = _gdn_prep_specs(T)

    def body(q_ref, k_ref, v_ref, b_ref, a_ref, al_ref, dt_ref, dqk_ref, du_ref, dw_ref, dqd_ref, dkd_ref, dgl_ref,
             dq_ref, dk_ref, dv_ref, db_ref, da_ref, dal_ref, ddt_ref):
        i = pl.program_id(1)
        cs = _gdn_consts()
        r8 = lax.broadcasted_iota(jnp.int32, (8, dk), 0)
        c8 = lax.broadcasted_iota(jnp.int32, (8, dk), 1)
        first = (r8 == 0) & (c8 == 0)

        @pl.when(i == 0)
        def _():
            dal_ref[...] = jnp.zeros_like(dal_ref)
            ddt_ref[...] = jnp.zeros_like(ddt_ref)

        def chunk(c, carry):
            rows = pl.ds(pl.multiple_of(c * C, C), C)
            prim = (q_ref[rows, :], k_ref[rows, :], v_ref[rows, :], b_ref[0, rows, :], a_ref[0, rows, :],
                    al_ref[0], dt_ref[0])
            _, vjp = jax.vjp(lambda *p: _gdn_prep_chunk(*p, cs), *prim)
            dgl = jnp.where(first, dgl_ref[0, pl.ds(pl.multiple_of(c * 8, 8), 8), :], 0.0)
            cts = (dqk_ref[0, rows, :], du_ref[0, rows, :], dw_ref[0, rows, :], dqd_ref[0, rows, :],
                   dkd_ref[0, rows, :], dgl)
            dq, dkk, dv, db, da, dal, ddt = vjp(cts)
            dq_ref[0, rows, :] = dq
            dk_ref[0, rows, :] = dkk
            dv_ref[0, rows, :] = dv
            db_ref[0, rows, :] = db
            da_ref[0, rows, :] = da
            dal_ref[0] += dal
            ddt_ref[0] += ddt
            return carry

        lax.fori_loop(0, npc, chunk, 0)

    big = pl.BlockSpec((1, tc, dk), lambda h, i: (h, i, 0))
    thin = pl.BlockSpec((1, tc, 1), lambda h, i: (h, i, 0))
    one = pl.BlockSpec((1, 1, 1), lambda h, i: (h, 0, 0))
    return pl.pallas_call(
        body, name="gdn_prep_bwd", grid=(H, T // tc),
        in_specs=in_specs + xs_specs,
        out_specs=[big, big, big, thin, thin, one, one],
        out_shape=[jax.ShapeDtypeStruct((H, T, dk), F32)] * 3 + [jax.ShapeDtypeStruct((H, T, 1), F32)] * 2
        + [jax.ShapeDtypeStruct((H, 1, 1), F32)] * 2,
        compiler_params=_cp(("parallel", "arbitrary")),
    )(qkv, qkv, qkv, b, a, alog, dtb, *dxs)


def _gdn_scan_specs(T):
    C, dk, H = GDN_CHUNK, GDN_HEAD_DIM, GDN_HEADS
    return [pl.BlockSpec((H, C, C), lambda n: (0, n, 0))] + [pl.BlockSpec((H, C, dk), lambda n: (0, n, 0))] * 4 + [
        pl.BlockSpec((H, 8, dk), lambda n: (0, n, 0))]


def _gdn_scan_fwd(xs):
    H, T, dk = xs[1].shape
    C = GDN_CHUNK
    n = T // C

    def body(qk_ref, u_ref, w_ref, qd_ref, kd_ref, gl_ref, o_ref, s_ref, state):
        c = pl.program_id(0)

        @pl.when(c == 0)
        def _():
            state[...] = jnp.zeros_like(state)

        S = state[...]
        s_ref[0] = S
        v_new = u_ref[...] - _bdot(w_ref[...], S)
        o_ref[...] = _bdot(qd_ref[...], S) + _bdot(qk_ref[...], v_new)
        state[...] = S * jnp.tile(gl_ref[...], (1, dk // 8, 1)) + _bdot(kd_ref[...], v_new, "tn")

    return pl.pallas_call(
        body, name="gdn_scan_fwd", grid=(n,),
        in_specs=_gdn_scan_specs(T),
        out_specs=[pl.BlockSpec((H, C, dk), lambda n: (0, n, 0)), pl.BlockSpec((1, H, dk, dk), lambda n: (n, 0, 0, 0))],
        out_shape=[jax.ShapeDtypeStruct((H, T, dk), F32), jax.ShapeDtypeStruct((n, H, dk, dk), F32)],
        scratch_shapes=[pltpu.VMEM((H, dk, dk), F32)],
        compiler_params=_cp(("arbitrary",)),
    )(*xs)


def _gdn_scan_bwd(xs, states, do):
    H, T, dk = xs[1].shape
    C = GDN_CHUNK
    n = T // C

    def rev(spec_shape, f):
        return pl.BlockSpec(spec_shape, lambda i: f(n - 1 - i))

    def body(qk_ref, u_ref, w_ref, qd_ref, kd_ref, gl_ref, s_ref, do_ref,
             dqk_ref, du_ref, dw_ref, dqd_ref, dkd_ref, dgl_ref, dstate):
        i = pl.program_id(0)

        @pl.when(i == 0)
        def _():
            dstate[...] = jnp.zeros_like(dstate)

        S = s_ref[0]
        dS = dstate[...]
        do_v = do_ref[...]
        qk, w, qd, kd = qk_ref[...], w_ref[...], qd_ref[...], kd_ref[...]
        v_new = u_ref[...] - _bdot(w, S)
        dv_new = _bdot(qk, do_v, "tn") + _bdot(kd, dS)
        dqk_ref[...] = _bdot(do_v, v_new, "nt")
        dqd_ref[...] = _bdot(do_v, S, "nt")
        dkd_ref[...] = _bdot(v_new, dS, "nt")
        du_ref[...] = dv_new
        dw_ref[...] = -_bdot(dv_new, S, "nt")
        dgl = jnp.sum(jnp.sum(S * dS, axis=2, keepdims=True), axis=1, keepdims=True)
        dgl_ref[...] = jnp.broadcast_to(dgl, dgl_ref.shape)
        dstate[...] = (dS * jnp.tile(gl_ref[...], (1, dk // 8, 1)) + _bdot(qd, do_v, "tn")
                       - _bdot(w, dv_new, "tn"))

    in_specs = [rev((H, C, C), lambda m: (0, m, 0))] + [rev((H, C, dk), lambda m: (0, m, 0))] * 4 + [
        rev((H, 8, dk), lambda m: (0, m, 0)), rev((1, H, dk, dk), lambda m: (m, 0, 0, 0)),
        rev((H, C, dk), lambda m: (0, m, 0))]
    out_specs = [rev((H, C, C), lambda m: (0, m, 0))] + [rev((H, C, dk), lambda m: (0, m, 0))] * 4 + [
        rev((H, 8, dk), lambda m: (0, m, 0))]
    out_shape = [jax.ShapeDtypeStruct((H, T, C), F32)] + [jax.ShapeDtypeStruct((H, T, dk), F32)] * 4 + [
        jax.ShapeDtypeStruct((H, 8 * n, dk), F32)]
    return pl.pallas_call(
        body, name="gdn_scan_bwd", grid=(n,),
        in_specs=in_specs, out_specs=out_specs, out_shape=out_shape,
        scratch_shapes=[pltpu.VMEM((H, dk, dk), F32)],
        compiler_params=_cp(("arbitrary",)),
    )(*xs, states, do)


def _gdn_post_fwd(o, proj, norm_w):
    H, T, dk = o.shape
    tt = _tile(T, 1024)
    zoff = 3 * GDN_HEADS

    def body(o_ref, z_ref, g_ref, y_ref):
        ov = o_ref[0]
        r = lax.rsqrt(jnp.mean(ov * ov, axis=-1, keepdims=True) + EPS)
        y_ref[...] = (ov * r * g_ref[...] * _silu(z_ref[...])).astype(BF16)

    return pl.pallas_call(
        body, name="gdn_post_fwd", grid=(H, T // tt),
        in_specs=[pl.BlockSpec((1, tt, dk), lambda h, i: (h, i, 0)), pl.BlockSpec((tt, dk), lambda h, i: (i, zoff + h)),
                  pl.BlockSpec((1, dk), lambda h, i: (0, 0))],
        out_specs=pl.BlockSpec((tt, dk), lambda h, i: (i, h)),
        out_shape=jax.ShapeDtypeStruct((T, H * dk), BF16),
        compiler_params=_cp(("parallel", "parallel")),
    )(o, proj, norm_w)


def _gdn_post_bwd(o, proj, norm_w, dy):
    H, T, dk = o.shape
    tt = _tile(T, 1024)
    zoff = 3 * GDN_HEADS

    def body(o_ref, z_ref, g_ref, dy_ref, do_ref, dz_ref, dg_ref):
        i = pl.program_id(1)
        ov, z, g, dyv = o_ref[0], z_ref[...], g_ref[...], dy_ref[...]
        r = lax.rsqrt(jnp.mean(ov * ov, axis=-1, keepdims=True) + EPS)
        on = ov * r
        sz = _silu(z)
        dz_ref[...] = (dyv * on * g * _dsilu(z)).astype(BF16)
        dn = dyv * sz
        gdn = dn * g
        do_ref[0] = r * (gdn - on * jnp.mean(gdn * on, axis=-1, keepdims=True))

        @pl.when(i == 0)
        def _():
            dg_ref[...] = jnp.zeros_like(dg_ref)

        dg_ref[0] += jnp.sum(dn * on, axis=0, keepdims=True)

    return pl.pallas_call(
        body, name="gdn_post_bwd", grid=(H, T // tt),
        in_specs=[pl.BlockSpec((1, tt, dk), lambda h, i: (h, i, 0)), pl.BlockSpec((tt, dk), lambda h, i: (i, zoff + h)),
                  pl.BlockSpec((1, dk), lambda h, i: (0, 0)), pl.BlockSpec((tt, dk), lambda h, i: (i, h))],
        out_specs=[pl.BlockSpec((1, tt, dk), lambda h, i: (h, i, 0)), pl.BlockSpec((tt, dk), lambda h, i: (i, h)),
                   pl.BlockSpec((1, 1, dk), lambda h, i: (h, 0, 0))],
        out_shape=[jax.ShapeDtypeStruct((H, T, dk), F32), jax.ShapeDtypeStruct((T, H * dk), BF16),
                   jax.ShapeDtypeStruct((H, 1, dk), F32)],
        compiler_params=_cp(("parallel", "arbitrary")),
    )(o, proj, norm_w, dy)


def _ple_fwd(x, pp, gl, name):
    T, D = x.shape
    tt = _tile(T, 512)

    def body(x_ref, p_ref, g_ref, o_ref):
        o_ref[...] = x_ref[...] + p_ref[...] * _sigmoid(g_ref[...])

    row = pl.BlockSpec((tt, D), lambda i: (i, 0))
    return pl.pallas_call(
        body, name=name, grid=(T // tt,), in_specs=[row, row, row], out_specs=row,
        out_shape=jax.ShapeDtypeStruct((T, D), F32), compiler_params=_cp(("parallel",)),
    )(x, pp, gl)


def _ple_bwd(dx, pp, gl, name):
    T, D = dx.shape
    tt = _tile(T, 512)

    def body(dx_ref, p_ref, g_ref, dp_ref, dg_ref):
        s = _sigmoid(g_ref[...])
        dxv = dx_ref[...]
        dp_ref[...] = (dxv * s).astype(BF16)
        dg_ref[...] = (dxv * p_ref[...] * s * (1.0 - s)).astype(BF16)

    row = pl.BlockSpec((tt, D), lambda i: (i, 0))
    return pl.pallas_call(
        body, name=name, grid=(T // tt,), in_specs=[row, row, row], out_specs=[row, row],
        out_shape=[jax.ShapeDtypeStruct((T, D), BF16)] * 2, compiler_params=_cp(("parallel",)),
    )(dx, pp, gl)


def _heads_in(a2d):
    T = a2d.shape[0]
    a = a2d.reshape(T, 3, SB_HEADS, SB_HEAD_DIM).transpose(1, 2, 0, 3).astype(BF16)
    return a[0], a[1], a[2]


def _heads_out(a):
    H, T, dh = a.shape
    return a.transpose(1, 0, 2).reshape(T, H * dh)


def _ffn_fwd(x, norm, w_up, conv_w, w_down, tag):
    hf = _rms_fwd(x, norm, f"ffn_norm{tag}")
    up = _mm(hf, w_up, "nn", f"ffn_up{tag}")
    act = _ffn_act_fwd(up, conv_w, f"ffn_act{tag}")
    x_out = _mm(act, w_down, "nn", f"ffn_down{tag}", res=x, tn=1024, tk=1408)
    return x_out, (x, hf, up, act)


def _ffn_bwd(dx_out, saved, norm, w_up, conv_w, w_down, tag):
    x, hf, up, act = saved
    dact = _mm(dx_out, w_down, "nt", f"ffn_dact{tag}", out_dtype=BF16, tn=1408)
    dw_down = _mm(act, dx_out, "tn", f"ffn_dwdown{tag}", tm=1408, tn=1024, tk=512)
    dgate, dval, dcw_g, dcw_v = _ffn_act_bwd(up, conv_w, dact, f"ffn_dact_conv{tag}")
    dup = jnp.concatenate([dgate, dval], axis=1)
    dw_up = _mm(hf, dup, "tn", f"ffn_dwup{tag}", tm=1024, tk=512)
    dhf = _mm(dup, w_up, "nt", f"ffn_dhf{tag}", tn=1024, tk=1408)
    dx, dnorm = _rms_bwd(x, norm, dhf, dx_out, f"ffn_dnorm{tag}")
    return dx, dnorm, dw_up, jnp.concatenate([dcw_g, dcw_v], axis=1), dw_down


def _ple_layer_fwd(x, p_i, norm, w_gate, w_ple, tag):
    hg = _rms_fwd(x, norm, f"ple_norm{tag}")
    gl = _mm(hg, w_gate, "nn", f"ple_gate{tag}", tn=1024)
    pp = _mm(p_i, w_ple, "nn", f"ple_proj{tag}", tn=1024)
    return _ple_fwd(x, pp, gl, f"ple_mix{tag}"), (x, hg, gl, pp)


def _ple_layer_bwd(dx_out, saved, p_i, norm, w_gate, tag):
    x, hg, gl, pp = saved
    dpp, dgl = _ple_bwd(dx_out, pp, gl, f"ple_dmix{tag}")
    dw_ple = _mm(p_i, dpp, "tn", f"ple_dwple{tag}", tm=256, tn=1024, tk=512)
    dw_gate = _mm(hg, dgl, "tn", f"ple_dwgate{tag}", tm=1024, tn=1024, tk=512)
    dhg = _mm(dgl, w_gate, "nt", f"ple_dhg{tag}", tn=1024)
    dx, dnorm = _rms_bwd(x, norm, dhg, dx_out, f"ple_dnorm{tag}")
    return dx, dnorm, dw_gate, dw_ple


def _local_step(x, p, target, W):
    T = x.shape[0]
    H = GDN_HEADS
    G = {}

    hn_e = _rms_fwd(x, W["mix_norm_e"], "mix_norm_e")
    proj_e = _mm(hn_e, W["w_in_e"], "nn", "in_e")
    pool_out = _pool_fwd(proj_e, W["pool_w"], W["pool_scale"])
    q, k, v = _heads_in(proj_e[:, POOL_WIDTH:])
    attn, ltot = _sb_fwd(q, k, v)
    mix_e = jnp.concatenate([pool_out, _heads_out(attn)], axis=1).astype(BF16)
    x1 = _mm(mix_e, W["w_out_e"], "nn", "out_e", res=x, tn=1024)
    x2, ffn0 = _ffn_fwd(x1, W["ffn_norm"][0:1], W["w_up"][0], W["ffn_conv"][0], W["w_down"][0], "0")
    x3, ple0 = _ple_layer_fwd(x2, p[0], W["ple_norm"][0:1], W["w_ple_gate"][0], W["w_ple"][0], "0")

    hn_o = _rms_fwd(x3, W["mix_norm_o"], "mix_norm_o")
    proj_o = _mm(hn_o, W["w_in_o"], "nn", "in_o", tn=384)
    qkv = _gdn_pre_fwd(proj_o, W["conv_qkv_o"])
    ba = proj_o[:, 4 * D_MODEL:4 * D_MODEL + 2 * H]
    b_h = ba[:, :H].T.reshape(H, T, 1)
    a_h = ba[:, H:].T.reshape(H, T, 1)
    alog = W["a_log_o"].reshape(H, 1, 1)
    dtb = W["dt_bias_o"].reshape(H, 1, 1)
    xs = _gdn_prep_fwd(qkv, b_h, a_h, alog, dtb)
    o, states = _gdn_scan_fwd(xs)
    og = _gdn_post_fwd(o, proj_o, W["gdn_norm_o"])
    x4 = _mm(og, W["w_out_o"], "nn", "out_o", res=x3, tn=1024)
    x5, ffn1 = _ffn_fwd(x4, W["ffn_norm"][1:2], W["w_up"][1], W["ffn_conv"][1], W["w_down"][1], "1")
    x6, ple1 = _ple_layer_fwd(x5, p[1], W["ple_norm"][1:2], W["w_ple_gate"][1], W["w_ple"][1], "1")

    sq, dx6, G["final_norm"] = _final_loss(x6, W["final_norm"], target, "final_loss")

    dx5, dpn1, dwg1, dwp1 = _ple_layer_bwd(dx6, ple1, p[1], W["ple_norm"][1:2], W["w_ple_gate"][1], "1")
    dx4, dfn1, dwu1, dfc1, dwd1 = _ffn_bwd(dx5, ffn1, W["ffn_norm"][1:2], W["w_up"][1], W["ffn_conv"][1],
                                           W["w_down"][1], "1")
    dog = _mm(dx4, W["w_out_o"], "nt", "d_og", tn=1024)
    G["w_out_o"] = _mm(og, dx4, "tn", "dw_out_o", tm=1024, tn=1024, tk=512)
    do, dz, dgn = _gdn_post_bwd(o, proj_o, W["gdn_norm_o"], dog)
    G["gdn_norm_o"] = jnp.sum(dgn, axis=0)
    dxs = _gdn_scan_bwd(xs, states, do)
    dq, dk, dv, db, da, dal, ddt = _gdn_prep_bwd(qkv, b_h, a_h, alog, dtb, dxs)
    G["a_log_o"] = dal.reshape(1, H)
    G["dt_bias_o"] = ddt.reshape(1, H)
    dqkv_act = jnp.concatenate([_t_heads(dq), _t_heads(dk), _t_heads(dv)], axis=1)
    dqkv, G["conv_qkv_o"] = _gdn_pre_bwd(proj_o, W["conv_qkv_o"], dqkv_act)
    dba = jnp.concatenate([db.reshape(H, T).T, da.reshape(H, T).T,
                           jnp.zeros((T, ODD_IN_PAD - ODD_IN + 0), F32)], axis=1).astype(BF16)
    dproj_o = jnp.concatenate([dqkv, dz, dba], axis=1)
    G["w_in_o"] = _mm(hn_o, dproj_o, "tn", "dw_in_o", tm=1024, tn=384, tk=512)
    dhn_o = _mm(dproj_o, W["w_in_o"], "nt", "d_hn_o", tn=1024, tk=1408)
    dx3, G["mix_norm_o"] = _rms_bwd(x3, W["mix_norm_o"], dhn_o, dx4, "d_mix_norm_o")

    dx2, dpn0, dwg0, dwp0 = _ple_layer_bwd(dx3, ple0, p[0], W["ple_norm"][0:1], W["w_ple_gate"][0], "0")
    dx1, dfn0, dwu0, dfc0, dwd0 = _ffn_bwd(dx2, ffn0, W["ffn_norm"][0:1], W["w_up"][0], W["ffn_conv"][0],
                                           W["w_down"][0], "0")
    dmix = _mm(dx1, W["w_out_e"], "nt", "d_mix_e", tn=1024)
    G["w_out_e"] = _mm(mix_e, dx1, "tn", "dw_out_e", tm=1024, tn=1024, tk=512)
    du, G["pool_w"], G["pool_scale"] = _pool_bwd(proj_e, W["pool_w"], W["pool_scale"], dmix)
    dattn = dmix[:, POOL_WIDTH:].reshape(T, SB_HEADS, SB_HEAD_DIM).transpose(1, 0, 2)
    dqa, dka, dva = _sb_bwd(q, k, v, dattn, ltot)
    dproj_e = jnp.concatenate([du, _heads_out(dqa), _heads_out(dka), _heads_out(dva)], axis=1).astype(BF16)
    G["w_in_e"] = _mm(hn_e, dproj_e, "tn", "dw_in_e", tm=1024, tk=512)
    dhn_e = _mm(dproj_e, W["w_in_e"], "nt", "d_hn_e", tn=1024)
    grad_x, G["mix_norm_e"] = _rms_bwd(x, W["mix_norm_e"], dhn_e, dx1, "d_mix_norm_e")

    G["ffn_norm"] = jnp.concatenate([dfn0, dfn1], axis=0)
    G["ple_norm"] = jnp.concatenate([dpn0, dpn1], axis=0)
    G["w_up"] = jnp.stack([dwu0, dwu1])
    G["ffn_conv"] = jnp.stack([dfc0, dfc1])
    G["w_down"] = jnp.stack([dwd0, dwd1])
    G["w_ple_gate"] = jnp.stack([dwg0, dwg1])
    G["w_ple"] = jnp.stack([dwp0, dwp1])
    return sq[0, 0], grad_x, G


def _t_heads(a):
    H, T, dk = a.shape
    return a.transpose(1, 0, 2).reshape(T, H * dk)


SHARDED_BIG = (("w_in_e", 2), ("w_out_e", 1), ("w_in_o", 2), ("w_out_o", 1), ("w_up", 2), ("w_down", 1),
               ("w_ple_gate", 1), ("w_ple", 2))
SHARDED_SMALL = (("mix_norm_o", 1), ("conv_qkv_o", 2), ("ffn_conv", 2))
SHARDED = SHARDED_BIG + SHARDED_SMALL
REPLICATED = ("mix_norm_e", "pool_w", "pool_scale", "a_log_o", "dt_bias_o", "gdn_norm_o", "ffn_norm", "ple_norm",
              "final_norm")
WEIGHT_ORDER = ("mix_norm_e", "w_in_e", "pool_w", "pool_scale", "w_out_e", "mix_norm_o", "w_in_o", "conv_qkv_o",
                "a_log_o", "dt_bias_o", "gdn_norm_o", "w_out_o", "ffn_norm", "w_up", "ffn_conv", "w_down", "ple_norm",
                "w_ple_gate", "w_ple", "final_norm")
PACK_W = 1024
PACK_ROWS = 32
SMALL_W = LANE
SMALL_ROWS = 16


def _size(shape):
    n = 1
    for s in shape:
        n *= s
    return n


def _pack(arrs, width, granule):
    flat = jnp.concatenate([a.reshape(-1) for a in arrs])
    rows = -(-flat.shape[0] // width)
    rows = -(-rows // granule) * granule
    return jnp.pad(flat, (0, rows * width - flat.shape[0])).reshape(rows, width)


def _unpack(flat2d, shapes):
    flat = flat2d.reshape(-1)
    out, off = [], 0
    for s in shapes:
        out.append(flat[off:off + _size(s)].reshape(s))
        off += _size(s)
    return out


MESH_ID = pl.DeviceIdType.MESH
HBM_SPEC = pl.BlockSpec(memory_space=pltpu.HBM)


def _where_am_i():
    return lax.axis_index("x"), lax.axis_index("y"), lax.axis_index("c")


def _other_chips(x, y):
    return [(1 - x, y), (x, 1 - y), (1 - x, 1 - y)]


def _remote(src, dst, send_sems, recv_sems, k, to):
    return pltpu.make_async_remote_copy(src_ref=src, dst_ref=dst, send_sem=send_sems.at[k], recv_sem=recv_sems.at[k],
                                        device_id=to, device_id_type=MESH_ID)


def _chip_allgather(pack, name):
    R, Wd = pack.shape
    Rh = R // 2

    def body(src_ref, out_ref, send_sems, recv_sems, local_sem):
        x, y, c = _where_am_i()
        me, sib = (x, y, c), (x, y, 1 - c)
        chips = _other_chips(x, y)
        mine_rows = pl.ds(pl.multiple_of(c * Rh, SMALL_ROWS), Rh)
        sib_rows = pl.ds(pl.multiple_of((1 - c) * Rh, SMALL_ROWS), Rh)
        j_me = 2 * x + y
        local = pltpu.make_async_copy(src_ref, out_ref.at[j_me], local_sem)
        local.start()
        first = [_remote(src_ref.at[mine_rows], out_ref.at[j_me, mine_rows], send_sems, recv_sems, k, (cx, cy, c))
                 for k, (cx, cy) in enumerate(chips)]
        for cp in first:
            cp.start()
        passed = []
        for k, (cx, cy) in enumerate(chips):
            blk = out_ref.at[2 * cx + cy, mine_rows]
            _remote(blk, blk, send_sems, recv_sems, k, me).wait_recv()
            fw = _remote(blk, blk, send_sems, recv_sems, 3 + k, sib)
            fw.start()
            passed.append(fw)
        for k, (cx, cy) in enumerate(chips):
            blk = out_ref.at[2 * cx + cy, sib_rows]
            _remote(blk, blk, send_sems, recv_sems, 3 + k, me).wait_recv()
        for cp in first + passed:
            cp.wait_send()
        local.wait()

    return pl.pallas_call(
        body, name=name, in_specs=[HBM_SPEC], out_specs=HBM_SPEC,
        out_shape=jax.ShapeDtypeStruct((N_CHIPS, R, Wd), pack.dtype),
        scratch_shapes=[pltpu.SemaphoreType.DMA((6,)), pltpu.SemaphoreType.DMA((6,)), pltpu.SemaphoreType.DMA],
    )(pack)


def _sibling_swap(src, name):
    n, R, Wd = src.shape
    Rh = R // 2

    def body(src_ref, out_ref, send_sems, recv_sems):
        x, y, c = _where_am_i()
        theirs = pl.ds(pl.multiple_of((1 - c) * Rh, SMALL_ROWS), Rh)
        cp = _remote(src_ref.at[:, theirs], out_ref, send_sems, recv_sems, 0, (x, y, 1 - c))
        cp.start()
        cp.wait()

    return pl.pallas_call(
        body, name=name, in_specs=[HBM_SPEC], out_specs=HBM_SPEC,
        out_shape=jax.ShapeDtypeStruct((n, Rh, Wd), src.dtype),
        scratch_shapes=[pltpu.SemaphoreType.DMA((1,)), pltpu.SemaphoreType.DMA((1,))],
    )(src)


def _chip_scatter(cs, name):
    _, Rh, Wd = cs.shape

    def body(src_ref, out_ref, send_sems, recv_sems):
        x, y, c = _where_am_i()
        cps = [_remote(src_ref.at[2 * cx + cy], out_ref.at[k], send_sems, recv_sems, k, (cx, cy, c))
               for k, (cx, cy) in enumerate(_other_chips(x, y))]
        for cp in cps:
            cp.start()
        for cp in cps:
            cp.wait()

    return pl.pallas_call(
        body, name=name, in_specs=[HBM_SPEC], out_specs=HBM_SPEC,
        out_shape=jax.ShapeDtypeStruct((N_CHIPS - 1, Rh, Wd), cs.dtype),
        scratch_shapes=[pltpu.SemaphoreType.DMA((3,)), pltpu.SemaphoreType.DMA((3,))],
    )(cs)


def _sibling_join(half, name):
    Rh, Wd = half.shape

    def body(src_ref, out_ref, send_sems, recv_sems, local_sem):
        x, y, c = _where_am_i()
        rows = pl.ds(pl.multiple_of(c * Rh, SMALL_ROWS // 2), Rh)
        local = pltpu.make_async_copy(src_ref, out_ref.at[rows], local_sem)
        local.start()
        cp = _remote(src_ref, out_ref.at[rows], send_sems, recv_sems, 0, (x, y, 1 - c))
        cp.start()
        cp.wait()
        local.wait()

    return pl.pallas_call(
        body, name=name, in_specs=[HBM_SPEC], out_specs=HBM_SPEC,
        out_shape=jax.ShapeDtypeStruct((2 * Rh, Wd), half.dtype),
        scratch_shapes=[pltpu.SemaphoreType.DMA((1,)), pltpu.SemaphoreType.DMA((1,)), pltpu.SemaphoreType.DMA],
    )(half)


def _row_tile(rows, pref=512):
    best = 8
    for t in range(8, pref + 1, 8):
        if rows % t == 0:
            best = t
    return best


def _where_ids():
    x, y, c = _where_am_i()
    return jnp.stack([c, 2 * x + y]).astype(jnp.int32)


def _chip_sums_bf16(P, A, ids):
    n, R, Wd = P.shape
    Rh = R // 2
    tr = _row_tile(Rh)
    nb = Rh // tr

    def body(ids_ref, p_ref, a_ref, o_ref):
        o_ref[...] = (p_ref[...] + a_ref[...]).astype(BF16)

    return pl.pallas_call(
        body, name="rs_chip_sums",
        grid_spec=pltpu.PrefetchScalarGridSpec(
            num_scalar_prefetch=1, grid=(n, nb),
            in_specs=[pl.BlockSpec((1, tr, Wd), lambda j, i, ids: (j, ids[0] * nb + i, 0)),
                      pl.BlockSpec((1, tr, Wd), lambda j, i, ids: (j, i, 0))],
            out_specs=pl.BlockSpec((1, tr, Wd), lambda j, i, ids: (j, i, 0))),
        out_shape=jax.ShapeDtypeStruct((n, Rh, Wd), BF16),
        compiler_params=_cp(("parallel", "parallel")),
    )(ids, P, A)


def _total_half(P, A, B, ids):
    n, R, Wd = P.shape
    Rh = R // 2
    tr = _row_tile(Rh)
    nb = Rh // tr

    def body(ids_ref, p_ref, a_ref, b_ref, o_ref):
        s = p_ref[0] + a_ref[0]
        for k in range(N_CHIPS - 1):
            s = s + b_ref[k].astype(F32)
        o_ref[...] = s

    return pl.pallas_call(
        body, name="rs_total",
        grid_spec=pltpu.PrefetchScalarGridSpec(
            num_scalar_prefetch=1, grid=(nb,),
            in_specs=[pl.BlockSpec((1, tr, Wd), lambda i, ids: (ids[1], ids[0] * nb + i, 0)),
                      pl.BlockSpec((1, tr, Wd), lambda i, ids: (ids[1], i, 0)),
                      pl.BlockSpec((N_CHIPS - 1, tr, Wd), lambda i, ids: (0, i, 0))],
            out_specs=pl.BlockSpec((tr, Wd), lambda i, ids: (i, 0))),
        out_shape=jax.ShapeDtypeStruct((Rh, Wd), F32),
        compiler_params=_cp(("parallel",)),
    )(ids, P, A, B)


def _small_allreduce(v, name):
    R, Wd = v.shape

    def body(x_ref, sum_ref, all_ref, send_sems, recv_sems, local_sem):
        x, y, c = _where_am_i()
        me, sib = (x, y, c), (x, y, 1 - c)
        chips = _other_chips(x, y)

        def slot(px, py, pc):
            return all_ref.at[4 * px + 2 * py + pc]

        local = pltpu.make_async_copy(x_ref, slot(*me), local_sem)
        local.start()
        first = [_remote(x_ref, slot(*me), send_sems, recv_sems, 0, sib)]
        first += [_remote(x_ref, slot(*me), send_sems, recv_sems, 1 + k, (cx, cy, c)) for k, (cx, cy) in enumerate(chips)]
        for cp in first:
            cp.start()
        passed = []
        for k, (cx, cy) in enumerate(chips):
            blk = slot(cx, cy, c)
            _remote(blk, blk, send_sems, recv_sems, 1 + k, me).wait_recv()
            fw = _remote(blk, blk, send_sems, recv_sems, 4 + k, sib)
            fw.start()
            passed.append(fw)
        _remote(slot(*sib), slot(*sib), send_sems, recv_sems, 0, me).wait_recv()
        for k, (cx, cy) in enumerate(chips):
            blk = slot(cx, cy, 1 - c)
            _remote(blk, blk, send_sems, recv_sems, 4 + k, me).wait_recv()
        for cp in first + passed:
            cp.wait_send()
        local.wait()
        s = all_ref[0]
        for d in range(1, N_DEV):
            s = s + all_ref[d]
        sum_ref[...] = s

    vm = pl.BlockSpec(memory_space=pltpu.VMEM)
    return pl.pallas_call(
        body, name=name, in_specs=[vm], out_specs=[vm, vm],
        out_shape=[jax.ShapeDtypeStruct((R, Wd), F32), jax.ShapeDtypeStruct((N_DEV, R, Wd), F32)],
        scratch_shapes=[pltpu.SemaphoreType.DMA((7,)), pltpu.SemaphoreType.DMA((7,)), pltpu.SemaphoreType.DMA],
    )(v)[0]


def _adamw(w, g, m, v, name):
    R, Wd = w.shape
    tr = _row_tile(R)
    c1 = 1.0 - ADAM_B1 ** ADAM_STEP
    c2 = 1.0 - ADAM_B2 ** ADAM_STEP

    def body(w_ref, g_ref, m_ref, v_ref, d_ref, nm_ref, nv_ref):
        gv = g_ref[...]
        nm = ADAM_B1 * m_ref[...] + (1.0 - ADAM_B1) * gv
        nv = ADAM_B2 * v_ref[...] + (1.0 - ADAM_B2) * (gv * gv)
        d_ref[...] = -ADAM_LR * ((nm / c1) / (jnp.sqrt(nv / c2) + ADAM_EPS) + ADAM_WD * w_ref[...])
        nm_ref[...] = nm
        nv_ref[...] = nv

    row = pl.BlockSpec((tr, Wd), lambda i: (i, 0))
    shp = jax.ShapeDtypeStruct((R, Wd), F32)
    return pl.pallas_call(
        body, name=name, grid=(R // tr,), in_specs=[row] * 4, out_specs=[row] * 3, out_shape=[shp] * 3,
        compiler_params=_cp(("parallel",)),
    )(w, g, m, v)


def _gather_weights(P):
    big_shapes = [P[n].shape for n, _ in SHARDED_BIG]
    small_shapes = [P[n].shape for n, _ in SHARDED_SMALL]
    big = _chip_allgather(_pack([P[n] for n, _ in SHARDED_BIG], PACK_W, PACK_ROWS).astype(BF16), "ag_weights")
    small = _chip_allgather(_pack([P[n] for n, _ in SHARDED_SMALL], SMALL_W, SMALL_ROWS), "ag_small")
    full = {}
    for gathered, group, shapes in ((big, SHARDED_BIG, big_shapes), (small, SHARDED_SMALL, small_shapes)):
        parts = [_unpack(gathered[j], shapes) for j in range(N_CHIPS)]
        for i, (n, ax) in enumerate(group):
            full[n] = jnp.concatenate([parts[j][i] for j in range(N_CHIPS)], axis=ax)
    W = {n: P[n] for n in REPLICATED}
    W["pool_w"] = P["pool_w"][0]
    W["final_norm"] = P["final_norm"].reshape(1, D_MODEL)
    W["w_in_e"] = full["w_in_e"][0]
    W["w_out_e"] = full["w_out_e"][0]
    W["w_in_o"] = jnp.pad(full["w_in_o"][0], ((0, 0), (0, ODD_IN_PAD - ODD_IN)))
    W["w_out_o"] = full["w_out_o"][0]
    W["mix_norm_o"] = full["mix_norm_o"]
    W["conv_qkv_o"] = full["conv_qkv_o"][0]
    for n in ("w_up", "ffn_conv", "w_down", "w_ple_gate", "w_ple"):
        W[n] = full[n]
    return W


def _reduce_gradients(G, P):
    H = GDN_HEADS
    full = dict(G)
    full["w_in_e"] = G["w_in_e"][None]
    full["w_out_e"] = G["w_out_e"][None]
    full["w_in_o"] = G["w_in_o"][None, :, :ODD_IN]
    full["w_out_o"] = G["w_out_o"][None]
    full["conv_qkv_o"] = G["conv_qkv_o"][None]
    full["pool_w"] = G["pool_w"][None]
    full["final_norm"] = G["final_norm"].reshape(D_MODEL)

    pieces = [[] for _ in range(N_CHIPS)]
    for n, ax in SHARDED:
        for j, part in enumerate(jnp.split(full[n], N_CHIPS, axis=ax)):
            pieces[j].append(part)
    packs = jnp.stack([_pack(pc, PACK_W, PACK_ROWS) for pc in pieces])
    ids = _where_ids()
    from_sibling = _sibling_swap(packs, "rs_sibling_swap")
    chip_sums = _chip_sums_bf16(packs, from_sibling, ids)
    from_chips = _chip_scatter(chip_sums, "rs_chip_scatter")
    half = _total_half(packs, from_sibling, from_chips, ids)
    g_sharded = _sibling_join(half, "rs_sibling_join")

    g_repl = _small_allreduce(_pack([full[n] for n in REPLICATED], SMALL_W, 8), "ar_small")
    return g_sharded, g_repl


def kernel(x, p, mix_norm_e, w_in_e, pool_w, pool_scale, w_out_e, mix_norm_o, w_in_o, conv_qkv_o, a_log_o, dt_bias_o, gdn_norm_o, w_out_o, ffn_norm, w_up, ffn_conv, w_down, ple_norm, w_ple_gate, w_ple, final_norm, loss_target, m_mix_norm_e, m_w_in_e, m_pool_w, m_pool_scale, m_w_out_e, m_mix_norm_o, m_w_in_o, m_conv_qkv_o, m_a_log_o, m_dt_bias_o, m_gdn_norm_o, m_w_out_o, m_ffn_norm, m_w_up, m_ffn_conv, m_w_down, m_ple_norm, m_w_ple_gate, m_w_ple, m_final_norm, v_mix_norm_e, v_w_in_e, v_pool_w, v_pool_scale, v_w_out_e, v_mix_norm_o, v_w_in_o, v_conv_qkv_o, v_a_log_o, v_dt_bias_o, v_gdn_norm_o, v_w_out_o, v_ffn_norm, v_w_up, v_ffn_conv, v_w_down, v_ple_norm, v_w_ple_gate, v_w_ple, v_final_norm):
    args = locals()
    P = {n: args[n] for n in WEIGHT_ORDER}
    M = {n: args["m_" + n] for n in WEIGHT_ORDER}
    V = {n: args["v_" + n] for n in WEIGHT_ORDER}

    W = _gather_weights(P)
    sq, grad_x, G = _local_step(x[0], p[:, 0], loss_target[0], W)
    g_sharded, g_repl = _reduce_gradients(G, P)

    def pack_sharded(D):
        return _pack([D[n] for n, _ in SHARDED], PACK_W, PACK_ROWS)

    def pack_repl(D):
        return _pack([D[n] for n in REPLICATED], SMALL_W, 8)

    upd_s = _adamw(pack_sharded(P), g_sharded, pack_sharded(M), pack_sharded(V), "adamw_sharded")
    upd_r = _adamw(pack_repl(P), g_repl, pack_repl(M), pack_repl(V), "adamw_replicated")

    s_shapes = [P[n].shape for n, _ in SHARDED]
    r_shapes = [P[n].shape for n in REPLICATED]
    results = []
    for flat_s, flat_r in zip((g_sharded,) + tuple(upd_s), (g_repl,) + tuple(upd_r)):
        by_name = dict(zip([n for n, _ in SHARDED], _unpack(flat_s, s_shapes)))
        by_name.update(zip(REPLICATED, _unpack(flat_r, r_shapes)))
        results.append([by_name[n] for n in WEIGHT_ORDER])

    loss = (0.5 / D_MODEL) * lax.psum(sq, ("x", "y", "c"))
    return (loss, grad_x[None], *results[0], *results[1], *results[2], *results[3])
```

```python
import functools

import jax
import jax.numpy as jnp
from jax import lax
from jax.experimental import pallas as pl
from jax.experimental.pallas import tpu as pltpu

F32 = jnp.float32
BF16 = jnp.bfloat16

D_MODEL = 1024
PLE_DIM = 256
POOL_WIDTH = 512
POOL_WINDOWS = (2, 4, 8, 16)
POOL_GROUP_DIM = 128
SB_HEADS = 8
SB_HEAD_DIM = 64
GDN_HEADS = 8
GDN_HEAD_DIM = 128
GDN_CONV = 4
GDN_CHUNK = 64
FFN_DIM = 2816
FFN_CONV = 3
EPS = 1e-6
ODD_IN = 4 * D_MODEL + 2 * GDN_HEADS
ODD_IN_PAD = 33 * 128
ADAM_LR, ADAM_B1, ADAM_B2, ADAM_EPS, ADAM_WD, ADAM_STEP = 0.001, 0.9, 0.999, 1e-08, 0.01, 10

LANE = 128
VMEM_LIMIT = 56 * 1024 * 1024

N_CHIPS = 4
N_DEV = 8


def _cp(sem=None):
    return pltpu.CompilerParams(dimension_semantics=sem, vmem_limit_bytes=VMEM_LIMIT)


def _tile(n, pref):
    if n <= pref:
        return n
    best = None
    for t in range(LANE, pref + 1, LANE):
        if n % t == 0:
            best = t
    assert best is not None, (n, pref)
    return best


_DIMS = {"nn": (((1,), (0,)), ((), ())), "nt": (((1,), (1,)), ((), ())), "tn": (((0,), (0,)), ((), ()))}
_BDIMS = {"nn": (((2,), (1,)), ((0,), (0,))), "nt": (((2,), (2,)), ((0,), (0,))), "tn": (((1,), (1,)), ((0,), (0,)))}


def _dims(mode, ndim):
    return (_BDIMS if ndim == 3 else _DIMS)[mode]


def _dot(a, b, mode="nn"):
    return lax.dot_general(a.astype(BF16), b.astype(BF16), _dims(mode, a.ndim), preferred_element_type=F32)


def _bdot(a, b, mode="nn"):
    return lax.dot_general(a.astype(BF16), b.astype(BF16), _BDIMS[mode], preferred_element_type=F32)


def _split2(x):
    hi = x.astype(BF16)
    lo = (x - hi.astype(F32)).astype(BF16)
    return hi, lo


def _split3(x):
    hi = x.astype(BF16)
    r = x - hi.astype(F32)
    mid = r.astype(BF16)
    lo = (r - mid.astype(F32)).astype(BF16)
    return hi, mid, lo


def _dot_x01(x, m01, mode="nn"):
    hi, lo = _split2(x)
    return (lax.dot_general(hi, m01, _DIMS[mode], preferred_element_type=F32)
            + lax.dot_general(lo, m01, _DIMS[mode], preferred_element_type=F32))


def _dot3_raw(a, b, mode):
    ah, al = _split2(a)
    bh, bl = _split2(b)
    d = _dims(mode, a.ndim)
    return (lax.dot_general(ah, bh, d, preferred_element_type=F32)
            + lax.dot_general(ah, bl, d, preferred_element_type=F32)
            + lax.dot_general(al, bh, d, preferred_element_type=F32))


@jax.custom_vjp
def _dot3(a, b):
    return _dot3_raw(a, b, "nn")


def _dot3_fwd(a, b):
    return _dot3_raw(a, b, "nn"), (a, b)


def _dot3_bwd(res, g):
    a, b = res
    return _dot3_raw(g, b, "nt"), _dot3_raw(a, g, "tn")


_dot3.defvjp(_dot3_fwd, _dot3_bwd)


@jax.custom_vjp
def _dot1_nt(a, b):
    return _dot(a, b, "nt")


def _dot1_nt_fwd(a, b):
    return _dot(a, b, "nt"), (a, b)


def _dot1_nt_bwd(res, g):
    a, b = res
    return _dot(g, b, "nn"), _dot(g, a, "tn")


_dot1_nt.defvjp(_dot1_nt_fwd, _dot1_nt_bwd)


def _m01_left_raw(m, x):
    d = _dims("nn", x.ndim)
    if x.ndim == 3:
        m = jnp.broadcast_to(m, (x.shape[0],) + m.shape)
    p0, p1, p2 = _split3(x)
    return (lax.dot_general(m, p0, d, preferred_element_type=F32)
            + lax.dot_general(m, p1, d, preferred_element_type=F32)
            + lax.dot_general(m, p2, d, preferred_element_type=F32))


@jax.custom_vjp
def _m01_left(m, mt, x):
    return _m01_left_raw(m, x)


def _m01_left_fwd(m, mt, x):
    return _m01_left_raw(m, x), (m, mt)


def _m01_left_bwd(res, g):
    m, mt = res
    return jnp.zeros_like(m), jnp.zeros_like(mt), _m01_left_raw(mt, g)


_m01_left.defvjp(_m01_left_fwd, _m01_left_bwd)


def _softplus(x):
    return jnp.maximum(x, 0.0) + jnp.log(1.0 + jnp.exp(-jnp.abs(x)))


def _sigmoid(x):
    return 1.0 / (1.0 + jnp.exp(-x))


def _silu(x):
    return x * _sigmoid(x)


def _dsilu(x):
    s = _sigmoid(x)
    return s * (1.0 + x * (1.0 - s))


def _mm(a, b, mode, name, out_dtype=F32, res=None, tm=512, tn=512, tk=1024):
    if mode == "nn":
        (M, K), (K2, N) = a.shape, b.shape
    elif mode == "nt":
        (M, K), (N, K2) = a.shape, b.shape
    else:
        (K, M), (K2, N) = a.shape, b.shape
    assert K == K2, (name, a.shape, b.shape)
    tm, tn, tk = _tile(M, tm), _tile(N, tn), _tile(K, tk)
    nk = K // tk
    a_spec = {"nn": pl.BlockSpec((tm, tk), lambda i, j, k: (i, k)),
              "nt": pl.BlockSpec((tm, tk), lambda i, j, k: (i, k)),
              "tn": pl.BlockSpec((tk, tm), lambda i, j, k: (k, i))}[mode]
    b_spec = {"nn": pl.BlockSpec((tk, tn), lambda i, j, k: (k, j)),
              "nt": pl.BlockSpec((tn, tk), lambda i, j, k: (j, k)),
              "tn": pl.BlockSpec((tk, tn), lambda i, j, k: (k, j))}[mode]
    o_spec = pl.BlockSpec((tm, tn), lambda i, j, k: (i, j))
    has_res = res is not None

    def body(*refs):
        if has_res:
            a_ref, b_ref, r_ref, o_ref, acc = refs
        else:
            a_ref, b_ref, o_ref, acc = refs
        k = pl.program_id(2)

        @pl.when(k == 0)
        def _():
            acc[...] = jnp.zeros_like(acc)

        acc[...] += _dot(a_ref[...], b_ref[...], mode)

        @pl.when(k == nk - 1)
        def _():
            r = acc[...]
            if has_res:
                r = r + r_ref[...]
            o_ref[...] = r.astype(out_dtype)

    ins = [a, b] + ([res] if has_res else [])
    in_specs = [a_spec, b_spec] + ([o_spec] if has_res else [])
    return pl.pallas_call(
        body, name=name, grid=(M // tm, N // tn, nk),
        in_specs=in_specs, out_specs=o_spec,
        out_shape=jax.ShapeDtypeStruct((M, N), out_dtype),
        scratch_shapes=[pltpu.VMEM((tm, tn), F32)],
        compiler_params=_cp(("parallel", "parallel", "arbitrary")),
    )(*ins)


def _rms_fwd(x, gain, name):
    T, D = x.shape
    tt = _tile(T, 512)

    def body(x_ref, g_ref, o_ref):
        xv = x_ref[...]
        r = lax.rsqrt(jnp.mean(xv * xv, axis=-1, keepdims=True) + EPS)
        o_ref[...] = (xv * r * g_ref[...]).astype(BF16)

    return pl.pallas_call(
        body, name=name, grid=(T // tt,),
        in_specs=[pl.BlockSpec((tt, D), lambda i: (i, 0)), pl.BlockSpec((1, D), lambda i: (0, 0))],
        out_specs=pl.BlockSpec((tt, D), lambda i: (i, 0)),
        out_shape=jax.ShapeDtypeStruct((T, D), BF16),
        compiler_params=_cp(("parallel",)),
    )(x, gain)


def _rms_bwd(x, gain, dh, dres, name):
    T, D = x.shape
    tt = _tile(T, 512)

    def body(x_ref, g_ref, dh_ref, dr_ref, dx_ref, dg_ref):
        i = pl.program_id(0)
        xv = x_ref[...]
        dy = dh_ref[...].astype(F32)
        r = lax.rsqrt(jnp.mean(xv * xv, axis=-1, keepdims=True) + EPS)
        xn = xv * r
        gdy = dy * g_ref[...]
        dx = r * (gdy - xn * jnp.mean(gdy * xn, axis=-1, keepdims=True))
        dx_ref[...] = dr_ref[...] + dx

        @pl.when(i == 0)
        def _():
            dg_ref[...] = jnp.zeros_like(dg_ref)

        dg_ref[...] += jnp.sum(dy * xn, axis=0, keepdims=True)

    row = pl.BlockSpec((tt, D), lambda i: (i, 0))
    vec = pl.BlockSpec((1, D), lambda i: (0, 0))
    return pl.pallas_call(
        body, name=name, grid=(T // tt,),
        in_specs=[row, vec, row, row], out_specs=[row, vec],
        out_shape=[jax.ShapeDtypeStruct((T, D), F32), jax.ShapeDtypeStruct((1, D), F32)],
        compiler_params=_cp(("arbitrary",)),
    )(x, gain, dh, dres)


def _final_loss(x, gain, target, name):
    T, D = x.shape
    tt = _tile(T, 512)

    def body(x_ref, g_ref, t_ref, l_ref, dx_ref, dg_ref):
        i = pl.program_id(0)
        xv = x_ref[...]
        r = lax.rsqrt(jnp.mean(xv * xv, axis=-1, keepdims=True) + EPS)
        xn = xv * r
        err = xn * g_ref[...] - t_ref[...]
        dy = err * (1.0 / D)
        gdy = dy * g_ref[...]
        dx_ref[...] = r * (gdy - xn * jnp.mean(gdy * xn, axis=-1, keepdims=True))

        @pl.when(i == 0)
        def _():
            dg_ref[...] = jnp.zeros_like(dg_ref)
            l_ref[...] = jnp.zeros_like(l_ref)

        dg_ref[...] += jnp.sum(dy * xn, axis=0, keepdims=True)
        l_ref[...] += jnp.sum(jnp.sum(err * err, axis=1, keepdims=True), axis=0, keepdims=True)

    row = pl.BlockSpec((tt, D), lambda i: (i, 0))
    vec = pl.BlockSpec((1, D), lambda i: (0, 0))
    return pl.pallas_call(
        body, name=name, grid=(T // tt,),
        in_specs=[row, vec, row],
        out_specs=[pl.BlockSpec((8, LANE), lambda i: (0, 0)), row, vec],
        out_shape=[jax.ShapeDtypeStruct((8, LANE), F32), jax.ShapeDtypeStruct((T, D), F32),
                   jax.ShapeDtypeStruct((1, D), F32)],
        compiler_params=_cp(("arbitrary",)),
    )(x, gain, target)


def _shift_down(x, i, t_idx):
    if i == 0:
        return x
    return jnp.where(t_idx >= i, pltpu.roll(x, i, 0), 0.0)


def _shift_up(x, i, t_idx):
    if i == 0:
        return x
    n = x.shape[0]
    return jnp.where(t_idx < n - i, pltpu.roll(x, n - i, 0), 0.0)


def _pool_select(g, vals):
    out = vals[-1]
    for gi in range(len(vals) - 2, -1, -1):
        out = jnp.where(g == gi, vals[gi], out)
    return out


def _pool_y(u, g, t_idx):
    s1 = u + _shift_down(u, 1, t_idx)
    s2 = s1 + _shift_down(s1, 2, t_idx)
    s3 = s2 + _shift_down(s2, 4, t_idx)
    s4 = s3 + _shift_down(s3, 8, t_idx)
    ws = _pool_select(g, [s1, s2, s3, s4])
    win = _pool_select(g, [jnp.float32(w) for w in POOL_WINDOWS])
    cnt = jnp.minimum(t_idx.astype(F32) + 1.0, win)
    return ws / cnt - u, cnt


def _pool_fwd(proj, pool_w, pool_scale):
    T = proj.shape[0]
    G, C = len(POOL_WINDOWS), POOL_GROUP_DIM

    def body(u_ref, w_ref, s_ref, o_ref):
        g = pl.program_id(0)
        t_idx = lax.broadcasted_iota(jnp.int32, (T, C), 0)
        y, _ = _pool_y(u_ref[...], g, t_idx)
        o_ref[...] = _dot(y, w_ref[0]) * s_ref[...]

    return pl.pallas_call(
        body, name="pool_fwd", grid=(G,),
        in_specs=[pl.BlockSpec((T, C), lambda g: (0, g)), pl.BlockSpec((1, C, C), lambda g: (g, 0, 0)),
                  pl.BlockSpec((1, C), lambda g: (0, g))],
        out_specs=pl.BlockSpec((T, C), lambda g: (0, g)),
        out_shape=jax.ShapeDtypeStruct((T, G * C), F32),
        compiler_params=_cp(("parallel",)),
    )(proj, pool_w, pool_scale)


def _pool_bwd(proj, pool_w, pool_scale, dmix):
    T = proj.shape[0]
    G, C = len(POOL_WINDOWS), POOL_GROUP_DIM

    def body(u_ref, w_ref, s_ref, do_ref, du_ref, dw_ref, ds_ref):
        g = pl.program_id(0)
        t_idx = lax.broadcasted_iota(jnp.int32, (T, C), 0)
        y, cnt = _pool_y(u_ref[...], g, t_idx)
        w = w_ref[0]
        dout = do_ref[...]
        ds_ref[...] = jnp.sum(dout * _dot(y, w), axis=0, keepdims=True)
        dy2 = dout * s_ref[...]
        dw_ref[0] = _dot(y, dy2, "tn")
        dy = _dot(dy2, w, "nt")
        dz = dy / cnt
        r1 = dz + _shift_up(dz, 1, t_idx)
        r2 = r1 + _shift_up(r1, 2, t_idx)
        r3 = r2 + _shift_up(r2, 4, t_idx)
        r4 = r3 + _shift_up(r3, 8, t_idx)
        du_ref[...] = _pool_select(g, [r1, r2, r3, r4]) - dy

    col = pl.BlockSpec((T, C), lambda g: (0, g))
    return pl.pallas_call(
        body, name="pool_bwd", grid=(G,),
        in_specs=[col, pl.BlockSpec((1, C, C), lambda g: (g, 0, 0)), pl.BlockSpec((1, C), lambda g: (0, g)), col],
        out_specs=[col, pl.BlockSpec((1, C, C), lambda g: (g, 0, 0)), pl.BlockSpec((1, C), lambda g: (0, g))],
        out_shape=[jax.ShapeDtypeStruct((T, G * C), F32), jax.ShapeDtypeStruct((G, C, C), F32),
                   jax.ShapeDtypeStruct((1, G * C), F32)],
        compiler_params=_cp(("parallel",)),
    )(proj, pool_w, pool_scale, dmix)


SB_SCALE = SB_HEAD_DIM ** -0.5
SB_BLOCKS_PER_PASS = 2


def _sb_tile_logits(qb, kblk, valid):
    z = _dot(qb, kblk, "nt") * SB_SCALE
    sp = _softplus(z)
    l1m = -sp
    if valid is not None:
        l1m = jnp.where(valid, l1m, 0.0)
    return z, sp, l1m


def _sb_fwd(q, k, v):
    H, T, dh = q.shape
    B = _tile(T, 256)
    nq = T // B

    def body(q_ref, k_ref, v_ref, o_ref, l_ref):
        qi = pl.program_id(1)
        qb = q_ref[0]
        row = lax.broadcasted_iota(jnp.int32, (B, B), 0)
        col = lax.broadcasted_iota(jnp.int32, (B, B), 1)
        later = (row > col).astype(BF16)

        def tiles(kbs, carry, acc, valid):
            ksl = [pl.ds(pl.multiple_of(kb * B, B), B) for kb in kbs]
            logits = [_sb_tile_logits(qb, k_ref[0, ks, :], valid) for ks in ksl]
            within = [_dot_x01(l1m, later) for _, _, l1m in logits]
            sums = [jnp.sum(l1m, axis=1, keepdims=True) for _, _, l1m in logits]
            for (z, sp, _), rc, s, ks in zip(logits, within, sums, ksl):
                a = jnp.exp(z - sp + rc + carry)
                if valid is not None:
                    a = jnp.where(valid, a, 0.0)
                acc = acc + _dot(a, v_ref[0, ks, :])
                carry = carry + s
            return carry, acc

        state = tiles([qi], jnp.zeros((B, 1), F32), jnp.zeros((B, dh), F32), col < row)
        n_pass = qi // SB_BLOCKS_PER_PASS
        state = lax.fori_loop(
            0, n_pass, lambda i, c: tiles([qi - 1 - SB_BLOCKS_PER_PASS * i - u for u in range(SB_BLOCKS_PER_PASS)],
                                          c[0], c[1], None), state)
        rest = qi - n_pass * SB_BLOCKS_PER_PASS
        carry, acc = lax.fori_loop(0, rest, lambda i, c: tiles([rest - 1 - i], c[0], c[1], None), state)
        o_ref[0] = acc
        l_ref[0] = carry

    qspec = pl.BlockSpec((1, B, dh), lambda h, i: (h, i, 0))
    full = pl.BlockSpec((1, T, dh), lambda h, i: (h, 0, 0))
    return pl.pallas_call(
        body, name="sb_fwd", grid=(H, nq),
        in_specs=[qspec, full, full],
        out_specs=[qspec, pl.BlockSpec((1, B, 1), lambda h, i: (h, i, 0))],
        out_shape=[jax.ShapeDtypeStruct((H, T, dh), F32), jax.ShapeDtypeStruct((H, T, 1), F32)],
        compiler_params=_cp(("parallel", "parallel")),
    )(q, k, v)


def _sb_bwd(q, k, v, dout, ltot):
    H, T, dh = q.shape
    B = _tile(T, 256)
    nq = T // B

    def body(q_ref, k_ref, v_ref, do_ref, l_ref, dq_ref, dk_ref, dv_ref):
        qi = pl.program_id(1)

        @pl.when(qi == 0)
        def _():
            dk_ref[...] = jnp.zeros_like(dk_ref)
            dv_ref[...] = jnp.zeros_like(dv_ref)

        qb = q_ref[0]
        dob = do_ref[0].astype(BF16)
        ltot_q = l_ref[0]
        row = lax.broadcasted_iota(jnp.int32, (B, B), 0)
        col = lax.broadcasted_iota(jnp.int32, (B, B), 1)
        upto = (row <= col).astype(BF16)
        before = (row < col).astype(BF16)

        def tiles(kbs, P, E, dq, valid):
            ksl = [pl.ds(pl.multiple_of(kb * B, B), B) for kb in kbs]
            kblks = [k_ref[0, ks, :] for ks in ksl]
            logits = [_sb_tile_logits(qb, kblk, valid) for kblk in kblks]
            das = [_dot(dob, v_ref[0, ks, :], "nt") for ks in ksl]
            within = [_dot_x01(l1m, upto) for _, _, l1m in logits]
            avals, es = [], []
            for (z, sp, l1m), pc, da in zip(logits, within, das):
                a = jnp.exp(z - sp + (ltot_q - P - pc))
                if valid is not None:
                    a = jnp.where(valid, a, 0.0)
                avals.append(a)
                es.append(da * a)
                P = P + jnp.sum(l1m, axis=1, keepdims=True)
            e_within = [_dot_x01(e, before) for e in es]
            for (z, sp, _), e, ew, a, kblk, ks in zip(logits, es, e_within, avals, kblks, ksl):
                dz = (e * jnp.exp(-sp) - jnp.exp(z - sp) * (ew + E)) * SB_SCALE
                if valid is not None:
                    dz = jnp.where(valid, dz, 0.0)
                dzb = dz.astype(BF16)
                dq = dq + _dot(dzb, kblk)
                dk_ref[0, ks, :] += _dot(dzb, qb, "tn")
                dv_ref[0, ks, :] += _dot(a, dob, "tn")
                E = E + jnp.sum(e, axis=1, keepdims=True)
            return P, E, dq

        zeros1 = jnp.zeros((B, 1), F32)
        n_pass = qi // SB_BLOCKS_PER_PASS
        state = lax.fori_loop(
            0, n_pass, lambda i, c: tiles([SB_BLOCKS_PER_PASS * i + u for u in range(SB_BLOCKS_PER_PASS)], *c, None),
            (zeros1, zeros1, jnp.zeros((B, dh), F32)))
        state = lax.fori_loop(n_pass * SB_BLOCKS_PER_PASS, qi, lambda kb, c: tiles([kb], *c, None), state)
        _, _, dq = tiles([qi], *state, col < row)
        dq_ref[0] = dq

    qspec = pl.BlockSpec((1, B, dh), lambda h, i: (h, i, 0))
    full = pl.BlockSpec((1, T, dh), lambda h, i: (h, 0, 0))
    shp = jax.ShapeDtypeStruct((H, T, dh), F32)
    return pl.pallas_call(
        body, name="sb_bwd", grid=(H, nq),
        in_specs=[qspec, full, full, qspec, pl.BlockSpec((1, B, 1), lambda h, i: (h, i, 0))],
        out_specs=[qspec, full, full],
        out_shape=[shp, shp, shp],
        compiler_params=_cp(("parallel", "arbitrary")),
    )(q, k, v, dout, ltot)


def _rows(w_ref, K):
    return [w_ref[i:i + 1, :] for i in range(K)]


def _conv(x, ws, t_idx):
    K = len(ws)
    y = ws[K - 1] * x
    for i in range(K - 1):
        y = y + ws[i] * _shift_down(x, K - 1 - i, t_idx)
    return y


def _conv_bwd(x, ws, dy, t_idx):
    K = len(ws)
    dx = ws[K - 1] * dy
    dws = []
    for i in range(K - 1):
        dx = dx + ws[i] * _shift_up(dy, K - 1 - i, t_idx)
        dws.append(jnp.sum(dy * _shift_down(x, K - 1 - i, t_idx), axis=0, keepdims=True))
    dws.append(jnp.sum(dy * x, axis=0, keepdims=True))
    return dx, dws


def _store_rows(ref, rows):
    for i, r in enumerate(rows):
        ref[i:i + 1, :] = r


def _ffn_act_fwd(up, conv_w, name):
    T = up.shape[0]
    F = FFN_DIM
    nb = F // LANE

    def body(g_ref, v_ref, wg_ref, wv_ref, o_ref):
        t_idx = lax.broadcasted_iota(jnp.int32, (T, LANE), 0)
        cg = _conv(g_ref[...], _rows(wg_ref, FFN_CONV), t_idx)
        cv = _conv(v_ref[...], _rows(wv_ref, FFN_CONV), t_idx)
        o_ref[...] = (_silu(cg) * cv).astype(BF16)

    return pl.pallas_call(
        body, name=name, grid=(nb,),
        in_specs=[pl.BlockSpec((T, LANE), lambda j: (0, j)), pl.BlockSpec((T, LANE), lambda j: (0, j + nb)),
                  pl.BlockSpec((FFN_CONV, LANE), lambda j: (0, j)),
                  pl.BlockSpec((FFN_CONV, LANE), lambda j: (0, j + nb))],
        out_specs=pl.BlockSpec((T, LANE), lambda j: (0, j)),
        out_shape=jax.ShapeDtypeStruct((T, F), BF16),
        compiler_params=_cp(("parallel",)),
    )(up, up, conv_w, conv_w)


def _ffn_act_bwd(up, conv_w, dact, name):
    T = up.shape[0]
    F = FFN_DIM
    nb = F // LANE

    def body(g_ref, v_ref, wg_ref, wv_ref, da_ref, dg_ref, dv_ref, dwg_ref, dwv_ref):
        t_idx = lax.broadcasted_iota(jnp.int32, (T, LANE), 0)
        xg, xv, wg, wv = g_ref[...], v_ref[...], _rows(wg_ref, FFN_CONV), _rows(wv_ref, FFN_CONV)
        cg = _conv(xg, wg, t_idx)
        cv = _conv(xv, wv, t_idx)
        da = da_ref[...].astype(F32)
        dxg, dwg = _conv_bwd(xg, wg, da * cv * _dsilu(cg), t_idx)
        dxv, dwv = _conv_bwd(xv, wv, da * _silu(cg), t_idx)
        dg_ref[...] = dxg.astype(BF16)
        dv_ref[...] = dxv.astype(BF16)
        _store_rows(dwg_ref, dwg)
        _store_rows(dwv_ref, dwv)

    col = pl.BlockSpec((T, LANE), lambda j: (0, j))
    wcol = pl.BlockSpec((FFN_CONV, LANE), lambda j: (0, j))
    return pl.pallas_call(
        body, name=name, grid=(nb,),
        in_specs=[col, pl.BlockSpec((T, LANE), lambda j: (0, j + nb)), wcol,
                  pl.BlockSpec((FFN_CONV, LANE), lambda j: (0, j + nb)), col],
        out_specs=[col, col, wcol, wcol],
        out_shape=[jax.ShapeDtypeStruct((T, F), BF16), jax.ShapeDtypeStruct((T, F), BF16),
                   jax.ShapeDtypeStruct((FFN_CONV, F), F32), jax.ShapeDtypeStruct((FFN_CONV, F), F32)],
        compiler_params=_cp(("parallel",)),
    )(up, up, conv_w, conv_w, dact)


N_QK_BLOCKS = 2 * GDN_HEADS


def _gdn_pre_fwd(proj, conv_w):
    T = proj.shape[0]
    nb = 3 * GDN_HEADS

    def body(x_ref, w_ref, o_ref):
        j = pl.program_id(0)
        t_idx = lax.broadcasted_iota(jnp.int32, (T, LANE), 0)
        s = _silu(_conv(x_ref[...], _rows(w_ref, GDN_CONV), t_idx))
        rn = lax.rsqrt(jnp.sum(s * s, axis=-1, keepdims=True) + EPS)
        o_ref[...] = s * jnp.where(j < N_QK_BLOCKS, rn, 1.0)

    return pl.pallas_call(
        body, name="gdn_pre_fwd", grid=(nb,),
        in_specs=[pl.BlockSpec((T, LANE), lambda j: (0, j)), pl.BlockSpec((GDN_CONV, LANE), lambda j: (0, j))],
        out_specs=pl.BlockSpec((T, LANE), lambda j: (0, j)),
        out_shape=jax.ShapeDtypeStruct((T, nb * LANE), F32),
        compiler_params=_cp(("parallel",)),
    )(proj, conv_w)


def _gdn_pre_bwd(proj, conv_w, dout):
    T = proj.shape[0]
    nb = 3 * GDN_HEADS

    def body(x_ref, w_ref, do_ref, dx_ref, dw_ref):
        j = pl.program_id(0)
        t_idx = lax.broadcasted_iota(jnp.int32, (T, LANE), 0)
        x, w = x_ref[...], _rows(w_ref, GDN_CONV)
        c = _conv(x, w, t_idx)
        s = _silu(c)
        rn = lax.rsqrt(jnp.sum(s * s, axis=-1, keepdims=True) + EPS)
        do = do_ref[...]
        y = s * rn
        ds_normed = rn * (do - y * jnp.sum(do * y, axis=-1, keepdims=True))
        ds = jnp.where(j < N_QK_BLOCKS, ds_normed, do)
        dx, dw = _conv_bwd(x, w, ds * _dsilu(c), t_idx)
        dx_ref[...] = dx.astype(BF16)
        _store_rows(dw_ref, dw)

    col = pl.BlockSpec((T, LANE), lambda j: (0, j))
    wcol = pl.BlockSpec((GDN_CONV, LANE), lambda j: (0, j))
    return pl.pallas_call(
        body, name="gdn_pre_bwd", grid=(nb,),
        in_specs=[col, wcol, col], out_specs=[col, wcol],
        out_shape=[jax.ShapeDtypeStruct((T, nb * LANE), BF16), jax.ShapeDtypeStruct((GDN_CONV, nb * LANE), F32)],
        compiler_params=_cp(("parallel",)),
    )(proj, conv_w, dout)


def _gdn_consts():
    C = GDN_CHUNK
    r = lax.broadcasted_iota(jnp.int32, (C, C), 0)
    c = lax.broadcasted_iota(jnp.int32, (C, C), 1)
    return dict(incl=r >= c, strict=r > c, eye=(r == c).astype(F32),
                low=(r >= c).astype(BF16), up=(r <= c).astype(BF16), ones=jnp.ones((C, C), BF16))


def _gdn_prep_chunk(q, k, v, b, a, alog, dtb, cs):
    n, C, dk = q.shape
    beta = _sigmoid(b)
    g = -jnp.exp(alog) * _softplus(a + dtb)
    g_sq = jnp.broadcast_to(g, (n, C, C))
    g_wide = jnp.broadcast_to(g, (n, C, dk))
    gc_i = _m01_left(cs["low"], cs["up"], g_sq)
    gc_j = _m01_left(cs["ones"], cs["ones"], g_sq * cs["up"].astype(F32))
    gc_wide = _m01_left(cs["low"], cs["up"], g_wide)
    gl_wide = _m01_left(cs["ones"], cs["ones"], g_wide)
    decay = jnp.where(cs["incl"], jnp.exp(jnp.where(cs["incl"], gc_i - gc_j, 0.0)), 0.0)
    egc = jnp.exp(gc_wide)
    qs = q * (dk ** -0.5)
    k_beta = k * beta
    a_mat = jnp.where(cs["strict"], _dot1_nt(k_beta, k) * decay, 0.0)
    inv = cs["eye"] - a_mat
    pw = _dot3(a_mat, a_mat)
    n_factors = C.bit_length() - 2
    for f in range(n_factors):
        inv = inv + _dot3(inv, pw)
        if f < n_factors - 1:
            pw = _dot3(pw, pw)
    u = _dot3(inv, v * beta)
    w = _dot3(inv, k_beta * egc)
    qk = _dot1_nt(qs, k) * decay
    q_dec = qs * egc
    k_dec = k * jnp.exp(gl_wide - gc_wide)
    g_last = jnp.exp(gl_wide)[:, 0:8, :]
    return qk, u, w, q_dec, k_dec, g_last


GDN_PREP_CHUNKS = 8


def _gdn_prep_specs(T):
    C, dk = GDN_CHUNK, GDN_HEAD_DIM
    npc = min(GDN_PREP_CHUNKS, T // C)
    tc = npc * C
    H = GDN_HEADS
    in_specs = [pl.BlockSpec((tc, dk), lambda h, i: (i, h)),
                pl.BlockSpec((tc, dk), lambda h, i: (i, H + h)),
                pl.BlockSpec((tc, dk), lambda h, i: (i, 2 * H + h)),
                pl.BlockSpec((1, tc, 1), lambda h, i: (h, i, 0)),
                pl.BlockSpec((1, tc, 1), lambda h, i: (h, i, 0)),
                pl.BlockSpec((1, 1, 1), lambda h, i: (h, 0, 0)),
                pl.BlockSpec((1, 1, 1), lambda h, i: (h, 0, 0))]
    xs_specs = [pl.BlockSpec((1, tc, C), lambda h, i: (h, i, 0)),
                pl.BlockSpec((1, tc, dk), lambda h, i: (h, i, 0)),
                pl.BlockSpec((1, tc, dk), lambda h, i: (h, i, 0)),
                pl.BlockSpec((1, tc, dk), lambda h, i: (h, i, 0)),
                pl.BlockSpec((1, tc, dk), lambda h, i: (h, i, 0)),
                pl.BlockSpec((1, npc * 8, dk), lambda h, i: (h, i, 0))]
    xs_shapes = [jax.ShapeDtypeStruct((H, T, C), F32)] + [jax.ShapeDtypeStruct((H, T, dk), F32)] * 4 + [
        jax.ShapeDtypeStruct((H, 8 * T // C, dk), F32)]
    return npc, tc, in_specs, xs_specs, xs_shapes


def _gdn_prep_fwd(qkv, b, a, alog, dtb):
    T = qkv.shape[0]
    C = GDN_CHUNK
    npc, tc, in_specs, xs_specs, xs_shapes = _gdn_prep_specs(T)

    def body(q_ref, k_ref, v_ref, b_ref, a_ref, al_ref, dt_ref, qk_ref, u_ref, w_ref, qd_ref, kd_ref, gl_ref):
        cs = _gdn_consts()

        def chunks(val):
            return val.reshape(npc, C, val.shape[-1])

        outs = _gdn_prep_chunk(chunks(q_ref[...]), chunks(k_ref[...]), chunks(v_ref[...]), chunks(b_ref[0]),
                               chunks(a_ref[0]), al_ref[0], dt_ref[0], cs)
        for ref, val in zip((qk_ref, u_ref, w_ref, qd_ref, kd_ref), outs[:5]):
            ref[0] = val.reshape(tc, val.shape[-1])
        gl_ref[0] = outs[5].reshape(npc * 8, outs[5].shape[-1])

    return pl.pallas_call(
        body, name="gdn_prep_fwd", grid=(GDN_HEADS, T // tc),
        in_specs=in_specs, out_specs=xs_specs, out_shape=xs_shapes,
        compiler_params=_cp(("parallel", "parallel")),
    )(qkv, qkv, qkv, b, a, alog, dtb)


def _gdn_prep_bwd(qkv, b, a, alog, dtb, dxs):
    T = qkv.shape[0]
    C, dk, H = GDN_CHUNK, GDN_HEAD_DIM, GDN_HEADS
    npc, tc, in_specs, xs_specs, _ = _gdn_prep_specs(T)

    def body(q_ref, k_ref, v_ref, b_ref, a_ref, al_ref, dt_ref, dqk_ref, du_ref, dw_ref, dqd_ref, dkd_ref, dgl_ref,
             dq_ref, dk_ref, dv_ref, db_ref, da_ref, dal_ref, ddt_ref):
        i = pl.program_id(1)
        cs = _gdn_consts()
        r8 = lax.broadcasted_iota(jnp.int32, (8, dk), 0)
        c8 = lax.broadcasted_iota(jnp.int32, (8, dk), 1)
        first = (r8 == 0) & (c8 == 0)

        @pl.when(i == 0)
        def _():
            dal_ref[...] = jnp.zeros_like(dal_ref)
            ddt_ref[...] = jnp.zeros_like(ddt_ref)

        def chunks(val):
            return val.reshape(npc, C, val.shape[-1])

        prim = (chunks(q_ref[...]), chunks(k_ref[...]), chunks(v_ref[...]), chunks(b_ref[0]), chunks(a_ref[0]),
                al_ref[0], dt_ref[0])
        _, vjp = jax.vjp(lambda *p: _gdn_prep_chunk(*p, cs), *prim)
        dgl = jnp.where(first, dgl_ref[0].reshape(npc, 8, dk), 0.0)
        cts = tuple(chunks(r[0]) for r in (dqk_ref, du_ref, dw_ref, dqd_ref, dkd_ref)) + (dgl,)
        dq, dkk, dv, db, da, dal, ddt = vjp(cts)
        for ref, val in zip((dq_ref, dk_ref, dv_ref, db_ref, da_ref), (dq, dkk, dv, db, da)):
            ref[0] = val.reshape(tc, val.shape[-1])
        dal_ref[0] += dal
        ddt_ref[0] += ddt

    big = pl.BlockSpec((1, tc, dk), lambda h, i: (h, i, 0))
    thin = pl.BlockSpec((1, tc, 1), lambda h, i: (h, i, 0))
    one = pl.BlockSpec((1, 1, 1), lambda h, i: (h, 0, 0))
    return pl.pallas_call(
        body, name="gdn_prep_bwd", grid=(H, T // tc),
        in_specs=in_specs + xs_specs,
        out_specs=[big, big, big, thin, thin, one, one],
        out_shape=[jax.ShapeDtypeStruct((H, T, dk), F32)] * 3 + [jax.ShapeDtypeStruct((H, T, 1), F32)] * 2
        + [jax.ShapeDtypeStruct((H, 1, 1), F32)] * 2,
        compiler_params=_cp(("parallel", "arbitrary")),
    )(qkv, qkv, qkv, b, a, alog, dtb, *dxs)


def _gdn_scan_specs(T):
    C, dk, H = GDN_CHUNK, GDN_HEAD_DIM, GDN_HEADS
    return [pl.BlockSpec((H, C, C), lambda n: (0, n, 0))] + [pl.BlockSpec((H, C, dk), lambda n: (0, n, 0))] * 4 + [
        pl.BlockSpec((H, 8, dk), lambda n: (0, n, 0))]


def _gdn_scan_fwd(xs):
    H, T, dk = xs[1].shape
    C = GDN_CHUNK
    n = T // C

    def body(qk_ref, u_ref, w_ref, qd_ref, kd_ref, gl_ref, o_ref, s_ref, state):
        c = pl.program_id(0)

        @pl.when(c == 0)
        def _():
            state[...] = jnp.zeros_like(state)

        S = state[...]
        s_ref[0] = S
        v_new = u_ref[...] - _bdot(w_ref[...], S)
        o_ref[...] = _bdot(qd_ref[...], S) + _bdot(qk_ref[...], v_new)
        state[...] = S * jnp.tile(gl_ref[...], (1, dk // 8, 1)) + _bdot(kd_ref[...], v_new, "tn")

    return pl.pallas_call(
        body, name="gdn_scan_fwd", grid=(n,),
        in_specs=_gdn_scan_specs(T),
        out_specs=[pl.BlockSpec((H, C, dk), lambda n: (0, n, 0)), pl.BlockSpec((1, H, dk, dk), lambda n: (n, 0, 0, 0))],
        out_shape=[jax.ShapeDtypeStruct((H, T, dk), F32), jax.ShapeDtypeStruct((n, H, dk, dk), F32)],
        scratch_shapes=[pltpu.VMEM((H, dk, dk), F32)],
        compiler_params=_cp(("arbitrary",)),
    )(*xs)


def _gdn_scan_bwd(xs, states, do):
    H, T, dk = xs[1].shape
    C = GDN_CHUNK
    n = T // C

    def rev(spec_shape, f):
        return pl.BlockSpec(spec_shape, lambda i: f(n - 1 - i))

    def body(qk_ref, u_ref, w_ref, qd_ref, kd_ref, gl_ref, s_ref, do_ref,
             dqk_ref, du_ref, dw_ref, dqd_ref, dkd_ref, dgl_ref, dstate):
        i = pl.program_id(0)

        @pl.when(i == 0)
        def _():
            dstate[...] = jnp.zeros_like(dstate)

        S = s_ref[0]
        dS = dstate[...]
        do_v = do_ref[...]
        qk, w, qd, kd = qk_ref[...], w_ref[...], qd_ref[...], kd_ref[...]
        v_new = u_ref[...] - _bdot(w, S)
        dv_new = _bdot(qk, do_v, "tn") + _bdot(kd, dS)
        dqk_ref[...] = _bdot(do_v, v_new, "nt")
        dqd_ref[...] = _bdot(do_v, S, "nt")
        dkd_ref[...] = _bdot(v_new, dS, "nt")
        du_ref[...] = dv_new
        dw_ref[...] = -_bdot(dv_new, S, "nt")
        dgl = jnp.sum(jnp.sum(S * dS, axis=2, keepdims=True), axis=1, keepdims=True)
        dgl_ref[...] = jnp.broadcast_to(dgl, dgl_ref.shape)
        dstate[...] = (dS * jnp.tile(gl_ref[...], (1, dk // 8, 1)) + _bdot(qd, do_v, "tn")
                       - _bdot(w, dv_new, "tn"))

    in_specs = [rev((H, C, C), lambda m: (0, m, 0))] + [rev((H, C, dk), lambda m: (0, m, 0))] * 4 + [
        rev((H, 8, dk), lambda m: (0, m, 0)), rev((1, H, dk, dk), lambda m: (m, 0, 0, 0)),
        rev((H, C, dk), lambda m: (0, m, 0))]
    out_specs = [rev((H, C, C), lambda m: (0, m, 0))] + [rev((H, C, dk), lambda m: (0, m, 0))] * 4 + [
        rev((H, 8, dk), lambda m: (0, m, 0))]
    out_shape = [jax.ShapeDtypeStruct((H, T, C), F32)] + [jax.ShapeDtypeStruct((H, T, dk), F32)] * 4 + [
        jax.ShapeDtypeStruct((H, 8 * n, dk), F32)]
    return pl.pallas_call(
        body, name="gdn_scan_bwd", grid=(n,),
        in_specs=in_specs, out_specs=out_specs, out_shape=out_shape,
        scratch_shapes=[pltpu.VMEM((H, dk, dk), F32)],
        compiler_params=_cp(("arbitrary",)),
    )(*xs, states, do)


def _gdn_post_fwd(o, proj, norm_w):
    H, T, dk = o.shape
    tt = _tile(T, 1024)
    zoff = 3 * GDN_HEADS

    def body(o_ref, z_ref, g_ref, y_ref):
        ov = o_ref[0]
        r = lax.rsqrt(jnp.mean(ov * ov, axis=-1, keepdims=True) + EPS)
        y_ref[...] = (ov * r * g_ref[...] * _silu(z_ref[...])).astype(BF16)

    return pl.pallas_call(
        body, name="gdn_post_fwd", grid=(H, T // tt),
        in_specs=[pl.BlockSpec((1, tt, dk), lambda h, i: (h, i, 0)), pl.BlockSpec((tt, dk), lambda h, i: (i, zoff + h)),
                  pl.BlockSpec((1, dk), lambda h, i: (0, 0))],
        out_specs=pl.BlockSpec((tt, dk), lambda h, i: (i, h)),
        out_shape=jax.ShapeDtypeStruct((T, H * dk), BF16),
        compiler_params=_cp(("parallel", "parallel")),
    )(o, proj, norm_w)


def _gdn_post_bwd(o, proj, norm_w, dy):
    H, T, dk = o.shape
    tt = _tile(T, 1024)
    zoff = 3 * GDN_HEADS

    def body(o_ref, z_ref, g_ref, dy_ref, do_ref, dz_ref, dg_ref):
        i = pl.program_id(1)
        ov, z, g, dyv = o_ref[0], z_ref[...], g_ref[...], dy_ref[...]
        r = lax.rsqrt(jnp.mean(ov * ov, axis=-1, keepdims=True) + EPS)
        on = ov * r
        sz = _silu(z)
        dz_ref[...] = (dyv * on * g * _dsilu(z)).astype(BF16)
        dn = dyv * sz
        gdn = dn * g
        do_ref[0] = r * (gdn - on * jnp.mean(gdn * on, axis=-1, keepdims=True))

        @pl.when(i == 0)
        def _():
            dg_ref[...] = jnp.zeros_like(dg_ref)

        dg_ref[0] += jnp.sum(dn * on, axis=0, keepdims=True)

    return pl.pallas_call(
        body, name="gdn_post_bwd", grid=(H, T // tt),
        in_specs=[pl.BlockSpec((1, tt, dk), lambda h, i: (h, i, 0)), pl.BlockSpec((tt, dk), lambda h, i: (i, zoff + h)),
                  pl.BlockSpec((1, dk), lambda h, i: (0, 0)), pl.BlockSpec((tt, dk), lambda h, i: (i, h))],
        out_specs=[pl.BlockSpec((1, tt, dk), lambda h, i: (h, i, 0)), pl.BlockSpec((tt, dk), lambda h, i: (i, h)),
                   pl.BlockSpec((1, 1, dk), lambda h, i: (h, 0, 0))],
        out_shape=[jax.ShapeDtypeStruct((H, T, dk), F32), jax.ShapeDtypeStruct((T, H * dk), BF16),
                   jax.ShapeDtypeStruct((H, 1, dk), F32)],
        compiler_params=_cp(("parallel", "arbitrary")),
    )(o, proj, norm_w, dy)


def _ple_fwd(x, pp, gl, name):
    T, D = x.shape
    tt = _tile(T, 512)

    def body(x_ref, p_ref, g_ref, o_ref):
        o_ref[...] = x_ref[...] + p_ref[...] * _sigmoid(g_ref[...])

    row = pl.BlockSpec((tt, D), lambda i: (i, 0))
    return pl.pallas_call(
        body, name=name, grid=(T // tt,), in_specs=[row, row, row], out_specs=row,
        out_shape=jax.ShapeDtypeStruct((T, D), F32), compiler_params=_cp(("parallel",)),
    )(x, pp, gl)


def _ple_bwd(dx, pp, gl, name):
    T, D = dx.shape
    tt = _tile(T, 512)

    def body(dx_ref, p_ref, g_ref, dp_ref, dg_ref):
        s = _sigmoid(g_ref[...])
        dxv = dx_ref[...]
        dp_ref[...] = (dxv * s).astype(BF16)
        dg_ref[...] = (dxv * p_ref[...] * s * (1.0 - s)).astype(BF16)

    row = pl.BlockSpec((tt, D), lambda i: (i, 0))
    return pl.pallas_call(
        body, name=name, grid=(T // tt,), in_specs=[row, row, row], out_specs=[row, row],
        out_shape=[jax.ShapeDtypeStruct((T, D), BF16)] * 2, compiler_params=_cp(("parallel",)),
    )(dx, pp, gl)


def _heads_in(a2d):
    T = a2d.shape[0]
    a = a2d.reshape(T, 3, SB_HEADS, SB_HEAD_DIM).transpose(1, 2, 0, 3).astype(BF16)
    return a[0], a[1], a[2]


def _heads_out(a):
    H, T, dh = a.shape
    return a.transpose(1, 0, 2).reshape(T, H * dh)


def _ffn_fwd(x, norm, w_up, conv_w, w_down, tag):
    hf = _rms_fwd(x, norm, f"ffn_norm{tag}")
    up = _mm(hf, w_up, "nn", f"ffn_up{tag}")
    act = _ffn_act_fwd(up, conv_w, f"ffn_act{tag}")
    x_out = _mm(act, w_down, "nn", f"ffn_down{tag}", res=x, tn=1024, tk=1408)
    return x_out, (x, hf, up, act)


def _ffn_bwd(dx_out, saved, norm, w_up, conv_w, w_down, tag):
    x, hf, up, act = saved
    dact = _mm(dx_out, w_down, "nt", f"ffn_dact{tag}", out_dtype=BF16, tn=1408)
    dw_down = _mm(act, dx_out, "tn", f"ffn_dwdown{tag}", tm=1408, tn=1024, tk=512)
    dgate, dval, dcw_g, dcw_v = _ffn_act_bwd(up, conv_w, dact, f"ffn_dact_conv{tag}")
    dup = jnp.concatenate([dgate, dval], axis=1)
    dw_up = _mm(hf, dup, "tn", f"ffn_dwup{tag}", tm=1024, tk=512)
    dhf = _mm(dup, w_up, "nt", f"ffn_dhf{tag}", tn=1024, tk=1408)
    dx, dnorm = _rms_bwd(x, norm, dhf, dx_out, f"ffn_dnorm{tag}")
    return dx, dnorm, dw_up, jnp.concatenate([dcw_g, dcw_v], axis=1), dw_down


def _ple_layer_fwd(x, p_i, norm, w_gate, w_ple, tag):
    hg = _rms_fwd(x, norm, f"ple_norm{tag}")
    gl = _mm(hg, w_gate, "nn", f"ple_gate{tag}", tn=1024)
    pp = _mm(p_i, w_ple, "nn", f"ple_proj{tag}", tn=1024)
    return _ple_fwd(x, pp, gl, f"ple_mix{tag}"), (x, hg, gl, pp)


def _ple_layer_bwd(dx_out, saved, p_i, norm, w_gate, tag):
    x, hg, gl, pp = saved
    dpp, dgl = _ple_bwd(dx_out, pp, gl, f"ple_dmix{tag}")
    dw_ple = _mm(p_i, dpp, "tn", f"ple_dwple{tag}", tm=256, tn=1024, tk=512)
    dw_gate = _mm(hg, dgl, "tn", f"ple_dwgate{tag}", tm=1024, tn=1024, tk=512)
    dhg = _mm(dgl, w_gate, "nt", f"ple_dhg{tag}", tn=1024)
    dx, dnorm = _rms_bwd(x, norm, dhg, dx_out, f"ple_dnorm{tag}")
    return dx, dnorm, dw_gate, dw_ple


def _local_step(x, p, target, W):
    T = x.shape[0]
    H = GDN_HEADS
    G = {}

    hn_e = _rms_fwd(x, W["mix_norm_e"], "mix_norm_e")
    proj_e = _mm(hn_e, W["w_in_e"], "nn", "in_e")
    pool_out = _pool_fwd(proj_e, W["pool_w"], W["pool_scale"])
    q, k, v = _heads_in(proj_e[:, POOL_WIDTH:])
    attn, ltot = _sb_fwd(q, k, v)
    mix_e = jnp.concatenate([pool_out, _heads_out(attn)], axis=1).astype(BF16)
    x1 = _mm(mix_e, W["w_out_e"], "nn", "out_e", res=x, tn=1024)
    x2, ffn0 = _ffn_fwd(x1, W["ffn_norm"][0:1], W["w_up"][0], W["ffn_conv"][0], W["w_down"][0], "0")
    x3, ple0 = _ple_layer_fwd(x2, p[0], W["ple_norm"][0:1], W["w_ple_gate"][0], W["w_ple"][0], "0")

    hn_o = _rms_fwd(x3, W["mix_norm_o"], "mix_norm_o")
    proj_o = _mm(hn_o, W["w_in_o"], "nn", "in_o", tn=384)
    qkv = _gdn_pre_fwd(proj_o, W["conv_qkv_o"])
    ba = proj_o[:, 4 * D_MODEL:4 * D_MODEL + 2 * H]
    b_h = ba[:, :H].T.reshape(H, T, 1)
    a_h = ba[:, H:].T.reshape(H, T, 1)
    alog = W["a_log_o"].reshape(H, 1, 1)
    dtb = W["dt_bias_o"].reshape(H, 1, 1)
    xs = _gdn_prep_fwd(qkv, b_h, a_h, alog, dtb)
    o, states = _gdn_scan_fwd(xs)
    og = _gdn_post_fwd(o, proj_o, W["gdn_norm_o"])
    x4 = _mm(og, W["w_out_o"], "nn", "out_o", res=x3, tn=1024)
    x5, ffn1 = _ffn_fwd(x4, W["ffn_norm"][1:2], W["w_up"][1], W["ffn_conv"][1], W["w_down"][1], "1")
    x6, ple1 = _ple_layer_fwd(x5, p[1], W["ple_norm"][1:2], W["w_ple_gate"][1], W["w_ple"][1], "1")

    sq, dx6, G["final_norm"] = _final_loss(x6, W["final_norm"], target, "final_loss")

    dx5, dpn1, dwg1, dwp1 = _ple_layer_bwd(dx6, ple1, p[1], W["ple_norm"][1:2], W["w_ple_gate"][1], "1")
    dx4, dfn1, dwu1, dfc1, dwd1 = _ffn_bwd(dx5, ffn1, W["ffn_norm"][1:2], W["w_up"][1], W["ffn_conv"][1],
                                           W["w_down"][1], "1")
    dog = _mm(dx4, W["w_out_o"], "nt", "d_og", tn=1024)
    G["w_out_o"] = _mm(og, dx4, "tn", "dw_out_o", tm=1024, tn=1024, tk=512)
    do, dz, dgn = _gdn_post_bwd(o, proj_o, W["gdn_norm_o"], dog)
    G["gdn_norm_o"] = jnp.sum(dgn, axis=0)
    dxs = _gdn_scan_bwd(xs, states, do)
    dq, dk, dv, db, da, dal, ddt = _gdn_prep_bwd(qkv, b_h, a_h, alog, dtb, dxs)
    G["a_log_o"] = dal.reshape(1, H)
    G["dt_bias_o"] = ddt.reshape(1, H)
    dqkv_act = jnp.concatenate([_t_heads(dq), _t_heads(dk), _t_heads(dv)], axis=1)
    dqkv, G["conv_qkv_o"] = _gdn_pre_bwd(proj_o, W["conv_qkv_o"], dqkv_act)
    dba = jnp.concatenate([db.reshape(H, T).T, da.reshape(H, T).T,
                           jnp.zeros((T, ODD_IN_PAD - ODD_IN + 0), F32)], axis=1).astype(BF16)
    dproj_o = jnp.concatenate([dqkv, dz, dba], axis=1)
    G["w_in_o"] = _mm(hn_o, dproj_o, "tn", "dw_in_o", tm=1024, tn=384, tk=512)
    dhn_o = _mm(dproj_o, W["w_in_o"], "nt", "d_hn_o", tn=1024, tk=1408)
    dx3, G["mix_norm_o"] = _rms_bwd(x3, W["mix_norm_o"], dhn_o, dx4, "d_mix_norm_o")

    dx2, dpn0, dwg0, dwp0 = _ple_layer_bwd(dx3, ple0, p[0], W["ple_norm"][0:1], W["w_ple_gate"][0], "0")
    dx1, dfn0, dwu0, dfc0, dwd0 = _ffn_bwd(dx2, ffn0, W["ffn_norm"][0:1], W["w_up"][0], W["ffn_conv"][0],
                                           W["w_down"][0], "0")
    dmix = _mm(dx1, W["w_out_e"], "nt", "d_mix_e", tn=1024)
    G["w_out_e"] = _mm(mix_e, dx1, "tn", "dw_out_e", tm=1024, tn=1024, tk=512)
    du, G["pool_w"], G["pool_scale"] = _pool_bwd(proj_e, W["pool_w"], W["pool_scale"], dmix)
    dattn = dmix[:, POOL_WIDTH:].reshape(T, SB_HEADS, SB_HEAD_DIM).transpose(1, 0, 2)
    dqa, dka, dva = _sb_bwd(q, k, v, dattn, ltot)
    dproj_e = jnp.concatenate([du, _heads_out(dqa), _heads_out(dka), _heads_out(dva)], axis=1).astype(BF16)
    G["w_in_e"] = _mm(hn_e, dproj_e, "tn", "dw_in_e", tm=1024, tk=512)
    dhn_e = _mm(dproj_e, W["w_in_e"], "nt", "d_hn_e", tn=1024)
    grad_x, G["mix_norm_e"] = _rms_bwd(x, W["mix_norm_e"], dhn_e, dx1, "d_mix_norm_e")

    G["ffn_norm"] = jnp.concatenate([dfn0, dfn1], axis=0)
    G["ple_norm"] = jnp.concatenate([dpn0, dpn1], axis=0)
    G["w_up"] = jnp.stack([dwu0, dwu1])
    G["ffn_conv"] = jnp.stack([dfc0, dfc1])
    G["w_down"] = jnp.stack([dwd0, dwd1])
    G["w_ple_gate"] = jnp.stack([dwg0, dwg1])
    G["w_ple"] = jnp.stack([dwp0, dwp1])
    return sq[0, 0], grad_x, G


def _t_heads(a):
    H, T, dk = a.shape
    return a.transpose(1, 0, 2).reshape(T, H * dk)


SHARDED_BIG = (("w_in_e", 2), ("w_out_e", 1), ("w_in_o", 2), ("w_out_o", 1), ("w_up", 2), ("w_down", 1),
               ("w_ple_gate", 1), ("w_ple", 2))
SHARDED_SMALL = (("mix_norm_o", 1), ("conv_qkv_o", 2), ("ffn_conv", 2))
SHARDED = SHARDED_BIG + SHARDED_SMALL
REPLICATED = ("mix_norm_e", "pool_w", "pool_scale", "a_log_o", "dt_bias_o", "gdn_norm_o", "ffn_norm", "ple_norm",
              "final_norm")
WEIGHT_ORDER = ("mix_norm_e", "w_in_e", "pool_w", "pool_scale", "w_out_e", "mix_norm_o", "w_in_o", "conv_qkv_o",
                "a_log_o", "dt_bias_o", "gdn_norm_o", "w_out_o", "ffn_norm", "w_up", "ffn_conv", "w_down", "ple_norm",
                "w_ple_gate", "w_ple", "final_norm")
PACK_W = 1024
PACK_ROWS = 32
SMALL_W = LANE
SMALL_ROWS = 16


def _size(shape):
    n = 1
    for s in shape:
        n *= s
    return n


def _pack(arrs, width, granule):
    flat = jnp.concatenate([a.reshape(-1) for a in arrs])
    rows = -(-flat.shape[0] // width)
    rows = -(-rows // granule) * granule
    return jnp.pad(flat, (0, rows * width - flat.shape[0])).reshape(rows, width)


def _unpack(flat2d, shapes):
    flat = flat2d.reshape(-1)
    out, off = [], 0
    for s in shapes:
        out.append(flat[off:off + _size(s)].reshape(s))
        off += _size(s)
    return out


MESH_ID = pl.DeviceIdType.MESH
HBM_SPEC = pl.BlockSpec(memory_space=pltpu.HBM)


def _where_am_i():
    return lax.axis_index("x"), lax.axis_index("y"), lax.axis_index("c")


def _other_chips(x, y):
    return [(1 - x, y), (x, 1 - y), (1 - x, 1 - y)]


def _remote(src, dst, send_sems, recv_sems, k, to):
    return pltpu.make_async_remote_copy(src_ref=src, dst_ref=dst, send_sem=send_sems.at[k], recv_sem=recv_sems.at[k],
                                        device_id=to, device_id_type=MESH_ID)


def _chip_allgather(pack, name):
    R, Wd = pack.shape
    Rh = R // 2

    def body(src_ref, out_ref, send_sems, recv_sems, local_sem):
        x, y, c = _where_am_i()
        me, sib = (x, y, c), (x, y, 1 - c)
        chips = _other_chips(x, y)
        mine_rows = pl.ds(pl.multiple_of(c * Rh, SMALL_ROWS), Rh)
        sib_rows = pl.ds(pl.multiple_of((1 - c) * Rh, SMALL_ROWS), Rh)
        j_me = 2 * x + y
        local = pltpu.make_async_copy(src_ref, out_ref.at[j_me], local_sem)
        local.start()
        first = [_remote(src_ref.at[mine_rows], out_ref.at[j_me, mine_rows], send_sems, recv_sems, k, (cx, cy, c))
                 for k, (cx, cy) in enumerate(chips)]
        for cp in first:
            cp.start()
        passed = []
        for k, (cx, cy) in enumerate(chips):
            blk = out_ref.at[2 * cx + cy, mine_rows]
            _remote(blk, blk, send_sems, recv_sems, k, me).wait_recv()
            fw = _remote(blk, blk, send_sems, recv_sems, 3 + k, sib)
            fw.start()
            passed.append(fw)
        for k, (cx, cy) in enumerate(chips):
            blk = out_ref.at[2 * cx + cy, sib_rows]
            _remote(blk, blk, send_sems, recv_sems, 3 + k, me).wait_recv()
        for cp in first + passed:
            cp.wait_send()
        local.wait()

    return pl.pallas_call(
        body, name=name, in_specs=[HBM_SPEC], out_specs=HBM_SPEC,
        out_shape=jax.ShapeDtypeStruct((N_CHIPS, R, Wd), pack.dtype),
        scratch_shapes=[pltpu.SemaphoreType.DMA((6,)), pltpu.SemaphoreType.DMA((6,)), pltpu.SemaphoreType.DMA],
    )(pack)


def _sibling_swap(src, name):
    n, R, Wd = src.shape
    Rh = R // 2

    def body(src_ref, out_ref, send_sems, recv_sems):
        x, y, c = _where_am_i()
        theirs = pl.ds(pl.multiple_of((1 - c) * Rh, SMALL_ROWS), Rh)
        cp = _remote(src_ref.at[:, theirs], out_ref, send_sems, recv_sems, 0, (x, y, 1 - c))
        cp.start()
        cp.wait()

    return pl.pallas_call(
        body, name=name, in_specs=[HBM_SPEC], out_specs=HBM_SPEC,
        out_shape=jax.ShapeDtypeStruct((n, Rh, Wd), src.dtype),
        scratch_shapes=[pltpu.SemaphoreType.DMA((1,)), pltpu.SemaphoreType.DMA((1,))],
    )(src)


def _chip_scatter(cs, name):
    _, Rh, Wd = cs.shape

    def body(src_ref, out_ref, send_sems, recv_sems):
        x, y, c = _where_am_i()
        cps = [_remote(src_ref.at[2 * cx + cy], out_ref.at[k], send_sems, recv_sems, k, (cx, cy, c))
               for k, (cx, cy) in enumerate(_other_chips(x, y))]
        for cp in cps:
            cp.start()
        for cp in cps:
            cp.wait()

    return pl.pallas_call(
        body, name=name, in_specs=[HBM_SPEC], out_specs=HBM_SPEC,
        out_shape=jax.ShapeDtypeStruct((N_CHIPS - 1, Rh, Wd), cs.dtype),
        scratch_shapes=[pltpu.SemaphoreType.DMA((3,)), pltpu.SemaphoreType.DMA((3,))],
    )(cs)


def _sibling_join(half, name):
    Rh, Wd = half.shape

    def body(src_ref, out_ref, send_sems, recv_sems, local_sem):
        x, y, c = _where_am_i()
        rows = pl.ds(pl.multiple_of(c * Rh, SMALL_ROWS // 2), Rh)
        local = pltpu.make_async_copy(src_ref, out_ref.at[rows], local_sem)
        local.start()
        cp = _remote(src_ref, out_ref.at[rows], send_sems, recv_sems, 0, (x, y, 1 - c))
        cp.start()
        cp.wait()
        local.wait()

    return pl.pallas_call(
        body, name=name, in_specs=[HBM_SPEC], out_specs=HBM_SPEC,
        out_shape=jax.ShapeDtypeStruct((2 * Rh, Wd), half.dtype),
        scratch_shapes=[pltpu.SemaphoreType.DMA((1,)), pltpu.SemaphoreType.DMA((1,)), pltpu.SemaphoreType.DMA],
    )(half)


def _row_tile(rows, pref=512):
    best = 8
    for t in range(8, pref + 1, 8):
        if rows % t == 0:
            best = t
    return best


def _where_ids():
    x, y, c = _where_am_i()
    return jnp.stack([c, 2 * x + y]).astype(jnp.int32)


def _chip_sums_bf16(P, A, ids):
    n, R, Wd = P.shape
    Rh = R // 2
    tr = _row_tile(Rh)
    nb = Rh // tr

    def body(ids_ref, p_ref, a_ref, o_ref):
        o_ref[...] = (p_ref[...] + a_ref[...]).astype(BF16)

    return pl.pallas_call(
        body, name="rs_chip_sums",
        grid_spec=pltpu.PrefetchScalarGridSpec(
            num_scalar_prefetch=1, grid=(n, nb),
            in_specs=[pl.BlockSpec((1, tr, Wd), lambda j, i, ids: (j, ids[0] * nb + i, 0)),
                      pl.BlockSpec((1, tr, Wd), lambda j, i, ids: (j, i, 0))],
            out_specs=pl.BlockSpec((1, tr, Wd), lambda j, i, ids: (j, i, 0))),
        out_shape=jax.ShapeDtypeStruct((n, Rh, Wd), BF16),
        compiler_params=_cp(("parallel", "parallel")),
    )(ids, P, A)


def _total_half(P, A, B, ids):
    n, R, Wd = P.shape
    Rh = R // 2
    tr = _row_tile(Rh)
    nb = Rh // tr

    def body(ids_ref, p_ref, a_ref, b_ref, o_ref):
        s = p_ref[0] + a_ref[0]
        for k in range(N_CHIPS - 1):
            s = s + b_ref[k].astype(F32)
        o_ref[...] = s

    return pl.pallas_call(
        body, name="rs_total",
        grid_spec=pltpu.PrefetchScalarGridSpec(
            num_scalar_prefetch=1, grid=(nb,),
            in_specs=[pl.BlockSpec((1, tr, Wd), lambda i, ids: (ids[1], ids[0] * nb + i, 0)),
                      pl.BlockSpec((1, tr, Wd), lambda i, ids: (ids[1], i, 0)),
                      pl.BlockSpec((N_CHIPS - 1, tr, Wd), lambda i, ids: (0, i, 0))],
            out_specs=pl.BlockSpec((tr, Wd), lambda i, ids: (i, 0))),
        out_shape=jax.ShapeDtypeStruct((Rh, Wd), F32),
        compiler_params=_cp(("parallel",)),
    )(ids, P, A, B)


def _small_allreduce(v, name):
    R, Wd = v.shape

    def body(x_ref, sum_ref, all_ref, send_sems, recv_sems, local_sem):
        x, y, c = _where_am_i()
        me, sib = (x, y, c), (x, y, 1 - c)
        chips = _other_chips(x, y)

        def slot(px, py, pc):
            return all_ref.at[4 * px + 2 * py + pc]

        local = pltpu.make_async_copy(x_ref, slot(*me), local_sem)
        local.start()
        first = [_remote(x_ref, slot(*me), send_sems, recv_sems, 0, sib)]
        first += [_remote(x_ref, slot(*me), send_sems, recv_sems, 1 + k, (cx, cy, c)) for k, (cx, cy) in enumerate(chips)]
        for cp in first:
            cp.start()
        passed = []
        for k, (cx, cy) in enumerate(chips):
            blk = slot(cx, cy, c)
            _remote(blk, blk, send_sems, recv_sems, 1 + k, me).wait_recv()
            fw = _remote(blk, blk, send_sems, recv_sems, 4 + k, sib)
            fw.start()
            passed.append(fw)
        _remote(slot(*sib), slot(*sib), send_sems, recv_sems, 0, me).wait_recv()
        for k, (cx, cy) in enumerate(chips):
            blk = slot(cx, cy, 1 - c)
            _remote(blk, blk, send_sems, recv_sems, 4 + k, me).wait_recv()
        for cp in first + passed:
            cp.wait_send()
        local.wait()
        s = all_ref[0]
        for d in range(1, N_DEV):
            s = s + all_ref[d]
        sum_ref[...] = s

    vm = pl.BlockSpec(memory_space=pltpu.VMEM)
    return pl.pallas_call(
        body, name=name, in_specs=[vm], out_specs=[vm, vm],
        out_shape=[jax.ShapeDtypeStruct((R, Wd), F32), jax.ShapeDtypeStruct((N_DEV, R, Wd), F32)],
        scratch_shapes=[pltpu.SemaphoreType.DMA((7,)), pltpu.SemaphoreType.DMA((7,)), pltpu.SemaphoreType.DMA],
    )(v)[0]


def _adamw(w, g, m, v, name):
    R, Wd = w.shape
    tr = _row_tile(R)
    c1 = 1.0 - ADAM_B1 ** ADAM_STEP
    c2 = 1.0 - ADAM_B2 ** ADAM_STEP

    def body(w_ref, g_ref, m_ref, v_ref, d_ref, nm_ref, nv_ref):
        gv = g_ref[...]
        nm = ADAM_B1 * m_ref[...] + (1.0 - ADAM_B1) * gv
        nv = ADAM_B2 * v_ref[...] + (1.0 - ADAM_B2) * (gv * gv)
        d_ref[...] = -ADAM_LR * ((nm / c1) / (jnp.sqrt(nv / c2) + ADAM_EPS) + ADAM_WD * w_ref[...])
        nm_ref[...] = nm
        nv_ref[...] = nv

    row = pl.BlockSpec((tr, Wd), lambda i: (i, 0))
    shp = jax.ShapeDtypeStruct((R, Wd), F32)
    return pl.pallas_call(
        body, name=name, grid=(R // tr,), in_specs=[row] * 4, out_specs=[row] * 3, out_shape=[shp] * 3,
        compiler_params=_cp(("parallel",)),
    )(w, g, m, v)


def _gather_weights(P):
    big_shapes = [P[n].shape for n, _ in SHARDED_BIG]
    small_shapes = [P[n].shape for n, _ in SHARDED_SMALL]
    big = _chip_allgather(_pack([P[n] for n, _ in SHARDED_BIG], PACK_W, PACK_ROWS).astype(BF16), "ag_weights")
    small = _chip_allgather(_pack([P[n] for n, _ in SHARDED_SMALL], SMALL_W, SMALL_ROWS), "ag_small")
    full = {}
    for gathered, group, shapes in ((big, SHARDED_BIG, big_shapes), (small, SHARDED_SMALL, small_shapes)):
        parts = [_unpack(gathered[j], shapes) for j in range(N_CHIPS)]
        for i, (n, ax) in enumerate(group):
            full[n] = jnp.concatenate([parts[j][i] for j in range(N_CHIPS)], axis=ax)
    W = {n: P[n] for n in REPLICATED}
    W["pool_w"] = P["pool_w"][0]
    W["final_norm"] = P["final_norm"].reshape(1, D_MODEL)
    W["w_in_e"] = full["w_in_e"][0]
    W["w_out_e"] = full["w_out_e"][0]
    W["w_in_o"] = jnp.pad(full["w_in_o"][0], ((0, 0), (0, ODD_IN_PAD - ODD_IN)))
    W["w_out_o"] = full["w_out_o"][0]
    W["mix_norm_o"] = full["mix_norm_o"]
    W["conv_qkv_o"] = full["conv_qkv_o"][0]
    for n in ("w_up", "ffn_conv", "w_down", "w_ple_gate", "w_ple"):
        W[n] = full[n]
    return W


def _reduce_gradients(G, P):
    H = GDN_HEADS
    full = dict(G)
    full["w_in_e"] = G["w_in_e"][None]
    full["w_out_e"] = G["w_out_e"][None]
    full["w_in_o"] = G["w_in_o"][None, :, :ODD_IN]
    full["w_out_o"] = G["w_out_o"][None]
    full["conv_qkv_o"] = G["conv_qkv_o"][None]
    full["pool_w"] = G["pool_w"][None]
    full["final_norm"] = G["final_norm"].reshape(D_MODEL)

    pieces = [[] for _ in range(N_CHIPS)]
    for n, ax in SHARDED:
        for j, part in enumerate(jnp.split(full[n], N_CHIPS, axis=ax)):
            pieces[j].append(part)
    packs = jnp.stack([_pack(pc, PACK_W, PACK_ROWS) for pc in pieces])
    ids = _where_ids()
    from_sibling = _sibling_swap(packs, "rs_sibling_swap")
    chip_sums = _chip_sums_bf16(packs, from_sibling, ids)
    from_chips = _chip_scatter(chip_sums, "rs_chip_scatter")
    half = _total_half(packs, from_sibling, from_chips, ids)
    g_sharded = _sibling_join(half, "rs_sibling_join")

    g_repl = _small_allreduce(_pack([full[n] for n in REPLICATED], SMALL_W, 8), "ar_small")
    return g_sharded, g_repl


def kernel(x, p, mix_norm_e, w_in_e, pool_w, pool_scale, w_out_e, mix_norm_o, w_in_o, conv_qkv_o, a_log_o, dt_bias_o, gdn_norm_o, w_out_o, ffn_norm, w_up, ffn_conv, w_down, ple_norm, w_ple_gate, w_ple, final_norm, loss_target, m_mix_norm_e, m_w_in_e, m_pool_w, m_pool_scale, m_w_out_e, m_mix_norm_o, m_w_in_o, m_conv_qkv_o, m_a_log_o, m_dt_bias_o, m_gdn_norm_o, m_w_out_o, m_ffn_norm, m_w_up, m_ffn_conv, m_w_down, m_ple_norm, m_w_ple_gate, m_w_ple, m_final_norm, v_mix_norm_e, v_w_in_e, v_pool_w, v_pool_scale, v_w_out_e, v_mix_norm_o, v_w_in_o, v_conv_qkv_o, v_a_log_o, v_dt_bias_o, v_gdn_norm_o, v_w_out_o, v_ffn_norm, v_w_up, v_ffn_conv, v_w_down, v_ple_norm, v_w_ple_gate, v_w_ple, v_final_norm):
    args = locals()
    P = {n: args[n] for n in WEIGHT_ORDER}
    M = {n: args["m_" + n] for n in WEIGHT_ORDER}
    V = {n: args["v_" + n] for n in WEIGHT_ORDER}

    W = _gather_weights(P)
    sq, grad_x, G = _local_step(x[0], p[:, 0], loss_target[0], W)
    g_sharded, g_repl = _reduce_gradients(G, P)

    def pack_sharded(D):
        return _pack([D[n] for n, _ in SHARDED], PACK_W, PACK_ROWS)

    def pack_repl(D):
        return _pack([D[n] for n in REPLICATED], SMALL_W, 8)

    upd_s = _adamw(pack_sharded(P), g_sharded, pack_sharded(M), pack_sharded(V), "adamw_sharded")
    upd_r = _adamw(pack_repl(P), g_repl, pack_repl(M), pack_repl(V), "adamw_replicated")

    s_shapes = [P[n].shape for n, _ in SHARDED]
    r_shapes = [P[n].shape for n in REPLICATED]
    results = []
    for flat_s, flat_r in zip((g_sharded,) + tuple(upd_s), (g_repl,) + tuple(upd_r)):
        by_name = dict(zip([n for n, _ in SHARDED], _unpack(flat_s, s_shapes)))
        by_name.update(zip(REPLICATED, _unpack(flat_r, r_shapes)))
        results.append([by_name[n] for n in WEIGHT_ORDER])

    loss = (0.5 / D_MODEL) * lax.psum(sq, ("x", "y", "c"))
    return (loss, grad_x[None], *results[0], *results[1], *results[2], *results[3])
```

```python
import functools

import jax
import jax.numpy as jnp
from jax import lax
from jax.experimental import pallas as pl
from jax.experimental.pallas import tpu as pltpu

F32 = jnp.float32
BF16 = jnp.bfloat16

D_MODEL = 1024
PLE_DIM = 256
POOL_WIDTH = 512
POOL_WINDOWS = (2, 4, 8, 16)
POOL_GROUP_DIM = 128
SB_HEADS = 8
SB_HEAD_DIM = 64
GDN_HEADS = 8
GDN_HEAD_DIM = 128
GDN_CONV = 4
GDN_CHUNK = 64
FFN_DIM = 2816
FFN_CONV = 3
EPS = 1e-6
ODD_IN = 4 * D_MODEL + 2 * GDN_HEADS
ODD_IN_PAD = 33 * 128
ADAM_LR, ADAM_B1, ADAM_B2, ADAM_EPS, ADAM_WD, ADAM_STEP = 0.001, 0.9, 0.999, 1e-08, 0.01, 10

LANE = 128
VMEM_LIMIT = 56 * 1024 * 1024

N_CHIPS = 4
N_DEV = 8


def _cp(sem=None):
    return pltpu.CompilerParams(dimension_semantics=sem, vmem_limit_bytes=VMEM_LIMIT)


def _tile(n, pref):
    if n <= pref:
        return n
    best = None
    for t in range(LANE, pref + 1, LANE):
        if n % t == 0:
            best = t
    assert best is not None, (n, pref)
    return best


_DIMS = {"nn": (((1,), (0,)), ((), ())), "nt": (((1,), (1,)), ((), ())), "tn": (((0,), (0,)), ((), ()))}
_BDIMS = {"nn": (((2,), (1,)), ((0,), (0,))), "nt": (((2,), (2,)), ((0,), (0,))), "tn": (((1,), (1,)), ((0,), (0,)))}


def _dims(mode, ndim):
    return (_BDIMS if ndim == 3 else _DIMS)[mode]


def _dot(a, b, mode="nn"):
    return lax.dot_general(a.astype(BF16), b.astype(BF16), _dims(mode, a.ndim), preferred_element_type=F32)


def _bdot(a, b, mode="nn"):
    return lax.dot_general(a.astype(BF16), b.astype(BF16), _BDIMS[mode], preferred_element_type=F32)


def _split2(x):
    hi = x.astype(BF16)
    lo = (x - hi.astype(F32)).astype(BF16)
    return hi, lo


def _split3(x):
    hi = x.astype(BF16)
    r = x - hi.astype(F32)
    mid = r.astype(BF16)
    lo = (r - mid.astype(F32)).astype(BF16)
    return hi, mid, lo


def _dot_x01(x, m01, mode="nn"):
    hi, lo = _split2(x)
    return (lax.dot_general(hi, m01, _DIMS[mode], preferred_element_type=F32)
            + lax.dot_general(lo, m01, _DIMS[mode], preferred_element_type=F32))


def _dot3_raw(a, b, mode):
    ah, al = _split2(a)
    bh, bl = _split2(b)
    d = _dims(mode, a.ndim)
    return (lax.dot_general(ah, bh, d, preferred_element_type=F32)
            + lax.dot_general(ah, bl, d, preferred_element_type=F32)
            + lax.dot_general(al, bh, d, preferred_element_type=F32))


@jax.custom_vjp
def _dot3(a, b):
    return _dot3_raw(a, b, "nn")


def _dot3_fwd(a, b):
    return _dot3_raw(a, b, "nn"), (a, b)


def _dot3_bwd(res, g):
    a, b = res
    return _dot3_raw(g, b, "nt"), _dot3_raw(a, g, "tn")


_dot3.defvjp(_dot3_fwd, _dot3_bwd)


@jax.custom_vjp
def _dot1_nt(a, b):
    return _dot(a, b, "nt")


def _dot1_nt_fwd(a, b):
    return _dot(a, b, "nt"), (a, b)


def _dot1_nt_bwd(res, g):
    a, b = res
    return _dot(g, b, "nn"), _dot(g, a, "tn")


_dot1_nt.defvjp(_dot1_nt_fwd, _dot1_nt_bwd)


def _m01_left_raw(m, x):
    d = _dims("nn", x.ndim)
    if x.ndim == 3:
        m = jnp.broadcast_to(m, (x.shape[0],) + m.shape)
    p0, p1, p2 = _split3(x)
    return (lax.dot_general(m, p0, d, preferred_element_type=F32)
            + lax.dot_general(m, p1, d, preferred_element_type=F32)
            + lax.dot_general(m, p2, d, preferred_element_type=F32))


@jax.custom_vjp
def _m01_left(m, mt, x):
    return _m01_left_raw(m, x)


def _m01_left_fwd(m, mt, x):
    return _m01_left_raw(m, x), (m, mt)


def _m01_left_bwd(res, g):
    m, mt = res
    return jnp.zeros_like(m), jnp.zeros_like(mt), _m01_left_raw(mt, g)


_m01_left.defvjp(_m01_left_fwd, _m01_left_bwd)


def _softplus(x):
    return jnp.maximum(x, 0.0) + jnp.log(1.0 + jnp.exp(-jnp.abs(x)))


def _sigmoid(x):
    return 1.0 / (1.0 + jnp.exp(-x))


def _silu(x):
    return x * _sigmoid(x)


def _dsilu(x):
    s = _sigmoid(x)
    return s * (1.0 + x * (1.0 - s))


def _cols_of(n_blocks_per_part, *fixed):
    return lambda r, c: (c // n_blocks_per_part,) + fixed + (r, c % n_blocks_per_part)


def _rows_of(n_blocks_per_part, *fixed):
    return lambda r, c: (r // n_blocks_per_part,) + fixed + (r % n_blocks_per_part, c)


def _layer_of(layer):
    return lambda r, c: (layer, r, c)


def _mm(a, b, mode, name, out_dtype=F32, res=None, tm=512, tn=512, tk=1024,
        dims=None, a_view=None, b_view=None, o_view=None, out_shape=None):
    if dims is None:
        if mode == "nn":
            (M, K), (K2, N) = a.shape, b.shape
        elif mode == "nt":
            (M, K), (N, K2) = a.shape, b.shape
        else:
            (K, M), (K2, N) = a.shape, b.shape
        assert K == K2, (name, a.shape, b.shape)
    else:
        M, N, K = dims
    tm, tn, tk = _tile(M, tm), _tile(N, tn), _tile(K, tk)
    nk = K // tk

    def spec(arr, blk, view, rc):
        view = view or (lambda r, c: (r, c))
        return pl.BlockSpec((None,) * (arr.ndim - 2) + blk, lambda i, j, k: view(*rc(i, j, k)))

    if mode == "tn":
        a_spec = spec(a, (tk, tm), a_view, lambda i, j, k: (k, i))
    else:
        a_spec = spec(a, (tm, tk), a_view, lambda i, j, k: (i, k))
    if mode == "nt":
        b_spec = spec(b, (tn, tk), b_view, lambda i, j, k: (j, k))
    else:
        b_spec = spec(b, (tk, tn), b_view, lambda i, j, k: (k, j))
    out_shape = out_shape or (M, N)
    o_spec = pl.BlockSpec((None,) * (len(out_shape) - 2) + (tm, tn),
                          lambda i, j, k: (o_view or (lambda r, c: (r, c)))(i, j))
    has_res = res is not None
    assert not (has_res and o_view), name

    def body(*refs):
        if has_res:
            a_ref, b_ref, r_ref, o_ref, acc = refs
        else:
            a_ref, b_ref, o_ref, acc = refs
        k = pl.program_id(2)

        @pl.when(k == 0)
        def _():
            acc[...] = jnp.zeros_like(acc)

        acc[...] += _dot(a_ref[...], b_ref[...], mode)

        @pl.when(k == nk - 1)
        def _():
            r = acc[...]
            if has_res:
                r = r + r_ref[...]
            o_ref[...] = r.astype(out_dtype)

    ins = [a, b] + ([res] if has_res else [])
    in_specs = [a_spec, b_spec] + ([o_spec] if has_res else [])
    return pl.pallas_call(
        body, name=name, grid=(M // tm, N // tn, nk),
        in_specs=in_specs, out_specs=o_spec,
        out_shape=jax.ShapeDtypeStruct(out_shape, out_dtype),
        scratch_shapes=[pltpu.VMEM((tm, tn), F32)],
        compiler_params=_cp(("parallel", "parallel", "arbitrary")),
    )(*ins)


def _rms_fwd(x, gain, name):
    T, D = x.shape
    tt = _tile(T, 512)

    def body(x_ref, g_ref, o_ref):
        xv = x_ref[...]
        r = lax.rsqrt(jnp.mean(xv * xv, axis=-1, keepdims=True) + EPS)
        o_ref[...] = (xv * r * g_ref[...]).astype(BF16)

    return pl.pallas_call(
        body, name=name, grid=(T // tt,),
        in_specs=[pl.BlockSpec((tt, D), lambda i: (i, 0)), pl.BlockSpec((1, D), lambda i: (0, 0))],
        out_specs=pl.BlockSpec((tt, D), lambda i: (i, 0)),
        out_shape=jax.ShapeDtypeStruct((T, D), BF16),
        compiler_params=_cp(("parallel",)),
    )(x, gain)


def _rms_bwd(x, gain, dh, dres, name):
    T, D = x.shape
    tt = _tile(T, 512)

    def body(x_ref, g_ref, dh_ref, dr_ref, dx_ref, dg_ref):
        i = pl.program_id(0)
        xv = x_ref[...]
        dy = dh_ref[...].astype(F32)
        r = lax.rsqrt(jnp.mean(xv * xv, axis=-1, keepdims=True) + EPS)
        xn = xv * r
        gdy = dy * g_ref[...]
        dx = r * (gdy - xn * jnp.mean(gdy * xn, axis=-1, keepdims=True))
        dx_ref[...] = dr_ref[...] + dx

        @pl.when(i == 0)
        def _():
            dg_ref[...] = jnp.zeros_like(dg_ref)

        dg_ref[...] += jnp.sum(dy * xn, axis=0, keepdims=True)

    row = pl.BlockSpec((tt, D), lambda i: (i, 0))
    vec = pl.BlockSpec((1, D), lambda i: (0, 0))
    return pl.pallas_call(
        body, name=name, grid=(T // tt,),
        in_specs=[row, vec, row, row], out_specs=[row, vec],
        out_shape=[jax.ShapeDtypeStruct((T, D), F32), jax.ShapeDtypeStruct((1, D), F32)],
        compiler_params=_cp(("arbitrary",)),
    )(x, gain, dh, dres)


def _final_loss(x, gain, target, name):
    T, D = x.shape
    tt = _tile(T, 512)

    def body(x_ref, g_ref, t_ref, l_ref, dx_ref, dg_ref):
        i = pl.program_id(0)
        xv = x_ref[...]
        r = lax.rsqrt(jnp.mean(xv * xv, axis=-1, keepdims=True) + EPS)
        xn = xv * r
        err = xn * g_ref[...] - t_ref[...]
        dy = err * (1.0 / D)
        gdy = dy * g_ref[...]
        dx_ref[...] = r * (gdy - xn * jnp.mean(gdy * xn, axis=-1, keepdims=True))

        @pl.when(i == 0)
        def _():
            dg_ref[...] = jnp.zeros_like(dg_ref)
            l_ref[...] = jnp.zeros_like(l_ref)

        dg_ref[...] += jnp.sum(dy * xn, axis=0, keepdims=True)
        l_ref[...] += jnp.sum(jnp.sum(err * err, axis=1, keepdims=True), axis=0, keepdims=True)

    row = pl.BlockSpec((tt, D), lambda i: (i, 0))
    vec = pl.BlockSpec((1, D), lambda i: (0, 0))
    return pl.pallas_call(
        body, name=name, grid=(T // tt,),
        in_specs=[row, vec, row],
        out_specs=[pl.BlockSpec((8, LANE), lambda i: (0, 0)), row, vec],
        out_shape=[jax.ShapeDtypeStruct((8, LANE), F32), jax.ShapeDtypeStruct((T, D), F32),
                   jax.ShapeDtypeStruct((1, D), F32)],
        compiler_params=_cp(("arbitrary",)),
    )(x, gain, target)


def _shift_down(x, i, t_idx):
    if i == 0:
        return x
    return jnp.where(t_idx >= i, pltpu.roll(x, i, 0), 0.0)


def _shift_up(x, i, t_idx):
    if i == 0:
        return x
    n = x.shape[0]
    return jnp.where(t_idx < n - i, pltpu.roll(x, n - i, 0), 0.0)


def _pool_select(g, vals):
    out = vals[-1]
    for gi in range(len(vals) - 2, -1, -1):
        out = jnp.where(g == gi, vals[gi], out)
    return out


def _pool_y(u, g, t_idx):
    s1 = u + _shift_down(u, 1, t_idx)
    s2 = s1 + _shift_down(s1, 2, t_idx)
    s3 = s2 + _shift_down(s2, 4, t_idx)
    s4 = s3 + _shift_down(s3, 8, t_idx)
    ws = _pool_select(g, [s1, s2, s3, s4])
    win = _pool_select(g, [jnp.float32(w) for w in POOL_WINDOWS])
    cnt = jnp.minimum(t_idx.astype(F32) + 1.0, win)
    return ws / cnt - u, cnt


def _pool_fwd(proj, pool_w, pool_scale):
    T = proj.shape[0]
    G, C = len(POOL_WINDOWS), POOL_GROUP_DIM

    def body(u_ref, w_ref, s_ref, o_ref):
        g = pl.program_id(0)
        t_idx = lax.broadcasted_iota(jnp.int32, (T, C), 0)
        y, _ = _pool_y(u_ref[...], g, t_idx)
        o_ref[...] = _dot(y, w_ref[0]) * s_ref[...]

    return pl.pallas_call(
        body, name="pool_fwd", grid=(G,),
        in_specs=[pl.BlockSpec((T, C), lambda g: (0, g)), pl.BlockSpec((1, C, C), lambda g: (g, 0, 0)),
                  pl.BlockSpec((1, C), lambda g: (0, g))],
        out_specs=pl.BlockSpec((T, C), lambda g: (0, g)),
        out_shape=jax.ShapeDtypeStruct((T, G * C), F32),
        compiler_params=_cp(("parallel",)),
    )(proj, pool_w, pool_scale)


def _pool_bwd(proj, pool_w, pool_scale, dmix):
    T = proj.shape[0]
    G, C = len(POOL_WINDOWS), POOL_GROUP_DIM

    def body(u_ref, w_ref, s_ref, do_ref, du_ref, dw_ref, ds_ref):
        g = pl.program_id(0)
        t_idx = lax.broadcasted_iota(jnp.int32, (T, C), 0)
        y, cnt = _pool_y(u_ref[...], g, t_idx)
        w = w_ref[0]
        dout = do_ref[...]
        ds_ref[...] = jnp.sum(dout * _dot(y, w), axis=0, keepdims=True)
        dy2 = dout * s_ref[...]
        dw_ref[0] = _dot(y, dy2, "tn")
        dy = _dot(dy2, w, "nt")
        dz = dy / cnt
        r1 = dz + _shift_up(dz, 1, t_idx)
        r2 = r1 + _shift_up(r1, 2, t_idx)
        r3 = r2 + _shift_up(r2, 4, t_idx)
        r4 = r3 + _shift_up(r3, 8, t_idx)
        du_ref[...] = _pool_select(g, [r1, r2, r3, r4]) - dy

    col = pl.BlockSpec((T, C), lambda g: (0, g))
    return pl.pallas_call(
        body, name="pool_bwd", grid=(G,),
        in_specs=[col, pl.BlockSpec((1, C, C), lambda g: (g, 0, 0)), pl.BlockSpec((1, C), lambda g: (0, g)), col],
        out_specs=[col, pl.BlockSpec((1, C, C), lambda g: (g, 0, 0)), pl.BlockSpec((1, C), lambda g: (0, g))],
        out_shape=[jax.ShapeDtypeStruct((T, G * C), F32), jax.ShapeDtypeStruct((G, C, C), F32),
                   jax.ShapeDtypeStruct((1, G * C), F32)],
        compiler_params=_cp(("parallel",)),
    )(proj, pool_w, pool_scale, dmix)


SB_SCALE = SB_HEAD_DIM ** -0.5
SB_BLOCKS_PER_PASS = 2


def _sb_tile_logits(qb, kblk, valid):
    z = _dot(qb, kblk, "nt") * SB_SCALE
    sp = _softplus(z)
    l1m = -sp
    if valid is not None:
        l1m = jnp.where(valid, l1m, 0.0)
    return z, sp, l1m


def _sb_fwd(q, k, v):
    H, T, dh = q.shape
    B = _tile(T, 256)
    nq = T // B

    def body(q_ref, k_ref, v_ref, o_ref, l_ref):
        qi = pl.program_id(1)
        qb = q_ref[0]
        row = lax.broadcasted_iota(jnp.int32, (B, B), 0)
        col = lax.broadcasted_iota(jnp.int32, (B, B), 1)
        later = (row > col).astype(BF16)

        def tiles(kbs, carry, acc, valid):
            ksl = [pl.ds(pl.multiple_of(kb * B, B), B) for kb in kbs]
            logits = [_sb_tile_logits(qb, k_ref[0, ks, :], valid) for ks in ksl]
            within = [_dot_x01(l1m, later) for _, _, l1m in logits]
            sums = [jnp.sum(l1m, axis=1, keepdims=True) for _, _, l1m in logits]
            for (z, sp, _), rc, s, ks in zip(logits, within, sums, ksl):
                a = jnp.exp(z - sp + rc + carry)
                if valid is not None:
                    a = jnp.where(valid, a, 0.0)
                acc = acc + _dot(a, v_ref[0, ks, :])
                carry = carry + s
            return carry, acc

        state = tiles([qi], jnp.zeros((B, 1), F32), jnp.zeros((B, dh), F32), col < row)
        n_pass = qi // SB_BLOCKS_PER_PASS
        state = lax.fori_loop(
            0, n_pass, lambda i, c: tiles([qi - 1 - SB_BLOCKS_PER_PASS * i - u for u in range(SB_BLOCKS_PER_PASS)],
                                          c[0], c[1], None), state)
        rest = qi - n_pass * SB_BLOCKS_PER_PASS
        carry, acc = lax.fori_loop(0, rest, lambda i, c: tiles([rest - 1 - i], c[0], c[1], None), state)
        o_ref[0] = acc
        l_ref[0] = carry

    qspec = pl.BlockSpec((1, B, dh), lambda h, i: (h, i, 0))
    full = pl.BlockSpec((1, T, dh), lambda h, i: (h, 0, 0))
    return pl.pallas_call(
        body, name="sb_fwd", grid=(H, nq),
        in_specs=[qspec, full, full],
        out_specs=[qspec, pl.BlockSpec((1, B, 1), lambda h, i: (h, i, 0))],
        out_shape=[jax.ShapeDtypeStruct((H, T, dh), F32), jax.ShapeDtypeStruct((H, T, 1), F32)],
        compiler_params=_cp(("parallel", "parallel")),
    )(q, k, v)


def _sb_bwd(q, k, v, dout, ltot):
    H, T, dh = q.shape
    B = _tile(T, 256)
    nq = T // B

    def body(q_ref, k_ref, v_ref, do_ref, l_ref, dq_ref, dk_ref, dv_ref):
        qi = pl.program_id(1)

        @pl.when(qi == 0)
        def _():
            dk_ref[...] = jnp.zeros_like(dk_ref)
            dv_ref[...] = jnp.zeros_like(dv_ref)

        qb = q_ref[0]
        dob = do_ref[0].astype(BF16)
        ltot_q = l_ref[0]
        row = lax.broadcasted_iota(jnp.int32, (B, B), 0)
        col = lax.broadcasted_iota(jnp.int32, (B, B), 1)
        upto = (row <= col).astype(BF16)
        before = (row < col).astype(BF16)

        def tiles(kbs, P, E, dq, valid):
            ksl = [pl.ds(pl.multiple_of(kb * B, B), B) for kb in kbs]
            kblks = [k_ref[0, ks, :] for ks in ksl]
            logits = [_sb_tile_logits(qb, kblk, valid) for kblk in kblks]
            das = [_dot(dob, v_ref[0, ks, :], "nt") for ks in ksl]
            within = [_dot_x01(l1m, upto) for _, _, l1m in logits]
            avals, es = [], []
            for (z, sp, l1m), pc, da in zip(logits, within, das):
                a = jnp.exp(z - sp + (ltot_q - P - pc))
                if valid is not None:
                    a = jnp.where(valid, a, 0.0)
                avals.append(a)
                es.append(da * a)
                P = P + jnp.sum(l1m, axis=1, keepdims=True)
            e_within = [_dot_x01(e, before) for e in es]
            for (z, sp, _), e, ew, a, kblk, ks in zip(logits, es, e_within, avals, kblks, ksl):
                dz = (e * jnp.exp(-sp) - jnp.exp(z - sp) * (ew + E)) * SB_SCALE
                if valid is not None:
                    dz = jnp.where(valid, dz, 0.0)
                dzb = dz.astype(BF16)
                dq = dq + _dot(dzb, kblk)
                dk_ref[0, ks, :] += _dot(dzb, qb, "tn")
                dv_ref[0, ks, :] += _dot(a, dob, "tn")
                E = E + jnp.sum(e, axis=1, keepdims=True)
            return P, E, dq

        zeros1 = jnp.zeros((B, 1), F32)
        n_pass = qi // SB_BLOCKS_PER_PASS
        state = lax.fori_loop(
            0, n_pass, lambda i, c: tiles([SB_BLOCKS_PER_PASS * i + u for u in range(SB_BLOCKS_PER_PASS)], *c, None),
            (zeros1, zeros1, jnp.zeros((B, dh), F32)))
        state = lax.fori_loop(n_pass * SB_BLOCKS_PER_PASS, qi, lambda kb, c: tiles([kb], *c, None), state)
        _, _, dq = tiles([qi], *state, col < row)
        dq_ref[0] = dq

    qspec = pl.BlockSpec((1, B, dh), lambda h, i: (h, i, 0))
    full = pl.BlockSpec((1, T, dh), lambda h, i: (h, 0, 0))
    shp = jax.ShapeDtypeStruct((H, T, dh), F32)
    return pl.pallas_call(
        body, name="sb_bwd", grid=(H, nq),
        in_specs=[qspec, full, full, qspec, pl.BlockSpec((1, B, 1), lambda h, i: (h, i, 0))],
        out_specs=[qspec, full, full],
        out_shape=[shp, shp, shp],
        compiler_params=_cp(("parallel", "arbitrary")),
    )(q, k, v, dout, ltot)


def _rows(w_ref, K):
    return [w_ref[i:i + 1, :] for i in range(K)]


def _conv(x, ws, t_idx):
    K = len(ws)
    y = ws[K - 1] * x
    for i in range(K - 1):
        y = y + ws[i] * _shift_down(x, K - 1 - i, t_idx)
    return y


def _conv_bwd(x, ws, dy, t_idx):
    K = len(ws)
    dx = ws[K - 1] * dy
    dws = []
    for i in range(K - 1):
        dx = dx + ws[i] * _shift_up(dy, K - 1 - i, t_idx)
        dws.append(jnp.sum(dy * _shift_down(x, K - 1 - i, t_idx), axis=0, keepdims=True))
    dws.append(jnp.sum(dy * x, axis=0, keepdims=True))
    return dx, dws


def _store_rows(ref, rows):
    for i, r in enumerate(rows):
        ref[i:i + 1, :] = r


def _ffn_act_fwd(up, conv_w, name):
    T = up.shape[0]
    F = FFN_DIM
    nb = F // LANE

    def body(g_ref, v_ref, wg_ref, wv_ref, o_ref):
        t_idx = lax.broadcasted_iota(jnp.int32, (T, LANE), 0)
        cg = _conv(g_ref[...], _rows(wg_ref, FFN_CONV), t_idx)
        cv = _conv(v_ref[...], _rows(wv_ref, FFN_CONV), t_idx)
        o_ref[...] = (_silu(cg) * cv).astype(BF16)

    return pl.pallas_call(
        body, name=name, grid=(nb,),
        in_specs=[pl.BlockSpec((T, LANE), lambda j: (0, j)), pl.BlockSpec((T, LANE), lambda j: (0, j + nb)),
                  pl.BlockSpec((FFN_CONV, LANE), lambda j: (0, j)),
                  pl.BlockSpec((FFN_CONV, LANE), lambda j: (0, j + nb))],
        out_specs=pl.BlockSpec((T, LANE), lambda j: (0, j)),
        out_shape=jax.ShapeDtypeStruct((T, F), BF16),
        compiler_params=_cp(("parallel",)),
    )(up, up, conv_w, conv_w)


def _ffn_act_bwd(up, conv_w, dact, name):
    T = up.shape[0]
    F = FFN_DIM
    nb = F // LANE

    def body(g_ref, v_ref, wg_ref, wv_ref, da_ref, dup_ref, dwg_ref, dwv_ref):
        t_idx = lax.broadcasted_iota(jnp.int32, (T, LANE), 0)
        xg, xv, wg, wv = g_ref[...], v_ref[...], _rows(wg_ref, FFN_CONV), _rows(wv_ref, FFN_CONV)
        cg = _conv(xg, wg, t_idx)
        cv = _conv(xv, wv, t_idx)
        da = da_ref[...].astype(F32)
        dxg, dwg = _conv_bwd(xg, wg, da * cv * _dsilu(cg), t_idx)
        dxv, dwv = _conv_bwd(xv, wv, da * _silu(cg), t_idx)
        dup_ref[0] = dxg.astype(BF16)
        dup_ref[1] = dxv.astype(BF16)
        _store_rows(dwg_ref, dwg)
        _store_rows(dwv_ref, dwv)

    col = pl.BlockSpec((T, LANE), lambda j: (0, j))
    wcol = pl.BlockSpec((FFN_CONV, LANE), lambda j: (0, j))
    return pl.pallas_call(
        body, name=name, grid=(nb,),
        in_specs=[col, pl.BlockSpec((T, LANE), lambda j: (0, j + nb)), wcol,
                  pl.BlockSpec((FFN_CONV, LANE), lambda j: (0, j + nb)), col],
        out_specs=[pl.BlockSpec((2, T, LANE), lambda j: (0, 0, j)), wcol, wcol],
        out_shape=[jax.ShapeDtypeStruct((2, T, F), BF16),
                   jax.ShapeDtypeStruct((FFN_CONV, F), F32), jax.ShapeDtypeStruct((FFN_CONV, F), F32)],
        compiler_params=_cp(("parallel",)),
    )(up, up, conv_w, conv_w, dact)


N_QK_BLOCKS = 2 * GDN_HEADS


def _gdn_pre_fwd(proj, conv_w):
    T = proj.shape[0]
    nb = 3 * GDN_HEADS

    def body(x_ref, w_ref, o_ref):
        j = pl.program_id(0)
        t_idx = lax.broadcasted_iota(jnp.int32, (T, LANE), 0)
        s = _silu(_conv(x_ref[...], _rows(w_ref, GDN_CONV), t_idx))
        rn = lax.rsqrt(jnp.sum(s * s, axis=-1, keepdims=True) + EPS)
        o_ref[...] = s * jnp.where(j < N_QK_BLOCKS, rn, 1.0)

    return pl.pallas_call(
        body, name="gdn_pre_fwd", grid=(nb,),
        in_specs=[pl.BlockSpec((T, LANE), lambda j: (0, j)), pl.BlockSpec((GDN_CONV, LANE), lambda j: (0, j))],
        out_specs=pl.BlockSpec((T, LANE), lambda j: (0, j)),
        out_shape=jax.ShapeDtypeStruct((T, nb * LANE), F32),
        compiler_params=_cp(("parallel",)),
    )(proj, conv_w)


def _gdn_pre_bwd(proj, conv_w, dout):
    T = proj.shape[0]
    nb = 3 * GDN_HEADS
    H = GDN_HEADS

    def body(x_ref, w_ref, do_ref, dx_ref, dw_ref):
        j = pl.program_id(0)
        t_idx = lax.broadcasted_iota(jnp.int32, (T, LANE), 0)
        x, w = x_ref[...], _rows(w_ref, GDN_CONV)
        c = _conv(x, w, t_idx)
        s = _silu(c)
        rn = lax.rsqrt(jnp.sum(s * s, axis=-1, keepdims=True) + EPS)
        do = do_ref[...]
        y = s * rn
        ds_normed = rn * (do - y * jnp.sum(do * y, axis=-1, keepdims=True))
        ds = jnp.where(j < N_QK_BLOCKS, ds_normed, do)
        dx, dw = _conv_bwd(x, w, ds * _dsilu(c), t_idx)
        dx_ref[...] = dx.astype(BF16)
        _store_rows(dw_ref, dw)

    col = pl.BlockSpec((T, LANE), lambda j: (0, j))
    wcol = pl.BlockSpec((GDN_CONV, LANE), lambda j: (0, j))
    return pl.pallas_call(
        body, name="gdn_pre_bwd", grid=(nb,),
        in_specs=[col, wcol, pl.BlockSpec((None, None, T, LANE), lambda j: (j // H, j % H, 0, 0))],
        out_specs=[col, wcol],
        out_shape=[jax.ShapeDtypeStruct((T, nb * LANE), BF16), jax.ShapeDtypeStruct((GDN_CONV, nb * LANE), F32)],
        compiler_params=_cp(("parallel",)),
    )(proj, conv_w, dout)


def _gdn_consts():
    C = GDN_CHUNK
    r = lax.broadcasted_iota(jnp.int32, (C, C), 0)
    c = lax.broadcasted_iota(jnp.int32, (C, C), 1)
    return dict(incl=r >= c, strict=r > c, eye=(r == c).astype(F32),
                low=(r >= c).astype(BF16), up=(r <= c).astype(BF16), ones=jnp.ones((C, C), BF16))


def _gdn_prep_chunk(q, k, v, b, a, alog, dtb, cs):
    n, C, dk = q.shape
    beta = _sigmoid(b)
    g = -jnp.exp(alog) * _softplus(a + dtb)
    g_sq = jnp.broadcast_to(g, (n, C, C))
    g_wide = jnp.broadcast_to(g, (n, C, dk))
    gc_i = _m01_left(cs["low"], cs["up"], g_sq)
    gc_j = _m01_left(cs["ones"], cs["ones"], g_sq * cs["up"].astype(F32))
    gc_wide = _m01_left(cs["low"], cs["up"], g_wide)
    gl_wide = _m01_left(cs["ones"], cs["ones"], g_wide)
    decay = jnp.where(cs["incl"], jnp.exp(jnp.where(cs["incl"], gc_i - gc_j, 0.0)), 0.0)
    egc = jnp.exp(gc_wide)
    qs = q * (dk ** -0.5)
    k_beta = k * beta
    a_mat = jnp.where(cs["strict"], _dot1_nt(k_beta, k) * decay, 0.0)
    inv = cs["eye"] - a_mat
    pw = _dot3(a_mat, a_mat)
    n_factors = C.bit_length() - 2
    for f in range(n_factors):
        inv = inv + _dot3(inv, pw)
        if f < n_factors - 1:
            pw = _dot3(pw, pw)
    u = _dot3(inv, v * beta)
    w = _dot3(inv, k_beta * egc)
    qk = _dot1_nt(qs, k) * decay
    q_dec = qs * egc
    k_dec = k * jnp.exp(gl_wide - gc_wide)
    g_last = jnp.exp(gl_wide)[:, 0:8, :]
    return qk, u, w, q_dec, k_dec, g_last


GDN_PREP_CHUNKS = 8


def _gdn_prep_specs(T):
    C, dk = GDN_CHUNK, GDN_HEAD_DIM
    npc = min(GDN_PREP_CHUNKS, T // C)
    tc = npc * C
    H = GDN_HEADS
    in_specs = [pl.BlockSpec((tc, dk), lambda h, i: (i, h)),
                pl.BlockSpec((tc, dk), lambda h, i: (i, H + h)),
                pl.BlockSpec((tc, dk), lambda h, i: (i, 2 * H + h)),
                pl.BlockSpec((1, tc, 1), lambda h, i: (h, i, 0)),
                pl.BlockSpec((1, tc, 1), lambda h, i: (h, i, 0)),
                pl.BlockSpec((1, 1, 1), lambda h, i: (h, 0, 0)),
                pl.BlockSpec((1, 1, 1), lambda h, i: (h, 0, 0))]
    xs_specs = [pl.BlockSpec((1, tc, C), lambda h, i: (h, i, 0)),
                pl.BlockSpec((1, tc, dk), lambda h, i: (h, i, 0)),
                pl.BlockSpec((1, tc, dk), lambda h, i: (h, i, 0)),
                pl.BlockSpec((1, tc, dk), lambda h, i: (h, i, 0)),
                pl.BlockSpec((1, tc, dk), lambda h, i: (h, i, 0)),
                pl.BlockSpec((1, npc * 8, dk), lambda h, i: (h, i, 0))]
    xs_shapes = [jax.ShapeDtypeStruct((H, T, C), F32)] + [jax.ShapeDtypeStruct((H, T, dk), F32)] * 4 + [
        jax.ShapeDtypeStruct((H, 8 * T // C, dk), F32)]
    return npc, tc, in_specs, xs_specs, xs_shapes


def _gdn_prep_fwd(qkv, b, a, alog, dtb):
    T = qkv.shape[0]
    C = GDN_CHUNK
    npc, tc, in_specs, xs_specs, xs_shapes = _gdn_prep_specs(T)

    def body(q_ref, k_ref, v_ref, b_ref, a_ref, al_ref, dt_ref, qk_ref, u_ref, w_ref, qd_ref, kd_ref, gl_ref):
        cs = _gdn_consts()

        def chunks(val):
            return val.reshape(npc, C, val.shape[-1])

        outs = _gdn_prep_chunk(chunks(q_ref[...]), chunks(k_ref[...]), chunks(v_ref[...]), chunks(b_ref[0]),
                               chunks(a_ref[0]), al_ref[0], dt_ref[0], cs)
        for ref, val in zip((qk_ref, u_ref, w_ref, qd_ref, kd_ref), outs[:5]):
            ref[0] = val.reshape(tc, val.shape[-1])
        gl_ref[0] = outs[5].reshape(npc * 8, outs[5].shape[-1])

    return pl.pallas_call(
        body, name="gdn_prep_fwd", grid=(GDN_HEADS, T // tc),
        in_specs=in_specs, out_specs=xs_specs, out_shape=xs_shapes,
        compiler_params=_cp(("parallel", "parallel")),
    )(qkv, qkv, qkv, b, a, alog, dtb)


def _gdn_prep_bwd(qkv, b, a, alog, dtb, dxs):
    T = qkv.shape[0]
    C, dk, H = GDN_CHUNK, GDN_HEAD_DIM, GDN_HEADS
    npc, tc, in_specs, xs_specs, _ = _gdn_prep_specs(T)

    def body(q_ref, k_ref, v_ref, b_ref, a_ref, al_ref, dt_ref, dqk_ref, du_ref, dw_ref, dqd_ref, dkd_ref, dgl_ref,
             dqkv_ref, db_ref, da_ref, dal_ref, ddt_ref):
        i = pl.program_id(1)
        cs = _gdn_consts()
        r8 = lax.broadcasted_iota(jnp.int32, (8, dk), 0)
        c8 = lax.broadcasted_iota(jnp.int32, (8, dk), 1)
        first = (r8 == 0) & (c8 == 0)

        @pl.when(i == 0)
        def _():
            dal_ref[...] = jnp.zeros_like(dal_ref)
            ddt_ref[...] = jnp.zeros_like(ddt_ref)

        def chunks(val):
            return val.reshape(npc, C, val.shape[-1])

        prim = (chunks(q_ref[...]), chunks(k_ref[...]), chunks(v_ref[...]), chunks(b_ref[0]), chunks(a_ref[0]),
                al_ref[0], dt_ref[0])
        _, vjp = jax.vjp(lambda *p: _gdn_prep_chunk(*p, cs), *prim)
        dgl = jnp.where(first, dgl_ref[0].reshape(npc, 8, dk), 0.0)
        cts = tuple(chunks(r[0]) for r in (dqk_ref, du_ref, dw_ref, dqd_ref, dkd_ref)) + (dgl,)
        dq, dkk, dv, db, da, dal, ddt = vjp(cts)
        for part, val in enumerate((dq, dkk, dv)):
            dqkv_ref[part, 0] = val.reshape(tc, dk)
        db_ref[0] = db.reshape(tc, 1)
        da_ref[0] = da.reshape(tc, 1)
        dal_ref[0] += dal
        ddt_ref[0] += ddt

    thin = pl.BlockSpec((1, tc, 1), lambda h, i: (h, i, 0))
    one = pl.BlockSpec((1, 1, 1), lambda h, i: (h, 0, 0))
    return pl.pallas_call(
        body, name="gdn_prep_bwd", grid=(H, T // tc),
        in_specs=in_specs + xs_specs,
        out_specs=[pl.BlockSpec((3, 1, tc, dk), lambda h, i: (0, h, i, 0)), thin, thin, one, one],
        out_shape=[jax.ShapeDtypeStruct((3, H, T, dk), F32)] + [jax.ShapeDtypeStruct((H, T, 1), F32)] * 2
        + [jax.ShapeDtypeStruct((H, 1, 1), F32)] * 2,
        compiler_params=_cp(("parallel", "arbitrary")),
    )(qkv, qkv, qkv, b, a, alog, dtb, *dxs)


def _gdn_scan_specs(T):
    C, dk, H = GDN_CHUNK, GDN_HEAD_DIM, GDN_HEADS
    return [pl.BlockSpec((H, C, C), lambda n: (0, n, 0))] + [pl.BlockSpec((H, C, dk), lambda n: (0, n, 0))] * 4 + [
        pl.BlockSpec((H, 8, dk), lambda n: (0, n, 0))]


def _gdn_scan_fwd(xs):
    H, T, dk = xs[1].shape
    C = GDN_CHUNK
    n = T // C

    def body(qk_ref, u_ref, w_ref, qd_ref, kd_ref, gl_ref, o_ref, s_ref, state):
        c = pl.program_id(0)

        @pl.when(c == 0)
        def _():
            state[...] = jnp.zeros_like(state)

        S = state[...]
        s_ref[0] = S
        v_new = u_ref[...] - _bdot(w_ref[...], S)
        o_ref[...] = _bdot(qd_ref[...], S) + _bdot(qk_ref[...], v_new)
        state[...] = S * jnp.tile(gl_ref[...], (1, dk // 8, 1)) + _bdot(kd_ref[...], v_new, "tn")

    return pl.pallas_call(
        body, name="gdn_scan_fwd", grid=(n,),
        in_specs=_gdn_scan_specs(T),
        out_specs=[pl.BlockSpec((H, C, dk), lambda n: (0, n, 0)), pl.BlockSpec((1, H, dk, dk), lambda n: (n, 0, 0, 0))],
        out_shape=[jax.ShapeDtypeStruct((H, T, dk), F32), jax.ShapeDtypeStruct((n, H, dk, dk), F32)],
        scratch_shapes=[pltpu.VMEM((H, dk, dk), F32)],
        compiler_params=_cp(("arbitrary",)),
    )(*xs)


def _gdn_scan_bwd(xs, states, do):
    H, T, dk = xs[1].shape
    C = GDN_CHUNK
    n = T // C

    def rev(spec_shape, f):
        return pl.BlockSpec(spec_shape, lambda i: f(n - 1 - i))

    def body(qk_ref, u_ref, w_ref, qd_ref, kd_ref, gl_ref, s_ref, do_ref,
             dqk_ref, du_ref, dw_ref, dqd_ref, dkd_ref, dgl_ref, dstate):
        i = pl.program_id(0)

        @pl.when(i == 0)
        def _():
            dstate[...] = jnp.zeros_like(dstate)

        S = s_ref[0]
        dS = dstate[...]
        do_v = do_ref[...]
        qk, w, qd, kd = qk_ref[...], w_ref[...], qd_ref[...], kd_ref[...]
        v_new = u_ref[...] - _bdot(w, S)
        dv_new = _bdot(qk, do_v, "tn") + _bdot(kd, dS)
        dqk_ref[...] = _bdot(do_v, v_new, "nt")
        dqd_ref[...] = _bdot(do_v, S, "nt")
        dkd_ref[...] = _bdot(v_new, dS, "nt")
        du_ref[...] = dv_new
        dw_ref[...] = -_bdot(dv_new, S, "nt")
        dgl = jnp.sum(jnp.sum(S * dS, axis=2, keepdims=True), axis=1, keepdims=True)
        dgl_ref[...] = jnp.broadcast_to(dgl, dgl_ref.shape)
        dstate[...] = (dS * jnp.tile(gl_ref[...], (1, dk // 8, 1)) + _bdot(qd, do_v, "tn")
                       - _bdot(w, dv_new, "tn"))

    in_specs = [rev((H, C, C), lambda m: (0, m, 0))] + [rev((H, C, dk), lambda m: (0, m, 0))] * 4 + [
        rev((H, 8, dk), lambda m: (0, m, 0)), rev((1, H, dk, dk), lambda m: (m, 0, 0, 0)),
        rev((H, C, dk), lambda m: (0, m, 0))]
    out_specs = [rev((H, C, C), lambda m: (0, m, 0))] + [rev((H, C, dk), lambda m: (0, m, 0))] * 4 + [
        rev((H, 8, dk), lambda m: (0, m, 0))]
    out_shape = [jax.ShapeDtypeStruct((H, T, C), F32)] + [jax.ShapeDtypeStruct((H, T, dk), F32)] * 4 + [
        jax.ShapeDtypeStruct((H, 8 * n, dk), F32)]
    return pl.pallas_call(
        body, name="gdn_scan_bwd", grid=(n,),
        in_specs=in_specs, out_specs=out_specs, out_shape=out_shape,
        scratch_shapes=[pltpu.VMEM((H, dk, dk), F32)],
        compiler_params=_cp(("arbitrary",)),
    )(*xs, states, do)


def _gdn_post_fwd(o, proj, norm_w):
    H, T, dk = o.shape
    tt = _tile(T, 1024)
    zoff = 3 * GDN_HEADS

    def body(o_ref, z_ref, g_ref, y_ref):
        ov = o_ref[0]
        r = lax.rsqrt(jnp.mean(ov * ov, axis=-1, keepdims=True) + EPS)
        y_ref[...] = (ov * r * g_ref[...] * _silu(z_ref[...])).astype(BF16)

    return pl.pallas_call(
        body, name="gdn_post_fwd", grid=(H, T // tt),
        in_specs=[pl.BlockSpec((1, tt, dk), lambda h, i: (h, i, 0)), pl.BlockSpec((tt, dk), lambda h, i: (i, zoff + h)),
                  pl.BlockSpec((1, dk), lambda h, i: (0, 0))],
        out_specs=pl.BlockSpec((tt, dk), lambda h, i: (i, h)),
        out_shape=jax.ShapeDtypeStruct((T, H * dk), BF16),
        compiler_params=_cp(("parallel", "parallel")),
    )(o, proj, norm_w)


def _gdn_post_bwd(o, proj, norm_w, dy):
    H, T, dk = o.shape
    tt = _tile(T, 1024)
    zoff = 3 * GDN_HEADS

    def body(o_ref, z_ref, g_ref, dy_ref, do_ref, dz_ref, dg_ref):
        i = pl.program_id(1)
        ov, z, g, dyv = o_ref[0], z_ref[...], g_ref[...], dy_ref[...]
        r = lax.rsqrt(jnp.mean(ov * ov, axis=-1, keepdims=True) + EPS)
        on = ov * r
        sz = _silu(z)
        dz_ref[...] = (dyv * on * g * _dsilu(z)).astype(BF16)
        dn = dyv * sz
        gdn = dn * g
        do_ref[0] = r * (gdn - on * jnp.mean(gdn * on, axis=-1, keepdims=True))

        @pl.when(i == 0)
        def _():
            dg_ref[...] = jnp.zeros_like(dg_ref)

        dg_ref[0] += jnp.sum(dn * on, axis=0, keepdims=True)

    return pl.pallas_call(
        body, name="gdn_post_bwd", grid=(H, T // tt),
        in_specs=[pl.BlockSpec((1, tt, dk), lambda h, i: (h, i, 0)), pl.BlockSpec((tt, dk), lambda h, i: (i, zoff + h)),
                  pl.BlockSpec((1, dk), lambda h, i: (0, 0)), pl.BlockSpec((tt, dk), lambda h, i: (i, h))],
        out_specs=[pl.BlockSpec((1, tt, dk), lambda h, i: (h, i, 0)), pl.BlockSpec((tt, dk), lambda h, i: (i, h)),
                   pl.BlockSpec((1, 1, dk), lambda h, i: (h, 0, 0))],
        out_shape=[jax.ShapeDtypeStruct((H, T, dk), F32), jax.ShapeDtypeStruct((T, H * dk), BF16),
                   jax.ShapeDtypeStruct((H, 1, dk), F32)],
        compiler_params=_cp(("parallel", "arbitrary")),
    )(o, proj, norm_w, dy)


def _ple_fwd(x, pp, gl, name):
    T, D = x.shape
    tt = _tile(T, 512)

    def body(x_ref, p_ref, g_ref, o_ref):
        o_ref[...] = x_ref[...] + p_ref[...] * _sigmoid(g_ref[...])

    row = pl.BlockSpec((tt, D), lambda i: (i, 0))
    return pl.pallas_call(
        body, name=name, grid=(T // tt,), in_specs=[row, row, row], out_specs=row,
        out_shape=jax.ShapeDtypeStruct((T, D), F32), compiler_params=_cp(("parallel",)),
    )(x, pp, gl)


def _ple_bwd(dx, pp, gl, name):
    T, D = dx.shape
    tt = _tile(T, 512)

    def body(dx_ref, p_ref, g_ref, dp_ref, dg_ref):
        s = _sigmoid(g_ref[...])
        dxv = dx_ref[...]
        dp_ref[...] = (dxv * s).astype(BF16)
        dg_ref[...] = (dxv * p_ref[...] * s * (1.0 - s)).astype(BF16)

    row = pl.BlockSpec((tt, D), lambda i: (i, 0))
    return pl.pallas_call(
        body, name=name, grid=(T // tt,), in_specs=[row, row, row], out_specs=[row, row],
        out_shape=[jax.ShapeDtypeStruct((T, D), BF16)] * 2, compiler_params=_cp(("parallel",)),
    )(dx, pp, gl)


def _heads_in(a2d):
    T = a2d.shape[0]
    a = a2d.reshape(T, 3, SB_HEADS, SB_HEAD_DIM).transpose(1, 2, 0, 3).astype(BF16)
    return a[0], a[1], a[2]


def _heads_out(a):
    H, T, dh = a.shape
    return a.transpose(1, 0, 2).reshape(T, H * dh)


UP_SHARD = 2 * FFN_DIM // N_CHIPS
DOWN_SHARD = FFN_DIM // N_CHIPS
GATE_SHARD = D_MODEL // N_CHIPS
IN_E_SHARD = 2 * D_MODEL // N_CHIPS


def _ffn_fwd(x, norm, W, conv_w, l):
    T = x.shape[0]
    hf = _rms_fwd(x, norm, f"ffn_norm{l}")
    up = _mm(hf, W["w_up"], "nn", f"ffn_up{l}", dims=(T, 2 * FFN_DIM, D_MODEL), b_view=_cols_of(1, l), tn=UP_SHARD)
    act = _ffn_act_fwd(up, conv_w, f"ffn_act{l}")
    x_out = _mm(act, W["w_down"], "nn", f"ffn_down{l}", dims=(T, D_MODEL, FFN_DIM), b_view=_layer_of(l), res=x,
                tn=1024, tk=1408)
    return x_out, (x, hf, up, act)


def _ffn_bwd(dx_out, saved, norm, W, conv_w, l):
    x, hf, up, act = saved
    T = x.shape[0]
    dact = _mm(dx_out, W["w_down"], "nt", f"ffn_dact{l}", dims=(T, FFN_DIM, D_MODEL), b_view=_layer_of(l),
               out_dtype=BF16, tn=1408)
    dw_down = _mm(act, dx_out, "tn", f"ffn_dwdown{l}", tm=1408, tn=1024, tk=512)
    dup, dcw_g, dcw_v = _ffn_act_bwd(up, conv_w, dact, f"ffn_dact_conv{l}")
    dw_up = _mm(hf, dup, "tn", f"ffn_dwup{l}", dims=(D_MODEL, 2 * FFN_DIM, T), b_view=_cols_of(FFN_DIM // UP_SHARD),
                o_view=_cols_of(1), out_shape=(N_CHIPS, D_MODEL, UP_SHARD), tm=1024, tn=UP_SHARD, tk=512)
    dhf = _mm(dup, W["w_up"], "nt", f"ffn_dhf{l}", dims=(T, D_MODEL, 2 * FFN_DIM),
              a_view=_cols_of(FFN_DIM // UP_SHARD), b_view=_cols_of(1, l), tn=1024, tk=UP_SHARD)
    dx, dnorm = _rms_bwd(x, norm, dhf, dx_out, f"ffn_dnorm{l}")
    return (dx, dnorm, dw_up, jnp.concatenate([dcw_g, dcw_v], axis=1),
            dw_down.reshape(N_CHIPS, DOWN_SHARD, D_MODEL))


def _ple_layer_fwd(x, p, norm, W, l):
    T = x.shape[0]
    hg = _rms_fwd(x, norm, f"ple_norm{l}")
    gl = _mm(hg, W["w_ple_gate"], "nn", f"ple_gate{l}", dims=(T, D_MODEL, D_MODEL), b_view=_rows_of(1, l),
             tn=1024, tk=GATE_SHARD)
    pp = _mm(p, W["w_ple"], "nn", f"ple_proj{l}", dims=(T, D_MODEL, PLE_DIM), a_view=_layer_of(l),
             b_view=_cols_of(1, l), tn=PLE_DIM)
    return _ple_fwd(x, pp, gl, f"ple_mix{l}"), (x, hg, gl, pp)


def _ple_layer_bwd(dx_out, saved, p, norm, W, l):
    x, hg, gl, pp = saved
    T = x.shape[0]
    dpp, dgl = _ple_bwd(dx_out, pp, gl, f"ple_dmix{l}")
    dw_ple = _mm(p, dpp, "tn", f"ple_dwple{l}", dims=(PLE_DIM, D_MODEL, T), a_view=_layer_of(l), o_view=_cols_of(1),
                 out_shape=(N_CHIPS, PLE_DIM, PLE_DIM), tm=PLE_DIM, tn=PLE_DIM, tk=512)
    dw_gate = _mm(hg, dgl, "tn", f"ple_dwgate{l}", tm=1024, tn=1024, tk=512)
    dhg = _mm(dgl, W["w_ple_gate"], "nt", f"ple_dhg{l}", dims=(T, D_MODEL, D_MODEL), b_view=_rows_of(1, l),
              tn=GATE_SHARD)
    dx, dnorm = _rms_bwd(x, norm, dhg, dx_out, f"ple_dnorm{l}")
    return dx, dnorm, dw_gate.reshape(N_CHIPS, GATE_SHARD, D_MODEL), dw_ple


def _local_step(x, p, target, W):
    T = x.shape[0]
    H = GDN_HEADS
    G = {}

    hn_e = _rms_fwd(x, W["mix_norm_e"], "mix_norm_e")
    proj_e = _mm(hn_e, W["w_in_e"], "nn", "in_e", dims=(T, 2 * D_MODEL, D_MODEL), b_view=_cols_of(1), tn=IN_E_SHARD)
    pool_out = _pool_fwd(proj_e, W["pool_w"], W["pool_scale"])
    q, k, v = _heads_in(proj_e[:, POOL_WIDTH:])
    attn, ltot = _sb_fwd(q, k, v)
    mix_e = jnp.concatenate([pool_out, _heads_out(attn)], axis=1).astype(BF16)
    x1 = _mm(mix_e, W["w_out_e"], "nn", "out_e", res=x, tn=1024)
    x2, ffn0 = _ffn_fwd(x1, W["ffn_norm"][0:1], W, W["ffn_conv"][0], 0)
    x3, ple0 = _ple_layer_fwd(x2, p, W["ple_norm"][0:1], W, 0)

    hn_o = _rms_fwd(x3, W["mix_norm_o"], "mix_norm_o")
    proj_o = _mm(hn_o, W["w_in_o"], "nn", "in_o", tn=384)
    qkv = _gdn_pre_fwd(proj_o, W["conv_qkv_o"])
    ba = proj_o[:, 4 * D_MODEL:4 * D_MODEL + 2 * H]
    b_h = ba[:, :H].T.reshape(H, T, 1)
    a_h = ba[:, H:].T.reshape(H, T, 1)
    alog = W["a_log_o"].reshape(H, 1, 1)
    dtb = W["dt_bias_o"].reshape(H, 1, 1)
    xs = _gdn_prep_fwd(qkv, b_h, a_h, alog, dtb)
    o, states = _gdn_scan_fwd(xs)
    og = _gdn_post_fwd(o, proj_o, W["gdn_norm_o"])
    x4 = _mm(og, W["w_out_o"], "nn", "out_o", res=x3, tn=1024)
    x5, ffn1 = _ffn_fwd(x4, W["ffn_norm"][1:2], W, W["ffn_conv"][1], 1)
    x6, ple1 = _ple_layer_fwd(x5, p, W["ple_norm"][1:2], W, 1)

    sq, dx6, G["final_norm"] = _final_loss(x6, W["final_norm"], target, "final_loss")

    dx5, dpn1, dwg1, dwp1 = _ple_layer_bwd(dx6, ple1, p, W["ple_norm"][1:2], W, 1)
    dx4, dfn1, dwu1, dfc1, dwd1 = _ffn_bwd(dx5, ffn1, W["ffn_norm"][1:2], W, W["ffn_conv"][1], 1)
    dog = _mm(dx4, W["w_out_o"], "nt", "d_og", tn=1024)
    G["w_out_o"] = _mm(og, dx4, "tn", "dw_out_o", tm=1024, tn=1024, tk=512).reshape(N_CHIPS, GATE_SHARD, D_MODEL)
    do, dz, dgn = _gdn_post_bwd(o, proj_o, W["gdn_norm_o"], dog)
    G["gdn_norm_o"] = jnp.sum(dgn, axis=0)
    dxs = _gdn_scan_bwd(xs, states, do)
    dqkv_act, db, da, dal, ddt = _gdn_prep_bwd(qkv, b_h, a_h, alog, dtb, dxs)
    G["a_log_o"] = dal.reshape(1, H)
    G["dt_bias_o"] = ddt.reshape(1, H)
    dqkv, G["conv_qkv_o"] = _gdn_pre_bwd(proj_o, W["conv_qkv_o"], dqkv_act)
    dba = jnp.concatenate([db.reshape(H, T).T, da.reshape(H, T).T,
                           jnp.zeros((T, ODD_IN_PAD - ODD_IN + 0), F32)], axis=1).astype(BF16)
    dproj_o = jnp.concatenate([dqkv, dz, dba], axis=1)
    G["w_in_o"] = _mm(hn_o, dproj_o, "tn", "dw_in_o", tm=1024, tn=384, tk=512)
    dhn_o = _mm(dproj_o, W["w_in_o"], "nt", "d_hn_o", tn=1024, tk=1408)
    dx3, G["mix_norm_o"] = _rms_bwd(x3, W["mix_norm_o"], dhn_o, dx4, "d_mix_norm_o")

    dx2, dpn0, dwg0, dwp0 = _ple_layer_bwd(dx3, ple0, p, W["ple_norm"][0:1], W, 0)
    dx1, dfn0, dwu0, dfc0, dwd0 = _ffn_bwd(dx2, ffn0, W["ffn_norm"][0:1], W, W["ffn_conv"][0], 0)
    dmix = _mm(dx1, W["w_out_e"], "nt", "d_mix_e", tn=1024)
    G["w_out_e"] = _mm(mix_e, dx1, "tn", "dw_out_e", tm=1024, tn=1024, tk=512).reshape(N_CHIPS, GATE_SHARD, D_MODEL)
    du, G["pool_w"], G["pool_scale"] = _pool_bwd(proj_e, W["pool_w"], W["pool_scale"], dmix)
    dattn = dmix[:, POOL_WIDTH:].reshape(T, SB_HEADS, SB_HEAD_DIM).transpose(1, 0, 2)
    dqa, dka, dva = _sb_bwd(q, k, v, dattn, ltot)
    dproj_e = jnp.concatenate([du, _heads_out(dqa), _heads_out(dka), _heads_out(dva)], axis=1).astype(BF16)
    G["w_in_e"] = _mm(hn_e, dproj_e, "tn", "dw_in_e", dims=(D_MODEL, 2 * D_MODEL, T), o_view=_cols_of(1),
                      out_shape=(N_CHIPS, D_MODEL, IN_E_SHARD), tm=1024, tn=IN_E_SHARD, tk=512)
    dhn_e = _mm(dproj_e, W["w_in_e"], "nt", "d_hn_e", dims=(T, D_MODEL, 2 * D_MODEL), b_view=_cols_of(1),
                tn=1024, tk=IN_E_SHARD)
    grad_x, G["mix_norm_e"] = _rms_bwd(x, W["mix_norm_e"], dhn_e, dx1, "d_mix_norm_e")

    G["ffn_norm"] = jnp.concatenate([dfn0, dfn1], axis=0)
    G["ple_norm"] = jnp.concatenate([dpn0, dpn1], axis=0)
    G["ffn_conv"] = jnp.stack([dfc0, dfc1])
    G["w_up"] = [dwu0, dwu1]
    G["w_down"] = [dwd0, dwd1]
    G["w_ple_gate"] = [dwg0, dwg1]
    G["w_ple"] = [dwp0, dwp1]
    return sq[0, 0], grad_x, G


BIG = ("w_in_e", "w_out_e", "w_in_o", "w_out_o", "w_up", "w_down", "w_ple_gate", "w_ple")
SHARDED_SMALL = (("mix_norm_o", 1), ("conv_qkv_o", 2), ("ffn_conv", 2))
REPLICATED = ("mix_norm_e", "pool_w", "pool_scale", "a_log_o", "dt_bias_o", "gdn_norm_o", "ffn_norm", "ple_norm",
              "final_norm")
WEIGHT_ORDER = ("mix_norm_e", "w_in_e", "pool_w", "pool_scale", "w_out_e", "mix_norm_o", "w_in_o", "conv_qkv_o",
                "a_log_o", "dt_bias_o", "gdn_norm_o", "w_out_o", "ffn_norm", "w_up", "ffn_conv", "w_down", "ple_norm",
                "w_ple_gate", "w_ple", "final_norm")
SMALL_W = LANE
SMALL_ROWS = 16


def _size(shape):
    n = 1
    for s in shape:
        n *= s
    return n


def _pack(arrs, width, granule):
    flat = jnp.concatenate([a.reshape(-1) for a in arrs])
    rows = -(-flat.shape[0] // width)
    rows = -(-rows // granule) * granule
    return jnp.pad(flat, (0, rows * width - flat.shape[0])).reshape(rows, width)


def _unpack(flat2d, shapes):
    flat = flat2d.reshape(-1)
    out, off = [], 0
    for s in shapes:
        out.append(flat[off:off + _size(s)].reshape(s))
        off += _size(s)
    return out


MESH_ID = pl.DeviceIdType.MESH
HBM_SPEC = pl.BlockSpec(memory_space=pltpu.HBM)


def _where_am_i():
    return lax.axis_index("x"), lax.axis_index("y"), lax.axis_index("c")


def _other_chips(x, y):
    return [(1 - x, y), (x, 1 - y), (1 - x, 1 - y)]


def _remote(src, dst, send_sems, recv_sems, k, to):
    return pltpu.make_async_remote_copy(src_ref=src, dst_ref=dst, send_sem=send_sems.at[k], recv_sem=recv_sems.at[k],
                                        device_id=to, device_id_type=MESH_ID)


def _chip_allgather(pack, name):
    R, Wd = pack.shape
    Rh = R // 2

    def body(src_ref, out_ref, send_sems, recv_sems, local_sem):
        x, y, c = _where_am_i()
        me, sib = (x, y, c), (x, y, 1 - c)
        chips = _other_chips(x, y)
        mine_rows = pl.ds(pl.multiple_of(c * Rh, SMALL_ROWS), Rh)
        sib_rows = pl.ds(pl.multiple_of((1 - c) * Rh, SMALL_ROWS), Rh)
        j_me = 2 * x + y
        local = pltpu.make_async_copy(src_ref, out_ref.at[j_me], local_sem)
        local.start()
        first = [_remote(src_ref.at[mine_rows], out_ref.at[j_me, mine_rows], send_sems, recv_sems, k, (cx, cy, c))
                 for k, (cx, cy) in enumerate(chips)]
        for cp in first:
            cp.start()
        passed = []
        for k, (cx, cy) in enumerate(chips):
            blk = out_ref.at[2 * cx + cy, mine_rows]
            _remote(blk, blk, send_sems, recv_sems, k, me).wait_recv()
            fw = _remote(blk, blk, send_sems, recv_sems, 3 + k, sib)
            fw.start()
            passed.append(fw)
        for k, (cx, cy) in enumerate(chips):
            blk = out_ref.at[2 * cx + cy, sib_rows]
            _remote(blk, blk, send_sems, recv_sems, 3 + k, me).wait_recv()
        for cp in first + passed:
            cp.wait_send()
        local.wait()

    return pl.pallas_call(
        body, name=name, in_specs=[HBM_SPEC], out_specs=HBM_SPEC,
        out_shape=jax.ShapeDtypeStruct((N_CHIPS, R, Wd), pack.dtype),
        scratch_shapes=[pltpu.SemaphoreType.DMA((6,)), pltpu.SemaphoreType.DMA((6,)), pltpu.SemaphoreType.DMA],
    )(pack)


def _chip_allgather_many(blocks, name):
    n = len(blocks)

    def body(*refs):
        srcs, outs = refs[:n], refs[n:2 * n]
        send_sems, recv_sems, local_sems = refs[2 * n:]
        x, y, c = _where_am_i()
        me, sib = (x, y, c), (x, y, 1 - c)
        chips = _other_chips(x, y)
        j_me = 2 * x + y
        local = [pltpu.make_async_copy(srcs[p], outs[p].at[j_me], local_sems.at[p]) for p in range(n)]
        for cp in local:
            cp.start()
        first = [_remote(srcs[p].at[c], outs[p].at[j_me, c], send_sems, recv_sems, 6 * p + k, (cx, cy, c))
                 for p in range(n) for k, (cx, cy) in enumerate(chips)]
        for cp in first:
            cp.start()
        passed = []
        for k, (cx, cy) in enumerate(chips):
            for p in range(n):
                blk = outs[p].at[2 * cx + cy, c]
                _remote(blk, blk, send_sems, recv_sems, 6 * p + k, me).wait_recv()
                fw = _remote(blk, blk, send_sems, recv_sems, 6 * p + 3 + k, sib)
                fw.start()
                passed.append(fw)
        for k, (cx, cy) in enumerate(chips):
            for p in range(n):
                blk = outs[p].at[2 * cx + cy, 1 - c]
                _remote(blk, blk, send_sems, recv_sems, 6 * p + 3 + k, me).wait_recv()
        for cp in first + passed:
            cp.wait_send()
        for cp in local:
            cp.wait()

    return pl.pallas_call(
        body, name=name, in_specs=[HBM_SPEC] * n, out_specs=[HBM_SPEC] * n,
        out_shape=[jax.ShapeDtypeStruct((N_CHIPS,) + b.shape, b.dtype) for b in blocks],
        scratch_shapes=[pltpu.SemaphoreType.DMA((6 * n,)), pltpu.SemaphoreType.DMA((6 * n,)),
                        pltpu.SemaphoreType.DMA((n,))],
    )(*blocks)


def _sibling_swap_many(pieces, name):
    n = len(pieces)

    def body(*refs):
        srcs, outs = refs[:n], refs[n:2 * n]
        send_sems, recv_sems = refs[2 * n:]
        x, y, c = _where_am_i()
        cps = [_remote(srcs[p].at[:, 1 - c], outs[p], send_sems, recv_sems, p, (x, y, 1 - c)) for p in range(n)]
        for cp in cps:
            cp.start()
        for cp in cps:
            cp.wait()

    return pl.pallas_call(
        body, name=name, in_specs=[HBM_SPEC] * n, out_specs=[HBM_SPEC] * n,
        out_shape=[jax.ShapeDtypeStruct((g.shape[0],) + g.shape[2:], g.dtype) for g in pieces],
        scratch_shapes=[pltpu.SemaphoreType.DMA((n,)), pltpu.SemaphoreType.DMA((n,))],
    )(*pieces)


def _chip_scatter_many(sums, name):
    n = len(sums)

    def body(*refs):
        srcs, outs = refs[:n], refs[n:2 * n]
        send_sems, recv_sems = refs[2 * n:]
        x, y, c = _where_am_i()
        cps = [_remote(srcs[p].at[2 * cx + cy], outs[p].at[k], send_sems, recv_sems, 3 * p + k, (cx, cy, c))
               for p in range(n) for k, (cx, cy) in enumerate(_other_chips(x, y))]
        for cp in cps:
            cp.start()
        for cp in cps:
            cp.wait()

    return pl.pallas_call(
        body, name=name, in_specs=[HBM_SPEC] * n, out_specs=[HBM_SPEC] * n,
        out_shape=[jax.ShapeDtypeStruct((N_CHIPS - 1,) + s.shape[1:], s.dtype) for s in sums],
        scratch_shapes=[pltpu.SemaphoreType.DMA((3 * n,)), pltpu.SemaphoreType.DMA((3 * n,))],
    )(*sums)


def _sibling_join_many(halves, layers_of, name):
    n = len(halves)
    n_out = len(layers_of)

    def body(*refs):
        srcs, outs = refs[:n], refs[n:n + n_out]
        send_sems, recv_sems, local_sems = refs[n + n_out:]
        x, y, c = _where_am_i()
        local, remote = [], []
        for q, pieces in enumerate(layers_of):
            for layer, p in enumerate(pieces):
                local.append(pltpu.make_async_copy(srcs[p], outs[q].at[layer, c], local_sems.at[p]))
                remote.append(_remote(srcs[p], outs[q].at[layer, c], send_sems, recv_sems, p, (x, y, 1 - c)))
        for cp in local + remote:
            cp.start()
        for cp in remote:
            cp.wait_send()
        for q, pieces in enumerate(layers_of):
            for layer, p in enumerate(pieces):
                blk = outs[q].at[layer, 1 - c]
                _remote(blk, blk, send_sems, recv_sems, p, (x, y, c)).wait_recv()
        for cp in local:
            cp.wait()

    out_shape = [jax.ShapeDtypeStruct((len(pieces), 2) + halves[pieces[0]].shape, F32) for pieces in layers_of]
    return pl.pallas_call(
        body, name=name, in_specs=[HBM_SPEC] * n, out_specs=[HBM_SPEC] * n_out, out_shape=out_shape,
        scratch_shapes=[pltpu.SemaphoreType.DMA((n,)), pltpu.SemaphoreType.DMA((n,)), pltpu.SemaphoreType.DMA((n,))],
    )(*halves)


def _row_tile(rows, pref=512):
    best = 8
    for t in range(8, pref + 1, 8):
        if rows % t == 0:
            best = t
    return best


def _where_ids():
    x, y, c = _where_am_i()
    return jnp.stack([c, 2 * x + y]).astype(jnp.int32)


RS_ROWS = 256


def _chip_sums_bf16(G, A, ids, name):
    n, _, hr, cols = G.shape
    tr = _row_tile(hr, RS_ROWS)

    def body(ids_ref, g_ref, a_ref, o_ref):
        o_ref[...] = (g_ref[...] + a_ref[...]).astype(BF16)

    return pl.pallas_call(
        body, name=name,
        grid_spec=pltpu.PrefetchScalarGridSpec(
            num_scalar_prefetch=1, grid=(n, hr // tr),
            in_specs=[pl.BlockSpec((None, None, tr, cols), lambda j, i, ids: (j, ids[0], i, 0)),
                      pl.BlockSpec((None, tr, cols), lambda j, i, ids: (j, i, 0))],
            out_specs=pl.BlockSpec((None, tr, cols), lambda j, i, ids: (j, i, 0))),
        out_shape=jax.ShapeDtypeStruct((n, hr, cols), BF16),
        compiler_params=_cp(("parallel", "parallel")),
    )(ids, G, A)


def _total_half(G, A, B, ids, name):
    _, _, hr, cols = G.shape
    tr = _row_tile(hr, RS_ROWS)

    def body(ids_ref, g_ref, a_ref, b_ref, o_ref):
        s = g_ref[...] + a_ref[...]
        for k in range(N_CHIPS - 1):
            s = s + b_ref[k].astype(F32)
        o_ref[...] = s

    return pl.pallas_call(
        body, name=name,
        grid_spec=pltpu.PrefetchScalarGridSpec(
            num_scalar_prefetch=1, grid=(hr // tr,),
            in_specs=[pl.BlockSpec((None, None, tr, cols), lambda i, ids: (ids[1], ids[0], i, 0)),
                      pl.BlockSpec((None, tr, cols), lambda i, ids: (ids[1], i, 0)),
                      pl.BlockSpec((N_CHIPS - 1, tr, cols), lambda i, ids: (0, i, 0))],
            out_specs=pl.BlockSpec((tr, cols), lambda i, ids: (i, 0))),
        out_shape=jax.ShapeDtypeStruct((hr, cols), F32),
        compiler_params=_cp(("parallel",)),
    )(ids, G, A, B)


def _small_allreduce(v, name):
    R, Wd = v.shape

    def body(x_ref, sum_ref, all_ref, send_sems, recv_sems, local_sem):
        x, y, c = _where_am_i()
        me, sib = (x, y, c), (x, y, 1 - c)
        chips = _other_chips(x, y)

        def slot(px, py, pc):
            return all_ref.at[4 * px + 2 * py + pc]

        local = pltpu.make_async_copy(x_ref, slot(*me), local_sem)
        local.start()
        first = [_remote(x_ref, slot(*me), send_sems, recv_sems, 0, sib)]
        first += [_remote(x_ref, slot(*me), send_sems, recv_sems, 1 + k, (cx, cy, c)) for k, (cx, cy) in enumerate(chips)]
        for cp in first:
            cp.start()
        passed = []
        for k, (cx, cy) in enumerate(chips):
            blk = slot(cx, cy, c)
            _remote(blk, blk, send_sems, recv_sems, 1 + k, me).wait_recv()
            fw = _remote(blk, blk, send_sems, recv_sems, 4 + k, sib)
            fw.start()
            passed.append(fw)
        _remote(slot(*sib), slot(*sib), send_sems, recv_sems, 0, me).wait_recv()
        for k, (cx, cy) in enumerate(chips):
            blk = slot(cx, cy, 1 - c)
            _remote(blk, blk, send_sems, recv_sems, 4 + k, me).wait_recv()
        for cp in first + passed:
            cp.wait_send()
        local.wait()
        s = all_ref[0]
        for d in range(1, N_DEV):
            s = s + all_ref[d]
        sum_ref[...] = s

    vm = pl.BlockSpec(memory_space=pltpu.VMEM)
    return pl.pallas_call(
        body, name=name, in_specs=[vm], out_specs=[vm, vm],
        out_shape=[jax.ShapeDtypeStruct((R, Wd), F32), jax.ShapeDtypeStruct((N_DEV, R, Wd), F32)],
        scratch_shapes=[pltpu.SemaphoreType.DMA((7,)), pltpu.SemaphoreType.DMA((7,)), pltpu.SemaphoreType.DMA],
    )(v)[0]


def _adamw(w, g, m, v, name):
    L, R, Wd = w.shape
    tr = _row_tile(R, RS_ROWS)
    c1 = 1.0 - ADAM_B1 ** ADAM_STEP
    c2 = 1.0 - ADAM_B2 ** ADAM_STEP

    def body(w_ref, g_ref, m_ref, v_ref, d_ref, nm_ref, nv_ref):
        gv = g_ref[...]
        nm = ADAM_B1 * m_ref[...] + (1.0 - ADAM_B1) * gv
        nv = ADAM_B2 * v_ref[...] + (1.0 - ADAM_B2) * (gv * gv)
        d_ref[...] = -ADAM_LR * ((nm / c1) / (jnp.sqrt(nv / c2) + ADAM_EPS) + ADAM_WD * w_ref[...])
        nm_ref[...] = nm
        nv_ref[...] = nv

    row = pl.BlockSpec((None, tr, Wd), lambda l, i: (l, i, 0))
    shp = jax.ShapeDtypeStruct((L, R, Wd), F32)
    return pl.pallas_call(
        body, name=name, grid=(L, R // tr), in_specs=[row] * 4, out_specs=[row] * 3, out_shape=[shp] * 3,
        compiler_params=_cp(("parallel", "parallel")),
    )(w, g, m, v)


def _two_halves(a):
    cols = a.shape[-1]
    return a.reshape(2, _size(a.shape) // (2 * cols), cols)


def _gather_weights(P):
    gathered = dict(zip(BIG, _chip_allgather_many([_two_halves(P[n].astype(BF16)) for n in BIG], "ag_weights")))
    small_shapes = [P[n].shape for n, _ in SHARDED_SMALL]
    small = _chip_allgather(_pack([P[n] for n, _ in SHARDED_SMALL], SMALL_W, SMALL_ROWS), "ag_small")
    parts = [_unpack(small[j], small_shapes) for j in range(N_CHIPS)]
    full = {n: jnp.concatenate([parts[j][i] for j in range(N_CHIPS)], axis=ax)
            for i, (n, ax) in enumerate(SHARDED_SMALL)}
    W = {n: P[n] for n in REPLICATED}
    W["pool_w"] = P["pool_w"][0]
    W["final_norm"] = P["final_norm"].reshape(1, D_MODEL)
    W["mix_norm_o"] = full["mix_norm_o"]
    W["conv_qkv_o"] = full["conv_qkv_o"][0]
    W["ffn_conv"] = full["ffn_conv"]
    W["w_in_e"] = gathered["w_in_e"].reshape(N_CHIPS, D_MODEL, IN_E_SHARD)
    W["w_out_e"] = gathered["w_out_e"].reshape(D_MODEL, D_MODEL)
    W["w_out_o"] = gathered["w_out_o"].reshape(D_MODEL, D_MODEL)
    w_in_o = gathered["w_in_o"].reshape(N_CHIPS, D_MODEL, ODD_IN // N_CHIPS)
    W["w_in_o"] = jnp.pad(jnp.concatenate([w_in_o[j] for j in range(N_CHIPS)], axis=1),
                          ((0, 0), (0, ODD_IN_PAD - ODD_IN)))
    W["w_up"] = gathered["w_up"]
    W["w_down"] = gathered["w_down"].transpose(1, 0, 2, 3).reshape(2, FFN_DIM, D_MODEL)
    W["w_ple_gate"] = gathered["w_ple_gate"]
    W["w_ple"] = gathered["w_ple"]
    return W


def _reduce_big_gradients(G):
    w_in_o = G["w_in_o"]
    shard = ODD_IN // N_CHIPS
    pieces, layers_of = [], []
    for n in BIG:
        if n == "w_in_o":
            gs = [jnp.stack([w_in_o[:, j * shard:(j + 1) * shard] for j in range(N_CHIPS)])]
        else:
            gs = G[n] if isinstance(G[n], list) else [G[n]]
        layers_of.append(list(range(len(pieces), len(pieces) + len(gs))))
        pieces += [g.reshape(N_CHIPS, 2, g.shape[1] // 2, g.shape[2]) for g in gs]
    ids = _where_ids()
    from_sibling = _sibling_swap_many(pieces, "rs_sibling_swap")
    sums = [_chip_sums_bf16(g, a, ids, f"rs_chip_sums{i}") for i, (g, a) in enumerate(zip(pieces, from_sibling))]
    from_chips = _chip_scatter_many(sums, "rs_chip_scatter")
    halves = [_total_half(g, a, b, ids, f"rs_total{i}")
              for i, (g, a, b) in enumerate(zip(pieces, from_sibling, from_chips))]
    joined = _sibling_join_many(halves, layers_of, "rs_sibling_join")
    return {n: j.reshape(j.shape[0], 2 * j.shape[2], j.shape[3]) for n, j in zip(BIG, joined)}


def kernel(x, p, mix_norm_e, w_in_e, pool_w, pool_scale, w_out_e, mix_norm_o, w_in_o, conv_qkv_o, a_log_o, dt_bias_o, gdn_norm_o, w_out_o, ffn_norm, w_up, ffn_conv, w_down, ple_norm, w_ple_gate, w_ple, final_norm, loss_target, m_mix_norm_e, m_w_in_e, m_pool_w, m_pool_scale, m_w_out_e, m_mix_norm_o, m_w_in_o, m_conv_qkv_o, m_a_log_o, m_dt_bias_o, m_gdn_norm_o, m_w_out_o, m_ffn_norm, m_w_up, m_ffn_conv, m_w_down, m_ple_norm, m_w_ple_gate, m_w_ple, m_final_norm, v_mix_norm_e, v_w_in_e, v_pool_w, v_pool_scale, v_w_out_e, v_mix_norm_o, v_w_in_o, v_conv_qkv_o, v_a_log_o, v_dt_bias_o, v_gdn_norm_o, v_w_out_o, v_ffn_norm, v_w_up, v_ffn_conv, v_w_down, v_ple_norm, v_w_ple_gate, v_w_ple, v_final_norm):
    args = locals()
    P = {n: args[n] for n in WEIGHT_ORDER}
    M = {n: args["m_" + n] for n in WEIGHT_ORDER}
    V = {n: args["v_" + n] for n in WEIGHT_ORDER}

    W = _gather_weights(P)
    T = x.shape[1]
    sq, grad_x, G = _local_step(x.reshape(T, D_MODEL), p.reshape(2, T, PLE_DIM), loss_target.reshape(T, D_MODEL), W)
    out = {}
    for n, g in _reduce_big_gradients(G).items():
        shape = P[n].shape
        delta, new_m, new_v = _adamw(P[n], g, M[n], V[n], f"adamw_{n}")
        out[n] = tuple(a.reshape(shape) for a in (g, delta, new_m, new_v))

    small_full = {n: G[n] for n in REPLICATED}
    small_full["pool_w"] = G["pool_w"][None]
    small_full["final_norm"] = G["final_norm"].reshape(D_MODEL)
    small_full["mix_norm_o"] = G["mix_norm_o"]
    small_full["conv_qkv_o"] = G["conv_qkv_o"][None]
    small_full["ffn_conv"] = G["ffn_conv"]
    small_names = REPLICATED + tuple(n for n, _ in SHARDED_SMALL)
    summed = _small_allreduce(_pack([small_full[n] for n in small_names], SMALL_W, 8), "ar_small")
    g_small = dict(zip(small_names, _unpack(summed, [small_full[n].shape for n in small_names])))
    chip = 2 * lax.axis_index("x") + lax.axis_index("y")
    for n, ax in SHARDED_SMALL:
        width = P[n].shape[ax]
        g_small[n] = lax.dynamic_slice_in_dim(g_small[n], chip * width, width, axis=ax)

    def pack_small(D):
        return _pack([D[n] for n in small_names], SMALL_W, RS_ROWS)[None]

    g_pack = pack_small(g_small)
    upd = _adamw(pack_small(P), g_pack, pack_small(M), pack_small(V), "adamw_small")
    shapes = [P[n].shape for n in small_names]
    for n, *vals in zip(small_names, *[_unpack(a[0], shapes) for a in (g_pack,) + tuple(upd)]):
        out[n] = tuple(vals)

    loss = (0.5 / D_MODEL) * lax.psum(sq, ("x", "y", "c"))
    return (loss, grad_x[None]) + tuple(out[n][i] for i in range(4) for n in WEIGHT_ORDER)
```

```python
import functools

import jax
import jax.numpy as jnp
from jax import lax
from jax.experimental import pallas as pl
from jax.experimental.pallas import tpu as pltpu

F32 = jnp.float32
BF16 = jnp.bfloat16

D_MODEL = 1024
PLE_DIM = 256
POOL_WIDTH = 512
POOL_WINDOWS = (2, 4, 8, 16)
POOL_GROUP_DIM = 128
SB_HEADS = 8
SB_HEAD_DIM = 64
GDN_HEADS = 8
GDN_HEAD_DIM = 128
GDN_CONV = 4
GDN_CHUNK = 64
FFN_DIM = 2816
FFN_CONV = 3
EPS = 1e-6
ODD_IN = 4 * D_MODEL + 2 * GDN_HEADS
ODD_IN_PAD = 33 * 128
ADAM_LR, ADAM_B1, ADAM_B2, ADAM_EPS, ADAM_WD, ADAM_STEP = 0.001, 0.9, 0.999, 1e-08, 0.01, 10

LANE = 128
VMEM_LIMIT = 56 * 1024 * 1024

N_CHIPS = 4
N_DEV = 8


def _cp(sem=None):
    return pltpu.CompilerParams(dimension_semantics=sem, vmem_limit_bytes=VMEM_LIMIT)


def _tile(n, pref):
    if n <= pref:
        return n
    best = None
    for t in range(LANE, pref + 1, LANE):
        if n % t == 0:
            best = t
    assert best is not None, (n, pref)
    return best


_DIMS = {"nn": (((1,), (0,)), ((), ())), "nt": (((1,), (1,)), ((), ())), "tn": (((0,), (0,)), ((), ()))}
_BDIMS = {"nn": (((2,), (1,)), ((0,), (0,))), "nt": (((2,), (2,)), ((0,), (0,))), "tn": (((1,), (1,)), ((0,), (0,)))}


def _dims(mode, ndim):
    return (_BDIMS if ndim == 3 else _DIMS)[mode]


def _dot(a, b, mode="nn"):
    return lax.dot_general(a.astype(BF16), b.astype(BF16), _dims(mode, a.ndim), preferred_element_type=F32)


def _bdot(a, b, mode="nn"):
    return lax.dot_general(a.astype(BF16), b.astype(BF16), _BDIMS[mode], preferred_element_type=F32)


def _split2(x):
    hi = x.astype(BF16)
    lo = (x - hi.astype(F32)).astype(BF16)
    return hi, lo


def _split3(x):
    hi = x.astype(BF16)
    r = x - hi.astype(F32)
    mid = r.astype(BF16)
    lo = (r - mid.astype(F32)).astype(BF16)
    return hi, mid, lo


def _dot_x01(x, m01, mode="nn"):
    hi, lo = _split2(x)
    return (lax.dot_general(hi, m01, _DIMS[mode], preferred_element_type=F32)
            + lax.dot_general(lo, m01, _DIMS[mode], preferred_element_type=F32))


def _dot3_raw(a, b, mode):
    ah, al = _split2(a)
    bh, bl = _split2(b)
    d = _dims(mode, a.ndim)
    return (lax.dot_general(ah, bh, d, preferred_element_type=F32)
            + lax.dot_general(ah, bl, d, preferred_element_type=F32)
            + lax.dot_general(al, bh, d, preferred_element_type=F32))


@jax.custom_vjp
def _dot3(a, b):
    return _dot3_raw(a, b, "nn")


def _dot3_fwd(a, b):
    return _dot3_raw(a, b, "nn"), (a, b)


def _dot3_bwd(res, g):
    a, b = res
    return _dot3_raw(g, b, "nt"), _dot3_raw(a, g, "tn")


_dot3.defvjp(_dot3_fwd, _dot3_bwd)


@jax.custom_vjp
def _dot1_nt(a, b):
    return _dot(a, b, "nt")


def _dot1_nt_fwd(a, b):
    return _dot(a, b, "nt"), (a, b)


def _dot1_nt_bwd(res, g):
    a, b = res
    return _dot(g, b, "nn"), _dot(g, a, "tn")


_dot1_nt.defvjp(_dot1_nt_fwd, _dot1_nt_bwd)


def _m01_left_raw(m, x):
    d = _dims("nn", x.ndim)
    if x.ndim == 3:
        m = jnp.broadcast_to(m, (x.shape[0],) + m.shape)
    p0, p1, p2 = _split3(x)
    return (lax.dot_general(m, p0, d, preferred_element_type=F32)
            + lax.dot_general(m, p1, d, preferred_element_type=F32)
            + lax.dot_general(m, p2, d, preferred_element_type=F32))


@jax.custom_vjp
def _m01_left(m, mt, x):
    return _m01_left_raw(m, x)


def _m01_left_fwd(m, mt, x):
    return _m01_left_raw(m, x), (m, mt)


def _m01_left_bwd(res, g):
    m, mt = res
    return jnp.zeros_like(m), jnp.zeros_like(mt), _m01_left_raw(mt, g)


_m01_left.defvjp(_m01_left_fwd, _m01_left_bwd)


def _softplus(x):
    return jnp.maximum(x, 0.0) + jnp.log(1.0 + jnp.exp(-jnp.abs(x)))


def _sigmoid(x):
    return 1.0 / (1.0 + jnp.exp(-x))


def _silu(x):
    return x * _sigmoid(x)


def _dsilu(x):
    s = _sigmoid(x)
    return s * (1.0 + x * (1.0 - s))


def _cols_of(n_blocks_per_part, *fixed):
    return lambda r, c: (c // n_blocks_per_part,) + fixed + (r, c % n_blocks_per_part)


def _rows_of(n_blocks_per_part, *fixed):
    return lambda r, c: (r // n_blocks_per_part,) + fixed + (r % n_blocks_per_part, c)


def _layer_of(layer):
    return lambda r, c: (layer, r, c)


def _mm(a, b, mode, name, out_dtype=F32, res=None, tm=512, tn=512, tk=1024,
        dims=None, a_view=None, b_view=None, o_view=None, out_shape=None):
    if dims is None:
        if mode == "nn":
            (M, K), (K2, N) = a.shape, b.shape
        elif mode == "nt":
            (M, K), (N, K2) = a.shape, b.shape
        else:
            (K, M), (K2, N) = a.shape, b.shape
        assert K == K2, (name, a.shape, b.shape)
    else:
        M, N, K = dims
    tm, tn, tk = _tile(M, tm), _tile(N, tn), _tile(K, tk)
    nk = K // tk

    def spec(arr, blk, view, rc):
        view = view or (lambda r, c: (r, c))
        return pl.BlockSpec((None,) * (arr.ndim - 2) + blk, lambda i, j, k: view(*rc(i, j, k)))

    if mode == "tn":
        a_spec = spec(a, (tk, tm), a_view, lambda i, j, k: (k, i))
    else:
        a_spec = spec(a, (tm, tk), a_view, lambda i, j, k: (i, k))
    if mode == "nt":
        b_spec = spec(b, (tn, tk), b_view, lambda i, j, k: (j, k))
    else:
        b_spec = spec(b, (tk, tn), b_view, lambda i, j, k: (k, j))
    out_shape = out_shape or (M, N)
    o_spec = pl.BlockSpec((None,) * (len(out_shape) - 2) + (tm, tn),
                          lambda i, j, k: (o_view or (lambda r, c: (r, c)))(i, j))
    has_res = res is not None
    assert not (has_res and o_view), name

    def body(*refs):
        if has_res:
            a_ref, b_ref, r_ref, o_ref, acc = refs
        else:
            a_ref, b_ref, o_ref, acc = refs
        k = pl.program_id(2)

        @pl.when(k == 0)
        def _():
            acc[...] = jnp.zeros_like(acc)

        acc[...] += _dot(a_ref[...], b_ref[...], mode)

        @pl.when(k == nk - 1)
        def _():
            r = acc[...]
            if has_res:
                r = r + r_ref[...]
            o_ref[...] = r.astype(out_dtype)

    ins = [a, b] + ([res] if has_res else [])
    in_specs = [a_spec, b_spec] + ([o_spec] if has_res else [])
    return pl.pallas_call(
        body, name=name, grid=(M // tm, N // tn, nk),
        in_specs=in_specs, out_specs=o_spec,
        out_shape=jax.ShapeDtypeStruct(out_shape, out_dtype),
        scratch_shapes=[pltpu.VMEM((tm, tn), F32)],
        compiler_params=_cp(("parallel", "parallel", "arbitrary")),
    )(*ins)


def _rms_fwd(x, gain, name):
    T, D = x.shape
    tt = _tile(T, 512)

    def body(x_ref, g_ref, o_ref):
        xv = x_ref[...]
        r = lax.rsqrt(jnp.mean(xv * xv, axis=-1, keepdims=True) + EPS)
        o_ref[...] = (xv * r * g_ref[...]).astype(BF16)

    return pl.pallas_call(
        body, name=name, grid=(T // tt,),
        in_specs=[pl.BlockSpec((tt, D), lambda i: (i, 0)), pl.BlockSpec((1, D), lambda i: (0, 0))],
        out_specs=pl.BlockSpec((tt, D), lambda i: (i, 0)),
        out_shape=jax.ShapeDtypeStruct((T, D), BF16),
        compiler_params=_cp(("parallel",)),
    )(x, gain)


def _rms_bwd(x, gain, dh, dres, name):
    T, D = x.shape
    tt = _tile(T, 512)

    def body(x_ref, g_ref, dh_ref, dr_ref, dx_ref, dg_ref):
        i = pl.program_id(0)
        xv = x_ref[...]
        dy = dh_ref[...].astype(F32)
        r = lax.rsqrt(jnp.mean(xv * xv, axis=-1, keepdims=True) + EPS)
        xn = xv * r
        gdy = dy * g_ref[...]
        dx = r * (gdy - xn * jnp.mean(gdy * xn, axis=-1, keepdims=True))
        dx_ref[...] = dr_ref[...] + dx

        @pl.when(i == 0)
        def _():
            dg_ref[...] = jnp.zeros_like(dg_ref)

        dg_ref[...] += jnp.sum(dy * xn, axis=0, keepdims=True)

    row = pl.BlockSpec((tt, D), lambda i: (i, 0))
    vec = pl.BlockSpec((1, D), lambda i: (0, 0))
    return pl.pallas_call(
        body, name=name, grid=(T // tt,),
        in_specs=[row, vec, row, row], out_specs=[row, vec],
        out_shape=[jax.ShapeDtypeStruct((T, D), F32), jax.ShapeDtypeStruct((1, D), F32)],
        compiler_params=_cp(("arbitrary",)),
    )(x, gain, dh, dres)


def _final_loss(x, gain, target, name):
    T, D = x.shape
    tt = _tile(T, 512)

    def body(x_ref, g_ref, t_ref, l_ref, dx_ref, dg_ref):
        i = pl.program_id(0)
        xv = x_ref[...]
        r = lax.rsqrt(jnp.mean(xv * xv, axis=-1, keepdims=True) + EPS)
        xn = xv * r
        err = xn * g_ref[...] - t_ref[...]
        dy = err * (1.0 / D)
        gdy = dy * g_ref[...]
        dx_ref[...] = r * (gdy - xn * jnp.mean(gdy * xn, axis=-1, keepdims=True))

        @pl.when(i == 0)
        def _():
            dg_ref[...] = jnp.zeros_like(dg_ref)
            l_ref[...] = jnp.zeros_like(l_ref)

        dg_ref[...] += jnp.sum(dy * xn, axis=0, keepdims=True)
        l_ref[...] += jnp.sum(jnp.sum(err * err, axis=1, keepdims=True), axis=0, keepdims=True)

    row = pl.BlockSpec((tt, D), lambda i: (i, 0))
    vec = pl.BlockSpec((1, D), lambda i: (0, 0))
    return pl.pallas_call(
        body, name=name, grid=(T // tt,),
        in_specs=[row, vec, row],
        out_specs=[pl.BlockSpec((8, LANE), lambda i: (0, 0)), row, vec],
        out_shape=[jax.ShapeDtypeStruct((8, LANE), F32), jax.ShapeDtypeStruct((T, D), F32),
                   jax.ShapeDtypeStruct((1, D), F32)],
        compiler_params=_cp(("arbitrary",)),
    )(x, gain, target)


def _shift_down(x, i, t_idx):
    if i == 0:
        return x
    return jnp.where(t_idx >= i, pltpu.roll(x, i, 0), 0.0)


def _shift_up(x, i, t_idx):
    if i == 0:
        return x
    n = x.shape[0]
    return jnp.where(t_idx < n - i, pltpu.roll(x, n - i, 0), 0.0)


def _pool_select(g, vals):
    out = vals[-1]
    for gi in range(len(vals) - 2, -1, -1):
        out = jnp.where(g == gi, vals[gi], out)
    return out


def _pool_y(u, g, t_idx):
    s1 = u + _shift_down(u, 1, t_idx)
    s2 = s1 + _shift_down(s1, 2, t_idx)
    s3 = s2 + _shift_down(s2, 4, t_idx)
    s4 = s3 + _shift_down(s3, 8, t_idx)
    ws = _pool_select(g, [s1, s2, s3, s4])
    win = _pool_select(g, [jnp.float32(w) for w in POOL_WINDOWS])
    cnt = jnp.minimum(t_idx.astype(F32) + 1.0, win)
    return ws / cnt - u, cnt


def _pool_fwd(proj, pool_w, pool_scale):
    T = proj.shape[0]
    G, C = len(POOL_WINDOWS), POOL_GROUP_DIM

    def body(u_ref, w_ref, s_ref, o_ref):
        g = pl.program_id(0)
        t_idx = lax.broadcasted_iota(jnp.int32, (T, C), 0)
        y, _ = _pool_y(u_ref[...], g, t_idx)
        o_ref[...] = _dot(y, w_ref[0]) * s_ref[...]

    return pl.pallas_call(
        body, name="pool_fwd", grid=(G,),
        in_specs=[pl.BlockSpec((T, C), lambda g: (0, g)), pl.BlockSpec((1, C, C), lambda g: (g, 0, 0)),
                  pl.BlockSpec((1, C), lambda g: (0, g))],
        out_specs=pl.BlockSpec((T, C), lambda g: (0, g)),
        out_shape=jax.ShapeDtypeStruct((T, G * C), F32),
        compiler_params=_cp(("parallel",)),
    )(proj, pool_w, pool_scale)


def _pool_bwd(proj, pool_w, pool_scale, dmix):
    T = proj.shape[0]
    G, C = len(POOL_WINDOWS), POOL_GROUP_DIM

    def body(u_ref, w_ref, s_ref, do_ref, du_ref, dw_ref, ds_ref):
        g = pl.program_id(0)
        t_idx = lax.broadcasted_iota(jnp.int32, (T, C), 0)
        y, cnt = _pool_y(u_ref[...], g, t_idx)
        w = w_ref[0]
        dout = do_ref[...]
        ds_ref[...] = jnp.sum(dout * _dot(y, w), axis=0, keepdims=True)
        dy2 = dout * s_ref[...]
        dw_ref[0] = _dot(y, dy2, "tn")
        dy = _dot(dy2, w, "nt")
        dz = dy / cnt
        r1 = dz + _shift_up(dz, 1, t_idx)
        r2 = r1 + _shift_up(r1, 2, t_idx)
        r3 = r2 + _shift_up(r2, 4, t_idx)
        r4 = r3 + _shift_up(r3, 8, t_idx)
        du_ref[...] = _pool_select(g, [r1, r2, r3, r4]) - dy

    col = pl.BlockSpec((T, C), lambda g: (0, g))
    return pl.pallas_call(
        body, name="pool_bwd", grid=(G,),
        in_specs=[col, pl.BlockSpec((1, C, C), lambda g: (g, 0, 0)), pl.BlockSpec((1, C), lambda g: (0, g)), col],
        out_specs=[col, pl.BlockSpec((1, C, C), lambda g: (g, 0, 0)), pl.BlockSpec((1, C), lambda g: (0, g))],
        out_shape=[jax.ShapeDtypeStruct((T, G * C), F32), jax.ShapeDtypeStruct((G, C, C), F32),
                   jax.ShapeDtypeStruct((1, G * C), F32)],
        compiler_params=_cp(("parallel",)),
    )(proj, pool_w, pool_scale, dmix)


SB_SCALE = SB_HEAD_DIM ** -0.5
SB_BLOCKS_PER_PASS = 2


def _sb_tile_logits(qb, kblk, valid):
    z = _dot(qb, kblk, "nt")
    sp = _softplus(z)
    l1m = -sp
    if valid is not None:
        l1m = jnp.where(valid, l1m, 0.0)
    return z, sp, l1m


def _sb_fwd(q, k, v):
    H, T, dh = q.shape
    B = _tile(T, 256)
    nq = T // B

    def body(q_ref, k_ref, v_ref, o_ref, l_ref):
        qi = pl.program_id(1)
        qb = q_ref[0]
        row = lax.broadcasted_iota(jnp.int32, (B, B), 0)
        col = lax.broadcasted_iota(jnp.int32, (B, B), 1)
        later = (row > col).astype(BF16)

        def tiles(kbs, carry, acc, valid):
            ksl = [pl.ds(pl.multiple_of(kb * B, B), B) for kb in kbs]
            logits = [_sb_tile_logits(qb, k_ref[0, ks, :], valid) for ks in ksl]
            within = [_dot_x01(l1m, later) for _, _, l1m in logits]
            sums = [jnp.sum(l1m, axis=1, keepdims=True) for _, _, l1m in logits]
            for (z, sp, _), rc, s, ks in zip(logits, within, sums, ksl):
                a = jnp.exp(z - sp + rc + carry)
                if valid is not None:
                    a = jnp.where(valid, a, 0.0)
                acc = acc + _dot(a, v_ref[0, ks, :])
                carry = carry + s
            return carry, acc

        state = tiles([qi], jnp.zeros((B, 1), F32), jnp.zeros((B, dh), F32), col < row)
        n_pass = qi // SB_BLOCKS_PER_PASS
        state = lax.fori_loop(
            0, n_pass, lambda i, c: tiles([qi - 1 - SB_BLOCKS_PER_PASS * i - u for u in range(SB_BLOCKS_PER_PASS)],
                                          c[0], c[1], None), state)
        rest = qi - n_pass * SB_BLOCKS_PER_PASS
        carry, acc = lax.fori_loop(0, rest, lambda i, c: tiles([rest - 1 - i], c[0], c[1], None), state)
        o_ref[0] = acc
        l_ref[0] = carry

    qspec = pl.BlockSpec((1, B, dh), lambda h, i: (h, i, 0))
    full = pl.BlockSpec((1, T, dh), lambda h, i: (h, 0, 0))
    return pl.pallas_call(
        body, name="sb_fwd", grid=(H, nq),
        in_specs=[qspec, full, full],
        out_specs=[qspec, pl.BlockSpec((1, B, 1), lambda h, i: (h, i, 0))],
        out_shape=[jax.ShapeDtypeStruct((H, T, dh), F32), jax.ShapeDtypeStruct((H, T, 1), F32)],
        compiler_params=_cp(("parallel", "parallel")),
    )(q, k, v)


def _sb_bwd(q, k, v, dout, ltot):
    H, T, dh = q.shape
    B = _tile(T, 256)
    nq = T // B

    def body(q_ref, k_ref, v_ref, do_ref, l_ref, dq_ref, dk_ref, dv_ref):
        qi = pl.program_id(1)

        @pl.when(qi == 0)
        def _():
            dk_ref[...] = jnp.zeros_like(dk_ref)
            dv_ref[...] = jnp.zeros_like(dv_ref)

        qb = q_ref[0]
        dob = do_ref[0].astype(BF16)
        ltot_q = l_ref[0]
        row = lax.broadcasted_iota(jnp.int32, (B, B), 0)
        col = lax.broadcasted_iota(jnp.int32, (B, B), 1)
        upto = (row <= col).astype(BF16)
        before = (row < col).astype(BF16)

        def tiles(kbs, P, E, dq, valid):
            ksl = [pl.ds(pl.multiple_of(kb * B, B), B) for kb in kbs]
            kblks = [k_ref[0, ks, :] for ks in ksl]
            logits = [_sb_tile_logits(qb, kblk, valid) for kblk in kblks]
            das = [_dot(dob, v_ref[0, ks, :], "nt") for ks in ksl]
            within = [_dot_x01(l1m, upto) for _, _, l1m in logits]
            avals, es = [], []
            for (z, sp, l1m), pc, da in zip(logits, within, das):
                a = jnp.exp(z - sp + (ltot_q - P - pc))
                if valid is not None:
                    a = jnp.where(valid, a, 0.0)
                avals.append(a)
                es.append(da * a)
                P = P + jnp.sum(l1m, axis=1, keepdims=True)
            e_within = [_dot_x01(e, before) for e in es]
            for (z, sp, _), e, ew, a, kblk, ks in zip(logits, es, e_within, avals, kblks, ksl):
                dz = e * jnp.exp(-sp) - jnp.exp(z - sp) * (ew + E)
                if valid is not None:
                    dz = jnp.where(valid, dz, 0.0)
                dzb = dz.astype(BF16)
                dq = dq + _dot(dzb, kblk)
                dk_ref[0, ks, :] += _dot(dzb, qb, "tn")
                dv_ref[0, ks, :] += _dot(a, dob, "tn")
                E = E + jnp.sum(e, axis=1, keepdims=True)
            return P, E, dq

        zeros1 = jnp.zeros((B, 1), F32)
        n_pass = qi // SB_BLOCKS_PER_PASS
        state = lax.fori_loop(
            0, n_pass, lambda i, c: tiles([SB_BLOCKS_PER_PASS * i + u for u in range(SB_BLOCKS_PER_PASS)], *c, None),
            (zeros1, zeros1, jnp.zeros((B, dh), F32)))
        state = lax.fori_loop(n_pass * SB_BLOCKS_PER_PASS, qi, lambda kb, c: tiles([kb], *c, None), state)
        _, _, dq = tiles([qi], *state, col < row)
        dq_ref[0] = dq * SB_SCALE

    qspec = pl.BlockSpec((1, B, dh), lambda h, i: (h, i, 0))
    full = pl.BlockSpec((1, T, dh), lambda h, i: (h, 0, 0))
    shp = jax.ShapeDtypeStruct((H, T, dh), F32)
    return pl.pallas_call(
        body, name="sb_bwd", grid=(H, nq),
        in_specs=[qspec, full, full, qspec, pl.BlockSpec((1, B, 1), lambda h, i: (h, i, 0))],
        out_specs=[qspec, full, full],
        out_shape=[shp, shp, shp],
        compiler_params=_cp(("parallel", "arbitrary")),
    )(q, k, v, dout, ltot)


def _rows(w_ref, K):
    return [w_ref[i:i + 1, :] for i in range(K)]


def _conv(x, ws, t_idx):
    K = len(ws)
    y = ws[K - 1] * x
    for i in range(K - 1):
        y = y + ws[i] * _shift_down(x, K - 1 - i, t_idx)
    return y


def _conv_bwd(x, ws, dy, t_idx):
    K = len(ws)
    dx = ws[K - 1] * dy
    dws = []
    for i in range(K - 1):
        dx = dx + ws[i] * _shift_up(dy, K - 1 - i, t_idx)
        dws.append(jnp.sum(dy * _shift_down(x, K - 1 - i, t_idx), axis=0, keepdims=True))
    dws.append(jnp.sum(dy * x, axis=0, keepdims=True))
    return dx, dws


def _store_rows(ref, rows):
    for i, r in enumerate(rows):
        ref[i:i + 1, :] = r


def _ffn_act_fwd(up, conv_w, name):
    T = up.shape[0]
    F = FFN_DIM
    nb = F // LANE

    def body(g_ref, v_ref, wg_ref, wv_ref, o_ref):
        t_idx = lax.broadcasted_iota(jnp.int32, (T, LANE), 0)
        cg = _conv(g_ref[...], _rows(wg_ref, FFN_CONV), t_idx)
        cv = _conv(v_ref[...], _rows(wv_ref, FFN_CONV), t_idx)
        o_ref[...] = (_silu(cg) * cv).astype(BF16)

    return pl.pallas_call(
        body, name=name, grid=(nb,),
        in_specs=[pl.BlockSpec((T, LANE), lambda j: (0, j)), pl.BlockSpec((T, LANE), lambda j: (0, j + nb)),
                  pl.BlockSpec((FFN_CONV, LANE), lambda j: (0, j)),
                  pl.BlockSpec((FFN_CONV, LANE), lambda j: (0, j + nb))],
        out_specs=pl.BlockSpec((T, LANE), lambda j: (0, j)),
        out_shape=jax.ShapeDtypeStruct((T, F), BF16),
        compiler_params=_cp(("parallel",)),
    )(up, up, conv_w, conv_w)


def _ffn_act_bwd(up, conv_w, dact, name):
    T = up.shape[0]
    F = FFN_DIM
    nb = F // LANE

    def body(g_ref, v_ref, wg_ref, wv_ref, da_ref, dup_ref, dwg_ref, dwv_ref):
        t_idx = lax.broadcasted_iota(jnp.int32, (T, LANE), 0)
        xg, xv, wg, wv = g_ref[...], v_ref[...], _rows(wg_ref, FFN_CONV), _rows(wv_ref, FFN_CONV)
        cg = _conv(xg, wg, t_idx)
        cv = _conv(xv, wv, t_idx)
        da = da_ref[...].astype(F32)
        dxg, dwg = _conv_bwd(xg, wg, da * cv * _dsilu(cg), t_idx)
        dxv, dwv = _conv_bwd(xv, wv, da * _silu(cg), t_idx)
        dup_ref[0] = dxg.astype(BF16)
        dup_ref[1] = dxv.astype(BF16)
        _store_rows(dwg_ref, dwg)
        _store_rows(dwv_ref, dwv)

    col = pl.BlockSpec((T, LANE), lambda j: (0, j))
    wcol = pl.BlockSpec((FFN_CONV, LANE), lambda j: (0, j))
    return pl.pallas_call(
        body, name=name, grid=(nb,),
        in_specs=[col, pl.BlockSpec((T, LANE), lambda j: (0, j + nb)), wcol,
                  pl.BlockSpec((FFN_CONV, LANE), lambda j: (0, j + nb)), col],
        out_specs=[pl.BlockSpec((2, T, LANE), lambda j: (0, 0, j)), wcol, wcol],
        out_shape=[jax.ShapeDtypeStruct((2, T, F), BF16),
                   jax.ShapeDtypeStruct((FFN_CONV, F), F32), jax.ShapeDtypeStruct((FFN_CONV, F), F32)],
        compiler_params=_cp(("parallel",)),
    )(up, up, conv_w, conv_w, dact)


N_QK_BLOCKS = 2 * GDN_HEADS


def _gdn_pre_fwd(proj, conv_w):
    T = proj.shape[0]
    nb = 3 * GDN_HEADS

    def body(x_ref, w_ref, o_ref):
        j = pl.program_id(0)
        t_idx = lax.broadcasted_iota(jnp.int32, (T, LANE), 0)
        s = _silu(_conv(x_ref[...], _rows(w_ref, GDN_CONV), t_idx))
        rn = lax.rsqrt(jnp.sum(s * s, axis=-1, keepdims=True) + EPS)
        o_ref[...] = s * jnp.where(j < N_QK_BLOCKS, rn, 1.0)

    return pl.pallas_call(
        body, name="gdn_pre_fwd", grid=(nb,),
        in_specs=[pl.BlockSpec((T, LANE), lambda j: (0, j)), pl.BlockSpec((GDN_CONV, LANE), lambda j: (0, j))],
        out_specs=pl.BlockSpec((T, LANE), lambda j: (0, j)),
        out_shape=jax.ShapeDtypeStruct((T, nb * LANE), F32),
        compiler_params=_cp(("parallel",)),
    )(proj, conv_w)


def _gdn_pre_bwd(proj, conv_w, dout):
    T = proj.shape[0]
    nb = 3 * GDN_HEADS
    H = GDN_HEADS

    def body(x_ref, w_ref, do_ref, dx_ref, dw_ref):
        j = pl.program_id(0)
        t_idx = lax.broadcasted_iota(jnp.int32, (T, LANE), 0)
        x, w = x_ref[...], _rows(w_ref, GDN_CONV)
        c = _conv(x, w, t_idx)
        s = _silu(c)
        rn = lax.rsqrt(jnp.sum(s * s, axis=-1, keepdims=True) + EPS)
        do = do_ref[...]
        y = s * rn
        ds_normed = rn * (do - y * jnp.sum(do * y, axis=-1, keepdims=True))
        ds = jnp.where(j < N_QK_BLOCKS, ds_normed, do)
        dx, dw = _conv_bwd(x, w, ds * _dsilu(c), t_idx)
        dx_ref[...] = dx.astype(BF16)
        _store_rows(dw_ref, dw)

    col = pl.BlockSpec((T, LANE), lambda j: (0, j))
    wcol = pl.BlockSpec((GDN_CONV, LANE), lambda j: (0, j))
    return pl.pallas_call(
        body, name="gdn_pre_bwd", grid=(nb,),
        in_specs=[col, wcol, pl.BlockSpec((None, None, T, LANE), lambda j: (j // H, j % H, 0, 0))],
        out_specs=[col, wcol],
        out_shape=[jax.ShapeDtypeStruct((T, nb * LANE), BF16), jax.ShapeDtypeStruct((GDN_CONV, nb * LANE), F32)],
        compiler_params=_cp(("parallel",)),
    )(proj, conv_w, dout)


def _gdn_consts():
    C = GDN_CHUNK
    r = lax.broadcasted_iota(jnp.int32, (C, C), 0)
    c = lax.broadcasted_iota(jnp.int32, (C, C), 1)
    return dict(incl=r >= c, strict=r > c, eye=(r == c).astype(F32),
                low=(r >= c).astype(BF16), up=(r <= c).astype(BF16), ones=jnp.ones((C, C), BF16))


def _gdn_prep_chunk(q, k, v, b, a, alog, dtb, cs):
    n, C, dk = q.shape
    beta = _sigmoid(b)
    g = -jnp.exp(alog) * _softplus(a + dtb)
    g_sq = jnp.broadcast_to(g, (n, C, C))
    g_wide = jnp.broadcast_to(g, (n, C, dk))
    gc_i = _m01_left(cs["low"], cs["up"], g_sq)
    gc_j = _m01_left(cs["ones"], cs["ones"], g_sq * cs["up"].astype(F32))
    gc_wide = _m01_left(cs["low"], cs["up"], g_wide)
    gl_wide = _m01_left(cs["ones"], cs["ones"], g_wide)
    decay = jnp.where(cs["incl"], jnp.exp(jnp.where(cs["incl"], gc_i - gc_j, 0.0)), 0.0)
    egc = jnp.exp(gc_wide)
    qs = q * (dk ** -0.5)
    k_beta = k * beta
    a_mat = jnp.where(cs["strict"], _dot1_nt(k_beta, k) * decay, 0.0)
    inv = cs["eye"] - a_mat
    pw = _dot3(a_mat, a_mat)
    n_factors = C.bit_length() - 2
    for f in range(n_factors):
        inv = inv + _dot3(inv, pw)
        if f < n_factors - 1:
            pw = _dot3(pw, pw)
    u = _dot3(inv, v * beta)
    w = _dot3(inv, k_beta * egc)
    qk = _dot1_nt(qs, k) * decay
    q_dec = qs * egc
    k_dec = k * jnp.exp(gl_wide - gc_wide)
    g_last = jnp.exp(gl_wide)[:, 0:8, :]
    return qk, u, w, q_dec, k_dec, g_last


GDN_PREP_CHUNKS = 8


def _gdn_prep_specs(T):
    C, dk = GDN_CHUNK, GDN_HEAD_DIM
    npc = min(GDN_PREP_CHUNKS, T // C)
    tc = npc * C
    H = GDN_HEADS
    in_specs = [pl.BlockSpec((tc, dk), lambda h, i: (i, h)),
                pl.BlockSpec((tc, dk), lambda h, i: (i, H + h)),
                pl.BlockSpec((tc, dk), lambda h, i: (i, 2 * H + h)),
                pl.BlockSpec((1, tc, 1), lambda h, i: (h, i, 0)),
                pl.BlockSpec((1, tc, 1), lambda h, i: (h, i, 0)),
                pl.BlockSpec((1, 1, 1), lambda h, i: (h, 0, 0)),
                pl.BlockSpec((1, 1, 1), lambda h, i: (h, 0, 0))]
    xs_specs = [pl.BlockSpec((1, tc, C), lambda h, i: (h, i, 0)),
                pl.BlockSpec((1, tc, dk), lambda h, i: (h, i, 0)),
                pl.BlockSpec((1, tc, dk), lambda h, i: (h, i, 0)),
                pl.BlockSpec((1, tc, dk), lambda h, i: (h, i, 0)),
                pl.BlockSpec((1, tc, dk), lambda h, i: (h, i, 0)),
                pl.BlockSpec((1, npc * 8, dk), lambda h, i: (h, i, 0))]
    xs_shapes = [jax.ShapeDtypeStruct((H, T, C), F32)] + [jax.ShapeDtypeStruct((H, T, dk), F32)] * 4 + [
        jax.ShapeDtypeStruct((H, 8 * T // C, dk), F32)]
    return npc, tc, in_specs, xs_specs, xs_shapes


def _gdn_prep_fwd(qkv, b, a, alog, dtb):
    T = qkv.shape[0]
    C = GDN_CHUNK
    npc, tc, in_specs, xs_specs, xs_shapes = _gdn_prep_specs(T)

    def body(q_ref, k_ref, v_ref, b_ref, a_ref, al_ref, dt_ref, qk_ref, u_ref, w_ref, qd_ref, kd_ref, gl_ref):
        cs = _gdn_consts()

        def chunks(val):
            return val.reshape(npc, C, val.shape[-1])

        outs = _gdn_prep_chunk(chunks(q_ref[...]), chunks(k_ref[...]), chunks(v_ref[...]), chunks(b_ref[0]),
                               chunks(a_ref[0]), al_ref[0], dt_ref[0], cs)
        for ref, val in zip((qk_ref, u_ref, w_ref, qd_ref, kd_ref), outs[:5]):
            ref[0] = val.reshape(tc, val.shape[-1])
        gl_ref[0] = outs[5].reshape(npc * 8, outs[5].shape[-1])

    return pl.pallas_call(
        body, name="gdn_prep_fwd", grid=(GDN_HEADS, T // tc),
        in_specs=in_specs, out_specs=xs_specs, out_shape=xs_shapes,
        compiler_params=_cp(("parallel", "parallel")),
    )(qkv, qkv, qkv, b, a, alog, dtb)


def _gdn_prep_bwd(qkv, b, a, alog, dtb, dxs):
    T = qkv.shape[0]
    C, dk, H = GDN_CHUNK, GDN_HEAD_DIM, GDN_HEADS
    npc, tc, in_specs, xs_specs, _ = _gdn_prep_specs(T)

    def body(q_ref, k_ref, v_ref, b_ref, a_ref, al_ref, dt_ref, dqk_ref, du_ref, dw_ref, dqd_ref, dkd_ref, dgl_ref,
             dqkv_ref, db_ref, da_ref, dal_ref, ddt_ref):
        i = pl.program_id(1)
        cs = _gdn_consts()
        r8 = lax.broadcasted_iota(jnp.int32, (8, dk), 0)
        c8 = lax.broadcasted_iota(jnp.int32, (8, dk), 1)
        first = (r8 == 0) & (c8 == 0)

        @pl.when(i == 0)
        def _():
            dal_ref[...] = jnp.zeros_like(dal_ref)
            ddt_ref[...] = jnp.zeros_like(ddt_ref)

        def chunks(val):
            return val.reshape(npc, C, val.shape[-1])

        prim = (chunks(q_ref[...]), chunks(k_ref[...]), chunks(v_ref[...]), chunks(b_ref[0]), chunks(a_ref[0]),
                al_ref[0], dt_ref[0])
        _, vjp = jax.vjp(lambda *p: _gdn_prep_chunk(*p, cs), *prim)
        dgl = jnp.where(first, dgl_ref[0].reshape(npc, 8, dk), 0.0)
        cts = tuple(chunks(r[0]) for r in (dqk_ref, du_ref, dw_ref, dqd_ref, dkd_ref)) + (dgl,)
        dq, dkk, dv, db, da, dal, ddt = vjp(cts)
        for part, val in enumerate((dq, dkk, dv)):
            dqkv_ref[part, 0] = val.reshape(tc, dk)
        db_ref[0] = db.reshape(tc, 1)
        da_ref[0] = da.reshape(tc, 1)
        dal_ref[0] += dal
        ddt_ref[0] += ddt

    thin = pl.BlockSpec((1, tc, 1), lambda h, i: (h, i, 0))
    one = pl.BlockSpec((1, 1, 1), lambda h, i: (h, 0, 0))
    return pl.pallas_call(
        body, name="gdn_prep_bwd", grid=(H, T // tc),
        in_specs=in_specs + xs_specs,
        out_specs=[pl.BlockSpec((3, 1, tc, dk), lambda h, i: (0, h, i, 0)), thin, thin, one, one],
        out_shape=[jax.ShapeDtypeStruct((3, H, T, dk), F32)] + [jax.ShapeDtypeStruct((H, T, 1), F32)] * 2
        + [jax.ShapeDtypeStruct((H, 1, 1), F32)] * 2,
        compiler_params=_cp(("parallel", "arbitrary")),
    )(qkv, qkv, qkv, b, a, alog, dtb, *dxs)


def _gdn_scan_specs(T):
    C, dk, H = GDN_CHUNK, GDN_HEAD_DIM, GDN_HEADS
    return [pl.BlockSpec((H, C, C), lambda n: (0, n, 0))] + [pl.BlockSpec((H, C, dk), lambda n: (0, n, 0))] * 4 + [
        pl.BlockSpec((H, 8, dk), lambda n: (0, n, 0))]


def _gdn_scan_fwd(xs):
    H, T, dk = xs[1].shape
    C = GDN_CHUNK
    n = T // C

    def body(qk_ref, u_ref, w_ref, qd_ref, kd_ref, gl_ref, o_ref, s_ref, state):
        c = pl.program_id(0)

        @pl.when(c == 0)
        def _():
            state[...] = jnp.zeros_like(state)

        S = state[...]
        s_ref[0] = S
        v_new = u_ref[...] - _bdot(w_ref[...], S)
        o_ref[...] = _bdot(qd_ref[...], S) + _bdot(qk_ref[...], v_new)
        state[...] = S * jnp.tile(gl_ref[...], (1, dk // 8, 1)) + _bdot(kd_ref[...], v_new, "tn")

    return pl.pallas_call(
        body, name="gdn_scan_fwd", grid=(n,),
        in_specs=_gdn_scan_specs(T),
        out_specs=[pl.BlockSpec((H, C, dk), lambda n: (0, n, 0)), pl.BlockSpec((1, H, dk, dk), lambda n: (n, 0, 0, 0))],
        out_shape=[jax.ShapeDtypeStruct((H, T, dk), F32), jax.ShapeDtypeStruct((n, H, dk, dk), F32)],
        scratch_shapes=[pltpu.VMEM((H, dk, dk), F32)],
        compiler_params=_cp(("arbitrary",)),
    )(*xs)


def _gdn_scan_bwd(xs, states, do):
    H, T, dk = xs[1].shape
    C = GDN_CHUNK
    n = T // C

    def rev(spec_shape, f):
        return pl.BlockSpec(spec_shape, lambda i: f(n - 1 - i))

    def body(qk_ref, u_ref, w_ref, qd_ref, kd_ref, gl_ref, s_ref, do_ref,
             dqk_ref, du_ref, dw_ref, dqd_ref, dkd_ref, dgl_ref, dstate):
        i = pl.program_id(0)

        @pl.when(i == 0)
        def _():
            dstate[...] = jnp.zeros_like(dstate)

        S = s_ref[0]
        dS = dstate[...]
        do_v = do_ref[...]
        qk, w, qd, kd = qk_ref[...], w_ref[...], qd_ref[...], kd_ref[...]
        v_new = u_ref[...] - _bdot(w, S)
        dv_new = _bdot(qk, do_v, "tn") + _bdot(kd, dS)
        dqk_ref[...] = _bdot(do_v, v_new, "nt")
        dqd_ref[...] = _bdot(do_v, S, "nt")
        dkd_ref[...] = _bdot(v_new, dS, "nt")
        du_ref[...] = dv_new
        dw_ref[...] = -_bdot(dv_new, S, "nt")
        dgl = jnp.sum(jnp.sum(S * dS, axis=2, keepdims=True), axis=1, keepdims=True)
        dgl_ref[...] = jnp.broadcast_to(dgl, dgl_ref.shape)
        dstate[...] = (dS * jnp.tile(gl_ref[...], (1, dk // 8, 1)) + _bdot(qd, do_v, "tn")
                       - _bdot(w, dv_new, "tn"))

    in_specs = [rev((H, C, C), lambda m: (0, m, 0))] + [rev((H, C, dk), lambda m: (0, m, 0))] * 4 + [
        rev((H, 8, dk), lambda m: (0, m, 0)), rev((1, H, dk, dk), lambda m: (m, 0, 0, 0)),
        rev((H, C, dk), lambda m: (0, m, 0))]
    out_specs = [rev((H, C, C), lambda m: (0, m, 0))] + [rev((H, C, dk), lambda m: (0, m, 0))] * 4 + [
        rev((H, 8, dk), lambda m: (0, m, 0))]
    out_shape = [jax.ShapeDtypeStruct((H, T, C), F32)] + [jax.ShapeDtypeStruct((H, T, dk), F32)] * 4 + [
        jax.ShapeDtypeStruct((H, 8 * n, dk), F32)]
    return pl.pallas_call(
        body, name="gdn_scan_bwd", grid=(n,),
        in_specs=in_specs, out_specs=out_specs, out_shape=out_shape,
        scratch_shapes=[pltpu.VMEM((H, dk, dk), F32)],
        compiler_params=_cp(("arbitrary",)),
    )(*xs, states, do)


def _gdn_post_fwd(o, proj, norm_w):
    H, T, dk = o.shape
    tt = _tile(T, 1024)
    zoff = 3 * GDN_HEADS

    def body(o_ref, z_ref, g_ref, y_ref):
        ov = o_ref[0]
        r = lax.rsqrt(jnp.mean(ov * ov, axis=-1, keepdims=True) + EPS)
        y_ref[...] = (ov * r * g_ref[...] * _silu(z_ref[...])).astype(BF16)

    return pl.pallas_call(
        body, name="gdn_post_fwd", grid=(H, T // tt),
        in_specs=[pl.BlockSpec((1, tt, dk), lambda h, i: (h, i, 0)), pl.BlockSpec((tt, dk), lambda h, i: (i, zoff + h)),
                  pl.BlockSpec((1, dk), lambda h, i: (0, 0))],
        out_specs=pl.BlockSpec((tt, dk), lambda h, i: (i, h)),
        out_shape=jax.ShapeDtypeStruct((T, H * dk), BF16),
        compiler_params=_cp(("parallel", "parallel")),
    )(o, proj, norm_w)


def _gdn_post_bwd(o, proj, norm_w, dy):
    H, T, dk = o.shape
    tt = _tile(T, 1024)
    zoff = 3 * GDN_HEADS

    def body(o_ref, z_ref, g_ref, dy_ref, do_ref, dz_ref, dg_ref):
        i = pl.program_id(1)
        ov, z, g, dyv = o_ref[0], z_ref[...], g_ref[...], dy_ref[...]
        r = lax.rsqrt(jnp.mean(ov * ov, axis=-1, keepdims=True) + EPS)
        on = ov * r
        sz = _silu(z)
        dz_ref[...] = (dyv * on * g * _dsilu(z)).astype(BF16)
        dn = dyv * sz
        gdn = dn * g
        do_ref[0] = r * (gdn - on * jnp.mean(gdn * on, axis=-1, keepdims=True))

        @pl.when(i == 0)
        def _():
            dg_ref[...] = jnp.zeros_like(dg_ref)

        dg_ref[0] += jnp.sum(dn * on, axis=0, keepdims=True)

    return pl.pallas_call(
        body, name="gdn_post_bwd", grid=(H, T // tt),
        in_specs=[pl.BlockSpec((1, tt, dk), lambda h, i: (h, i, 0)), pl.BlockSpec((tt, dk), lambda h, i: (i, zoff + h)),
                  pl.BlockSpec((1, dk), lambda h, i: (0, 0)), pl.BlockSpec((tt, dk), lambda h, i: (i, h))],
        out_specs=[pl.BlockSpec((1, tt, dk), lambda h, i: (h, i, 0)), pl.BlockSpec((tt, dk), lambda h, i: (i, h)),
                   pl.BlockSpec((1, 1, dk), lambda h, i: (h, 0, 0))],
        out_shape=[jax.ShapeDtypeStruct((H, T, dk), F32), jax.ShapeDtypeStruct((T, H * dk), BF16),
                   jax.ShapeDtypeStruct((H, 1, dk), F32)],
        compiler_params=_cp(("parallel", "arbitrary")),
    )(o, proj, norm_w, dy)


def _ple_fwd(x, pp, gl, name):
    T, D = x.shape
    tt = _tile(T, 512)

    def body(x_ref, p_ref, g_ref, o_ref):
        o_ref[...] = x_ref[...] + p_ref[...] * _sigmoid(g_ref[...])

    row = pl.BlockSpec((tt, D), lambda i: (i, 0))
    return pl.pallas_call(
        body, name=name, grid=(T // tt,), in_specs=[row, row, row], out_specs=row,
        out_shape=jax.ShapeDtypeStruct((T, D), F32), compiler_params=_cp(("parallel",)),
    )(x, pp, gl)


def _ple_bwd(dx, pp, gl, name):
    T, D = dx.shape
    tt = _tile(T, 512)

    def body(dx_ref, p_ref, g_ref, dp_ref, dg_ref):
        s = _sigmoid(g_ref[...])
        dxv = dx_ref[...]
        dp_ref[...] = (dxv * s).astype(BF16)
        dg_ref[...] = (dxv * p_ref[...] * s * (1.0 - s)).astype(BF16)

    row = pl.BlockSpec((tt, D), lambda i: (i, 0))
    return pl.pallas_call(
        body, name=name, grid=(T // tt,), in_specs=[row, row, row], out_specs=[row, row],
        out_shape=[jax.ShapeDtypeStruct((T, D), BF16)] * 2, compiler_params=_cp(("parallel",)),
    )(dx, pp, gl)


def _heads_in(a2d):
    T = a2d.shape[0]
    a = a2d.reshape(T, 3, SB_HEADS, SB_HEAD_DIM).transpose(1, 2, 0, 3).astype(BF16)
    return a[0] * jnp.asarray(SB_SCALE, BF16), a[1], a[2]


def _heads_out(a):
    H, T, dh = a.shape
    return a.transpose(1, 0, 2).reshape(T, H * dh)


UP_SHARD = 2 * FFN_DIM // N_CHIPS
DOWN_SHARD = FFN_DIM // N_CHIPS
GATE_SHARD = D_MODEL // N_CHIPS
IN_E_SHARD = 2 * D_MODEL // N_CHIPS


def _ffn_fwd(x, norm, W, conv_w, l):
    T = x.shape[0]
    hf = _rms_fwd(x, norm, f"ffn_norm{l}")
    up = _mm(hf, W["w_up"], "nn", f"ffn_up{l}", dims=(T, 2 * FFN_DIM, D_MODEL), b_view=_cols_of(1, l), tn=UP_SHARD)
    act = _ffn_act_fwd(up, conv_w, f"ffn_act{l}")
    x_out = _mm(act, W["w_down"], "nn", f"ffn_down{l}", dims=(T, D_MODEL, FFN_DIM), b_view=_layer_of(l), res=x,
                tn=1024, tk=1408)
    return x_out, (x, hf, up, act)


def _ffn_bwd(dx_out, saved, norm, W, conv_w, l):
    x, hf, up, act = saved
    T = x.shape[0]
    dact = _mm(dx_out, W["w_down"], "nt", f"ffn_dact{l}", dims=(T, FFN_DIM, D_MODEL), b_view=_layer_of(l),
               out_dtype=BF16, tn=1408)
    dw_down = _mm(act, dx_out, "tn", f"ffn_dwdown{l}", tm=1408, tn=1024, tk=512)
    dup, dcw_g, dcw_v = _ffn_act_bwd(up, conv_w, dact, f"ffn_dact_conv{l}")
    dw_up = _mm(hf, dup, "tn", f"ffn_dwup{l}", dims=(D_MODEL, 2 * FFN_DIM, T), b_view=_cols_of(FFN_DIM // UP_SHARD),
                o_view=_cols_of(1), out_shape=(N_CHIPS, D_MODEL, UP_SHARD), tm=1024, tn=UP_SHARD, tk=512)
    dhf = _mm(dup, W["w_up"], "nt", f"ffn_dhf{l}", dims=(T, D_MODEL, 2 * FFN_DIM),
              a_view=_cols_of(FFN_DIM // UP_SHARD), b_view=_cols_of(1, l), tn=1024, tk=UP_SHARD)
    dx, dnorm = _rms_bwd(x, norm, dhf, dx_out, f"ffn_dnorm{l}")
    return (dx, dnorm, dw_up, jnp.concatenate([dcw_g, dcw_v], axis=1),
            dw_down.reshape(N_CHIPS, DOWN_SHARD, D_MODEL))


def _ple_layer_fwd(x, p, norm, W, l):
    T = x.shape[0]
    hg = _rms_fwd(x, norm, f"ple_norm{l}")
    gl = _mm(hg, W["w_ple_gate"], "nn", f"ple_gate{l}", dims=(T, D_MODEL, D_MODEL), b_view=_rows_of(1, l),
             tn=1024, tk=GATE_SHARD)
    pp = _mm(p, W["w_ple"], "nn", f"ple_proj{l}", dims=(T, D_MODEL, PLE_DIM), a_view=_layer_of(l),
             b_view=_cols_of(1, l), tn=PLE_DIM)
    return _ple_fwd(x, pp, gl, f"ple_mix{l}"), (x, hg, gl, pp)


def _ple_layer_bwd(dx_out, saved, p, norm, W, l):
    x, hg, gl, pp = saved
    T = x.shape[0]
    dpp, dgl = _ple_bwd(dx_out, pp, gl, f"ple_dmix{l}")
    dw_ple = _mm(p, dpp, "tn", f"ple_dwple{l}", dims=(PLE_DIM, D_MODEL, T), a_view=_layer_of(l), o_view=_cols_of(1),
                 out_shape=(N_CHIPS, PLE_DIM, PLE_DIM), tm=PLE_DIM, tn=PLE_DIM, tk=512)
    dw_gate = _mm(hg, dgl, "tn", f"ple_dwgate{l}", tm=1024, tn=1024, tk=512)
    dhg = _mm(dgl, W["w_ple_gate"], "nt", f"ple_dhg{l}", dims=(T, D_MODEL, D_MODEL), b_view=_rows_of(1, l),
              tn=GATE_SHARD)
    dx, dnorm = _rms_bwd(x, norm, dhg, dx_out, f"ple_dnorm{l}")
    return dx, dnorm, dw_gate.reshape(N_CHIPS, GATE_SHARD, D_MODEL), dw_ple


def _local_step(x, p, target, W):
    T = x.shape[0]
    H = GDN_HEADS
    G = {}

    hn_e = _rms_fwd(x, W["mix_norm_e"], "mix_norm_e")
    proj_e = _mm(hn_e, W["w_in_e"], "nn", "in_e", dims=(T, 2 * D_MODEL, D_MODEL), b_view=_cols_of(1), tn=IN_E_SHARD)
    pool_out = _pool_fwd(proj_e, W["pool_w"], W["pool_scale"])
    q, k, v = _heads_in(proj_e[:, POOL_WIDTH:])
    attn, ltot = _sb_fwd(q, k, v)
    mix_e = jnp.concatenate([pool_out, _heads_out(attn)], axis=1).astype(BF16)
    x1 = _mm(mix_e, W["w_out_e"], "nn", "out_e", res=x, tn=1024)
    x2, ffn0 = _ffn_fwd(x1, W["ffn_norm"][0:1], W, W["ffn_conv"][0], 0)
    x3, ple0 = _ple_layer_fwd(x2, p, W["ple_norm"][0:1], W, 0)

    hn_o = _rms_fwd(x3, W["mix_norm_o"], "mix_norm_o")
    proj_o = _mm(hn_o, W["w_in_o"], "nn", "in_o", tn=384)
    qkv = _gdn_pre_fwd(proj_o, W["conv_qkv_o"])
    ba = proj_o[:, 4 * D_MODEL:4 * D_MODEL + 2 * H]
    b_h = ba[:, :H].T.reshape(H, T, 1)
    a_h = ba[:, H:].T.reshape(H, T, 1)
    alog = W["a_log_o"].reshape(H, 1, 1)
    dtb = W["dt_bias_o"].reshape(H, 1, 1)
    xs = _gdn_prep_fwd(qkv, b_h, a_h, alog, dtb)
    o, states = _gdn_scan_fwd(xs)
    og = _gdn_post_fwd(o, proj_o, W["gdn_norm_o"])
    x4 = _mm(og, W["w_out_o"], "nn", "out_o", res=x3, tn=1024)
    x5, ffn1 = _ffn_fwd(x4, W["ffn_norm"][1:2], W, W["ffn_conv"][1], 1)
    x6, ple1 = _ple_layer_fwd(x5, p, W["ple_norm"][1:2], W, 1)

    sq, dx6, G["final_norm"] = _final_loss(x6, W["final_norm"], target, "final_loss")

    dx5, dpn1, dwg1, dwp1 = _ple_layer_bwd(dx6, ple1, p, W["ple_norm"][1:2], W, 1)
    dx4, dfn1, dwu1, dfc1, dwd1 = _ffn_bwd(dx5, ffn1, W["ffn_norm"][1:2], W, W["ffn_conv"][1], 1)
    dog = _mm(dx4, W["w_out_o"], "nt", "d_og", tn=1024)
    G["w_out_o"] = _mm(og, dx4, "tn", "dw_out_o", tm=1024, tn=1024, tk=512).reshape(N_CHIPS, GATE_SHARD, D_MODEL)
    do, dz, dgn = _gdn_post_bwd(o, proj_o, W["gdn_norm_o"], dog)
    G["gdn_norm_o"] = jnp.sum(dgn, axis=0)
    dxs = _gdn_scan_bwd(xs, states, do)
    dqkv_act, db, da, dal, ddt = _gdn_prep_bwd(qkv, b_h, a_h, alog, dtb, dxs)
    G["a_log_o"] = dal.reshape(1, H)
    G["dt_bias_o"] = ddt.reshape(1, H)
    dqkv, G["conv_qkv_o"] = _gdn_pre_bwd(proj_o, W["conv_qkv_o"], dqkv_act)
    dba = jnp.concatenate([db.reshape(H, T).T, da.reshape(H, T).T,
                           jnp.zeros((T, ODD_IN_PAD - ODD_IN + 0), F32)], axis=1).astype(BF16)
    dproj_o = jnp.concatenate([dqkv, dz, dba], axis=1)
    G["w_in_o"] = _mm(hn_o, dproj_o, "tn", "dw_in_o", tm=1024, tn=384, tk=512)
    dhn_o = _mm(dproj_o, W["w_in_o"], "nt", "d_hn_o", tn=1024, tk=1408)
    dx3, G["mix_norm_o"] = _rms_bwd(x3, W["mix_norm_o"], dhn_o, dx4, "d_mix_norm_o")

    dx2, dpn0, dwg0, dwp0 = _ple_layer_bwd(dx3, ple0, p, W["ple_norm"][0:1], W, 0)
    dx1, dfn0, dwu0, dfc0, dwd0 = _ffn_bwd(dx2, ffn0, W["ffn_norm"][0:1], W, W["ffn_conv"][0], 0)
    dmix = _mm(dx1, W["w_out_e"], "nt", "d_mix_e", tn=1024)
    G["w_out_e"] = _mm(mix_e, dx1, "tn", "dw_out_e", tm=1024, tn=1024, tk=512).reshape(N_CHIPS, GATE_SHARD, D_MODEL)
    du, G["pool_w"], G["pool_scale"] = _pool_bwd(proj_e, W["pool_w"], W["pool_scale"], dmix)
    dattn = dmix[:, POOL_WIDTH:].reshape(T, SB_HEADS, SB_HEAD_DIM).transpose(1, 0, 2)
    dqa, dka, dva = _sb_bwd(q, k, v, dattn, ltot)
    dproj_e = jnp.concatenate([du, _heads_out(dqa), _heads_out(dka), _heads_out(dva)], axis=1).astype(BF16)
    G["w_in_e"] = _mm(hn_e, dproj_e, "tn", "dw_in_e", dims=(D_MODEL, 2 * D_MODEL, T), o_view=_cols_of(1),
                      out_shape=(N_CHIPS, D_MODEL, IN_E_SHARD), tm=1024, tn=IN_E_SHARD, tk=512)
    dhn_e = _mm(dproj_e, W["w_in_e"], "nt", "d_hn_e", dims=(T, D_MODEL, 2 * D_MODEL), b_view=_cols_of(1),
                tn=1024, tk=IN_E_SHARD)
    grad_x, G["mix_norm_e"] = _rms_bwd(x, W["mix_norm_e"], dhn_e, dx1, "d_mix_norm_e")

    G["ffn_norm"] = jnp.concatenate([dfn0, dfn1], axis=0)
    G["ple_norm"] = jnp.concatenate([dpn0, dpn1], axis=0)
    G["ffn_conv"] = jnp.stack([dfc0, dfc1])
    G["w_up"] = [dwu0, dwu1]
    G["w_down"] = [dwd0, dwd1]
    G["w_ple_gate"] = [dwg0, dwg1]
    G["w_ple"] = [dwp0, dwp1]
    return sq[0, 0], grad_x, G


BIG = ("w_in_e", "w_out_e", "w_in_o", "w_out_o", "w_up", "w_down", "w_ple_gate", "w_ple")
SHARDED_SMALL = (("mix_norm_o", 1), ("conv_qkv_o", 2), ("ffn_conv", 2))
REPLICATED = ("mix_norm_e", "pool_w", "pool_scale", "a_log_o", "dt_bias_o", "gdn_norm_o", "ffn_norm", "ple_norm",
              "final_norm")
WEIGHT_ORDER = ("mix_norm_e", "w_in_e", "pool_w", "pool_scale", "w_out_e", "mix_norm_o", "w_in_o", "conv_qkv_o",
                "a_log_o", "dt_bias_o", "gdn_norm_o", "w_out_o", "ffn_norm", "w_up", "ffn_conv", "w_down", "ple_norm",
                "w_ple_gate", "w_ple", "final_norm")
SMALL_W = LANE
SMALL_ROWS = 16


def _size(shape):
    n = 1
    for s in shape:
        n *= s
    return n


def _pack(arrs, width, granule):
    flat = jnp.concatenate([a.reshape(-1) for a in arrs])
    rows = -(-flat.shape[0] // width)
    rows = -(-rows // granule) * granule
    return jnp.pad(flat, (0, rows * width - flat.shape[0])).reshape(rows, width)


def _unpack(flat2d, shapes):
    flat = flat2d.reshape(-1)
    out, off = [], 0
    for s in shapes:
        out.append(flat[off:off + _size(s)].reshape(s))
        off += _size(s)
    return out


MESH_ID = pl.DeviceIdType.MESH
HBM_SPEC = pl.BlockSpec(memory_space=pltpu.HBM)


def _where_am_i():
    return lax.axis_index("x"), lax.axis_index("y"), lax.axis_index("c")


def _other_chips(x, y):
    return [(1 - x, y), (x, 1 - y), (1 - x, 1 - y)]


def _remote(src, dst, send_sems, recv_sems, k, to):
    return pltpu.make_async_remote_copy(src_ref=src, dst_ref=dst, send_sem=send_sems.at[k], recv_sem=recv_sems.at[k],
                                        device_id=to, device_id_type=MESH_ID)


def _chip_allgather(pack, name):
    R, Wd = pack.shape
    Rh = R // 2

    def body(src_ref, out_ref, send_sems, recv_sems, local_sem):
        x, y, c = _where_am_i()
        me, sib = (x, y, c), (x, y, 1 - c)
        chips = _other_chips(x, y)
        mine_rows = pl.ds(pl.multiple_of(c * Rh, SMALL_ROWS), Rh)
        sib_rows = pl.ds(pl.multiple_of((1 - c) * Rh, SMALL_ROWS), Rh)
        j_me = 2 * x + y
        local = pltpu.make_async_copy(src_ref, out_ref.at[j_me], local_sem)
        local.start()
        first = [_remote(src_ref.at[mine_rows], out_ref.at[j_me, mine_rows], send_sems, recv_sems, k, (cx, cy, c))
                 for k, (cx, cy) in enumerate(chips)]
        for cp in first:
            cp.start()
        passed = []
        for k, (cx, cy) in enumerate(chips):
            blk = out_ref.at[2 * cx + cy, mine_rows]
            _remote(blk, blk, send_sems, recv_sems, k, me).wait_recv()
            fw = _remote(blk, blk, send_sems, recv_sems, 3 + k, sib)
            fw.start()
            passed.append(fw)
        for k, (cx, cy) in enumerate(chips):
            blk = out_ref.at[2 * cx + cy, sib_rows]
            _remote(blk, blk, send_sems, recv_sems, 3 + k, me).wait_recv()
        for cp in first + passed:
            cp.wait_send()
        local.wait()

    return pl.pallas_call(
        body, name=name, in_specs=[HBM_SPEC], out_specs=HBM_SPEC,
        out_shape=jax.ShapeDtypeStruct((N_CHIPS, R, Wd), pack.dtype),
        scratch_shapes=[pltpu.SemaphoreType.DMA((6,)), pltpu.SemaphoreType.DMA((6,)), pltpu.SemaphoreType.DMA],
    )(pack)


def _chip_allgather_many(blocks, name):
    n = len(blocks)

    def body(*refs):
        srcs, outs = refs[:n], refs[n:2 * n]
        send_sems, recv_sems = refs[2 * n:]
        x, y, c = _where_am_i()
        me, sib = (x, y, c), (x, y, 1 - c)
        chips = _other_chips(x, y)
        j_me = 2 * x + y
        first = [_remote(srcs[p].at[c], outs[p].at[j_me, c], send_sems, recv_sems, 6 * p + k, (cx, cy, c))
                 for p in range(n) for k, (cx, cy) in enumerate(chips)]
        for cp in first:
            cp.start()
        passed = []
        for k, (cx, cy) in enumerate(chips):
            for p in range(n):
                blk = outs[p].at[2 * cx + cy, c]
                _remote(blk, blk, send_sems, recv_sems, 6 * p + k, me).wait_recv()
                fw = _remote(blk, blk, send_sems, recv_sems, 6 * p + 3 + k, sib)
                fw.start()
                passed.append(fw)
        for k, (cx, cy) in enumerate(chips):
            for p in range(n):
                blk = outs[p].at[2 * cx + cy, 1 - c]
                _remote(blk, blk, send_sems, recv_sems, 6 * p + 3 + k, me).wait_recv()
        for cp in first + passed:
            cp.wait_send()

    return pl.pallas_call(
        body, name=name, in_specs=[HBM_SPEC] * n, out_specs=[HBM_SPEC] * n,
        out_shape=[jax.ShapeDtypeStruct((N_CHIPS,) + b.shape, b.dtype) for b in blocks],
        scratch_shapes=[pltpu.SemaphoreType.DMA((6 * n,)), pltpu.SemaphoreType.DMA((6 * n,))],
    )(*blocks)


def _sibling_swap_many(pieces, name):
    n = len(pieces)

    def body(*refs):
        srcs, outs = refs[:n], refs[n:2 * n]
        send_sems, recv_sems = refs[2 * n:]
        x, y, c = _where_am_i()
        cps = [_remote(srcs[p].at[:, 1 - c], outs[p], send_sems, recv_sems, p, (x, y, 1 - c)) for p in range(n)]
        for cp in cps:
            cp.start()
        for cp in cps:
            cp.wait()

    return pl.pallas_call(
        body, name=name, in_specs=[HBM_SPEC] * n, out_specs=[HBM_SPEC] * n,
        out_shape=[jax.ShapeDtypeStruct((g.shape[0],) + g.shape[2:], g.dtype) for g in pieces],
        scratch_shapes=[pltpu.SemaphoreType.DMA((n,)), pltpu.SemaphoreType.DMA((n,))],
    )(*pieces)


def _chip_scatter_many(sums, name):
    n = len(sums)

    def body(*refs):
        srcs, outs = refs[:n], refs[n:2 * n]
        send_sems, recv_sems = refs[2 * n:]
        x, y, c = _where_am_i()
        cps = [_remote(srcs[p].at[2 * cx + cy], outs[p].at[k], send_sems, recv_sems, 3 * p + k, (cx, cy, c))
               for p in range(n) for k, (cx, cy) in enumerate(_other_chips(x, y))]
        for cp in cps:
            cp.start()
        for cp in cps:
            cp.wait()

    return pl.pallas_call(
        body, name=name, in_specs=[HBM_SPEC] * n, out_specs=[HBM_SPEC] * n,
        out_shape=[jax.ShapeDtypeStruct((N_CHIPS - 1,) + s.shape[1:], s.dtype) for s in sums],
        scratch_shapes=[pltpu.SemaphoreType.DMA((3 * n,)), pltpu.SemaphoreType.DMA((3 * n,))],
    )(*sums)


def _sibling_send_many(halves, name):
    n = len(halves)

    def body(*refs):
        srcs, outs = refs[:n], refs[n:2 * n]
        send_sems, recv_sems = refs[2 * n:]
        x, y, c = _where_am_i()
        cps = [_remote(srcs[p], outs[p], send_sems, recv_sems, p, (x, y, 1 - c)) for p in range(n)]
        for cp in cps:
            cp.start()
        for cp in cps:
            cp.wait()

    return pl.pallas_call(
        body, name=name, in_specs=[HBM_SPEC] * n, out_specs=[HBM_SPEC] * n,
        out_shape=[jax.ShapeDtypeStruct(h.shape, h.dtype) for h in halves],
        scratch_shapes=[pltpu.SemaphoreType.DMA((n,)), pltpu.SemaphoreType.DMA((n,))],
    )(*halves)


def _row_tile(rows, pref=512):
    best = 8
    for t in range(8, pref + 1, 8):
        if rows % t == 0:
            best = t
    return best


def _where_ids():
    x, y, c = _where_am_i()
    return jnp.stack([c, 2 * x + y]).astype(jnp.int32)


RS_ROWS = 256


def _chip_sums_bf16(G, A, ids, name):
    n, _, hr, cols = G.shape
    tr = _row_tile(hr, RS_ROWS)

    def body(ids_ref, g_ref, a_ref, o_ref):
        o_ref[...] = (g_ref[...] + a_ref[...]).astype(BF16)

    return pl.pallas_call(
        body, name=name,
        grid_spec=pltpu.PrefetchScalarGridSpec(
            num_scalar_prefetch=1, grid=(n, hr // tr),
            in_specs=[pl.BlockSpec((None, None, tr, cols), lambda j, i, ids: (j, ids[0], i, 0)),
                      pl.BlockSpec((None, tr, cols), lambda j, i, ids: (j, i, 0))],
            out_specs=pl.BlockSpec((None, tr, cols), lambda j, i, ids: (j, i, 0))),
        out_shape=jax.ShapeDtypeStruct((n, hr, cols), BF16),
        compiler_params=_cp(("parallel", "parallel")),
    )(ids, G, A)


def _total_half(G, A, B, ids, name):
    _, _, hr, cols = G.shape
    tr = _row_tile(hr, RS_ROWS)

    def body(ids_ref, g_ref, a_ref, b_ref, o_ref):
        s = g_ref[...] + a_ref[...]
        for k in range(N_CHIPS - 1):
            s = s + b_ref[k].astype(F32)
        o_ref[...] = s

    return pl.pallas_call(
        body, name=name,
        grid_spec=pltpu.PrefetchScalarGridSpec(
            num_scalar_prefetch=1, grid=(hr // tr,),
            in_specs=[pl.BlockSpec((None, None, tr, cols), lambda i, ids: (ids[1], ids[0], i, 0)),
                      pl.BlockSpec((None, tr, cols), lambda i, ids: (ids[1], i, 0)),
                      pl.BlockSpec((N_CHIPS - 1, tr, cols), lambda i, ids: (0, i, 0))],
            out_specs=pl.BlockSpec((tr, cols), lambda i, ids: (i, 0))),
        out_shape=jax.ShapeDtypeStruct((hr, cols), F32),
        compiler_params=_cp(("parallel",)),
    )(ids, G, A, B)


def _small_allreduce(v, name):
    R, Wd = v.shape

    def body(x_ref, sum_ref, all_ref, send_sems, recv_sems, local_sem):
        x, y, c = _where_am_i()
        me, sib = (x, y, c), (x, y, 1 - c)
        chips = _other_chips(x, y)

        def slot(px, py, pc):
            return all_ref.at[4 * px + 2 * py + pc]

        local = pltpu.make_async_copy(x_ref, slot(*me), local_sem)
        local.start()
        first = [_remote(x_ref, slot(*me), send_sems, recv_sems, 0, sib)]
        first += [_remote(x_ref, slot(*me), send_sems, recv_sems, 1 + k, (cx, cy, c)) for k, (cx, cy) in enumerate(chips)]
        for cp in first:
            cp.start()
        passed = []
        for k, (cx, cy) in enumerate(chips):
            blk = slot(cx, cy, c)
            _remote(blk, blk, send_sems, recv_sems, 1 + k, me).wait_recv()
            fw = _remote(blk, blk, send_sems, recv_sems, 4 + k, sib)
            fw.start()
            passed.append(fw)
        _remote(slot(*sib), slot(*sib), send_sems, recv_sems, 0, me).wait_recv()
        for k, (cx, cy) in enumerate(chips):
            blk = slot(cx, cy, 1 - c)
            _remote(blk, blk, send_sems, recv_sems, 4 + k, me).wait_recv()
        for cp in first + passed:
            cp.wait_send()
        local.wait()
        s = all_ref[0]
        for d in range(1, N_DEV):
            s = s + all_ref[d]
        sum_ref[...] = s

    vm = pl.BlockSpec(memory_space=pltpu.VMEM)
    return pl.pallas_call(
        body, name=name, in_specs=[vm], out_specs=[vm, vm],
        out_shape=[jax.ShapeDtypeStruct((R, Wd), F32), jax.ShapeDtypeStruct((N_DEV, R, Wd), F32)],
        scratch_shapes=[pltpu.SemaphoreType.DMA((7,)), pltpu.SemaphoreType.DMA((7,)), pltpu.SemaphoreType.DMA],
    )(v)[0]


def _adamw(w, g, m, v, name):
    L, R, Wd = w.shape
    tr = _row_tile(R, RS_ROWS)
    c1 = 1.0 - ADAM_B1 ** ADAM_STEP
    c2 = 1.0 - ADAM_B2 ** ADAM_STEP

    def body(w_ref, g_ref, m_ref, v_ref, d_ref, nm_ref, nv_ref):
        gv = g_ref[...]
        nm = ADAM_B1 * m_ref[...] + (1.0 - ADAM_B1) * gv
        nv = ADAM_B2 * v_ref[...] + (1.0 - ADAM_B2) * (gv * gv)
        d_ref[...] = -ADAM_LR * ((nm / c1) / (jnp.sqrt(nv / c2) + ADAM_EPS) + ADAM_WD * w_ref[...])
        nm_ref[...] = nm
        nv_ref[...] = nv

    row = pl.BlockSpec((None, tr, Wd), lambda l, i: (l, i, 0))
    shp = jax.ShapeDtypeStruct((L, R, Wd), F32)
    return pl.pallas_call(
        body, name=name, grid=(L, R // tr), in_specs=[row] * 4, out_specs=[row] * 3, out_shape=[shp] * 3,
        compiler_params=_cp(("parallel", "parallel")),
    )(w, g, m, v)


def _adamw_halves(w, m, v, mine, theirs, ids, name):
    L, R, Wd = w.shape
    hr = R // 2
    tr = _row_tile(hr, RS_ROWS)
    c1 = 1.0 - ADAM_B1 ** ADAM_STEP
    c2 = 1.0 - ADAM_B2 ** ADAM_STEP

    def body(ids_ref, w_ref, m_ref, v_ref, *refs):
        g_refs, (g_ref, d_ref, nm_ref, nv_ref) = refs[:2 * L], refs[2 * L:]
        layer, half = pl.program_id(0), pl.program_id(1)
        own = half == ids_ref[0]
        gv = jnp.where(own, g_refs[0][...], g_refs[L][...])
        for l in range(1, L):
            gv = jnp.where(layer == l, jnp.where(own, g_refs[l][...], g_refs[L + l][...]), gv)
        nm = ADAM_B1 * m_ref[...] + (1.0 - ADAM_B1) * gv
        nv = ADAM_B2 * v_ref[...] + (1.0 - ADAM_B2) * (gv * gv)
        g_ref[...] = gv
        d_ref[...] = -ADAM_LR * ((nm / c1) / (jnp.sqrt(nv / c2) + ADAM_EPS) + ADAM_WD * w_ref[...])
        nm_ref[...] = nm
        nv_ref[...] = nv

    blk = pl.BlockSpec((None, None, tr, Wd), lambda l, h, i, ids: (l, h, i, 0))
    g_blk = pl.BlockSpec((tr, Wd), lambda l, h, i, ids: (i, 0))
    shp = jax.ShapeDtypeStruct((L, 2, hr, Wd), F32)
    outs = pl.pallas_call(
        body, name=name,
        grid_spec=pltpu.PrefetchScalarGridSpec(
            num_scalar_prefetch=1, grid=(L, 2, hr // tr),
            in_specs=[blk] * 3 + [g_blk] * (2 * L), out_specs=[blk] * 4),
        out_shape=[shp] * 4,
        compiler_params=_cp(("parallel", "parallel", "parallel")),
    )(ids, *[a.reshape(L, 2, hr, Wd) for a in (w, m, v)], *mine, *theirs)
    return tuple(o.reshape(L, R, Wd) for o in outs)


def _two_halves(a):
    cols = a.shape[-1]
    return a.reshape(2, _size(a.shape) // (2 * cols), cols)


def _gather_weights(P):
    mine = [_two_halves(P[n].astype(BF16)) for n in BIG]
    chip = 2 * lax.axis_index("x") + lax.axis_index("y")
    gathered = {n: lax.dynamic_update_slice_in_dim(g, own[None], chip, axis=0)
                for n, g, own in zip(BIG, _chip_allgather_many(mine, "ag_weights"), mine)}
    small_shapes = [P[n].shape for n, _ in SHARDED_SMALL]
    small = _chip_allgather(_pack([P[n] for n, _ in SHARDED_SMALL], SMALL_W, SMALL_ROWS), "ag_small")
    parts = [_unpack(small[j], small_shapes) for j in range(N_CHIPS)]
    full = {n: jnp.concatenate([parts[j][i] for j in range(N_CHIPS)], axis=ax)
            for i, (n, ax) in enumerate(SHARDED_SMALL)}
    W = {n: P[n] for n in REPLICATED}
    W["pool_w"] = P["pool_w"][0]
    W["final_norm"] = P["final_norm"].reshape(1, D_MODEL)
    W["mix_norm_o"] = full["mix_norm_o"]
    W["conv_qkv_o"] = full["conv_qkv_o"][0]
    W["ffn_conv"] = full["ffn_conv"]
    W["w_in_e"] = gathered["w_in_e"].reshape(N_CHIPS, D_MODEL, IN_E_SHARD)
    W["w_out_e"] = gathered["w_out_e"].reshape(D_MODEL, D_MODEL)
    W["w_out_o"] = gathered["w_out_o"].reshape(D_MODEL, D_MODEL)
    w_in_o = gathered["w_in_o"].reshape(N_CHIPS, D_MODEL, ODD_IN // N_CHIPS)
    W["w_in_o"] = jnp.pad(jnp.concatenate([w_in_o[j] for j in range(N_CHIPS)], axis=1),
                          ((0, 0), (0, ODD_IN_PAD - ODD_IN)))
    W["w_up"] = gathered["w_up"]
    W["w_down"] = gathered["w_down"].transpose(1, 0, 2, 3).reshape(2, FFN_DIM, D_MODEL)
    W["w_ple_gate"] = gathered["w_ple_gate"]
    W["w_ple"] = gathered["w_ple"]
    return W


def _reduce_big_gradients(G):
    w_in_o = G["w_in_o"]
    shard = ODD_IN // N_CHIPS
    pieces, layers_of = [], []
    for n in BIG:
        if n == "w_in_o":
            gs = [jnp.stack([w_in_o[:, j * shard:(j + 1) * shard] for j in range(N_CHIPS)])]
        else:
            gs = G[n] if isinstance(G[n], list) else [G[n]]
        layers_of.append(list(range(len(pieces), len(pieces) + len(gs))))
        pieces += [g.reshape(N_CHIPS, 2, g.shape[1] // 2, g.shape[2]) for g in gs]
    ids = _where_ids()
    from_sibling = _sibling_swap_many(pieces, "rs_sibling_swap")
    sums = [_chip_sums_bf16(g, a, ids, f"rs_chip_sums{i}") for i, (g, a) in enumerate(zip(pieces, from_sibling))]
    from_chips = _chip_scatter_many(sums, "rs_chip_scatter")
    halves = [_total_half(g, a, b, ids, f"rs_total{i}")
              for i, (g, a, b) in enumerate(zip(pieces, from_sibling, from_chips))]
    from_sibling_total = _sibling_send_many(halves, "rs_sibling_send")
    return {n: ([halves[p] for p in ps], [from_sibling_total[p] for p in ps]) for n, ps in zip(BIG, layers_of)}, ids


def kernel(x, p, mix_norm_e, w_in_e, pool_w, pool_scale, w_out_e, mix_norm_o, w_in_o, conv_qkv_o, a_log_o, dt_bias_o, gdn_norm_o, w_out_o, ffn_norm, w_up, ffn_conv, w_down, ple_norm, w_ple_gate, w_ple, final_norm, loss_target, m_mix_norm_e, m_w_in_e, m_pool_w, m_pool_scale, m_w_out_e, m_mix_norm_o, m_w_in_o, m_conv_qkv_o, m_a_log_o, m_dt_bias_o, m_gdn_norm_o, m_w_out_o, m_ffn_norm, m_w_up, m_ffn_conv, m_w_down, m_ple_norm, m_w_ple_gate, m_w_ple, m_final_norm, v_mix_norm_e, v_w_in_e, v_pool_w, v_pool_scale, v_w_out_e, v_mix_norm_o, v_w_in_o, v_conv_qkv_o, v_a_log_o, v_dt_bias_o, v_gdn_norm_o, v_w_out_o, v_ffn_norm, v_w_up, v_ffn_conv, v_w_down, v_ple_norm, v_w_ple_gate, v_w_ple, v_final_norm):
    args = locals()
    P = {n: args[n] for n in WEIGHT_ORDER}
    M = {n: args["m_" + n] for n in WEIGHT_ORDER}
    V = {n: args["v_" + n] for n in WEIGHT_ORDER}

    W = _gather_weights(P)
    T = x.shape[1]
    sq, grad_x, G = _local_step(x.reshape(T, D_MODEL), p.reshape(2, T, PLE_DIM), loss_target.reshape(T, D_MODEL), W)
    out = {}
    reduced, ids = _reduce_big_gradients(G)
    for n, (mine, theirs) in reduced.items():
        out[n] = _adamw_halves(P[n], M[n], V[n], mine, theirs, ids, f"adamw_{n}")

    small_full = {n: G[n] for n in REPLICATED}
    small_full["pool_w"] = G["pool_w"][None]
    small_full["final_norm"] = G["final_norm"].reshape(D_MODEL)
    small_full["mix_norm_o"] = G["mix_norm_o"]
    small_full["conv_qkv_o"] = G["conv_qkv_o"][None]
    small_full["ffn_conv"] = G["ffn_conv"]
    small_names = REPLICATED + tuple(n for n, _ in SHARDED_SMALL)
    summed = _small_allreduce(_pack([small_full[n] for n in small_names], SMALL_W, 8), "ar_small")
    g_small = dict(zip(small_names, _unpack(summed, [small_full[n].shape for n in small_names])))
    chip = 2 * lax.axis_index("x") + lax.axis_index("y")
    for n, ax in SHARDED_SMALL:
        width = P[n].shape[ax]
        g_small[n] = lax.dynamic_slice_in_dim(g_small[n], chip * width, width, axis=ax)

    def pack_small(D):
        return _pack([D[n] for n in small_names], SMALL_W, RS_ROWS)[None]

    g_pack = pack_small(g_small)
    upd = _adamw(pack_small(P), g_pack, pack_small(M), pack_small(V), "adamw_small")
    shapes = [P[n].shape for n in small_names]
    for n, *vals in zip(small_names, *[_unpack(a[0], shapes) for a in (g_pack,) + tuple(upd)]):
        out[n] = tuple(vals)

    loss = (0.5 / D_MODEL) * lax.psum(sq, ("x", "y", "c"))
    return (loss, grad_x[None]) + tuple(out[n][i] for i in range(4) for n in WEIGHT_ORDER)
```

```python
import functools

import jax
import jax.numpy as jnp
from jax import lax
from jax.experimental import pallas as pl
from jax.experimental.pallas import tpu as pltpu

F32 = jnp.float32
BF16 = jnp.bfloat16

D_MODEL = 1024
PLE_DIM = 256
POOL_WIDTH = 512
POOL_WINDOWS = (2, 4, 8, 16)
POOL_GROUP_DIM = 128
SB_HEADS = 8
SB_HEAD_DIM = 64
GDN_HEADS = 8
GDN_HEAD_DIM = 128
GDN_CONV = 4
GDN_CHUNK = 64
FFN_DIM = 2816
FFN_CONV = 3
EPS = 1e-6
ODD_IN = 4 * D_MODEL + 2 * GDN_HEADS
ODD_IN_PAD = 33 * 128
ADAM_LR, ADAM_B1, ADAM_B2, ADAM_EPS, ADAM_WD, ADAM_STEP = 0.001, 0.9, 0.999, 1e-08, 0.01, 10

LANE = 128
VMEM_LIMIT = 56 * 1024 * 1024

N_CHIPS = 4
N_DEV = 8


def _cp(sem=None):
    return pltpu.CompilerParams(dimension_semantics=sem, vmem_limit_bytes=VMEM_LIMIT)


def _tile(n, pref):
    if n <= pref:
        return n
    best = None
    for t in range(LANE, pref + 1, LANE):
        if n % t == 0:
            best = t
    assert best is not None, (n, pref)
    return best


_DIMS = {"nn": (((1,), (0,)), ((), ())), "nt": (((1,), (1,)), ((), ())), "tn": (((0,), (0,)), ((), ()))}
_BDIMS = {"nn": (((2,), (1,)), ((0,), (0,))), "nt": (((2,), (2,)), ((0,), (0,))), "tn": (((1,), (1,)), ((0,), (0,)))}


def _dims(mode, ndim):
    return (_BDIMS if ndim == 3 else _DIMS)[mode]


def _dot(a, b, mode="nn"):
    return lax.dot_general(a.astype(BF16), b.astype(BF16), _dims(mode, a.ndim), preferred_element_type=F32)


def _bdot(a, b, mode="nn"):
    return lax.dot_general(a.astype(BF16), b.astype(BF16), _BDIMS[mode], preferred_element_type=F32)


def _split2(x):
    hi = x.astype(BF16)
    lo = (x - hi.astype(F32)).astype(BF16)
    return hi, lo


def _split3(x):
    hi = x.astype(BF16)
    r = x - hi.astype(F32)
    mid = r.astype(BF16)
    lo = (r - mid.astype(F32)).astype(BF16)
    return hi, mid, lo


def _dot_x01(x, m01, mode="nn"):
    hi, lo = _split2(x)
    return (lax.dot_general(hi, m01, _DIMS[mode], preferred_element_type=F32)
            + lax.dot_general(lo, m01, _DIMS[mode], preferred_element_type=F32))


def _dot3_raw(a, b, mode):
    ah, al = _split2(a)
    bh, bl = _split2(b)
    d = _dims(mode, a.ndim)
    return (lax.dot_general(ah, bh, d, preferred_element_type=F32)
            + lax.dot_general(ah, bl, d, preferred_element_type=F32)
            + lax.dot_general(al, bh, d, preferred_element_type=F32))


@jax.custom_vjp
def _dot3(a, b):
    return _dot3_raw(a, b, "nn")


def _dot3_fwd(a, b):
    return _dot3_raw(a, b, "nn"), (a, b)


def _dot3_bwd(res, g):
    a, b = res
    return _dot(g, b, "nt"), _dot(a, g, "tn")


_dot3.defvjp(_dot3_fwd, _dot3_bwd)


@jax.custom_vjp
def _dot1_nt(a, b):
    return _dot(a, b, "nt")


def _dot1_nt_fwd(a, b):
    return _dot(a, b, "nt"), (a, b)


def _dot1_nt_bwd(res, g):
    a, b = res
    return _dot(g, b, "nn"), _dot(g, a, "tn")


_dot1_nt.defvjp(_dot1_nt_fwd, _dot1_nt_bwd)


def _m01_left_raw(m, x):
    d = _dims("nn", x.ndim)
    if x.ndim == 3:
        m = jnp.broadcast_to(m, (x.shape[0],) + m.shape)
    p0, p1, p2 = _split3(x)
    return (lax.dot_general(m, p0, d, preferred_element_type=F32)
            + lax.dot_general(m, p1, d, preferred_element_type=F32)
            + lax.dot_general(m, p2, d, preferred_element_type=F32))


@jax.custom_vjp
def _m01_left(m, mt, x):
    return _m01_left_raw(m, x)


def _m01_left_fwd(m, mt, x):
    return _m01_left_raw(m, x), (m, mt)


def _m01_left_bwd(res, g):
    m, mt = res
    return jnp.zeros_like(m), jnp.zeros_like(mt), _m01_left_raw(mt, g)


_m01_left.defvjp(_m01_left_fwd, _m01_left_bwd)


def _softplus(x):
    return jnp.maximum(x, 0.0) + jnp.log(1.0 + jnp.exp(-jnp.abs(x)))


def _sigmoid(x):
    return 1.0 / (1.0 + jnp.exp(-x))


def _silu(x):
    return x * _sigmoid(x)


def _dsilu(x):
    s = _sigmoid(x)
    return s * (1.0 + x * (1.0 - s))


def _cols_of(n_blocks_per_part, *fixed):
    return lambda r, c: (c // n_blocks_per_part,) + fixed + (r, c % n_blocks_per_part)


def _rows_of(n_blocks_per_part, *fixed):
    return lambda r, c: (r // n_blocks_per_part,) + fixed + (r % n_blocks_per_part, c)


def _layer_of(layer):
    return lambda r, c: (layer, r, c)


def _mm(a, b, mode, name, out_dtype=F32, res=None, tm=512, tn=512, tk=1024,
        dims=None, a_view=None, b_view=None, o_view=None, out_shape=None):
    if dims is None:
        if mode == "nn":
            (M, K), (K2, N) = a.shape, b.shape
        elif mode == "nt":
            (M, K), (N, K2) = a.shape, b.shape
        else:
            (K, M), (K2, N) = a.shape, b.shape
        assert K == K2, (name, a.shape, b.shape)
    else:
        M, N, K = dims
    tm, tn, tk = _tile(M, tm), _tile(N, tn), _tile(K, tk)
    nk = K // tk

    def spec(arr, blk, view, rc):
        view = view or (lambda r, c: (r, c))
        return pl.BlockSpec((None,) * (arr.ndim - 2) + blk, lambda i, j, k: view(*rc(i, j, k)))

    if mode == "tn":
        a_spec = spec(a, (tk, tm), a_view, lambda i, j, k: (k, i))
    else:
        a_spec = spec(a, (tm, tk), a_view, lambda i, j, k: (i, k))
    if mode == "nt":
        b_spec = spec(b, (tn, tk), b_view, lambda i, j, k: (j, k))
    else:
        b_spec = spec(b, (tk, tn), b_view, lambda i, j, k: (k, j))
    out_shape = out_shape or (M, N)
    o_spec = pl.BlockSpec((None,) * (len(out_shape) - 2) + (tm, tn),
                          lambda i, j, k: (o_view or (lambda r, c: (r, c)))(i, j))
    has_res = res is not None
    assert not (has_res and o_view), name

    def body(*refs):
        if has_res:
            a_ref, b_ref, r_ref, o_ref, acc = refs
        else:
            a_ref, b_ref, o_ref, acc = refs
        k = pl.program_id(2)

        @pl.when(k == 0)
        def _():
            acc[...] = jnp.zeros_like(acc)

        acc[...] += _dot(a_ref[...], b_ref[...], mode)

        @pl.when(k == nk - 1)
        def _():
            r = acc[...]
            if has_res:
                r = r + r_ref[...]
            o_ref[...] = r.astype(out_dtype)

    ins = [a, b] + ([res] if has_res else [])
    in_specs = [a_spec, b_spec] + ([o_spec] if has_res else [])
    return pl.pallas_call(
        body, name=name, grid=(M // tm, N // tn, nk),
        in_specs=in_specs, out_specs=o_spec,
        out_shape=jax.ShapeDtypeStruct(out_shape, out_dtype),
        scratch_shapes=[pltpu.VMEM((tm, tn), F32)],
        compiler_params=_cp(("parallel", "parallel", "arbitrary")),
    )(*ins)


def _rms_fwd(x, gain, name):
    T, D = x.shape
    tt = _tile(T, 512)

    def body(x_ref, g_ref, o_ref):
        xv = x_ref[...]
        r = lax.rsqrt(jnp.mean(xv * xv, axis=-1, keepdims=True) + EPS)
        o_ref[...] = (xv * r * g_ref[...]).astype(BF16)

    return pl.pallas_call(
        body, name=name, grid=(T // tt,),
        in_specs=[pl.BlockSpec((tt, D), lambda i: (i, 0)), pl.BlockSpec((1, D), lambda i: (0, 0))],
        out_specs=pl.BlockSpec((tt, D), lambda i: (i, 0)),
        out_shape=jax.ShapeDtypeStruct((T, D), BF16),
        compiler_params=_cp(("parallel",)),
    )(x, gain)


def _rms_bwd(x, gain, dh, dres, name):
    T, D = x.shape
    tt = _tile(T, 512)

    def body(x_ref, g_ref, dh_ref, dr_ref, dx_ref, dg_ref):
        i = pl.program_id(0)
        xv = x_ref[...]
        dy = dh_ref[...].astype(F32)
        r = lax.rsqrt(jnp.mean(xv * xv, axis=-1, keepdims=True) + EPS)
        xn = xv * r
        gdy = dy * g_ref[...]
        dx = r * (gdy - xn * jnp.mean(gdy * xn, axis=-1, keepdims=True))
        dx_ref[...] = dr_ref[...] + dx

        @pl.when(i == 0)
        def _():
            dg_ref[...] = jnp.zeros_like(dg_ref)

        dg_ref[...] += jnp.sum(dy * xn, axis=0, keepdims=True)

    row = pl.BlockSpec((tt, D), lambda i: (i, 0))
    vec = pl.BlockSpec((1, D), lambda i: (0, 0))
    return pl.pallas_call(
        body, name=name, grid=(T // tt,),
        in_specs=[row, vec, row, row], out_specs=[row, vec],
        out_shape=[jax.ShapeDtypeStruct((T, D), F32), jax.ShapeDtypeStruct((1, D), F32)],
        compiler_params=_cp(("arbitrary",)),
    )(x, gain, dh, dres)


def _final_loss(x, gain, target, name):
    T, D = x.shape
    tt = _tile(T, 512)

    def body(x_ref, g_ref, t_ref, l_ref, dx_ref, dg_ref):
        i = pl.program_id(0)
        xv = x_ref[...]
        r = lax.rsqrt(jnp.mean(xv * xv, axis=-1, keepdims=True) + EPS)
        xn = xv * r
        err = xn * g_ref[...] - t_ref[...]
        dy = err * (1.0 / D)
        gdy = dy * g_ref[...]
        dx_ref[...] = r * (gdy - xn * jnp.mean(gdy * xn, axis=-1, keepdims=True))

        @pl.when(i == 0)
        def _():
            dg_ref[...] = jnp.zeros_like(dg_ref)
            l_ref[...] = jnp.zeros_like(l_ref)

        dg_ref[...] += jnp.sum(dy * xn, axis=0, keepdims=True)
        l_ref[...] += jnp.sum(jnp.sum(err * err, axis=1, keepdims=True), axis=0, keepdims=True)

    row = pl.BlockSpec((tt, D), lambda i: (i, 0))
    vec = pl.BlockSpec((1, D), lambda i: (0, 0))
    return pl.pallas_call(
        body, name=name, grid=(T // tt,),
        in_specs=[row, vec, row],
        out_specs=[pl.BlockSpec((8, LANE), lambda i: (0, 0)), row, vec],
        out_shape=[jax.ShapeDtypeStruct((8, LANE), F32), jax.ShapeDtypeStruct((T, D), F32),
                   jax.ShapeDtypeStruct((1, D), F32)],
        compiler_params=_cp(("arbitrary",)),
    )(x, gain, target)


def _shift_down(x, i, t_idx):
    if i == 0:
        return x
    return jnp.where(t_idx >= i, pltpu.roll(x, i, 0), 0.0)


def _shift_up(x, i, t_idx):
    if i == 0:
        return x
    n = x.shape[0]
    return jnp.where(t_idx < n - i, pltpu.roll(x, n - i, 0), 0.0)


def _pool_select(g, vals):
    out = vals[-1]
    for gi in range(len(vals) - 2, -1, -1):
        out = jnp.where(g == gi, vals[gi], out)
    return out


def _pool_y(u, g, t_idx):
    s1 = u + _shift_down(u, 1, t_idx)
    s2 = s1 + _shift_down(s1, 2, t_idx)
    s3 = s2 + _shift_down(s2, 4, t_idx)
    s4 = s3 + _shift_down(s3, 8, t_idx)
    ws = _pool_select(g, [s1, s2, s3, s4])
    win = _pool_select(g, [jnp.float32(w) for w in POOL_WINDOWS])
    cnt = jnp.minimum(t_idx.astype(F32) + 1.0, win)
    return ws / cnt - u, cnt


def _pool_fwd(proj, pool_w, pool_scale):
    T = proj.shape[0]
    G, C = len(POOL_WINDOWS), POOL_GROUP_DIM

    def body(u_ref, w_ref, s_ref, o_ref):
        g = pl.program_id(0)
        t_idx = lax.broadcasted_iota(jnp.int32, (T, C), 0)
        y, _ = _pool_y(u_ref[...], g, t_idx)
        o_ref[...] = _dot(y, w_ref[0]) * s_ref[...]

    return pl.pallas_call(
        body, name="pool_fwd", grid=(G,),
        in_specs=[pl.BlockSpec((T, C), lambda g: (0, g)), pl.BlockSpec((1, C, C), lambda g: (g, 0, 0)),
                  pl.BlockSpec((1, C), lambda g: (0, g))],
        out_specs=pl.BlockSpec((T, C), lambda g: (0, g)),
        out_shape=jax.ShapeDtypeStruct((T, G * C), F32),
        compiler_params=_cp(("parallel",)),
    )(proj, pool_w, pool_scale)


def _pool_bwd(proj, pool_w, pool_scale, dmix):
    T = proj.shape[0]
    G, C = len(POOL_WINDOWS), POOL_GROUP_DIM

    def body(u_ref, w_ref, s_ref, do_ref, du_ref, dw_ref, ds_ref):
        g = pl.program_id(0)
        t_idx = lax.broadcasted_iota(jnp.int32, (T, C), 0)
        y, cnt = _pool_y(u_ref[...], g, t_idx)
        w = w_ref[0]
        dout = do_ref[...]
        ds_ref[...] = jnp.sum(dout * _dot(y, w), axis=0, keepdims=True)
        dy2 = dout * s_ref[...]
        dw_ref[0] = _dot(y, dy2, "tn")
        dy = _dot(dy2, w, "nt")
        dz = dy / cnt
        r1 = dz + _shift_up(dz, 1, t_idx)
        r2 = r1 + _shift_up(r1, 2, t_idx)
        r3 = r2 + _shift_up(r2, 4, t_idx)
        r4 = r3 + _shift_up(r3, 8, t_idx)
        du_ref[...] = _pool_select(g, [r1, r2, r3, r4]) - dy

    col = pl.BlockSpec((T, C), lambda g: (0, g))
    return pl.pallas_call(
        body, name="pool_bwd", grid=(G,),
        in_specs=[col, pl.BlockSpec((1, C, C), lambda g: (g, 0, 0)), pl.BlockSpec((1, C), lambda g: (0, g)), col],
        out_specs=[col, pl.BlockSpec((1, C, C), lambda g: (g, 0, 0)), pl.BlockSpec((1, C), lambda g: (0, g))],
        out_shape=[jax.ShapeDtypeStruct((T, G * C), F32), jax.ShapeDtypeStruct((G, C, C), F32),
                   jax.ShapeDtypeStruct((1, G * C), F32)],
        compiler_params=_cp(("parallel",)),
    )(proj, pool_w, pool_scale, dmix)


SB_SCALE = SB_HEAD_DIM ** -0.5
SB_PASS_SIZES = (4, 2, 1)


def _sb_tile_logits(qb, kblk, valid):
    z = _dot(qb, kblk, "nt")
    sp = _softplus(z)
    l1m = -sp
    if valid is not None:
        l1m = jnp.where(valid, l1m, 0.0)
    return z, sp, l1m


def _sb_fwd(q, k, v):
    H, T, dh = q.shape
    B = _tile(T, 256)
    nq = T // B

    def body(q_ref, k_ref, v_ref, o_ref, l_ref):
        qi = pl.program_id(1)
        qb = q_ref[0]
        row = lax.broadcasted_iota(jnp.int32, (B, B), 0)
        col = lax.broadcasted_iota(jnp.int32, (B, B), 1)
        later = (row > col).astype(BF16)

        def tiles(kbs, carry, acc, valid):
            ksl = [pl.ds(pl.multiple_of(kb * B, B), B) for kb in kbs]
            logits = [_sb_tile_logits(qb, k_ref[0, ks, :], valid) for ks in ksl]
            within = [_dot_x01(l1m, later) for _, _, l1m in logits]
            sums = [jnp.sum(l1m, axis=1, keepdims=True) for _, _, l1m in logits]
            for (z, sp, _), rc, s, ks in zip(logits, within, sums, ksl):
                a = jnp.exp(z - sp + rc + carry)
                if valid is not None:
                    a = jnp.where(valid, a, 0.0)
                acc = acc + _dot(a, v_ref[0, ks, :])
                carry = carry + s
            return carry, acc

        state = tiles([qi], jnp.zeros((B, 1), F32), jnp.zeros((B, dh), F32), col < row)
        left = qi
        for size in SB_PASS_SIZES:
            n_pass = left // size
            state = lax.fori_loop(
                0, n_pass, lambda i, c, left=left, size=size: tiles([left - 1 - size * i - u for u in range(size)],
                                                                     c[0], c[1], None), state)
            left = left - n_pass * size
        carry, acc = state
        o_ref[0] = acc
        l_ref[0] = carry

    qspec = pl.BlockSpec((1, B, dh), lambda h, i: (h, i, 0))
    full = pl.BlockSpec((1, T, dh), lambda h, i: (h, 0, 0))
    return pl.pallas_call(
        body, name="sb_fwd", grid=(H, nq),
        in_specs=[qspec, full, full],
        out_specs=[qspec, pl.BlockSpec((1, B, 1), lambda h, i: (h, i, 0))],
        out_shape=[jax.ShapeDtypeStruct((H, T, dh), F32), jax.ShapeDtypeStruct((H, T, 1), F32)],
        compiler_params=_cp(("parallel", "parallel")),
    )(q, k, v)


def _sb_bwd(q, k, v, dout, ltot):
    H, T, dh = q.shape
    B = _tile(T, 256)
    nq = T // B

    def body(q_ref, k_ref, v_ref, do_ref, l_ref, dq_ref, dk_ref, dv_ref):
        qi = pl.program_id(1)

        @pl.when(qi == 0)
        def _():
            dk_ref[...] = jnp.zeros_like(dk_ref)
            dv_ref[...] = jnp.zeros_like(dv_ref)

        qb = q_ref[0]
        dob = do_ref[0].astype(BF16)
        ltot_q = l_ref[0]
        row = lax.broadcasted_iota(jnp.int32, (B, B), 0)
        col = lax.broadcasted_iota(jnp.int32, (B, B), 1)
        upto = (row <= col).astype(BF16)
        before = (row < col).astype(BF16)

        def tiles(kbs, P, E, dq, valid):
            ksl = [pl.ds(pl.multiple_of(kb * B, B), B) for kb in kbs]
            kblks = [k_ref[0, ks, :] for ks in ksl]
            logits = [_sb_tile_logits(qb, kblk, valid) for kblk in kblks]
            das = [_dot(dob, v_ref[0, ks, :], "nt") for ks in ksl]
            within = [_dot_x01(l1m, upto) for _, _, l1m in logits]
            avals, es = [], []
            for (z, sp, l1m), pc, da in zip(logits, within, das):
                a = jnp.exp(z - sp + (ltot_q - P - pc))
                if valid is not None:
                    a = jnp.where(valid, a, 0.0)
                avals.append(a)
                es.append(da * a)
                P = P + jnp.sum(l1m, axis=1, keepdims=True)
            e_within = [_dot_x01(e, before) for e in es]
            for (z, sp, _), e, ew, a, kblk, ks in zip(logits, es, e_within, avals, kblks, ksl):
                dz = e * jnp.exp(-sp) - jnp.exp(z - sp) * (ew + E)
                if valid is not None:
                    dz = jnp.where(valid, dz, 0.0)
                dzb = dz.astype(BF16)
                dq = dq + _dot(dzb, kblk)
                dk_ref[0, ks, :] += _dot(dzb, qb, "tn")
                dv_ref[0, ks, :] += _dot(a, dob, "tn")
                E = E + jnp.sum(e, axis=1, keepdims=True)
            return P, E, dq

        zeros1 = jnp.zeros((B, 1), F32)
        state = (zeros1, zeros1, jnp.zeros((B, dh), F32))
        done = 0
        for size in SB_PASS_SIZES:
            n_pass = (qi - done) // size
            state = lax.fori_loop(
                0, n_pass, lambda i, c, done=done, size=size: tiles([done + size * i + u for u in range(size)], *c, None),
                state)
            done = done + n_pass * size
        _, _, dq = tiles([qi], *state, col < row)
        dq_ref[0] = dq * SB_SCALE

    qspec = pl.BlockSpec((1, B, dh), lambda h, i: (h, i, 0))
    full = pl.BlockSpec((1, T, dh), lambda h, i: (h, 0, 0))
    shp = jax.ShapeDtypeStruct((H, T, dh), F32)
    return pl.pallas_call(
        body, name="sb_bwd", grid=(H, nq),
        in_specs=[qspec, full, full, qspec, pl.BlockSpec((1, B, 1), lambda h, i: (h, i, 0))],
        out_specs=[qspec, full, full],
        out_shape=[shp, shp, shp],
        compiler_params=_cp(("parallel", "arbitrary")),
    )(q, k, v, dout, ltot)


def _rows(w_ref, K):
    return [w_ref[i:i + 1, :] for i in range(K)]


def _conv(x, ws, t_idx):
    K = len(ws)
    y = ws[K - 1] * x
    for i in range(K - 1):
        y = y + ws[i] * _shift_down(x, K - 1 - i, t_idx)
    return y


def _conv_bwd(x, ws, dy, t_idx):
    K = len(ws)
    dx = ws[K - 1] * dy
    dws = []
    for i in range(K - 1):
        dx = dx + ws[i] * _shift_up(dy, K - 1 - i, t_idx)
        dws.append(jnp.sum(dy * _shift_down(x, K - 1 - i, t_idx), axis=0, keepdims=True))
    dws.append(jnp.sum(dy * x, axis=0, keepdims=True))
    return dx, dws


def _store_rows(ref, rows):
    for i, r in enumerate(rows):
        ref[i:i + 1, :] = r


def _ffn_act_fwd(up, conv_w, name):
    T = up.shape[0]
    F = FFN_DIM
    nb = F // LANE

    def body(g_ref, v_ref, wg_ref, wv_ref, o_ref):
        t_idx = lax.broadcasted_iota(jnp.int32, (T, LANE), 0)
        cg = _conv(g_ref[...], _rows(wg_ref, FFN_CONV), t_idx)
        cv = _conv(v_ref[...], _rows(wv_ref, FFN_CONV), t_idx)
        o_ref[...] = (_silu(cg) * cv).astype(BF16)

    return pl.pallas_call(
        body, name=name, grid=(nb,),
        in_specs=[pl.BlockSpec((T, LANE), lambda j: (0, j)), pl.BlockSpec((T, LANE), lambda j: (0, j + nb)),
                  pl.BlockSpec((FFN_CONV, LANE), lambda j: (0, j)),
                  pl.BlockSpec((FFN_CONV, LANE), lambda j: (0, j + nb))],
        out_specs=pl.BlockSpec((T, LANE), lambda j: (0, j)),
        out_shape=jax.ShapeDtypeStruct((T, F), BF16),
        compiler_params=_cp(("parallel",)),
    )(up, up, conv_w, conv_w)


def _ffn_act_bwd(up, conv_w, dact, name):
    T = up.shape[0]
    F = FFN_DIM
    nb = F // LANE

    def body(g_ref, v_ref, wg_ref, wv_ref, da_ref, dup_ref, dwg_ref, dwv_ref):
        t_idx = lax.broadcasted_iota(jnp.int32, (T, LANE), 0)
        xg, xv, wg, wv = g_ref[...], v_ref[...], _rows(wg_ref, FFN_CONV), _rows(wv_ref, FFN_CONV)
        cg = _conv(xg, wg, t_idx)
        cv = _conv(xv, wv, t_idx)
        da = da_ref[...].astype(F32)
        dxg, dwg = _conv_bwd(xg, wg, da * cv * _dsilu(cg), t_idx)
        dxv, dwv = _conv_bwd(xv, wv, da * _silu(cg), t_idx)
        dup_ref[0] = dxg.astype(BF16)
        dup_ref[1] = dxv.astype(BF16)
        _store_rows(dwg_ref, dwg)
        _store_rows(dwv_ref, dwv)

    col = pl.BlockSpec((T, LANE), lambda j: (0, j))
    wcol = pl.BlockSpec((FFN_CONV, LANE), lambda j: (0, j))
    return pl.pallas_call(
        body, name=name, grid=(nb,),
        in_specs=[col, pl.BlockSpec((T, LANE), lambda j: (0, j + nb)), wcol,
                  pl.BlockSpec((FFN_CONV, LANE), lambda j: (0, j + nb)), col],
        out_specs=[pl.BlockSpec((2, T, LANE), lambda j: (0, 0, j)), wcol, wcol],
        out_shape=[jax.ShapeDtypeStruct((2, T, F), BF16),
                   jax.ShapeDtypeStruct((FFN_CONV, F), F32), jax.ShapeDtypeStruct((FFN_CONV, F), F32)],
        compiler_params=_cp(("parallel",)),
    )(up, up, conv_w, conv_w, dact)


N_QK_BLOCKS = 2 * GDN_HEADS


def _gdn_pre_fwd(proj, conv_w):
    T = proj.shape[0]
    nb = 3 * GDN_HEADS

    def body(x_ref, w_ref, o_ref):
        j = pl.program_id(0)
        t_idx = lax.broadcasted_iota(jnp.int32, (T, LANE), 0)
        s = _silu(_conv(x_ref[...], _rows(w_ref, GDN_CONV), t_idx))
        rn = lax.rsqrt(jnp.sum(s * s, axis=-1, keepdims=True) + EPS)
        o_ref[...] = s * jnp.where(j < N_QK_BLOCKS, rn, 1.0)

    return pl.pallas_call(
        body, name="gdn_pre_fwd", grid=(nb,),
        in_specs=[pl.BlockSpec((T, LANE), lambda j: (0, j)), pl.BlockSpec((GDN_CONV, LANE), lambda j: (0, j))],
        out_specs=pl.BlockSpec((T, LANE), lambda j: (0, j)),
        out_shape=jax.ShapeDtypeStruct((T, nb * LANE), F32),
        compiler_params=_cp(("parallel",)),
    )(proj, conv_w)


def _gdn_pre_bwd(proj, conv_w, dout):
    T = proj.shape[0]
    nb = 3 * GDN_HEADS
    H = GDN_HEADS

    def body(x_ref, w_ref, do_ref, dx_ref, dw_ref):
        j = pl.program_id(0)
        t_idx = lax.broadcasted_iota(jnp.int32, (T, LANE), 0)
        x, w = x_ref[...], _rows(w_ref, GDN_CONV)
        c = _conv(x, w, t_idx)
        s = _silu(c)
        rn = lax.rsqrt(jnp.sum(s * s, axis=-1, keepdims=True) + EPS)
        do = do_ref[...]
        y = s * rn
        ds_normed = rn * (do - y * jnp.sum(do * y, axis=-1, keepdims=True))
        ds = jnp.where(j < N_QK_BLOCKS, ds_normed, do)
        dx, dw = _conv_bwd(x, w, ds * _dsilu(c), t_idx)
        dx_ref[...] = dx.astype(BF16)
        _store_rows(dw_ref, dw)

    col = pl.BlockSpec((T, LANE), lambda j: (0, j))
    wcol = pl.BlockSpec((GDN_CONV, LANE), lambda j: (0, j))
    return pl.pallas_call(
        body, name="gdn_pre_bwd", grid=(nb,),
        in_specs=[col, wcol, pl.BlockSpec((None, None, T, LANE), lambda j: (j // H, j % H, 0, 0))],
        out_specs=[col, wcol],
        out_shape=[jax.ShapeDtypeStruct((T, nb * LANE), BF16), jax.ShapeDtypeStruct((GDN_CONV, nb * LANE), F32)],
        compiler_params=_cp(("parallel",)),
    )(proj, conv_w, dout)


def _gdn_consts():
    C = GDN_CHUNK
    r = lax.broadcasted_iota(jnp.int32, (C, C), 0)
    c = lax.broadcasted_iota(jnp.int32, (C, C), 1)
    return dict(incl=r >= c, strict=r > c, eye=(r == c).astype(F32),
                low=(r >= c).astype(BF16), up=(r <= c).astype(BF16), ones=jnp.ones((C, C), BF16))


def _gdn_prep_chunk(q, k, v, b, a, alog, dtb, cs):
    n, C, dk = q.shape
    beta = _sigmoid(b)
    g = -jnp.exp(alog) * _softplus(a + dtb)
    g_sq = jnp.broadcast_to(g, (n, C, C))
    g_wide = jnp.broadcast_to(g, (n, C, dk))
    gc_i = _m01_left(cs["low"], cs["up"], g_sq)
    gc_j = _m01_left(cs["ones"], cs["ones"], g_sq * cs["up"].astype(F32))
    gc_wide = _m01_left(cs["low"], cs["up"], g_wide)
    gl_wide = _m01_left(cs["ones"], cs["ones"], g_wide)
    decay = jnp.where(cs["incl"], jnp.exp(jnp.where(cs["incl"], gc_i - gc_j, 0.0)), 0.0)
    egc = jnp.exp(gc_wide)
    qs = q * (dk ** -0.5)
    k_beta = k * beta
    a_mat = jnp.where(cs["strict"], _dot1_nt(k_beta, k) * decay, 0.0)
    inv = cs["eye"] - a_mat
    pw = _dot3(a_mat, a_mat)
    n_factors = C.bit_length() - 2
    for f in range(n_factors):
        inv = inv + _dot3(inv, pw)
        if f < n_factors - 1:
            pw = _dot3(pw, pw)
    u = _dot3(inv, v * beta)
    w = _dot3(inv, k_beta * egc)
    qk = _dot1_nt(qs, k) * decay
    q_dec = qs * egc
    k_dec = k * jnp.exp(gl_wide - gc_wide)
    g_last = jnp.exp(gl_wide)[:, 0:8, :]
    return qk, u, w, q_dec, k_dec, g_last


GDN_PREP_CHUNKS = 8


def _gdn_prep_specs(T):
    C, dk = GDN_CHUNK, GDN_HEAD_DIM
    npc = min(GDN_PREP_CHUNKS, T // C)
    tc = npc * C
    H = GDN_HEADS
    in_specs = [pl.BlockSpec((tc, dk), lambda h, i: (i, h)),
                pl.BlockSpec((tc, dk), lambda h, i: (i, H + h)),
                pl.BlockSpec((tc, dk), lambda h, i: (i, 2 * H + h)),
                pl.BlockSpec((1, tc, 1), lambda h, i: (h, i, 0)),
                pl.BlockSpec((1, tc, 1), lambda h, i: (h, i, 0)),
                pl.BlockSpec((1, 1, 1), lambda h, i: (h, 0, 0)),
                pl.BlockSpec((1, 1, 1), lambda h, i: (h, 0, 0))]
    xs_specs = [pl.BlockSpec((1, tc, C), lambda h, i: (h, i, 0)),
                pl.BlockSpec((1, tc, dk), lambda h, i: (h, i, 0)),
                pl.BlockSpec((1, tc, dk), lambda h, i: (h, i, 0)),
                pl.BlockSpec((1, tc, dk), lambda h, i: (h, i, 0)),
                pl.BlockSpec((1, tc, dk), lambda h, i: (h, i, 0)),
                pl.BlockSpec((1, npc * 8, dk), lambda h, i: (h, i, 0))]
    xs_shapes = [jax.ShapeDtypeStruct((H, T, C), F32)] + [jax.ShapeDtypeStruct((H, T, dk), F32)] * 4 + [
        jax.ShapeDtypeStruct((H, 8 * T // C, dk), F32)]
    return npc, tc, in_specs, xs_specs, xs_shapes


def _gdn_prep_fwd(qkv, b, a, alog, dtb):
    T = qkv.shape[0]
    C = GDN_CHUNK
    npc, tc, in_specs, xs_specs, xs_shapes = _gdn_prep_specs(T)

    def body(q_ref, k_ref, v_ref, b_ref, a_ref, al_ref, dt_ref, qk_ref, u_ref, w_ref, qd_ref, kd_ref, gl_ref):
        cs = _gdn_consts()

        def chunks(val):
            return val.reshape(npc, C, val.shape[-1])

        outs = _gdn_prep_chunk(chunks(q_ref[...]), chunks(k_ref[...]), chunks(v_ref[...]), chunks(b_ref[0]),
                               chunks(a_ref[0]), al_ref[0], dt_ref[0], cs)
        for ref, val in zip((qk_ref, u_ref, w_ref, qd_ref, kd_ref), outs[:5]):
            ref[0] = val.reshape(tc, val.shape[-1])
        gl_ref[0] = outs[5].reshape(npc * 8, outs[5].shape[-1])

    return pl.pallas_call(
        body, name="gdn_prep_fwd", grid=(GDN_HEADS, T // tc),
        in_specs=in_specs, out_specs=xs_specs, out_shape=xs_shapes,
        compiler_params=_cp(("parallel", "parallel")),
    )(qkv, qkv, qkv, b, a, alog, dtb)


def _gdn_prep_bwd(qkv, b, a, alog, dtb, dxs):
    T = qkv.shape[0]
    C, dk, H = GDN_CHUNK, GDN_HEAD_DIM, GDN_HEADS
    npc, tc, in_specs, xs_specs, _ = _gdn_prep_specs(T)

    def body(q_ref, k_ref, v_ref, b_ref, a_ref, al_ref, dt_ref, dqk_ref, du_ref, dw_ref, dqd_ref, dkd_ref, dgl_ref,
             dqkv_ref, db_ref, da_ref, dal_ref, ddt_ref):
        i = pl.program_id(1)
        cs = _gdn_consts()
        r8 = lax.broadcasted_iota(jnp.int32, (8, dk), 0)
        c8 = lax.broadcasted_iota(jnp.int32, (8, dk), 1)
        first = (r8 == 0) & (c8 == 0)

        @pl.when(i == 0)
        def _():
            dal_ref[...] = jnp.zeros_like(dal_ref)
            ddt_ref[...] = jnp.zeros_like(ddt_ref)

        def chunks(val):
            return val.reshape(npc, C, val.shape[-1])

        prim = (chunks(q_ref[...]), chunks(k_ref[...]), chunks(v_ref[...]), chunks(b_ref[0]), chunks(a_ref[0]),
                al_ref[0], dt_ref[0])
        _, vjp = jax.vjp(lambda *p: _gdn_prep_chunk(*p, cs), *prim)
        dgl = jnp.where(first, dgl_ref[0].reshape(npc, 8, dk), 0.0)
        cts = tuple(chunks(r[0]) for r in (dqk_ref, du_ref, dw_ref, dqd_ref, dkd_ref)) + (dgl,)
        dq, dkk, dv, db, da, dal, ddt = vjp(cts)
        for part, val in enumerate((dq, dkk, dv)):
            dqkv_ref[part, 0] = val.reshape(tc, dk)
        db_ref[0] = db.reshape(tc, 1)
        da_ref[0] = da.reshape(tc, 1)
        dal_ref[0] += dal
        ddt_ref[0] += ddt

    thin = pl.BlockSpec((1, tc, 1), lambda h, i: (h, i, 0))
    one = pl.BlockSpec((1, 1, 1), lambda h, i: (h, 0, 0))
    return pl.pallas_call(
        body, name="gdn_prep_bwd", grid=(H, T // tc),
        in_specs=in_specs + xs_specs,
        out_specs=[pl.BlockSpec((3, 1, tc, dk), lambda h, i: (0, h, i, 0)), thin, thin, one, one],
        out_shape=[jax.ShapeDtypeStruct((3, H, T, dk), F32)] + [jax.ShapeDtypeStruct((H, T, 1), F32)] * 2
        + [jax.ShapeDtypeStruct((H, 1, 1), F32)] * 2,
        compiler_params=_cp(("parallel", "arbitrary")),
    )(qkv, qkv, qkv, b, a, alog, dtb, *dxs)


def _gdn_scan_specs(T):
    C, dk, H = GDN_CHUNK, GDN_HEAD_DIM, GDN_HEADS
    return [pl.BlockSpec((H, C, C), lambda n: (0, n, 0))] + [pl.BlockSpec((H, C, dk), lambda n: (0, n, 0))] * 4 + [
        pl.BlockSpec((H, 8, dk), lambda n: (0, n, 0))]


def _gdn_scan_fwd(xs):
    H, T, dk = xs[1].shape
    C = GDN_CHUNK
    n = T // C

    def body(qk_ref, u_ref, w_ref, qd_ref, kd_ref, gl_ref, o_ref, s_ref, state):
        c = pl.program_id(0)

        @pl.when(c == 0)
        def _():
            state[...] = jnp.zeros_like(state)

        S = state[...]
        s_ref[0] = S
        v_new = u_ref[...] - _bdot(w_ref[...], S)
        o_ref[...] = _bdot(qd_ref[...], S) + _bdot(qk_ref[...], v_new)
        state[...] = S * jnp.tile(gl_ref[...], (1, dk // 8, 1)) + _bdot(kd_ref[...], v_new, "tn")

    return pl.pallas_call(
        body, name="gdn_scan_fwd", grid=(n,),
        in_specs=_gdn_scan_specs(T),
        out_specs=[pl.BlockSpec((H, C, dk), lambda n: (0, n, 0)), pl.BlockSpec((1, H, dk, dk), lambda n: (n, 0, 0, 0))],
        out_shape=[jax.ShapeDtypeStruct((H, T, dk), F32), jax.ShapeDtypeStruct((n, H, dk, dk), F32)],
        scratch_shapes=[pltpu.VMEM((H, dk, dk), F32)],
        compiler_params=_cp(("arbitrary",)),
    )(*xs)


def _gdn_scan_bwd(xs, states, do):
    H, T, dk = xs[1].shape
    C = GDN_CHUNK
    n = T // C

    def rev(spec_shape, f):
        return pl.BlockSpec(spec_shape, lambda i: f(n - 1 - i))

    def body(qk_ref, u_ref, w_ref, qd_ref, kd_ref, gl_ref, s_ref, do_ref,
             dqk_ref, du_ref, dw_ref, dqd_ref, dkd_ref, dgl_ref, dstate):
        i = pl.program_id(0)

        @pl.when(i == 0)
        def _():
            dstate[...] = jnp.zeros_like(dstate)

        S = s_ref[0]
        dS = dstate[...]
        do_v = do_ref[...]
        qk, w, qd, kd = qk_ref[...], w_ref[...], qd_ref[...], kd_ref[...]
        v_new = u_ref[...] - _bdot(w, S)
        dv_new = _bdot(qk, do_v, "tn") + _bdot(kd, dS)
        dqk_ref[...] = _bdot(do_v, v_new, "nt")
        dqd_ref[...] = _bdot(do_v, S, "nt")
        dkd_ref[...] = _bdot(v_new, dS, "nt")
        du_ref[...] = dv_new
        dw_ref[...] = -_bdot(dv_new, S, "nt")
        dgl = jnp.sum(jnp.sum(S * dS, axis=2, keepdims=True), axis=1, keepdims=True)
        dgl_ref[...] = jnp.broadcast_to(dgl, dgl_ref.shape)
        dstate[...] = (dS * jnp.tile(gl_ref[...], (1, dk // 8, 1)) + _bdot(qd, do_v, "tn")
                       - _bdot(w, dv_new, "tn"))

    in_specs = [rev((H, C, C), lambda m: (0, m, 0))] + [rev((H, C, dk), lambda m: (0, m, 0))] * 4 + [
        rev((H, 8, dk), lambda m: (0, m, 0)), rev((1, H, dk, dk), lambda m: (m, 0, 0, 0)),
        rev((H, C, dk), lambda m: (0, m, 0))]
    out_specs = [rev((H, C, C), lambda m: (0, m, 0))] + [rev((H, C, dk), lambda m: (0, m, 0))] * 4 + [
        rev((H, 8, dk), lambda m: (0, m, 0))]
    out_shape = [jax.ShapeDtypeStruct((H, T, C), F32)] + [jax.ShapeDtypeStruct((H, T, dk), F32)] * 4 + [
        jax.ShapeDtypeStruct((H, 8 * n, dk), F32)]
    return pl.pallas_call(
        body, name="gdn_scan_bwd", grid=(n,),
        in_specs=in_specs, out_specs=out_specs, out_shape=out_shape,
        scratch_shapes=[pltpu.VMEM((H, dk, dk), F32)],
        compiler_params=_cp(("arbitrary",)),
    )(*xs, states, do)


def _gdn_post_fwd(o, proj, norm_w):
    H, T, dk = o.shape
    tt = _tile(T, 1024)
    zoff = 3 * GDN_HEADS

    def body(o_ref, z_ref, g_ref, y_ref):
        ov = o_ref[0]
        r = lax.rsqrt(jnp.mean(ov * ov, axis=-1, keepdims=True) + EPS)
        y_ref[...] = (ov * r * g_ref[...] * _silu(z_ref[...])).astype(BF16)

    return pl.pallas_call(
        body, name="gdn_post_fwd", grid=(H, T // tt),
        in_specs=[pl.BlockSpec((1, tt, dk), lambda h, i: (h, i, 0)), pl.BlockSpec((tt, dk), lambda h, i: (i, zoff + h)),
                  pl.BlockSpec((1, dk), lambda h, i: (0, 0))],
        out_specs=pl.BlockSpec((tt, dk), lambda h, i: (i, h)),
        out_shape=jax.ShapeDtypeStruct((T, H * dk), BF16),
        compiler_params=_cp(("parallel", "parallel")),
    )(o, proj, norm_w)


def _gdn_post_bwd(o, proj, norm_w, dy):
    H, T, dk = o.shape
    tt = _tile(T, 1024)
    zoff = 3 * GDN_HEADS

    def body(o_ref, z_ref, g_ref, dy_ref, do_ref, dz_ref, dg_ref):
        i = pl.program_id(1)
        ov, z, g, dyv = o_ref[0], z_ref[...], g_ref[...], dy_ref[...]
        r = lax.rsqrt(jnp.mean(ov * ov, axis=-1, keepdims=True) + EPS)
        on = ov * r
        sz = _silu(z)
        dz_ref[...] = (dyv * on * g * _dsilu(z)).astype(BF16)
        dn = dyv * sz
        gdn = dn * g
        do_ref[0] = r * (gdn - on * jnp.mean(gdn * on, axis=-1, keepdims=True))

        @pl.when(i == 0)
        def _():
            dg_ref[...] = jnp.zeros_like(dg_ref)

        dg_ref[0] += jnp.sum(dn * on, axis=0, keepdims=True)

    return pl.pallas_call(
        body, name="gdn_post_bwd", grid=(H, T // tt),
        in_specs=[pl.BlockSpec((1, tt, dk), lambda h, i: (h, i, 0)), pl.BlockSpec((tt, dk), lambda h, i: (i, zoff + h)),
                  pl.BlockSpec((1, dk), lambda h, i: (0, 0)), pl.BlockSpec((tt, dk), lambda h, i: (i, h))],
        out_specs=[pl.BlockSpec((1, tt, dk), lambda h, i: (h, i, 0)), pl.BlockSpec((tt, dk), lambda h, i: (i, h)),
                   pl.BlockSpec((1, 1, dk), lambda h, i: (h, 0, 0))],
        out_shape=[jax.ShapeDtypeStruct((H, T, dk), F32), jax.ShapeDtypeStruct((T, H * dk), BF16),
                   jax.ShapeDtypeStruct((H, 1, dk), F32)],
        compiler_params=_cp(("parallel", "arbitrary")),
    )(o, proj, norm_w, dy)


def _ple_fwd(x, pp, gl, name):
    T, D = x.shape
    tt = _tile(T, 512)

    def body(x_ref, p_ref, g_ref, o_ref):
        o_ref[...] = x_ref[...] + p_ref[...] * _sigmoid(g_ref[...])

    row = pl.BlockSpec((tt, D), lambda i: (i, 0))
    return pl.pallas_call(
        body, name=name, grid=(T // tt,), in_specs=[row, row, row], out_specs=row,
        out_shape=jax.ShapeDtypeStruct((T, D), F32), compiler_params=_cp(("parallel",)),
    )(x, pp, gl)


def _ple_bwd(dx, pp, gl, name):
    T, D = dx.shape
    tt = _tile(T, 512)

    def body(dx_ref, p_ref, g_ref, dp_ref, dg_ref):
        s = _sigmoid(g_ref[...])
        dxv = dx_ref[...]
        dp_ref[...] = (dxv * s).astype(BF16)
        dg_ref[...] = (dxv * p_ref[...] * s * (1.0 - s)).astype(BF16)

    row = pl.BlockSpec((tt, D), lambda i: (i, 0))
    return pl.pallas_call(
        body, name=name, grid=(T // tt,), in_specs=[row, row, row], out_specs=[row, row],
        out_shape=[jax.ShapeDtypeStruct((T, D), BF16)] * 2, compiler_params=_cp(("parallel",)),
    )(dx, pp, gl)


def _heads_in(a2d):
    T = a2d.shape[0]
    a = a2d.reshape(T, 3, SB_HEADS, SB_HEAD_DIM).transpose(1, 2, 0, 3).astype(BF16)
    return a[0] * jnp.asarray(SB_SCALE, BF16), a[1], a[2]


def _heads_out(a):
    H, T, dh = a.shape
    return a.transpose(1, 0, 2).reshape(T, H * dh)


UP_SHARD = 2 * FFN_DIM // N_CHIPS
DOWN_SHARD = FFN_DIM // N_CHIPS
GATE_SHARD = D_MODEL // N_CHIPS
IN_E_SHARD = 2 * D_MODEL // N_CHIPS


def _ffn_fwd(x, norm, W, conv_w, l):
    T = x.shape[0]
    hf = _rms_fwd(x, norm, f"ffn_norm{l}")
    up = _mm(hf, W["w_up"], "nn", f"ffn_up{l}", dims=(T, 2 * FFN_DIM, D_MODEL), b_view=_cols_of(1, l), tn=UP_SHARD)
    act = _ffn_act_fwd(up, conv_w, f"ffn_act{l}")
    x_out = _mm(act, W["w_down"], "nn", f"ffn_down{l}", dims=(T, D_MODEL, FFN_DIM), b_view=_layer_of(l), res=x,
                tn=1024, tk=1408)
    return x_out, (x, hf, up, act)


def _ffn_bwd(dx_out, saved, norm, W, conv_w, l):
    x, hf, up, act = saved
    T = x.shape[0]
    dact = _mm(dx_out, W["w_down"], "nt", f"ffn_dact{l}", dims=(T, FFN_DIM, D_MODEL), b_view=_layer_of(l),
               out_dtype=BF16, tn=1408)
    dw_down = _mm(act, dx_out, "tn", f"ffn_dwdown{l}", tm=1408, tn=1024, tk=512)
    dup, dcw_g, dcw_v = _ffn_act_bwd(up, conv_w, dact, f"ffn_dact_conv{l}")
    dw_up = _mm(hf, dup, "tn", f"ffn_dwup{l}", dims=(D_MODEL, 2 * FFN_DIM, T), b_view=_cols_of(FFN_DIM // UP_SHARD),
                o_view=_cols_of(1), out_shape=(N_CHIPS, D_MODEL, UP_SHARD), tm=1024, tn=UP_SHARD, tk=512)
    dhf = _mm(dup, W["w_up"], "nt", f"ffn_dhf{l}", dims=(T, D_MODEL, 2 * FFN_DIM),
              a_view=_cols_of(FFN_DIM // UP_SHARD), b_view=_cols_of(1, l), tn=1024, tk=UP_SHARD)
    dx, dnorm = _rms_bwd(x, norm, dhf, dx_out, f"ffn_dnorm{l}")
    return (dx, dnorm, dw_up, jnp.concatenate([dcw_g, dcw_v], axis=1),
            dw_down.reshape(N_CHIPS, DOWN_SHARD, D_MODEL))


def _ple_layer_fwd(x, p, norm, W, l):
    T = x.shape[0]
    hg = _rms_fwd(x, norm, f"ple_norm{l}")
    gl = _mm(hg, W["w_ple_gate"], "nn", f"ple_gate{l}", dims=(T, D_MODEL, D_MODEL), b_view=_layer_of(l), tn=1024)
    pp = _mm(p, W["w_ple"], "nn", f"ple_proj{l}", dims=(T, D_MODEL, PLE_DIM), a_view=_layer_of(l),
             b_view=_layer_of(l), tn=1024)
    return _ple_fwd(x, pp, gl, f"ple_mix{l}"), (x, hg, gl, pp)


def _ple_layer_bwd(dx_out, saved, p, norm, W, l):
    x, hg, gl, pp = saved
    T = x.shape[0]
    dpp, dgl = _ple_bwd(dx_out, pp, gl, f"ple_dmix{l}")
    dw_ple = _mm(p, dpp, "tn", f"ple_dwple{l}", dims=(PLE_DIM, D_MODEL, T), a_view=_layer_of(l), o_view=_cols_of(1),
                 out_shape=(N_CHIPS, PLE_DIM, PLE_DIM), tm=PLE_DIM, tn=PLE_DIM, tk=512)
    dw_gate = _mm(hg, dgl, "tn", f"ple_dwgate{l}", tm=1024, tn=1024, tk=512)
    dhg = _mm(dgl, W["w_ple_gate"], "nt", f"ple_dhg{l}", dims=(T, D_MODEL, D_MODEL), b_view=_layer_of(l), tn=1024)
    dx, dnorm = _rms_bwd(x, norm, dhg, dx_out, f"ple_dnorm{l}")
    return dx, dnorm, dw_gate.reshape(N_CHIPS, GATE_SHARD, D_MODEL), dw_ple


def _local_step(x, p, target, W):
    T = x.shape[0]
    H = GDN_HEADS
    G = {}

    hn_e = _rms_fwd(x, W["mix_norm_e"], "mix_norm_e")
    proj_e = _mm(hn_e, W["w_in_e"], "nn", "in_e", dims=(T, 2 * D_MODEL, D_MODEL), b_view=_cols_of(1), tn=IN_E_SHARD)
    pool_out = _pool_fwd(proj_e, W["pool_w"], W["pool_scale"])
    q, k, v = _heads_in(proj_e[:, POOL_WIDTH:])
    attn, ltot = _sb_fwd(q, k, v)
    mix_e = jnp.concatenate([pool_out, _heads_out(attn)], axis=1).astype(BF16)
    x1 = _mm(mix_e, W["w_out_e"], "nn", "out_e", res=x, tn=1024)
    x2, ffn0 = _ffn_fwd(x1, W["ffn_norm"][0:1], W, W["ffn_conv"][0], 0)
    x3, ple0 = _ple_layer_fwd(x2, p, W["ple_norm"][0:1], W, 0)

    hn_o = _rms_fwd(x3, W["mix_norm_o"], "mix_norm_o")
    proj_o = _mm(hn_o, W["w_in_o"], "nn", "in_o", tn=1408)
    qkv = _gdn_pre_fwd(proj_o, W["conv_qkv_o"])
    ba = proj_o[:, 4 * D_MODEL:4 * D_MODEL + 2 * H]
    b_h = ba[:, :H].T.reshape(H, T, 1)
    a_h = ba[:, H:].T.reshape(H, T, 1)
    alog = W["a_log_o"].reshape(H, 1, 1)
    dtb = W["dt_bias_o"].reshape(H, 1, 1)
    xs = _gdn_prep_fwd(qkv, b_h, a_h, alog, dtb)
    o, states = _gdn_scan_fwd(xs)
    og = _gdn_post_fwd(o, proj_o, W["gdn_norm_o"])
    x4 = _mm(og, W["w_out_o"], "nn", "out_o", res=x3, tn=1024)
    x5, ffn1 = _ffn_fwd(x4, W["ffn_norm"][1:2], W, W["ffn_conv"][1], 1)
    x6, ple1 = _ple_layer_fwd(x5, p, W["ple_norm"][1:2], W, 1)

    sq, dx6, G["final_norm"] = _final_loss(x6, W["final_norm"], target, "final_loss")

    dx5, dpn1, dwg1, dwp1 = _ple_layer_bwd(dx6, ple1, p, W["ple_norm"][1:2], W, 1)
    dx4, dfn1, dwu1, dfc1, dwd1 = _ffn_bwd(dx5, ffn1, W["ffn_norm"][1:2], W, W["ffn_conv"][1], 1)
    dog = _mm(dx4, W["w_out_o"], "nt", "d_og", tn=1024)
    G["w_out_o"] = _mm(og, dx4, "tn", "dw_out_o", tm=1024, tn=1024, tk=512).reshape(N_CHIPS, GATE_SHARD, D_MODEL)
    do, dz, dgn = _gdn_post_bwd(o, proj_o, W["gdn_norm_o"], dog)
    G["gdn_norm_o"] = jnp.sum(dgn, axis=0)
    dxs = _gdn_scan_bwd(xs, states, do)
    dqkv_act, db, da, dal, ddt = _gdn_prep_bwd(qkv, b_h, a_h, alog, dtb, dxs)
    G["a_log_o"] = dal.reshape(1, H)
    G["dt_bias_o"] = ddt.reshape(1, H)
    dqkv, G["conv_qkv_o"] = _gdn_pre_bwd(proj_o, W["conv_qkv_o"], dqkv_act)
    dba = jnp.concatenate([db.reshape(H, T).T, da.reshape(H, T).T,
                           jnp.zeros((T, ODD_IN_PAD - ODD_IN + 0), F32)], axis=1).astype(BF16)
    dproj_o = jnp.concatenate([dqkv, dz, dba], axis=1)
    G["w_in_o"] = _mm(hn_o, dproj_o, "tn", "dw_in_o", tm=1024, tn=1408, tk=512)
    dhn_o = _mm(dproj_o, W["w_in_o"], "nt", "d_hn_o", tn=1024, tk=1408)
    dx3, G["mix_norm_o"] = _rms_bwd(x3, W["mix_norm_o"], dhn_o, dx4, "d_mix_norm_o")

    dx2, dpn0, dwg0, dwp0 = _ple_layer_bwd(dx3, ple0, p, W["ple_norm"][0:1], W, 0)
    dx1, dfn0, dwu0, dfc0, dwd0 = _ffn_bwd(dx2, ffn0, W["ffn_norm"][0:1], W, W["ffn_conv"][0], 0)
    dmix = _mm(dx1, W["w_out_e"], "nt", "d_mix_e", tn=1024)
    G["w_out_e"] = _mm(mix_e, dx1, "tn", "dw_out_e", tm=1024, tn=1024, tk=512).reshape(N_CHIPS, GATE_SHARD, D_MODEL)
    du, G["pool_w"], G["pool_scale"] = _pool_bwd(proj_e, W["pool_w"], W["pool_scale"], dmix)
    dattn = dmix[:, POOL_WIDTH:].reshape(T, SB_HEADS, SB_HEAD_DIM).transpose(1, 0, 2)
    dqa, dka, dva = _sb_bwd(q, k, v, dattn, ltot)
    dproj_e = jnp.concatenate([du, _heads_out(dqa), _heads_out(dka), _heads_out(dva)], axis=1).astype(BF16)
    G["w_in_e"] = _mm(hn_e, dproj_e, "tn", "dw_in_e", dims=(D_MODEL, 2 * D_MODEL, T), o_view=_cols_of(1),
                      out_shape=(N_CHIPS, D_MODEL, IN_E_SHARD), tm=1024, tn=IN_E_SHARD, tk=512)
    dhn_e = _mm(dproj_e, W["w_in_e"], "nt", "d_hn_e", dims=(T, D_MODEL, 2 * D_MODEL), b_view=_cols_of(1),
                tn=1024, tk=IN_E_SHARD)
    grad_x, G["mix_norm_e"] = _rms_bwd(x, W["mix_norm_e"], dhn_e, dx1, "d_mix_norm_e")

    G["ffn_norm"] = jnp.concatenate([dfn0, dfn1], axis=0)
    G["ple_norm"] = jnp.concatenate([dpn0, dpn1], axis=0)
    G["ffn_conv"] = jnp.stack([dfc0, dfc1])
    G["w_up"] = [dwu0, dwu1]
    G["w_down"] = [dwd0, dwd1]
    G["w_ple_gate"] = [dwg0, dwg1]
    G["w_ple"] = [dwp0, dwp1]
    return sq[0, 0], grad_x, G


BIG = ("w_in_e", "w_out_e", "w_in_o", "w_out_o", "w_up", "w_down", "w_ple_gate", "w_ple")
SHARDED_SMALL = (("mix_norm_o", 1), ("conv_qkv_o", 2), ("ffn_conv", 2))
REPLICATED = ("mix_norm_e", "pool_w", "pool_scale", "a_log_o", "dt_bias_o", "gdn_norm_o", "ffn_norm", "ple_norm",
              "final_norm")
WEIGHT_ORDER = ("mix_norm_e", "w_in_e", "pool_w", "pool_scale", "w_out_e", "mix_norm_o", "w_in_o", "conv_qkv_o",
                "a_log_o", "dt_bias_o", "gdn_norm_o", "w_out_o", "ffn_norm", "w_up", "ffn_conv", "w_down", "ple_norm",
                "w_ple_gate", "w_ple", "final_norm")
SMALL_W = LANE
SMALL_ROWS = 16


def _size(shape):
    n = 1
    for s in shape:
        n *= s
    return n


def _pack(arrs, width, granule):
    flat = jnp.concatenate([a.reshape(-1) for a in arrs])
    rows = -(-flat.shape[0] // width)
    rows = -(-rows // granule) * granule
    return jnp.pad(flat, (0, rows * width - flat.shape[0])).reshape(rows, width)


def _unpack(flat2d, shapes):
    flat = flat2d.reshape(-1)
    out, off = [], 0
    for s in shapes:
        out.append(flat[off:off + _size(s)].reshape(s))
        off += _size(s)
    return out


MESH_ID = pl.DeviceIdType.MESH
HBM_SPEC = pl.BlockSpec(memory_space=pltpu.HBM)


def _where_am_i():
    return lax.axis_index("x"), lax.axis_index("y"), lax.axis_index("c")


def _other_chips(x, y):
    return [(1 - x, y), (x, 1 - y), (1 - x, 1 - y)]


def _remote(src, dst, send_sems, recv_sems, k, to):
    return pltpu.make_async_remote_copy(src_ref=src, dst_ref=dst, send_sem=send_sems.at[k], recv_sem=recv_sems.at[k],
                                        device_id=to, device_id_type=MESH_ID)


def _chip_allgather(pack, name):
    R, Wd = pack.shape
    Rh = R // 2

    def body(src_ref, out_ref, send_sems, recv_sems, local_sem):
        x, y, c = _where_am_i()
        me, sib = (x, y, c), (x, y, 1 - c)
        chips = _other_chips(x, y)
        mine_rows = pl.ds(pl.multiple_of(c * Rh, SMALL_ROWS), Rh)
        sib_rows = pl.ds(pl.multiple_of((1 - c) * Rh, SMALL_ROWS), Rh)
        j_me = 2 * x + y
        local = pltpu.make_async_copy(src_ref, out_ref.at[j_me], local_sem)
        local.start()
        first = [_remote(src_ref.at[mine_rows], out_ref.at[j_me, mine_rows], send_sems, recv_sems, k, (cx, cy, c))
                 for k, (cx, cy) in enumerate(chips)]
        for cp in first:
            cp.start()
        passed = []
        for k, (cx, cy) in enumerate(chips):
            blk = out_ref.at[2 * cx + cy, mine_rows]
            _remote(blk, blk, send_sems, recv_sems, k, me).wait_recv()
            fw = _remote(blk, blk, send_sems, recv_sems, 3 + k, sib)
            fw.start()
            passed.append(fw)
        for k, (cx, cy) in enumerate(chips):
            blk = out_ref.at[2 * cx + cy, sib_rows]
            _remote(blk, blk, send_sems, recv_sems, 3 + k, me).wait_recv()
        for cp in first + passed:
            cp.wait_send()
        local.wait()

    return pl.pallas_call(
        body, name=name, in_specs=[HBM_SPEC], out_specs=HBM_SPEC,
        out_shape=jax.ShapeDtypeStruct((N_CHIPS, R, Wd), pack.dtype),
        scratch_shapes=[pltpu.SemaphoreType.DMA((6,)), pltpu.SemaphoreType.DMA((6,)), pltpu.SemaphoreType.DMA],
    )(pack)


def _chip_allgather_many(blocks, name):
    n = len(blocks)

    def body(*refs):
        srcs, outs = refs[:n], refs[n:2 * n]
        send_sems, recv_sems = refs[2 * n:]
        x, y, c = _where_am_i()
        me, sib = (x, y, c), (x, y, 1 - c)
        chips = _other_chips(x, y)
        j_me = 2 * x + y
        first = [_remote(srcs[p].at[c], outs[p].at[j_me, c], send_sems, recv_sems, 6 * p + k, (cx, cy, c))
                 for p in range(n) for k, (cx, cy) in enumerate(chips)]
        for cp in first:
            cp.start()
        passed = []
        for k, (cx, cy) in enumerate(chips):
            for p in range(n):
                blk = outs[p].at[2 * cx + cy, c]
                _remote(blk, blk, send_sems, recv_sems, 6 * p + k, me).wait_recv()
                fw = _remote(blk, blk, send_sems, recv_sems, 6 * p + 3 + k, sib)
                fw.start()
                passed.append(fw)
        for k, (cx, cy) in enumerate(chips):
            for p in range(n):
                blk = outs[p].at[2 * cx + cy, 1 - c]
                _remote(blk, blk, send_sems, recv_sems, 6 * p + 3 + k, me).wait_recv()
        for cp in first + passed:
            cp.wait_send()

    return pl.pallas_call(
        body, name=name, in_specs=[HBM_SPEC] * n, out_specs=[HBM_SPEC] * n,
        out_shape=[jax.ShapeDtypeStruct((N_CHIPS,) + b.shape, b.dtype) for b in blocks],
        scratch_shapes=[pltpu.SemaphoreType.DMA((6 * n,)), pltpu.SemaphoreType.DMA((6 * n,))],
    )(*blocks)


def _sibling_swap_many(pieces, name):
    n = len(pieces)

    def body(*refs):
        srcs, outs = refs[:n], refs[n:2 * n]
        send_sems, recv_sems = refs[2 * n:]
        x, y, c = _where_am_i()
        cps = [_remote(srcs[p].at[:, 1 - c], outs[p], send_sems, recv_sems, p, (x, y, 1 - c)) for p in range(n)]
        for cp in cps:
            cp.start()
        for cp in cps:
            cp.wait()

    return pl.pallas_call(
        body, name=name, in_specs=[HBM_SPEC] * n, out_specs=[HBM_SPEC] * n,
        out_shape=[jax.ShapeDtypeStruct((g.shape[0],) + g.shape[2:], g.dtype) for g in pieces],
        scratch_shapes=[pltpu.SemaphoreType.DMA((n,)), pltpu.SemaphoreType.DMA((n,))],
    )(*pieces)


def _chip_scatter_many(sums, name):
    n = len(sums)

    def body(*refs):
        srcs, outs = refs[:n], refs[n:2 * n]
        send_sems, recv_sems = refs[2 * n:]
        x, y, c = _where_am_i()
        cps = [_remote(srcs[p].at[2 * cx + cy], outs[p].at[k], send_sems, recv_sems, 3 * p + k, (cx, cy, c))
               for p in range(n) for k, (cx, cy) in enumerate(_other_chips(x, y))]
        for cp in cps:
            cp.start()
        for cp in cps:
            cp.wait()

    return pl.pallas_call(
        body, name=name, in_specs=[HBM_SPEC] * n, out_specs=[HBM_SPEC] * n,
        out_shape=[jax.ShapeDtypeStruct((N_CHIPS - 1,) + s.shape[1:], s.dtype) for s in sums],
        scratch_shapes=[pltpu.SemaphoreType.DMA((3 * n,)), pltpu.SemaphoreType.DMA((3 * n,))],
    )(*sums)


def _sibling_send_many(halves, name):
    n = len(halves)

    def body(*refs):
        srcs, outs = refs[:n], refs[n:2 * n]
        send_sems, recv_sems = refs[2 * n:]
        x, y, c = _where_am_i()
        cps = [_remote(srcs[p], outs[p], send_sems, recv_sems, p, (x, y, 1 - c)) for p in range(n)]
        for cp in cps:
            cp.start()
        for cp in cps:
            cp.wait()

    return pl.pallas_call(
        body, name=name, in_specs=[HBM_SPEC] * n, out_specs=[HBM_SPEC] * n,
        out_shape=[jax.ShapeDtypeStruct(h.shape, h.dtype) for h in halves],
        scratch_shapes=[pltpu.SemaphoreType.DMA((n,)), pltpu.SemaphoreType.DMA((n,))],
    )(*halves)


def _row_tile(rows, pref=512):
    best = 8
    for t in range(8, pref + 1, 8):
        if rows % t == 0:
            best = t
    return best


def _where_ids():
    x, y, c = _where_am_i()
    return jnp.stack([c, 2 * x + y]).astype(jnp.int32)


RS_ROWS = 256


def _chip_sums_bf16(G, A, ids, name):
    n, _, hr, cols = G.shape
    tr = _row_tile(hr, RS_ROWS)

    def body(ids_ref, g_ref, a_ref, o_ref):
        o_ref[...] = (g_ref[...] + a_ref[...]).astype(BF16)

    return pl.pallas_call(
        body, name=name,
        grid_spec=pltpu.PrefetchScalarGridSpec(
            num_scalar_prefetch=1, grid=(n, hr // tr),
            in_specs=[pl.BlockSpec((None, None, tr, cols), lambda j, i, ids: (j, ids[0], i, 0)),
                      pl.BlockSpec((None, tr, cols), lambda j, i, ids: (j, i, 0))],
            out_specs=pl.BlockSpec((None, tr, cols), lambda j, i, ids: (j, i, 0))),
        out_shape=jax.ShapeDtypeStruct((n, hr, cols), BF16),
        compiler_params=_cp(("parallel", "parallel")),
    )(ids, G, A)


def _total_half(G, A, B, ids, name):
    _, _, hr, cols = G.shape
    tr = _row_tile(hr, RS_ROWS)

    def body(ids_ref, g_ref, a_ref, b_ref, o_ref):
        s = g_ref[...] + a_ref[...]
        for k in range(N_CHIPS - 1):
            s = s + b_ref[k].astype(F32)
        o_ref[...] = s

    return pl.pallas_call(
        body, name=name,
        grid_spec=pltpu.PrefetchScalarGridSpec(
            num_scalar_prefetch=1, grid=(hr // tr,),
            in_specs=[pl.BlockSpec((None, None, tr, cols), lambda i, ids: (ids[1], ids[0], i, 0)),
                      pl.BlockSpec((None, tr, cols), lambda i, ids: (ids[1], i, 0)),
                      pl.BlockSpec((N_CHIPS - 1, tr, cols), lambda i, ids: (0, i, 0))],
            out_specs=pl.BlockSpec((tr, cols), lambda i, ids: (i, 0))),
        out_shape=jax.ShapeDtypeStruct((hr, cols), F32),
        compiler_params=_cp(("parallel",)),
    )(ids, G, A, B)


def _small_allreduce(v, name):
    R, Wd = v.shape

    def body(x_ref, sum_ref, all_ref, send_sems, recv_sems, local_sem):
        x, y, c = _where_am_i()
        me, sib = (x, y, c), (x, y, 1 - c)
        chips = _other_chips(x, y)

        def slot(px, py, pc):
            return all_ref.at[4 * px + 2 * py + pc]

        local = pltpu.make_async_copy(x_ref, slot(*me), local_sem)
        local.start()
        first = [_remote(x_ref, slot(*me), send_sems, recv_sems, 0, sib)]
        first += [_remote(x_ref, slot(*me), send_sems, recv_sems, 1 + k, (cx, cy, c)) for k, (cx, cy) in enumerate(chips)]
        for cp in first:
            cp.start()
        passed = []
        for k, (cx, cy) in enumerate(chips):
            blk = slot(cx, cy, c)
            _remote(blk, blk, send_sems, recv_sems, 1 + k, me).wait_recv()
            fw = _remote(blk, blk, send_sems, recv_sems, 4 + k, sib)
            fw.start()
            passed.append(fw)
        _remote(slot(*sib), slot(*sib), send_sems, recv_sems, 0, me).wait_recv()
        for k, (cx, cy) in enumerate(chips):
            blk = slot(cx, cy, 1 - c)
            _remote(blk, blk, send_sems, recv_sems, 4 + k, me).wait_recv()
        for cp in first + passed:
            cp.wait_send()
        local.wait()
        s = all_ref[0]
        for d in range(1, N_DEV):
            s = s + all_ref[d]
        sum_ref[...] = s

    vm = pl.BlockSpec(memory_space=pltpu.VMEM)
    return pl.pallas_call(
        body, name=name, in_specs=[vm], out_specs=[vm, vm],
        out_shape=[jax.ShapeDtypeStruct((R, Wd), F32), jax.ShapeDtypeStruct((N_DEV, R, Wd), F32)],
        scratch_shapes=[pltpu.SemaphoreType.DMA((7,)), pltpu.SemaphoreType.DMA((7,)), pltpu.SemaphoreType.DMA],
    )(v)[0]


def _adamw(w, g, m, v, name):
    L, R, Wd = w.shape
    tr = _row_tile(R, RS_ROWS)
    c1 = 1.0 - ADAM_B1 ** ADAM_STEP
    c2 = 1.0 - ADAM_B2 ** ADAM_STEP

    def body(w_ref, g_ref, m_ref, v_ref, d_ref, nm_ref, nv_ref):
        gv = g_ref[...]
        nm = ADAM_B1 * m_ref[...] + (1.0 - ADAM_B1) * gv
        nv = ADAM_B2 * v_ref[...] + (1.0 - ADAM_B2) * (gv * gv)
        d_ref[...] = -ADAM_LR * ((nm / c1) / (jnp.sqrt(nv / c2) + ADAM_EPS) + ADAM_WD * w_ref[...])
        nm_ref[...] = nm
        nv_ref[...] = nv

    row = pl.BlockSpec((None, tr, Wd), lambda l, i: (l, i, 0))
    shp = jax.ShapeDtypeStruct((L, R, Wd), F32)
    return pl.pallas_call(
        body, name=name, grid=(L, R // tr), in_specs=[row] * 4, out_specs=[row] * 3, out_shape=[shp] * 3,
        compiler_params=_cp(("parallel", "parallel")),
    )(w, g, m, v)


def _adamw_halves(w, m, v, mine, theirs, ids, name):
    L, R, Wd = w.shape
    hr = R // 2
    tr = _row_tile(hr, RS_ROWS)
    c1 = 1.0 - ADAM_B1 ** ADAM_STEP
    c2 = 1.0 - ADAM_B2 ** ADAM_STEP

    def body(ids_ref, w_ref, m_ref, v_ref, *refs):
        g_refs, (g_ref, d_ref, nm_ref, nv_ref) = refs[:2 * L], refs[2 * L:]
        layer, half = pl.program_id(0), pl.program_id(1)
        own = half == ids_ref[0]
        gv = jnp.where(own, g_refs[0][...], g_refs[L][...])
        for l in range(1, L):
            gv = jnp.where(layer == l, jnp.where(own, g_refs[l][...], g_refs[L + l][...]), gv)
        nm = ADAM_B1 * m_ref[...] + (1.0 - ADAM_B1) * gv
        nv = ADAM_B2 * v_ref[...] + (1.0 - ADAM_B2) * (gv * gv)
        g_ref[...] = gv
        d_ref[...] = -ADAM_LR * ((nm / c1) / (jnp.sqrt(nv / c2) + ADAM_EPS) + ADAM_WD * w_ref[...])
        nm_ref[...] = nm
        nv_ref[...] = nv

    blk = pl.BlockSpec((None, None, tr, Wd), lambda l, h, i, ids: (l, h, i, 0))
    g_blk = pl.BlockSpec((tr, Wd), lambda l, h, i, ids: (i, 0))
    shp = jax.ShapeDtypeStruct((L, 2, hr, Wd), F32)
    outs = pl.pallas_call(
        body, name=name,
        grid_spec=pltpu.PrefetchScalarGridSpec(
            num_scalar_prefetch=1, grid=(L, 2, hr // tr),
            in_specs=[blk] * 3 + [g_blk] * (2 * L), out_specs=[blk] * 4),
        out_shape=[shp] * 4,
        compiler_params=_cp(("parallel", "parallel", "parallel")),
    )(ids, *[a.reshape(L, 2, hr, Wd) for a in (w, m, v)], *mine, *theirs)
    return tuple(o.reshape(L, R, Wd) for o in outs)


def _two_halves(a):
    cols = a.shape[-1]
    return a.reshape(2, _size(a.shape) // (2 * cols), cols)


def _gather_weights(P):
    mine = [_two_halves(P[n].astype(BF16)) for n in BIG]
    chip = 2 * lax.axis_index("x") + lax.axis_index("y")
    gathered = {n: lax.dynamic_update_slice_in_dim(g, own[None], chip, axis=0)
                for n, g, own in zip(BIG, _chip_allgather_many(mine, "ag_weights"), mine)}
    small_shapes = [P[n].shape for n, _ in SHARDED_SMALL]
    small = _chip_allgather(_pack([P[n] for n, _ in SHARDED_SMALL], SMALL_W, SMALL_ROWS), "ag_small")
    parts = [_unpack(small[j], small_shapes) for j in range(N_CHIPS)]
    full = {n: jnp.concatenate([parts[j][i] for j in range(N_CHIPS)], axis=ax)
            for i, (n, ax) in enumerate(SHARDED_SMALL)}
    W = {n: P[n] for n in REPLICATED}
    W["pool_w"] = P["pool_w"][0]
    W["final_norm"] = P["final_norm"].reshape(1, D_MODEL)
    W["mix_norm_o"] = full["mix_norm_o"]
    W["conv_qkv_o"] = full["conv_qkv_o"][0]
    W["ffn_conv"] = full["ffn_conv"]
    W["w_in_e"] = gathered["w_in_e"].reshape(N_CHIPS, D_MODEL, IN_E_SHARD)
    W["w_out_e"] = gathered["w_out_e"].reshape(D_MODEL, D_MODEL)
    W["w_out_o"] = gathered["w_out_o"].reshape(D_MODEL, D_MODEL)
    w_in_o = gathered["w_in_o"].reshape(N_CHIPS, D_MODEL, ODD_IN // N_CHIPS)
    W["w_in_o"] = jnp.pad(jnp.concatenate([w_in_o[j] for j in range(N_CHIPS)], axis=1),
                          ((0, 0), (0, ODD_IN_PAD - ODD_IN)))
    W["w_up"] = gathered["w_up"]
    W["w_down"] = gathered["w_down"].transpose(1, 0, 2, 3).reshape(2, FFN_DIM, D_MODEL)
    W["w_ple_gate"] = gathered["w_ple_gate"].transpose(1, 0, 2, 3).reshape(2, D_MODEL, D_MODEL)
    W["w_ple"] = gathered["w_ple"].transpose(1, 2, 0, 3).reshape(2, PLE_DIM, D_MODEL)
    return W


def _reduce_big_gradients(G):
    w_in_o = G["w_in_o"]
    shard = ODD_IN // N_CHIPS
    pieces, layers_of = [], []
    for n in BIG:
        if n == "w_in_o":
            gs = [jnp.stack([w_in_o[:, j * shard:(j + 1) * shard] for j in range(N_CHIPS)])]
        else:
            gs = G[n] if isinstance(G[n], list) else [G[n]]
        layers_of.append(list(range(len(pieces), len(pieces) + len(gs))))
        pieces += [g.reshape(N_CHIPS, 2, g.shape[1] // 2, g.shape[2]) for g in gs]
    ids = _where_ids()
    from_sibling = _sibling_swap_many(pieces, "rs_sibling_swap")
    sums = [_chip_sums_bf16(g, a, ids, f"rs_chip_sums{i}") for i, (g, a) in enumerate(zip(pieces, from_sibling))]
    from_chips = _chip_scatter_many(sums, "rs_chip_scatter")
    halves = [_total_half(g, a, b, ids, f"rs_total{i}")
              for i, (g, a, b) in enumerate(zip(pieces, from_sibling, from_chips))]
    from_sibling_total = _sibling_send_many(halves, "rs_sibling_send")
    return {n: ([halves[p] for p in ps], [from_sibling_total[p] for p in ps]) for n, ps in zip(BIG, layers_of)}, ids


def kernel(x, p, mix_norm_e, w_in_e, pool_w, pool_scale, w_out_e, mix_norm_o, w_in_o, conv_qkv_o, a_log_o, dt_bias_o, gdn_norm_o, w_out_o, ffn_norm, w_up, ffn_conv, w_down, ple_norm, w_ple_gate, w_ple, final_norm, loss_target, m_mix_norm_e, m_w_in_e, m_pool_w, m_pool_scale, m_w_out_e, m_mix_norm_o, m_w_in_o, m_conv_qkv_o, m_a_log_o, m_dt_bias_o, m_gdn_norm_o, m_w_out_o, m_ffn_norm, m_w_up, m_ffn_conv, m_w_down, m_ple_norm, m_w_ple_gate, m_w_ple, m_final_norm, v_mix_norm_e, v_w_in_e, v_pool_w, v_pool_scale, v_w_out_e, v_mix_norm_o, v_w_in_o, v_conv_qkv_o, v_a_log_o, v_dt_bias_o, v_gdn_norm_o, v_w_out_o, v_ffn_norm, v_w_up, v_ffn_conv, v_w_down, v_ple_norm, v_w_ple_gate, v_w_ple, v_final_norm):
    args = locals()
    P = {n: args[n] for n in WEIGHT_ORDER}
    M = {n: args["m_" + n] for n in WEIGHT_ORDER}
    V = {n: args["v_" + n] for n in WEIGHT_ORDER}

    W = _gather_weights(P)
    T = x.shape[1]
    sq, grad_x, G = _local_step(x.reshape(T, D_MODEL), p.reshape(2, T, PLE_DIM), loss_target.reshape(T, D_MODEL), W)
    out = {}
    reduced, ids = _reduce_big_gradients(G)
    for n, (mine, theirs) in reduced.items():
        out[n] = _adamw_halves(P[n], M[n], V[n], mine, theirs, ids, f"adamw_{n}")

    small_full = {n: G[n] for n in REPLICATED}
    small_full["pool_w"] = G["pool_w"][None]
    small_full["final_norm"] = G["final_norm"].reshape(D_MODEL)
    small_full["mix_norm_o"] = G["mix_norm_o"]
    small_full["conv_qkv_o"] = G["conv_qkv_o"][None]
    small_full["ffn_conv"] = G["ffn_conv"]
    small_names = REPLICATED + tuple(n for n, _ in SHARDED_SMALL)
    summed = _small_allreduce(_pack([small_full[n] for n in small_names] + [sq.reshape(1)], SMALL_W, 8), "ar_small")
    *g_list, sq_total = _unpack(summed, [small_full[n].shape for n in small_names] + [(1,)])
    g_small = dict(zip(small_names, g_list))
    chip = 2 * lax.axis_index("x") + lax.axis_index("y")
    for n, ax in SHARDED_SMALL:
        width = P[n].shape[ax]
        g_small[n] = lax.dynamic_slice_in_dim(g_small[n], chip * width, width, axis=ax)

    def pack_small(D):
        return _pack([D[n] for n in small_names], SMALL_W, RS_ROWS)[None]

    g_pack = pack_small(g_small)
    upd = _adamw(pack_small(P), g_pack, pack_small(M), pack_small(V), "adamw_small")
    shapes = [P[n].shape for n in small_names]
    for n, *vals in zip(small_names, *[_unpack(a[0], shapes) for a in (g_pack,) + tuple(upd)]):
        out[n] = tuple(vals)

    loss = (0.5 / D_MODEL) * sq_total[0]
    return (loss, grad_x[None]) + tuple(out[n][i] for i in range(4) for n in WEIGHT_ORDER)
```

```python
import functools

import jax
import jax.numpy as jnp
from jax import lax
from jax.experimental import pallas as pl
from jax.experimental.pallas import tpu as pltpu

F32 = jnp.float32
BF16 = jnp.bfloat16

D_MODEL = 1024
PLE_DIM = 256
POOL_WIDTH = 512
POOL_WINDOWS = (2, 4, 8, 16)
POOL_GROUP_DIM = 128
SB_HEADS = 8
SB_HEAD_DIM = 64
GDN_HEADS = 8
GDN_HEAD_DIM = 128
GDN_CONV = 4
GDN_CHUNK = 64
FFN_DIM = 2816
FFN_CONV = 3
EPS = 1e-6
ODD_IN = 4 * D_MODEL + 2 * GDN_HEADS
ODD_IN_PAD = 33 * 128
ADAM_LR, ADAM_B1, ADAM_B2, ADAM_EPS, ADAM_WD, ADAM_STEP = 0.001, 0.9, 0.999, 1e-08, 0.01, 10

LANE = 128
VMEM_LIMIT = 56 * 1024 * 1024

N_CHIPS = 4
N_DEV = 8


def _cp(sem=None):
    return pltpu.CompilerParams(dimension_semantics=sem, vmem_limit_bytes=VMEM_LIMIT)


def _tile(n, pref):
    if n <= pref:
        return n
    best = None
    for t in range(LANE, pref + 1, LANE):
        if n % t == 0:
            best = t
    assert best is not None, (n, pref)
    return best


_DIMS = {"nn": (((1,), (0,)), ((), ())), "nt": (((1,), (1,)), ((), ())), "tn": (((0,), (0,)), ((), ()))}
_BDIMS = {"nn": (((2,), (1,)), ((0,), (0,))), "nt": (((2,), (2,)), ((0,), (0,))), "tn": (((1,), (1,)), ((0,), (0,)))}


def _dims(mode, ndim):
    return (_BDIMS if ndim == 3 else _DIMS)[mode]


def _dot(a, b, mode="nn"):
    return lax.dot_general(a.astype(BF16), b.astype(BF16), _dims(mode, a.ndim), preferred_element_type=F32)


def _bdot(a, b, mode="nn"):
    return lax.dot_general(a.astype(BF16), b.astype(BF16), _BDIMS[mode], preferred_element_type=F32)


def _split2(x):
    hi = x.astype(BF16)
    lo = (x - hi.astype(F32)).astype(BF16)
    return hi, lo


def _split3(x):
    hi = x.astype(BF16)
    r = x - hi.astype(F32)
    mid = r.astype(BF16)
    lo = (r - mid.astype(F32)).astype(BF16)
    return hi, mid, lo


def _dot_x01(x, m01, mode="nn"):
    hi, lo = _split2(x)
    return (lax.dot_general(hi, m01, _DIMS[mode], preferred_element_type=F32)
            + lax.dot_general(lo, m01, _DIMS[mode], preferred_element_type=F32))


def _dot3_raw(a, b, mode):
    ah, al = _split2(a)
    bh, bl = _split2(b)
    d = _dims(mode, a.ndim)
    return (lax.dot_general(ah, bh, d, preferred_element_type=F32)
            + lax.dot_general(ah, bl, d, preferred_element_type=F32)
            + lax.dot_general(al, bh, d, preferred_element_type=F32))


@jax.custom_vjp
def _dot3(a, b):
    return _dot3_raw(a, b, "nn")


def _dot3_fwd(a, b):
    return _dot3_raw(a, b, "nn"), (a, b)


def _dot3_bwd(res, g):
    a, b = res
    return _dot(g, b, "nt"), _dot(a, g, "tn")


_dot3.defvjp(_dot3_fwd, _dot3_bwd)


@jax.custom_vjp
def _dot1_nt(a, b):
    return _dot(a, b, "nt")


def _dot1_nt_fwd(a, b):
    return _dot(a, b, "nt"), (a, b)


def _dot1_nt_bwd(res, g):
    a, b = res
    return _dot(g, b, "nn"), _dot(g, a, "tn")


_dot1_nt.defvjp(_dot1_nt_fwd, _dot1_nt_bwd)


def _m01_left_raw(m, x):
    d = _dims("nn", x.ndim)
    if x.ndim == 3:
        m = jnp.broadcast_to(m, (x.shape[0],) + m.shape)
    p0, p1, p2 = _split3(x)
    return (lax.dot_general(m, p0, d, preferred_element_type=F32)
            + lax.dot_general(m, p1, d, preferred_element_type=F32)
            + lax.dot_general(m, p2, d, preferred_element_type=F32))


@jax.custom_vjp
def _m01_left(m, mt, x):
    return _m01_left_raw(m, x)


def _m01_left_fwd(m, mt, x):
    return _m01_left_raw(m, x), (m, mt)


def _m01_left_bwd(res, g):
    m, mt = res
    return jnp.zeros_like(m), jnp.zeros_like(mt), _m01_left_raw(mt, g)


_m01_left.defvjp(_m01_left_fwd, _m01_left_bwd)


def _softplus(x):
    return jnp.maximum(x, 0.0) + jnp.log(1.0 + jnp.exp(-jnp.abs(x)))


def _sigmoid(x):
    return 1.0 / (1.0 + jnp.exp(-x))


def _silu(x):
    return x * _sigmoid(x)


def _dsilu(x):
    s = _sigmoid(x)
    return s * (1.0 + x * (1.0 - s))


def _cols_of(n_blocks_per_part, *fixed):
    return lambda r, c: (c // n_blocks_per_part,) + fixed + (r, c % n_blocks_per_part)


def _rows_of(n_blocks_per_part, *fixed):
    return lambda r, c: (r // n_blocks_per_part,) + fixed + (r % n_blocks_per_part, c)


def _layer_of(layer):
    return lambda r, c: (layer, r, c)


def _mm(a, b, mode, name, out_dtype=F32, res=None, tm=1024, tn=512, tk=1024,
        dims=None, a_view=None, b_view=None, o_view=None, out_shape=None):
    if dims is None:
        if mode == "nn":
            (M, K), (K2, N) = a.shape, b.shape
        elif mode == "nt":
            (M, K), (N, K2) = a.shape, b.shape
        else:
            (K, M), (K2, N) = a.shape, b.shape
        assert K == K2, (name, a.shape, b.shape)
    else:
        M, N, K = dims
    tm, tn, tk = _tile(M, tm), _tile(N, tn), _tile(K, tk)
    nk = K // tk

    def spec(arr, blk, view, rc):
        view = view or (lambda r, c: (r, c))
        return pl.BlockSpec((None,) * (arr.ndim - 2) + blk, lambda i, j, k: view(*rc(i, j, k)))

    if mode == "tn":
        a_spec = spec(a, (tk, tm), a_view, lambda i, j, k: (k, i))
    else:
        a_spec = spec(a, (tm, tk), a_view, lambda i, j, k: (i, k))
    if mode == "nt":
        b_spec = spec(b, (tn, tk), b_view, lambda i, j, k: (j, k))
    else:
        b_spec = spec(b, (tk, tn), b_view, lambda i, j, k: (k, j))
    out_shape = out_shape or (M, N)
    o_spec = pl.BlockSpec((None,) * (len(out_shape) - 2) + (tm, tn),
                          lambda i, j, k: (o_view or (lambda r, c: (r, c)))(i, j))
    has_res = res is not None
    assert not (has_res and o_view), name

    def body(*refs):
        a_ref, b_ref = refs[:2]
        r_ref = refs[2] if has_res else None
        o_ref = refs[3] if has_res else refs[2]

        def finish(r):
            if has_res:
                r = r + r_ref[...]
            o_ref[...] = r.astype(out_dtype)

        part = _dot(a_ref[...], b_ref[...], mode)
        if nk == 1:
            finish(part)
            return
        acc = refs[-1]
        k = pl.program_id(2)

        @pl.when(k == 0)
        def _():
            acc[...] = part

        @pl.when((k > 0) & (k < nk - 1))
        def _():
            acc[...] += part

        @pl.when(k == nk - 1)
        def _():
            finish(acc[...] + part)

    ins = [a, b] + ([res] if has_res else [])
    in_specs = [a_spec, b_spec] + ([o_spec] if has_res else [])
    return pl.pallas_call(
        body, name=name, grid=(M // tm, N // tn, nk),
        in_specs=in_specs, out_specs=o_spec,
        out_shape=jax.ShapeDtypeStruct(out_shape, out_dtype),
        scratch_shapes=[pltpu.VMEM((tm, tn), F32)] if nk > 1 else [],
        compiler_params=_cp(("parallel", "parallel", "arbitrary")),
    )(*ins)


def _rms_fwd(x, gain, name):
    T, D = x.shape
    tt = _tile(T, 512)

    def body(x_ref, g_ref, o_ref):
        xv = x_ref[...]
        r = lax.rsqrt(jnp.mean(xv * xv, axis=-1, keepdims=True) + EPS)
        o_ref[...] = (xv * r * g_ref[...]).astype(BF16)

    return pl.pallas_call(
        body, name=name, grid=(T // tt,),
        in_specs=[pl.BlockSpec((tt, D), lambda i: (i, 0)), pl.BlockSpec((1, D), lambda i: (0, 0))],
        out_specs=pl.BlockSpec((tt, D), lambda i: (i, 0)),
        out_shape=jax.ShapeDtypeStruct((T, D), BF16),
        compiler_params=_cp(("parallel",)),
    )(x, gain)


def _rms_bwd(x, gain, dh, dres, name):
    T, D = x.shape
    tt = _tile(T, 512)

    def body(x_ref, g_ref, dh_ref, dr_ref, dx_ref, dg_ref):
        i = pl.program_id(0)
        xv = x_ref[...]
        dy = dh_ref[...].astype(F32)
        r = lax.rsqrt(jnp.mean(xv * xv, axis=-1, keepdims=True) + EPS)
        xn = xv * r
        gdy = dy * g_ref[...]
        dx = r * (gdy - xn * jnp.mean(gdy * xn, axis=-1, keepdims=True))
        dx_ref[...] = dr_ref[...] + dx

        @pl.when(i == 0)
        def _():
            dg_ref[...] = jnp.zeros_like(dg_ref)

        dg_ref[...] += jnp.sum(dy * xn, axis=0, keepdims=True)

    row = pl.BlockSpec((tt, D), lambda i: (i, 0))
    vec = pl.BlockSpec((1, D), lambda i: (0, 0))
    return pl.pallas_call(
        body, name=name, grid=(T // tt,),
        in_specs=[row, vec, row, row], out_specs=[row, vec],
        out_shape=[jax.ShapeDtypeStruct((T, D), F32), jax.ShapeDtypeStruct((1, D), F32)],
        compiler_params=_cp(("arbitrary",)),
    )(x, gain, dh, dres)


def _final_loss(x, gain, target, name):
    T, D = x.shape
    tt = _tile(T, 512)

    def body(x_ref, g_ref, t_ref, l_ref, dx_ref, dg_ref):
        i = pl.program_id(0)
        xv = x_ref[...]
        r = lax.rsqrt(jnp.mean(xv * xv, axis=-1, keepdims=True) + EPS)
        xn = xv * r
        err = xn * g_ref[...] - t_ref[...]
        dy = err * (1.0 / D)
        gdy = dy * g_ref[...]
        dx_ref[...] = r * (gdy - xn * jnp.mean(gdy * xn, axis=-1, keepdims=True))

        @pl.when(i == 0)
        def _():
            dg_ref[...] = jnp.zeros_like(dg_ref)
            l_ref[...] = jnp.zeros_like(l_ref)

        dg_ref[...] += jnp.sum(dy * xn, axis=0, keepdims=True)
        l_ref[...] += jnp.sum(jnp.sum(err * err, axis=1, keepdims=True), axis=0, keepdims=True)

    row = pl.BlockSpec((tt, D), lambda i: (i, 0))
    vec = pl.BlockSpec((1, D), lambda i: (0, 0))
    return pl.pallas_call(
        body, name=name, grid=(T // tt,),
        in_specs=[row, vec, row],
        out_specs=[pl.BlockSpec((8, LANE), lambda i: (0, 0)), row, vec],
        out_shape=[jax.ShapeDtypeStruct((8, LANE), F32), jax.ShapeDtypeStruct((T, D), F32),
                   jax.ShapeDtypeStruct((1, D), F32)],
        compiler_params=_cp(("arbitrary",)),
    )(x, gain, target)


def _shift_down(x, i, t_idx):
    if i == 0:
        return x
    return jnp.where(t_idx >= i, pltpu.roll(x, i, 0), 0.0)


def _shift_up(x, i, t_idx):
    if i == 0:
        return x
    n = x.shape[0]
    return jnp.where(t_idx < n - i, pltpu.roll(x, n - i, 0), 0.0)


def _pool_select(g, vals):
    out = vals[-1]
    for gi in range(len(vals) - 2, -1, -1):
        out = jnp.where(g == gi, vals[gi], out)
    return out


def _pool_y(u, g, t_idx):
    s1 = u + _shift_down(u, 1, t_idx)
    s2 = s1 + _shift_down(s1, 2, t_idx)
    s3 = s2 + _shift_down(s2, 4, t_idx)
    s4 = s3 + _shift_down(s3, 8, t_idx)
    ws = _pool_select(g, [s1, s2, s3, s4])
    win = _pool_select(g, [jnp.float32(w) for w in POOL_WINDOWS])
    cnt = jnp.minimum(t_idx.astype(F32) + 1.0, win)
    return ws / cnt - u, cnt


def _pool_fwd(proj, pool_w, pool_scale):
    T = proj.shape[0]
    G, C = len(POOL_WINDOWS), POOL_GROUP_DIM

    def body(u_ref, w_ref, s_ref, o_ref):
        g = pl.program_id(0)
        t_idx = lax.broadcasted_iota(jnp.int32, (T, C), 0)
        y, _ = _pool_y(u_ref[...], g, t_idx)
        o_ref[...] = _dot(y, w_ref[0]) * s_ref[...]

    return pl.pallas_call(
        body, name="pool_fwd", grid=(G,),
        in_specs=[pl.BlockSpec((T, C), lambda g: (0, g)), pl.BlockSpec((1, C, C), lambda g: (g, 0, 0)),
                  pl.BlockSpec((1, C), lambda g: (0, g))],
        out_specs=pl.BlockSpec((T, C), lambda g: (0, g)),
        out_shape=jax.ShapeDtypeStruct((T, G * C), F32),
        compiler_params=_cp(("parallel",)),
    )(proj, pool_w, pool_scale)


def _pool_bwd(proj, pool_w, pool_scale, dmix):
    T = proj.shape[0]
    G, C = len(POOL_WINDOWS), POOL_GROUP_DIM

    def body(u_ref, w_ref, s_ref, do_ref, du_ref, dw_ref, ds_ref):
        g = pl.program_id(0)
        t_idx = lax.broadcasted_iota(jnp.int32, (T, C), 0)
        y, cnt = _pool_y(u_ref[...], g, t_idx)
        w = w_ref[0]
        dout = do_ref[...]
        ds_ref[...] = jnp.sum(dout * _dot(y, w), axis=0, keepdims=True)
        dy2 = dout * s_ref[...]
        dw_ref[0] = _dot(y, dy2, "tn")
        dy = _dot(dy2, w, "nt")
        dz = dy / cnt
        r1 = dz + _shift_up(dz, 1, t_idx)
        r2 = r1 + _shift_up(r1, 2, t_idx)
        r3 = r2 + _shift_up(r2, 4, t_idx)
        r4 = r3 + _shift_up(r3, 8, t_idx)
        du_ref[...] = _pool_select(g, [r1, r2, r3, r4]) - dy

    col = pl.BlockSpec((T, C), lambda g: (0, g))
    return pl.pallas_call(
        body, name="pool_bwd", grid=(G,),
        in_specs=[col, pl.BlockSpec((1, C, C), lambda g: (g, 0, 0)), pl.BlockSpec((1, C), lambda g: (0, g)), col],
        out_specs=[col, pl.BlockSpec((1, C, C), lambda g: (g, 0, 0)), pl.BlockSpec((1, C), lambda g: (0, g))],
        out_shape=[jax.ShapeDtypeStruct((T, G * C), F32), jax.ShapeDtypeStruct((G, C, C), F32),
                   jax.ShapeDtypeStruct((1, G * C), F32)],
        compiler_params=_cp(("parallel",)),
    )(proj, pool_w, pool_scale, dmix)


SB_SCALE = SB_HEAD_DIM ** -0.5
SB_PASS_SIZES = (2, 1)


def _sb_tile_logits(qb, kblk, valid):
    z = _dot(qb, kblk, "nt")
    sp = _softplus(z)
    l1m = -sp
    if valid is not None:
        l1m = jnp.where(valid, l1m, 0.0)
    return z, sp, l1m


SB_PAIR = LANE // SB_HEAD_DIM
SB_Q0 = POOL_WIDTH // LANE
SB_NB = SB_HEADS // SB_PAIR


def _sb_head_masks():
    lane = lax.broadcasted_iota(jnp.int32, (1, LANE), 1)
    return [(lane // SB_HEAD_DIM == h).astype(F32) for h in range(SB_PAIR)]


def _sb_fwd(proj):
    T = proj.shape[0]
    B = _tile(T, 256)
    nq = T // B

    def body(q_ref, k_ref, v_ref, o_ref, l_ref, k_bf, v_bf):
        qi = pl.program_id(1)

        @pl.when(qi == 0)
        def _():
            k_bf[...] = k_ref[...].astype(BF16)
            v_bf[...] = v_ref[...].astype(BF16)

        masks = _sb_head_masks()
        q_all = q_ref[...]
        qbs = [(q_all * (m * SB_SCALE)).astype(BF16) for m in masks]
        row = lax.broadcasted_iota(jnp.int32, (B, B), 0)
        col = lax.broadcasted_iota(jnp.int32, (B, B), 1)
        later = (row > col).astype(BF16)

        def tiles(kbs, state, valid):
            ksl = [pl.ds(pl.multiple_of(kb * B, B), B) for kb in kbs]
            kblks = [k_bf[ks, :] for ks in ksl]
            logits = [[_sb_tile_logits(qb, kblk, valid) for kblk in kblks] for qb in qbs]
            within = [[_dot_x01(l1m, later) for _, _, l1m in lg] for lg in logits]
            sums = [[jnp.sum(l1m, axis=1, keepdims=True) for _, _, l1m in lg] for lg in logits]
            out = []
            for h, (carry, acc) in enumerate(state):
                for (z, sp, _), rc, s, ks in zip(logits[h], within[h], sums[h], ksl):
                    a = jnp.exp(z - sp + rc + carry)
                    if valid is not None:
                        a = jnp.where(valid, a, 0.0)
                    acc = acc + _dot(a, v_bf[ks, :])
                    carry = carry + s
                out.append((carry, acc))
            return tuple(out)

        state = tiles([qi], ((jnp.zeros((B, 1), F32), jnp.zeros((B, LANE), F32)),) * SB_PAIR, col < row)
        left = qi
        for size in SB_PASS_SIZES:
            n_pass = left // size
            state = lax.fori_loop(
                0, n_pass, lambda i, c, left=left, size=size: tiles([left - 1 - size * i - u for u in range(size)],
                                                                     c, None), state)
            left = left - n_pass * size
        o_ref[...] = sum(acc * m for (_, acc), m in zip(state, masks))
        for h, (carry, _) in enumerate(state):
            l_ref[h] = carry

    return pl.pallas_call(
        body, name="sb_fwd", grid=(SB_NB, nq),
        in_specs=[pl.BlockSpec((B, LANE), lambda hp, i: (i, SB_Q0 + hp)),
                  pl.BlockSpec((T, LANE), lambda hp, i: (0, SB_Q0 + SB_NB + hp)),
                  pl.BlockSpec((T, LANE), lambda hp, i: (0, SB_Q0 + 2 * SB_NB + hp))],
        out_specs=[pl.BlockSpec((B, LANE), lambda hp, i: (i, hp)),
                   pl.BlockSpec((SB_PAIR, B, 1), lambda hp, i: (hp, i, 0))],
        out_shape=[jax.ShapeDtypeStruct((T, SB_HEADS * SB_HEAD_DIM), F32), jax.ShapeDtypeStruct((SB_HEADS, T, 1), F32)],
        scratch_shapes=[pltpu.VMEM((T, LANE), BF16), pltpu.VMEM((T, LANE), BF16)],
        compiler_params=_cp(("parallel", "arbitrary")),
    )(proj, proj, proj)


def _sb_bwd(proj, dmix, ltot):
    T = proj.shape[0]
    B = _tile(T, 256)
    nq = T // B

    def body(q_ref, k_ref, v_ref, do_ref, l_ref, dq_ref, dk_ref, dv_ref, k_bf, v_bf):
        qi = pl.program_id(1)

        @pl.when(qi == 0)
        def _():
            k_bf[...] = k_ref[...].astype(BF16)
            v_bf[...] = v_ref[...].astype(BF16)
            dk_ref[...] = jnp.zeros_like(dk_ref)
            dv_ref[...] = jnp.zeros_like(dv_ref)

        masks = _sb_head_masks()
        q_all, do_all = q_ref[...], do_ref[...]
        qbs = [(q_all * (m * SB_SCALE)).astype(BF16) for m in masks]
        dobs = [(do_all * m).astype(BF16) for m in masks]
        ltots = [l_ref[h] for h in range(SB_PAIR)]
        row = lax.broadcasted_iota(jnp.int32, (B, B), 0)
        col = lax.broadcasted_iota(jnp.int32, (B, B), 1)
        upto = (row <= col).astype(BF16)
        before = (row < col).astype(BF16)

        def tiles(kbs, state, valid):
            ksl = [pl.ds(pl.multiple_of(kb * B, B), B) for kb in kbs]
            kblks = [k_bf[ks, :] for ks in ksl]
            vblks = [v_bf[ks, :] for ks in ksl]
            logits = [[_sb_tile_logits(qb, kblk, valid) for kblk in kblks] for qb in qbs]
            das = [[_dot(dob, vblk, "nt") for vblk in vblks] for dob in dobs]
            within = [[_dot_x01(l1m, upto) for _, _, l1m in lg] for lg in logits]
            avals, es, Ps = [], [], []
            for h, (P, _, _) in enumerate(state):
                a_h, e_h = [], []
                for (z, sp, l1m), pc, da in zip(logits[h], within[h], das[h]):
                    a = jnp.exp(z - sp + (ltots[h] - P - pc))
                    if valid is not None:
                        a = jnp.where(valid, a, 0.0)
                    a_h.append(a)
                    e_h.append(da * a)
                    P = P + jnp.sum(l1m, axis=1, keepdims=True)
                avals.append(a_h)
                es.append(e_h)
                Ps.append(P)
            e_within = [[_dot_x01(e, before) for e in e_h] for e_h in es]
            out = []
            for h, (_, E, dq) in enumerate(state):
                for (z, sp, _), e, ew, a, kblk, ks in zip(logits[h], es[h], e_within[h], avals[h], kblks, ksl):
                    dz = e * jnp.exp(-sp) - jnp.exp(z - sp) * (ew + E)
                    if valid is not None:
                        dz = jnp.where(valid, dz, 0.0)
                    dzb = dz.astype(BF16)
                    dq = dq + _dot(dzb, kblk)
                    dk_ref[ks, :] += _dot(dzb, qbs[h], "tn")
                    dv_ref[ks, :] += _dot(a, dobs[h], "tn")
                    E = E + jnp.sum(e, axis=1, keepdims=True)
                out.append((Ps[h], E, dq))
            return tuple(out)

        zeros1 = jnp.zeros((B, 1), F32)
        state = ((zeros1, zeros1, jnp.zeros((B, LANE), F32)),) * SB_PAIR
        done = 0
        for size in SB_PASS_SIZES:
            n_pass = (qi - done) // size
            state = lax.fori_loop(
                0, n_pass, lambda i, c, done=done, size=size: tiles([done + size * i + u for u in range(size)], c, None),
                state)
            done = done + n_pass * size
        state = tiles([qi], state, col < row)
        dq_ref[...] = sum(dq * (m * SB_SCALE) for (_, _, dq), m in zip(state, masks))

    qspec = pl.BlockSpec((B, LANE), lambda hp, i: (i, SB_Q0 + hp))
    wide = jax.ShapeDtypeStruct((T, SB_HEADS * SB_HEAD_DIM), F32)
    return pl.pallas_call(
        body, name="sb_bwd", grid=(SB_NB, nq),
        in_specs=[qspec,
                  pl.BlockSpec((T, LANE), lambda hp, i: (0, SB_Q0 + SB_NB + hp)),
                  pl.BlockSpec((T, LANE), lambda hp, i: (0, SB_Q0 + 2 * SB_NB + hp)),
                  qspec,
                  pl.BlockSpec((SB_PAIR, B, 1), lambda hp, i: (hp, i, 0))],
        out_specs=[pl.BlockSpec((B, LANE), lambda hp, i: (i, hp)),
                   pl.BlockSpec((T, LANE), lambda hp, i: (0, hp)),
                   pl.BlockSpec((T, LANE), lambda hp, i: (0, hp))],
        out_shape=[wide, wide, wide],
        scratch_shapes=[pltpu.VMEM((T, LANE), BF16), pltpu.VMEM((T, LANE), BF16)],
        compiler_params=_cp(("parallel", "arbitrary")),
    )(proj, proj, proj, dmix, ltot)


def _rows(w_ref, K):
    return [w_ref[i:i + 1, :] for i in range(K)]


def _conv(x, ws, t_idx):
    K = len(ws)
    y = ws[K - 1] * x
    for i in range(K - 1):
        y = y + ws[i] * _shift_down(x, K - 1 - i, t_idx)
    return y


def _conv_bwd(x, ws, dy, t_idx):
    K = len(ws)
    dx = ws[K - 1] * dy
    dws = []
    for i in range(K - 1):
        dx = dx + ws[i] * _shift_up(dy, K - 1 - i, t_idx)
        dws.append(jnp.sum(dy * _shift_down(x, K - 1 - i, t_idx), axis=0, keepdims=True))
    dws.append(jnp.sum(dy * x, axis=0, keepdims=True))
    return dx, dws


def _store_rows(ref, rows):
    for i, r in enumerate(rows):
        ref[i:i + 1, :] = r


def _ffn_act_fwd(up, conv_w, name):
    T = up.shape[0]
    F = FFN_DIM
    nb = F // LANE

    def body(g_ref, v_ref, wg_ref, wv_ref, o_ref):
        t_idx = lax.broadcasted_iota(jnp.int32, (T, LANE), 0)
        cg = _conv(g_ref[...].astype(F32), _rows(wg_ref, FFN_CONV), t_idx)
        cv = _conv(v_ref[...].astype(F32), _rows(wv_ref, FFN_CONV), t_idx)
        o_ref[...] = (_silu(cg) * cv).astype(BF16)

    return pl.pallas_call(
        body, name=name, grid=(nb,),
        in_specs=[pl.BlockSpec((T, LANE), lambda j: (0, j)), pl.BlockSpec((T, LANE), lambda j: (0, j + nb)),
                  pl.BlockSpec((FFN_CONV, LANE), lambda j: (0, j)),
                  pl.BlockSpec((FFN_CONV, LANE), lambda j: (0, j + nb))],
        out_specs=pl.BlockSpec((T, LANE), lambda j: (0, j)),
        out_shape=jax.ShapeDtypeStruct((T, F), BF16),
        compiler_params=_cp(("parallel",)),
    )(up, up, conv_w, conv_w)


def _ffn_act_bwd(up, conv_w, dact, name):
    T = up.shape[0]
    F = FFN_DIM
    nb = F // LANE

    def body(g_ref, v_ref, wg_ref, wv_ref, da_ref, dup_ref, dwg_ref, dwv_ref):
        t_idx = lax.broadcasted_iota(jnp.int32, (T, LANE), 0)
        xg, xv = g_ref[...].astype(F32), v_ref[...].astype(F32)
        wg, wv = _rows(wg_ref, FFN_CONV), _rows(wv_ref, FFN_CONV)
        cg = _conv(xg, wg, t_idx)
        cv = _conv(xv, wv, t_idx)
        da = da_ref[...].astype(F32)
        dxg, dwg = _conv_bwd(xg, wg, da * cv * _dsilu(cg), t_idx)
        dxv, dwv = _conv_bwd(xv, wv, da * _silu(cg), t_idx)
        dup_ref[0] = dxg.astype(BF16)
        dup_ref[1] = dxv.astype(BF16)
        _store_rows(dwg_ref, dwg)
        _store_rows(dwv_ref, dwv)

    col = pl.BlockSpec((T, LANE), lambda j: (0, j))
    wcol = pl.BlockSpec((FFN_CONV, LANE), lambda j: (0, j))
    return pl.pallas_call(
        body, name=name, grid=(nb,),
        in_specs=[col, pl.BlockSpec((T, LANE), lambda j: (0, j + nb)), wcol,
                  pl.BlockSpec((FFN_CONV, LANE), lambda j: (0, j + nb)), col],
        out_specs=[pl.BlockSpec((2, T, LANE), lambda j: (0, 0, j)), wcol, wcol],
        out_shape=[jax.ShapeDtypeStruct((2, T, F), BF16),
                   jax.ShapeDtypeStruct((FFN_CONV, F), F32), jax.ShapeDtypeStruct((FFN_CONV, F), F32)],
        compiler_params=_cp(("parallel",)),
    )(up, up, conv_w, conv_w, dact)


N_QK_BLOCKS = 2 * GDN_HEADS


def _gdn_pre_fwd(proj, conv_w):
    T = proj.shape[0]
    nb = 3 * GDN_HEADS

    def body(x_ref, w_ref, o_ref):
        j = pl.program_id(0)
        t_idx = lax.broadcasted_iota(jnp.int32, (T, LANE), 0)
        s = _silu(_conv(x_ref[...], _rows(w_ref, GDN_CONV), t_idx))
        rn = lax.rsqrt(jnp.sum(s * s, axis=-1, keepdims=True) + EPS)
        o_ref[...] = s * jnp.where(j < N_QK_BLOCKS, rn, 1.0)

    return pl.pallas_call(
        body, name="gdn_pre_fwd", grid=(nb,),
        in_specs=[pl.BlockSpec((T, LANE), lambda j: (0, j)), pl.BlockSpec((GDN_CONV, LANE), lambda j: (0, j))],
        out_specs=pl.BlockSpec((T, LANE), lambda j: (0, j)),
        out_shape=jax.ShapeDtypeStruct((T, nb * LANE), F32),
        compiler_params=_cp(("parallel",)),
    )(proj, conv_w)


def _gdn_pre_bwd(proj, conv_w, dout):
    T = proj.shape[0]
    nb = 3 * GDN_HEADS
    H = GDN_HEADS

    def body(x_ref, w_ref, do_ref, dx_ref, dw_ref):
        j = pl.program_id(0)
        t_idx = lax.broadcasted_iota(jnp.int32, (T, LANE), 0)
        x, w = x_ref[...], _rows(w_ref, GDN_CONV)
        c = _conv(x, w, t_idx)
        s = _silu(c)
        rn = lax.rsqrt(jnp.sum(s * s, axis=-1, keepdims=True) + EPS)
        do = do_ref[...]
        y = s * rn
        ds_normed = rn * (do - y * jnp.sum(do * y, axis=-1, keepdims=True))
        ds = jnp.where(j < N_QK_BLOCKS, ds_normed, do)
        dx, dw = _conv_bwd(x, w, ds * _dsilu(c), t_idx)
        dx_ref[...] = dx.astype(BF16)
        _store_rows(dw_ref, dw)

    col = pl.BlockSpec((T, LANE), lambda j: (0, j))
    wcol = pl.BlockSpec((GDN_CONV, LANE), lambda j: (0, j))
    return pl.pallas_call(
        body, name="gdn_pre_bwd", grid=(nb,),
        in_specs=[col, wcol, pl.BlockSpec((None, None, T, LANE), lambda j: (j // H, j % H, 0, 0))],
        out_specs=[col, wcol],
        out_shape=[jax.ShapeDtypeStruct((T, nb * LANE), BF16), jax.ShapeDtypeStruct((GDN_CONV, nb * LANE), F32)],
        compiler_params=_cp(("parallel",)),
    )(proj, conv_w, dout)


def _gdn_consts():
    C = GDN_CHUNK
    r = lax.broadcasted_iota(jnp.int32, (C, C), 0)
    c = lax.broadcasted_iota(jnp.int32, (C, C), 1)
    return dict(incl=r >= c, strict=r > c, eye=(r == c).astype(F32),
                low=(r >= c).astype(BF16), up=(r <= c).astype(BF16), ones=jnp.ones((C, C), BF16))


def _unit_lower_inverse_raw(a_mat, eye):
    inv = eye - a_mat
    pw = _dot3_raw(a_mat, a_mat, "nn")
    n_factors = a_mat.shape[-1].bit_length() - 2
    for f in range(n_factors):
        inv = inv + _dot3_raw(inv, pw, "nn")
        if f < n_factors - 1:
            pw = _dot3_raw(pw, pw, "nn")
    return inv


@jax.custom_vjp
def _unit_lower_inverse(a_mat, eye):
    return _unit_lower_inverse_raw(a_mat, eye)


def _unit_lower_inverse_fwd(a_mat, eye):
    inv = _unit_lower_inverse_raw(a_mat, eye)
    return inv, (inv, eye)


def _unit_lower_inverse_bwd(res, g):
    inv, eye = res
    return -_dot(_dot(inv, g, "tn"), inv, "nt"), jnp.zeros_like(eye)


_unit_lower_inverse.defvjp(_unit_lower_inverse_fwd, _unit_lower_inverse_bwd)


def _gdn_prep_chunk(q, k, v, b, a, alog, dtb, cs):
    n, C, dk = q.shape
    beta = _sigmoid(b)
    g = -jnp.exp(alog) * _softplus(a + dtb)
    g_sq = jnp.broadcast_to(g, (n, C, C))
    g_wide = jnp.broadcast_to(g, (n, C, dk))
    gc_i = _m01_left(cs["low"], cs["up"], g_sq)
    gc_j = _m01_left(cs["ones"], cs["ones"], g_sq * cs["up"].astype(F32))
    gc_wide = _m01_left(cs["low"], cs["up"], g_wide)
    gl_wide = _m01_left(cs["ones"], cs["ones"], g_wide)
    decay = jnp.where(cs["incl"], jnp.exp(jnp.where(cs["incl"], gc_i - gc_j, 0.0)), 0.0)
    egc = jnp.exp(gc_wide)
    qs = q * (dk ** -0.5)
    k_beta = k * beta
    a_mat = jnp.where(cs["strict"], _dot1_nt(k_beta, k) * decay, 0.0)
    inv = _unit_lower_inverse(a_mat, cs["eye"])
    u = _dot3(inv, v * beta)
    w = _dot3(inv, k_beta * egc)
    qk = _dot1_nt(qs, k) * decay
    q_dec = qs * egc
    k_dec = k * jnp.exp(gl_wide - gc_wide)
    g_last = jnp.exp(gl_wide)[:, 0:8, :]
    return qk, u, w, q_dec, k_dec, g_last


GDN_PREP_CHUNKS = 8


def _gdn_prep_specs(T):
    C, dk = GDN_CHUNK, GDN_HEAD_DIM
    npc = min(GDN_PREP_CHUNKS, T // C)
    tc = npc * C
    H = GDN_HEADS
    in_specs = [pl.BlockSpec((tc, dk), lambda h, i: (i, h)),
                pl.BlockSpec((tc, dk), lambda h, i: (i, H + h)),
                pl.BlockSpec((tc, dk), lambda h, i: (i, 2 * H + h)),
                pl.BlockSpec((1, tc, 1), lambda h, i: (h, i, 0)),
                pl.BlockSpec((1, tc, 1), lambda h, i: (h, i, 0)),
                pl.BlockSpec((1, 1, 1), lambda h, i: (h, 0, 0)),
                pl.BlockSpec((1, 1, 1), lambda h, i: (h, 0, 0))]
    xs_specs = [pl.BlockSpec((1, tc, C), lambda h, i: (h, i, 0)),
                pl.BlockSpec((1, tc, dk), lambda h, i: (h, i, 0)),
                pl.BlockSpec((1, tc, dk), lambda h, i: (h, i, 0)),
                pl.BlockSpec((1, tc, dk), lambda h, i: (h, i, 0)),
                pl.BlockSpec((1, tc, dk), lambda h, i: (h, i, 0)),
                pl.BlockSpec((1, npc * 8, dk), lambda h, i: (h, i, 0))]
    xs_shapes = [jax.ShapeDtypeStruct((H, T, C), F32)] + [jax.ShapeDtypeStruct((H, T, dk), F32)] * 4 + [
        jax.ShapeDtypeStruct((H, 8 * T // C, dk), F32)]
    return npc, tc, in_specs, xs_specs, xs_shapes


def _gdn_prep_fwd(qkv, b, a, alog, dtb):
    T = qkv.shape[0]
    C = GDN_CHUNK
    npc, tc, in_specs, xs_specs, xs_shapes = _gdn_prep_specs(T)

    def body(q_ref, k_ref, v_ref, b_ref, a_ref, al_ref, dt_ref, qk_ref, u_ref, w_ref, qd_ref, kd_ref, gl_ref):
        cs = _gdn_consts()

        def chunks(val):
            return val.reshape(npc, C, val.shape[-1])

        outs = _gdn_prep_chunk(chunks(q_ref[...]), chunks(k_ref[...]), chunks(v_ref[...]), chunks(b_ref[0]),
                               chunks(a_ref[0]), al_ref[0], dt_ref[0], cs)
        for ref, val in zip((qk_ref, u_ref, w_ref, qd_ref, kd_ref), outs[:5]):
            ref[0] = val.reshape(tc, val.shape[-1])
        gl_ref[0] = outs[5].reshape(npc * 8, outs[5].shape[-1])

    return pl.pallas_call(
        body, name="gdn_prep_fwd", grid=(GDN_HEADS, T // tc),
        in_specs=in_specs, out_specs=xs_specs, out_shape=xs_shapes,
        compiler_params=_cp(("parallel", "parallel")),
    )(qkv, qkv, qkv, b, a, alog, dtb)


def _gdn_prep_bwd(qkv, b, a, alog, dtb, dxs):
    T = qkv.shape[0]
    C, dk, H = GDN_CHUNK, GDN_HEAD_DIM, GDN_HEADS
    npc, tc, in_specs, xs_specs, _ = _gdn_prep_specs(T)

    def body(q_ref, k_ref, v_ref, b_ref, a_ref, al_ref, dt_ref, dqk_ref, du_ref, dw_ref, dqd_ref, dkd_ref, dgl_ref,
             dqkv_ref, db_ref, da_ref, dal_ref, ddt_ref):
        i = pl.program_id(1)
        cs = _gdn_consts()
        r8 = lax.broadcasted_iota(jnp.int32, (8, dk), 0)
        c8 = lax.broadcasted_iota(jnp.int32, (8, dk), 1)
        first = (r8 == 0) & (c8 == 0)

        @pl.when(i == 0)
        def _():
            dal_ref[...] = jnp.zeros_like(dal_ref)
            ddt_ref[...] = jnp.zeros_like(ddt_ref)

        def chunks(val):
            return val.reshape(npc, C, val.shape[-1])

        prim = (chunks(q_ref[...]), chunks(k_ref[...]), chunks(v_ref[...]), chunks(b_ref[0]), chunks(a_ref[0]),
                al_ref[0], dt_ref[0])
        _, vjp = jax.vjp(lambda *p: _gdn_prep_chunk(*p, cs), *prim)
        dgl = jnp.where(first, dgl_ref[0].reshape(npc, 8, dk), 0.0)
        cts = tuple(chunks(r[0]) for r in (dqk_ref, du_ref, dw_ref, dqd_ref, dkd_ref)) + (dgl,)
        dq, dkk, dv, db, da, dal, ddt = vjp(cts)
        for part, val in enumerate((dq, dkk, dv)):
            dqkv_ref[part, 0] = val.reshape(tc, dk)
        db_ref[0] = db.reshape(tc, 1)
        da_ref[0] = da.reshape(tc, 1)
        dal_ref[0] += dal
        ddt_ref[0] += ddt

    thin = pl.BlockSpec((1, tc, 1), lambda h, i: (h, i, 0))
    one = pl.BlockSpec((1, 1, 1), lambda h, i: (h, 0, 0))
    return pl.pallas_call(
        body, name="gdn_prep_bwd", grid=(H, T // tc),
        in_specs=in_specs + xs_specs,
        out_specs=[pl.BlockSpec((3, 1, tc, dk), lambda h, i: (0, h, i, 0)), thin, thin, one, one],
        out_shape=[jax.ShapeDtypeStruct((3, H, T, dk), F32)] + [jax.ShapeDtypeStruct((H, T, 1), F32)] * 2
        + [jax.ShapeDtypeStruct((H, 1, 1), F32)] * 2,
        compiler_params=_cp(("parallel", "arbitrary")),
    )(qkv, qkv, qkv, b, a, alog, dtb, *dxs)


def _gdn_scan_specs(T):
    C, dk, H = GDN_CHUNK, GDN_HEAD_DIM, GDN_HEADS
    return [pl.BlockSpec((H, C, C), lambda n: (0, n, 0))] + [pl.BlockSpec((H, C, dk), lambda n: (0, n, 0))] * 4 + [
        pl.BlockSpec((H, 8, dk), lambda n: (0, n, 0))]


def _gdn_scan_fwd(xs):
    H, T, dk = xs[1].shape
    C = GDN_CHUNK
    n = T // C

    def body(qk_ref, u_ref, w_ref, qd_ref, kd_ref, gl_ref, o_ref, s_ref, state):
        c = pl.program_id(0)

        @pl.when(c == 0)
        def _():
            state[...] = jnp.zeros_like(state)

        S = state[...]
        s_ref[0] = S
        v_new = u_ref[...] - _bdot(w_ref[...], S)
        o_ref[...] = _bdot(qd_ref[...], S) + _bdot(qk_ref[...], v_new)
        state[...] = S * jnp.tile(gl_ref[...], (1, dk // 8, 1)) + _bdot(kd_ref[...], v_new, "tn")

    return pl.pallas_call(
        body, name="gdn_scan_fwd", grid=(n,),
        in_specs=_gdn_scan_specs(T),
        out_specs=[pl.BlockSpec((H, C, dk), lambda n: (0, n, 0)), pl.BlockSpec((1, H, dk, dk), lambda n: (n, 0, 0, 0))],
        out_shape=[jax.ShapeDtypeStruct((H, T, dk), F32), jax.ShapeDtypeStruct((n, H, dk, dk), F32)],
        scratch_shapes=[pltpu.VMEM((H, dk, dk), F32)],
        compiler_params=_cp(("arbitrary",)),
    )(*xs)


def _gdn_scan_bwd(xs, states, do):
    H, T, dk = xs[1].shape
    C = GDN_CHUNK
    n = T // C

    def rev(spec_shape, f):
        return pl.BlockSpec(spec_shape, lambda i: f(n - 1 - i))

    def body(qk_ref, u_ref, w_ref, qd_ref, kd_ref, gl_ref, s_ref, do_ref,
             dqk_ref, du_ref, dw_ref, dqd_ref, dkd_ref, dgl_ref, dstate):
        i = pl.program_id(0)

        @pl.when(i == 0)
        def _():
            dstate[...] = jnp.zeros_like(dstate)

        S = s_ref[0]
        dS = dstate[...]
        do_v = do_ref[...]
        qk, w, qd, kd = qk_ref[...], w_ref[...], qd_ref[...], kd_ref[...]
        v_new = u_ref[...] - _bdot(w, S)
        dv_new = _bdot(qk, do_v, "tn") + _bdot(kd, dS)
        dqk_ref[...] = _bdot(do_v, v_new, "nt")
        dqd_ref[...] = _bdot(do_v, S, "nt")
        dkd_ref[...] = _bdot(v_new, dS, "nt")
        du_ref[...] = dv_new
        dw_ref[...] = -_bdot(dv_new, S, "nt")
        dgl = jnp.sum(jnp.sum(S * dS, axis=2, keepdims=True), axis=1, keepdims=True)
        dgl_ref[...] = jnp.broadcast_to(dgl, dgl_ref.shape)
        dstate[...] = (dS * jnp.tile(gl_ref[...], (1, dk // 8, 1)) + _bdot(qd, do_v, "tn")
                       - _bdot(w, dv_new, "tn"))

    in_specs = [rev((H, C, C), lambda m: (0, m, 0))] + [rev((H, C, dk), lambda m: (0, m, 0))] * 4 + [
        rev((H, 8, dk), lambda m: (0, m, 0)), rev((1, H, dk, dk), lambda m: (m, 0, 0, 0)),
        rev((H, C, dk), lambda m: (0, m, 0))]
    out_specs = [rev((H, C, C), lambda m: (0, m, 0))] + [rev((H, C, dk), lambda m: (0, m, 0))] * 4 + [
        rev((H, 8, dk), lambda m: (0, m, 0))]
    out_shape = [jax.ShapeDtypeStruct((H, T, C), F32)] + [jax.ShapeDtypeStruct((H, T, dk), F32)] * 4 + [
        jax.ShapeDtypeStruct((H, 8 * n, dk), F32)]
    return pl.pallas_call(
        body, name="gdn_scan_bwd", grid=(n,),
        in_specs=in_specs, out_specs=out_specs, out_shape=out_shape,
        scratch_shapes=[pltpu.VMEM((H, dk, dk), F32)],
        compiler_params=_cp(("arbitrary",)),
    )(*xs, states, do)


def _gdn_post_fwd(o, proj, norm_w):
    H, T, dk = o.shape
    tt = _tile(T, 1024)
    zoff = 3 * GDN_HEADS

    def body(o_ref, z_ref, g_ref, y_ref):
        ov = o_ref[0]
        r = lax.rsqrt(jnp.mean(ov * ov, axis=-1, keepdims=True) + EPS)
        y_ref[...] = (ov * r * g_ref[...] * _silu(z_ref[...])).astype(BF16)

    return pl.pallas_call(
        body, name="gdn_post_fwd", grid=(H, T // tt),
        in_specs=[pl.BlockSpec((1, tt, dk), lambda h, i: (h, i, 0)), pl.BlockSpec((tt, dk), lambda h, i: (i, zoff + h)),
                  pl.BlockSpec((1, dk), lambda h, i: (0, 0))],
        out_specs=pl.BlockSpec((tt, dk), lambda h, i: (i, h)),
        out_shape=jax.ShapeDtypeStruct((T, H * dk), BF16),
        compiler_params=_cp(("parallel", "parallel")),
    )(o, proj, norm_w)


def _gdn_post_bwd(o, proj, norm_w, dy):
    H, T, dk = o.shape
    tt = _tile(T, 1024)
    zoff = 3 * GDN_HEADS

    def body(o_ref, z_ref, g_ref, dy_ref, do_ref, dz_ref, dg_ref):
        i = pl.program_id(1)
        ov, z, g, dyv = o_ref[0], z_ref[...], g_ref[...], dy_ref[...]
        r = lax.rsqrt(jnp.mean(ov * ov, axis=-1, keepdims=True) + EPS)
        on = ov * r
        sz = _silu(z)
        dz_ref[...] = (dyv * on * g * _dsilu(z)).astype(BF16)
        dn = dyv * sz
        gdn = dn * g
        do_ref[0] = r * (gdn - on * jnp.mean(gdn * on, axis=-1, keepdims=True))

        @pl.when(i == 0)
        def _():
            dg_ref[...] = jnp.zeros_like(dg_ref)

        dg_ref[0] += jnp.sum(dn * on, axis=0, keepdims=True)

    return pl.pallas_call(
        body, name="gdn_post_bwd", grid=(H, T // tt),
        in_specs=[pl.BlockSpec((1, tt, dk), lambda h, i: (h, i, 0)), pl.BlockSpec((tt, dk), lambda h, i: (i, zoff + h)),
                  pl.BlockSpec((1, dk), lambda h, i: (0, 0)), pl.BlockSpec((tt, dk), lambda h, i: (i, h))],
        out_specs=[pl.BlockSpec((1, tt, dk), lambda h, i: (h, i, 0)), pl.BlockSpec((tt, dk), lambda h, i: (i, h)),
                   pl.BlockSpec((1, 1, dk), lambda h, i: (h, 0, 0))],
        out_shape=[jax.ShapeDtypeStruct((H, T, dk), F32), jax.ShapeDtypeStruct((T, H * dk), BF16),
                   jax.ShapeDtypeStruct((H, 1, dk), F32)],
        compiler_params=_cp(("parallel", "arbitrary")),
    )(o, proj, norm_w, dy)


def _ple_fwd(x, pp, gl, name):
    T, D = x.shape
    tt = _tile(T, 512)

    def body(x_ref, p_ref, g_ref, o_ref):
        o_ref[...] = x_ref[...] + p_ref[...] * _sigmoid(g_ref[...])

    row = pl.BlockSpec((tt, D), lambda i: (i, 0))
    return pl.pallas_call(
        body, name=name, grid=(T // tt,), in_specs=[row, row, row], out_specs=row,
        out_shape=jax.ShapeDtypeStruct((T, D), F32), compiler_params=_cp(("parallel",)),
    )(x, pp, gl)


def _ple_bwd(dx, pp, gl, name):
    T, D = dx.shape
    tt = _tile(T, 512)

    def body(dx_ref, p_ref, g_ref, dp_ref, dg_ref):
        s = _sigmoid(g_ref[...])
        dxv = dx_ref[...]
        dp_ref[...] = (dxv * s).astype(BF16)
        dg_ref[...] = (dxv * p_ref[...] * s * (1.0 - s)).astype(BF16)

    row = pl.BlockSpec((tt, D), lambda i: (i, 0))
    return pl.pallas_call(
        body, name=name, grid=(T // tt,), in_specs=[row, row, row], out_specs=[row, row],
        out_shape=[jax.ShapeDtypeStruct((T, D), BF16)] * 2, compiler_params=_cp(("parallel",)),
    )(dx, pp, gl)


UP_SHARD = 2 * FFN_DIM // N_CHIPS
DOWN_SHARD = FFN_DIM // N_CHIPS
GATE_SHARD = D_MODEL // N_CHIPS
IN_E_SHARD = 2 * D_MODEL // N_CHIPS


def _ffn_fwd(x, norm, W, conv_w, l):
    T = x.shape[0]
    hf = _rms_fwd(x, norm, f"ffn_norm{l}")
    up = _mm(hf, W["w_up"], "nn", f"ffn_up{l}", dims=(T, 2 * FFN_DIM, D_MODEL), b_view=_cols_of(1, l), tn=UP_SHARD,
             out_dtype=BF16)
    act = _ffn_act_fwd(up, conv_w, f"ffn_act{l}")
    x_out = _mm(act, W["w_down"], "nn", f"ffn_down{l}", dims=(T, D_MODEL, FFN_DIM), b_view=_layer_of(l), res=x,
                tn=1024, tk=1408)
    return x_out, (x, hf, up, act)


def _ffn_bwd(dx_out, saved, norm, W, conv_w, l):
    x, hf, up, act = saved
    T = x.shape[0]
    dact = _mm(dx_out, W["w_down"], "nt", f"ffn_dact{l}", dims=(T, FFN_DIM, D_MODEL), b_view=_layer_of(l),
               out_dtype=BF16, tn=1408)
    dw_down = _mm(act, dx_out, "tn", f"ffn_dwdown{l}", tm=1408, tn=1024, tk=512)
    dup, dcw_g, dcw_v = _ffn_act_bwd(up, conv_w, dact, f"ffn_dact_conv{l}")
    dw_up = _mm(hf, dup, "tn", f"ffn_dwup{l}", dims=(D_MODEL, 2 * FFN_DIM, T), b_view=_cols_of(FFN_DIM // UP_SHARD),
                o_view=_cols_of(1), out_shape=(N_CHIPS, D_MODEL, UP_SHARD), tm=1024, tn=UP_SHARD, tk=512)
    dhf = _mm(dup, W["w_up"], "nt", f"ffn_dhf{l}", dims=(T, D_MODEL, 2 * FFN_DIM),
              a_view=_cols_of(FFN_DIM // UP_SHARD), b_view=_cols_of(1, l), tn=1024, tk=UP_SHARD)
    dx, dnorm = _rms_bwd(x, norm, dhf, dx_out, f"ffn_dnorm{l}")
    return (dx, dnorm, dw_up, jnp.concatenate([dcw_g, dcw_v], axis=1),
            dw_down.reshape(N_CHIPS, DOWN_SHARD, D_MODEL))


def _ple_layer_fwd(x, p, norm, W, l):
    T = x.shape[0]
    hg = _rms_fwd(x, norm, f"ple_norm{l}")
    gl = _mm(hg, W["w_ple_gate"], "nn", f"ple_gate{l}", dims=(T, D_MODEL, D_MODEL), b_view=_layer_of(l), tn=1024)
    pp = _mm(p, W["w_ple"], "nn", f"ple_proj{l}", dims=(T, D_MODEL, PLE_DIM), a_view=_layer_of(l),
             b_view=_layer_of(l), tn=1024)
    return _ple_fwd(x, pp, gl, f"ple_mix{l}"), (x, hg, gl, pp)


def _ple_layer_bwd(dx_out, saved, p, norm, W, l):
    x, hg, gl, pp = saved
    T = x.shape[0]
    dpp, dgl = _ple_bwd(dx_out, pp, gl, f"ple_dmix{l}")
    dw_ple = _mm(p, dpp, "tn", f"ple_dwple{l}", dims=(PLE_DIM, D_MODEL, T), a_view=_layer_of(l), o_view=_cols_of(1),
                 out_shape=(N_CHIPS, PLE_DIM, PLE_DIM), tm=PLE_DIM, tn=PLE_DIM, tk=512)
    dw_gate = _mm(hg, dgl, "tn", f"ple_dwgate{l}", tm=1024, tn=1024, tk=512)
    dhg = _mm(dgl, W["w_ple_gate"], "nt", f"ple_dhg{l}", dims=(T, D_MODEL, D_MODEL), b_view=_layer_of(l), tn=1024)
    dx, dnorm = _rms_bwd(x, norm, dhg, dx_out, f"ple_dnorm{l}")
    return dx, dnorm, dw_gate.reshape(N_CHIPS, GATE_SHARD, D_MODEL), dw_ple


def _local_step(x, p, target, W):
    T = x.shape[0]
    H = GDN_HEADS
    G = {}

    hn_e = _rms_fwd(x, W["mix_norm_e"], "mix_norm_e")
    proj_e = _mm(hn_e, W["w_in_e"], "nn", "in_e", dims=(T, 2 * D_MODEL, D_MODEL), b_view=_cols_of(1), tn=IN_E_SHARD)
    pool_out = _pool_fwd(proj_e, W["pool_w"], W["pool_scale"])
    attn, ltot = _sb_fwd(proj_e)
    mix_e = jnp.concatenate([pool_out, attn], axis=1).astype(BF16)
    x1 = _mm(mix_e, W["w_out_e"], "nn", "out_e", res=x, tn=1024)
    x2, ffn0 = _ffn_fwd(x1, W["ffn_norm"][0:1], W, W["ffn_conv"][0], 0)
    x3, ple0 = _ple_layer_fwd(x2, p, W["ple_norm"][0:1], W, 0)

    hn_o = _rms_fwd(x3, W["mix_norm_o"], "mix_norm_o")
    proj_o = _mm(hn_o, W["w_in_o"], "nn", "in_o", tn=1408)
    qkv = _gdn_pre_fwd(proj_o, W["conv_qkv_o"])
    ba = proj_o[:, 4 * D_MODEL:4 * D_MODEL + 2 * H]
    b_h = ba[:, :H].T.reshape(H, T, 1)
    a_h = ba[:, H:].T.reshape(H, T, 1)
    alog = W["a_log_o"].reshape(H, 1, 1)
    dtb = W["dt_bias_o"].reshape(H, 1, 1)
    xs = _gdn_prep_fwd(qkv, b_h, a_h, alog, dtb)
    o, states = _gdn_scan_fwd(xs)
    og = _gdn_post_fwd(o, proj_o, W["gdn_norm_o"])
    x4 = _mm(og, W["w_out_o"], "nn", "out_o", res=x3, tn=1024)
    x5, ffn1 = _ffn_fwd(x4, W["ffn_norm"][1:2], W, W["ffn_conv"][1], 1)
    x6, ple1 = _ple_layer_fwd(x5, p, W["ple_norm"][1:2], W, 1)

    sq, dx6, G["final_norm"] = _final_loss(x6, W["final_norm"], target, "final_loss")

    dx5, dpn1, dwg1, dwp1 = _ple_layer_bwd(dx6, ple1, p, W["ple_norm"][1:2], W, 1)
    dx4, dfn1, dwu1, dfc1, dwd1 = _ffn_bwd(dx5, ffn1, W["ffn_norm"][1:2], W, W["ffn_conv"][1], 1)
    dog = _mm(dx4, W["w_out_o"], "nt", "d_og", tn=1024)
    G["w_out_o"] = _mm(og, dx4, "tn", "dw_out_o", tm=1024, tn=1024, tk=512).reshape(N_CHIPS, GATE_SHARD, D_MODEL)
    do, dz, dgn = _gdn_post_bwd(o, proj_o, W["gdn_norm_o"], dog)
    G["gdn_norm_o"] = jnp.sum(dgn, axis=0)
    dxs = _gdn_scan_bwd(xs, states, do)
    dqkv_act, db, da, dal, ddt = _gdn_prep_bwd(qkv, b_h, a_h, alog, dtb, dxs)
    G["a_log_o"] = dal.reshape(1, H)
    G["dt_bias_o"] = ddt.reshape(1, H)
    dqkv, G["conv_qkv_o"] = _gdn_pre_bwd(proj_o, W["conv_qkv_o"], dqkv_act)
    dba = jnp.concatenate([db.reshape(H, T).T, da.reshape(H, T).T,
                           jnp.zeros((T, ODD_IN_PAD - ODD_IN + 0), F32)], axis=1).astype(BF16)
    dproj_o = jnp.concatenate([dqkv, dz, dba], axis=1)
    G["w_in_o"] = _mm(hn_o, dproj_o, "tn", "dw_in_o", tm=1024, tn=1408, tk=512)
    dhn_o = _mm(dproj_o, W["w_in_o"], "nt", "d_hn_o", tn=1024, tk=1408)
    dx3, G["mix_norm_o"] = _rms_bwd(x3, W["mix_norm_o"], dhn_o, dx4, "d_mix_norm_o")

    dx2, dpn0, dwg0, dwp0 = _ple_layer_bwd(dx3, ple0, p, W["ple_norm"][0:1], W, 0)
    dx1, dfn0, dwu0, dfc0, dwd0 = _ffn_bwd(dx2, ffn0, W["ffn_norm"][0:1], W, W["ffn_conv"][0], 0)
    dmix = _mm(dx1, W["w_out_e"], "nt", "d_mix_e", tn=1024)
    G["w_out_e"] = _mm(mix_e, dx1, "tn", "dw_out_e", tm=1024, tn=1024, tk=512).reshape(N_CHIPS, GATE_SHARD, D_MODEL)
    du, G["pool_w"], G["pool_scale"] = _pool_bwd(proj_e, W["pool_w"], W["pool_scale"], dmix)
    dqa, dka, dva = _sb_bwd(proj_e, dmix, ltot)
    dproj_e = jnp.concatenate([du, dqa, dka, dva], axis=1).astype(BF16)
    G["w_in_e"] = _mm(hn_e, dproj_e, "tn", "dw_in_e", dims=(D_MODEL, 2 * D_MODEL, T), o_view=_cols_of(1),
                      out_shape=(N_CHIPS, D_MODEL, IN_E_SHARD), tm=1024, tn=IN_E_SHARD, tk=512)
    dhn_e = _mm(dproj_e, W["w_in_e"], "nt", "d_hn_e", dims=(T, D_MODEL, 2 * D_MODEL), b_view=_cols_of(1),
                tn=1024, tk=IN_E_SHARD)
    grad_x, G["mix_norm_e"] = _rms_bwd(x, W["mix_norm_e"], dhn_e, dx1, "d_mix_norm_e")

    G["ffn_norm"] = jnp.concatenate([dfn0, dfn1], axis=0)
    G["ple_norm"] = jnp.concatenate([dpn0, dpn1], axis=0)
    G["ffn_conv"] = jnp.stack([dfc0, dfc1])
    G["w_up"] = [dwu0, dwu1]
    G["w_down"] = [dwd0, dwd1]
    G["w_ple_gate"] = [dwg0, dwg1]
    G["w_ple"] = [dwp0, dwp1]
    return sq[0, 0], grad_x, G


BIG = ("w_in_e", "w_out_e", "w_in_o", "w_out_o", "w_up", "w_down", "w_ple_gate", "w_ple")
SHARDED_SMALL = (("mix_norm_o", 1), ("conv_qkv_o", 2), ("ffn_conv", 2))
REPLICATED = ("mix_norm_e", "pool_w", "pool_scale", "a_log_o", "dt_bias_o", "gdn_norm_o", "ffn_norm", "ple_norm",
              "final_norm")
WEIGHT_ORDER = ("mix_norm_e", "w_in_e", "pool_w", "pool_scale", "w_out_e", "mix_norm_o", "w_in_o", "conv_qkv_o",
                "a_log_o", "dt_bias_o", "gdn_norm_o", "w_out_o", "ffn_norm", "w_up", "ffn_conv", "w_down", "ple_norm",
                "w_ple_gate", "w_ple", "final_norm")
SMALL_W = LANE
SMALL_ROWS = 16


def _size(shape):
    n = 1
    for s in shape:
        n *= s
    return n


def _pack(arrs, width, granule):
    flat = jnp.concatenate([a.reshape(-1) for a in arrs])
    rows = -(-flat.shape[0] // width)
    rows = -(-rows // granule) * granule
    return jnp.pad(flat, (0, rows * width - flat.shape[0])).reshape(rows, width)


def _unpack(flat2d, shapes):
    flat = flat2d.reshape(-1)
    out, off = [], 0
    for s in shapes:
        out.append(flat[off:off + _size(s)].reshape(s))
        off += _size(s)
    return out


MESH_ID = pl.DeviceIdType.MESH
HBM_SPEC = pl.BlockSpec(memory_space=pltpu.HBM)


def _where_am_i():
    return lax.axis_index("x"), lax.axis_index("y"), lax.axis_index("c")


def _other_chips(x, y):
    return [(1 - x, y), (x, 1 - y), (1 - x, 1 - y)]


def _remote(src, dst, send_sems, recv_sems, k, to):
    return pltpu.make_async_remote_copy(src_ref=src, dst_ref=dst, send_sem=send_sems.at[k], recv_sem=recv_sems.at[k],
                                        device_id=to, device_id_type=MESH_ID)


def _chip_allgather(pack, name):
    R, Wd = pack.shape
    Rh = R // 2

    def body(src_ref, out_ref, send_sems, recv_sems, local_sem):
        x, y, c = _where_am_i()
        me, sib = (x, y, c), (x, y, 1 - c)
        chips = _other_chips(x, y)
        mine_rows = pl.ds(pl.multiple_of(c * Rh, SMALL_ROWS), Rh)
        sib_rows = pl.ds(pl.multiple_of((1 - c) * Rh, SMALL_ROWS), Rh)
        j_me = 2 * x + y
        local = pltpu.make_async_copy(src_ref, out_ref.at[j_me], local_sem)
        local.start()
        first = [_remote(src_ref.at[mine_rows], out_ref.at[j_me, mine_rows], send_sems, recv_sems, k, (cx, cy, c))
                 for k, (cx, cy) in enumerate(chips)]
        for cp in first:
            cp.start()
        passed = []
        for k, (cx, cy) in enumerate(chips):
            blk = out_ref.at[2 * cx + cy, mine_rows]
            _remote(blk, blk, send_sems, recv_sems, k, me).wait_recv()
            fw = _remote(blk, blk, send_sems, recv_sems, 3 + k, sib)
            fw.start()
            passed.append(fw)
        for k, (cx, cy) in enumerate(chips):
            blk = out_ref.at[2 * cx + cy, sib_rows]
            _remote(blk, blk, send_sems, recv_sems, 3 + k, me).wait_recv()
        for cp in first + passed:
            cp.wait_send()
        local.wait()

    return pl.pallas_call(
        body, name=name, in_specs=[HBM_SPEC], out_specs=HBM_SPEC,
        out_shape=jax.ShapeDtypeStruct((N_CHIPS, R, Wd), pack.dtype),
        scratch_shapes=[pltpu.SemaphoreType.DMA((6,)), pltpu.SemaphoreType.DMA((6,)), pltpu.SemaphoreType.DMA],
    )(pack)


def _chip_allgather_many(blocks, name):
    n = len(blocks)

    def body(*refs):
        srcs, outs = refs[:n], refs[n:2 * n]
        send_sems, recv_sems = refs[2 * n:]
        x, y, c = _where_am_i()
        me, sib = (x, y, c), (x, y, 1 - c)
        chips = _other_chips(x, y)
        j_me = 2 * x + y
        first = [_remote(srcs[p].at[c], outs[p].at[j_me, c], send_sems, recv_sems, 6 * p + k, (cx, cy, c))
                 for p in range(n) for k, (cx, cy) in enumerate(chips)]
        for cp in first:
            cp.start()
        passed = []
        for k, (cx, cy) in enumerate(chips):
            for p in range(n):
                blk = outs[p].at[2 * cx + cy, c]
                _remote(blk, blk, send_sems, recv_sems, 6 * p + k, me).wait_recv()
                fw = _remote(blk, blk, send_sems, recv_sems, 6 * p + 3 + k, sib)
                fw.start()
                passed.append(fw)
        for k, (cx, cy) in enumerate(chips):
            for p in range(n):
                blk = outs[p].at[2 * cx + cy, 1 - c]
                _remote(blk, blk, send_sems, recv_sems, 6 * p + 3 + k, me).wait_recv()
        for cp in first + passed:
            cp.wait_send()

    return pl.pallas_call(
        body, name=name, in_specs=[HBM_SPEC] * n, out_specs=[HBM_SPEC] * n,
        out_shape=[jax.ShapeDtypeStruct((N_CHIPS,) + b.shape, b.dtype) for b in blocks],
        scratch_shapes=[pltpu.SemaphoreType.DMA((6 * n,)), pltpu.SemaphoreType.DMA((6 * n,))],
    )(*blocks)


def _sibling_swap_many(pieces, name):
    n = len(pieces)

    def body(*refs):
        srcs, outs = refs[:n], refs[n:2 * n]
        send_sems, recv_sems = refs[2 * n:]
        x, y, c = _where_am_i()
        cps = [_remote(srcs[p].at[:, 1 - c], outs[p], send_sems, recv_sems, p, (x, y, 1 - c)) for p in range(n)]
        for cp in cps:
            cp.start()
        for cp in cps:
            cp.wait()

    return pl.pallas_call(
        body, name=name, in_specs=[HBM_SPEC] * n, out_specs=[HBM_SPEC] * n,
        out_shape=[jax.ShapeDtypeStruct((g.shape[0],) + g.shape[2:], g.dtype) for g in pieces],
        scratch_shapes=[pltpu.SemaphoreType.DMA((n,)), pltpu.SemaphoreType.DMA((n,))],
    )(*pieces)


def _chip_scatter_many(sums, name):
    n = len(sums)

    def body(*refs):
        srcs, outs = refs[:n], refs[n:2 * n]
        send_sems, recv_sems = refs[2 * n:]
        x, y, c = _where_am_i()
        cps = [_remote(srcs[p].at[2 * cx + cy], outs[p].at[k], send_sems, recv_sems, 3 * p + k, (cx, cy, c))
               for p in range(n) for k, (cx, cy) in enumerate(_other_chips(x, y))]
        for cp in cps:
            cp.start()
        for cp in cps:
            cp.wait()

    return pl.pallas_call(
        body, name=name, in_specs=[HBM_SPEC] * n, out_specs=[HBM_SPEC] * n,
        out_shape=[jax.ShapeDtypeStruct((N_CHIPS - 1,) + s.shape[1:], s.dtype) for s in sums],
        scratch_shapes=[pltpu.SemaphoreType.DMA((3 * n,)), pltpu.SemaphoreType.DMA((3 * n,))],
    )(*sums)


def _sibling_send_many(halves, name):
    n = len(halves)

    def body(*refs):
        srcs, outs = refs[:n], refs[n:2 * n]
        send_sems, recv_sems = refs[2 * n:]
        x, y, c = _where_am_i()
        cps = [_remote(srcs[p], outs[p], send_sems, recv_sems, p, (x, y, 1 - c)) for p in range(n)]
        for cp in cps:
            cp.start()
        for cp in cps:
            cp.wait()

    return pl.pallas_call(
        body, name=name, in_specs=[HBM_SPEC] * n, out_specs=[HBM_SPEC] * n,
        out_shape=[jax.ShapeDtypeStruct(h.shape, h.dtype) for h in halves],
        scratch_shapes=[pltpu.SemaphoreType.DMA((n,)), pltpu.SemaphoreType.DMA((n,))],
    )(*halves)


def _row_tile(rows, pref=512):
    best = 8
    for t in range(8, pref + 1, 8):
        if rows % t == 0:
            best = t
    return best


def _where_ids():
    x, y, c = _where_am_i()
    return jnp.stack([c, 2 * x + y]).astype(jnp.int32)


RS_ROWS = 256


def _chip_sums_bf16(G, A, ids, name):
    n, _, hr, cols = G.shape
    tr = _row_tile(hr, RS_ROWS)

    def body(ids_ref, g_ref, a_ref, o_ref):
        o_ref[...] = (g_ref[...] + a_ref[...]).astype(BF16)

    return pl.pallas_call(
        body, name=name,
        grid_spec=pltpu.PrefetchScalarGridSpec(
            num_scalar_prefetch=1, grid=(n, hr // tr),
            in_specs=[pl.BlockSpec((None, None, tr, cols), lambda j, i, ids: (j, ids[0], i, 0)),
                      pl.BlockSpec((None, tr, cols), lambda j, i, ids: (j, i, 0))],
            out_specs=pl.BlockSpec((None, tr, cols), lambda j, i, ids: (j, i, 0))),
        out_shape=jax.ShapeDtypeStruct((n, hr, cols), BF16),
        compiler_params=_cp(("parallel", "parallel")),
    )(ids, G, A)


def _total_half(G, A, B, ids, name):
    _, _, hr, cols = G.shape
    tr = _row_tile(hr, RS_ROWS)

    def body(ids_ref, g_ref, a_ref, b_ref, o_ref):
        s = g_ref[...] + a_ref[...]
        for k in range(N_CHIPS - 1):
            s = s + b_ref[k].astype(F32)
        o_ref[...] = s

    return pl.pallas_call(
        body, name=name,
        grid_spec=pltpu.PrefetchScalarGridSpec(
            num_scalar_prefetch=1, grid=(hr // tr,),
            in_specs=[pl.BlockSpec((None, None, tr, cols), lambda i, ids: (ids[1], ids[0], i, 0)),
                      pl.BlockSpec((None, tr, cols), lambda i, ids: (ids[1], i, 0)),
                      pl.BlockSpec((N_CHIPS - 1, tr, cols), lambda i, ids: (0, i, 0))],
            out_specs=pl.BlockSpec((tr, cols), lambda i, ids: (i, 0))),
        out_shape=jax.ShapeDtypeStruct((hr, cols), F32),
        compiler_params=_cp(("parallel",)),
    )(ids, G, A, B)


def _small_allreduce(v, name):
    R, Wd = v.shape

    def body(x_ref, sum_ref, all_ref, send_sems, recv_sems, local_sem):
        x, y, c = _where_am_i()
        me, sib = (x, y, c), (x, y, 1 - c)
        chips = _other_chips(x, y)

        def slot(px, py, pc):
            return all_ref.at[4 * px + 2 * py + pc]

        local = pltpu.make_async_copy(x_ref, slot(*me), local_sem)
        local.start()
        first = [_remote(x_ref, slot(*me), send_sems, recv_sems, 0, sib)]
        first += [_remote(x_ref, slot(*me), send_sems, recv_sems, 1 + k, (cx, cy, c)) for k, (cx, cy) in enumerate(chips)]
        for cp in first:
            cp.start()
        passed = []
        for k, (cx, cy) in enumerate(chips):
            blk = slot(cx, cy, c)
            _remote(blk, blk, send_sems, recv_sems, 1 + k, me).wait_recv()
            fw = _remote(blk, blk, send_sems, recv_sems, 4 + k, sib)
            fw.start()
            passed.append(fw)
        _remote(slot(*sib), slot(*sib), send_sems, recv_sems, 0, me).wait_recv()
        for k, (cx, cy) in enumerate(chips):
            blk = slot(cx, cy, 1 - c)
            _remote(blk, blk, send_sems, recv_sems, 4 + k, me).wait_recv()
        for cp in first + passed:
            cp.wait_send()
        local.wait()
        s = all_ref[0]
        for d in range(1, N_DEV):
            s = s + all_ref[d]
        sum_ref[...] = s

    vm = pl.BlockSpec(memory_space=pltpu.VMEM)
    return pl.pallas_call(
        body, name=name, in_specs=[vm], out_specs=[vm, vm],
        out_shape=[jax.ShapeDtypeStruct((R, Wd), F32), jax.ShapeDtypeStruct((N_DEV, R, Wd), F32)],
        scratch_shapes=[pltpu.SemaphoreType.DMA((7,)), pltpu.SemaphoreType.DMA((7,)), pltpu.SemaphoreType.DMA],
    )(v)[0]


def _adamw(w, g, m, v, name):
    L, R, Wd = w.shape
    tr = _row_tile(R, RS_ROWS)
    c1 = 1.0 - ADAM_B1 ** ADAM_STEP
    c2 = 1.0 - ADAM_B2 ** ADAM_STEP

    def body(w_ref, g_ref, m_ref, v_ref, d_ref, nm_ref, nv_ref):
        gv = g_ref[...]
        nm = ADAM_B1 * m_ref[...] + (1.0 - ADAM_B1) * gv
        nv = ADAM_B2 * v_ref[...] + (1.0 - ADAM_B2) * (gv * gv)
        d_ref[...] = -ADAM_LR * ((nm / c1) / (jnp.sqrt(nv / c2) + ADAM_EPS) + ADAM_WD * w_ref[...])
        nm_ref[...] = nm
        nv_ref[...] = nv

    row = pl.BlockSpec((None, tr, Wd), lambda l, i: (l, i, 0))
    shp = jax.ShapeDtypeStruct((L, R, Wd), F32)
    return pl.pallas_call(
        body, name=name, grid=(L, R // tr), in_specs=[row] * 4, out_specs=[row] * 3, out_shape=[shp] * 3,
        compiler_params=_cp(("parallel", "parallel")),
    )(w, g, m, v)


def _adamw_halves(w, m, v, mine, theirs, ids, name):
    L, R, Wd = w.shape
    hr = R // 2
    tr = _row_tile(hr, RS_ROWS)
    c1 = 1.0 - ADAM_B1 ** ADAM_STEP
    c2 = 1.0 - ADAM_B2 ** ADAM_STEP

    def body(ids_ref, w_ref, m_ref, v_ref, *refs):
        g_refs, (g_ref, d_ref, nm_ref, nv_ref) = refs[:2 * L], refs[2 * L:]
        layer, half = pl.program_id(0), pl.program_id(1)
        own = half == ids_ref[0]
        gv = jnp.where(own, g_refs[0][...], g_refs[L][...])
        for l in range(1, L):
            gv = jnp.where(layer == l, jnp.where(own, g_refs[l][...], g_refs[L + l][...]), gv)
        nm = ADAM_B1 * m_ref[...] + (1.0 - ADAM_B1) * gv
        nv = ADAM_B2 * v_ref[...] + (1.0 - ADAM_B2) * (gv * gv)
        g_ref[...] = gv
        d_ref[...] = -ADAM_LR * ((nm / c1) / (jnp.sqrt(nv / c2) + ADAM_EPS) + ADAM_WD * w_ref[...])
        nm_ref[...] = nm
        nv_ref[...] = nv

    blk = pl.BlockSpec((None, None, tr, Wd), lambda l, h, i, ids: (l, h, i, 0))
    g_blk = pl.BlockSpec((tr, Wd), lambda l, h, i, ids: (i, 0))
    shp = jax.ShapeDtypeStruct((L, 2, hr, Wd), F32)
    outs = pl.pallas_call(
        body, name=name,
        grid_spec=pltpu.PrefetchScalarGridSpec(
            num_scalar_prefetch=1, grid=(L, 2, hr // tr),
            in_specs=[blk] * 3 + [g_blk] * (2 * L), out_specs=[blk] * 4),
        out_shape=[shp] * 4,
        compiler_params=_cp(("parallel", "parallel", "parallel")),
    )(ids, *[a.reshape(L, 2, hr, Wd) for a in (w, m, v)], *mine, *theirs)
    return tuple(o.reshape(L, R, Wd) for o in outs)


def _two_halves(a):
    cols = a.shape[-1]
    return a.reshape(2, _size(a.shape) // (2 * cols), cols)


def _gather_weights(P):
    mine = [_two_halves(P[n].astype(BF16)) for n in BIG]
    chip = 2 * lax.axis_index("x") + lax.axis_index("y")
    gathered = {n: lax.dynamic_update_slice_in_dim(g, own[None], chip, axis=0)
                for n, g, own in zip(BIG, _chip_allgather_many(mine, "ag_weights"), mine)}
    small_shapes = [P[n].shape for n, _ in SHARDED_SMALL]
    small = _chip_allgather(_pack([P[n] for n, _ in SHARDED_SMALL], SMALL_W, SMALL_ROWS), "ag_small")
    parts = [_unpack(small[j], small_shapes) for j in range(N_CHIPS)]
    full = {n: jnp.concatenate([parts[j][i] for j in range(N_CHIPS)], axis=ax)
            for i, (n, ax) in enumerate(SHARDED_SMALL)}
    W = {n: P[n] for n in REPLICATED}
    W["pool_w"] = P["pool_w"][0]
    W["final_norm"] = P["final_norm"].reshape(1, D_MODEL)
    W["mix_norm_o"] = full["mix_norm_o"]
    W["conv_qkv_o"] = full["conv_qkv_o"][0]
    W["ffn_conv"] = full["ffn_conv"]
    W["w_in_e"] = gathered["w_in_e"].reshape(N_CHIPS, D_MODEL, IN_E_SHARD)
    W["w_out_e"] = gathered["w_out_e"].reshape(D_MODEL, D_MODEL)
    W["w_out_o"] = gathered["w_out_o"].reshape(D_MODEL, D_MODEL)
    w_in_o = gathered["w_in_o"].reshape(N_CHIPS, D_MODEL, ODD_IN // N_CHIPS)
    W["w_in_o"] = jnp.pad(jnp.concatenate([w_in_o[j] for j in range(N_CHIPS)], axis=1),
                          ((0, 0), (0, ODD_IN_PAD - ODD_IN)))
    W["w_up"] = gathered["w_up"]
    W["w_down"] = gathered["w_down"].transpose(1, 0, 2, 3).reshape(2, FFN_DIM, D_MODEL)
    W["w_ple_gate"] = gathered["w_ple_gate"].transpose(1, 0, 2, 3).reshape(2, D_MODEL, D_MODEL)
    W["w_ple"] = gathered["w_ple"].transpose(1, 2, 0, 3).reshape(2, PLE_DIM, D_MODEL)
    return W


def _reduce_big_gradients(G):
    w_in_o = G["w_in_o"]
    shard = ODD_IN // N_CHIPS
    pieces, layers_of = [], []
    for n in BIG:
        if n == "w_in_o":
            gs = [jnp.stack([w_in_o[:, j * shard:(j + 1) * shard] for j in range(N_CHIPS)])]
        else:
            gs = G[n] if isinstance(G[n], list) else [G[n]]
        layers_of.append(list(range(len(pieces), len(pieces) + len(gs))))
        pieces += [g.reshape(N_CHIPS, 2, g.shape[1] // 2, g.shape[2]) for g in gs]
    ids = _where_ids()
    from_sibling = _sibling_swap_many(pieces, "rs_sibling_swap")
    sums = [_chip_sums_bf16(g, a, ids, f"rs_chip_sums{i}") for i, (g, a) in enumerate(zip(pieces, from_sibling))]
    from_chips = _chip_scatter_many(sums, "rs_chip_scatter")
    halves = [_total_half(g, a, b, ids, f"rs_total{i}")
              for i, (g, a, b) in enumerate(zip(pieces, from_sibling, from_chips))]
    from_sibling_total = _sibling_send_many(halves, "rs_sibling_send")
    return {n: ([halves[p] for p in ps], [from_sibling_total[p] for p in ps]) for n, ps in zip(BIG, layers_of)}, ids


def kernel(x, p, mix_norm_e, w_in_e, pool_w, pool_scale, w_out_e, mix_norm_o, w_in_o, conv_qkv_o, a_log_o, dt_bias_o, gdn_norm_o, w_out_o, ffn_norm, w_up, ffn_conv, w_down, ple_norm, w_ple_gate, w_ple, final_norm, loss_target, m_mix_norm_e, m_w_in_e, m_pool_w, m_pool_scale, m_w_out_e, m_mix_norm_o, m_w_in_o, m_conv_qkv_o, m_a_log_o, m_dt_bias_o, m_gdn_norm_o, m_w_out_o, m_ffn_norm, m_w_up, m_ffn_conv, m_w_down, m_ple_norm, m_w_ple_gate, m_w_ple, m_final_norm, v_mix_norm_e, v_w_in_e, v_pool_w, v_pool_scale, v_w_out_e, v_mix_norm_o, v_w_in_o, v_conv_qkv_o, v_a_log_o, v_dt_bias_o, v_gdn_norm_o, v_w_out_o, v_ffn_norm, v_w_up, v_ffn_conv, v_w_down, v_ple_norm, v_w_ple_gate, v_w_ple, v_final_norm):
    args = locals()
    P = {n: args[n] for n in WEIGHT_ORDER}
    M = {n: args["m_" + n] for n in WEIGHT_ORDER}
    V = {n: args["v_" + n] for n in WEIGHT_ORDER}

    W = _gather_weights(P)
    T = x.shape[1]
    sq, grad_x, G = _local_step(x.reshape(T, D_MODEL), p.reshape(2, T, PLE_DIM), loss_target.reshape(T, D_MODEL), W)
    out = {}
    reduced, ids = _reduce_big_gradients(G)
    for n, (mine, theirs) in reduced.items():
        out[n] = _adamw_halves(P[n], M[n], V[n], mine, theirs, ids, f"adamw_{n}")

    small_full = {n: G[n] for n in REPLICATED}
    small_full["pool_w"] = G["pool_w"][None]
    small_full["final_norm"] = G["final_norm"].reshape(D_MODEL)
    small_full["mix_norm_o"] = G["mix_norm_o"]
    small_full["conv_qkv_o"] = G["conv_qkv_o"][None]
    small_full["ffn_conv"] = G["ffn_conv"]
    small_names = REPLICATED + tuple(n for n, _ in SHARDED_SMALL)
    summed = _small_allreduce(_pack([small_full[n] for n in small_names] + [sq.reshape(1)], SMALL_W, 8), "ar_small")
    *g_list, sq_total = _unpack(summed, [small_full[n].shape for n in small_names] + [(1,)])
    g_small = dict(zip(small_names, g_list))
    chip = 2 * lax.axis_index("x") + lax.axis_index("y")
    for n, ax in SHARDED_SMALL:
        width = P[n].shape[ax]
        g_small[n] = lax.dynamic_slice_in_dim(g_small[n], chip * width, width, axis=ax)

    def pack_small(D):
        return _pack([D[n] for n in small_names], SMALL_W, RS_ROWS)[None]

    g_pack = pack_small(g_small)
    upd = _adamw(pack_small(P), g_pack, pack_small(M), pack_small(V), "adamw_small")
    shapes = [P[n].shape for n in small_names]
    for n, *vals in zip(small_names, *[_unpack(a[0], shapes) for a in (g_pack,) + tuple(upd)]):
        out[n] = tuple(vals)

    loss = (0.5 / D_MODEL) * sq_total[0]
    return (loss, grad_x[None]) + tuple(out[n][i] for i in range(4) for n in WEIGHT_ORDER)
```

```python
import functools

import jax
import jax.numpy as jnp
from jax import lax
from jax.experimental import pallas as pl
from jax.experimental.pallas import tpu as pltpu

F32 = jnp.float32
BF16 = jnp.bfloat16

D_MODEL = 1024
PLE_DIM = 256
POOL_WIDTH = 512
POOL_WINDOWS = (2, 4, 8, 16)
POOL_GROUP_DIM = 128
SB_HEADS = 8
SB_HEAD_DIM = 64
GDN_HEADS = 8
GDN_HEAD_DIM = 128
GDN_CONV = 4
GDN_CHUNK = 64
FFN_DIM = 2816
FFN_CONV = 3
EPS = 1e-6
ODD_IN = 4 * D_MODEL + 2 * GDN_HEADS
ODD_IN_PAD = 33 * 128
ADAM_LR, ADAM_B1, ADAM_B2, ADAM_EPS, ADAM_WD, ADAM_STEP = 0.001, 0.9, 0.999, 1e-08, 0.01, 10

LANE = 128
VMEM_LIMIT = 56 * 1024 * 1024

N_CHIPS = 4
N_DEV = 8


def _cp(sem=None):
    return pltpu.CompilerParams(dimension_semantics=sem, vmem_limit_bytes=VMEM_LIMIT)


def _tile(n, pref):
    if n <= pref:
        return n
    best = None
    for t in range(LANE, pref + 1, LANE):
        if n % t == 0:
            best = t
    assert best is not None, (n, pref)
    return best


_DIMS = {"nn": (((1,), (0,)), ((), ())), "nt": (((1,), (1,)), ((), ())), "tn": (((0,), (0,)), ((), ()))}
_BDIMS = {"nn": (((2,), (1,)), ((0,), (0,))), "nt": (((2,), (2,)), ((0,), (0,))), "tn": (((1,), (1,)), ((0,), (0,)))}


def _dims(mode, ndim):
    return (_BDIMS if ndim == 3 else _DIMS)[mode]


def _dot(a, b, mode="nn"):
    return lax.dot_general(a.astype(BF16), b.astype(BF16), _dims(mode, a.ndim), preferred_element_type=F32)


def _bdot(a, b, mode="nn"):
    return lax.dot_general(a.astype(BF16), b.astype(BF16), _BDIMS[mode], preferred_element_type=F32)


def _split2(x):
    hi = x.astype(BF16)
    lo = (x - hi.astype(F32)).astype(BF16)
    return hi, lo


def _split3(x):
    hi = x.astype(BF16)
    r = x - hi.astype(F32)
    mid = r.astype(BF16)
    lo = (r - mid.astype(F32)).astype(BF16)
    return hi, mid, lo


def _dot_x01(x, m01, mode="nn"):
    hi, lo = _split2(x)
    return (lax.dot_general(hi, m01, _DIMS[mode], preferred_element_type=F32)
            + lax.dot_general(lo, m01, _DIMS[mode], preferred_element_type=F32))


def _dot3_raw(a, b, mode):
    ah, al = _split2(a)
    bh, bl = _split2(b)
    d = _dims(mode, a.ndim)
    return (lax.dot_general(ah, bh, d, preferred_element_type=F32)
            + lax.dot_general(ah, bl, d, preferred_element_type=F32)
            + lax.dot_general(al, bh, d, preferred_element_type=F32))


@jax.custom_vjp
def _dot3(a, b):
    return _dot3_raw(a, b, "nn")


def _dot3_fwd(a, b):
    return _dot3_raw(a, b, "nn"), (a, b)


def _dot3_bwd(res, g):
    a, b = res
    return _dot(g, b, "nt"), _dot(a, g, "tn")


_dot3.defvjp(_dot3_fwd, _dot3_bwd)


@jax.custom_vjp
def _dot1_nt(a, b):
    return _dot(a, b, "nt")


def _dot1_nt_fwd(a, b):
    return _dot(a, b, "nt"), (a, b)


def _dot1_nt_bwd(res, g):
    a, b = res
    return _dot(g, b, "nn"), _dot(g, a, "tn")


_dot1_nt.defvjp(_dot1_nt_fwd, _dot1_nt_bwd)


def _m01_left_raw(m, x):
    d = _dims("nn", x.ndim)
    if x.ndim == 3:
        m = jnp.broadcast_to(m, (x.shape[0],) + m.shape)
    p0, p1, p2 = _split3(x)
    return (lax.dot_general(m, p0, d, preferred_element_type=F32)
            + lax.dot_general(m, p1, d, preferred_element_type=F32)
            + lax.dot_general(m, p2, d, preferred_element_type=F32))


@jax.custom_vjp
def _m01_left(m, mt, x):
    return _m01_left_raw(m, x)


def _m01_left_fwd(m, mt, x):
    return _m01_left_raw(m, x), (m, mt)


def _m01_left_bwd(res, g):
    m, mt = res
    return jnp.zeros_like(m), jnp.zeros_like(mt), _m01_left_raw(mt, g)


_m01_left.defvjp(_m01_left_fwd, _m01_left_bwd)


def _softplus(x):
    return jnp.maximum(x, 0.0) + jnp.log(1.0 + jnp.exp(-jnp.abs(x)))


def _sigmoid(x):
    return 1.0 / (1.0 + jnp.exp(-x))


def _silu(x):
    return x * _sigmoid(x)


def _dsilu(x):
    s = _sigmoid(x)
    return s * (1.0 + x * (1.0 - s))


def _cols_of(n_blocks_per_part, *fixed):
    return lambda r, c: (c // n_blocks_per_part,) + fixed + (r, c % n_blocks_per_part)


def _rows_of(n_blocks_per_part, *fixed):
    return lambda r, c: (r // n_blocks_per_part,) + fixed + (r % n_blocks_per_part, c)


def _layer_of(layer):
    return lambda r, c: (layer, r, c)


def _mm(a, b, mode, name, out_dtype=F32, res=None, tm=1024, tn=512, tk=1024,
        dims=None, a_view=None, b_view=None, o_view=None, out_shape=None):
    if dims is None:
        if mode == "nn":
            (M, K), (K2, N) = a.shape, b.shape
        elif mode == "nt":
            (M, K), (N, K2) = a.shape, b.shape
        else:
            (K, M), (K2, N) = a.shape, b.shape
        assert K == K2, (name, a.shape, b.shape)
    else:
        M, N, K = dims
    tm, tn, tk = _tile(M, tm), _tile(N, tn), _tile(K, tk)
    nk = K // tk

    def spec(arr, blk, view, rc):
        view = view or (lambda r, c: (r, c))
        return pl.BlockSpec((None,) * (arr.ndim - 2) + blk, lambda i, j, k: view(*rc(i, j, k)))

    if mode == "tn":
        a_spec = spec(a, (tk, tm), a_view, lambda i, j, k: (k, i))
    else:
        a_spec = spec(a, (tm, tk), a_view, lambda i, j, k: (i, k))
    if mode == "nt":
        b_spec = spec(b, (tn, tk), b_view, lambda i, j, k: (j, k))
    else:
        b_spec = spec(b, (tk, tn), b_view, lambda i, j, k: (k, j))
    out_shape = out_shape or (M, N)
    o_spec = pl.BlockSpec((None,) * (len(out_shape) - 2) + (tm, tn),
                          lambda i, j, k: (o_view or (lambda r, c: (r, c)))(i, j))
    has_res = res is not None
    assert not (has_res and o_view), name

    def body(*refs):
        a_ref, b_ref = refs[:2]
        r_ref = refs[2] if has_res else None
        o_ref = refs[3] if has_res else refs[2]

        def finish(r):
            if has_res:
                r = r + r_ref[...]
            o_ref[...] = r.astype(out_dtype)

        if nk == 1:
            finish(_dot(a_ref[...], b_ref[...], mode))
            return
        acc = refs[-1]
        k = pl.program_id(2)

        @pl.when(k == 0)
        def _():
            acc[...] = jnp.zeros_like(acc)

        acc[...] += _dot(a_ref[...], b_ref[...], mode)

        @pl.when(k == nk - 1)
        def _():
            finish(acc[...])

    ins = [a, b] + ([res] if has_res else [])
    in_specs = [a_spec, b_spec] + ([o_spec] if has_res else [])
    return pl.pallas_call(
        body, name=name, grid=(M // tm, N // tn, nk),
        in_specs=in_specs, out_specs=o_spec,
        out_shape=jax.ShapeDtypeStruct(out_shape, out_dtype),
        scratch_shapes=[pltpu.VMEM((tm, tn), F32)] if nk > 1 else [],
        compiler_params=_cp(("parallel", "parallel", "arbitrary")),
    )(*ins)


def _rms_fwd(x, gain, name, after=None):
    T, D = x.shape
    tt = _tile(T, 512)

    def body(x_ref, g_ref, *rest):
        o_ref = rest[-1]
        xv = x_ref[...]
        r = lax.rsqrt(jnp.mean(xv * xv, axis=-1, keepdims=True) + EPS)
        o_ref[...] = (xv * r * g_ref[...]).astype(BF16)

    ordered = [] if after is None else [after]
    return pl.pallas_call(
        body, name=name, grid=(T // tt,),
        in_specs=[pl.BlockSpec((tt, D), lambda i: (i, 0)), pl.BlockSpec((1, D), lambda i: (0, 0))]
        + [pl.BlockSpec((8, LANE), lambda i: (0, 0)) for _ in ordered],
        out_specs=pl.BlockSpec((tt, D), lambda i: (i, 0)),
        out_shape=jax.ShapeDtypeStruct((T, D), BF16),
        compiler_params=_cp(("parallel",)),
    )(x, gain, *ordered)


def _rms_bwd(x, gain, dh, dres, name):
    T, D = x.shape
    tt = _tile(T, 512)

    def body(x_ref, g_ref, dh_ref, dr_ref, dx_ref, dg_ref):
        i = pl.program_id(0)
        xv = x_ref[...]
        dy = dh_ref[...].astype(F32)
        r = lax.rsqrt(jnp.mean(xv * xv, axis=-1, keepdims=True) + EPS)
        xn = xv * r
        gdy = dy * g_ref[...]
        dx = r * (gdy - xn * jnp.mean(gdy * xn, axis=-1, keepdims=True))
        dx_ref[...] = dr_ref[...] + dx

        @pl.when(i == 0)
        def _():
            dg_ref[...] = jnp.zeros_like(dg_ref)

        dg_ref[...] += jnp.sum(dy * xn, axis=0, keepdims=True)

    row = pl.BlockSpec((tt, D), lambda i: (i, 0))
    vec = pl.BlockSpec((1, D), lambda i: (0, 0))
    return pl.pallas_call(
        body, name=name, grid=(T // tt,),
        in_specs=[row, vec, row, row], out_specs=[row, vec],
        out_shape=[jax.ShapeDtypeStruct((T, D), F32), jax.ShapeDtypeStruct((1, D), F32)],
        compiler_params=_cp(("arbitrary",)),
    )(x, gain, dh, dres)


def _final_loss(x, gain, target, name):
    T, D = x.shape
    tt = _tile(T, 512)

    def body(x_ref, g_ref, t_ref, l_ref, dx_ref, dg_ref):
        i = pl.program_id(0)
        xv = x_ref[...]
        r = lax.rsqrt(jnp.mean(xv * xv, axis=-1, keepdims=True) + EPS)
        xn = xv * r
        err = xn * g_ref[...] - t_ref[...]
        dy = err * (1.0 / D)
        gdy = dy * g_ref[...]
        dx_ref[...] = r * (gdy - xn * jnp.mean(gdy * xn, axis=-1, keepdims=True))

        @pl.when(i == 0)
        def _():
            dg_ref[...] = jnp.zeros_like(dg_ref)
            l_ref[...] = jnp.zeros_like(l_ref)

        dg_ref[...] += jnp.sum(dy * xn, axis=0, keepdims=True)
        l_ref[...] += jnp.sum(jnp.sum(err * err, axis=1, keepdims=True), axis=0, keepdims=True)

    row = pl.BlockSpec((tt, D), lambda i: (i, 0))
    vec = pl.BlockSpec((1, D), lambda i: (0, 0))
    return pl.pallas_call(
        body, name=name, grid=(T // tt,),
        in_specs=[row, vec, row],
        out_specs=[pl.BlockSpec((8, LANE), lambda i: (0, 0)), row, vec],
        out_shape=[jax.ShapeDtypeStruct((8, LANE), F32), jax.ShapeDtypeStruct((T, D), F32),
                   jax.ShapeDtypeStruct((1, D), F32)],
        compiler_params=_cp(("arbitrary",)),
    )(x, gain, target)


def _shift_down(x, i, t_idx):
    if i == 0:
        return x
    return jnp.where(t_idx >= i, pltpu.roll(x, i, 0), 0.0)


def _shift_up(x, i, t_idx):
    if i == 0:
        return x
    n = x.shape[0]
    return jnp.where(t_idx < n - i, pltpu.roll(x, n - i, 0), 0.0)


def _pool_select(g, vals):
    out = vals[-1]
    for gi in range(len(vals) - 2, -1, -1):
        out = jnp.where(g == gi, vals[gi], out)
    return out


def _pool_y(u, g, t_idx):
    s1 = u + _shift_down(u, 1, t_idx)
    s2 = s1 + _shift_down(s1, 2, t_idx)
    s3 = s2 + _shift_down(s2, 4, t_idx)
    s4 = s3 + _shift_down(s3, 8, t_idx)
    ws = _pool_select(g, [s1, s2, s3, s4])
    win = _pool_select(g, [jnp.float32(w) for w in POOL_WINDOWS])
    cnt = jnp.minimum(t_idx.astype(F32) + 1.0, win)
    return ws / cnt - u, cnt


def _pool_fwd(proj, pool_w, pool_scale):
    T = proj.shape[0]
    G, C = len(POOL_WINDOWS), POOL_GROUP_DIM

    def body(u_ref, w_ref, s_ref, o_ref):
        g = pl.program_id(0)
        t_idx = lax.broadcasted_iota(jnp.int32, (T, C), 0)
        y, _ = _pool_y(u_ref[...], g, t_idx)
        o_ref[...] = _dot(y, w_ref[0]) * s_ref[...]

    return pl.pallas_call(
        body, name="pool_fwd", grid=(G,),
        in_specs=[pl.BlockSpec((T, C), lambda g: (0, g)), pl.BlockSpec((1, C, C), lambda g: (g, 0, 0)),
                  pl.BlockSpec((1, C), lambda g: (0, g))],
        out_specs=pl.BlockSpec((T, C), lambda g: (0, g)),
        out_shape=jax.ShapeDtypeStruct((T, G * C), F32),
        compiler_params=_cp(("parallel",)),
    )(proj, pool_w, pool_scale)


def _pool_bwd(proj, pool_w, pool_scale, dmix):
    T = proj.shape[0]
    G, C = len(POOL_WINDOWS), POOL_GROUP_DIM

    def body(u_ref, w_ref, s_ref, do_ref, du_ref, dw_ref, ds_ref):
        g = pl.program_id(0)
        t_idx = lax.broadcasted_iota(jnp.int32, (T, C), 0)
        y, cnt = _pool_y(u_ref[...], g, t_idx)
        w = w_ref[0]
        dout = do_ref[...]
        ds_ref[...] = jnp.sum(dout * _dot(y, w), axis=0, keepdims=True)
        dy2 = dout * s_ref[...]
        dw_ref[0] = _dot(y, dy2, "tn")
        dy = _dot(dy2, w, "nt")
        dz = dy / cnt
        r1 = dz + _shift_up(dz, 1, t_idx)
        r2 = r1 + _shift_up(r1, 2, t_idx)
        r3 = r2 + _shift_up(r2, 4, t_idx)
        r4 = r3 + _shift_up(r3, 8, t_idx)
        du_ref[...] = _pool_select(g, [r1, r2, r3, r4]) - dy

    col = pl.BlockSpec((T, C), lambda g: (0, g))
    return pl.pallas_call(
        body, name="pool_bwd", grid=(G,),
        in_specs=[col, pl.BlockSpec((1, C, C), lambda g: (g, 0, 0)), pl.BlockSpec((1, C), lambda g: (0, g)), col],
        out_specs=[col, pl.BlockSpec((1, C, C), lambda g: (g, 0, 0)), pl.BlockSpec((1, C), lambda g: (0, g))],
        out_shape=[jax.ShapeDtypeStruct((T, G * C), F32), jax.ShapeDtypeStruct((G, C, C), F32),
                   jax.ShapeDtypeStruct((1, G * C), F32)],
        compiler_params=_cp(("parallel",)),
    )(proj, pool_w, pool_scale, dmix)


SB_SCALE = SB_HEAD_DIM ** -0.5
SB_PASS_SIZES = (2, 1)


def _sb_tile_logits(qb, kblk, valid):
    z = _dot(qb, kblk, "nt")
    sp = _softplus(z)
    l1m = -sp
    if valid is not None:
        l1m = jnp.where(valid, l1m, 0.0)
    return z, sp, l1m


SB_PAIR = LANE // SB_HEAD_DIM
SB_Q0 = POOL_WIDTH // LANE
SB_NB = SB_HEADS // SB_PAIR


def _sb_head_masks():
    lane = lax.broadcasted_iota(jnp.int32, (1, LANE), 1)
    return [(lane // SB_HEAD_DIM == h).astype(F32) for h in range(SB_PAIR)]


def _sb_fwd(proj):
    T = proj.shape[0]
    B = _tile(T, 256)
    nq = T // B

    def body(q_ref, k_ref, v_ref, o_ref, l_ref, k_bf, v_bf):
        qi = pl.program_id(1)

        @pl.when(qi == 0)
        def _():
            k_bf[...] = k_ref[...].astype(BF16)
            v_bf[...] = v_ref[...].astype(BF16)

        masks = _sb_head_masks()
        q_all = q_ref[...]
        qbs = [(q_all * (m * SB_SCALE)).astype(BF16) for m in masks]
        row = lax.broadcasted_iota(jnp.int32, (B, B), 0)
        col = lax.broadcasted_iota(jnp.int32, (B, B), 1)
        later = (row > col).astype(BF16)

        def tiles(kbs, state, valid):
            ksl = [pl.ds(pl.multiple_of(kb * B, B), B) for kb in kbs]
            kblks = [k_bf[ks, :] for ks in ksl]
            logits = [[_sb_tile_logits(qb, kblk, valid) for kblk in kblks] for qb in qbs]
            within = [[_dot_x01(l1m, later) for _, _, l1m in lg] for lg in logits]
            sums = [[jnp.sum(l1m, axis=1, keepdims=True) for _, _, l1m in lg] for lg in logits]
            out = []
            for h, (carry, acc) in enumerate(state):
                for (z, sp, _), rc, s, ks in zip(logits[h], within[h], sums[h], ksl):
                    a = jnp.exp(z - sp + rc + carry)
                    if valid is not None:
                        a = jnp.where(valid, a, 0.0)
                    acc = acc + _dot(a, v_bf[ks, :])
                    carry = carry + s
                out.append((carry, acc))
            return tuple(out)

        state = tiles([qi], ((jnp.zeros((B, 1), F32), jnp.zeros((B, LANE), F32)),) * SB_PAIR, col < row)
        left = qi
        for size in SB_PASS_SIZES:
            n_pass = left // size
            state = lax.fori_loop(
                0, n_pass, lambda i, c, left=left, size=size: tiles([left - 1 - size * i - u for u in range(size)],
                                                                     c, None), state)
            left = left - n_pass * size
        o_ref[...] = sum(acc * m for (_, acc), m in zip(state, masks))
        for h, (carry, _) in enumerate(state):
            l_ref[h] = carry

    return pl.pallas_call(
        body, name="sb_fwd", grid=(SB_NB, nq),
        in_specs=[pl.BlockSpec((B, LANE), lambda hp, i: (i, SB_Q0 + hp)),
                  pl.BlockSpec((T, LANE), lambda hp, i: (0, SB_Q0 + SB_NB + hp)),
                  pl.BlockSpec((T, LANE), lambda hp, i: (0, SB_Q0 + 2 * SB_NB + hp))],
        out_specs=[pl.BlockSpec((B, LANE), lambda hp, i: (i, hp)),
                   pl.BlockSpec((SB_PAIR, B, 1), lambda hp, i: (hp, i, 0))],
        out_shape=[jax.ShapeDtypeStruct((T, SB_HEADS * SB_HEAD_DIM), F32), jax.ShapeDtypeStruct((SB_HEADS, T, 1), F32)],
        scratch_shapes=[pltpu.VMEM((T, LANE), BF16), pltpu.VMEM((T, LANE), BF16)],
        compiler_params=_cp(("parallel", "arbitrary")),
    )(proj, proj, proj)


def _sb_bwd(proj, dmix, ltot):
    T = proj.shape[0]
    B = _tile(T, 256)
    nq = T // B

    def body(q_ref, k_ref, v_ref, do_ref, l_ref, dq_ref, dk_ref, dv_ref, k_bf, v_bf):
        qi = pl.program_id(1)

        @pl.when(qi == 0)
        def _():
            k_bf[...] = k_ref[...].astype(BF16)
            v_bf[...] = v_ref[...].astype(BF16)
            dk_ref[...] = jnp.zeros_like(dk_ref)
            dv_ref[...] = jnp.zeros_like(dv_ref)

        masks = _sb_head_masks()
        q_all, do_all = q_ref[...], do_ref[...]
        qbs = [(q_all * (m * SB_SCALE)).astype(BF16) for m in masks]
        dobs = [(do_all * m).astype(BF16) for m in masks]
        ltots = [l_ref[h] for h in range(SB_PAIR)]
        row = lax.broadcasted_iota(jnp.int32, (B, B), 0)
        col = lax.broadcasted_iota(jnp.int32, (B, B), 1)
        upto = (row <= col).astype(BF16)
        before = (row < col).astype(BF16)

        def tiles(kbs, state, valid):
            ksl = [pl.ds(pl.multiple_of(kb * B, B), B) for kb in kbs]
            kblks = [k_bf[ks, :] for ks in ksl]
            vblks = [v_bf[ks, :] for ks in ksl]
            logits = [[_sb_tile_logits(qb, kblk, valid) for kblk in kblks] for qb in qbs]
            das = [[_dot(dob, vblk, "nt") for vblk in vblks] for dob in dobs]
            within = [[_dot_x01(l1m, upto) for _, _, l1m in lg] for lg in logits]
            avals, es, Ps = [], [], []
            for h, (P, _, _) in enumerate(state):
                a_h, e_h = [], []
                for (z, sp, l1m), pc, da in zip(logits[h], within[h], das[h]):
                    a = jnp.exp(z - sp + (ltots[h] - P - pc))
                    if valid is not None:
                        a = jnp.where(valid, a, 0.0)
                    a_h.append(a)
                    e_h.append(da * a)
                    P = P + jnp.sum(l1m, axis=1, keepdims=True)
                avals.append(a_h)
                es.append(e_h)
                Ps.append(P)
            e_within = [[_dot_x01(e, before) for e in e_h] for e_h in es]
            out = []
            for h, (_, E, dq) in enumerate(state):
                for (z, sp, _), e, ew, a, kblk, ks in zip(logits[h], es[h], e_within[h], avals[h], kblks, ksl):
                    dz = e * jnp.exp(-sp) - jnp.exp(z - sp) * (ew + E)
                    if valid is not None:
                        dz = jnp.where(valid, dz, 0.0)
                    dzb = dz.astype(BF16)
                    dq = dq + _dot(dzb, kblk)
                    dk_ref[ks, :] += _dot(dzb, qbs[h], "tn")
                    dv_ref[ks, :] += _dot(a, dobs[h], "tn")
                    E = E + jnp.sum(e, axis=1, keepdims=True)
                out.append((Ps[h], E, dq))
            return tuple(out)

        zeros1 = jnp.zeros((B, 1), F32)
        state = ((zeros1, zeros1, jnp.zeros((B, LANE), F32)),) * SB_PAIR
        done = 0
        for size in SB_PASS_SIZES:
            n_pass = (qi - done) // size
            state = lax.fori_loop(
                0, n_pass, lambda i, c, done=done, size=size: tiles([done + size * i + u for u in range(size)], c, None),
                state)
            done = done + n_pass * size
        state = tiles([qi], state, col < row)
        dq_ref[...] = sum(dq * (m * SB_SCALE) for (_, _, dq), m in zip(state, masks))

    qspec = pl.BlockSpec((B, LANE), lambda hp, i: (i, SB_Q0 + hp))
    wide = jax.ShapeDtypeStruct((T, SB_HEADS * SB_HEAD_DIM), F32)
    return pl.pallas_call(
        body, name="sb_bwd", grid=(SB_NB, nq),
        in_specs=[qspec,
                  pl.BlockSpec((T, LANE), lambda hp, i: (0, SB_Q0 + SB_NB + hp)),
                  pl.BlockSpec((T, LANE), lambda hp, i: (0, SB_Q0 + 2 * SB_NB + hp)),
                  qspec,
                  pl.BlockSpec((SB_PAIR, B, 1), lambda hp, i: (hp, i, 0))],
        out_specs=[pl.BlockSpec((B, LANE), lambda hp, i: (i, hp)),
                   pl.BlockSpec((T, LANE), lambda hp, i: (0, hp)),
                   pl.BlockSpec((T, LANE), lambda hp, i: (0, hp))],
        out_shape=[wide, wide, wide],
        scratch_shapes=[pltpu.VMEM((T, LANE), BF16), pltpu.VMEM((T, LANE), BF16)],
        compiler_params=_cp(("parallel", "arbitrary")),
    )(proj, proj, proj, dmix, ltot)


def _rows(w_ref, K):
    return [w_ref[i:i + 1, :] for i in range(K)]


def _conv(x, ws, t_idx):
    K = len(ws)
    y = ws[K - 1] * x
    for i in range(K - 1):
        y = y + ws[i] * _shift_down(x, K - 1 - i, t_idx)
    return y


def _conv_bwd(x, ws, dy, t_idx):
    K = len(ws)
    dx = ws[K - 1] * dy
    dws = []
    for i in range(K - 1):
        dx = dx + ws[i] * _shift_up(dy, K - 1 - i, t_idx)
        dws.append(jnp.sum(dy * _shift_down(x, K - 1 - i, t_idx), axis=0, keepdims=True))
    dws.append(jnp.sum(dy * x, axis=0, keepdims=True))
    return dx, dws


def _store_rows(ref, rows):
    for i, r in enumerate(rows):
        ref[i:i + 1, :] = r


def _ffn_act_fwd(up, conv_w, name):
    T = up.shape[0]
    F = FFN_DIM
    nb = F // LANE

    def body(g_ref, v_ref, wg_ref, wv_ref, o_ref):
        t_idx = lax.broadcasted_iota(jnp.int32, (T, LANE), 0)
        cg = _conv(g_ref[...].astype(F32), _rows(wg_ref, FFN_CONV), t_idx)
        cv = _conv(v_ref[...].astype(F32), _rows(wv_ref, FFN_CONV), t_idx)
        o_ref[...] = (_silu(cg) * cv).astype(BF16)

    return pl.pallas_call(
        body, name=name, grid=(nb,),
        in_specs=[pl.BlockSpec((T, LANE), lambda j: (0, j)), pl.BlockSpec((T, LANE), lambda j: (0, j + nb)),
                  pl.BlockSpec((FFN_CONV, LANE), lambda j: (0, j)),
                  pl.BlockSpec((FFN_CONV, LANE), lambda j: (0, j + nb))],
        out_specs=pl.BlockSpec((T, LANE), lambda j: (0, j)),
        out_shape=jax.ShapeDtypeStruct((T, F), BF16),
        compiler_params=_cp(("parallel",)),
    )(up, up, conv_w, conv_w)


def _ffn_act_bwd(up, conv_w, dact, name):
    T = up.shape[0]
    F = FFN_DIM
    nb = F // LANE

    def body(g_ref, v_ref, wg_ref, wv_ref, da_ref, dup_ref, dwg_ref, dwv_ref):
        t_idx = lax.broadcasted_iota(jnp.int32, (T, LANE), 0)
        xg, xv = g_ref[...].astype(F32), v_ref[...].astype(F32)
        wg, wv = _rows(wg_ref, FFN_CONV), _rows(wv_ref, FFN_CONV)
        cg = _conv(xg, wg, t_idx)
        cv = _conv(xv, wv, t_idx)
        da = da_ref[...].astype(F32)
        dxg, dwg = _conv_bwd(xg, wg, da * cv * _dsilu(cg), t_idx)
        dxv, dwv = _conv_bwd(xv, wv, da * _silu(cg), t_idx)
        dup_ref[0] = dxg.astype(BF16)
        dup_ref[1] = dxv.astype(BF16)
        _store_rows(dwg_ref, dwg)
        _store_rows(dwv_ref, dwv)

    col = pl.BlockSpec((T, LANE), lambda j: (0, j))
    wcol = pl.BlockSpec((FFN_CONV, LANE), lambda j: (0, j))
    return pl.pallas_call(
        body, name=name, grid=(nb,),
        in_specs=[col, pl.BlockSpec((T, LANE), lambda j: (0, j + nb)), wcol,
                  pl.BlockSpec((FFN_CONV, LANE), lambda j: (0, j + nb)), col],
        out_specs=[pl.BlockSpec((2, T, LANE), lambda j: (0, 0, j)), wcol, wcol],
        out_shape=[jax.ShapeDtypeStruct((2, T, F), BF16),
                   jax.ShapeDtypeStruct((FFN_CONV, F), F32), jax.ShapeDtypeStruct((FFN_CONV, F), F32)],
        compiler_params=_cp(("parallel",)),
    )(up, up, conv_w, conv_w, dact)


N_QK_BLOCKS = 2 * GDN_HEADS


def _gdn_pre_fwd(proj, conv_w):
    T = proj.shape[0]
    nb = 3 * GDN_HEADS

    def body(x_ref, w_ref, o_ref):
        j = pl.program_id(0)
        t_idx = lax.broadcasted_iota(jnp.int32, (T, LANE), 0)
        s = _silu(_conv(x_ref[...], _rows(w_ref, GDN_CONV), t_idx))
        rn = lax.rsqrt(jnp.sum(s * s, axis=-1, keepdims=True) + EPS)
        o_ref[...] = s * jnp.where(j < N_QK_BLOCKS, rn, 1.0)

    return pl.pallas_call(
        body, name="gdn_pre_fwd", grid=(nb,),
        in_specs=[pl.BlockSpec((T, LANE), lambda j: (0, j)), pl.BlockSpec((GDN_CONV, LANE), lambda j: (0, j))],
        out_specs=pl.BlockSpec((T, LANE), lambda j: (0, j)),
        out_shape=jax.ShapeDtypeStruct((T, nb * LANE), F32),
        compiler_params=_cp(("parallel",)),
    )(proj, conv_w)


def _gdn_pre_bwd(proj, conv_w, dout):
    T = proj.shape[0]
    nb = 3 * GDN_HEADS
    H = GDN_HEADS

    def body(x_ref, w_ref, do_ref, dx_ref, dw_ref):
        j = pl.program_id(0)
        t_idx = lax.broadcasted_iota(jnp.int32, (T, LANE), 0)
        x, w = x_ref[...], _rows(w_ref, GDN_CONV)
        c = _conv(x, w, t_idx)
        s = _silu(c)
        rn = lax.rsqrt(jnp.sum(s * s, axis=-1, keepdims=True) + EPS)
        do = do_ref[...]
        y = s * rn
        ds_normed = rn * (do - y * jnp.sum(do * y, axis=-1, keepdims=True))
        ds = jnp.where(j < N_QK_BLOCKS, ds_normed, do)
        dx, dw = _conv_bwd(x, w, ds * _dsilu(c), t_idx)
        dx_ref[...] = dx.astype(BF16)
        _store_rows(dw_ref, dw)

    col = pl.BlockSpec((T, LANE), lambda j: (0, j))
    wcol = pl.BlockSpec((GDN_CONV, LANE), lambda j: (0, j))
    return pl.pallas_call(
        body, name="gdn_pre_bwd", grid=(nb,),
        in_specs=[col, wcol, pl.BlockSpec((None, None, T, LANE), lambda j: (j // H, j % H, 0, 0))],
        out_specs=[col, wcol],
        out_shape=[jax.ShapeDtypeStruct((T, nb * LANE), BF16), jax.ShapeDtypeStruct((GDN_CONV, nb * LANE), F32)],
        compiler_params=_cp(("parallel",)),
    )(proj, conv_w, dout)


def _gdn_consts():
    C = GDN_CHUNK
    r = lax.broadcasted_iota(jnp.int32, (C, C), 0)
    c = lax.broadcasted_iota(jnp.int32, (C, C), 1)
    return dict(incl=r >= c, strict=r > c, eye=(r == c).astype(F32),
                low=(r >= c).astype(BF16), up=(r <= c).astype(BF16), ones=jnp.ones((C, C), BF16))


def _unit_lower_inverse_raw(a_mat, eye):
    inv = eye - a_mat
    pw = _dot3_raw(a_mat, a_mat, "nn")
    n_factors = a_mat.shape[-1].bit_length() - 2
    for f in range(n_factors):
        inv = inv + _dot3_raw(inv, pw, "nn")
        if f < n_factors - 1:
            pw = _dot3_raw(pw, pw, "nn")
    return inv


@jax.custom_vjp
def _unit_lower_inverse(a_mat, eye):
    return _unit_lower_inverse_raw(a_mat, eye)


def _unit_lower_inverse_fwd(a_mat, eye):
    inv = _unit_lower_inverse_raw(a_mat, eye)
    return inv, (inv, eye)


def _unit_lower_inverse_bwd(res, g):
    inv, eye = res
    return -_dot(_dot(inv, g, "tn"), inv, "nt"), jnp.zeros_like(eye)


_unit_lower_inverse.defvjp(_unit_lower_inverse_fwd, _unit_lower_inverse_bwd)


def _gdn_prep_chunk(q, k, v, b, a, alog, dtb, cs):
    n, C, dk = q.shape
    beta = _sigmoid(b)
    g = -jnp.exp(alog) * _softplus(a + dtb)
    g_sq = jnp.broadcast_to(g, (n, C, C))
    g_wide = jnp.broadcast_to(g, (n, C, dk))
    gc_i = _m01_left(cs["low"], cs["up"], g_sq)
    gc_j = _m01_left(cs["ones"], cs["ones"], g_sq * cs["up"].astype(F32))
    gc_wide = _m01_left(cs["low"], cs["up"], g_wide)
    gl_wide = _m01_left(cs["ones"], cs["ones"], g_wide)
    decay = jnp.where(cs["incl"], jnp.exp(jnp.where(cs["incl"], gc_i - gc_j, 0.0)), 0.0)
    egc = jnp.exp(gc_wide)
    qs = q * (dk ** -0.5)
    k_beta = k * beta
    a_mat = jnp.where(cs["strict"], _dot1_nt(k_beta, k) * decay, 0.0)
    inv = _unit_lower_inverse(a_mat, cs["eye"])
    u = _dot3(inv, v * beta)
    w = _dot3(inv, k_beta * egc)
    qk = _dot1_nt(qs, k) * decay
    q_dec = qs * egc
    k_dec = k * jnp.exp(gl_wide - gc_wide)
    g_last = jnp.exp(gl_wide)[:, 0:8, :]
    return qk, u, w, q_dec, k_dec, g_last


GDN_PREP_CHUNKS = 8


def _gdn_prep_specs(T):
    C, dk = GDN_CHUNK, GDN_HEAD_DIM
    npc = min(GDN_PREP_CHUNKS, T // C)
    tc = npc * C
    H = GDN_HEADS
    in_specs = [pl.BlockSpec((tc, dk), lambda h, i: (i, h)),
                pl.BlockSpec((tc, dk), lambda h, i: (i, H + h)),
                pl.BlockSpec((tc, dk), lambda h, i: (i, 2 * H + h)),
                pl.BlockSpec((1, tc, 1), lambda h, i: (h, i, 0)),
                pl.BlockSpec((1, tc, 1), lambda h, i: (h, i, 0)),
                pl.BlockSpec((1, 1, 1), lambda h, i: (h, 0, 0)),
                pl.BlockSpec((1, 1, 1), lambda h, i: (h, 0, 0))]
    xs_specs = [pl.BlockSpec((1, tc, C), lambda h, i: (h, i, 0)),
                pl.BlockSpec((1, tc, dk), lambda h, i: (h, i, 0)),
                pl.BlockSpec((1, tc, dk), lambda h, i: (h, i, 0)),
                pl.BlockSpec((1, tc, dk), lambda h, i: (h, i, 0)),
                pl.BlockSpec((1, tc, dk), lambda h, i: (h, i, 0)),
                pl.BlockSpec((1, npc * 8, dk), lambda h, i: (h, i, 0))]
    xs_shapes = [jax.ShapeDtypeStruct((H, T, C), F32)] + [jax.ShapeDtypeStruct((H, T, dk), F32)] * 4 + [
        jax.ShapeDtypeStruct((H, 8 * T // C, dk), F32)]
    return npc, tc, in_specs, xs_specs, xs_shapes


def _gdn_prep_fwd(qkv, b, a, alog, dtb):
    T = qkv.shape[0]
    C = GDN_CHUNK
    npc, tc, in_specs, xs_specs, xs_shapes = _gdn_prep_specs(T)

    def body(q_ref, k_ref, v_ref, b_ref, a_ref, al_ref, dt_ref, qk_ref, u_ref, w_ref, qd_ref, kd_ref, gl_ref):
        cs = _gdn_consts()

        def chunks(val):
            return val.reshape(npc, C, val.shape[-1])

        outs = _gdn_prep_chunk(chunks(q_ref[...]), chunks(k_ref[...]), chunks(v_ref[...]), chunks(b_ref[0]),
                               chunks(a_ref[0]), al_ref[0], dt_ref[0], cs)
        for ref, val in zip((qk_ref, u_ref, w_ref, qd_ref, kd_ref), outs[:5]):
            ref[0] = val.reshape(tc, val.shape[-1])
        gl_ref[0] = outs[5].reshape(npc * 8, outs[5].shape[-1])

    return pl.pallas_call(
        body, name="gdn_prep_fwd", grid=(GDN_HEADS, T // tc),
        in_specs=in_specs, out_specs=xs_specs, out_shape=xs_shapes,
        compiler_params=_cp(("parallel", "parallel")),
    )(qkv, qkv, qkv, b, a, alog, dtb)


def _gdn_prep_bwd(qkv, b, a, alog, dtb, dxs):
    T = qkv.shape[0]
    C, dk, H = GDN_CHUNK, GDN_HEAD_DIM, GDN_HEADS
    npc, tc, in_specs, xs_specs, _ = _gdn_prep_specs(T)

    def body(q_ref, k_ref, v_ref, b_ref, a_ref, al_ref, dt_ref, dqk_ref, du_ref, dw_ref, dqd_ref, dkd_ref, dgl_ref,
             dqkv_ref, db_ref, da_ref, dal_ref, ddt_ref):
        i = pl.program_id(1)
        cs = _gdn_consts()
        r8 = lax.broadcasted_iota(jnp.int32, (8, dk), 0)
        c8 = lax.broadcasted_iota(jnp.int32, (8, dk), 1)
        first = (r8 == 0) & (c8 == 0)

        @pl.when(i == 0)
        def _():
            dal_ref[...] = jnp.zeros_like(dal_ref)
            ddt_ref[...] = jnp.zeros_like(ddt_ref)

        def chunks(val):
            return val.reshape(npc, C, val.shape[-1])

        prim = (chunks(q_ref[...]), chunks(k_ref[...]), chunks(v_ref[...]), chunks(b_ref[0]), chunks(a_ref[0]),
                al_ref[0], dt_ref[0])
        _, vjp = jax.vjp(lambda *p: _gdn_prep_chunk(*p, cs), *prim)
        dgl = jnp.where(first, dgl_ref[0].reshape(npc, 8, dk), 0.0)
        cts = tuple(chunks(r[0]) for r in (dqk_ref, du_ref, dw_ref, dqd_ref, dkd_ref)) + (dgl,)
        dq, dkk, dv, db, da, dal, ddt = vjp(cts)
        for part, val in enumerate((dq, dkk, dv)):
            dqkv_ref[part, 0] = val.reshape(tc, dk)
        db_ref[0] = db.reshape(tc, 1)
        da_ref[0] = da.reshape(tc, 1)
        dal_ref[0] += dal
        ddt_ref[0] += ddt

    thin = pl.BlockSpec((1, tc, 1), lambda h, i: (h, i, 0))
    one = pl.BlockSpec((1, 1, 1), lambda h, i: (h, 0, 0))
    return pl.pallas_call(
        body, name="gdn_prep_bwd", grid=(H, T // tc),
        in_specs=in_specs + xs_specs,
        out_specs=[pl.BlockSpec((3, 1, tc, dk), lambda h, i: (0, h, i, 0)), thin, thin, one, one],
        out_shape=[jax.ShapeDtypeStruct((3, H, T, dk), F32)] + [jax.ShapeDtypeStruct((H, T, 1), F32)] * 2
        + [jax.ShapeDtypeStruct((H, 1, 1), F32)] * 2,
        compiler_params=_cp(("parallel", "arbitrary")),
    )(qkv, qkv, qkv, b, a, alog, dtb, *dxs)


def _gdn_scan_specs(T):
    C, dk, H = GDN_CHUNK, GDN_HEAD_DIM, GDN_HEADS
    return [pl.BlockSpec((H, C, C), lambda n: (0, n, 0))] + [pl.BlockSpec((H, C, dk), lambda n: (0, n, 0))] * 4 + [
        pl.BlockSpec((H, 8, dk), lambda n: (0, n, 0))]


def _gdn_scan_fwd(xs):
    H, T, dk = xs[1].shape
    C = GDN_CHUNK
    n = T // C

    def body(qk_ref, u_ref, w_ref, qd_ref, kd_ref, gl_ref, o_ref, s_ref, state):
        c = pl.program_id(0)

        @pl.when(c == 0)
        def _():
            state[...] = jnp.zeros_like(state)

        S = state[...]
        s_ref[0] = S
        v_new = u_ref[...] - _bdot(w_ref[...], S)
        o_ref[...] = _bdot(qd_ref[...], S) + _bdot(qk_ref[...], v_new)
        state[...] = S * jnp.tile(gl_ref[...], (1, dk // 8, 1)) + _bdot(kd_ref[...], v_new, "tn")

    return pl.pallas_call(
        body, name="gdn_scan_fwd", grid=(n,),
        in_specs=_gdn_scan_specs(T),
        out_specs=[pl.BlockSpec((H, C, dk), lambda n: (0, n, 0)), pl.BlockSpec((1, H, dk, dk), lambda n: (n, 0, 0, 0))],
        out_shape=[jax.ShapeDtypeStruct((H, T, dk), F32), jax.ShapeDtypeStruct((n, H, dk, dk), F32)],
        scratch_shapes=[pltpu.VMEM((H, dk, dk), F32)],
        compiler_params=_cp(("arbitrary",)),
    )(*xs)


def _gdn_scan_bwd(xs, states, do):
    H, T, dk = xs[1].shape
    C = GDN_CHUNK
    n = T // C

    def rev(spec_shape, f):
        return pl.BlockSpec(spec_shape, lambda i: f(n - 1 - i))

    def body(qk_ref, u_ref, w_ref, qd_ref, kd_ref, gl_ref, s_ref, do_ref,
             dqk_ref, du_ref, dw_ref, dqd_ref, dkd_ref, dgl_ref, dstate):
        i = pl.program_id(0)

        @pl.when(i == 0)
        def _():
            dstate[...] = jnp.zeros_like(dstate)

        S = s_ref[0]
        dS = dstate[...]
        do_v = do_ref[...]
        qk, w, qd, kd = qk_ref[...], w_ref[...], qd_ref[...], kd_ref[...]
        v_new = u_ref[...] - _bdot(w, S)
        dv_new = _bdot(qk, do_v, "tn") + _bdot(kd, dS)
        dqk_ref[...] = _bdot(do_v, v_new, "nt")
        dqd_ref[...] = _bdot(do_v, S, "nt")
        dkd_ref[...] = _bdot(v_new, dS, "nt")
        du_ref[...] = dv_new
        dw_ref[...] = -_bdot(dv_new, S, "nt")
        dgl = jnp.sum(jnp.sum(S * dS, axis=2, keepdims=True), axis=1, keepdims=True)
        dgl_ref[...] = jnp.broadcast_to(dgl, dgl_ref.shape)
        dstate[...] = (dS * jnp.tile(gl_ref[...], (1, dk // 8, 1)) + _bdot(qd, do_v, "tn")
                       - _bdot(w, dv_new, "tn"))

    in_specs = [rev((H, C, C), lambda m: (0, m, 0))] + [rev((H, C, dk), lambda m: (0, m, 0))] * 4 + [
        rev((H, 8, dk), lambda m: (0, m, 0)), rev((1, H, dk, dk), lambda m: (m, 0, 0, 0)),
        rev((H, C, dk), lambda m: (0, m, 0))]
    out_specs = [rev((H, C, C), lambda m: (0, m, 0))] + [rev((H, C, dk), lambda m: (0, m, 0))] * 4 + [
        rev((H, 8, dk), lambda m: (0, m, 0))]
    out_shape = [jax.ShapeDtypeStruct((H, T, C), F32)] + [jax.ShapeDtypeStruct((H, T, dk), F32)] * 4 + [
        jax.ShapeDtypeStruct((H, 8 * n, dk), F32)]
    return pl.pallas_call(
        body, name="gdn_scan_bwd", grid=(n,),
        in_specs=in_specs, out_specs=out_specs, out_shape=out_shape,
        scratch_shapes=[pltpu.VMEM((H, dk, dk), F32)],
        compiler_params=_cp(("arbitrary",)),
    )(*xs, states, do)


def _gdn_post_fwd(o, proj, norm_w):
    H, T, dk = o.shape
    tt = _tile(T, 1024)
    zoff = 3 * GDN_HEADS

    def body(o_ref, z_ref, g_ref, y_ref):
        ov = o_ref[0]
        r = lax.rsqrt(jnp.mean(ov * ov, axis=-1, keepdims=True) + EPS)
        y_ref[...] = (ov * r * g_ref[...] * _silu(z_ref[...])).astype(BF16)

    return pl.pallas_call(
        body, name="gdn_post_fwd", grid=(H, T // tt),
        in_specs=[pl.BlockSpec((1, tt, dk), lambda h, i: (h, i, 0)), pl.BlockSpec((tt, dk), lambda h, i: (i, zoff + h)),
                  pl.BlockSpec((1, dk), lambda h, i: (0, 0))],
        out_specs=pl.BlockSpec((tt, dk), lambda h, i: (i, h)),
        out_shape=jax.ShapeDtypeStruct((T, H * dk), BF16),
        compiler_params=_cp(("parallel", "parallel")),
    )(o, proj, norm_w)


def _gdn_post_bwd(o, proj, norm_w, dy):
    H, T, dk = o.shape
    tt = _tile(T, 1024)
    zoff = 3 * GDN_HEADS

    def body(o_ref, z_ref, g_ref, dy_ref, do_ref, dz_ref, dg_ref):
        i = pl.program_id(1)
        ov, z, g, dyv = o_ref[0], z_ref[...], g_ref[...], dy_ref[...]
        r = lax.rsqrt(jnp.mean(ov * ov, axis=-1, keepdims=True) + EPS)
        on = ov * r
        sz = _silu(z)
        dz_ref[...] = (dyv * on * g * _dsilu(z)).astype(BF16)
        dn = dyv * sz
        gdn = dn * g
        do_ref[0] = r * (gdn - on * jnp.mean(gdn * on, axis=-1, keepdims=True))

        @pl.when(i == 0)
        def _():
            dg_ref[...] = jnp.zeros_like(dg_ref)

        dg_ref[0] += jnp.sum(dn * on, axis=0, keepdims=True)

    return pl.pallas_call(
        body, name="gdn_post_bwd", grid=(H, T // tt),
        in_specs=[pl.BlockSpec((1, tt, dk), lambda h, i: (h, i, 0)), pl.BlockSpec((tt, dk), lambda h, i: (i, zoff + h)),
                  pl.BlockSpec((1, dk), lambda h, i: (0, 0)), pl.BlockSpec((tt, dk), lambda h, i: (i, h))],
        out_specs=[pl.BlockSpec((1, tt, dk), lambda h, i: (h, i, 0)), pl.BlockSpec((tt, dk), lambda h, i: (i, h)),
                   pl.BlockSpec((1, 1, dk), lambda h, i: (h, 0, 0))],
        out_shape=[jax.ShapeDtypeStruct((H, T, dk), F32), jax.ShapeDtypeStruct((T, H * dk), BF16),
                   jax.ShapeDtypeStruct((H, 1, dk), F32)],
        compiler_params=_cp(("parallel", "arbitrary")),
    )(o, proj, norm_w, dy)


def _ple_fwd(x, pp, gl, name):
    T, D = x.shape
    tt = _tile(T, 512)

    def body(x_ref, p_ref, g_ref, o_ref):
        o_ref[...] = x_ref[...] + p_ref[...] * _sigmoid(g_ref[...])

    row = pl.BlockSpec((tt, D), lambda i: (i, 0))
    return pl.pallas_call(
        body, name=name, grid=(T // tt,), in_specs=[row, row, row], out_specs=row,
        out_shape=jax.ShapeDtypeStruct((T, D), F32), compiler_params=_cp(("parallel",)),
    )(x, pp, gl)


def _ple_bwd(dx, pp, gl, name):
    T, D = dx.shape
    tt = _tile(T, 512)

    def body(dx_ref, p_ref, g_ref, dp_ref, dg_ref):
        s = _sigmoid(g_ref[...])
        dxv = dx_ref[...]
        dp_ref[...] = (dxv * s).astype(BF16)
        dg_ref[...] = (dxv * p_ref[...] * s * (1.0 - s)).astype(BF16)

    row = pl.BlockSpec((tt, D), lambda i: (i, 0))
    return pl.pallas_call(
        body, name=name, grid=(T // tt,), in_specs=[row, row, row], out_specs=[row, row],
        out_shape=[jax.ShapeDtypeStruct((T, D), BF16)] * 2, compiler_params=_cp(("parallel",)),
    )(dx, pp, gl)


UP_SHARD = 2 * FFN_DIM // N_CHIPS
DOWN_SHARD = FFN_DIM // N_CHIPS
GATE_SHARD = D_MODEL // N_CHIPS
IN_E_SHARD = 2 * D_MODEL // N_CHIPS


def _ffn_fwd(x, norm, W, conv_w, l):
    T = x.shape[0]
    hf = _rms_fwd(x, norm, f"ffn_norm{l}")
    up = _mm(hf, W["w_up"], "nn", f"ffn_up{l}", dims=(T, 2 * FFN_DIM, D_MODEL), b_view=_cols_of(1, l), tn=UP_SHARD,
             out_dtype=BF16)
    act = _ffn_act_fwd(up, conv_w, f"ffn_act{l}")
    x_out = _mm(act, W["w_down"], "nn", f"ffn_down{l}", dims=(T, D_MODEL, FFN_DIM), b_view=_layer_of(l), res=x,
                tn=1024, tk=1408)
    return x_out, (x, hf, up, act)


def _ffn_bwd(dx_out, saved, norm, W, conv_w, l):
    x, hf, up, act = saved
    T = x.shape[0]
    dact = _mm(dx_out, W["w_down"], "nt", f"ffn_dact{l}", dims=(T, FFN_DIM, D_MODEL), b_view=_layer_of(l),
               out_dtype=BF16, tn=1408)
    dw_down = _mm(act, dx_out, "tn", f"ffn_dwdown{l}", tm=1408, tn=1024, tk=512)
    dup, dcw_g, dcw_v = _ffn_act_bwd(up, conv_w, dact, f"ffn_dact_conv{l}")
    dw_up = _mm(hf, dup, "tn", f"ffn_dwup{l}", dims=(D_MODEL, 2 * FFN_DIM, T), b_view=_cols_of(FFN_DIM // UP_SHARD),
                o_view=_cols_of(1), out_shape=(N_CHIPS, D_MODEL, UP_SHARD), tm=1024, tn=UP_SHARD, tk=512)
    dhf = _mm(dup, W["w_up"], "nt", f"ffn_dhf{l}", dims=(T, D_MODEL, 2 * FFN_DIM),
              a_view=_cols_of(FFN_DIM // UP_SHARD), b_view=_cols_of(1, l), tn=1024, tk=UP_SHARD)
    dx, dnorm = _rms_bwd(x, norm, dhf, dx_out, f"ffn_dnorm{l}")
    return (dx, dnorm, dw_up, jnp.concatenate([dcw_g, dcw_v], axis=1),
            dw_down.reshape(N_CHIPS, DOWN_SHARD, D_MODEL))


def _ple_layer_fwd(x, p, norm, W, l):
    T = x.shape[0]
    hg = _rms_fwd(x, norm, f"ple_norm{l}")
    gl = _mm(hg, W["w_ple_gate"], "nn", f"ple_gate{l}", dims=(T, D_MODEL, D_MODEL), b_view=_layer_of(l), tn=1024)
    pp = _mm(p, W["w_ple"], "nn", f"ple_proj{l}", dims=(T, D_MODEL, PLE_DIM), a_view=_layer_of(l),
             b_view=_layer_of(l), tn=1024)
    return _ple_fwd(x, pp, gl, f"ple_mix{l}"), (x, hg, gl, pp)


def _ple_layer_bwd(dx_out, saved, p, norm, W, l):
    x, hg, gl, pp = saved
    T = x.shape[0]
    dpp, dgl = _ple_bwd(dx_out, pp, gl, f"ple_dmix{l}")
    dw_ple = _mm(p, dpp, "tn", f"ple_dwple{l}", dims=(PLE_DIM, D_MODEL, T), a_view=_layer_of(l), o_view=_cols_of(1),
                 out_shape=(N_CHIPS, PLE_DIM, PLE_DIM), tm=PLE_DIM, tn=PLE_DIM, tk=512)
    dw_gate = _mm(hg, dgl, "tn", f"ple_dwgate{l}", tm=1024, tn=1024, tk=512)
    dhg = _mm(dgl, W["w_ple_gate"], "nt", f"ple_dhg{l}", dims=(T, D_MODEL, D_MODEL), b_view=_layer_of(l), tn=1024)
    dx, dnorm = _rms_bwd(x, norm, dhg, dx_out, f"ple_dnorm{l}")
    return dx, dnorm, dw_gate.reshape(N_CHIPS, GATE_SHARD, D_MODEL), dw_ple


def _local_step(x, p, target, W, token=None, later_weights=None):
    T = x.shape[0]
    H = GDN_HEADS
    G = {}

    hn_e = _rms_fwd(x, W["mix_norm_e"], "mix_norm_e", after=token)
    proj_e = _mm(hn_e, W["w_in_e"], "nn", "in_e", dims=(T, 2 * D_MODEL, D_MODEL), b_view=_cols_of(1), tn=IN_E_SHARD)
    pool_out = _pool_fwd(proj_e, W["pool_w"], W["pool_scale"])
    attn, ltot = _sb_fwd(proj_e)
    mix_e = jnp.concatenate([pool_out, attn], axis=1).astype(BF16)
    x1 = _mm(mix_e, W["w_out_e"], "nn", "out_e", res=x, tn=1024)
    if later_weights is not None:
        W = {**W, **later_weights(x1)}
    x2, ffn0 = _ffn_fwd(x1, W["ffn_norm"][0:1], W, W["ffn_conv"][0], 0)
    x3, ple0 = _ple_layer_fwd(x2, p, W["ple_norm"][0:1], W, 0)

    hn_o = _rms_fwd(x3, W["mix_norm_o"], "mix_norm_o")
    proj_o = _mm(hn_o, W["w_in_o"], "nn", "in_o", tn=1408)
    qkv = _gdn_pre_fwd(proj_o, W["conv_qkv_o"])
    ba = proj_o[:, 4 * D_MODEL:4 * D_MODEL + 2 * H]
    b_h = ba[:, :H].T.reshape(H, T, 1)
    a_h = ba[:, H:].T.reshape(H, T, 1)
    alog = W["a_log_o"].reshape(H, 1, 1)
    dtb = W["dt_bias_o"].reshape(H, 1, 1)
    xs = _gdn_prep_fwd(qkv, b_h, a_h, alog, dtb)
    o, states = _gdn_scan_fwd(xs)
    og = _gdn_post_fwd(o, proj_o, W["gdn_norm_o"])
    x4 = _mm(og, W["w_out_o"], "nn", "out_o", res=x3, tn=1024)
    x5, ffn1 = _ffn_fwd(x4, W["ffn_norm"][1:2], W, W["ffn_conv"][1], 1)
    x6, ple1 = _ple_layer_fwd(x5, p, W["ple_norm"][1:2], W, 1)

    sq, dx6, G["final_norm"] = _final_loss(x6, W["final_norm"], target, "final_loss")

    dx5, dpn1, dwg1, dwp1 = _ple_layer_bwd(dx6, ple1, p, W["ple_norm"][1:2], W, 1)
    dx4, dfn1, dwu1, dfc1, dwd1 = _ffn_bwd(dx5, ffn1, W["ffn_norm"][1:2], W, W["ffn_conv"][1], 1)
    dog = _mm(dx4, W["w_out_o"], "nt", "d_og", tn=1024)
    G["w_out_o"] = _mm(og, dx4, "tn", "dw_out_o", tm=1024, tn=1024, tk=512).reshape(N_CHIPS, GATE_SHARD, D_MODEL)
    do, dz, dgn = _gdn_post_bwd(o, proj_o, W["gdn_norm_o"], dog)
    G["gdn_norm_o"] = jnp.sum(dgn, axis=0)
    dxs = _gdn_scan_bwd(xs, states, do)
    dqkv_act, db, da, dal, ddt = _gdn_prep_bwd(qkv, b_h, a_h, alog, dtb, dxs)
    G["a_log_o"] = dal.reshape(1, H)
    G["dt_bias_o"] = ddt.reshape(1, H)
    dqkv, G["conv_qkv_o"] = _gdn_pre_bwd(proj_o, W["conv_qkv_o"], dqkv_act)
    dba = jnp.concatenate([db.reshape(H, T).T, da.reshape(H, T).T,
                           jnp.zeros((T, ODD_IN_PAD - ODD_IN + 0), F32)], axis=1).astype(BF16)
    dproj_o = jnp.concatenate([dqkv, dz, dba], axis=1)
    G["w_in_o"] = _mm(hn_o, dproj_o, "tn", "dw_in_o", tm=1024, tn=1408, tk=512)
    dhn_o = _mm(dproj_o, W["w_in_o"], "nt", "d_hn_o", tn=1024, tk=1408)
    dx3, G["mix_norm_o"] = _rms_bwd(x3, W["mix_norm_o"], dhn_o, dx4, "d_mix_norm_o")

    dx2, dpn0, dwg0, dwp0 = _ple_layer_bwd(dx3, ple0, p, W["ple_norm"][0:1], W, 0)
    dx1, dfn0, dwu0, dfc0, dwd0 = _ffn_bwd(dx2, ffn0, W["ffn_norm"][0:1], W, W["ffn_conv"][0], 0)
    dmix = _mm(dx1, W["w_out_e"], "nt", "d_mix_e", tn=1024)
    G["w_out_e"] = _mm(mix_e, dx1, "tn", "dw_out_e", tm=1024, tn=1024, tk=512).reshape(N_CHIPS, GATE_SHARD, D_MODEL)
    du, G["pool_w"], G["pool_scale"] = _pool_bwd(proj_e, W["pool_w"], W["pool_scale"], dmix)
    dqa, dka, dva = _sb_bwd(proj_e, dmix, ltot)
    dproj_e = jnp.concatenate([du, dqa, dka, dva], axis=1).astype(BF16)
    G["w_in_e"] = _mm(hn_e, dproj_e, "tn", "dw_in_e", dims=(D_MODEL, 2 * D_MODEL, T), o_view=_cols_of(1),
                      out_shape=(N_CHIPS, D_MODEL, IN_E_SHARD), tm=1024, tn=IN_E_SHARD, tk=512)
    dhn_e = _mm(dproj_e, W["w_in_e"], "nt", "d_hn_e", dims=(T, D_MODEL, 2 * D_MODEL), b_view=_cols_of(1),
                tn=1024, tk=IN_E_SHARD)
    grad_x, G["mix_norm_e"] = _rms_bwd(x, W["mix_norm_e"], dhn_e, dx1, "d_mix_norm_e")

    G["ffn_norm"] = jnp.concatenate([dfn0, dfn1], axis=0)
    G["ple_norm"] = jnp.concatenate([dpn0, dpn1], axis=0)
    G["ffn_conv"] = jnp.stack([dfc0, dfc1])
    G["w_up"] = [dwu0, dwu1]
    G["w_down"] = [dwd0, dwd1]
    G["w_ple_gate"] = [dwg0, dwg1]
    G["w_ple"] = [dwp0, dwp1]
    return sq[0, 0], grad_x, G


BIG = ("w_in_e", "w_out_e", "w_in_o", "w_out_o", "w_up", "w_down", "w_ple_gate", "w_ple")
SHARDED_SMALL = (("mix_norm_o", 1), ("conv_qkv_o", 2), ("ffn_conv", 2))
REPLICATED = ("mix_norm_e", "pool_w", "pool_scale", "a_log_o", "dt_bias_o", "gdn_norm_o", "ffn_norm", "ple_norm",
              "final_norm")
WEIGHT_ORDER = ("mix_norm_e", "w_in_e", "pool_w", "pool_scale", "w_out_e", "mix_norm_o", "w_in_o", "conv_qkv_o",
                "a_log_o", "dt_bias_o", "gdn_norm_o", "w_out_o", "ffn_norm", "w_up", "ffn_conv", "w_down", "ple_norm",
                "w_ple_gate", "w_ple", "final_norm")
SMALL_W = LANE
SMALL_ROWS = 16


def _size(shape):
    n = 1
    for s in shape:
        n *= s
    return n


def _pack(arrs, width, granule):
    flat = jnp.concatenate([a.reshape(-1) for a in arrs])
    rows = -(-flat.shape[0] // width)
    rows = -(-rows // granule) * granule
    return jnp.pad(flat, (0, rows * width - flat.shape[0])).reshape(rows, width)


def _unpack(flat2d, shapes):
    flat = flat2d.reshape(-1)
    out, off = [], 0
    for s in shapes:
        out.append(flat[off:off + _size(s)].reshape(s))
        off += _size(s)
    return out


MESH_ID = pl.DeviceIdType.MESH
HBM_SPEC = pl.BlockSpec(memory_space=pltpu.HBM)


def _where_am_i():
    return lax.axis_index("x"), lax.axis_index("y"), lax.axis_index("c")


def _other_chips(x, y):
    return [(1 - x, y), (x, 1 - y), (1 - x, 1 - y)]


def _remote(src, dst, send_sems, recv_sems, k, to):
    return pltpu.make_async_remote_copy(src_ref=src, dst_ref=dst, send_sem=send_sems.at[k], recv_sem=recv_sems.at[k],
                                        device_id=to, device_id_type=MESH_ID)


def _chip_allgather(pack, name):
    R, Wd = pack.shape
    Rh = R // 2

    def body(src_ref, out_ref, send_sems, recv_sems, local_sem):
        x, y, c = _where_am_i()
        me, sib = (x, y, c), (x, y, 1 - c)
        chips = _other_chips(x, y)
        mine_rows = pl.ds(pl.multiple_of(c * Rh, SMALL_ROWS), Rh)
        sib_rows = pl.ds(pl.multiple_of((1 - c) * Rh, SMALL_ROWS), Rh)
        j_me = 2 * x + y
        local = pltpu.make_async_copy(src_ref, out_ref.at[j_me], local_sem)
        local.start()
        first = [_remote(src_ref.at[mine_rows], out_ref.at[j_me, mine_rows], send_sems, recv_sems, k, (cx, cy, c))
                 for k, (cx, cy) in enumerate(chips)]
        for cp in first:
            cp.start()
        passed = []
        for k, (cx, cy) in enumerate(chips):
            blk = out_ref.at[2 * cx + cy, mine_rows]
            _remote(blk, blk, send_sems, recv_sems, k, me).wait_recv()
            fw = _remote(blk, blk, send_sems, recv_sems, 3 + k, sib)
            fw.start()
            passed.append(fw)
        for k, (cx, cy) in enumerate(chips):
            blk = out_ref.at[2 * cx + cy, sib_rows]
            _remote(blk, blk, send_sems, recv_sems, 3 + k, me).wait_recv()
        for cp in first + passed:
            cp.wait_send()
        local.wait()

    return pl.pallas_call(
        body, name=name, in_specs=[HBM_SPEC], out_specs=HBM_SPEC,
        out_shape=jax.ShapeDtypeStruct((N_CHIPS, R, Wd), pack.dtype),
        scratch_shapes=[pltpu.SemaphoreType.DMA((6,)), pltpu.SemaphoreType.DMA((6,)), pltpu.SemaphoreType.DMA],
    )(pack)


def _chip_allgather_many(blocks, name):
    n = len(blocks)

    def body(*refs):
        srcs, outs = refs[:n], refs[n:2 * n]
        send_sems, recv_sems = refs[2 * n:]
        x, y, c = _where_am_i()
        me, sib = (x, y, c), (x, y, 1 - c)
        chips = _other_chips(x, y)
        j_me = 2 * x + y
        first = [_remote(srcs[p].at[c], outs[p].at[j_me, c], send_sems, recv_sems, 6 * p + k, (cx, cy, c))
                 for p in range(n) for k, (cx, cy) in enumerate(chips)]
        for cp in first:
            cp.start()
        passed = []
        for k, (cx, cy) in enumerate(chips):
            for p in range(n):
                blk = outs[p].at[2 * cx + cy, c]
                _remote(blk, blk, send_sems, recv_sems, 6 * p + k, me).wait_recv()
                fw = _remote(blk, blk, send_sems, recv_sems, 6 * p + 3 + k, sib)
                fw.start()
                passed.append(fw)
        for k, (cx, cy) in enumerate(chips):
            for p in range(n):
                blk = outs[p].at[2 * cx + cy, 1 - c]
                _remote(blk, blk, send_sems, recv_sems, 6 * p + 3 + k, me).wait_recv()
        for cp in first + passed:
            cp.wait_send()

    return pl.pallas_call(
        body, name=name, in_specs=[HBM_SPEC] * n, out_specs=[HBM_SPEC] * n,
        out_shape=[jax.ShapeDtypeStruct((N_CHIPS,) + b.shape, b.dtype) for b in blocks],
        scratch_shapes=[pltpu.SemaphoreType.DMA((6 * n,)), pltpu.SemaphoreType.DMA((6 * n,))],
    )(*blocks)


SEM_SPEC = pl.BlockSpec(memory_space=pltpu.SEMAPHORE)
DATAFLOW_EFFECT = pltpu.SideEffectType.DATAFLOW_SIDE_EFFECTING


def _chip_allgather_start(blocks, name):
    n = len(blocks)

    def body(*refs):
        srcs, lands = refs[:n], refs[n:2 * n]
        send_sems, recv_sems, token = refs[2 * n], refs[2 * n + 1], refs[-1]
        x, y, c = _where_am_i()
        j_me = 2 * x + y
        for p in range(n):
            for k, (cx, cy) in enumerate(_other_chips(x, y)):
                _remote(srcs[p].at[c], lands[p].at[j_me, c], send_sems, recv_sems, 3 * p + k, (cx, cy, c)).start()
        token[...] = jnp.zeros_like(token)

    lands = [pltpu.with_memory_space_constraint(lax.empty((N_CHIPS,) + b.shape, b.dtype), pltpu.HBM) for b in blocks]
    blocks = [pltpu.with_memory_space_constraint(b, pltpu.HBM) for b in blocks]
    outs = pl.pallas_call(
        body, name=name,
        in_specs=[HBM_SPEC] * (2 * n),
        out_specs=[SEM_SPEC, SEM_SPEC] + [HBM_SPEC] * (2 * n) + [pl.BlockSpec(memory_space=pltpu.VMEM)],
        out_shape=[pltpu.SemaphoreType.DMA((3 * n,)), pltpu.SemaphoreType.DMA((3 * n,))]
        + [pltpu.HBM(a.shape, a.dtype) for a in blocks + lands] + [jax.ShapeDtypeStruct((8, LANE), F32)],
        input_output_aliases={i: 2 + i for i in range(2 * n)},
        compiler_params=pltpu.CompilerParams(has_side_effects=DATAFLOW_EFFECT),
    )(*blocks, *lands)
    return outs[0], outs[1], list(outs[2:2 + n]), list(outs[2 + n:2 + 2 * n]), outs[-1]


def _chip_allgather_wait(send_sems, recv_sems, blocks, lands, after, name):
    n = len(blocks)

    def body(*refs):
        srcs, zones = refs[:n], refs[n:2 * n]
        send, recv = refs[2 * n], refs[2 * n + 1]
        x, y, c = _where_am_i()
        for p in range(n):
            for k, (cx, cy) in enumerate(_other_chips(x, y)):
                cp = _remote(srcs[p].at[c], zones[p].at[2 * cx + cy, c], send, recv, 3 * p + k, (x, y, c))
                cp.wait_send()
                cp.wait_recv()

    outs = pl.pallas_call(
        body, name=name,
        in_specs=[HBM_SPEC] * (2 * n) + [SEM_SPEC, SEM_SPEC, pl.BlockSpec(memory_space=pl.ANY)],
        out_specs=[HBM_SPEC] * (2 * n),
        out_shape=[pltpu.HBM(a.shape, a.dtype) for a in list(blocks) + list(lands)],
        input_output_aliases={i: i for i in range(2 * n)},
        compiler_params=pltpu.CompilerParams(has_side_effects=DATAFLOW_EFFECT),
    )(*blocks, *lands, send_sems, recv_sems, after)
    return list(outs[n:])


def _chip_allgather_forward(lands, name):
    n = len(lands)

    def body(*refs):
        ins, outs = refs[:n], refs[n:2 * n]
        send_sems, recv_sems = refs[2 * n:]
        x, y, c = _where_am_i()
        me, sib = (x, y, c), (x, y, 1 - c)
        chips = _other_chips(x, y)
        passed = [_remote(ins[p].at[2 * cx + cy, c], outs[p].at[2 * cx + cy, c], send_sems, recv_sems, 3 * p + k, sib)
                  for p in range(n) for k, (cx, cy) in enumerate(chips)]
        for cp in passed:
            cp.start()
        for p in range(n):
            for k, (cx, cy) in enumerate(chips):
                blk = outs[p].at[2 * cx + cy, 1 - c]
                _remote(blk, blk, send_sems, recv_sems, 3 * p + k, me).wait_recv()
        for cp in passed:
            cp.wait_send()

    return pl.pallas_call(
        body, name=name, in_specs=[HBM_SPEC] * n, out_specs=[HBM_SPEC] * n,
        out_shape=[jax.ShapeDtypeStruct(a.shape, a.dtype) for a in lands],
        input_output_aliases={i: i for i in range(n)},
        scratch_shapes=[pltpu.SemaphoreType.DMA((3 * n,)), pltpu.SemaphoreType.DMA((3 * n,))],
    )(*lands)


def _sibling_swap_many(pieces, name):
    n = len(pieces)

    def body(*refs):
        srcs, outs = refs[:n], refs[n:2 * n]
        send_sems, recv_sems = refs[2 * n:]
        x, y, c = _where_am_i()
        cps = [_remote(srcs[p].at[:, 1 - c], outs[p], send_sems, recv_sems, p, (x, y, 1 - c)) for p in range(n)]
        for cp in cps:
            cp.start()
        for cp in cps:
            cp.wait()

    return pl.pallas_call(
        body, name=name, in_specs=[HBM_SPEC] * n, out_specs=[HBM_SPEC] * n,
        out_shape=[jax.ShapeDtypeStruct((g.shape[0],) + g.shape[2:], g.dtype) for g in pieces],
        scratch_shapes=[pltpu.SemaphoreType.DMA((n,)), pltpu.SemaphoreType.DMA((n,))],
    )(*pieces)


def _chip_scatter_many(sums, name):
    n = len(sums)

    def body(*refs):
        srcs, outs = refs[:n], refs[n:2 * n]
        send_sems, recv_sems = refs[2 * n:]
        x, y, c = _where_am_i()
        cps = [_remote(srcs[p].at[2 * cx + cy], outs[p].at[k], send_sems, recv_sems, 3 * p + k, (cx, cy, c))
               for p in range(n) for k, (cx, cy) in enumerate(_other_chips(x, y))]
        for cp in cps:
            cp.start()
        for cp in cps:
            cp.wait()

    return pl.pallas_call(
        body, name=name, in_specs=[HBM_SPEC] * n, out_specs=[HBM_SPEC] * n,
        out_shape=[jax.ShapeDtypeStruct((N_CHIPS - 1,) + s.shape[1:], s.dtype) for s in sums],
        scratch_shapes=[pltpu.SemaphoreType.DMA((3 * n,)), pltpu.SemaphoreType.DMA((3 * n,))],
    )(*sums)


def _sibling_send_many(halves, name):
    n = len(halves)

    def body(*refs):
        srcs, outs = refs[:n], refs[n:2 * n]
        send_sems, recv_sems = refs[2 * n:]
        x, y, c = _where_am_i()
        cps = [_remote(srcs[p], outs[p], send_sems, recv_sems, p, (x, y, 1 - c)) for p in range(n)]
        for cp in cps:
            cp.start()
        for cp in cps:
            cp.wait()

    return pl.pallas_call(
        body, name=name, in_specs=[HBM_SPEC] * n, out_specs=[HBM_SPEC] * n,
        out_shape=[jax.ShapeDtypeStruct(h.shape, h.dtype) for h in halves],
        scratch_shapes=[pltpu.SemaphoreType.DMA((n,)), pltpu.SemaphoreType.DMA((n,))],
    )(*halves)


def _row_tile(rows, pref=512):
    best = 8
    for t in range(8, pref + 1, 8):
        if rows % t == 0:
            best = t
    return best


def _where_ids():
    x, y, c = _where_am_i()
    return jnp.stack([c, 2 * x + y]).astype(jnp.int32)


RS_ROWS = 256


def _chip_sums_bf16(G, A, ids, name):
    n, _, hr, cols = G.shape
    tr = _row_tile(hr, RS_ROWS)

    def body(ids_ref, g_ref, a_ref, o_ref):
        o_ref[...] = (g_ref[...] + a_ref[...]).astype(BF16)

    return pl.pallas_call(
        body, name=name,
        grid_spec=pltpu.PrefetchScalarGridSpec(
            num_scalar_prefetch=1, grid=(n, hr // tr),
            in_specs=[pl.BlockSpec((None, None, tr, cols), lambda j, i, ids: (j, ids[0], i, 0)),
                      pl.BlockSpec((None, tr, cols), lambda j, i, ids: (j, i, 0))],
            out_specs=pl.BlockSpec((None, tr, cols), lambda j, i, ids: (j, i, 0))),
        out_shape=jax.ShapeDtypeStruct((n, hr, cols), BF16),
        compiler_params=_cp(("parallel", "parallel")),
    )(ids, G, A)


def _total_half(G, A, B, ids, name):
    _, _, hr, cols = G.shape
    tr = _row_tile(hr, RS_ROWS)

    def body(ids_ref, g_ref, a_ref, b_ref, o_ref):
        s = g_ref[...] + a_ref[...]
        for k in range(N_CHIPS - 1):
            s = s + b_ref[k].astype(F32)
        o_ref[...] = s

    return pl.pallas_call(
        body, name=name,
        grid_spec=pltpu.PrefetchScalarGridSpec(
            num_scalar_prefetch=1, grid=(hr // tr,),
            in_specs=[pl.BlockSpec((None, None, tr, cols), lambda i, ids: (ids[1], ids[0], i, 0)),
                      pl.BlockSpec((None, tr, cols), lambda i, ids: (ids[1], i, 0)),
                      pl.BlockSpec((N_CHIPS - 1, tr, cols), lambda i, ids: (0, i, 0))],
            out_specs=pl.BlockSpec((tr, cols), lambda i, ids: (i, 0))),
        out_shape=jax.ShapeDtypeStruct((hr, cols), F32),
        compiler_params=_cp(("parallel",)),
    )(ids, G, A, B)


def _small_allreduce(v, name):
    R, Wd = v.shape

    def body(x_ref, sum_ref, all_ref, send_sems, recv_sems, local_sem):
        x, y, c = _where_am_i()
        me, sib = (x, y, c), (x, y, 1 - c)
        chips = _other_chips(x, y)

        def slot(px, py, pc):
            return all_ref.at[4 * px + 2 * py + pc]

        local = pltpu.make_async_copy(x_ref, slot(*me), local_sem)
        local.start()
        first = [_remote(x_ref, slot(*me), send_sems, recv_sems, 0, sib)]
        first += [_remote(x_ref, slot(*me), send_sems, recv_sems, 1 + k, (cx, cy, c)) for k, (cx, cy) in enumerate(chips)]
        for cp in first:
            cp.start()
        passed = []
        for k, (cx, cy) in enumerate(chips):
            blk = slot(cx, cy, c)
            _remote(blk, blk, send_sems, recv_sems, 1 + k, me).wait_recv()
            fw = _remote(blk, blk, send_sems, recv_sems, 4 + k, sib)
            fw.start()
            passed.append(fw)
        _remote(slot(*sib), slot(*sib), send_sems, recv_sems, 0, me).wait_recv()
        for k, (cx, cy) in enumerate(chips):
            blk = slot(cx, cy, 1 - c)
            _remote(blk, blk, send_sems, recv_sems, 4 + k, me).wait_recv()
        for cp in first + passed:
            cp.wait_send()
        local.wait()
        s = all_ref[0]
        for d in range(1, N_DEV):
            s = s + all_ref[d]
        sum_ref[...] = s

    vm = pl.BlockSpec(memory_space=pltpu.VMEM)
    return pl.pallas_call(
        body, name=name, in_specs=[vm], out_specs=[vm, vm],
        out_shape=[jax.ShapeDtypeStruct((R, Wd), F32), jax.ShapeDtypeStruct((N_DEV, R, Wd), F32)],
        scratch_shapes=[pltpu.SemaphoreType.DMA((7,)), pltpu.SemaphoreType.DMA((7,)), pltpu.SemaphoreType.DMA],
    )(v)[0]


def _adamw(w, g, m, v, name):
    L, R, Wd = w.shape
    tr = _row_tile(R, RS_ROWS)
    c1 = 1.0 - ADAM_B1 ** ADAM_STEP
    c2 = 1.0 - ADAM_B2 ** ADAM_STEP

    def body(w_ref, g_ref, m_ref, v_ref, d_ref, nm_ref, nv_ref):
        gv = g_ref[...]
        nm = ADAM_B1 * m_ref[...] + (1.0 - ADAM_B1) * gv
        nv = ADAM_B2 * v_ref[...] + (1.0 - ADAM_B2) * (gv * gv)
        d_ref[...] = -ADAM_LR * ((nm / c1) / (jnp.sqrt(nv / c2) + ADAM_EPS) + ADAM_WD * w_ref[...])
        nm_ref[...] = nm
        nv_ref[...] = nv

    row = pl.BlockSpec((None, tr, Wd), lambda l, i: (l, i, 0))
    shp = jax.ShapeDtypeStruct((L, R, Wd), F32)
    return pl.pallas_call(
        body, name=name, grid=(L, R // tr), in_specs=[row] * 4, out_specs=[row] * 3, out_shape=[shp] * 3,
        compiler_params=_cp(("parallel", "parallel")),
    )(w, g, m, v)


def _adamw_halves(w, m, v, mine, theirs, ids, name):
    L, R, Wd = w.shape
    hr = R // 2
    tr = _row_tile(hr, RS_ROWS)
    c1 = 1.0 - ADAM_B1 ** ADAM_STEP
    c2 = 1.0 - ADAM_B2 ** ADAM_STEP

    def body(ids_ref, w_ref, m_ref, v_ref, *refs):
        g_refs, (g_ref, d_ref, nm_ref, nv_ref) = refs[:2 * L], refs[2 * L:]
        layer, half = pl.program_id(0), pl.program_id(1)
        own = half == ids_ref[0]
        gv = jnp.where(own, g_refs[0][...], g_refs[L][...])
        for l in range(1, L):
            gv = jnp.where(layer == l, jnp.where(own, g_refs[l][...], g_refs[L + l][...]), gv)
        nm = ADAM_B1 * m_ref[...] + (1.0 - ADAM_B1) * gv
        nv = ADAM_B2 * v_ref[...] + (1.0 - ADAM_B2) * (gv * gv)
        g_ref[...] = gv
        d_ref[...] = -ADAM_LR * ((nm / c1) / (jnp.sqrt(nv / c2) + ADAM_EPS) + ADAM_WD * w_ref[...])
        nm_ref[...] = nm
        nv_ref[...] = nv

    blk = pl.BlockSpec((None, None, tr, Wd), lambda l, h, i, ids: (l, h, i, 0))
    g_blk = pl.BlockSpec((tr, Wd), lambda l, h, i, ids: (i, 0))
    shp = jax.ShapeDtypeStruct((L, 2, hr, Wd), F32)
    outs = pl.pallas_call(
        body, name=name,
        grid_spec=pltpu.PrefetchScalarGridSpec(
            num_scalar_prefetch=1, grid=(L, 2, hr // tr),
            in_specs=[blk] * 3 + [g_blk] * (2 * L), out_specs=[blk] * 4),
        out_shape=[shp] * 4,
        compiler_params=_cp(("parallel", "parallel", "parallel")),
    )(ids, *[a.reshape(L, 2, hr, Wd) for a in (w, m, v)], *mine, *theirs)
    return tuple(o.reshape(L, R, Wd) for o in outs)


def _two_halves(a):
    cols = a.shape[-1]
    return a.reshape(2, _size(a.shape) // (2 * cols), cols)


FIRST_NEEDED = ("w_in_e", "w_out_e")
LATER_NEEDED = tuple(n for n in BIG if n not in FIRST_NEEDED)


def _gather_weights(P):
    chip = 2 * lax.axis_index("x") + lax.axis_index("y")

    def with_own(landed, own):
        return lax.dynamic_update_slice_in_dim(landed, own[None], chip, axis=0)

    mine = {n: _two_halves(P[n].astype(BF16)) for n in BIG}
    first = _chip_allgather_many([mine[n] for n in FIRST_NEEDED], "ag_first")
    gathered = {n: with_own(g, mine[n]) for n, g in zip(FIRST_NEEDED, first)}
    send_sems, recv_sems, blocks, lands, token = _chip_allgather_start([mine[n] for n in LATER_NEEDED], "ag_start")

    def later(after):
        landed = _chip_allgather_wait(send_sems, recv_sems, blocks, lands, after, "ag_wait")
        g = {n: with_own(a, mine[n]) for n, a in zip(LATER_NEEDED, _chip_allgather_forward(landed, "ag_forward"))}
        w_in_o = g["w_in_o"].reshape(N_CHIPS, D_MODEL, ODD_IN // N_CHIPS)
        return {
            "w_out_o": g["w_out_o"].reshape(D_MODEL, D_MODEL),
            "w_in_o": jnp.pad(jnp.concatenate([w_in_o[j] for j in range(N_CHIPS)], axis=1),
                              ((0, 0), (0, ODD_IN_PAD - ODD_IN))),
            "w_up": g["w_up"],
            "w_down": g["w_down"].transpose(1, 0, 2, 3).reshape(2, FFN_DIM, D_MODEL),
            "w_ple_gate": g["w_ple_gate"].transpose(1, 0, 2, 3).reshape(2, D_MODEL, D_MODEL),
            "w_ple": g["w_ple"].transpose(1, 2, 0, 3).reshape(2, PLE_DIM, D_MODEL),
        }

    small_shapes = [P[n].shape for n, _ in SHARDED_SMALL]
    small = _chip_allgather(_pack([P[n] for n, _ in SHARDED_SMALL], SMALL_W, SMALL_ROWS), "ag_small")
    parts = [_unpack(small[j], small_shapes) for j in range(N_CHIPS)]
    full = {n: jnp.concatenate([parts[j][i] for j in range(N_CHIPS)], axis=ax)
            for i, (n, ax) in enumerate(SHARDED_SMALL)}
    W = {n: P[n] for n in REPLICATED}
    W["pool_w"] = P["pool_w"][0]
    W["final_norm"] = P["final_norm"].reshape(1, D_MODEL)
    W["mix_norm_o"] = full["mix_norm_o"]
    W["conv_qkv_o"] = full["conv_qkv_o"][0]
    W["ffn_conv"] = full["ffn_conv"]
    W["w_in_e"] = gathered["w_in_e"].reshape(N_CHIPS, D_MODEL, IN_E_SHARD)
    W["w_out_e"] = gathered["w_out_e"].reshape(D_MODEL, D_MODEL)
    return W, token, later


def _reduce_big_gradients(G):
    w_in_o = G["w_in_o"]
    shard = ODD_IN // N_CHIPS
    pieces, layers_of = [], []
    for n in BIG:
        if n == "w_in_o":
            gs = [jnp.stack([w_in_o[:, j * shard:(j + 1) * shard] for j in range(N_CHIPS)])]
        else:
            gs = G[n] if isinstance(G[n], list) else [G[n]]
        layers_of.append(list(range(len(pieces), len(pieces) + len(gs))))
        pieces += [g.reshape(N_CHIPS, 2, g.shape[1] // 2, g.shape[2]) for g in gs]
    ids = _where_ids()
    from_sibling = _sibling_swap_many(pieces, "rs_sibling_swap")
    sums = [_chip_sums_bf16(g, a, ids, f"rs_chip_sums{i}") for i, (g, a) in enumerate(zip(pieces, from_sibling))]
    from_chips = _chip_scatter_many(sums, "rs_chip_scatter")
    halves = [_total_half(g, a, b, ids, f"rs_total{i}")
              for i, (g, a, b) in enumerate(zip(pieces, from_sibling, from_chips))]
    from_sibling_total = _sibling_send_many(halves, "rs_sibling_send")
    return {n: ([halves[p] for p in ps], [from_sibling_total[p] for p in ps]) for n, ps in zip(BIG, layers_of)}, ids


def kernel(x, p, mix_norm_e, w_in_e, pool_w, pool_scale, w_out_e, mix_norm_o, w_in_o, conv_qkv_o, a_log_o, dt_bias_o, gdn_norm_o, w_out_o, ffn_norm, w_up, ffn_conv, w_down, ple_norm, w_ple_gate, w_ple, final_norm, loss_target, m_mix_norm_e, m_w_in_e, m_pool_w, m_pool_scale, m_w_out_e, m_mix_norm_o, m_w_in_o, m_conv_qkv_o, m_a_log_o, m_dt_bias_o, m_gdn_norm_o, m_w_out_o, m_ffn_norm, m_w_up, m_ffn_conv, m_w_down, m_ple_norm, m_w_ple_gate, m_w_ple, m_final_norm, v_mix_norm_e, v_w_in_e, v_pool_w, v_pool_scale, v_w_out_e, v_mix_norm_o, v_w_in_o, v_conv_qkv_o, v_a_log_o, v_dt_bias_o, v_gdn_norm_o, v_w_out_o, v_ffn_norm, v_w_up, v_ffn_conv, v_w_down, v_ple_norm, v_w_ple_gate, v_w_ple, v_final_norm):
    args = locals()
    P = {n: args[n] for n in WEIGHT_ORDER}
    M = {n: args["m_" + n] for n in WEIGHT_ORDER}
    V = {n: args["v_" + n] for n in WEIGHT_ORDER}

    W, token, later_weights = _gather_weights(P)
    T = x.shape[1]
    sq, grad_x, G = _local_step(x.reshape(T, D_MODEL), p.reshape(2, T, PLE_DIM), loss_target.reshape(T, D_MODEL), W,
                                token, later_weights)
    out = {}
    reduced, ids = _reduce_big_gradients(G)
    for n, (mine, theirs) in reduced.items():
        out[n] = _adamw_halves(P[n], M[n], V[n], mine, theirs, ids, f"adamw_{n}")

    small_full = {n: G[n] for n in REPLICATED}
    small_full["pool_w"] = G["pool_w"][None]
    small_full["final_norm"] = G["final_norm"].reshape(D_MODEL)
    small_full["mix_norm_o"] = G["mix_norm_o"]
    small_full["conv_qkv_o"] = G["conv_qkv_o"][None]
    small_full["ffn_conv"] = G["ffn_conv"]
    small_names = REPLICATED + tuple(n for n, _ in SHARDED_SMALL)
    summed = _small_allreduce(_pack([small_full[n] for n in small_names] + [sq.reshape(1)], SMALL_W, 8), "ar_small")
    *g_list, sq_total = _unpack(summed, [small_full[n].shape for n in small_names] + [(1,)])
    g_small = dict(zip(small_names, g_list))
    chip = 2 * lax.axis_index("x") + lax.axis_index("y")
    for n, ax in SHARDED_SMALL:
        width = P[n].shape[ax]
        g_small[n] = lax.dynamic_slice_in_dim(g_small[n], chip * width, width, axis=ax)

    def pack_small(D):
        return _pack([D[n] for n in small_names], SMALL_W, RS_ROWS)[None]

    g_pack = pack_small(g_small)
    upd = _adamw(pack_small(P), g_pack, pack_small(M), pack_small(V), "adamw_small")
    shapes = [P[n].shape for n in small_names]
    for n, *vals in zip(small_names, *[_unpack(a[0], shapes) for a in (g_pack,) + tuple(upd)]):
        out[n] = tuple(vals)

    loss = (0.5 / D_MODEL) * sq_total[0]
    return (loss, grad_x[None]) + tuple(out[n][i] for i in range(4) for n in WEIGHT_ORDER)
```

```python
import functools

import jax
import jax.numpy as jnp
from jax import lax
from jax.experimental import pallas as pl
from jax.experimental.pallas import tpu as pltpu

F32 = jnp.float32
BF16 = jnp.bfloat16

D_MODEL = 1024
PLE_DIM = 256
POOL_WIDTH = 512
POOL_WINDOWS = (2, 4, 8, 16)
POOL_GROUP_DIM = 128
SB_HEADS = 8
SB_HEAD_DIM = 64
GDN_HEADS = 8
GDN_HEAD_DIM = 128
GDN_CONV = 4
GDN_CHUNK = 64
FFN_DIM = 2816
FFN_CONV = 3
EPS = 1e-6
ODD_IN = 4 * D_MODEL + 2 * GDN_HEADS
ODD_IN_PAD = 33 * 128
ADAM_LR, ADAM_B1, ADAM_B2, ADAM_EPS, ADAM_WD, ADAM_STEP = 0.001, 0.9, 0.999, 1e-08, 0.01, 10

LANE = 128
VMEM_LIMIT = 56 * 1024 * 1024

N_CHIPS = 4
N_DEV = 8


def _cp(sem=None):
    return pltpu.CompilerParams(dimension_semantics=sem, vmem_limit_bytes=VMEM_LIMIT)


def _tile(n, pref):
    if n <= pref:
        return n
    best = None
    for t in range(LANE, pref + 1, LANE):
        if n % t == 0:
            best = t
    assert best is not None, (n, pref)
    return best


_DIMS = {"nn": (((1,), (0,)), ((), ())), "nt": (((1,), (1,)), ((), ())), "tn": (((0,), (0,)), ((), ()))}
_BDIMS = {"nn": (((2,), (1,)), ((0,), (0,))), "nt": (((2,), (2,)), ((0,), (0,))), "tn": (((1,), (1,)), ((0,), (0,)))}


def _dims(mode, ndim):
    return (_BDIMS if ndim == 3 else _DIMS)[mode]


def _dot(a, b, mode="nn"):
    return lax.dot_general(a.astype(BF16), b.astype(BF16), _dims(mode, a.ndim), preferred_element_type=F32)


def _bdot(a, b, mode="nn"):
    return lax.dot_general(a.astype(BF16), b.astype(BF16), _BDIMS[mode], preferred_element_type=F32)


def _split2(x):
    hi = x.astype(BF16)
    lo = (x - hi.astype(F32)).astype(BF16)
    return hi, lo


def _split3(x):
    hi = x.astype(BF16)
    r = x - hi.astype(F32)
    mid = r.astype(BF16)
    lo = (r - mid.astype(F32)).astype(BF16)
    return hi, mid, lo


def _dot_x01(x, m01, mode="nn"):
    hi, lo = _split2(x)
    return (lax.dot_general(hi, m01, _DIMS[mode], preferred_element_type=F32)
            + lax.dot_general(lo, m01, _DIMS[mode], preferred_element_type=F32))


def _dot3_raw(a, b, mode):
    ah, al = _split2(a)
    bh, bl = _split2(b)
    d = _dims(mode, a.ndim)
    return (lax.dot_general(ah, bh, d, preferred_element_type=F32)
            + lax.dot_general(ah, bl, d, preferred_element_type=F32)
            + lax.dot_general(al, bh, d, preferred_element_type=F32))


@jax.custom_vjp
def _dot3(a, b):
    return _dot3_raw(a, b, "nn")


def _dot3_fwd(a, b):
    return _dot3_raw(a, b, "nn"), (a, b)


def _dot3_bwd(res, g):
    a, b = res
    return _dot(g, b, "nt"), _dot(a, g, "tn")


_dot3.defvjp(_dot3_fwd, _dot3_bwd)


@jax.custom_vjp
def _dot1_nt(a, b):
    return _dot(a, b, "nt")


def _dot1_nt_fwd(a, b):
    return _dot(a, b, "nt"), (a, b)


def _dot1_nt_bwd(res, g):
    a, b = res
    return _dot(g, b, "nn"), _dot(g, a, "tn")


_dot1_nt.defvjp(_dot1_nt_fwd, _dot1_nt_bwd)


def _m01_left_raw(m, x):
    d = _dims("nn", x.ndim)
    if x.ndim == 3:
        m = jnp.broadcast_to(m, (x.shape[0],) + m.shape)
    p0, p1, p2 = _split3(x)
    return (lax.dot_general(m, p0, d, preferred_element_type=F32)
            + lax.dot_general(m, p1, d, preferred_element_type=F32)
            + lax.dot_general(m, p2, d, preferred_element_type=F32))


@jax.custom_vjp
def _m01_left(m, mt, x):
    return _m01_left_raw(m, x)


def _m01_left_fwd(m, mt, x):
    return _m01_left_raw(m, x), (m, mt)


def _m01_left_bwd(res, g):
    m, mt = res
    return jnp.zeros_like(m), jnp.zeros_like(mt), _m01_left_raw(mt, g)


_m01_left.defvjp(_m01_left_fwd, _m01_left_bwd)


def _softplus(x):
    return jnp.maximum(x, 0.0) + jnp.log(1.0 + jnp.exp(-jnp.abs(x)))


def _sigmoid(x):
    return 1.0 / (1.0 + jnp.exp(-x))


def _silu(x):
    return x * _sigmoid(x)


def _dsilu(x):
    s = _sigmoid(x)
    return s * (1.0 + x * (1.0 - s))


def _cols_of(n_blocks_per_part, *fixed):
    return lambda r, c: (c // n_blocks_per_part,) + fixed + (r, c % n_blocks_per_part)


def _rows_of(n_blocks_per_part, *fixed):
    return lambda r, c: (r // n_blocks_per_part,) + fixed + (r % n_blocks_per_part, c)


def _layer_of(layer):
    return lambda r, c: (layer, r, c)


def _mm(a, b, mode, name, out_dtype=F32, res=None, tm=1024, tn=512, tk=1024,
        dims=None, a_view=None, b_view=None, o_view=None, out_shape=None):
    if dims is None:
        if mode == "nn":
            (M, K), (K2, N) = a.shape, b.shape
        elif mode == "nt":
            (M, K), (N, K2) = a.shape, b.shape
        else:
            (K, M), (K2, N) = a.shape, b.shape
        assert K == K2, (name, a.shape, b.shape)
    else:
        M, N, K = dims
    tm, tn, tk = _tile(M, tm), _tile(N, tn), _tile(K, tk)
    nk = K // tk

    def spec(arr, blk, view, rc):
        view = view or (lambda r, c: (r, c))
        return pl.BlockSpec((None,) * (arr.ndim - 2) + blk, lambda i, j, k: view(*rc(i, j, k)))

    if mode == "tn":
        a_spec = spec(a, (tk, tm), a_view, lambda i, j, k: (k, i))
    else:
        a_spec = spec(a, (tm, tk), a_view, lambda i, j, k: (i, k))
    if mode == "nt":
        b_spec = spec(b, (tn, tk), b_view, lambda i, j, k: (j, k))
    else:
        b_spec = spec(b, (tk, tn), b_view, lambda i, j, k: (k, j))
    out_shape = out_shape or (M, N)
    o_spec = pl.BlockSpec((None,) * (len(out_shape) - 2) + (tm, tn),
                          lambda i, j, k: (o_view or (lambda r, c: (r, c)))(i, j))
    has_res = res is not None
    assert not (has_res and o_view), name

    def body(*refs):
        a_ref, b_ref = refs[:2]
        r_ref = refs[2] if has_res else None
        o_ref = refs[3] if has_res else refs[2]

        def finish(r):
            if has_res:
                r = r + r_ref[...]
            o_ref[...] = r.astype(out_dtype)

        if nk == 1:
            finish(_dot(a_ref[...], b_ref[...], mode))
            return
        acc = refs[-1]
        k = pl.program_id(2)

        @pl.when(k == 0)
        def _():
            acc[...] = jnp.zeros_like(acc)

        acc[...] += _dot(a_ref[...], b_ref[...], mode)

        @pl.when(k == nk - 1)
        def _():
            finish(acc[...])

    ins = [a, b] + ([res] if has_res else [])
    in_specs = [a_spec, b_spec] + ([o_spec] if has_res else [])
    return pl.pallas_call(
        body, name=name, grid=(M // tm, N // tn, nk),
        in_specs=in_specs, out_specs=o_spec,
        out_shape=jax.ShapeDtypeStruct(out_shape, out_dtype),
        scratch_shapes=[pltpu.VMEM((tm, tn), F32)] if nk > 1 else [],
        compiler_params=_cp(("parallel", "parallel", "arbitrary")),
    )(*ins)


def _rms_fwd(x, gain, name, after=None):
    T, D = x.shape
    tt = _tile(T, 512)

    def body(x_ref, g_ref, *rest):
        o_ref = rest[-1]
        xv = x_ref[...]
        r = lax.rsqrt(jnp.mean(xv * xv, axis=-1, keepdims=True) + EPS)
        o_ref[...] = (xv * r * g_ref[...]).astype(BF16)

    ordered = [] if after is None else [after]
    return pl.pallas_call(
        body, name=name, grid=(T // tt,),
        in_specs=[pl.BlockSpec((tt, D), lambda i: (i, 0)), pl.BlockSpec((1, D), lambda i: (0, 0))]
        + [pl.BlockSpec((8, LANE), lambda i: (0, 0)) for _ in ordered],
        out_specs=pl.BlockSpec((tt, D), lambda i: (i, 0)),
        out_shape=jax.ShapeDtypeStruct((T, D), BF16),
        compiler_params=_cp(("parallel",)),
    )(x, gain, *ordered)


def _rms_bwd(x, gain, dh, dres, name):
    T, D = x.shape
    tt = _tile(T, 512)

    def body(x_ref, g_ref, dh_ref, dr_ref, dx_ref, dg_ref):
        i = pl.program_id(0)
        xv = x_ref[...]
        dy = dh_ref[...].astype(F32)
        r = lax.rsqrt(jnp.mean(xv * xv, axis=-1, keepdims=True) + EPS)
        xn = xv * r
        gdy = dy * g_ref[...]
        dx = r * (gdy - xn * jnp.mean(gdy * xn, axis=-1, keepdims=True))
        dx_ref[...] = dr_ref[...] + dx

        @pl.when(i == 0)
        def _():
            dg_ref[...] = jnp.zeros_like(dg_ref)

        dg_ref[...] += jnp.sum(dy * xn, axis=0, keepdims=True)

    row = pl.BlockSpec((tt, D), lambda i: (i, 0))
    vec = pl.BlockSpec((1, D), lambda i: (0, 0))
    return pl.pallas_call(
        body, name=name, grid=(T // tt,),
        in_specs=[row, vec, row, row], out_specs=[row, vec],
        out_shape=[jax.ShapeDtypeStruct((T, D), F32), jax.ShapeDtypeStruct((1, D), F32)],
        compiler_params=_cp(("arbitrary",)),
    )(x, gain, dh, dres)


def _final_loss(x, gain, target, name):
    T, D = x.shape
    tt = _tile(T, 512)

    def body(x_ref, g_ref, t_ref, l_ref, dx_ref, dg_ref):
        i = pl.program_id(0)
        xv = x_ref[...]
        r = lax.rsqrt(jnp.mean(xv * xv, axis=-1, keepdims=True) + EPS)
        xn = xv * r
        err = xn * g_ref[...] - t_ref[...]
        dy = err * (1.0 / D)
        gdy = dy * g_ref[...]
        dx_ref[...] = r * (gdy - xn * jnp.mean(gdy * xn, axis=-1, keepdims=True))

        @pl.when(i == 0)
        def _():
            dg_ref[...] = jnp.zeros_like(dg_ref)
            l_ref[...] = jnp.zeros_like(l_ref)

        dg_ref[...] += jnp.sum(dy * xn, axis=0, keepdims=True)
        l_ref[...] += jnp.sum(jnp.sum(err * err, axis=1, keepdims=True), axis=0, keepdims=True)

    row = pl.BlockSpec((tt, D), lambda i: (i, 0))
    vec = pl.BlockSpec((1, D), lambda i: (0, 0))
    return pl.pallas_call(
        body, name=name, grid=(T // tt,),
        in_specs=[row, vec, row],
        out_specs=[pl.BlockSpec((8, LANE), lambda i: (0, 0)), row, vec],
        out_shape=[jax.ShapeDtypeStruct((8, LANE), F32), jax.ShapeDtypeStruct((T, D), F32),
                   jax.ShapeDtypeStruct((1, D), F32)],
        compiler_params=_cp(("arbitrary",)),
    )(x, gain, target)


def _shift_down(x, i, t_idx):
    if i == 0:
        return x
    return jnp.where(t_idx >= i, pltpu.roll(x, i, 0), 0.0)


def _shift_up(x, i, t_idx):
    if i == 0:
        return x
    n = x.shape[0]
    return jnp.where(t_idx < n - i, pltpu.roll(x, n - i, 0), 0.0)


def _pool_select(g, vals):
    out = vals[-1]
    for gi in range(len(vals) - 2, -1, -1):
        out = jnp.where(g == gi, vals[gi], out)
    return out


def _pool_y(u, g, t_idx):
    s1 = u + _shift_down(u, 1, t_idx)
    s2 = s1 + _shift_down(s1, 2, t_idx)
    s3 = s2 + _shift_down(s2, 4, t_idx)
    s4 = s3 + _shift_down(s3, 8, t_idx)
    ws = _pool_select(g, [s1, s2, s3, s4])
    win = _pool_select(g, [jnp.float32(w) for w in POOL_WINDOWS])
    cnt = jnp.minimum(t_idx.astype(F32) + 1.0, win)
    return ws / cnt - u, cnt


def _pool_fwd(proj, pool_w, pool_scale):
    T = proj.shape[0]
    G, C = len(POOL_WINDOWS), POOL_GROUP_DIM

    def body(u_ref, w_ref, s_ref, o_ref):
        g = pl.program_id(0)
        t_idx = lax.broadcasted_iota(jnp.int32, (T, C), 0)
        y, _ = _pool_y(u_ref[...], g, t_idx)
        o_ref[...] = _dot(y, w_ref[0]) * s_ref[...]

    return pl.pallas_call(
        body, name="pool_fwd", grid=(G,),
        in_specs=[pl.BlockSpec((T, C), lambda g: (0, g)), pl.BlockSpec((1, C, C), lambda g: (g, 0, 0)),
                  pl.BlockSpec((1, C), lambda g: (0, g))],
        out_specs=pl.BlockSpec((T, C), lambda g: (0, g)),
        out_shape=jax.ShapeDtypeStruct((T, G * C), F32),
        compiler_params=_cp(("parallel",)),
    )(proj, pool_w, pool_scale)


def _pool_bwd(proj, pool_w, pool_scale, dmix):
    T = proj.shape[0]
    G, C = len(POOL_WINDOWS), POOL_GROUP_DIM

    def body(u_ref, w_ref, s_ref, do_ref, du_ref, dw_ref, ds_ref):
        g = pl.program_id(0)
        t_idx = lax.broadcasted_iota(jnp.int32, (T, C), 0)
        y, cnt = _pool_y(u_ref[...], g, t_idx)
        w = w_ref[0]
        dout = do_ref[...]
        ds_ref[...] = jnp.sum(dout * _dot(y, w), axis=0, keepdims=True)
        dy2 = dout * s_ref[...]
        dw_ref[0] = _dot(y, dy2, "tn")
        dy = _dot(dy2, w, "nt")
        dz = dy / cnt
        r1 = dz + _shift_up(dz, 1, t_idx)
        r2 = r1 + _shift_up(r1, 2, t_idx)
        r3 = r2 + _shift_up(r2, 4, t_idx)
        r4 = r3 + _shift_up(r3, 8, t_idx)
        du_ref[...] = _pool_select(g, [r1, r2, r3, r4]) - dy

    col = pl.BlockSpec((T, C), lambda g: (0, g))
    return pl.pallas_call(
        body, name="pool_bwd", grid=(G,),
        in_specs=[col, pl.BlockSpec((1, C, C), lambda g: (g, 0, 0)), pl.BlockSpec((1, C), lambda g: (0, g)), col],
        out_specs=[col, pl.BlockSpec((1, C, C), lambda g: (g, 0, 0)), pl.BlockSpec((1, C), lambda g: (0, g))],
        out_shape=[jax.ShapeDtypeStruct((T, G * C), F32), jax.ShapeDtypeStruct((G, C, C), F32),
                   jax.ShapeDtypeStruct((1, G * C), F32)],
        compiler_params=_cp(("parallel",)),
    )(proj, pool_w, pool_scale, dmix)


SB_SCALE = SB_HEAD_DIM ** -0.5
SB_PASS_SIZES = (2, 1)


def _sb_tile_logits(qb, kblk, valid):
    z = _dot(qb, kblk, "nt")
    sp = _softplus(z)
    l1m = -sp
    if valid is not None:
        l1m = jnp.where(valid, l1m, 0.0)
    return z, sp, l1m


SB_PAIR = LANE // SB_HEAD_DIM
SB_Q0 = POOL_WIDTH // LANE
SB_NB = SB_HEADS // SB_PAIR


def _sb_head_masks():
    lane = lax.broadcasted_iota(jnp.int32, (1, LANE), 1)
    return [(lane // SB_HEAD_DIM == h).astype(F32) for h in range(SB_PAIR)]


def _sb_fwd(proj):
    T = proj.shape[0]
    B = _tile(T, 256)
    nq = T // B

    def body(q_ref, k_ref, v_ref, o_ref, l_ref, k_bf, v_bf):
        qi = pl.program_id(1)

        @pl.when(qi == 0)
        def _():
            k_bf[...] = k_ref[...].astype(BF16)
            v_bf[...] = v_ref[...].astype(BF16)

        masks = _sb_head_masks()
        q_all = q_ref[...]
        qbs = [(q_all * (m * SB_SCALE)).astype(BF16) for m in masks]
        row = lax.broadcasted_iota(jnp.int32, (B, B), 0)
        col = lax.broadcasted_iota(jnp.int32, (B, B), 1)
        later = (row > col).astype(BF16)

        def tiles(kbs, state, valid):
            ksl = [pl.ds(pl.multiple_of(kb * B, B), B) for kb in kbs]
            kblks = [k_bf[ks, :] for ks in ksl]
            logits = [[_sb_tile_logits(qb, kblk, valid) for kblk in kblks] for qb in qbs]
            within = [[_dot_x01(l1m, later) for _, _, l1m in lg] for lg in logits]
            sums = [[jnp.sum(l1m, axis=1, keepdims=True) for _, _, l1m in lg] for lg in logits]
            out = []
            for h, (carry, acc) in enumerate(state):
                for (z, sp, _), rc, s, ks in zip(logits[h], within[h], sums[h], ksl):
                    a = jnp.exp(z - sp + rc + carry)
                    if valid is not None:
                        a = jnp.where(valid, a, 0.0)
                    acc = acc + _dot(a, v_bf[ks, :])
                    carry = carry + s
                out.append((carry, acc))
            return tuple(out)

        state = tiles([qi], ((jnp.zeros((B, 1), F32), jnp.zeros((B, LANE), F32)),) * SB_PAIR, col < row)
        left = qi
        for size in SB_PASS_SIZES:
            n_pass = left // size
            state = lax.fori_loop(
                0, n_pass, lambda i, c, left=left, size=size: tiles([left - 1 - size * i - u for u in range(size)],
                                                                     c, None), state)
            left = left - n_pass * size
        o_ref[...] = sum(acc * m for (_, acc), m in zip(state, masks))
        for h, (carry, _) in enumerate(state):
            l_ref[h] = carry

    return pl.pallas_call(
        body, name="sb_fwd", grid=(SB_NB, nq),
        in_specs=[pl.BlockSpec((B, LANE), lambda hp, i: (i, SB_Q0 + hp)),
                  pl.BlockSpec((T, LANE), lambda hp, i: (0, SB_Q0 + SB_NB + hp)),
                  pl.BlockSpec((T, LANE), lambda hp, i: (0, SB_Q0 + 2 * SB_NB + hp))],
        out_specs=[pl.BlockSpec((B, LANE), lambda hp, i: (i, hp)),
                   pl.BlockSpec((SB_PAIR, B, 1), lambda hp, i: (hp, i, 0))],
        out_shape=[jax.ShapeDtypeStruct((T, SB_HEADS * SB_HEAD_DIM), F32), jax.ShapeDtypeStruct((SB_HEADS, T, 1), F32)],
        scratch_shapes=[pltpu.VMEM((T, LANE), BF16), pltpu.VMEM((T, LANE), BF16)],
        compiler_params=_cp(("parallel", "arbitrary")),
    )(proj, proj, proj)


def _sb_bwd(proj, dmix, ltot):
    T = proj.shape[0]
    B = _tile(T, 256)
    nq = T // B

    def body(q_ref, k_ref, v_ref, do_ref, l_ref, dq_ref, dk_ref, dv_ref, k_bf, v_bf):
        qi = pl.program_id(1)

        @pl.when(qi == 0)
        def _():
            k_bf[...] = k_ref[...].astype(BF16)
            v_bf[...] = v_ref[...].astype(BF16)
            dk_ref[...] = jnp.zeros_like(dk_ref)
            dv_ref[...] = jnp.zeros_like(dv_ref)

        masks = _sb_head_masks()
        q_all, do_all = q_ref[...], do_ref[...]
        qbs = [(q_all * (m * SB_SCALE)).astype(BF16) for m in masks]
        dobs = [(do_all * m).astype(BF16) for m in masks]
        ltots = [l_ref[h] for h in range(SB_PAIR)]
        row = lax.broadcasted_iota(jnp.int32, (B, B), 0)
        col = lax.broadcasted_iota(jnp.int32, (B, B), 1)
        upto = (row <= col).astype(BF16)
        before = (row < col).astype(BF16)

        def tiles(kbs, state, valid):
            ksl = [pl.ds(pl.multiple_of(kb * B, B), B) for kb in kbs]
            kblks = [k_bf[ks, :] for ks in ksl]
            vblks = [v_bf[ks, :] for ks in ksl]
            logits = [[_sb_tile_logits(qb, kblk, valid) for kblk in kblks] for qb in qbs]
            das = [[_dot(dob, vblk, "nt") for vblk in vblks] for dob in dobs]
            within = [[_dot_x01(l1m, upto) for _, _, l1m in lg] for lg in logits]
            avals, es, Ps = [], [], []
            for h, (P, _, _) in enumerate(state):
                a_h, e_h = [], []
                for (z, sp, l1m), pc, da in zip(logits[h], within[h], das[h]):
                    a = jnp.exp(z - sp + (ltots[h] - P - pc))
                    if valid is not None:
                        a = jnp.where(valid, a, 0.0)
                    a_h.append(a)
                    e_h.append(da * a)
                    P = P + jnp.sum(l1m, axis=1, keepdims=True)
                avals.append(a_h)
                es.append(e_h)
                Ps.append(P)
            e_within = [[_dot_x01(e, before) for e in e_h] for e_h in es]
            out = []
            for h, (_, E, dq) in enumerate(state):
                for (z, sp, _), e, ew, a, kblk, ks in zip(logits[h], es[h], e_within[h], avals[h], kblks, ksl):
                    dz = e * jnp.exp(-sp) - jnp.exp(z - sp) * (ew + E)
                    if valid is not None:
                        dz = jnp.where(valid, dz, 0.0)
                    dzb = dz.astype(BF16)
                    dq = dq + _dot(dzb, kblk)
                    dk_ref[ks, :] += _dot(dzb, qbs[h], "tn")
                    dv_ref[ks, :] += _dot(a, dobs[h], "tn")
                    E = E + jnp.sum(e, axis=1, keepdims=True)
                out.append((Ps[h], E, dq))
            return tuple(out)

        zeros1 = jnp.zeros((B, 1), F32)
        state = ((zeros1, zeros1, jnp.zeros((B, LANE), F32)),) * SB_PAIR
        done = 0
        for size in SB_PASS_SIZES:
            n_pass = (qi - done) // size
            state = lax.fori_loop(
                0, n_pass, lambda i, c, done=done, size=size: tiles([done + size * i + u for u in range(size)], c, None),
                state)
            done = done + n_pass * size
        state = tiles([qi], state, col < row)
        dq_ref[...] = sum(dq * (m * SB_SCALE) for (_, _, dq), m in zip(state, masks))

    qspec = pl.BlockSpec((B, LANE), lambda hp, i: (i, SB_Q0 + hp))
    wide = jax.ShapeDtypeStruct((T, SB_HEADS * SB_HEAD_DIM), F32)
    return pl.pallas_call(
        body, name="sb_bwd", grid=(SB_NB, nq),
        in_specs=[qspec,
                  pl.BlockSpec((T, LANE), lambda hp, i: (0, SB_Q0 + SB_NB + hp)),
                  pl.BlockSpec((T, LANE), lambda hp, i: (0, SB_Q0 + 2 * SB_NB + hp)),
                  qspec,
                  pl.BlockSpec((SB_PAIR, B, 1), lambda hp, i: (hp, i, 0))],
        out_specs=[pl.BlockSpec((B, LANE), lambda hp, i: (i, hp)),
                   pl.BlockSpec((T, LANE), lambda hp, i: (0, hp)),
                   pl.BlockSpec((T, LANE), lambda hp, i: (0, hp))],
        out_shape=[wide, wide, wide],
        scratch_shapes=[pltpu.VMEM((T, LANE), BF16), pltpu.VMEM((T, LANE), BF16)],
        compiler_params=_cp(("parallel", "arbitrary")),
    )(proj, proj, proj, dmix, ltot)


def _rows(w_ref, K):
    return [w_ref[i:i + 1, :] for i in range(K)]


def _conv(x, ws, t_idx):
    K = len(ws)
    y = ws[K - 1] * x
    for i in range(K - 1):
        y = y + ws[i] * _shift_down(x, K - 1 - i, t_idx)
    return y


def _conv_bwd(x, ws, dy, t_idx):
    K = len(ws)
    dx = ws[K - 1] * dy
    dws = []
    for i in range(K - 1):
        dx = dx + ws[i] * _shift_up(dy, K - 1 - i, t_idx)
        dws.append(jnp.sum(dy * _shift_down(x, K - 1 - i, t_idx), axis=0, keepdims=True))
    dws.append(jnp.sum(dy * x, axis=0, keepdims=True))
    return dx, dws


def _store_rows(ref, rows):
    for i, r in enumerate(rows):
        ref[i:i + 1, :] = r


def _ffn_act_fwd(up, conv_w, name):
    T = up.shape[0]
    F = FFN_DIM
    nb = F // LANE

    def body(g_ref, v_ref, wg_ref, wv_ref, o_ref):
        t_idx = lax.broadcasted_iota(jnp.int32, (T, LANE), 0)
        cg = _conv(g_ref[...].astype(F32), _rows(wg_ref, FFN_CONV), t_idx)
        cv = _conv(v_ref[...].astype(F32), _rows(wv_ref, FFN_CONV), t_idx)
        o_ref[...] = (_silu(cg) * cv).astype(BF16)

    return pl.pallas_call(
        body, name=name, grid=(nb,),
        in_specs=[pl.BlockSpec((T, LANE), lambda j: (0, j)), pl.BlockSpec((T, LANE), lambda j: (0, j + nb)),
                  pl.BlockSpec((FFN_CONV, LANE), lambda j: (0, j)),
                  pl.BlockSpec((FFN_CONV, LANE), lambda j: (0, j + nb))],
        out_specs=pl.BlockSpec((T, LANE), lambda j: (0, j)),
        out_shape=jax.ShapeDtypeStruct((T, F), BF16),
        compiler_params=_cp(("parallel",)),
    )(up, up, conv_w, conv_w)


def _ffn_act_bwd(up, conv_w, dact, name):
    T = up.shape[0]
    F = FFN_DIM
    nb = F // LANE

    def body(g_ref, v_ref, wg_ref, wv_ref, da_ref, dup_ref, dwg_ref, dwv_ref):
        t_idx = lax.broadcasted_iota(jnp.int32, (T, LANE), 0)
        xg, xv = g_ref[...].astype(F32), v_ref[...].astype(F32)
        wg, wv = _rows(wg_ref, FFN_CONV), _rows(wv_ref, FFN_CONV)
        cg = _conv(xg, wg, t_idx)
        cv = _conv(xv, wv, t_idx)
        da = da_ref[...].astype(F32)
        dxg, dwg = _conv_bwd(xg, wg, da * cv * _dsilu(cg), t_idx)
        dxv, dwv = _conv_bwd(xv, wv, da * _silu(cg), t_idx)
        dup_ref[0] = dxg.astype(BF16)
        dup_ref[1] = dxv.astype(BF16)
        _store_rows(dwg_ref, dwg)
        _store_rows(dwv_ref, dwv)

    col = pl.BlockSpec((T, LANE), lambda j: (0, j))
    wcol = pl.BlockSpec((FFN_CONV, LANE), lambda j: (0, j))
    return pl.pallas_call(
        body, name=name, grid=(nb,),
        in_specs=[col, pl.BlockSpec((T, LANE), lambda j: (0, j + nb)), wcol,
                  pl.BlockSpec((FFN_CONV, LANE), lambda j: (0, j + nb)), col],
        out_specs=[pl.BlockSpec((2, T, LANE), lambda j: (0, 0, j)), wcol, wcol],
        out_shape=[jax.ShapeDtypeStruct((2, T, F), BF16),
                   jax.ShapeDtypeStruct((FFN_CONV, F), F32), jax.ShapeDtypeStruct((FFN_CONV, F), F32)],
        compiler_params=_cp(("parallel",)),
    )(up, up, conv_w, conv_w, dact)


N_QK_BLOCKS = 2 * GDN_HEADS


def _gdn_pre_fwd(proj, conv_w):
    T = proj.shape[0]
    nb = 3 * GDN_HEADS

    def body(x_ref, w_ref, o_ref):
        j = pl.program_id(0)
        t_idx = lax.broadcasted_iota(jnp.int32, (T, LANE), 0)
        s = _silu(_conv(x_ref[...], _rows(w_ref, GDN_CONV), t_idx))
        rn = lax.rsqrt(jnp.sum(s * s, axis=-1, keepdims=True) + EPS)
        o_ref[...] = s * jnp.where(j < N_QK_BLOCKS, rn, 1.0)

    return pl.pallas_call(
        body, name="gdn_pre_fwd", grid=(nb,),
        in_specs=[pl.BlockSpec((T, LANE), lambda j: (0, j)), pl.BlockSpec((GDN_CONV, LANE), lambda j: (0, j))],
        out_specs=pl.BlockSpec((T, LANE), lambda j: (0, j)),
        out_shape=jax.ShapeDtypeStruct((T, nb * LANE), F32),
        compiler_params=_cp(("parallel",)),
    )(proj, conv_w)


def _gdn_pre_bwd(proj, conv_w, dout):
    T = proj.shape[0]
    nb = 3 * GDN_HEADS
    H = GDN_HEADS

    def body(x_ref, w_ref, do_ref, dx_ref, dw_ref):
        j = pl.program_id(0)
        t_idx = lax.broadcasted_iota(jnp.int32, (T, LANE), 0)
        x, w = x_ref[...], _rows(w_ref, GDN_CONV)
        c = _conv(x, w, t_idx)
        s = _silu(c)
        rn = lax.rsqrt(jnp.sum(s * s, axis=-1, keepdims=True) + EPS)
        do = do_ref[...]
        y = s * rn
        ds_normed = rn * (do - y * jnp.sum(do * y, axis=-1, keepdims=True))
        ds = jnp.where(j < N_QK_BLOCKS, ds_normed, do)
        dx, dw = _conv_bwd(x, w, ds * _dsilu(c), t_idx)
        dx_ref[...] = dx.astype(BF16)
        _store_rows(dw_ref, dw)

    col = pl.BlockSpec((T, LANE), lambda j: (0, j))
    wcol = pl.BlockSpec((GDN_CONV, LANE), lambda j: (0, j))
    return pl.pallas_call(
        body, name="gdn_pre_bwd", grid=(nb,),
        in_specs=[col, wcol, pl.BlockSpec((None, None, T, LANE), lambda j: (j // H, j % H, 0, 0))],
        out_specs=[col, wcol],
        out_shape=[jax.ShapeDtypeStruct((T, nb * LANE), BF16), jax.ShapeDtypeStruct((GDN_CONV, nb * LANE), F32)],
        compiler_params=_cp(("parallel",)),
    )(proj, conv_w, dout)


def _gdn_consts():
    C = GDN_CHUNK
    r = lax.broadcasted_iota(jnp.int32, (C, C), 0)
    c = lax.broadcasted_iota(jnp.int32, (C, C), 1)
    return dict(incl=r >= c, strict=r > c, eye=(r == c).astype(F32),
                low=(r >= c).astype(BF16), up=(r <= c).astype(BF16), ones=jnp.ones((C, C), BF16))


def _unit_lower_inverse_raw(a_mat, eye):
    inv = eye - a_mat
    pw = _dot3_raw(a_mat, a_mat, "nn")
    n_factors = a_mat.shape[-1].bit_length() - 2
    for f in range(n_factors):
        inv = inv + _dot3_raw(inv, pw, "nn")
        if f < n_factors - 1:
            pw = _dot3_raw(pw, pw, "nn")
    return inv


@jax.custom_vjp
def _unit_lower_inverse(a_mat, eye):
    return _unit_lower_inverse_raw(a_mat, eye)


def _unit_lower_inverse_fwd(a_mat, eye):
    inv = _unit_lower_inverse_raw(a_mat, eye)
    return inv, (inv, eye)


def _unit_lower_inverse_bwd(res, g):
    inv, eye = res
    return -_dot(_dot(inv, g, "tn"), inv, "nt"), jnp.zeros_like(eye)


_unit_lower_inverse.defvjp(_unit_lower_inverse_fwd, _unit_lower_inverse_bwd)


def _gdn_prep_chunk(q, k, v, b, a, alog, dtb, cs):
    n, C, dk = q.shape
    beta = _sigmoid(b)
    g = -jnp.exp(alog) * _softplus(a + dtb)
    g_sq = jnp.broadcast_to(g, (n, C, C))
    g_wide = jnp.broadcast_to(g, (n, C, dk))
    gc_i = _m01_left(cs["low"], cs["up"], g_sq)
    gc_j = _m01_left(cs["ones"], cs["ones"], g_sq * cs["up"].astype(F32))
    gc_wide = _m01_left(cs["low"], cs["up"], g_wide)
    gl_wide = _m01_left(cs["ones"], cs["ones"], g_wide)
    decay = jnp.where(cs["incl"], jnp.exp(jnp.where(cs["incl"], gc_i - gc_j, 0.0)), 0.0)
    egc = jnp.exp(gc_wide)
    qs = q * (dk ** -0.5)
    k_beta = k * beta
    a_mat = jnp.where(cs["strict"], _dot1_nt(k_beta, k) * decay, 0.0)
    inv = _unit_lower_inverse(a_mat, cs["eye"])
    u = _dot3(inv, v * beta)
    w = _dot3(inv, k_beta * egc)
    qk = _dot1_nt(qs, k) * decay
    q_dec = qs * egc
    k_dec = k * jnp.exp(gl_wide - gc_wide)
    g_last = jnp.exp(gl_wide)[:, 0:8, :]
    return qk, u, w, q_dec, k_dec, g_last


GDN_PREP_CHUNKS = 8


def _gdn_prep_specs(T):
    C, dk = GDN_CHUNK, GDN_HEAD_DIM
    npc = min(GDN_PREP_CHUNKS, T // C)
    tc = npc * C
    H = GDN_HEADS
    in_specs = [pl.BlockSpec((tc, dk), lambda h, i: (i, h)),
                pl.BlockSpec((tc, dk), lambda h, i: (i, H + h)),
                pl.BlockSpec((tc, dk), lambda h, i: (i, 2 * H + h)),
                pl.BlockSpec((1, tc, 1), lambda h, i: (h, i, 0)),
                pl.BlockSpec((1, tc, 1), lambda h, i: (h, i, 0)),
                pl.BlockSpec((1, 1, 1), lambda h, i: (h, 0, 0)),
                pl.BlockSpec((1, 1, 1), lambda h, i: (h, 0, 0))]
    xs_specs = [pl.BlockSpec((1, tc, C), lambda h, i: (h, i, 0)),
                pl.BlockSpec((1, tc, dk), lambda h, i: (h, i, 0)),
                pl.BlockSpec((1, tc, dk), lambda h, i: (h, i, 0)),
                pl.BlockSpec((1, tc, dk), lambda h, i: (h, i, 0)),
                pl.BlockSpec((1, tc, dk), lambda h, i: (h, i, 0)),
                pl.BlockSpec((1, npc * 8, dk), lambda h, i: (h, i, 0))]
    xs_shapes = [jax.ShapeDtypeStruct((H, T, C), F32)] + [jax.ShapeDtypeStruct((H, T, dk), F32)] * 4 + [
        jax.ShapeDtypeStruct((H, 8 * T // C, dk), F32)]
    return npc, tc, in_specs, xs_specs, xs_shapes


def _gdn_prep_fwd(qkv, b, a, alog, dtb):
    T = qkv.shape[0]
    C = GDN_CHUNK
    npc, tc, in_specs, xs_specs, xs_shapes = _gdn_prep_specs(T)

    def body(q_ref, k_ref, v_ref, b_ref, a_ref, al_ref, dt_ref, qk_ref, u_ref, w_ref, qd_ref, kd_ref, gl_ref):
        cs = _gdn_consts()

        def chunks(val):
            return val.reshape(npc, C, val.shape[-1])

        outs = _gdn_prep_chunk(chunks(q_ref[...]), chunks(k_ref[...]), chunks(v_ref[...]), chunks(b_ref[0]),
                               chunks(a_ref[0]), al_ref[0], dt_ref[0], cs)
        for ref, val in zip((qk_ref, u_ref, w_ref, qd_ref, kd_ref), outs[:5]):
            ref[0] = val.reshape(tc, val.shape[-1])
        gl_ref[0] = outs[5].reshape(npc * 8, outs[5].shape[-1])

    return pl.pallas_call(
        body, name="gdn_prep_fwd", grid=(GDN_HEADS, T // tc),
        in_specs=in_specs, out_specs=xs_specs, out_shape=xs_shapes,
        compiler_params=_cp(("parallel", "parallel")),
    )(qkv, qkv, qkv, b, a, alog, dtb)


def _gdn_prep_bwd(qkv, b, a, alog, dtb, dxs):
    T = qkv.shape[0]
    C, dk, H = GDN_CHUNK, GDN_HEAD_DIM, GDN_HEADS
    npc, tc, in_specs, xs_specs, _ = _gdn_prep_specs(T)

    def body(q_ref, k_ref, v_ref, b_ref, a_ref, al_ref, dt_ref, dqk_ref, du_ref, dw_ref, dqd_ref, dkd_ref, dgl_ref,
             dqkv_ref, db_ref, da_ref, dal_ref, ddt_ref):
        i = pl.program_id(1)
        cs = _gdn_consts()
        r8 = lax.broadcasted_iota(jnp.int32, (8, dk), 0)
        c8 = lax.broadcasted_iota(jnp.int32, (8, dk), 1)
        first = (r8 == 0) & (c8 == 0)

        @pl.when(i == 0)
        def _():
            dal_ref[...] = jnp.zeros_like(dal_ref)
            ddt_ref[...] = jnp.zeros_like(ddt_ref)

        def chunks(val):
            return val.reshape(npc, C, val.shape[-1])

        prim = (chunks(q_ref[...]), chunks(k_ref[...]), chunks(v_ref[...]), chunks(b_ref[0]), chunks(a_ref[0]),
                al_ref[0], dt_ref[0])
        _, vjp = jax.vjp(lambda *p: _gdn_prep_chunk(*p, cs), *prim)
        dgl = jnp.where(first, dgl_ref[0].reshape(npc, 8, dk), 0.0)
        cts = tuple(chunks(r[0]) for r in (dqk_ref, du_ref, dw_ref, dqd_ref, dkd_ref)) + (dgl,)
        dq, dkk, dv, db, da, dal, ddt = vjp(cts)
        for part, val in enumerate((dq, dkk, dv)):
            dqkv_ref[part, 0] = val.reshape(tc, dk)
        db_ref[0] = db.reshape(tc, 1)
        da_ref[0] = da.reshape(tc, 1)
        dal_ref[0] += dal
        ddt_ref[0] += ddt

    thin = pl.BlockSpec((1, tc, 1), lambda h, i: (h, i, 0))
    one = pl.BlockSpec((1, 1, 1), lambda h, i: (h, 0, 0))
    return pl.pallas_call(
        body, name="gdn_prep_bwd", grid=(H, T // tc),
        in_specs=in_specs + xs_specs,
        out_specs=[pl.BlockSpec((3, 1, tc, dk), lambda h, i: (0, h, i, 0)), thin, thin, one, one],
        out_shape=[jax.ShapeDtypeStruct((3, H, T, dk), F32)] + [jax.ShapeDtypeStruct((H, T, 1), F32)] * 2
        + [jax.ShapeDtypeStruct((H, 1, 1), F32)] * 2,
        compiler_params=_cp(("parallel", "arbitrary")),
    )(qkv, qkv, qkv, b, a, alog, dtb, *dxs)


def _gdn_scan_specs(T):
    C, dk, H = GDN_CHUNK, GDN_HEAD_DIM, GDN_HEADS
    return [pl.BlockSpec((H, C, C), lambda n: (0, n, 0))] + [pl.BlockSpec((H, C, dk), lambda n: (0, n, 0))] * 4 + [
        pl.BlockSpec((H, 8, dk), lambda n: (0, n, 0))]


def _gdn_scan_fwd(xs):
    H, T, dk = xs[1].shape
    C = GDN_CHUNK
    n = T // C

    def body(qk_ref, u_ref, w_ref, qd_ref, kd_ref, gl_ref, o_ref, s_ref, state):
        c = pl.program_id(0)

        @pl.when(c == 0)
        def _():
            state[...] = jnp.zeros_like(state)

        S = state[...]
        s_ref[0] = S
        v_new = u_ref[...] - _bdot(w_ref[...], S)
        o_ref[...] = _bdot(qd_ref[...], S) + _bdot(qk_ref[...], v_new)
        state[...] = S * jnp.tile(gl_ref[...], (1, dk // 8, 1)) + _bdot(kd_ref[...], v_new, "tn")

    return pl.pallas_call(
        body, name="gdn_scan_fwd", grid=(n,),
        in_specs=_gdn_scan_specs(T),
        out_specs=[pl.BlockSpec((H, C, dk), lambda n: (0, n, 0)), pl.BlockSpec((1, H, dk, dk), lambda n: (n, 0, 0, 0))],
        out_shape=[jax.ShapeDtypeStruct((H, T, dk), F32), jax.ShapeDtypeStruct((n, H, dk, dk), F32)],
        scratch_shapes=[pltpu.VMEM((H, dk, dk), F32)],
        compiler_params=_cp(("arbitrary",)),
    )(*xs)


def _gdn_scan_bwd(xs, states, do):
    H, T, dk = xs[1].shape
    C = GDN_CHUNK
    n = T // C

    def rev(spec_shape, f):
        return pl.BlockSpec(spec_shape, lambda i: f(n - 1 - i))

    def body(qk_ref, u_ref, w_ref, qd_ref, kd_ref, gl_ref, s_ref, do_ref,
             dqk_ref, du_ref, dw_ref, dqd_ref, dkd_ref, dgl_ref, dstate):
        i = pl.program_id(0)

        @pl.when(i == 0)
        def _():
            dstate[...] = jnp.zeros_like(dstate)

        S = s_ref[0]
        dS = dstate[...]
        do_v = do_ref[...]
        qk, w, qd, kd = qk_ref[...], w_ref[...], qd_ref[...], kd_ref[...]
        v_new = u_ref[...] - _bdot(w, S)
        dv_new = _bdot(qk, do_v, "tn") + _bdot(kd, dS)
        dqk_ref[...] = _bdot(do_v, v_new, "nt")
        dqd_ref[...] = _bdot(do_v, S, "nt")
        dkd_ref[...] = _bdot(v_new, dS, "nt")
        du_ref[...] = dv_new
        dw_ref[...] = -_bdot(dv_new, S, "nt")
        dgl = jnp.sum(jnp.sum(S * dS, axis=2, keepdims=True), axis=1, keepdims=True)
        dgl_ref[...] = jnp.broadcast_to(dgl, dgl_ref.shape)
        dstate[...] = (dS * jnp.tile(gl_ref[...], (1, dk // 8, 1)) + _bdot(qd, do_v, "tn")
                       - _bdot(w, dv_new, "tn"))

    in_specs = [rev((H, C, C), lambda m: (0, m, 0))] + [rev((H, C, dk), lambda m: (0, m, 0))] * 4 + [
        rev((H, 8, dk), lambda m: (0, m, 0)), rev((1, H, dk, dk), lambda m: (m, 0, 0, 0)),
        rev((H, C, dk), lambda m: (0, m, 0))]
    out_specs = [rev((H, C, C), lambda m: (0, m, 0))] + [rev((H, C, dk), lambda m: (0, m, 0))] * 4 + [
        rev((H, 8, dk), lambda m: (0, m, 0))]
    out_shape = [jax.ShapeDtypeStruct((H, T, C), F32)] + [jax.ShapeDtypeStruct((H, T, dk), F32)] * 4 + [
        jax.ShapeDtypeStruct((H, 8 * n, dk), F32)]
    return pl.pallas_call(
        body, name="gdn_scan_bwd", grid=(n,),
        in_specs=in_specs, out_specs=out_specs, out_shape=out_shape,
        scratch_shapes=[pltpu.VMEM((H, dk, dk), F32)],
        compiler_params=_cp(("arbitrary",)),
    )(*xs, states, do)


def _gdn_post_fwd(o, proj, norm_w):
    H, T, dk = o.shape
    tt = _tile(T, 1024)
    zoff = 3 * GDN_HEADS

    def body(o_ref, z_ref, g_ref, y_ref):
        ov = o_ref[0]
        r = lax.rsqrt(jnp.mean(ov * ov, axis=-1, keepdims=True) + EPS)
        y_ref[...] = (ov * r * g_ref[...] * _silu(z_ref[...])).astype(BF16)

    return pl.pallas_call(
        body, name="gdn_post_fwd", grid=(H, T // tt),
        in_specs=[pl.BlockSpec((1, tt, dk), lambda h, i: (h, i, 0)), pl.BlockSpec((tt, dk), lambda h, i: (i, zoff + h)),
                  pl.BlockSpec((1, dk), lambda h, i: (0, 0))],
        out_specs=pl.BlockSpec((tt, dk), lambda h, i: (i, h)),
        out_shape=jax.ShapeDtypeStruct((T, H * dk), BF16),
        compiler_params=_cp(("parallel", "parallel")),
    )(o, proj, norm_w)


def _gdn_post_bwd(o, proj, norm_w, dy):
    H, T, dk = o.shape
    tt = _tile(T, 1024)
    zoff = 3 * GDN_HEADS

    def body(o_ref, z_ref, g_ref, dy_ref, do_ref, dz_ref, dg_ref):
        i = pl.program_id(1)
        ov, z, g, dyv = o_ref[0], z_ref[...], g_ref[...], dy_ref[...]
        r = lax.rsqrt(jnp.mean(ov * ov, axis=-1, keepdims=True) + EPS)
        on = ov * r
        sz = _silu(z)
        dz_ref[...] = (dyv * on * g * _dsilu(z)).astype(BF16)
        dn = dyv * sz
        gdn = dn * g
        do_ref[0] = r * (gdn - on * jnp.mean(gdn * on, axis=-1, keepdims=True))

        @pl.when(i == 0)
        def _():
            dg_ref[...] = jnp.zeros_like(dg_ref)

        dg_ref[0] += jnp.sum(dn * on, axis=0, keepdims=True)

    return pl.pallas_call(
        body, name="gdn_post_bwd", grid=(H, T // tt),
        in_specs=[pl.BlockSpec((1, tt, dk), lambda h, i: (h, i, 0)), pl.BlockSpec((tt, dk), lambda h, i: (i, zoff + h)),
                  pl.BlockSpec((1, dk), lambda h, i: (0, 0)), pl.BlockSpec((tt, dk), lambda h, i: (i, h))],
        out_specs=[pl.BlockSpec((1, tt, dk), lambda h, i: (h, i, 0)), pl.BlockSpec((tt, dk), lambda h, i: (i, h)),
                   pl.BlockSpec((1, 1, dk), lambda h, i: (h, 0, 0))],
        out_shape=[jax.ShapeDtypeStruct((H, T, dk), F32), jax.ShapeDtypeStruct((T, H * dk), BF16),
                   jax.ShapeDtypeStruct((H, 1, dk), F32)],
        compiler_params=_cp(("parallel", "arbitrary")),
    )(o, proj, norm_w, dy)


def _ple_fwd(x, pp, gl, name):
    T, D = x.shape
    tt = _tile(T, 512)

    def body(x_ref, p_ref, g_ref, o_ref):
        o_ref[...] = x_ref[...] + p_ref[...] * _sigmoid(g_ref[...])

    row = pl.BlockSpec((tt, D), lambda i: (i, 0))
    return pl.pallas_call(
        body, name=name, grid=(T // tt,), in_specs=[row, row, row], out_specs=row,
        out_shape=jax.ShapeDtypeStruct((T, D), F32), compiler_params=_cp(("parallel",)),
    )(x, pp, gl)


def _ple_bwd(dx, pp, gl, name, after=None):
    T, D = dx.shape
    tt = _tile(T, 512)

    def body(dx_ref, p_ref, g_ref, *rest):
        dp_ref, dg_ref = rest[-2:]
        s = _sigmoid(g_ref[...])
        dxv = dx_ref[...]
        dp_ref[...] = (dxv * s).astype(BF16)
        dg_ref[...] = (dxv * p_ref[...] * s * (1.0 - s)).astype(BF16)

    row = pl.BlockSpec((tt, D), lambda i: (i, 0))
    ordered = [] if after is None else [after]
    return pl.pallas_call(
        body, name=name, grid=(T // tt,),
        in_specs=[row, row, row] + [pl.BlockSpec((8, LANE), lambda i: (0, 0)) for _ in ordered], out_specs=[row, row],
        out_shape=[jax.ShapeDtypeStruct((T, D), BF16)] * 2, compiler_params=_cp(("parallel",)),
    )(dx, pp, gl, *ordered)


UP_SHARD = 2 * FFN_DIM // N_CHIPS
DOWN_SHARD = FFN_DIM // N_CHIPS
GATE_SHARD = D_MODEL // N_CHIPS
IN_E_SHARD = 2 * D_MODEL // N_CHIPS


def _ffn_fwd(x, norm, W, conv_w, l):
    T = x.shape[0]
    hf = _rms_fwd(x, norm, f"ffn_norm{l}")
    up = _mm(hf, W["w_up"], "nn", f"ffn_up{l}", dims=(T, 2 * FFN_DIM, D_MODEL), b_view=_cols_of(1, l), tn=UP_SHARD,
             out_dtype=BF16)
    act = _ffn_act_fwd(up, conv_w, f"ffn_act{l}")
    x_out = _mm(act, W["w_down"], "nn", f"ffn_down{l}", dims=(T, D_MODEL, FFN_DIM), b_view=_layer_of(l), res=x,
                tn=1024, tk=1408)
    return x_out, (x, hf, up, act)


def _ffn_bwd(dx_out, saved, norm, W, conv_w, l):
    x, hf, up, act = saved
    T = x.shape[0]
    dact = _mm(dx_out, W["w_down"], "nt", f"ffn_dact{l}", dims=(T, FFN_DIM, D_MODEL), b_view=_layer_of(l),
               out_dtype=BF16, tn=1408)
    dw_down = _mm(act, dx_out, "tn", f"ffn_dwdown{l}", tm=1408, tn=1024, tk=512)
    dup, dcw_g, dcw_v = _ffn_act_bwd(up, conv_w, dact, f"ffn_dact_conv{l}")
    dw_up = _mm(hf, dup, "tn", f"ffn_dwup{l}", dims=(D_MODEL, 2 * FFN_DIM, T), b_view=_cols_of(FFN_DIM // UP_SHARD),
                o_view=_cols_of(1), out_shape=(N_CHIPS, D_MODEL, UP_SHARD), tm=1024, tn=UP_SHARD, tk=512)
    dhf = _mm(dup, W["w_up"], "nt", f"ffn_dhf{l}", dims=(T, D_MODEL, 2 * FFN_DIM),
              a_view=_cols_of(FFN_DIM // UP_SHARD), b_view=_cols_of(1, l), tn=1024, tk=UP_SHARD)
    dx, dnorm = _rms_bwd(x, norm, dhf, dx_out, f"ffn_dnorm{l}")
    return (dx, dnorm, dw_up, jnp.concatenate([dcw_g, dcw_v], axis=1),
            dw_down.reshape(N_CHIPS, DOWN_SHARD, D_MODEL))


def _ple_layer_fwd(x, p, norm, W, l):
    T = x.shape[0]
    hg = _rms_fwd(x, norm, f"ple_norm{l}")
    gl = _mm(hg, W["w_ple_gate"], "nn", f"ple_gate{l}", dims=(T, D_MODEL, D_MODEL), b_view=_layer_of(l), tn=1024)
    pp = _mm(p, W["w_ple"], "nn", f"ple_proj{l}", dims=(T, D_MODEL, PLE_DIM), a_view=_layer_of(l),
             b_view=_layer_of(l), tn=1024)
    return _ple_fwd(x, pp, gl, f"ple_mix{l}"), (x, hg, gl, pp)


def _ple_layer_bwd(dx_out, saved, p, norm, W, l, after=None):
    x, hg, gl, pp = saved
    T = x.shape[0]
    dpp, dgl = _ple_bwd(dx_out, pp, gl, f"ple_dmix{l}", after)
    dw_ple = _mm(p, dpp, "tn", f"ple_dwple{l}", dims=(PLE_DIM, D_MODEL, T), a_view=_layer_of(l), o_view=_cols_of(1),
                 out_shape=(N_CHIPS, PLE_DIM, PLE_DIM), tm=PLE_DIM, tn=PLE_DIM, tk=512)
    dw_gate = _mm(hg, dgl, "tn", f"ple_dwgate{l}", tm=1024, tn=1024, tk=512)
    dhg = _mm(dgl, W["w_ple_gate"], "nt", f"ple_dhg{l}", dims=(T, D_MODEL, D_MODEL), b_view=_layer_of(l), tn=1024)
    dx, dnorm = _rms_bwd(x, norm, dhg, dx_out, f"ple_dnorm{l}")
    return dx, dnorm, dw_gate.reshape(N_CHIPS, GATE_SHARD, D_MODEL), dw_ple


def _local_step(x, p, target, W, token=None, later_weights=None, on_second_layer_grads=None):
    T = x.shape[0]
    H = GDN_HEADS
    G = {}

    hn_e = _rms_fwd(x, W["mix_norm_e"], "mix_norm_e", after=token)
    proj_e = _mm(hn_e, W["w_in_e"], "nn", "in_e", dims=(T, 2 * D_MODEL, D_MODEL), b_view=_cols_of(1), tn=IN_E_SHARD)
    pool_out = _pool_fwd(proj_e, W["pool_w"], W["pool_scale"])
    attn, ltot = _sb_fwd(proj_e)
    mix_e = jnp.concatenate([pool_out, attn], axis=1).astype(BF16)
    x1 = _mm(mix_e, W["w_out_e"], "nn", "out_e", res=x, tn=1024)
    if later_weights is not None:
        W = {**W, **later_weights(x1)}
    x2, ffn0 = _ffn_fwd(x1, W["ffn_norm"][0:1], W, W["ffn_conv"][0], 0)
    x3, ple0 = _ple_layer_fwd(x2, p, W["ple_norm"][0:1], W, 0)

    hn_o = _rms_fwd(x3, W["mix_norm_o"], "mix_norm_o")
    proj_o = _mm(hn_o, W["w_in_o"], "nn", "in_o", tn=1408)
    qkv = _gdn_pre_fwd(proj_o, W["conv_qkv_o"])
    ba = proj_o[:, 4 * D_MODEL:4 * D_MODEL + 2 * H]
    b_h = ba[:, :H].T.reshape(H, T, 1)
    a_h = ba[:, H:].T.reshape(H, T, 1)
    alog = W["a_log_o"].reshape(H, 1, 1)
    dtb = W["dt_bias_o"].reshape(H, 1, 1)
    xs = _gdn_prep_fwd(qkv, b_h, a_h, alog, dtb)
    o, states = _gdn_scan_fwd(xs)
    og = _gdn_post_fwd(o, proj_o, W["gdn_norm_o"])
    x4 = _mm(og, W["w_out_o"], "nn", "out_o", res=x3, tn=1024)
    x5, ffn1 = _ffn_fwd(x4, W["ffn_norm"][1:2], W, W["ffn_conv"][1], 1)
    x6, ple1 = _ple_layer_fwd(x5, p, W["ple_norm"][1:2], W, 1)

    sq, dx6, G["final_norm"] = _final_loss(x6, W["final_norm"], target, "final_loss")

    dx5, dpn1, dwg1, dwp1 = _ple_layer_bwd(dx6, ple1, p, W["ple_norm"][1:2], W, 1)
    dx4, dfn1, dwu1, dfc1, dwd1 = _ffn_bwd(dx5, ffn1, W["ffn_norm"][1:2], W, W["ffn_conv"][1], 1)
    dog = _mm(dx4, W["w_out_o"], "nt", "d_og", tn=1024)
    G["w_out_o"] = _mm(og, dx4, "tn", "dw_out_o", tm=1024, tn=1024, tk=512).reshape(N_CHIPS, GATE_SHARD, D_MODEL)
    do, dz, dgn = _gdn_post_bwd(o, proj_o, W["gdn_norm_o"], dog)
    G["gdn_norm_o"] = jnp.sum(dgn, axis=0)
    dxs = _gdn_scan_bwd(xs, states, do)
    dqkv_act, db, da, dal, ddt = _gdn_prep_bwd(qkv, b_h, a_h, alog, dtb, dxs)
    G["a_log_o"] = dal.reshape(1, H)
    G["dt_bias_o"] = ddt.reshape(1, H)
    dqkv, G["conv_qkv_o"] = _gdn_pre_bwd(proj_o, W["conv_qkv_o"], dqkv_act)
    dba = jnp.concatenate([db.reshape(H, T).T, da.reshape(H, T).T,
                           jnp.zeros((T, ODD_IN_PAD - ODD_IN + 0), F32)], axis=1).astype(BF16)
    dproj_o = jnp.concatenate([dqkv, dz, dba], axis=1)
    G["w_in_o"] = _mm(hn_o, dproj_o, "tn", "dw_in_o", tm=1024, tn=1408, tk=512)
    dhn_o = _mm(dproj_o, W["w_in_o"], "nt", "d_hn_o", tn=1024, tk=1408)
    dx3, G["mix_norm_o"] = _rms_bwd(x3, W["mix_norm_o"], dhn_o, dx4, "d_mix_norm_o")

    token1 = None
    if on_second_layer_grads is not None:
        token1 = on_second_layer_grads({("w_in_o", 0): _chip_major_w_in_o(G["w_in_o"]), ("w_out_o", 0): G["w_out_o"],
                                        ("w_up", 1): dwu1, ("w_down", 1): dwd1, ("w_ple_gate", 1): dwg1,
                                        ("w_ple", 1): dwp1})
    dx2, dpn0, dwg0, dwp0 = _ple_layer_bwd(dx3, ple0, p, W["ple_norm"][0:1], W, 0, after=token1)
    dx1, dfn0, dwu0, dfc0, dwd0 = _ffn_bwd(dx2, ffn0, W["ffn_norm"][0:1], W, W["ffn_conv"][0], 0)
    dmix = _mm(dx1, W["w_out_e"], "nt", "d_mix_e", tn=1024)
    G["w_out_e"] = _mm(mix_e, dx1, "tn", "dw_out_e", tm=1024, tn=1024, tk=512).reshape(N_CHIPS, GATE_SHARD, D_MODEL)
    du, G["pool_w"], G["pool_scale"] = _pool_bwd(proj_e, W["pool_w"], W["pool_scale"], dmix)
    dqa, dka, dva = _sb_bwd(proj_e, dmix, ltot)
    dproj_e = jnp.concatenate([du, dqa, dka, dva], axis=1).astype(BF16)
    G["w_in_e"] = _mm(hn_e, dproj_e, "tn", "dw_in_e", dims=(D_MODEL, 2 * D_MODEL, T), o_view=_cols_of(1),
                      out_shape=(N_CHIPS, D_MODEL, IN_E_SHARD), tm=1024, tn=IN_E_SHARD, tk=512)
    dhn_e = _mm(dproj_e, W["w_in_e"], "nt", "d_hn_e", dims=(T, D_MODEL, 2 * D_MODEL), b_view=_cols_of(1),
                tn=1024, tk=IN_E_SHARD)
    grad_x, G["mix_norm_e"] = _rms_bwd(x, W["mix_norm_e"], dhn_e, dx1, "d_mix_norm_e")

    G["ffn_norm"] = jnp.concatenate([dfn0, dfn1], axis=0)
    G["ple_norm"] = jnp.concatenate([dpn0, dpn1], axis=0)
    G["ffn_conv"] = jnp.stack([dfc0, dfc1])
    G["w_up"] = [dwu0, dwu1]
    G["w_down"] = [dwd0, dwd1]
    G["w_ple_gate"] = [dwg0, dwg1]
    G["w_ple"] = [dwp0, dwp1]
    return sq[0, 0], grad_x, G


BIG = ("w_in_e", "w_out_e", "w_in_o", "w_out_o", "w_up", "w_down", "w_ple_gate", "w_ple")
SHARDED_SMALL = (("mix_norm_o", 1), ("conv_qkv_o", 2), ("ffn_conv", 2))
REPLICATED = ("mix_norm_e", "pool_w", "pool_scale", "a_log_o", "dt_bias_o", "gdn_norm_o", "ffn_norm", "ple_norm",
              "final_norm")
WEIGHT_ORDER = ("mix_norm_e", "w_in_e", "pool_w", "pool_scale", "w_out_e", "mix_norm_o", "w_in_o", "conv_qkv_o",
                "a_log_o", "dt_bias_o", "gdn_norm_o", "w_out_o", "ffn_norm", "w_up", "ffn_conv", "w_down", "ple_norm",
                "w_ple_gate", "w_ple", "final_norm")
SMALL_W = LANE
SMALL_ROWS = 16


def _size(shape):
    n = 1
    for s in shape:
        n *= s
    return n


def _pack(arrs, width, granule):
    flat = jnp.concatenate([a.reshape(-1) for a in arrs])
    rows = -(-flat.shape[0] // width)
    rows = -(-rows // granule) * granule
    return jnp.pad(flat, (0, rows * width - flat.shape[0])).reshape(rows, width)


def _unpack(flat2d, shapes):
    flat = flat2d.reshape(-1)
    out, off = [], 0
    for s in shapes:
        out.append(flat[off:off + _size(s)].reshape(s))
        off += _size(s)
    return out


MESH_ID = pl.DeviceIdType.MESH
HBM_SPEC = pl.BlockSpec(memory_space=pltpu.HBM)


def _where_am_i():
    return lax.axis_index("x"), lax.axis_index("y"), lax.axis_index("c")


def _other_chips(x, y):
    return [(1 - x, y), (x, 1 - y), (1 - x, 1 - y)]


def _remote(src, dst, send_sems, recv_sems, k, to):
    return pltpu.make_async_remote_copy(src_ref=src, dst_ref=dst, send_sem=send_sems.at[k], recv_sem=recv_sems.at[k],
                                        device_id=to, device_id_type=MESH_ID)


def _chip_allgather(pack, name):
    R, Wd = pack.shape
    Rh = R // 2

    def body(src_ref, out_ref, send_sems, recv_sems, local_sem):
        x, y, c = _where_am_i()
        me, sib = (x, y, c), (x, y, 1 - c)
        chips = _other_chips(x, y)
        mine_rows = pl.ds(pl.multiple_of(c * Rh, SMALL_ROWS), Rh)
        sib_rows = pl.ds(pl.multiple_of((1 - c) * Rh, SMALL_ROWS), Rh)
        j_me = 2 * x + y
        local = pltpu.make_async_copy(src_ref, out_ref.at[j_me], local_sem)
        local.start()
        first = [_remote(src_ref.at[mine_rows], out_ref.at[j_me, mine_rows], send_sems, recv_sems, k, (cx, cy, c))
                 for k, (cx, cy) in enumerate(chips)]
        for cp in first:
            cp.start()
        passed = []
        for k, (cx, cy) in enumerate(chips):
            blk = out_ref.at[2 * cx + cy, mine_rows]
            _remote(blk, blk, send_sems, recv_sems, k, me).wait_recv()
            fw = _remote(blk, blk, send_sems, recv_sems, 3 + k, sib)
            fw.start()
            passed.append(fw)
        for k, (cx, cy) in enumerate(chips):
            blk = out_ref.at[2 * cx + cy, sib_rows]
            _remote(blk, blk, send_sems, recv_sems, 3 + k, me).wait_recv()
        for cp in first + passed:
            cp.wait_send()
        local.wait()

    return pl.pallas_call(
        body, name=name, in_specs=[HBM_SPEC], out_specs=HBM_SPEC,
        out_shape=jax.ShapeDtypeStruct((N_CHIPS, R, Wd), pack.dtype),
        scratch_shapes=[pltpu.SemaphoreType.DMA((6,)), pltpu.SemaphoreType.DMA((6,)), pltpu.SemaphoreType.DMA],
    )(pack)


def _chip_allgather_many(blocks, name):
    n = len(blocks)

    def body(*refs):
        srcs, outs = refs[:n], refs[n:2 * n]
        send_sems, recv_sems = refs[2 * n:]
        x, y, c = _where_am_i()
        me, sib = (x, y, c), (x, y, 1 - c)
        chips = _other_chips(x, y)
        j_me = 2 * x + y
        first = [_remote(srcs[p].at[c], outs[p].at[j_me, c], send_sems, recv_sems, 6 * p + k, (cx, cy, c))
                 for p in range(n) for k, (cx, cy) in enumerate(chips)]
        for cp in first:
            cp.start()
        passed = []
        for k, (cx, cy) in enumerate(chips):
            for p in range(n):
                blk = outs[p].at[2 * cx + cy, c]
                _remote(blk, blk, send_sems, recv_sems, 6 * p + k, me).wait_recv()
                fw = _remote(blk, blk, send_sems, recv_sems, 6 * p + 3 + k, sib)
                fw.start()
                passed.append(fw)
        for k, (cx, cy) in enumerate(chips):
            for p in range(n):
                blk = outs[p].at[2 * cx + cy, 1 - c]
                _remote(blk, blk, send_sems, recv_sems, 6 * p + 3 + k, me).wait_recv()
        for cp in first + passed:
            cp.wait_send()

    return pl.pallas_call(
        body, name=name, in_specs=[HBM_SPEC] * n, out_specs=[HBM_SPEC] * n,
        out_shape=[jax.ShapeDtypeStruct((N_CHIPS,) + b.shape, b.dtype) for b in blocks],
        scratch_shapes=[pltpu.SemaphoreType.DMA((6 * n,)), pltpu.SemaphoreType.DMA((6 * n,))],
    )(*blocks)


SEM_SPEC = pl.BlockSpec(memory_space=pltpu.SEMAPHORE)
DATAFLOW_EFFECT = pltpu.SideEffectType.DATAFLOW_SIDE_EFFECTING


def _chip_allgather_start(blocks, name):
    n = len(blocks)

    def body(*refs):
        srcs, lands = refs[:n], refs[n:2 * n]
        send_sems, recv_sems, token = refs[2 * n], refs[2 * n + 1], refs[-1]
        x, y, c = _where_am_i()
        j_me = 2 * x + y
        for p in range(n):
            for k, (cx, cy) in enumerate(_other_chips(x, y)):
                _remote(srcs[p].at[c], lands[p].at[j_me, c], send_sems, recv_sems, 3 * p + k, (cx, cy, c)).start()
        token[...] = jnp.zeros_like(token)

    lands = [pltpu.with_memory_space_constraint(lax.empty((N_CHIPS,) + b.shape, b.dtype), pltpu.HBM) for b in blocks]
    blocks = [pltpu.with_memory_space_constraint(b, pltpu.HBM) for b in blocks]
    outs = pl.pallas_call(
        body, name=name,
        in_specs=[HBM_SPEC] * (2 * n),
        out_specs=[SEM_SPEC, SEM_SPEC] + [HBM_SPEC] * (2 * n) + [pl.BlockSpec(memory_space=pltpu.VMEM)],
        out_shape=[pltpu.SemaphoreType.DMA((3 * n,)), pltpu.SemaphoreType.DMA((3 * n,))]
        + [pltpu.HBM(a.shape, a.dtype) for a in blocks + lands] + [jax.ShapeDtypeStruct((8, LANE), F32)],
        input_output_aliases={i: 2 + i for i in range(2 * n)},
        compiler_params=pltpu.CompilerParams(has_side_effects=DATAFLOW_EFFECT),
    )(*blocks, *lands)
    return outs[0], outs[1], list(outs[2:2 + n]), list(outs[2 + n:2 + 2 * n]), outs[-1]


def _chip_allgather_wait(send_sems, recv_sems, blocks, lands, after, name):
    n = len(blocks)

    def body(*refs):
        srcs, zones = refs[:n], refs[n:2 * n]
        send, recv = refs[2 * n], refs[2 * n + 1]
        x, y, c = _where_am_i()
        for p in range(n):
            for k, (cx, cy) in enumerate(_other_chips(x, y)):
                cp = _remote(srcs[p].at[c], zones[p].at[2 * cx + cy, c], send, recv, 3 * p + k, (x, y, c))
                cp.wait_send()
                cp.wait_recv()

    outs = pl.pallas_call(
        body, name=name,
        in_specs=[HBM_SPEC] * (2 * n) + [SEM_SPEC, SEM_SPEC, pl.BlockSpec(memory_space=pl.ANY)],
        out_specs=[HBM_SPEC] * (2 * n),
        out_shape=[pltpu.HBM(a.shape, a.dtype) for a in list(blocks) + list(lands)],
        input_output_aliases={i: i for i in range(2 * n)},
        compiler_params=pltpu.CompilerParams(has_side_effects=DATAFLOW_EFFECT),
    )(*blocks, *lands, send_sems, recv_sems, after)
    return list(outs[n:])


def _chip_scatter_start(sums, name):
    n = len(sums)

    def body(*refs):
        srcs, lands = refs[:n], refs[n:2 * n]
        send_sems, recv_sems, token = refs[2 * n], refs[2 * n + 1], refs[-1]
        x, y, c = _where_am_i()
        for p in range(n):
            for k, (cx, cy) in enumerate(_other_chips(x, y)):
                _remote(srcs[p].at[2 * cx + cy], lands[p].at[k], send_sems, recv_sems, 3 * p + k, (cx, cy, c)).start()
        token[...] = jnp.zeros_like(token)

    lands = [pltpu.with_memory_space_constraint(lax.empty((N_CHIPS - 1,) + s.shape[1:], s.dtype), pltpu.HBM)
             for s in sums]
    sums = [pltpu.with_memory_space_constraint(s, pltpu.HBM) for s in sums]
    outs = pl.pallas_call(
        body, name=name,
        in_specs=[HBM_SPEC] * (2 * n),
        out_specs=[SEM_SPEC, SEM_SPEC] + [HBM_SPEC] * (2 * n) + [pl.BlockSpec(memory_space=pltpu.VMEM)],
        out_shape=[pltpu.SemaphoreType.DMA((3 * n,)), pltpu.SemaphoreType.DMA((3 * n,))]
        + [pltpu.HBM(a.shape, a.dtype) for a in sums + lands] + [jax.ShapeDtypeStruct((8, LANE), F32)],
        input_output_aliases={i: 2 + i for i in range(2 * n)},
        compiler_params=pltpu.CompilerParams(has_side_effects=DATAFLOW_EFFECT),
    )(*sums, *lands)
    return outs[0], outs[1], list(outs[2:2 + n]), list(outs[2 + n:2 + 2 * n]), outs[-1]


def _chip_scatter_wait(send_sems, recv_sems, sums, lands, after, name):
    n = len(sums)

    def body(*refs):
        srcs, zones = refs[:n], refs[n:2 * n]
        send, recv = refs[2 * n], refs[2 * n + 1]
        x, y, c = _where_am_i()
        for p in range(n):
            for k, (cx, cy) in enumerate(_other_chips(x, y)):
                cp = _remote(srcs[p].at[2 * cx + cy], zones[p].at[k], send, recv, 3 * p + k, (x, y, c))
                cp.wait_send()
                cp.wait_recv()

    outs = pl.pallas_call(
        body, name=name,
        in_specs=[HBM_SPEC] * (2 * n) + [SEM_SPEC, SEM_SPEC, pl.BlockSpec(memory_space=pl.ANY)],
        out_specs=[HBM_SPEC] * (2 * n),
        out_shape=[pltpu.HBM(a.shape, a.dtype) for a in list(sums) + list(lands)],
        input_output_aliases={i: i for i in range(2 * n)},
        compiler_params=pltpu.CompilerParams(has_side_effects=DATAFLOW_EFFECT),
    )(*sums, *lands, send_sems, recv_sems, after)
    return list(outs[n:])


def _chip_allgather_forward(lands, name):
    n = len(lands)

    def body(*refs):
        ins, outs = refs[:n], refs[n:2 * n]
        send_sems, recv_sems = refs[2 * n:]
        x, y, c = _where_am_i()
        me, sib = (x, y, c), (x, y, 1 - c)
        chips = _other_chips(x, y)
        passed = [_remote(ins[p].at[2 * cx + cy, c], outs[p].at[2 * cx + cy, c], send_sems, recv_sems, 3 * p + k, sib)
                  for p in range(n) for k, (cx, cy) in enumerate(chips)]
        for cp in passed:
            cp.start()
        for p in range(n):
            for k, (cx, cy) in enumerate(chips):
                blk = outs[p].at[2 * cx + cy, 1 - c]
                _remote(blk, blk, send_sems, recv_sems, 3 * p + k, me).wait_recv()
        for cp in passed:
            cp.wait_send()

    return pl.pallas_call(
        body, name=name, in_specs=[HBM_SPEC] * n, out_specs=[HBM_SPEC] * n,
        out_shape=[jax.ShapeDtypeStruct(a.shape, a.dtype) for a in lands],
        input_output_aliases={i: i for i in range(n)},
        scratch_shapes=[pltpu.SemaphoreType.DMA((3 * n,)), pltpu.SemaphoreType.DMA((3 * n,))],
    )(*lands)


def _sibling_swap_many(pieces, name):
    n = len(pieces)

    def body(*refs):
        srcs, outs = refs[:n], refs[n:2 * n]
        send_sems, recv_sems = refs[2 * n:]
        x, y, c = _where_am_i()
        cps = [_remote(srcs[p].at[:, 1 - c], outs[p], send_sems, recv_sems, p, (x, y, 1 - c)) for p in range(n)]
        for cp in cps:
            cp.start()
        for cp in cps:
            cp.wait()

    return pl.pallas_call(
        body, name=name, in_specs=[HBM_SPEC] * n, out_specs=[HBM_SPEC] * n,
        out_shape=[jax.ShapeDtypeStruct((g.shape[0],) + g.shape[2:], g.dtype) for g in pieces],
        scratch_shapes=[pltpu.SemaphoreType.DMA((n,)), pltpu.SemaphoreType.DMA((n,))],
    )(*pieces)


def _chip_scatter_many(sums, name):
    n = len(sums)

    def body(*refs):
        srcs, outs = refs[:n], refs[n:2 * n]
        send_sems, recv_sems = refs[2 * n:]
        x, y, c = _where_am_i()
        cps = [_remote(srcs[p].at[2 * cx + cy], outs[p].at[k], send_sems, recv_sems, 3 * p + k, (cx, cy, c))
               for p in range(n) for k, (cx, cy) in enumerate(_other_chips(x, y))]
        for cp in cps:
            cp.start()
        for cp in cps:
            cp.wait()

    return pl.pallas_call(
        body, name=name, in_specs=[HBM_SPEC] * n, out_specs=[HBM_SPEC] * n,
        out_shape=[jax.ShapeDtypeStruct((N_CHIPS - 1,) + s.shape[1:], s.dtype) for s in sums],
        scratch_shapes=[pltpu.SemaphoreType.DMA((3 * n,)), pltpu.SemaphoreType.DMA((3 * n,))],
    )(*sums)


def _sibling_send_many(halves, name):
    n = len(halves)

    def body(*refs):
        srcs, outs = refs[:n], refs[n:2 * n]
        send_sems, recv_sems = refs[2 * n:]
        x, y, c = _where_am_i()
        cps = [_remote(srcs[p], outs[p], send_sems, recv_sems, p, (x, y, 1 - c)) for p in range(n)]
        for cp in cps:
            cp.start()
        for cp in cps:
            cp.wait()

    return pl.pallas_call(
        body, name=name, in_specs=[HBM_SPEC] * n, out_specs=[HBM_SPEC] * n,
        out_shape=[jax.ShapeDtypeStruct(h.shape, h.dtype) for h in halves],
        scratch_shapes=[pltpu.SemaphoreType.DMA((n,)), pltpu.SemaphoreType.DMA((n,))],
    )(*halves)


def _row_tile(rows, pref=512):
    best = 8
    for t in range(8, pref + 1, 8):
        if rows % t == 0:
            best = t
    return best


def _where_ids():
    x, y, c = _where_am_i()
    return jnp.stack([c, 2 * x + y]).astype(jnp.int32)


RS_ROWS = 256


def _chip_sums_bf16(G, A, ids, name):
    n, _, hr, cols = G.shape
    tr = _row_tile(hr, RS_ROWS)

    def body(ids_ref, g_ref, a_ref, o_ref):
        o_ref[...] = (g_ref[...] + a_ref[...]).astype(BF16)

    return pl.pallas_call(
        body, name=name,
        grid_spec=pltpu.PrefetchScalarGridSpec(
            num_scalar_prefetch=1, grid=(n, hr // tr),
            in_specs=[pl.BlockSpec((None, None, tr, cols), lambda j, i, ids: (j, ids[0], i, 0)),
                      pl.BlockSpec((None, tr, cols), lambda j, i, ids: (j, i, 0))],
            out_specs=pl.BlockSpec((None, tr, cols), lambda j, i, ids: (j, i, 0))),
        out_shape=jax.ShapeDtypeStruct((n, hr, cols), BF16),
        compiler_params=_cp(("parallel", "parallel")),
    )(ids, G, A)


def _total_half(G, A, B, ids, name):
    _, _, hr, cols = G.shape
    tr = _row_tile(hr, RS_ROWS)

    def body(ids_ref, g_ref, a_ref, b_ref, o_ref):
        s = g_ref[...] + a_ref[...]
        for k in range(N_CHIPS - 1):
            s = s + b_ref[k].astype(F32)
        o_ref[...] = s

    return pl.pallas_call(
        body, name=name,
        grid_spec=pltpu.PrefetchScalarGridSpec(
            num_scalar_prefetch=1, grid=(hr // tr,),
            in_specs=[pl.BlockSpec((None, None, tr, cols), lambda i, ids: (ids[1], ids[0], i, 0)),
                      pl.BlockSpec((None, tr, cols), lambda i, ids: (ids[1], i, 0)),
                      pl.BlockSpec((N_CHIPS - 1, tr, cols), lambda i, ids: (0, i, 0))],
            out_specs=pl.BlockSpec((tr, cols), lambda i, ids: (i, 0))),
        out_shape=jax.ShapeDtypeStruct((hr, cols), F32),
        compiler_params=_cp(("parallel",)),
    )(ids, G, A, B)


def _small_allreduce(v, name):
    R, Wd = v.shape

    def body(x_ref, sum_ref, all_ref, send_sems, recv_sems, local_sem):
        x, y, c = _where_am_i()
        me, sib = (x, y, c), (x, y, 1 - c)
        chips = _other_chips(x, y)

        def slot(px, py, pc):
            return all_ref.at[4 * px + 2 * py + pc]

        local = pltpu.make_async_copy(x_ref, slot(*me), local_sem)
        local.start()
        first = [_remote(x_ref, slot(*me), send_sems, recv_sems, 0, sib)]
        first += [_remote(x_ref, slot(*me), send_sems, recv_sems, 1 + k, (cx, cy, c)) for k, (cx, cy) in enumerate(chips)]
        for cp in first:
            cp.start()
        passed = []
        for k, (cx, cy) in enumerate(chips):
            blk = slot(cx, cy, c)
            _remote(blk, blk, send_sems, recv_sems, 1 + k, me).wait_recv()
            fw = _remote(blk, blk, send_sems, recv_sems, 4 + k, sib)
            fw.start()
            passed.append(fw)
        _remote(slot(*sib), slot(*sib), send_sems, recv_sems, 0, me).wait_recv()
        for k, (cx, cy) in enumerate(chips):
            blk = slot(cx, cy, 1 - c)
            _remote(blk, blk, send_sems, recv_sems, 4 + k, me).wait_recv()
        for cp in first + passed:
            cp.wait_send()
        local.wait()
        s = all_ref[0]
        for d in range(1, N_DEV):
            s = s + all_ref[d]
        sum_ref[...] = s

    vm = pl.BlockSpec(memory_space=pltpu.VMEM)
    return pl.pallas_call(
        body, name=name, in_specs=[vm], out_specs=[vm, vm],
        out_shape=[jax.ShapeDtypeStruct((R, Wd), F32), jax.ShapeDtypeStruct((N_DEV, R, Wd), F32)],
        scratch_shapes=[pltpu.SemaphoreType.DMA((7,)), pltpu.SemaphoreType.DMA((7,)), pltpu.SemaphoreType.DMA],
    )(v)[0]


def _adamw(w, g, m, v, name):
    L, R, Wd = w.shape
    tr = _row_tile(R, RS_ROWS)
    c1 = 1.0 - ADAM_B1 ** ADAM_STEP
    c2 = 1.0 - ADAM_B2 ** ADAM_STEP

    def body(w_ref, g_ref, m_ref, v_ref, d_ref, nm_ref, nv_ref):
        gv = g_ref[...]
        nm = ADAM_B1 * m_ref[...] + (1.0 - ADAM_B1) * gv
        nv = ADAM_B2 * v_ref[...] + (1.0 - ADAM_B2) * (gv * gv)
        d_ref[...] = -ADAM_LR * ((nm / c1) / (jnp.sqrt(nv / c2) + ADAM_EPS) + ADAM_WD * w_ref[...])
        nm_ref[...] = nm
        nv_ref[...] = nv

    row = pl.BlockSpec((None, tr, Wd), lambda l, i: (l, i, 0))
    shp = jax.ShapeDtypeStruct((L, R, Wd), F32)
    return pl.pallas_call(
        body, name=name, grid=(L, R // tr), in_specs=[row] * 4, out_specs=[row] * 3, out_shape=[shp] * 3,
        compiler_params=_cp(("parallel", "parallel")),
    )(w, g, m, v)


def _adamw_halves(w, m, v, mine, theirs, ids, name):
    L, R, Wd = w.shape
    hr = R // 2
    tr = _row_tile(hr, RS_ROWS)
    c1 = 1.0 - ADAM_B1 ** ADAM_STEP
    c2 = 1.0 - ADAM_B2 ** ADAM_STEP

    def body(ids_ref, w_ref, m_ref, v_ref, *refs):
        g_refs, (g_ref, d_ref, nm_ref, nv_ref) = refs[:2 * L], refs[2 * L:]
        layer, half = pl.program_id(0), pl.program_id(1)
        own = half == ids_ref[0]
        gv = jnp.where(own, g_refs[0][...], g_refs[L][...])
        for l in range(1, L):
            gv = jnp.where(layer == l, jnp.where(own, g_refs[l][...], g_refs[L + l][...]), gv)
        nm = ADAM_B1 * m_ref[...] + (1.0 - ADAM_B1) * gv
        nv = ADAM_B2 * v_ref[...] + (1.0 - ADAM_B2) * (gv * gv)
        g_ref[...] = gv
        d_ref[...] = -ADAM_LR * ((nm / c1) / (jnp.sqrt(nv / c2) + ADAM_EPS) + ADAM_WD * w_ref[...])
        nm_ref[...] = nm
        nv_ref[...] = nv

    blk = pl.BlockSpec((None, None, tr, Wd), lambda l, h, i, ids: (l, h, i, 0))
    g_blk = pl.BlockSpec((tr, Wd), lambda l, h, i, ids: (i, 0))
    shp = jax.ShapeDtypeStruct((L, 2, hr, Wd), F32)
    outs = pl.pallas_call(
        body, name=name,
        grid_spec=pltpu.PrefetchScalarGridSpec(
            num_scalar_prefetch=1, grid=(L, 2, hr // tr),
            in_specs=[blk] * 3 + [g_blk] * (2 * L), out_specs=[blk] * 4),
        out_shape=[shp] * 4,
        compiler_params=_cp(("parallel", "parallel", "parallel")),
    )(ids, *[a.reshape(L, 2, hr, Wd) for a in (w, m, v)], *mine, *theirs)
    return tuple(o.reshape(L, R, Wd) for o in outs)


def _two_halves(a):
    cols = a.shape[-1]
    return a.reshape(2, _size(a.shape) // (2 * cols), cols)


FIRST_NEEDED = ("w_in_e", "w_out_e")
LATER_NEEDED = tuple(n for n in BIG if n not in FIRST_NEEDED)


def _gather_weights(P):
    chip = 2 * lax.axis_index("x") + lax.axis_index("y")

    def with_own(landed, own):
        return lax.dynamic_update_slice_in_dim(landed, own[None], chip, axis=0)

    mine = {n: _two_halves(P[n].astype(BF16)) for n in BIG}
    first = _chip_allgather_many([mine[n] for n in FIRST_NEEDED], "ag_first")
    gathered = {n: with_own(g, mine[n]) for n, g in zip(FIRST_NEEDED, first)}
    send_sems, recv_sems, blocks, lands, token = _chip_allgather_start([mine[n] for n in LATER_NEEDED], "ag_start")

    def later(after):
        landed = _chip_allgather_wait(send_sems, recv_sems, blocks, lands, after, "ag_wait")
        g = {n: with_own(a, mine[n]) for n, a in zip(LATER_NEEDED, _chip_allgather_forward(landed, "ag_forward"))}
        w_in_o = g["w_in_o"].reshape(N_CHIPS, D_MODEL, ODD_IN // N_CHIPS)
        return {
            "w_out_o": g["w_out_o"].reshape(D_MODEL, D_MODEL),
            "w_in_o": jnp.pad(jnp.concatenate([w_in_o[j] for j in range(N_CHIPS)], axis=1),
                              ((0, 0), (0, ODD_IN_PAD - ODD_IN))),
            "w_up": g["w_up"],
            "w_down": g["w_down"].transpose(1, 0, 2, 3).reshape(2, FFN_DIM, D_MODEL),
            "w_ple_gate": g["w_ple_gate"].transpose(1, 0, 2, 3).reshape(2, D_MODEL, D_MODEL),
            "w_ple": g["w_ple"].transpose(1, 2, 0, 3).reshape(2, PLE_DIM, D_MODEL),
        }

    small_shapes = [P[n].shape for n, _ in SHARDED_SMALL]
    small = _chip_allgather(_pack([P[n] for n, _ in SHARDED_SMALL], SMALL_W, SMALL_ROWS), "ag_small")
    parts = [_unpack(small[j], small_shapes) for j in range(N_CHIPS)]
    full = {n: jnp.concatenate([parts[j][i] for j in range(N_CHIPS)], axis=ax)
            for i, (n, ax) in enumerate(SHARDED_SMALL)}
    W = {n: P[n] for n in REPLICATED}
    W["pool_w"] = P["pool_w"][0]
    W["final_norm"] = P["final_norm"].reshape(1, D_MODEL)
    W["mix_norm_o"] = full["mix_norm_o"]
    W["conv_qkv_o"] = full["conv_qkv_o"][0]
    W["ffn_conv"] = full["ffn_conv"]
    W["w_in_e"] = gathered["w_in_e"].reshape(N_CHIPS, D_MODEL, IN_E_SHARD)
    W["w_out_e"] = gathered["w_out_e"].reshape(D_MODEL, D_MODEL)
    return W, token, later


def _chip_major_w_in_o(g):
    shard = ODD_IN // N_CHIPS
    return jnp.stack([g[:, j * shard:(j + 1) * shard] for j in range(N_CHIPS)])


def _reduce_begin(grads, ids, tag, travel_later):
    keys = list(grads)
    pieces = [g.reshape(N_CHIPS, 2, g.shape[1] // 2, g.shape[2]) for g in grads.values()]
    from_sibling = _sibling_swap_many(pieces, f"rs_sibling_swap_{tag}")
    sums = [_chip_sums_bf16(g, a, ids, f"rs_chip_sums_{tag}{i}") for i, (g, a) in enumerate(zip(pieces, from_sibling))]
    state = dict(keys=keys, pieces=pieces, from_sibling=from_sibling, ids=ids, tag=tag, token=None)
    if travel_later:
        state["flight"] = _chip_scatter_start(sums, f"rs_scatter_start_{tag}")
        state["token"] = state["flight"][-1]
    else:
        state["from_chips"] = _chip_scatter_many(sums, f"rs_chip_scatter_{tag}")
    return state


def _reduce_end(state, after=None):
    tag, ids = state["tag"], state["ids"]
    if "flight" in state:
        send_sems, recv_sems, sums, lands, _ = state["flight"]
        from_chips = _chip_scatter_wait(send_sems, recv_sems, sums, lands, after, f"rs_scatter_wait_{tag}")
    else:
        from_chips = state["from_chips"]
    halves = [_total_half(g, a, b, ids, f"rs_total_{tag}{i}")
              for i, (g, a, b) in enumerate(zip(state["pieces"], state["from_sibling"], from_chips))]
    theirs = _sibling_send_many(halves, f"rs_sibling_send_{tag}")
    return {k: (h, t) for k, h, t in zip(state["keys"], halves, theirs)}


def kernel(x, p, mix_norm_e, w_in_e, pool_w, pool_scale, w_out_e, mix_norm_o, w_in_o, conv_qkv_o, a_log_o, dt_bias_o, gdn_norm_o, w_out_o, ffn_norm, w_up, ffn_conv, w_down, ple_norm, w_ple_gate, w_ple, final_norm, loss_target, m_mix_norm_e, m_w_in_e, m_pool_w, m_pool_scale, m_w_out_e, m_mix_norm_o, m_w_in_o, m_conv_qkv_o, m_a_log_o, m_dt_bias_o, m_gdn_norm_o, m_w_out_o, m_ffn_norm, m_w_up, m_ffn_conv, m_w_down, m_ple_norm, m_w_ple_gate, m_w_ple, m_final_norm, v_mix_norm_e, v_w_in_e, v_pool_w, v_pool_scale, v_w_out_e, v_mix_norm_o, v_w_in_o, v_conv_qkv_o, v_a_log_o, v_dt_bias_o, v_gdn_norm_o, v_w_out_o, v_ffn_norm, v_w_up, v_ffn_conv, v_w_down, v_ple_norm, v_w_ple_gate, v_w_ple, v_final_norm):
    args = locals()
    P = {n: args[n] for n in WEIGHT_ORDER}
    M = {n: args["m_" + n] for n in WEIGHT_ORDER}
    V = {n: args["v_" + n] for n in WEIGHT_ORDER}

    W, token, later_weights = _gather_weights(P)
    T = x.shape[1]
    ids = _where_ids()
    second = {}

    def on_second_layer_grads(grads):
        second["state"] = _reduce_begin(grads, ids, "second", travel_later=True)
        return second["state"]["token"]

    sq, grad_x, G = _local_step(x.reshape(T, D_MODEL), p.reshape(2, T, PLE_DIM), loss_target.reshape(T, D_MODEL), W,
                                token, later_weights, on_second_layer_grads)
    first = _reduce_begin({("w_in_e", 0): G["w_in_e"], ("w_out_e", 0): G["w_out_e"], ("w_up", 0): G["w_up"][0],
                           ("w_down", 0): G["w_down"][0], ("w_ple_gate", 0): G["w_ple_gate"][0],
                           ("w_ple", 0): G["w_ple"][0]}, ids, "first", travel_later=False)
    reduced = {**_reduce_end(second["state"], after=grad_x), **_reduce_end(first)}
    out = {}
    for n in BIG:
        halves = [reduced[(n, l)] for l in range(P[n].shape[0])]
        out[n] = _adamw_halves(P[n], M[n], V[n], [h[0] for h in halves], [h[1] for h in halves], ids, f"adamw_{n}")

    small_full = {n: G[n] for n in REPLICATED}
    small_full["pool_w"] = G["pool_w"][None]
    small_full["final_norm"] = G["final_norm"].reshape(D_MODEL)
    small_full["mix_norm_o"] = G["mix_norm_o"]
    small_full["conv_qkv_o"] = G["conv_qkv_o"][None]
    small_full["ffn_conv"] = G["ffn_conv"]
    small_names = REPLICATED + tuple(n for n, _ in SHARDED_SMALL)
    summed = _small_allreduce(_pack([small_full[n] for n in small_names] + [sq.reshape(1)], SMALL_W, 8), "ar_small")
    *g_list, sq_total = _unpack(summed, [small_full[n].shape for n in small_names] + [(1,)])
    g_small = dict(zip(small_names, g_list))
    chip = 2 * lax.axis_index("x") + lax.axis_index("y")
    for n, ax in SHARDED_SMALL:
        width = P[n].shape[ax]
        g_small[n] = lax.dynamic_slice_in_dim(g_small[n], chip * width, width, axis=ax)

    def pack_small(D):
        return _pack([D[n] for n in small_names], SMALL_W, RS_ROWS)[None]

    g_pack = pack_small(g_small)
    upd = _adamw(pack_small(P), g_pack, pack_small(M), pack_small(V), "adamw_small")
    shapes = [P[n].shape for n in small_names]
    for n, *vals in zip(small_names, *[_unpack(a[0], shapes) for a in (g_pack,) + tuple(upd)]):
        out[n] = tuple(vals)

    loss = (0.5 / D_MODEL) * sq_total[0]
    return (loss, grad_x[None]) + tuple(out[n][i] for i in range(4) for n in WEIGHT_ORDER)
```

```python
import functools

import jax
import jax.numpy as jnp
from jax import lax
from jax.experimental import pallas as pl
from jax.experimental.pallas import tpu as pltpu

F32 = jnp.float32
BF16 = jnp.bfloat16

D_MODEL = 1024
PLE_DIM = 256
POOL_WIDTH = 512
POOL_WINDOWS = (2, 4, 8, 16)
POOL_GROUP_DIM = 128
SB_HEADS = 8
SB_HEAD_DIM = 64
GDN_HEADS = 8
GDN_HEAD_DIM = 128
GDN_CONV = 4
GDN_CHUNK = 64
FFN_DIM = 2816
FFN_CONV = 3
EPS = 1e-6
ODD_IN = 4 * D_MODEL + 2 * GDN_HEADS
ODD_IN_PAD = 33 * 128
ADAM_LR, ADAM_B1, ADAM_B2, ADAM_EPS, ADAM_WD, ADAM_STEP = 0.001, 0.9, 0.999, 1e-08, 0.01, 10

LANE = 128
VMEM_LIMIT = 56 * 1024 * 1024

N_CHIPS = 4
N_DEV = 8


def _cp(sem=None):
    return pltpu.CompilerParams(dimension_semantics=sem, vmem_limit_bytes=VMEM_LIMIT)


def _tile(n, pref):
    if n <= pref:
        return n
    best = None
    for t in range(LANE, pref + 1, LANE):
        if n % t == 0:
            best = t
    assert best is not None, (n, pref)
    return best


_DIMS = {"nn": (((1,), (0,)), ((), ())), "nt": (((1,), (1,)), ((), ())), "tn": (((0,), (0,)), ((), ()))}
_BDIMS = {"nn": (((2,), (1,)), ((0,), (0,))), "nt": (((2,), (2,)), ((0,), (0,))), "tn": (((1,), (1,)), ((0,), (0,)))}


def _dims(mode, ndim):
    return (_BDIMS if ndim == 3 else _DIMS)[mode]


def _dot(a, b, mode="nn"):
    return lax.dot_general(a.astype(BF16), b.astype(BF16), _dims(mode, a.ndim), preferred_element_type=F32)


def _bdot(a, b, mode="nn"):
    return lax.dot_general(a.astype(BF16), b.astype(BF16), _BDIMS[mode], preferred_element_type=F32)


def _split2(x):
    hi = x.astype(BF16)
    lo = (x - hi.astype(F32)).astype(BF16)
    return hi, lo


def _split3(x):
    hi = x.astype(BF16)
    r = x - hi.astype(F32)
    mid = r.astype(BF16)
    lo = (r - mid.astype(F32)).astype(BF16)
    return hi, mid, lo


def _dot_x01(x, m01, mode="nn"):
    hi, lo = _split2(x)
    return (lax.dot_general(hi, m01, _DIMS[mode], preferred_element_type=F32)
            + lax.dot_general(lo, m01, _DIMS[mode], preferred_element_type=F32))


def _dot3_raw(a, b, mode):
    ah, al = _split2(a)
    bh, bl = _split2(b)
    d = _dims(mode, a.ndim)
    return (lax.dot_general(ah, bh, d, preferred_element_type=F32)
            + lax.dot_general(ah, bl, d, preferred_element_type=F32)
            + lax.dot_general(al, bh, d, preferred_element_type=F32))


@jax.custom_vjp
def _dot3(a, b):
    return _dot3_raw(a, b, "nn")


def _dot3_fwd(a, b):
    return _dot3_raw(a, b, "nn"), (a, b)


def _dot3_bwd(res, g):
    a, b = res
    return _dot(g, b, "nt"), _dot(a, g, "tn")


_dot3.defvjp(_dot3_fwd, _dot3_bwd)


@jax.custom_vjp
def _dot1_nt(a, b):
    return _dot(a, b, "nt")


def _dot1_nt_fwd(a, b):
    return _dot(a, b, "nt"), (a, b)


def _dot1_nt_bwd(res, g):
    a, b = res
    return _dot(g, b, "nn"), _dot(g, a, "tn")


_dot1_nt.defvjp(_dot1_nt_fwd, _dot1_nt_bwd)


def _m01_left_raw(m, x):
    d = _dims("nn", x.ndim)
    if x.ndim == 3:
        m = jnp.broadcast_to(m, (x.shape[0],) + m.shape)
    p0, p1, p2 = _split3(x)
    return (lax.dot_general(m, p0, d, preferred_element_type=F32)
            + lax.dot_general(m, p1, d, preferred_element_type=F32)
            + lax.dot_general(m, p2, d, preferred_element_type=F32))


@jax.custom_vjp
def _m01_left(m, mt, x):
    return _m01_left_raw(m, x)


def _m01_left_fwd(m, mt, x):
    return _m01_left_raw(m, x), (m, mt)


def _m01_left_bwd(res, g):
    m, mt = res
    return jnp.zeros_like(m), jnp.zeros_like(mt), _m01_left_raw(mt, g)


_m01_left.defvjp(_m01_left_fwd, _m01_left_bwd)


def _softplus(x):
    return jnp.maximum(x, 0.0) + jnp.log(1.0 + jnp.exp(-jnp.abs(x)))


def _sigmoid(x):
    return 1.0 / (1.0 + jnp.exp(-x))


def _silu(x):
    return x * _sigmoid(x)


def _dsilu(x):
    s = _sigmoid(x)
    return s * (1.0 + x * (1.0 - s))


def _cols_of(n_blocks_per_part, *fixed):
    return lambda r, c: (c // n_blocks_per_part,) + fixed + (r, c % n_blocks_per_part)


def _rows_of(n_blocks_per_part, *fixed):
    return lambda r, c: (r // n_blocks_per_part,) + fixed + (r % n_blocks_per_part, c)


def _layer_of(layer):
    return lambda r, c: (layer, r, c)


def _mm(a, b, mode, name, out_dtype=F32, res=None, tm=1024, tn=512, tk=1024,
        dims=None, a_view=None, b_view=None, o_view=None, out_shape=None):
    if dims is None:
        if mode == "nn":
            (M, K), (K2, N) = a.shape, b.shape
        elif mode == "nt":
            (M, K), (N, K2) = a.shape, b.shape
        else:
            (K, M), (K2, N) = a.shape, b.shape
        assert K == K2, (name, a.shape, b.shape)
    else:
        M, N, K = dims
    tm, tn, tk = _tile(M, tm), _tile(N, tn), _tile(K, tk)
    nk = K // tk

    def spec(arr, blk, view, rc):
        view = view or (lambda r, c: (r, c))
        return pl.BlockSpec((None,) * (arr.ndim - 2) + blk, lambda i, j, k: view(*rc(i, j, k)))

    if mode == "tn":
        a_spec = spec(a, (tk, tm), a_view, lambda i, j, k: (k, i))
    else:
        a_spec = spec(a, (tm, tk), a_view, lambda i, j, k: (i, k))
    if mode == "nt":
        b_spec = spec(b, (tn, tk), b_view, lambda i, j, k: (j, k))
    else:
        b_spec = spec(b, (tk, tn), b_view, lambda i, j, k: (k, j))
    out_shape = out_shape or (M, N)
    o_spec = pl.BlockSpec((None,) * (len(out_shape) - 2) + (tm, tn),
                          lambda i, j, k: (o_view or (lambda r, c: (r, c)))(i, j))
    has_res = res is not None
    assert not (has_res and o_view), name

    def body(*refs):
        a_ref, b_ref = refs[:2]
        r_ref = refs[2] if has_res else None
        o_ref = refs[3] if has_res else refs[2]

        def finish(r):
            if has_res:
                r = r + r_ref[...]
            o_ref[...] = r.astype(out_dtype)

        if nk == 1:
            finish(_dot(a_ref[...], b_ref[...], mode))
            return
        acc = refs[-1]
        k = pl.program_id(2)

        @pl.when(k == 0)
        def _():
            acc[...] = jnp.zeros_like(acc)

        acc[...] += _dot(a_ref[...], b_ref[...], mode)

        @pl.when(k == nk - 1)
        def _():
            finish(acc[...])

    ins = [a, b] + ([res] if has_res else [])
    in_specs = [a_spec, b_spec] + ([o_spec] if has_res else [])
    return pl.pallas_call(
        body, name=name, grid=(M // tm, N // tn, nk),
        in_specs=in_specs, out_specs=o_spec,
        out_shape=jax.ShapeDtypeStruct(out_shape, out_dtype),
        scratch_shapes=[pltpu.VMEM((tm, tn), F32)] if nk > 1 else [],
        compiler_params=_cp(("parallel", "parallel", "arbitrary")),
    )(*ins)


def _rms_fwd(x, gain, name, after=None):
    T, D = x.shape
    tt = _tile(T, 512)

    def body(x_ref, g_ref, *rest):
        o_ref = rest[-1]
        xv = x_ref[...]
        r = lax.rsqrt(jnp.mean(xv * xv, axis=-1, keepdims=True) + EPS)
        o_ref[...] = (xv * r * g_ref[...]).astype(BF16)

    ordered = [] if after is None else [after]
    return pl.pallas_call(
        body, name=name, grid=(T // tt,),
        in_specs=[pl.BlockSpec((tt, D), lambda i: (i, 0)), pl.BlockSpec((1, D), lambda i: (0, 0))]
        + [pl.BlockSpec((8, LANE), lambda i: (0, 0)) for _ in ordered],
        out_specs=pl.BlockSpec((tt, D), lambda i: (i, 0)),
        out_shape=jax.ShapeDtypeStruct((T, D), BF16),
        compiler_params=_cp(("parallel",)),
    )(x, gain, *ordered)


def _rms_bwd(x, gain, dh, dres, name):
    T, D = x.shape
    tt = _tile(T, 512)

    def body(x_ref, g_ref, dh_ref, dr_ref, dx_ref, dg_ref):
        i = pl.program_id(0)
        xv = x_ref[...]
        dy = dh_ref[...].astype(F32)
        r = lax.rsqrt(jnp.mean(xv * xv, axis=-1, keepdims=True) + EPS)
        xn = xv * r
        gdy = dy * g_ref[...]
        dx = r * (gdy - xn * jnp.mean(gdy * xn, axis=-1, keepdims=True))
        dx_ref[...] = dr_ref[...] + dx

        @pl.when(i == 0)
        def _():
            dg_ref[...] = jnp.zeros_like(dg_ref)

        dg_ref[...] += jnp.sum(dy * xn, axis=0, keepdims=True)

    row = pl.BlockSpec((tt, D), lambda i: (i, 0))
    vec = pl.BlockSpec((1, D), lambda i: (0, 0))
    return pl.pallas_call(
        body, name=name, grid=(T // tt,),
        in_specs=[row, vec, row, row], out_specs=[row, vec],
        out_shape=[jax.ShapeDtypeStruct((T, D), F32), jax.ShapeDtypeStruct((1, D), F32)],
        compiler_params=_cp(("arbitrary",)),
    )(x, gain, dh, dres)


def _final_loss(x, gain, target, name):
    T, D = x.shape
    tt = _tile(T, 512)

    def body(x_ref, g_ref, t_ref, l_ref, dx_ref, dg_ref):
        i = pl.program_id(0)
        xv = x_ref[...]
        r = lax.rsqrt(jnp.mean(xv * xv, axis=-1, keepdims=True) + EPS)
        xn = xv * r
        err = xn * g_ref[...] - t_ref[...]
        dy = err * (1.0 / D)
        gdy = dy * g_ref[...]
        dx_ref[...] = r * (gdy - xn * jnp.mean(gdy * xn, axis=-1, keepdims=True))

        @pl.when(i == 0)
        def _():
            dg_ref[...] = jnp.zeros_like(dg_ref)
            l_ref[...] = jnp.zeros_like(l_ref)

        dg_ref[...] += jnp.sum(dy * xn, axis=0, keepdims=True)
        l_ref[...] += jnp.sum(jnp.sum(err * err, axis=1, keepdims=True), axis=0, keepdims=True)

    row = pl.BlockSpec((tt, D), lambda i: (i, 0))
    vec = pl.BlockSpec((1, D), lambda i: (0, 0))
    return pl.pallas_call(
        body, name=name, grid=(T // tt,),
        in_specs=[row, vec, row],
        out_specs=[pl.BlockSpec((8, LANE), lambda i: (0, 0)), row, vec],
        out_shape=[jax.ShapeDtypeStruct((8, LANE), F32), jax.ShapeDtypeStruct((T, D), F32),
                   jax.ShapeDtypeStruct((1, D), F32)],
        compiler_params=_cp(("arbitrary",)),
    )(x, gain, target)


def _shift_down(x, i, t_idx):
    if i == 0:
        return x
    return jnp.where(t_idx >= i, pltpu.roll(x, i, 0), 0.0)


def _shift_up(x, i, t_idx):
    if i == 0:
        return x
    n = x.shape[0]
    return jnp.where(t_idx < n - i, pltpu.roll(x, n - i, 0), 0.0)


def _pool_select(g, vals):
    out = vals[-1]
    for gi in range(len(vals) - 2, -1, -1):
        out = jnp.where(g == gi, vals[gi], out)
    return out


def _pool_y(u, g, t_idx):
    s1 = u + _shift_down(u, 1, t_idx)
    s2 = s1 + _shift_down(s1, 2, t_idx)
    s3 = s2 + _shift_down(s2, 4, t_idx)
    s4 = s3 + _shift_down(s3, 8, t_idx)
    ws = _pool_select(g, [s1, s2, s3, s4])
    win = _pool_select(g, [jnp.float32(w) for w in POOL_WINDOWS])
    cnt = jnp.minimum(t_idx.astype(F32) + 1.0, win)
    return ws / cnt - u, cnt


def _pool_fwd(proj, pool_w, pool_scale):
    T = proj.shape[0]
    G, C = len(POOL_WINDOWS), POOL_GROUP_DIM

    def body(u_ref, w_ref, s_ref, o_ref):
        g = pl.program_id(0)
        t_idx = lax.broadcasted_iota(jnp.int32, (T, C), 0)
        y, _ = _pool_y(u_ref[...], g, t_idx)
        o_ref[...] = _dot(y, w_ref[0]) * s_ref[...]

    return pl.pallas_call(
        body, name="pool_fwd", grid=(G,),
        in_specs=[pl.BlockSpec((T, C), lambda g: (0, g)), pl.BlockSpec((1, C, C), lambda g: (g, 0, 0)),
                  pl.BlockSpec((1, C), lambda g: (0, g))],
        out_specs=pl.BlockSpec((T, C), lambda g: (0, g)),
        out_shape=jax.ShapeDtypeStruct((T, G * C), F32),
        compiler_params=_cp(("parallel",)),
    )(proj, pool_w, pool_scale)


def _pool_bwd(proj, pool_w, pool_scale, dmix):
    T = proj.shape[0]
    G, C = len(POOL_WINDOWS), POOL_GROUP_DIM

    def body(u_ref, w_ref, s_ref, do_ref, du_ref, dw_ref, ds_ref):
        g = pl.program_id(0)
        t_idx = lax.broadcasted_iota(jnp.int32, (T, C), 0)
        y, cnt = _pool_y(u_ref[...], g, t_idx)
        w = w_ref[0]
        dout = do_ref[...]
        ds_ref[...] = jnp.sum(dout * _dot(y, w), axis=0, keepdims=True)
        dy2 = dout * s_ref[...]
        dw_ref[0] = _dot(y, dy2, "tn")
        dy = _dot(dy2, w, "nt")
        dz = dy / cnt
        r1 = dz + _shift_up(dz, 1, t_idx)
        r2 = r1 + _shift_up(r1, 2, t_idx)
        r3 = r2 + _shift_up(r2, 4, t_idx)
        r4 = r3 + _shift_up(r3, 8, t_idx)
        du_ref[...] = _pool_select(g, [r1, r2, r3, r4]) - dy

    col = pl.BlockSpec((T, C), lambda g: (0, g))
    return pl.pallas_call(
        body, name="pool_bwd", grid=(G,),
        in_specs=[col, pl.BlockSpec((1, C, C), lambda g: (g, 0, 0)), pl.BlockSpec((1, C), lambda g: (0, g)), col],
        out_specs=[col, pl.BlockSpec((1, C, C), lambda g: (g, 0, 0)), pl.BlockSpec((1, C), lambda g: (0, g))],
        out_shape=[jax.ShapeDtypeStruct((T, G * C), F32), jax.ShapeDtypeStruct((G, C, C), F32),
                   jax.ShapeDtypeStruct((1, G * C), F32)],
        compiler_params=_cp(("parallel",)),
    )(proj, pool_w, pool_scale, dmix)


SB_SCALE = SB_HEAD_DIM ** -0.5
SB_PASS_SIZES = (2, 1)


def _sb_tile_logits(qb, kblk, valid):
    z = _dot(qb, kblk, "nt")
    sp = _softplus(z)
    l1m = -sp
    if valid is not None:
        l1m = jnp.where(valid, l1m, 0.0)
    return z, sp, l1m


SB_PAIR = LANE // SB_HEAD_DIM
SB_Q0 = POOL_WIDTH // LANE
SB_NB = SB_HEADS // SB_PAIR


def _sb_head_masks():
    lane = lax.broadcasted_iota(jnp.int32, (1, LANE), 1)
    return [(lane // SB_HEAD_DIM == h).astype(F32) for h in range(SB_PAIR)]


def _sb_fwd(proj):
    T = proj.shape[0]
    B = _tile(T, 256)
    nq = T // B

    def body(q_ref, k_ref, v_ref, o_ref, l_ref, k_bf, v_bf):
        qi = pl.program_id(1)

        @pl.when(qi == 0)
        def _():
            k_bf[...] = k_ref[...].astype(BF16)
            v_bf[...] = v_ref[...].astype(BF16)

        masks = _sb_head_masks()
        q_all = q_ref[...]
        qbs = [(q_all * (m * SB_SCALE)).astype(BF16) for m in masks]
        row = lax.broadcasted_iota(jnp.int32, (B, B), 0)
        col = lax.broadcasted_iota(jnp.int32, (B, B), 1)
        later = (row > col).astype(BF16)

        def tiles(kbs, state, valid):
            ksl = [pl.ds(pl.multiple_of(kb * B, B), B) for kb in kbs]
            kblks = [k_bf[ks, :] for ks in ksl]
            logits = [[_sb_tile_logits(qb, kblk, valid) for kblk in kblks] for qb in qbs]
            within = [[_dot_x01(l1m, later) for _, _, l1m in lg] for lg in logits]
            sums = [[jnp.sum(l1m, axis=1, keepdims=True) for _, _, l1m in lg] for lg in logits]
            out = []
            for h, (carry, acc) in enumerate(state):
                for (z, sp, _), rc, s, ks in zip(logits[h], within[h], sums[h], ksl):
                    a = jnp.exp(z - sp + rc + carry)
                    if valid is not None:
                        a = jnp.where(valid, a, 0.0)
                    acc = acc + _dot(a, v_bf[ks, :])
                    carry = carry + s
                out.append((carry, acc))
            return tuple(out)

        state = tiles([qi], ((jnp.zeros((B, 1), F32), jnp.zeros((B, LANE), F32)),) * SB_PAIR, col < row)
        left = qi
        for size in SB_PASS_SIZES:
            n_pass = left // size
            state = lax.fori_loop(
                0, n_pass, lambda i, c, left=left, size=size: tiles([left - 1 - size * i - u for u in range(size)],
                                                                     c, None), state)
            left = left - n_pass * size
        o_ref[...] = sum(acc * m for (_, acc), m in zip(state, masks))
        for h, (carry, _) in enumerate(state):
            l_ref[h] = carry

    return pl.pallas_call(
        body, name="sb_fwd", grid=(SB_NB, nq),
        in_specs=[pl.BlockSpec((B, LANE), lambda hp, i: (i, SB_Q0 + hp)),
                  pl.BlockSpec((T, LANE), lambda hp, i: (0, SB_Q0 + SB_NB + hp)),
                  pl.BlockSpec((T, LANE), lambda hp, i: (0, SB_Q0 + 2 * SB_NB + hp))],
        out_specs=[pl.BlockSpec((B, LANE), lambda hp, i: (i, hp)),
                   pl.BlockSpec((SB_PAIR, B, 1), lambda hp, i: (hp, i, 0))],
        out_shape=[jax.ShapeDtypeStruct((T, SB_HEADS * SB_HEAD_DIM), F32), jax.ShapeDtypeStruct((SB_HEADS, T, 1), F32)],
        scratch_shapes=[pltpu.VMEM((T, LANE), BF16), pltpu.VMEM((T, LANE), BF16)],
        compiler_params=_cp(("parallel", "arbitrary")),
    )(proj, proj, proj)


def _sb_bwd(proj, dmix, ltot, after=None):
    T = proj.shape[0]
    B = _tile(T, 256)
    nq = T // B
    ordered = [] if after is None else [after]

    def body(q_ref, k_ref, v_ref, do_ref, l_ref, *rest):
        dq_ref, dk_ref, dv_ref, k_bf, v_bf = rest[len(ordered):]
        qi = pl.program_id(1)

        @pl.when(qi == 0)
        def _():
            k_bf[...] = k_ref[...].astype(BF16)
            v_bf[...] = v_ref[...].astype(BF16)
            dk_ref[...] = jnp.zeros_like(dk_ref)
            dv_ref[...] = jnp.zeros_like(dv_ref)

        masks = _sb_head_masks()
        q_all, do_all = q_ref[...], do_ref[...]
        qbs = [(q_all * (m * SB_SCALE)).astype(BF16) for m in masks]
        dobs = [(do_all * m).astype(BF16) for m in masks]
        ltots = [l_ref[h] for h in range(SB_PAIR)]
        row = lax.broadcasted_iota(jnp.int32, (B, B), 0)
        col = lax.broadcasted_iota(jnp.int32, (B, B), 1)
        upto = (row <= col).astype(BF16)
        before = (row < col).astype(BF16)

        def tiles(kbs, state, valid):
            ksl = [pl.ds(pl.multiple_of(kb * B, B), B) for kb in kbs]
            kblks = [k_bf[ks, :] for ks in ksl]
            vblks = [v_bf[ks, :] for ks in ksl]
            logits = [[_sb_tile_logits(qb, kblk, valid) for kblk in kblks] for qb in qbs]
            das = [[_dot(dob, vblk, "nt") for vblk in vblks] for dob in dobs]
            within = [[_dot_x01(l1m, upto) for _, _, l1m in lg] for lg in logits]
            avals, es, Ps = [], [], []
            for h, (P, _, _) in enumerate(state):
                a_h, e_h = [], []
                for (z, sp, l1m), pc, da in zip(logits[h], within[h], das[h]):
                    a = jnp.exp(z - sp + (ltots[h] - P - pc))
                    if valid is not None:
                        a = jnp.where(valid, a, 0.0)
                    a_h.append(a)
                    e_h.append(da * a)
                    P = P + jnp.sum(l1m, axis=1, keepdims=True)
                avals.append(a_h)
                es.append(e_h)
                Ps.append(P)
            e_within = [[_dot_x01(e, before) for e in e_h] for e_h in es]
            out = []
            for h, (_, E, dq) in enumerate(state):
                for (z, sp, _), e, ew, a, kblk, ks in zip(logits[h], es[h], e_within[h], avals[h], kblks, ksl):
                    dz = e * jnp.exp(-sp) - jnp.exp(z - sp) * (ew + E)
                    if valid is not None:
                        dz = jnp.where(valid, dz, 0.0)
                    dzb = dz.astype(BF16)
                    dq = dq + _dot(dzb, kblk)
                    dk_ref[ks, :] += _dot(dzb, qbs[h], "tn")
                    dv_ref[ks, :] += _dot(a, dobs[h], "tn")
                    E = E + jnp.sum(e, axis=1, keepdims=True)
                out.append((Ps[h], E, dq))
            return tuple(out)

        zeros1 = jnp.zeros((B, 1), F32)
        state = ((zeros1, zeros1, jnp.zeros((B, LANE), F32)),) * SB_PAIR
        done = 0
        for size in SB_PASS_SIZES:
            n_pass = (qi - done) // size
            state = lax.fori_loop(
                0, n_pass, lambda i, c, done=done, size=size: tiles([done + size * i + u for u in range(size)], c, None),
                state)
            done = done + n_pass * size
        state = tiles([qi], state, col < row)
        dq_ref[...] = sum(dq * (m * SB_SCALE) for (_, _, dq), m in zip(state, masks))

    qspec = pl.BlockSpec((B, LANE), lambda hp, i: (i, SB_Q0 + hp))
    wide = jax.ShapeDtypeStruct((T, SB_HEADS * SB_HEAD_DIM), F32)
    return pl.pallas_call(
        body, name="sb_bwd", grid=(SB_NB, nq),
        in_specs=[qspec,
                  pl.BlockSpec((T, LANE), lambda hp, i: (0, SB_Q0 + SB_NB + hp)),
                  pl.BlockSpec((T, LANE), lambda hp, i: (0, SB_Q0 + 2 * SB_NB + hp)),
                  qspec,
                  pl.BlockSpec((SB_PAIR, B, 1), lambda hp, i: (hp, i, 0))]
        + [pl.BlockSpec((8, LANE), lambda hp, i: (0, 0)) for _ in ordered],
        out_specs=[pl.BlockSpec((B, LANE), lambda hp, i: (i, hp)),
                   pl.BlockSpec((T, LANE), lambda hp, i: (0, hp)),
                   pl.BlockSpec((T, LANE), lambda hp, i: (0, hp))],
        out_shape=[wide, wide, wide],
        scratch_shapes=[pltpu.VMEM((T, LANE), BF16), pltpu.VMEM((T, LANE), BF16)],
        compiler_params=_cp(("parallel", "arbitrary")),
    )(proj, proj, proj, dmix, ltot, *ordered)


def _rows(w_ref, K):
    return [w_ref[i:i + 1, :] for i in range(K)]


def _conv(x, ws, t_idx):
    K = len(ws)
    y = ws[K - 1] * x
    for i in range(K - 1):
        y = y + ws[i] * _shift_down(x, K - 1 - i, t_idx)
    return y


def _conv_bwd(x, ws, dy, t_idx):
    K = len(ws)
    dx = ws[K - 1] * dy
    dws = []
    for i in range(K - 1):
        dx = dx + ws[i] * _shift_up(dy, K - 1 - i, t_idx)
        dws.append(jnp.sum(dy * _shift_down(x, K - 1 - i, t_idx), axis=0, keepdims=True))
    dws.append(jnp.sum(dy * x, axis=0, keepdims=True))
    return dx, dws


def _store_rows(ref, rows):
    for i, r in enumerate(rows):
        ref[i:i + 1, :] = r


def _ffn_act_fwd(up, conv_w, name):
    T = up.shape[0]
    F = FFN_DIM
    nb = F // LANE

    def body(g_ref, v_ref, wg_ref, wv_ref, o_ref):
        t_idx = lax.broadcasted_iota(jnp.int32, (T, LANE), 0)
        cg = _conv(g_ref[...].astype(F32), _rows(wg_ref, FFN_CONV), t_idx)
        cv = _conv(v_ref[...].astype(F32), _rows(wv_ref, FFN_CONV), t_idx)
        o_ref[...] = (_silu(cg) * cv).astype(BF16)

    return pl.pallas_call(
        body, name=name, grid=(nb,),
        in_specs=[pl.BlockSpec((T, LANE), lambda j: (0, j)), pl.BlockSpec((T, LANE), lambda j: (0, j + nb)),
                  pl.BlockSpec((FFN_CONV, LANE), lambda j: (0, j)),
                  pl.BlockSpec((FFN_CONV, LANE), lambda j: (0, j + nb))],
        out_specs=pl.BlockSpec((T, LANE), lambda j: (0, j)),
        out_shape=jax.ShapeDtypeStruct((T, F), BF16),
        compiler_params=_cp(("parallel",)),
    )(up, up, conv_w, conv_w)


def _ffn_act_bwd(up, conv_w, dact, name):
    T = up.shape[0]
    F = FFN_DIM
    nb = F // LANE

    def body(g_ref, v_ref, wg_ref, wv_ref, da_ref, dup_ref, dwg_ref, dwv_ref):
        t_idx = lax.broadcasted_iota(jnp.int32, (T, LANE), 0)
        xg, xv = g_ref[...].astype(F32), v_ref[...].astype(F32)
        wg, wv = _rows(wg_ref, FFN_CONV), _rows(wv_ref, FFN_CONV)
        cg = _conv(xg, wg, t_idx)
        cv = _conv(xv, wv, t_idx)
        da = da_ref[...].astype(F32)
        dxg, dwg = _conv_bwd(xg, wg, da * cv * _dsilu(cg), t_idx)
        dxv, dwv = _conv_bwd(xv, wv, da * _silu(cg), t_idx)
        dup_ref[0] = dxg.astype(BF16)
        dup_ref[1] = dxv.astype(BF16)
        _store_rows(dwg_ref, dwg)
        _store_rows(dwv_ref, dwv)

    col = pl.BlockSpec((T, LANE), lambda j: (0, j))
    wcol = pl.BlockSpec((FFN_CONV, LANE), lambda j: (0, j))
    return pl.pallas_call(
        body, name=name, grid=(nb,),
        in_specs=[col, pl.BlockSpec((T, LANE), lambda j: (0, j + nb)), wcol,
                  pl.BlockSpec((FFN_CONV, LANE), lambda j: (0, j + nb)), col],
        out_specs=[pl.BlockSpec((2, T, LANE), lambda j: (0, 0, j)), wcol, wcol],
        out_shape=[jax.ShapeDtypeStruct((2, T, F), BF16),
                   jax.ShapeDtypeStruct((FFN_CONV, F), F32), jax.ShapeDtypeStruct((FFN_CONV, F), F32)],
        compiler_params=_cp(("parallel",)),
    )(up, up, conv_w, conv_w, dact)


N_QK_BLOCKS = 2 * GDN_HEADS


def _gdn_pre_fwd(proj, conv_w):
    T = proj.shape[0]
    nb = 3 * GDN_HEADS

    def body(x_ref, w_ref, o_ref):
        j = pl.program_id(0)
        t_idx = lax.broadcasted_iota(jnp.int32, (T, LANE), 0)
        s = _silu(_conv(x_ref[...], _rows(w_ref, GDN_CONV), t_idx))
        rn = lax.rsqrt(jnp.sum(s * s, axis=-1, keepdims=True) + EPS)
        o_ref[...] = s * jnp.where(j < N_QK_BLOCKS, rn, 1.0)

    return pl.pallas_call(
        body, name="gdn_pre_fwd", grid=(nb,),
        in_specs=[pl.BlockSpec((T, LANE), lambda j: (0, j)), pl.BlockSpec((GDN_CONV, LANE), lambda j: (0, j))],
        out_specs=pl.BlockSpec((T, LANE), lambda j: (0, j)),
        out_shape=jax.ShapeDtypeStruct((T, nb * LANE), F32),
        compiler_params=_cp(("parallel",)),
    )(proj, conv_w)


def _gdn_pre_bwd(proj, conv_w, dout):
    T = proj.shape[0]
    nb = 3 * GDN_HEADS
    H = GDN_HEADS

    def body(x_ref, w_ref, do_ref, dx_ref, dw_ref):
        j = pl.program_id(0)
        t_idx = lax.broadcasted_iota(jnp.int32, (T, LANE), 0)
        x, w = x_ref[...], _rows(w_ref, GDN_CONV)
        c = _conv(x, w, t_idx)
        s = _silu(c)
        rn = lax.rsqrt(jnp.sum(s * s, axis=-1, keepdims=True) + EPS)
        do = do_ref[...]
        y = s * rn
        ds_normed = rn * (do - y * jnp.sum(do * y, axis=-1, keepdims=True))
        ds = jnp.where(j < N_QK_BLOCKS, ds_normed, do)
        dx, dw = _conv_bwd(x, w, ds * _dsilu(c), t_idx)
        dx_ref[...] = dx.astype(BF16)
        _store_rows(dw_ref, dw)

    col = pl.BlockSpec((T, LANE), lambda j: (0, j))
    wcol = pl.BlockSpec((GDN_CONV, LANE), lambda j: (0, j))
    return pl.pallas_call(
        body, name="gdn_pre_bwd", grid=(nb,),
        in_specs=[col, wcol, pl.BlockSpec((None, None, T, LANE), lambda j: (j // H, j % H, 0, 0))],
        out_specs=[col, wcol],
        out_shape=[jax.ShapeDtypeStruct((T, nb * LANE), BF16), jax.ShapeDtypeStruct((GDN_CONV, nb * LANE), F32)],
        compiler_params=_cp(("parallel",)),
    )(proj, conv_w, dout)


def _gdn_consts():
    C = GDN_CHUNK
    r = lax.broadcasted_iota(jnp.int32, (C, C), 0)
    c = lax.broadcasted_iota(jnp.int32, (C, C), 1)
    return dict(incl=r >= c, strict=r > c, eye=(r == c).astype(F32),
                low=(r >= c).astype(BF16), up=(r <= c).astype(BF16), ones=jnp.ones((C, C), BF16))


def _unit_lower_inverse_raw(a_mat, eye):
    inv = eye - a_mat
    pw = _dot3_raw(a_mat, a_mat, "nn")
    n_factors = a_mat.shape[-1].bit_length() - 2
    for f in range(n_factors):
        inv = inv + _dot3_raw(inv, pw, "nn")
        if f < n_factors - 1:
            pw = _dot3_raw(pw, pw, "nn")
    return inv


@jax.custom_vjp
def _unit_lower_inverse(a_mat, eye):
    return _unit_lower_inverse_raw(a_mat, eye)


def _unit_lower_inverse_fwd(a_mat, eye):
    inv = _unit_lower_inverse_raw(a_mat, eye)
    return inv, (inv, eye)


def _unit_lower_inverse_bwd(res, g):
    inv, eye = res
    return -_dot(_dot(inv, g, "tn"), inv, "nt"), jnp.zeros_like(eye)


_unit_lower_inverse.defvjp(_unit_lower_inverse_fwd, _unit_lower_inverse_bwd)


def _gdn_prep_chunk(q, k, v, b, a, alog, dtb, cs):
    n, C, dk = q.shape
    beta = _sigmoid(b)
    g = -jnp.exp(alog) * _softplus(a + dtb)
    g_sq = jnp.broadcast_to(g, (n, C, C))
    g_wide = jnp.broadcast_to(g, (n, C, dk))
    gc_i = _m01_left(cs["low"], cs["up"], g_sq)
    gc_j = _m01_left(cs["ones"], cs["ones"], g_sq * cs["up"].astype(F32))
    gc_wide = _m01_left(cs["low"], cs["up"], g_wide)
    gl_wide = _m01_left(cs["ones"], cs["ones"], g_wide)
    decay = jnp.where(cs["incl"], jnp.exp(jnp.where(cs["incl"], gc_i - gc_j, 0.0)), 0.0)
    egc = jnp.exp(gc_wide)
    qs = q * (dk ** -0.5)
    k_beta = k * beta
    a_mat = jnp.where(cs["strict"], _dot1_nt(k_beta, k) * decay, 0.0)
    inv = _unit_lower_inverse(a_mat, cs["eye"])
    u = _dot3(inv, v * beta)
    w = _dot3(inv, k_beta * egc)
    qk = _dot1_nt(qs, k) * decay
    q_dec = qs * egc
    k_dec = k * jnp.exp(gl_wide - gc_wide)
    g_last = jnp.exp(gl_wide)[:, 0:8, :]
    return qk, u, w, q_dec, k_dec, g_last


GDN_PREP_CHUNKS = 8
GDN_BA_BLOCK = 4 * D_MODEL // LANE


def _gdn_prep_specs(T):
    C, dk = GDN_CHUNK, GDN_HEAD_DIM
    npc = min(GDN_PREP_CHUNKS, T // C)
    tc = npc * C
    H = GDN_HEADS
    in_specs = [pl.BlockSpec((tc, dk), lambda i, h: (i, h)),
                pl.BlockSpec((tc, dk), lambda i, h: (i, H + h)),
                pl.BlockSpec((tc, dk), lambda i, h: (i, 2 * H + h)),
                pl.BlockSpec((tc, dk), lambda i, h: (i, GDN_BA_BLOCK)),
                pl.BlockSpec((8, dk), lambda i, h: (0, 0))]
    xs_specs = [pl.BlockSpec((1, tc, C), lambda i, h: (h, i, 0)),
                pl.BlockSpec((1, tc, dk), lambda i, h: (h, i, 0)),
                pl.BlockSpec((1, tc, dk), lambda i, h: (h, i, 0)),
                pl.BlockSpec((1, tc, dk), lambda i, h: (h, i, 0)),
                pl.BlockSpec((1, tc, dk), lambda i, h: (h, i, 0)),
                pl.BlockSpec((1, npc * 8, dk), lambda i, h: (h, i, 0))]
    xs_shapes = [jax.ShapeDtypeStruct((H, T, C), F32)] + [jax.ShapeDtypeStruct((H, T, dk), F32)] * 4 + [
        jax.ShapeDtypeStruct((H, 8 * T // C, dk), F32)]
    return npc, tc, in_specs, xs_specs, xs_shapes


def _lane_pick(x, lane, j):
    return jnp.sum(jnp.where(lane == j, x, 0.0), axis=1, keepdims=True)


def _gdn_head_gates(ba_ref, gates_ref, h, npc):
    lane = lax.broadcasted_iota(jnp.int32, (1, GDN_HEAD_DIM), 1)
    ba = ba_ref[...]
    b = _lane_pick(ba, lane, h).reshape(npc, GDN_CHUNK, 1)
    a = _lane_pick(ba, lane, GDN_HEADS + h).reshape(npc, GDN_CHUNK, 1)
    return b, a, _lane_pick(gates_ref[0:1, :], lane, h), _lane_pick(gates_ref[1:2, :], lane, h), lane


def _gdn_prep_fwd(qkv, proj, gates):
    T = qkv.shape[0]
    C = GDN_CHUNK
    npc, tc, in_specs, xs_specs, xs_shapes = _gdn_prep_specs(T)

    def body(q_ref, k_ref, v_ref, ba_ref, gates_ref, qk_ref, u_ref, w_ref, qd_ref, kd_ref, gl_ref):
        cs = _gdn_consts()
        b, a, alog, dtb, _ = _gdn_head_gates(ba_ref, gates_ref, pl.program_id(1), npc)

        def chunks(val):
            return val.reshape(npc, C, val.shape[-1])

        outs = _gdn_prep_chunk(chunks(q_ref[...]), chunks(k_ref[...]), chunks(v_ref[...]), b, a, alog, dtb, cs)
        for ref, val in zip((qk_ref, u_ref, w_ref, qd_ref, kd_ref), outs[:5]):
            ref[0] = val.reshape(tc, val.shape[-1])
        gl_ref[0] = outs[5].reshape(npc * 8, outs[5].shape[-1])

    return pl.pallas_call(
        body, name="gdn_prep_fwd", grid=(T // tc, GDN_HEADS),
        in_specs=in_specs, out_specs=xs_specs, out_shape=xs_shapes,
        compiler_params=_cp(("parallel", "parallel")),
    )(qkv, qkv, qkv, proj, gates)


def _gdn_prep_bwd(qkv, proj, gates, dxs):
    T = qkv.shape[0]
    C, dk, H = GDN_CHUNK, GDN_HEAD_DIM, GDN_HEADS
    npc, tc, in_specs, xs_specs, _ = _gdn_prep_specs(T)

    def body(q_ref, k_ref, v_ref, ba_ref, gates_ref, dqk_ref, du_ref, dw_ref, dqd_ref, dkd_ref, dgl_ref,
             dqkv_ref, dba_ref, dgates_ref):
        i, h = pl.program_id(0), pl.program_id(1)
        cs = _gdn_consts()
        r8 = lax.broadcasted_iota(jnp.int32, (8, dk), 0)
        c8 = lax.broadcasted_iota(jnp.int32, (8, dk), 1)
        first = (r8 == 0) & (c8 == 0)

        @pl.when((i == 0) & (h == 0))
        def _():
            dgates_ref[...] = jnp.zeros_like(dgates_ref)

        @pl.when(h == 0)
        def _():
            dba_ref[...] = jnp.zeros_like(dba_ref)

        def chunks(val):
            return val.reshape(npc, C, val.shape[-1])

        b, a, alog, dtb, lane = _gdn_head_gates(ba_ref, gates_ref, h, npc)
        prim = (chunks(q_ref[...]), chunks(k_ref[...]), chunks(v_ref[...]), b, a, alog, dtb)
        _, vjp = jax.vjp(lambda *p: _gdn_prep_chunk(*p, cs), *prim)
        dgl = jnp.where(first, dgl_ref[0].reshape(npc, 8, dk), 0.0)
        cts = tuple(chunks(r[0]) for r in (dqk_ref, du_ref, dw_ref, dqd_ref, dkd_ref)) + (dgl,)
        dq, dkk, dv, db, da, dal, ddt = vjp(cts)
        for part, val in enumerate((dq, dkk, dv)):
            dqkv_ref[part, 0] = val.reshape(tc, dk)
        dba_ref[...] += (jnp.where(lane == h, db.reshape(tc, 1), 0.0)
                         + jnp.where(lane == H + h, da.reshape(tc, 1), 0.0))
        dgates_ref[0:1, :] += jnp.where(lane == h, dal, 0.0)
        dgates_ref[1:2, :] += jnp.where(lane == h, ddt, 0.0)

    return pl.pallas_call(
        body, name="gdn_prep_bwd", grid=(T // tc, H),
        in_specs=in_specs + xs_specs,
        out_specs=[pl.BlockSpec((3, 1, tc, dk), lambda i, h: (0, h, i, 0)), pl.BlockSpec((tc, dk), lambda i, h: (i, 0)),
                   pl.BlockSpec((8, dk), lambda i, h: (0, 0))],
        out_shape=[jax.ShapeDtypeStruct((3, H, T, dk), F32), jax.ShapeDtypeStruct((T, dk), F32),
                   jax.ShapeDtypeStruct((8, dk), F32)],
        compiler_params=_cp(("arbitrary", "arbitrary")),
    )(qkv, qkv, qkv, proj, gates, *dxs)


def _gdn_scan_specs(T):
    C, dk, H = GDN_CHUNK, GDN_HEAD_DIM, GDN_HEADS
    return [pl.BlockSpec((H, C, C), lambda n: (0, n, 0))] + [pl.BlockSpec((H, C, dk), lambda n: (0, n, 0))] * 4 + [
        pl.BlockSpec((H, 8, dk), lambda n: (0, n, 0))]


def _gdn_scan_fwd(xs):
    H, T, dk = xs[1].shape
    C = GDN_CHUNK
    n = T // C

    def body(qk_ref, u_ref, w_ref, qd_ref, kd_ref, gl_ref, o_ref, s_ref, state):
        c = pl.program_id(0)

        @pl.when(c == 0)
        def _():
            state[...] = jnp.zeros_like(state)

        S = state[...]
        s_ref[0] = S
        v_new = u_ref[...] - _bdot(w_ref[...], S)
        o_ref[...] = _bdot(qd_ref[...], S) + _bdot(qk_ref[...], v_new)
        state[...] = S * jnp.tile(gl_ref[...], (1, dk // 8, 1)) + _bdot(kd_ref[...], v_new, "tn")

    return pl.pallas_call(
        body, name="gdn_scan_fwd", grid=(n,),
        in_specs=_gdn_scan_specs(T),
        out_specs=[pl.BlockSpec((H, C, dk), lambda n: (0, n, 0)), pl.BlockSpec((1, H, dk, dk), lambda n: (n, 0, 0, 0))],
        out_shape=[jax.ShapeDtypeStruct((H, T, dk), F32), jax.ShapeDtypeStruct((n, H, dk, dk), F32)],
        scratch_shapes=[pltpu.VMEM((H, dk, dk), F32)],
        compiler_params=_cp(("arbitrary",)),
    )(*xs)


def _gdn_scan_bwd(xs, states, do):
    H, T, dk = xs[1].shape
    C = GDN_CHUNK
    n = T // C

    def rev(spec_shape, f):
        return pl.BlockSpec(spec_shape, lambda i: f(n - 1 - i))

    def body(qk_ref, u_ref, w_ref, qd_ref, kd_ref, gl_ref, s_ref, do_ref,
             dqk_ref, du_ref, dw_ref, dqd_ref, dkd_ref, dgl_ref, dstate):
        i = pl.program_id(0)

        @pl.when(i == 0)
        def _():
            dstate[...] = jnp.zeros_like(dstate)

        S = s_ref[0]
        dS = dstate[...]
        do_v = do_ref[...]
        qk, w, qd, kd = qk_ref[...], w_ref[...], qd_ref[...], kd_ref[...]
        v_new = u_ref[...] - _bdot(w, S)
        dv_new = _bdot(qk, do_v, "tn") + _bdot(kd, dS)
        dqk_ref[...] = _bdot(do_v, v_new, "nt")
        dqd_ref[...] = _bdot(do_v, S, "nt")
        dkd_ref[...] = _bdot(v_new, dS, "nt")
        du_ref[...] = dv_new
        dw_ref[...] = -_bdot(dv_new, S, "nt")
        dgl = jnp.sum(jnp.sum(S * dS, axis=2, keepdims=True), axis=1, keepdims=True)
        dgl_ref[...] = jnp.broadcast_to(dgl, dgl_ref.shape)
        dstate[...] = (dS * jnp.tile(gl_ref[...], (1, dk // 8, 1)) + _bdot(qd, do_v, "tn")
                       - _bdot(w, dv_new, "tn"))

    in_specs = [rev((H, C, C), lambda m: (0, m, 0))] + [rev((H, C, dk), lambda m: (0, m, 0))] * 4 + [
        rev((H, 8, dk), lambda m: (0, m, 0)), rev((1, H, dk, dk), lambda m: (m, 0, 0, 0)),
        rev((H, C, dk), lambda m: (0, m, 0))]
    out_specs = [rev((H, C, C), lambda m: (0, m, 0))] + [rev((H, C, dk), lambda m: (0, m, 0))] * 4 + [
        rev((H, 8, dk), lambda m: (0, m, 0))]
    out_shape = [jax.ShapeDtypeStruct((H, T, C), F32)] + [jax.ShapeDtypeStruct((H, T, dk), F32)] * 4 + [
        jax.ShapeDtypeStruct((H, 8 * n, dk), F32)]
    return pl.pallas_call(
        body, name="gdn_scan_bwd", grid=(n,),
        in_specs=in_specs, out_specs=out_specs, out_shape=out_shape,
        scratch_shapes=[pltpu.VMEM((H, dk, dk), F32)],
        compiler_params=_cp(("arbitrary",)),
    )(*xs, states, do)


def _gdn_post_fwd(o, proj, norm_w):
    H, T, dk = o.shape
    tt = _tile(T, 1024)
    zoff = 3 * GDN_HEADS

    def body(o_ref, z_ref, g_ref, y_ref):
        ov = o_ref[0]
        r = lax.rsqrt(jnp.mean(ov * ov, axis=-1, keepdims=True) + EPS)
        y_ref[...] = (ov * r * g_ref[...] * _silu(z_ref[...])).astype(BF16)

    return pl.pallas_call(
        body, name="gdn_post_fwd", grid=(H, T // tt),
        in_specs=[pl.BlockSpec((1, tt, dk), lambda h, i: (h, i, 0)), pl.BlockSpec((tt, dk), lambda h, i: (i, zoff + h)),
                  pl.BlockSpec((1, dk), lambda h, i: (0, 0))],
        out_specs=pl.BlockSpec((tt, dk), lambda h, i: (i, h)),
        out_shape=jax.ShapeDtypeStruct((T, H * dk), BF16),
        compiler_params=_cp(("parallel", "parallel")),
    )(o, proj, norm_w)


def _gdn_post_bwd(o, proj, norm_w, dy):
    H, T, dk = o.shape
    tt = _tile(T, 1024)
    zoff = 3 * GDN_HEADS

    def body(o_ref, z_ref, g_ref, dy_ref, do_ref, dz_ref, dg_ref):
        i = pl.program_id(1)
        ov, z, g, dyv = o_ref[0], z_ref[...], g_ref[...], dy_ref[...]
        r = lax.rsqrt(jnp.mean(ov * ov, axis=-1, keepdims=True) + EPS)
        on = ov * r
        sz = _silu(z)
        dz_ref[...] = (dyv * on * g * _dsilu(z)).astype(BF16)
        dn = dyv * sz
        gdn = dn * g
        do_ref[0] = r * (gdn - on * jnp.mean(gdn * on, axis=-1, keepdims=True))

        @pl.when(i == 0)
        def _():
            dg_ref[...] = jnp.zeros_like(dg_ref)

        dg_ref[0] += jnp.sum(dn * on, axis=0, keepdims=True)

    return pl.pallas_call(
        body, name="gdn_post_bwd", grid=(H, T // tt),
        in_specs=[pl.BlockSpec((1, tt, dk), lambda h, i: (h, i, 0)), pl.BlockSpec((tt, dk), lambda h, i: (i, zoff + h)),
                  pl.BlockSpec((1, dk), lambda h, i: (0, 0)), pl.BlockSpec((tt, dk), lambda h, i: (i, h))],
        out_specs=[pl.BlockSpec((1, tt, dk), lambda h, i: (h, i, 0)), pl.BlockSpec((tt, dk), lambda h, i: (i, h)),
                   pl.BlockSpec((1, 1, dk), lambda h, i: (h, 0, 0))],
        out_shape=[jax.ShapeDtypeStruct((H, T, dk), F32), jax.ShapeDtypeStruct((T, H * dk), BF16),
                   jax.ShapeDtypeStruct((H, 1, dk), F32)],
        compiler_params=_cp(("parallel", "arbitrary")),
    )(o, proj, norm_w, dy)


def _ple_fwd(x, pp, gl, name):
    T, D = x.shape
    tt = _tile(T, 512)

    def body(x_ref, p_ref, g_ref, o_ref):
        o_ref[...] = x_ref[...] + p_ref[...] * _sigmoid(g_ref[...])

    row = pl.BlockSpec((tt, D), lambda i: (i, 0))
    return pl.pallas_call(
        body, name=name, grid=(T // tt,), in_specs=[row, row, row], out_specs=row,
        out_shape=jax.ShapeDtypeStruct((T, D), F32), compiler_params=_cp(("parallel",)),
    )(x, pp, gl)


def _ple_bwd(dx, pp, gl, name, after=None):
    T, D = dx.shape
    tt = _tile(T, 512)

    def body(dx_ref, p_ref, g_ref, *rest):
        dp_ref, dg_ref = rest[-2:]
        s = _sigmoid(g_ref[...])
        dxv = dx_ref[...]
        dp_ref[...] = (dxv * s).astype(BF16)
        dg_ref[...] = (dxv * p_ref[...] * s * (1.0 - s)).astype(BF16)

    row = pl.BlockSpec((tt, D), lambda i: (i, 0))
    ordered = [] if after is None else [after]
    return pl.pallas_call(
        body, name=name, grid=(T // tt,),
        in_specs=[row, row, row] + [pl.BlockSpec((8, LANE), lambda i: (0, 0)) for _ in ordered], out_specs=[row, row],
        out_shape=[jax.ShapeDtypeStruct((T, D), BF16)] * 2, compiler_params=_cp(("parallel",)),
    )(dx, pp, gl, *ordered)


UP_SHARD = 2 * FFN_DIM // N_CHIPS
DOWN_SHARD = FFN_DIM // N_CHIPS
GATE_SHARD = D_MODEL // N_CHIPS
IN_E_SHARD = 2 * D_MODEL // N_CHIPS


def _ffn_fwd(x, norm, W, conv_w, l):
    T = x.shape[0]
    hf = _rms_fwd(x, norm, f"ffn_norm{l}")
    up = _mm(hf, W["w_up"], "nn", f"ffn_up{l}", dims=(T, 2 * FFN_DIM, D_MODEL), b_view=_cols_of(1, l), tn=UP_SHARD,
             out_dtype=BF16)
    act = _ffn_act_fwd(up, conv_w, f"ffn_act{l}")
    x_out = _mm(act, W["w_down"], "nn", f"ffn_down{l}", dims=(T, D_MODEL, FFN_DIM), b_view=_layer_of(l), res=x,
                tn=1024, tk=1408)
    return x_out, (x, hf, up, act)


def _ffn_bwd(dx_out, saved, norm, W, conv_w, l):
    x, hf, up, act = saved
    T = x.shape[0]
    dact = _mm(dx_out, W["w_down"], "nt", f"ffn_dact{l}", dims=(T, FFN_DIM, D_MODEL), b_view=_layer_of(l),
               out_dtype=BF16, tn=1408)
    dw_down = _mm(act, dx_out, "tn", f"ffn_dwdown{l}", tm=1408, tn=1024, tk=512)
    dup, dcw_g, dcw_v = _ffn_act_bwd(up, conv_w, dact, f"ffn_dact_conv{l}")
    dw_up = _mm(hf, dup, "tn", f"ffn_dwup{l}", dims=(D_MODEL, 2 * FFN_DIM, T), b_view=_cols_of(FFN_DIM // UP_SHARD),
                o_view=_cols_of(1), out_shape=(N_CHIPS, D_MODEL, UP_SHARD), tm=1024, tn=UP_SHARD, tk=512)
    dhf = _mm(dup, W["w_up"], "nt", f"ffn_dhf{l}", dims=(T, D_MODEL, 2 * FFN_DIM),
              a_view=_cols_of(FFN_DIM // UP_SHARD), b_view=_cols_of(1, l), tn=1024, tk=UP_SHARD)
    dx, dnorm = _rms_bwd(x, norm, dhf, dx_out, f"ffn_dnorm{l}")
    return (dx, dnorm, dw_up, jnp.concatenate([dcw_g, dcw_v], axis=1),
            dw_down.reshape(N_CHIPS, DOWN_SHARD, D_MODEL))


def _ple_layer_fwd(x, p, norm, W, l):
    T = x.shape[0]
    hg = _rms_fwd(x, norm, f"ple_norm{l}")
    gl = _mm(hg, W["w_ple_gate"], "nn", f"ple_gate{l}", dims=(T, D_MODEL, D_MODEL), b_view=_layer_of(l), tn=1024)
    pp = _mm(p, W["w_ple"], "nn", f"ple_proj{l}", dims=(T, D_MODEL, PLE_DIM), a_view=_layer_of(l),
             b_view=_layer_of(l), tn=1024)
    return _ple_fwd(x, pp, gl, f"ple_mix{l}"), (x, hg, gl, pp)


def _ple_layer_bwd(dx_out, saved, p, norm, W, l, after=None):
    x, hg, gl, pp = saved
    T = x.shape[0]
    dpp, dgl = _ple_bwd(dx_out, pp, gl, f"ple_dmix{l}", after)
    dw_ple = _mm(p, dpp, "tn", f"ple_dwple{l}", dims=(PLE_DIM, D_MODEL, T), a_view=_layer_of(l), o_view=_cols_of(1),
                 out_shape=(N_CHIPS, PLE_DIM, PLE_DIM), tm=PLE_DIM, tn=PLE_DIM, tk=512)
    dw_gate = _mm(hg, dgl, "tn", f"ple_dwgate{l}", tm=1024, tn=1024, tk=512)
    dhg = _mm(dgl, W["w_ple_gate"], "nt", f"ple_dhg{l}", dims=(T, D_MODEL, D_MODEL), b_view=_layer_of(l), tn=1024)
    dx, dnorm = _rms_bwd(x, norm, dhg, dx_out, f"ple_dnorm{l}")
    return dx, dnorm, dw_gate.reshape(N_CHIPS, GATE_SHARD, D_MODEL), dw_ple


def _local_step(x, p, target, W, token=None, later_weights=None, on_grads=None):
    T = x.shape[0]
    H = GDN_HEADS
    G = {}

    hn_e = _rms_fwd(x, W["mix_norm_e"], "mix_norm_e", after=token)
    proj_e = _mm(hn_e, W["w_in_e"], "nn", "in_e", dims=(T, 2 * D_MODEL, D_MODEL), b_view=_cols_of(1), tn=IN_E_SHARD)
    pool_out = _pool_fwd(proj_e, W["pool_w"], W["pool_scale"])
    attn, ltot = _sb_fwd(proj_e)
    mix_e = jnp.concatenate([pool_out, attn], axis=1).astype(BF16)
    x1 = _mm(mix_e, W["w_out_e"], "nn", "out_e", res=x, tn=1024)
    if later_weights is not None:
        W = {**W, **later_weights(x1)}
    x2, ffn0 = _ffn_fwd(x1, W["ffn_norm"][0:1], W, W["ffn_conv"][0], 0)
    x3, ple0 = _ple_layer_fwd(x2, p, W["ple_norm"][0:1], W, 0)

    hn_o = _rms_fwd(x3, W["mix_norm_o"], "mix_norm_o")
    proj_o = _mm(hn_o, W["w_in_o"], "nn", "in_o", tn=1408)
    qkv = _gdn_pre_fwd(proj_o, W["conv_qkv_o"])
    gates = jnp.pad(jnp.concatenate([W["a_log_o"], W["dt_bias_o"]], axis=0), ((0, 6), (0, LANE - H)))
    xs = _gdn_prep_fwd(qkv, proj_o, gates)
    o, states = _gdn_scan_fwd(xs)
    og = _gdn_post_fwd(o, proj_o, W["gdn_norm_o"])
    x4 = _mm(og, W["w_out_o"], "nn", "out_o", res=x3, tn=1024)
    x5, ffn1 = _ffn_fwd(x4, W["ffn_norm"][1:2], W, W["ffn_conv"][1], 1)
    x6, ple1 = _ple_layer_fwd(x5, p, W["ple_norm"][1:2], W, 1)

    sq, dx6, G["final_norm"] = _final_loss(x6, W["final_norm"], target, "final_loss")

    dx5, dpn1, dwg1, dwp1 = _ple_layer_bwd(dx6, ple1, p, W["ple_norm"][1:2], W, 1)
    dx4, dfn1, dwu1, dfc1, dwd1 = _ffn_bwd(dx5, ffn1, W["ffn_norm"][1:2], W, W["ffn_conv"][1], 1)
    dog = _mm(dx4, W["w_out_o"], "nt", "d_og", tn=1024)
    G["w_out_o"] = _mm(og, dx4, "tn", "dw_out_o", tm=1024, tn=1024, tk=512).reshape(N_CHIPS, GATE_SHARD, D_MODEL)
    do, dz, dgn = _gdn_post_bwd(o, proj_o, W["gdn_norm_o"], dog)
    G["gdn_norm_o"] = jnp.sum(dgn, axis=0)
    dxs = _gdn_scan_bwd(xs, states, do)
    dqkv_act, dba, dgates = _gdn_prep_bwd(qkv, proj_o, gates, dxs)
    G["a_log_o"] = dgates[0:1, :H]
    G["dt_bias_o"] = dgates[1:2, :H]
    dqkv, G["conv_qkv_o"] = _gdn_pre_bwd(proj_o, W["conv_qkv_o"], dqkv_act)
    dproj_o = jnp.concatenate([dqkv, dz, dba.astype(BF16)], axis=1)
    G["w_in_o"] = _mm(hn_o, dproj_o, "tn", "dw_in_o", tm=1024, tn=1408, tk=512)
    dhn_o = _mm(dproj_o, W["w_in_o"], "nt", "d_hn_o", tn=1024, tk=1408)
    dx3, G["mix_norm_o"] = _rms_bwd(x3, W["mix_norm_o"], dhn_o, dx4, "d_mix_norm_o")

    token1 = token0 = None
    if on_grads is not None:
        token1 = on_grads("second", {("w_in_o", 0): _chip_major_w_in_o(G["w_in_o"]), ("w_out_o", 0): G["w_out_o"],
                                     ("w_up", 1): dwu1, ("w_down", 1): dwd1, ("w_ple_gate", 1): dwg1,
                                     ("w_ple", 1): dwp1})
    dx2, dpn0, dwg0, dwp0 = _ple_layer_bwd(dx3, ple0, p, W["ple_norm"][0:1], W, 0, after=token1)
    dx1, dfn0, dwu0, dfc0, dwd0 = _ffn_bwd(dx2, ffn0, W["ffn_norm"][0:1], W, W["ffn_conv"][0], 0)
    if on_grads is not None:
        token0 = on_grads("first_ffn", {("w_up", 0): dwu0, ("w_down", 0): dwd0, ("w_ple_gate", 0): dwg0,
                                        ("w_ple", 0): dwp0})
    dmix = _mm(dx1, W["w_out_e"], "nt", "d_mix_e", tn=1024)
    G["w_out_e"] = _mm(mix_e, dx1, "tn", "dw_out_e", tm=1024, tn=1024, tk=512).reshape(N_CHIPS, GATE_SHARD, D_MODEL)
    du, G["pool_w"], G["pool_scale"] = _pool_bwd(proj_e, W["pool_w"], W["pool_scale"], dmix)
    dqa, dka, dva = _sb_bwd(proj_e, dmix, ltot, after=token0)
    dproj_e = jnp.concatenate([du, dqa, dka, dva], axis=1).astype(BF16)
    G["w_in_e"] = _mm(hn_e, dproj_e, "tn", "dw_in_e", dims=(D_MODEL, 2 * D_MODEL, T), o_view=_cols_of(1),
                      out_shape=(N_CHIPS, D_MODEL, IN_E_SHARD), tm=1024, tn=IN_E_SHARD, tk=512)
    dhn_e = _mm(dproj_e, W["w_in_e"], "nt", "d_hn_e", dims=(T, D_MODEL, 2 * D_MODEL), b_view=_cols_of(1),
                tn=1024, tk=IN_E_SHARD)
    grad_x, G["mix_norm_e"] = _rms_bwd(x, W["mix_norm_e"], dhn_e, dx1, "d_mix_norm_e")

    G["ffn_norm"] = jnp.concatenate([dfn0, dfn1], axis=0)
    G["ple_norm"] = jnp.concatenate([dpn0, dpn1], axis=0)
    G["ffn_conv"] = jnp.stack([dfc0, dfc1])
    G["w_up"] = [dwu0, dwu1]
    G["w_down"] = [dwd0, dwd1]
    G["w_ple_gate"] = [dwg0, dwg1]
    G["w_ple"] = [dwp0, dwp1]
    return sq[0, 0], grad_x, G


BIG = ("w_in_e", "w_out_e", "w_in_o", "w_out_o", "w_up", "w_down", "w_ple_gate", "w_ple")
SHARDED_SMALL = (("mix_norm_o", 1), ("conv_qkv_o", 2), ("ffn_conv", 2))
REPLICATED = ("mix_norm_e", "pool_w", "pool_scale", "a_log_o", "dt_bias_o", "gdn_norm_o", "ffn_norm", "ple_norm",
              "final_norm")
WEIGHT_ORDER = ("mix_norm_e", "w_in_e", "pool_w", "pool_scale", "w_out_e", "mix_norm_o", "w_in_o", "conv_qkv_o",
                "a_log_o", "dt_bias_o", "gdn_norm_o", "w_out_o", "ffn_norm", "w_up", "ffn_conv", "w_down", "ple_norm",
                "w_ple_gate", "w_ple", "final_norm")
SMALL_W = LANE
SMALL_ROWS = 16


def _size(shape):
    n = 1
    for s in shape:
        n *= s
    return n


def _pack(arrs, width, granule):
    flat = jnp.concatenate([a.reshape(-1) for a in arrs])
    rows = -(-flat.shape[0] // width)
    rows = -(-rows // granule) * granule
    return jnp.pad(flat, (0, rows * width - flat.shape[0])).reshape(rows, width)


def _unpack(flat2d, shapes):
    flat = flat2d.reshape(-1)
    out, off = [], 0
    for s in shapes:
        out.append(flat[off:off + _size(s)].reshape(s))
        off += _size(s)
    return out


MESH_ID = pl.DeviceIdType.MESH
HBM_SPEC = pl.BlockSpec(memory_space=pltpu.HBM)


def _where_am_i():
    return lax.axis_index("x"), lax.axis_index("y"), lax.axis_index("c")


def _other_chips(x, y):
    return [(1 - x, y), (x, 1 - y), (1 - x, 1 - y)]


def _remote(src, dst, send_sems, recv_sems, k, to):
    return pltpu.make_async_remote_copy(src_ref=src, dst_ref=dst, send_sem=send_sems.at[k], recv_sem=recv_sems.at[k],
                                        device_id=to, device_id_type=MESH_ID)


def _chip_allgather(pack, name):
    R, Wd = pack.shape
    Rh = R // 2

    def body(src_ref, out_ref, send_sems, recv_sems, local_sem):
        x, y, c = _where_am_i()
        me, sib = (x, y, c), (x, y, 1 - c)
        chips = _other_chips(x, y)
        mine_rows = pl.ds(pl.multiple_of(c * Rh, SMALL_ROWS), Rh)
        sib_rows = pl.ds(pl.multiple_of((1 - c) * Rh, SMALL_ROWS), Rh)
        j_me = 2 * x + y
        local = pltpu.make_async_copy(src_ref, out_ref.at[j_me], local_sem)
        local.start()
        first = [_remote(src_ref.at[mine_rows], out_ref.at[j_me, mine_rows], send_sems, recv_sems, k, (cx, cy, c))
                 for k, (cx, cy) in enumerate(chips)]
        for cp in first:
            cp.start()
        passed = []
        for k, (cx, cy) in enumerate(chips):
            blk = out_ref.at[2 * cx + cy, mine_rows]
            _remote(blk, blk, send_sems, recv_sems, k, me).wait_recv()
            fw = _remote(blk, blk, send_sems, recv_sems, 3 + k, sib)
            fw.start()
            passed.append(fw)
        for k, (cx, cy) in enumerate(chips):
            blk = out_ref.at[2 * cx + cy, sib_rows]
            _remote(blk, blk, send_sems, recv_sems, 3 + k, me).wait_recv()
        for cp in first + passed:
            cp.wait_send()
        local.wait()

    return pl.pallas_call(
        body, name=name, in_specs=[HBM_SPEC], out_specs=HBM_SPEC,
        out_shape=jax.ShapeDtypeStruct((N_CHIPS, R, Wd), pack.dtype),
        scratch_shapes=[pltpu.SemaphoreType.DMA((6,)), pltpu.SemaphoreType.DMA((6,)), pltpu.SemaphoreType.DMA],
    )(pack)


def _chip_allgather_many(blocks, name):
    n = len(blocks)

    def body(*refs):
        srcs, outs = refs[:n], refs[n:2 * n]
        send_sems, recv_sems = refs[2 * n:]
        x, y, c = _where_am_i()
        me, sib = (x, y, c), (x, y, 1 - c)
        chips = _other_chips(x, y)
        j_me = 2 * x + y
        first = [_remote(srcs[p].at[c], outs[p].at[j_me, c], send_sems, recv_sems, 6 * p + k, (cx, cy, c))
                 for p in range(n) for k, (cx, cy) in enumerate(chips)]
        for cp in first:
            cp.start()
        passed = []
        for k, (cx, cy) in enumerate(chips):
            for p in range(n):
                blk = outs[p].at[2 * cx + cy, c]
                _remote(blk, blk, send_sems, recv_sems, 6 * p + k, me).wait_recv()
                fw = _remote(blk, blk, send_sems, recv_sems, 6 * p + 3 + k, sib)
                fw.start()
                passed.append(fw)
        for k, (cx, cy) in enumerate(chips):
            for p in range(n):
                blk = outs[p].at[2 * cx + cy, 1 - c]
                _remote(blk, blk, send_sems, recv_sems, 6 * p + 3 + k, me).wait_recv()
        for cp in first + passed:
            cp.wait_send()

    return pl.pallas_call(
        body, name=name, in_specs=[HBM_SPEC] * n, out_specs=[HBM_SPEC] * n,
        out_shape=[jax.ShapeDtypeStruct((N_CHIPS,) + b.shape, b.dtype) for b in blocks],
        scratch_shapes=[pltpu.SemaphoreType.DMA((6 * n,)), pltpu.SemaphoreType.DMA((6 * n,))],
    )(*blocks)


SEM_SPEC = pl.BlockSpec(memory_space=pltpu.SEMAPHORE)
DATAFLOW_EFFECT = pltpu.SideEffectType.DATAFLOW_SIDE_EFFECTING


def _chip_allgather_start(blocks, name):
    n = len(blocks)

    def body(*refs):
        srcs, lands = refs[:n], refs[n:2 * n]
        send_sems, recv_sems, token = refs[2 * n], refs[2 * n + 1], refs[-1]
        x, y, c = _where_am_i()
        j_me = 2 * x + y
        for p in range(n):
            for k, (cx, cy) in enumerate(_other_chips(x, y)):
                _remote(srcs[p].at[c], lands[p].at[j_me, c], send_sems, recv_sems, 3 * p + k, (cx, cy, c)).start()
        token[...] = jnp.zeros_like(token)

    lands = [pltpu.with_memory_space_constraint(lax.empty((N_CHIPS,) + b.shape, b.dtype), pltpu.HBM) for b in blocks]
    blocks = [pltpu.with_memory_space_constraint(b, pltpu.HBM) for b in blocks]
    outs = pl.pallas_call(
        body, name=name,
        in_specs=[HBM_SPEC] * (2 * n),
        out_specs=[SEM_SPEC, SEM_SPEC] + [HBM_SPEC] * (2 * n) + [pl.BlockSpec(memory_space=pltpu.VMEM)],
        out_shape=[pltpu.SemaphoreType.DMA((3 * n,)), pltpu.SemaphoreType.DMA((3 * n,))]
        + [pltpu.HBM(a.shape, a.dtype) for a in blocks + lands] + [jax.ShapeDtypeStruct((8, LANE), F32)],
        input_output_aliases={i: 2 + i for i in range(2 * n)},
        compiler_params=pltpu.CompilerParams(has_side_effects=DATAFLOW_EFFECT),
    )(*blocks, *lands)
    return outs[0], outs[1], list(outs[2:2 + n]), list(outs[2 + n:2 + 2 * n]), outs[-1]


def _chip_allgather_wait(send_sems, recv_sems, blocks, lands, after, name):
    n = len(blocks)

    def body(*refs):
        srcs, zones = refs[:n], refs[n:2 * n]
        send, recv = refs[2 * n], refs[2 * n + 1]
        x, y, c = _where_am_i()
        for p in range(n):
            for k, (cx, cy) in enumerate(_other_chips(x, y)):
                cp = _remote(srcs[p].at[c], zones[p].at[2 * cx + cy, c], send, recv, 3 * p + k, (x, y, c))
                cp.wait_send()
                cp.wait_recv()

    outs = pl.pallas_call(
        body, name=name,
        in_specs=[HBM_SPEC] * (2 * n) + [SEM_SPEC, SEM_SPEC, pl.BlockSpec(memory_space=pl.ANY)],
        out_specs=[HBM_SPEC] * (2 * n),
        out_shape=[pltpu.HBM(a.shape, a.dtype) for a in list(blocks) + list(lands)],
        input_output_aliases={i: i for i in range(2 * n)},
        compiler_params=pltpu.CompilerParams(has_side_effects=DATAFLOW_EFFECT),
    )(*blocks, *lands, send_sems, recv_sems, after)
    return list(outs[n:])


def _chip_scatter_start(sums, name):
    n = len(sums)

    def body(*refs):
        srcs, lands = refs[:n], refs[n:2 * n]
        send_sems, recv_sems, token = refs[2 * n], refs[2 * n + 1], refs[-1]
        x, y, c = _where_am_i()
        for p in range(n):
            for k, (cx, cy) in enumerate(_other_chips(x, y)):
                _remote(srcs[p].at[2 * cx + cy], lands[p].at[k], send_sems, recv_sems, 3 * p + k, (cx, cy, c)).start()
        token[...] = jnp.zeros_like(token)

    lands = [pltpu.with_memory_space_constraint(lax.empty((N_CHIPS - 1,) + s.shape[1:], s.dtype), pltpu.HBM)
             for s in sums]
    sums = [pltpu.with_memory_space_constraint(s, pltpu.HBM) for s in sums]
    outs = pl.pallas_call(
        body, name=name,
        in_specs=[HBM_SPEC] * (2 * n),
        out_specs=[SEM_SPEC, SEM_SPEC] + [HBM_SPEC] * (2 * n) + [pl.BlockSpec(memory_space=pltpu.VMEM)],
        out_shape=[pltpu.SemaphoreType.DMA((3 * n,)), pltpu.SemaphoreType.DMA((3 * n,))]
        + [pltpu.HBM(a.shape, a.dtype) for a in sums + lands] + [jax.ShapeDtypeStruct((8, LANE), F32)],
        input_output_aliases={i: 2 + i for i in range(2 * n)},
        compiler_params=pltpu.CompilerParams(has_side_effects=DATAFLOW_EFFECT),
    )(*sums, *lands)
    return outs[0], outs[1], list(outs[2:2 + n]), list(outs[2 + n:2 + 2 * n]), outs[-1]


def _chip_scatter_wait(send_sems, recv_sems, sums, lands, after, name):
    n = len(sums)

    def body(*refs):
        srcs, zones = refs[:n], refs[n:2 * n]
        send, recv = refs[2 * n], refs[2 * n + 1]
        x, y, c = _where_am_i()
        for p in range(n):
            for k, (cx, cy) in enumerate(_other_chips(x, y)):
                cp = _remote(srcs[p].at[2 * cx + cy], zones[p].at[k], send, recv, 3 * p + k, (x, y, c))
                cp.wait_send()
                cp.wait_recv()

    outs = pl.pallas_call(
        body, name=name,
        in_specs=[HBM_SPEC] * (2 * n) + [SEM_SPEC, SEM_SPEC, pl.BlockSpec(memory_space=pl.ANY)],
        out_specs=[HBM_SPEC] * (2 * n),
        out_shape=[pltpu.HBM(a.shape, a.dtype) for a in list(sums) + list(lands)],
        input_output_aliases={i: i for i in range(2 * n)},
        compiler_params=pltpu.CompilerParams(has_side_effects=DATAFLOW_EFFECT),
    )(*sums, *lands, send_sems, recv_sems, after)
    return list(outs[n:])


def _chip_allgather_forward(lands, name):
    n = len(lands)

    def body(*refs):
        ins, outs = refs[:n], refs[n:2 * n]
        send_sems, recv_sems = refs[2 * n:]
        x, y, c = _where_am_i()
        me, sib = (x, y, c), (x, y, 1 - c)
        chips = _other_chips(x, y)
        passed = [_remote(ins[p].at[2 * cx + cy, c], outs[p].at[2 * cx + cy, c], send_sems, recv_sems, 3 * p + k, sib)
                  for p in range(n) for k, (cx, cy) in enumerate(chips)]
        for cp in passed:
            cp.start()
        for p in range(n):
            for k, (cx, cy) in enumerate(chips):
                blk = outs[p].at[2 * cx + cy, 1 - c]
                _remote(blk, blk, send_sems, recv_sems, 3 * p + k, me).wait_recv()
        for cp in passed:
            cp.wait_send()

    return pl.pallas_call(
        body, name=name, in_specs=[HBM_SPEC] * n, out_specs=[HBM_SPEC] * n,
        out_shape=[jax.ShapeDtypeStruct(a.shape, a.dtype) for a in lands],
        input_output_aliases={i: i for i in range(n)},
        scratch_shapes=[pltpu.SemaphoreType.DMA((3 * n,)), pltpu.SemaphoreType.DMA((3 * n,))],
    )(*lands)


def _sibling_swap_many(pieces, name):
    n = len(pieces)

    def body(*refs):
        srcs, outs = refs[:n], refs[n:2 * n]
        send_sems, recv_sems = refs[2 * n:]
        x, y, c = _where_am_i()
        cps = [_remote(srcs[p].at[:, 1 - c], outs[p], send_sems, recv_sems, p, (x, y, 1 - c)) for p in range(n)]
        for cp in cps:
            cp.start()
        for cp in cps:
            cp.wait()

    return pl.pallas_call(
        body, name=name, in_specs=[HBM_SPEC] * n, out_specs=[HBM_SPEC] * n,
        out_shape=[jax.ShapeDtypeStruct((g.shape[0],) + g.shape[2:], g.dtype) for g in pieces],
        scratch_shapes=[pltpu.SemaphoreType.DMA((n,)), pltpu.SemaphoreType.DMA((n,))],
    )(*pieces)


def _chip_scatter_many(sums, name):
    n = len(sums)

    def body(*refs):
        srcs, outs = refs[:n], refs[n:2 * n]
        send_sems, recv_sems = refs[2 * n:]
        x, y, c = _where_am_i()
        cps = [_remote(srcs[p].at[2 * cx + cy], outs[p].at[k], send_sems, recv_sems, 3 * p + k, (cx, cy, c))
               for p in range(n) for k, (cx, cy) in enumerate(_other_chips(x, y))]
        for cp in cps:
            cp.start()
        for cp in cps:
            cp.wait()

    return pl.pallas_call(
        body, name=name, in_specs=[HBM_SPEC] * n, out_specs=[HBM_SPEC] * n,
        out_shape=[jax.ShapeDtypeStruct((N_CHIPS - 1,) + s.shape[1:], s.dtype) for s in sums],
        scratch_shapes=[pltpu.SemaphoreType.DMA((3 * n,)), pltpu.SemaphoreType.DMA((3 * n,))],
    )(*sums)


def _sibling_send_many(halves, name):
    n = len(halves)

    def body(*refs):
        srcs, outs = refs[:n], refs[n:2 * n]
        send_sems, recv_sems = refs[2 * n:]
        x, y, c = _where_am_i()
        cps = [_remote(srcs[p], outs[p], send_sems, recv_sems, p, (x, y, 1 - c)) for p in range(n)]
        for cp in cps:
            cp.start()
        for cp in cps:
            cp.wait()

    return pl.pallas_call(
        body, name=name, in_specs=[HBM_SPEC] * n, out_specs=[HBM_SPEC] * n,
        out_shape=[jax.ShapeDtypeStruct(h.shape, h.dtype) for h in halves],
        scratch_shapes=[pltpu.SemaphoreType.DMA((n,)), pltpu.SemaphoreType.DMA((n,))],
    )(*halves)


def _row_tile(rows, pref=512):
    best = 8
    for t in range(8, pref + 1, 8):
        if rows % t == 0:
            best = t
    return best


def _where_ids():
    x, y, c = _where_am_i()
    return jnp.stack([c, 2 * x + y]).astype(jnp.int32)


RS_ROWS = 256


def _chip_sums_bf16(G, A, ids, name):
    n, _, hr, cols = G.shape
    tr = _row_tile(hr, RS_ROWS)

    def body(ids_ref, g_ref, a_ref, o_ref):
        o_ref[...] = (g_ref[...] + a_ref[...]).astype(BF16)

    return pl.pallas_call(
        body, name=name,
        grid_spec=pltpu.PrefetchScalarGridSpec(
            num_scalar_prefetch=1, grid=(n, hr // tr),
            in_specs=[pl.BlockSpec((None, None, tr, cols), lambda j, i, ids: (j, ids[0], i, 0)),
                      pl.BlockSpec((None, tr, cols), lambda j, i, ids: (j, i, 0))],
            out_specs=pl.BlockSpec((None, tr, cols), lambda j, i, ids: (j, i, 0))),
        out_shape=jax.ShapeDtypeStruct((n, hr, cols), BF16),
        compiler_params=_cp(("parallel", "parallel")),
    )(ids, G, A)


def _total_half(G, A, B, ids, name):
    _, _, hr, cols = G.shape
    tr = _row_tile(hr, RS_ROWS)

    def body(ids_ref, g_ref, a_ref, b_ref, o_ref):
        s = g_ref[...] + a_ref[...]
        for k in range(N_CHIPS - 1):
            s = s + b_ref[k].astype(F32)
        o_ref[...] = s

    return pl.pallas_call(
        body, name=name,
        grid_spec=pltpu.PrefetchScalarGridSpec(
            num_scalar_prefetch=1, grid=(hr // tr,),
            in_specs=[pl.BlockSpec((None, None, tr, cols), lambda i, ids: (ids[1], ids[0], i, 0)),
                      pl.BlockSpec((None, tr, cols), lambda i, ids: (ids[1], i, 0)),
                      pl.BlockSpec((N_CHIPS - 1, tr, cols), lambda i, ids: (0, i, 0))],
            out_specs=pl.BlockSpec((tr, cols), lambda i, ids: (i, 0))),
        out_shape=jax.ShapeDtypeStruct((hr, cols), F32),
        compiler_params=_cp(("parallel",)),
    )(ids, G, A, B)


def _small_allreduce(v, name):
    R, Wd = v.shape

    def body(x_ref, sum_ref, all_ref, send_sems, recv_sems, local_sem):
        x, y, c = _where_am_i()
        me, sib = (x, y, c), (x, y, 1 - c)
        chips = _other_chips(x, y)

        def slot(px, py, pc):
            return all_ref.at[4 * px + 2 * py + pc]

        local = pltpu.make_async_copy(x_ref, slot(*me), local_sem)
        local.start()
        first = [_remote(x_ref, slot(*me), send_sems, recv_sems, 0, sib)]
        first += [_remote(x_ref, slot(*me), send_sems, recv_sems, 1 + k, (cx, cy, c)) for k, (cx, cy) in enumerate(chips)]
        for cp in first:
            cp.start()
        passed = []
        for k, (cx, cy) in enumerate(chips):
            blk = slot(cx, cy, c)
            _remote(blk, blk, send_sems, recv_sems, 1 + k, me).wait_recv()
            fw = _remote(blk, blk, send_sems, recv_sems, 4 + k, sib)
            fw.start()
            passed.append(fw)
        _remote(slot(*sib), slot(*sib), send_sems, recv_sems, 0, me).wait_recv()
        for k, (cx, cy) in enumerate(chips):
            blk = slot(cx, cy, 1 - c)
            _remote(blk, blk, send_sems, recv_sems, 4 + k, me).wait_recv()
        for cp in first + passed:
            cp.wait_send()
        local.wait()
        s = all_ref[0]
        for d in range(1, N_DEV):
            s = s + all_ref[d]
        sum_ref[...] = s

    vm = pl.BlockSpec(memory_space=pltpu.VMEM)
    return pl.pallas_call(
        body, name=name, in_specs=[vm], out_specs=[vm, vm],
        out_shape=[jax.ShapeDtypeStruct((R, Wd), F32), jax.ShapeDtypeStruct((N_DEV, R, Wd), F32)],
        scratch_shapes=[pltpu.SemaphoreType.DMA((7,)), pltpu.SemaphoreType.DMA((7,)), pltpu.SemaphoreType.DMA],
    )(v)[0]


def _adamw(w, g, m, v, name):
    L, R, Wd = w.shape
    tr = _row_tile(R, RS_ROWS)
    c1 = 1.0 - ADAM_B1 ** ADAM_STEP
    c2 = 1.0 - ADAM_B2 ** ADAM_STEP

    def body(w_ref, g_ref, m_ref, v_ref, d_ref, nm_ref, nv_ref):
        gv = g_ref[...]
        nm = ADAM_B1 * m_ref[...] + (1.0 - ADAM_B1) * gv
        nv = ADAM_B2 * v_ref[...] + (1.0 - ADAM_B2) * (gv * gv)
        d_ref[...] = -ADAM_LR * ((nm / c1) / (jnp.sqrt(nv / c2) + ADAM_EPS) + ADAM_WD * w_ref[...])
        nm_ref[...] = nm
        nv_ref[...] = nv

    row = pl.BlockSpec((None, tr, Wd), lambda l, i: (l, i, 0))
    shp = jax.ShapeDtypeStruct((L, R, Wd), F32)
    return pl.pallas_call(
        body, name=name, grid=(L, R // tr), in_specs=[row] * 4, out_specs=[row] * 3, out_shape=[shp] * 3,
        compiler_params=_cp(("parallel", "parallel")),
    )(w, g, m, v)


def _adamw_halves(w, m, v, mine, theirs, ids, name):
    L, R, Wd = w.shape
    hr = R // 2
    tr = _row_tile(hr, RS_ROWS)
    c1 = 1.0 - ADAM_B1 ** ADAM_STEP
    c2 = 1.0 - ADAM_B2 ** ADAM_STEP

    def body(ids_ref, w_ref, m_ref, v_ref, *refs):
        g_refs, (g_ref, d_ref, nm_ref, nv_ref) = refs[:2 * L], refs[2 * L:]
        layer, half = pl.program_id(0), pl.program_id(1)
        own = half == ids_ref[0]
        gv = jnp.where(own, g_refs[0][...], g_refs[L][...])
        for l in range(1, L):
            gv = jnp.where(layer == l, jnp.where(own, g_refs[l][...], g_refs[L + l][...]), gv)
        nm = ADAM_B1 * m_ref[...] + (1.0 - ADAM_B1) * gv
        nv = ADAM_B2 * v_ref[...] + (1.0 - ADAM_B2) * (gv * gv)
        g_ref[...] = gv
        d_ref[...] = -ADAM_LR * ((nm / c1) / (jnp.sqrt(nv / c2) + ADAM_EPS) + ADAM_WD * w_ref[...])
        nm_ref[...] = nm
        nv_ref[...] = nv

    blk = pl.BlockSpec((None, None, tr, Wd), lambda l, h, i, ids: (l, h, i, 0))
    g_blk = pl.BlockSpec((tr, Wd), lambda l, h, i, ids: (i, 0))
    shp = jax.ShapeDtypeStruct((L, 2, hr, Wd), F32)
    outs = pl.pallas_call(
        body, name=name,
        grid_spec=pltpu.PrefetchScalarGridSpec(
            num_scalar_prefetch=1, grid=(L, 2, hr // tr),
            in_specs=[blk] * 3 + [g_blk] * (2 * L), out_specs=[blk] * 4),
        out_shape=[shp] * 4,
        compiler_params=_cp(("parallel", "parallel", "parallel")),
    )(ids, *[a.reshape(L, 2, hr, Wd) for a in (w, m, v)], *mine, *theirs)
    return tuple(o.reshape(L, R, Wd) for o in outs)


def _two_halves(a):
    cols = a.shape[-1]
    return a.reshape(2, _size(a.shape) // (2 * cols), cols)


FIRST_NEEDED = ("w_in_e", "w_out_e")
LATER_NEEDED = tuple(n for n in BIG if n not in FIRST_NEEDED)


def _gather_weights(P):
    chip = 2 * lax.axis_index("x") + lax.axis_index("y")

    def with_own(landed, own):
        return lax.dynamic_update_slice_in_dim(landed, own[None], chip, axis=0)

    mine = {n: _two_halves(P[n].astype(BF16)) for n in BIG}
    first = _chip_allgather_many([mine[n] for n in FIRST_NEEDED], "ag_first")
    gathered = {n: with_own(g, mine[n]) for n, g in zip(FIRST_NEEDED, first)}
    send_sems, recv_sems, blocks, lands, token = _chip_allgather_start([mine[n] for n in LATER_NEEDED], "ag_start")

    def later(after):
        landed = _chip_allgather_wait(send_sems, recv_sems, blocks, lands, after, "ag_wait")
        g = {n: with_own(a, mine[n]) for n, a in zip(LATER_NEEDED, _chip_allgather_forward(landed, "ag_forward"))}
        w_in_o = g["w_in_o"].reshape(N_CHIPS, D_MODEL, ODD_IN // N_CHIPS)
        return {
            "w_out_o": g["w_out_o"].reshape(D_MODEL, D_MODEL),
            "w_in_o": jnp.pad(jnp.concatenate([w_in_o[j] for j in range(N_CHIPS)], axis=1),
                              ((0, 0), (0, ODD_IN_PAD - ODD_IN))),
            "w_up": g["w_up"],
            "w_down": g["w_down"].transpose(1, 0, 2, 3).reshape(2, FFN_DIM, D_MODEL),
            "w_ple_gate": g["w_ple_gate"].transpose(1, 0, 2, 3).reshape(2, D_MODEL, D_MODEL),
            "w_ple": g["w_ple"].transpose(1, 2, 0, 3).reshape(2, PLE_DIM, D_MODEL),
        }

    small_shapes = [P[n].shape for n, _ in SHARDED_SMALL]
    small = _chip_allgather(_pack([P[n] for n, _ in SHARDED_SMALL], SMALL_W, SMALL_ROWS), "ag_small")
    parts = [_unpack(small[j], small_shapes) for j in range(N_CHIPS)]
    full = {n: jnp.concatenate([parts[j][i] for j in range(N_CHIPS)], axis=ax)
            for i, (n, ax) in enumerate(SHARDED_SMALL)}
    W = {n: P[n] for n in REPLICATED}
    W["pool_w"] = P["pool_w"][0]
    W["final_norm"] = P["final_norm"].reshape(1, D_MODEL)
    W["mix_norm_o"] = full["mix_norm_o"]
    W["conv_qkv_o"] = full["conv_qkv_o"][0]
    W["ffn_conv"] = full["ffn_conv"]
    W["w_in_e"] = gathered["w_in_e"].reshape(N_CHIPS, D_MODEL, IN_E_SHARD)
    W["w_out_e"] = gathered["w_out_e"].reshape(D_MODEL, D_MODEL)
    return W, token, later


def _chip_major_w_in_o(g):
    shard = ODD_IN // N_CHIPS
    return jnp.stack([g[:, j * shard:(j + 1) * shard] for j in range(N_CHIPS)])


def _reduce_begin(grads, ids, tag, travel_later):
    keys = list(grads)
    pieces = [g.reshape(N_CHIPS, 2, g.shape[1] // 2, g.shape[2]) for g in grads.values()]
    from_sibling = _sibling_swap_many(pieces, f"rs_sibling_swap_{tag}")
    sums = [_chip_sums_bf16(g, a, ids, f"rs_chip_sums_{tag}{i}") for i, (g, a) in enumerate(zip(pieces, from_sibling))]
    state = dict(keys=keys, pieces=pieces, from_sibling=from_sibling, ids=ids, tag=tag, token=None)
    if travel_later:
        state["flight"] = _chip_scatter_start(sums, f"rs_scatter_start_{tag}")
        state["token"] = state["flight"][-1]
    else:
        state["from_chips"] = _chip_scatter_many(sums, f"rs_chip_scatter_{tag}")
    return state


def _reduce_end(state, after=None):
    tag, ids = state["tag"], state["ids"]
    if "flight" in state:
        send_sems, recv_sems, sums, lands, _ = state["flight"]
        from_chips = _chip_scatter_wait(send_sems, recv_sems, sums, lands, after, f"rs_scatter_wait_{tag}")
    else:
        from_chips = state["from_chips"]
    halves = [_total_half(g, a, b, ids, f"rs_total_{tag}{i}")
              for i, (g, a, b) in enumerate(zip(state["pieces"], state["from_sibling"], from_chips))]
    theirs = _sibling_send_many(halves, f"rs_sibling_send_{tag}")
    return {k: (h, t) for k, h, t in zip(state["keys"], halves, theirs)}


def kernel(x, p, mix_norm_e, w_in_e, pool_w, pool_scale, w_out_e, mix_norm_o, w_in_o, conv_qkv_o, a_log_o, dt_bias_o, gdn_norm_o, w_out_o, ffn_norm, w_up, ffn_conv, w_down, ple_norm, w_ple_gate, w_ple, final_norm, loss_target, m_mix_norm_e, m_w_in_e, m_pool_w, m_pool_scale, m_w_out_e, m_mix_norm_o, m_w_in_o, m_conv_qkv_o, m_a_log_o, m_dt_bias_o, m_gdn_norm_o, m_w_out_o, m_ffn_norm, m_w_up, m_ffn_conv, m_w_down, m_ple_norm, m_w_ple_gate, m_w_ple, m_final_norm, v_mix_norm_e, v_w_in_e, v_pool_w, v_pool_scale, v_w_out_e, v_mix_norm_o, v_w_in_o, v_conv_qkv_o, v_a_log_o, v_dt_bias_o, v_gdn_norm_o, v_w_out_o, v_ffn_norm, v_w_up, v_ffn_conv, v_w_down, v_ple_norm, v_w_ple_gate, v_w_ple, v_final_norm):
    args = locals()
    P = {n: args[n] for n in WEIGHT_ORDER}
    M = {n: args["m_" + n] for n in WEIGHT_ORDER}
    V = {n: args["v_" + n] for n in WEIGHT_ORDER}

    W, token, later_weights = _gather_weights(P)
    T = x.shape[1]
    ids = _where_ids()
    early = {}

    def on_grads(stage, grads):
        early[stage] = _reduce_begin(grads, ids, stage, travel_later=True)
        return early[stage]["token"]

    sq, grad_x, G = _local_step(x.reshape(T, D_MODEL), p.reshape(2, T, PLE_DIM), loss_target.reshape(T, D_MODEL), W,
                                token, later_weights, on_grads)
    last = _reduce_begin({("w_in_e", 0): G["w_in_e"], ("w_out_e", 0): G["w_out_e"]}, ids, "first_mixer",
                         travel_later=False)
    reduced = _reduce_end(last)
    for state in early.values():
        reduced.update(_reduce_end(state, after=grad_x))
    out = {}
    for n in BIG:
        halves = [reduced[(n, l)] for l in range(P[n].shape[0])]
        out[n] = _adamw_halves(P[n], M[n], V[n], [h[0] for h in halves], [h[1] for h in halves], ids, f"adamw_{n}")

    small_full = {n: G[n] for n in REPLICATED}
    small_full["pool_w"] = G["pool_w"][None]
    small_full["final_norm"] = G["final_norm"].reshape(D_MODEL)
    small_full["mix_norm_o"] = G["mix_norm_o"]
    small_full["conv_qkv_o"] = G["conv_qkv_o"][None]
    small_full["ffn_conv"] = G["ffn_conv"]
    small_names = REPLICATED + tuple(n for n, _ in SHARDED_SMALL)
    summed = _small_allreduce(_pack([small_full[n] for n in small_names] + [sq.reshape(1)], SMALL_W, 8), "ar_small")
    *g_list, sq_total = _unpack(summed, [small_full[n].shape for n in small_names] + [(1,)])
    g_small = dict(zip(small_names, g_list))
    chip = 2 * lax.axis_index("x") + lax.axis_index("y")
    for n, ax in SHARDED_SMALL:
        width = P[n].shape[ax]
        g_small[n] = lax.dynamic_slice_in_dim(g_small[n], chip * width, width, axis=ax)

    def pack_small(D):
        return _pack([D[n] for n in small_names], SMALL_W, RS_ROWS)[None]

    g_pack = pack_small(g_small)
    upd = _adamw(pack_small(P), g_pack, pack_small(M), pack_small(V), "adamw_small")
    shapes = [P[n].shape for n in small_names]
    for n, *vals in zip(small_names, *[_unpack(a[0], shapes) for a in (g_pack,) + tuple(upd)]):
        out[n] = tuple(vals)

    loss = (0.5 / D_MODEL) * sq_total[0]
    return (loss, grad_x[None]) + tuple(out[n][i] for i in range(4) for n in WEIGHT_ORDER)
```

```python
import functools

import jax
import jax.numpy as jnp
from jax import lax
from jax.experimental import pallas as pl
from jax.experimental.pallas import tpu as pltpu

F32 = jnp.float32
BF16 = jnp.bfloat16

D_MODEL = 1024
PLE_DIM = 256
POOL_WIDTH = 512
POOL_WINDOWS = (2, 4, 8, 16)
POOL_GROUP_DIM = 128
SB_HEADS = 8
SB_HEAD_DIM = 64
GDN_HEADS = 8
GDN_HEAD_DIM = 128
GDN_CONV = 4
GDN_CHUNK = 64
FFN_DIM = 2816
FFN_CONV = 3
EPS = 1e-6
ODD_IN = 4 * D_MODEL + 2 * GDN_HEADS
ODD_IN_PAD = 33 * 128
ADAM_LR, ADAM_B1, ADAM_B2, ADAM_EPS, ADAM_WD, ADAM_STEP = 0.001, 0.9, 0.999, 1e-08, 0.01, 10

LANE = 128
VMEM_LIMIT = 56 * 1024 * 1024

N_CHIPS = 4
N_DEV = 8


def _cp(sem=None):
    return pltpu.CompilerParams(dimension_semantics=sem, vmem_limit_bytes=VMEM_LIMIT)


def _tile(n, pref):
    if n <= pref:
        return n
    best = None
    for t in range(LANE, pref + 1, LANE):
        if n % t == 0:
            best = t
    assert best is not None, (n, pref)
    return best


_DIMS = {"nn": (((1,), (0,)), ((), ())), "nt": (((1,), (1,)), ((), ())), "tn": (((0,), (0,)), ((), ()))}
_BDIMS = {"nn": (((2,), (1,)), ((0,), (0,))), "nt": (((2,), (2,)), ((0,), (0,))), "tn": (((1,), (1,)), ((0,), (0,)))}


def _dims(mode, ndim):
    return (_BDIMS if ndim == 3 else _DIMS)[mode]


def _dot(a, b, mode="nn"):
    return lax.dot_general(a.astype(BF16), b.astype(BF16), _dims(mode, a.ndim), preferred_element_type=F32)


def _bdot(a, b, mode="nn"):
    return lax.dot_general(a.astype(BF16), b.astype(BF16), _BDIMS[mode], preferred_element_type=F32)


def _split2(x):
    hi = x.astype(BF16)
    lo = (x - hi.astype(F32)).astype(BF16)
    return hi, lo


def _split3(x):
    hi = x.astype(BF16)
    r = x - hi.astype(F32)
    mid = r.astype(BF16)
    lo = (r - mid.astype(F32)).astype(BF16)
    return hi, mid, lo


def _dot_x01(x, m01, mode="nn"):
    hi, lo = _split2(x)
    return (lax.dot_general(hi, m01, _DIMS[mode], preferred_element_type=F32)
            + lax.dot_general(lo, m01, _DIMS[mode], preferred_element_type=F32))


def _dot3_raw(a, b, mode):
    ah, al = _split2(a)
    bh, bl = _split2(b)
    d = _dims(mode, a.ndim)
    return (lax.dot_general(ah, bh, d, preferred_element_type=F32)
            + lax.dot_general(ah, bl, d, preferred_element_type=F32)
            + lax.dot_general(al, bh, d, preferred_element_type=F32))


@jax.custom_vjp
def _dot3(a, b):
    return _dot3_raw(a, b, "nn")


def _dot3_fwd(a, b):
    return _dot3_raw(a, b, "nn"), (a, b)


def _dot3_bwd(res, g):
    a, b = res
    return _dot(g, b, "nt"), _dot(a, g, "tn")


_dot3.defvjp(_dot3_fwd, _dot3_bwd)


@jax.custom_vjp
def _dot1_nt(a, b):
    return _dot(a, b, "nt")


def _dot1_nt_fwd(a, b):
    return _dot(a, b, "nt"), (a, b)


def _dot1_nt_bwd(res, g):
    a, b = res
    return _dot(g, b, "nn"), _dot(g, a, "tn")


_dot1_nt.defvjp(_dot1_nt_fwd, _dot1_nt_bwd)


def _m01_left_raw(m, x):
    d = _dims("nn", x.ndim)
    if x.ndim == 3:
        m = jnp.broadcast_to(m, (x.shape[0],) + m.shape)
    p0, p1, p2 = _split3(x)
    return (lax.dot_general(m, p0, d, preferred_element_type=F32)
            + lax.dot_general(m, p1, d, preferred_element_type=F32)
            + lax.dot_general(m, p2, d, preferred_element_type=F32))


@jax.custom_vjp
def _m01_left(m, mt, x):
    return _m01_left_raw(m, x)


def _m01_left_fwd(m, mt, x):
    return _m01_left_raw(m, x), (m, mt)


def _m01_left_bwd(res, g):
    m, mt = res
    return jnp.zeros_like(m), jnp.zeros_like(mt), _m01_left_raw(mt, g)


_m01_left.defvjp(_m01_left_fwd, _m01_left_bwd)


def _softplus(x):
    return jnp.maximum(x, 0.0) + jnp.log(1.0 + jnp.exp(-jnp.abs(x)))


def _sigmoid(x):
    return 0.5 * jnp.tanh(0.5 * x) + 0.5


def _silu(x):
    return x * _sigmoid(x)


def _dsilu(x):
    s = _sigmoid(x)
    return s * (1.0 + x * (1.0 - s))


def _cols_of(n_blocks_per_part, *fixed):
    return lambda r, c: (c // n_blocks_per_part,) + fixed + (r, c % n_blocks_per_part)


def _rows_of(n_blocks_per_part, *fixed):
    return lambda r, c: (r // n_blocks_per_part,) + fixed + (r % n_blocks_per_part, c)


def _layer_of(layer):
    return lambda r, c: (layer, r, c)


def _mm(a, b, mode, name, out_dtype=F32, res=None, tm=1024, tn=512, tk=1024,
        dims=None, a_view=None, b_view=None, o_view=None, out_shape=None):
    if dims is None:
        if mode == "nn":
            (M, K), (K2, N) = a.shape, b.shape
        elif mode == "nt":
            (M, K), (N, K2) = a.shape, b.shape
        else:
            (K, M), (K2, N) = a.shape, b.shape
        assert K == K2, (name, a.shape, b.shape)
    else:
        M, N, K = dims
    tm, tn, tk = _tile(M, tm), _tile(N, tn), _tile(K, tk)
    nk = K // tk

    def spec(arr, blk, view, rc):
        view = view or (lambda r, c: (r, c))
        return pl.BlockSpec((None,) * (arr.ndim - 2) + blk, lambda i, j, k: view(*rc(i, j, k)))

    if mode == "tn":
        a_spec = spec(a, (tk, tm), a_view, lambda i, j, k: (k, i))
    else:
        a_spec = spec(a, (tm, tk), a_view, lambda i, j, k: (i, k))
    if mode == "nt":
        b_spec = spec(b, (tn, tk), b_view, lambda i, j, k: (j, k))
    else:
        b_spec = spec(b, (tk, tn), b_view, lambda i, j, k: (k, j))
    out_shape = out_shape or (M, N)
    o_spec = pl.BlockSpec((None,) * (len(out_shape) - 2) + (tm, tn),
                          lambda i, j, k: (o_view or (lambda r, c: (r, c)))(i, j))
    has_res = res is not None
    assert not (has_res and o_view), name

    def body(*refs):
        a_ref, b_ref = refs[:2]
        r_ref = refs[2] if has_res else None
        o_ref = refs[3] if has_res else refs[2]

        def finish(r):
            if has_res:
                r = r + r_ref[...]
            o_ref[...] = r.astype(out_dtype)

        if nk == 1:
            finish(_dot(a_ref[...], b_ref[...], mode))
            return
        acc = refs[-1]
        k = pl.program_id(2)

        @pl.when(k == 0)
        def _():
            acc[...] = jnp.zeros_like(acc)

        acc[...] += _dot(a_ref[...], b_ref[...], mode)

        @pl.when(k == nk - 1)
        def _():
            finish(acc[...])

    ins = [a, b] + ([res] if has_res else [])
    in_specs = [a_spec, b_spec] + ([o_spec] if has_res else [])
    return pl.pallas_call(
        body, name=name, grid=(M // tm, N // tn, nk),
        in_specs=in_specs, out_specs=o_spec,
        out_shape=jax.ShapeDtypeStruct(out_shape, out_dtype),
        scratch_shapes=[pltpu.VMEM((tm, tn), F32)] if nk > 1 else [],
        compiler_params=_cp(("parallel", "parallel", "arbitrary")),
    )(*ins)


def _rms_fwd(x, gain, name, after=None):
    T, D = x.shape
    tt = _tile(T, 512)

    def body(x_ref, g_ref, *rest):
        o_ref = rest[-1]
        xv = x_ref[...]
        r = lax.rsqrt(jnp.mean(xv * xv, axis=-1, keepdims=True) + EPS)
        o_ref[...] = (xv * r * g_ref[...]).astype(BF16)

    ordered = [] if after is None else [after]
    return pl.pallas_call(
        body, name=name, grid=(T // tt,),
        in_specs=[pl.BlockSpec((tt, D), lambda i: (i, 0)), pl.BlockSpec((1, D), lambda i: (0, 0))]
        + [pl.BlockSpec((8, LANE), lambda i: (0, 0)) for _ in ordered],
        out_specs=pl.BlockSpec((tt, D), lambda i: (i, 0)),
        out_shape=jax.ShapeDtypeStruct((T, D), BF16),
        compiler_params=_cp(("parallel",)),
    )(x, gain, *ordered)


def _rms_bwd(x, gain, dh, dres, name):
    T, D = x.shape
    tt = _tile(T, 512)

    def body(x_ref, g_ref, dh_ref, dr_ref, dx_ref, dg_ref):
        i = pl.program_id(0)
        xv = x_ref[...]
        dy = dh_ref[...].astype(F32)
        r = lax.rsqrt(jnp.mean(xv * xv, axis=-1, keepdims=True) + EPS)
        xn = xv * r
        gdy = dy * g_ref[...]
        dx = r * (gdy - xn * jnp.mean(gdy * xn, axis=-1, keepdims=True))
        dx_ref[...] = dr_ref[...] + dx

        @pl.when(i == 0)
        def _():
            dg_ref[...] = jnp.zeros_like(dg_ref)

        dg_ref[...] += jnp.sum(dy * xn, axis=0, keepdims=True)

    row = pl.BlockSpec((tt, D), lambda i: (i, 0))
    vec = pl.BlockSpec((1, D), lambda i: (0, 0))
    return pl.pallas_call(
        body, name=name, grid=(T // tt,),
        in_specs=[row, vec, row, row], out_specs=[row, vec],
        out_shape=[jax.ShapeDtypeStruct((T, D), F32), jax.ShapeDtypeStruct((1, D), F32)],
        compiler_params=_cp(("arbitrary",)),
    )(x, gain, dh, dres)


def _final_loss(x, gain, target, name):
    T, D = x.shape
    tt = _tile(T, 512)

    def body(x_ref, g_ref, t_ref, l_ref, dx_ref, dg_ref):
        i = pl.program_id(0)
        xv = x_ref[...]
        r = lax.rsqrt(jnp.mean(xv * xv, axis=-1, keepdims=True) + EPS)
        xn = xv * r
        err = xn * g_ref[...] - t_ref[...]
        dy = err * (1.0 / D)
        gdy = dy * g_ref[...]
        dx_ref[...] = r * (gdy - xn * jnp.mean(gdy * xn, axis=-1, keepdims=True))

        @pl.when(i == 0)
        def _():
            dg_ref[...] = jnp.zeros_like(dg_ref)
            l_ref[...] = jnp.zeros_like(l_ref)

        dg_ref[...] += jnp.sum(dy * xn, axis=0, keepdims=True)
        l_ref[...] += jnp.sum(jnp.sum(err * err, axis=1, keepdims=True), axis=0, keepdims=True)

    row = pl.BlockSpec((tt, D), lambda i: (i, 0))
    vec = pl.BlockSpec((1, D), lambda i: (0, 0))
    return pl.pallas_call(
        body, name=name, grid=(T // tt,),
        in_specs=[row, vec, row],
        out_specs=[pl.BlockSpec((8, LANE), lambda i: (0, 0)), row, vec],
        out_shape=[jax.ShapeDtypeStruct((8, LANE), F32), jax.ShapeDtypeStruct((T, D), F32),
                   jax.ShapeDtypeStruct((1, D), F32)],
        compiler_params=_cp(("arbitrary",)),
    )(x, gain, target)


def _shift_down(x, i, t_idx):
    if i == 0:
        return x
    return jnp.where(t_idx >= i, pltpu.roll(x, i, 0), 0.0)


def _shift_up(x, i, t_idx):
    if i == 0:
        return x
    n = x.shape[0]
    return jnp.where(t_idx < n - i, pltpu.roll(x, n - i, 0), 0.0)


def _pool_select(g, vals):
    out = vals[-1]
    for gi in range(len(vals) - 2, -1, -1):
        out = jnp.where(g == gi, vals[gi], out)
    return out


def _pool_y(u, g, t_idx):
    s1 = u + _shift_down(u, 1, t_idx)
    s2 = s1 + _shift_down(s1, 2, t_idx)
    s3 = s2 + _shift_down(s2, 4, t_idx)
    s4 = s3 + _shift_down(s3, 8, t_idx)
    ws = _pool_select(g, [s1, s2, s3, s4])
    win = _pool_select(g, [jnp.float32(w) for w in POOL_WINDOWS])
    cnt = jnp.minimum(t_idx.astype(F32) + 1.0, win)
    return ws / cnt - u, cnt


def _pool_fwd(proj, pool_w, pool_scale):
    T = proj.shape[0]
    G, C = len(POOL_WINDOWS), POOL_GROUP_DIM

    def body(u_ref, w_ref, s_ref, o_ref):
        g = pl.program_id(0)
        t_idx = lax.broadcasted_iota(jnp.int32, (T, C), 0)
        y, _ = _pool_y(u_ref[...], g, t_idx)
        o_ref[...] = _dot(y, w_ref[0]) * s_ref[...]

    return pl.pallas_call(
        body, name="pool_fwd", grid=(G,),
        in_specs=[pl.BlockSpec((T, C), lambda g: (0, g)), pl.BlockSpec((1, C, C), lambda g: (g, 0, 0)),
                  pl.BlockSpec((1, C), lambda g: (0, g))],
        out_specs=pl.BlockSpec((T, C), lambda g: (0, g)),
        out_shape=jax.ShapeDtypeStruct((T, G * C), F32),
        compiler_params=_cp(("parallel",)),
    )(proj, pool_w, pool_scale)


def _pool_bwd(proj, pool_w, pool_scale, dmix):
    T = proj.shape[0]
    G, C = len(POOL_WINDOWS), POOL_GROUP_DIM

    def body(u_ref, w_ref, s_ref, do_ref, du_ref, dw_ref, ds_ref):
        g = pl.program_id(0)
        t_idx = lax.broadcasted_iota(jnp.int32, (T, C), 0)
        y, cnt = _pool_y(u_ref[...], g, t_idx)
        w = w_ref[0]
        dout = do_ref[...]
        ds_ref[...] = jnp.sum(dout * _dot(y, w), axis=0, keepdims=True)
        dy2 = dout * s_ref[...]
        dw_ref[0] = _dot(y, dy2, "tn")
        dy = _dot(dy2, w, "nt")
        dz = dy / cnt
        r1 = dz + _shift_up(dz, 1, t_idx)
        r2 = r1 + _shift_up(r1, 2, t_idx)
        r3 = r2 + _shift_up(r2, 4, t_idx)
        r4 = r3 + _shift_up(r3, 8, t_idx)
        du_ref[...] = _pool_select(g, [r1, r2, r3, r4]) - dy

    col = pl.BlockSpec((T, C), lambda g: (0, g))
    return pl.pallas_call(
        body, name="pool_bwd", grid=(G,),
        in_specs=[col, pl.BlockSpec((1, C, C), lambda g: (g, 0, 0)), pl.BlockSpec((1, C), lambda g: (0, g)), col],
        out_specs=[col, pl.BlockSpec((1, C, C), lambda g: (g, 0, 0)), pl.BlockSpec((1, C), lambda g: (0, g))],
        out_shape=[jax.ShapeDtypeStruct((T, G * C), F32), jax.ShapeDtypeStruct((G, C, C), F32),
                   jax.ShapeDtypeStruct((1, G * C), F32)],
        compiler_params=_cp(("parallel",)),
    )(proj, pool_w, pool_scale, dmix)


SB_SCALE = SB_HEAD_DIM ** -0.5
SB_PASS_SIZES = (4, 2, 1)
SB_PASS_SIZES_BWD = (2, 1)


def _sb_tile_logits(qb, kblk, valid):
    z = _dot(qb, kblk, "nt")
    sp = _softplus(z)
    l1m = -sp
    if valid is not None:
        l1m = jnp.where(valid, l1m, 0.0)
    return z, sp, l1m


SB_PAIR = LANE // SB_HEAD_DIM
SB_Q0 = POOL_WIDTH // LANE
SB_NB = SB_HEADS // SB_PAIR


def _sb_head_masks():
    lane = lax.broadcasted_iota(jnp.int32, (1, LANE), 1)
    return [(lane // SB_HEAD_DIM == h).astype(F32) for h in range(SB_PAIR)]


def _sb_fwd(proj):
    T = proj.shape[0]
    B = _tile(T, 256)
    nq = T // B

    def body(q_ref, k_ref, v_ref, o_ref, l_ref, k_bf, v_bf):
        qi = pl.program_id(1)

        @pl.when(qi == 0)
        def _():
            k_bf[...] = k_ref[...].astype(BF16)
            v_bf[...] = v_ref[...].astype(BF16)

        masks = _sb_head_masks()
        q_all = q_ref[...]
        qbs = [(q_all * (m * SB_SCALE)).astype(BF16) for m in masks]
        row = lax.broadcasted_iota(jnp.int32, (B, B), 0)
        col = lax.broadcasted_iota(jnp.int32, (B, B), 1)
        later = (row > col).astype(BF16)

        def tiles(kbs, state, valid):
            ksl = [pl.ds(pl.multiple_of(kb * B, B), B) for kb in kbs]
            kblks = [k_bf[ks, :] for ks in ksl]
            logits = [[_sb_tile_logits(qb, kblk, valid) for kblk in kblks] for qb in qbs]
            within = [[_dot_x01(l1m, later) for _, _, l1m in lg] for lg in logits]
            sums = [[jnp.sum(l1m, axis=1, keepdims=True) for _, _, l1m in lg] for lg in logits]
            out = []
            for h, (carry, acc) in enumerate(state):
                for (z, sp, _), rc, s, ks in zip(logits[h], within[h], sums[h], ksl):
                    a = jnp.exp(z - sp + rc + carry)
                    if valid is not None:
                        a = jnp.where(valid, a, 0.0)
                    acc = acc + _dot(a, v_bf[ks, :])
                    carry = carry + s
                out.append((carry, acc))
            return tuple(out)

        state = tiles([qi], ((jnp.zeros((B, 1), F32), jnp.zeros((B, LANE), F32)),) * SB_PAIR, col < row)
        left = qi
        for size in SB_PASS_SIZES:
            n_pass = left // size
            state = lax.fori_loop(
                0, n_pass, lambda i, c, left=left, size=size: tiles([left - 1 - size * i - u for u in range(size)],
                                                                     c, None), state)
            left = left - n_pass * size
        o_ref[...] = sum(acc * m for (_, acc), m in zip(state, masks))
        for h, (carry, _) in enumerate(state):
            l_ref[h] = carry

    return pl.pallas_call(
        body, name="sb_fwd", grid=(SB_NB, nq),
        in_specs=[pl.BlockSpec((B, LANE), lambda hp, i: (i, SB_Q0 + hp)),
                  pl.BlockSpec((T, LANE), lambda hp, i: (0, SB_Q0 + SB_NB + hp)),
                  pl.BlockSpec((T, LANE), lambda hp, i: (0, SB_Q0 + 2 * SB_NB + hp))],
        out_specs=[pl.BlockSpec((B, LANE), lambda hp, i: (i, hp)),
                   pl.BlockSpec((SB_PAIR, B, 1), lambda hp, i: (hp, i, 0))],
        out_shape=[jax.ShapeDtypeStruct((T, SB_HEADS * SB_HEAD_DIM), F32), jax.ShapeDtypeStruct((SB_HEADS, T, 1), F32)],
        scratch_shapes=[pltpu.VMEM((T, LANE), BF16), pltpu.VMEM((T, LANE), BF16)],
        compiler_params=_cp(("parallel", "arbitrary")),
    )(proj, proj, proj)


def _sb_bwd(proj, dmix, ltot, after=None):
    T = proj.shape[0]
    B = _tile(T, 256)
    nq = T // B
    ordered = [] if after is None else [after]

    def body(q_ref, k_ref, v_ref, do_ref, l_ref, *rest):
        dq_ref, dk_ref, dv_ref, k_bf, v_bf = rest[len(ordered):]
        qi = pl.program_id(1)

        @pl.when(qi == 0)
        def _():
            k_bf[...] = k_ref[...].astype(BF16)
            v_bf[...] = v_ref[...].astype(BF16)
            dk_ref[...] = jnp.zeros_like(dk_ref)
            dv_ref[...] = jnp.zeros_like(dv_ref)

        masks = _sb_head_masks()
        q_all, do_all = q_ref[...], do_ref[...]
        qbs = [(q_all * (m * SB_SCALE)).astype(BF16) for m in masks]
        dobs = [(do_all * m).astype(BF16) for m in masks]
        ltots = [l_ref[h] for h in range(SB_PAIR)]
        row = lax.broadcasted_iota(jnp.int32, (B, B), 0)
        col = lax.broadcasted_iota(jnp.int32, (B, B), 1)
        upto = (row <= col).astype(BF16)
        before = (row < col).astype(BF16)

        def tiles(kbs, state, valid):
            ksl = [pl.ds(pl.multiple_of(kb * B, B), B) for kb in kbs]
            kblks = [k_bf[ks, :] for ks in ksl]
            vblks = [v_bf[ks, :] for ks in ksl]
            logits = [[_sb_tile_logits(qb, kblk, valid) for kblk in kblks] for qb in qbs]
            das = [[_dot(dob, vblk, "nt") for vblk in vblks] for dob in dobs]
            within = [[_dot_x01(l1m, upto) for _, _, l1m in lg] for lg in logits]
            avals, es, Ps = [], [], []
            for h, (P, _, _) in enumerate(state):
                a_h, e_h = [], []
                for (z, sp, l1m), pc, da in zip(logits[h], within[h], das[h]):
                    a = jnp.exp(z - sp + (ltots[h] - P - pc))
                    if valid is not None:
                        a = jnp.where(valid, a, 0.0)
                    a_h.append(a)
                    e_h.append(da * a)
                    P = P + jnp.sum(l1m, axis=1, keepdims=True)
                avals.append(a_h)
                es.append(e_h)
                Ps.append(P)
            e_within = [[_dot_x01(e, before) for e in e_h] for e_h in es]
            out = []
            for h, (_, E, dq) in enumerate(state):
                for (z, sp, _), e, ew, a, kblk, ks in zip(logits[h], es[h], e_within[h], avals[h], kblks, ksl):
                    dz = e * jnp.exp(-sp) - jnp.exp(z - sp) * (ew + E)
                    if valid is not None:
                        dz = jnp.where(valid, dz, 0.0)
                    dzb = dz.astype(BF16)
                    dq = dq + _dot(dzb, kblk)
                    dk_ref[ks, :] += _dot(dzb, qbs[h], "tn")
                    dv_ref[ks, :] += _dot(a, dobs[h], "tn")
                    E = E + jnp.sum(e, axis=1, keepdims=True)
                out.append((Ps[h], E, dq))
            return tuple(out)

        zeros1 = jnp.zeros((B, 1), F32)
        state = ((zeros1, zeros1, jnp.zeros((B, LANE), F32)),) * SB_PAIR
        done = 0
        for size in SB_PASS_SIZES_BWD:
            n_pass = (qi - done) // size
            state = lax.fori_loop(
                0, n_pass, lambda i, c, done=done, size=size: tiles([done + size * i + u for u in range(size)], c, None),
                state)
            done = done + n_pass * size
        state = tiles([qi], state, col < row)
        dq_ref[...] = sum(dq * (m * SB_SCALE) for (_, _, dq), m in zip(state, masks))

    qspec = pl.BlockSpec((B, LANE), lambda hp, i: (i, SB_Q0 + hp))
    wide = jax.ShapeDtypeStruct((T, SB_HEADS * SB_HEAD_DIM), F32)
    return pl.pallas_call(
        body, name="sb_bwd", grid=(SB_NB, nq),
        in_specs=[qspec,
                  pl.BlockSpec((T, LANE), lambda hp, i: (0, SB_Q0 + SB_NB + hp)),
                  pl.BlockSpec((T, LANE), lambda hp, i: (0, SB_Q0 + 2 * SB_NB + hp)),
                  qspec,
                  pl.BlockSpec((SB_PAIR, B, 1), lambda hp, i: (hp, i, 0))]
        + [pl.BlockSpec((8, LANE), lambda hp, i: (0, 0)) for _ in ordered],
        out_specs=[pl.BlockSpec((B, LANE), lambda hp, i: (i, hp)),
                   pl.BlockSpec((T, LANE), lambda hp, i: (0, hp)),
                   pl.BlockSpec((T, LANE), lambda hp, i: (0, hp))],
        out_shape=[wide, wide, wide],
        scratch_shapes=[pltpu.VMEM((T, LANE), BF16), pltpu.VMEM((T, LANE), BF16)],
        compiler_params=_cp(("parallel", "arbitrary")),
    )(proj, proj, proj, dmix, ltot, *ordered)


def _rows(w_ref, K):
    return [w_ref[i:i + 1, :] for i in range(K)]


def _conv(x, ws, t_idx):
    K = len(ws)
    y = ws[K - 1] * x
    for i in range(K - 1):
        y = y + ws[i] * _shift_down(x, K - 1 - i, t_idx)
    return y


def _conv_bwd(x, ws, dy, t_idx):
    K = len(ws)
    dx = ws[K - 1] * dy
    dws = []
    for i in range(K - 1):
        dx = dx + ws[i] * _shift_up(dy, K - 1 - i, t_idx)
        dws.append(jnp.sum(dy * _shift_down(x, K - 1 - i, t_idx), axis=0, keepdims=True))
    dws.append(jnp.sum(dy * x, axis=0, keepdims=True))
    return dx, dws


def _store_rows(ref, rows):
    for i, r in enumerate(rows):
        ref[i:i + 1, :] = r


CONV_ROWS = 64


def _ffn_act_fwd(up, conv_w, name):
    T = up.shape[0]
    F = FFN_DIM
    nb = F // LANE

    def body(g_ref, v_ref, wg_ref, wv_ref, o_ref):
        t_idx = lax.broadcasted_iota(jnp.int32, (T, LANE), 0)
        cg = _conv(g_ref[...].astype(F32), _rows(wg_ref, FFN_CONV), t_idx)
        cv = _conv(v_ref[...].astype(F32), _rows(wv_ref, FFN_CONV), t_idx)
        o_ref[...] = (_silu(cg) * cv).astype(BF16)

    return pl.pallas_call(
        body, name=name, grid=(nb,),
        in_specs=[pl.BlockSpec((T, LANE), lambda j: (0, j)), pl.BlockSpec((T, LANE), lambda j: (0, j + nb)),
                  pl.BlockSpec((FFN_CONV, LANE), lambda j: (0, j)),
                  pl.BlockSpec((FFN_CONV, LANE), lambda j: (0, j + nb))],
        out_specs=pl.BlockSpec((T, LANE), lambda j: (0, j)),
        out_shape=jax.ShapeDtypeStruct((T, F), BF16),
        compiler_params=_cp(("parallel",)),
    )(up, up, conv_w, conv_w)


def _ffn_act_bwd(up, conv_w, dact, name):
    T = up.shape[0]
    F = FFN_DIM
    nb = F // LANE

    K = FFN_CONV
    R = CONV_ROWS
    assert T % R == 0, T
    n_chunks = T // R
    PAD = 8

    def body(g_ref, v_ref, wg_ref, wv_ref, da_ref, dup_ref, dwg_ref, dwv_ref, xg_s, xv_s, dyg_s, dyv_s):
        zeros = jnp.zeros((PAD, LANE), F32)
        for s in (xg_s, xv_s, dyg_s, dyv_s):
            s[0:PAD, :] = zeros
            s[T + PAD:T + 2 * PAD, :] = zeros
        xg_s[PAD:T + PAD, :] = g_ref[...].astype(F32)
        xv_s[PAD:T + PAD, :] = v_ref[...].astype(F32)
        wg, wv = _rows(wg_ref, K), _rows(wv_ref, K)

        def window(ext, shift):
            if shift == 0:
                return ext[PAD:PAD + R, :]
            return pltpu.roll(ext, shift % (R + 2 * PAD), 0)[PAD:PAD + R, :]

        def forward(c, carry):
            r0 = pl.multiple_of(c * R, R)
            ge, ve = xg_s[pl.ds(r0, R + 2 * PAD), :], xv_s[pl.ds(r0, R + 2 * PAD), :]
            gw = [window(ge, K - 1 - i) for i in range(K)]
            vw = [window(ve, K - 1 - i) for i in range(K)]
            cg = sum(w * x for w, x in zip(wg, gw))
            cv = sum(w * x for w, x in zip(wv, vw))
            da = da_ref[pl.ds(r0, R), :].astype(F32)
            sg = _sigmoid(cg)
            dyg = da * cv * (sg * (1.0 + cg * (1.0 - sg)))
            dyv = da * (cg * sg)
            dyg_s[pl.ds(pl.multiple_of(r0 + PAD, PAD), R), :] = dyg
            dyv_s[pl.ds(pl.multiple_of(r0 + PAD, PAD), R), :] = dyv
            return tuple(acc + jnp.sum(dy * x, axis=0, keepdims=True)
                         for acc, (dy, x) in zip(carry, [(dyg, x) for x in gw] + [(dyv, x) for x in vw]))

        sums = lax.fori_loop(0, n_chunks, forward, (jnp.zeros((1, LANE), F32),) * (2 * K))
        _store_rows(dwg_ref, sums[:K])
        _store_rows(dwv_ref, sums[K:])

        def backward(c, carry):
            r0 = pl.multiple_of(c * R, R)
            ge, ve = dyg_s[pl.ds(r0, R + 2 * PAD), :], dyv_s[pl.ds(r0, R + 2 * PAD), :]
            dxg = sum(w * window(ge, -(K - 1 - i)) for i, w in enumerate(wg))
            dxv = sum(w * window(ve, -(K - 1 - i)) for i, w in enumerate(wv))
            dup_ref[0, pl.ds(r0, R), :] = dxg.astype(BF16)
            dup_ref[1, pl.ds(r0, R), :] = dxv.astype(BF16)
            return carry

        lax.fori_loop(0, n_chunks, backward, 0)

    col = pl.BlockSpec((T, LANE), lambda j: (0, j))
    wcol = pl.BlockSpec((FFN_CONV, LANE), lambda j: (0, j))
    return pl.pallas_call(
        body, name=name, grid=(nb,),
        in_specs=[col, pl.BlockSpec((T, LANE), lambda j: (0, j + nb)), wcol,
                  pl.BlockSpec((FFN_CONV, LANE), lambda j: (0, j + nb)), col],
        out_specs=[pl.BlockSpec((2, T, LANE), lambda j: (0, 0, j)), wcol, wcol],
        out_shape=[jax.ShapeDtypeStruct((2, T, F), BF16),
                   jax.ShapeDtypeStruct((FFN_CONV, F), F32), jax.ShapeDtypeStruct((FFN_CONV, F), F32)],
        scratch_shapes=[pltpu.VMEM((T + 2 * PAD, LANE), F32)] * 4,
        compiler_params=_cp(("parallel",)),
    )(up, up, conv_w, conv_w, dact)


N_QK_BLOCKS = 2 * GDN_HEADS


def _gdn_pre_fwd(proj, conv_w):
    T = proj.shape[0]
    nb = 3 * GDN_HEADS

    def body(x_ref, w_ref, o_ref):
        j = pl.program_id(0)
        t_idx = lax.broadcasted_iota(jnp.int32, (T, LANE), 0)
        s = _silu(_conv(x_ref[...], _rows(w_ref, GDN_CONV), t_idx))
        rn = lax.rsqrt(jnp.sum(s * s, axis=-1, keepdims=True) + EPS)
        o_ref[...] = s * jnp.where(j < N_QK_BLOCKS, rn, 1.0)

    return pl.pallas_call(
        body, name="gdn_pre_fwd", grid=(nb,),
        in_specs=[pl.BlockSpec((T, LANE), lambda j: (0, j)), pl.BlockSpec((GDN_CONV, LANE), lambda j: (0, j))],
        out_specs=pl.BlockSpec((T, LANE), lambda j: (0, j)),
        out_shape=jax.ShapeDtypeStruct((T, nb * LANE), F32),
        compiler_params=_cp(("parallel",)),
    )(proj, conv_w)


def _gdn_pre_bwd(proj, conv_w, dout):
    T = proj.shape[0]
    nb = 3 * GDN_HEADS
    H = GDN_HEADS

    def body(x_ref, w_ref, do_ref, dx_ref, dw_ref):
        j = pl.program_id(0)
        t_idx = lax.broadcasted_iota(jnp.int32, (T, LANE), 0)
        x, w = x_ref[...], _rows(w_ref, GDN_CONV)
        c = _conv(x, w, t_idx)
        s = _silu(c)
        rn = lax.rsqrt(jnp.sum(s * s, axis=-1, keepdims=True) + EPS)
        do = do_ref[...]
        y = s * rn
        ds_normed = rn * (do - y * jnp.sum(do * y, axis=-1, keepdims=True))
        ds = jnp.where(j < N_QK_BLOCKS, ds_normed, do)
        dx, dw = _conv_bwd(x, w, ds * _dsilu(c), t_idx)
        dx_ref[...] = dx.astype(BF16)
        _store_rows(dw_ref, dw)

    col = pl.BlockSpec((T, LANE), lambda j: (0, j))
    wcol = pl.BlockSpec((GDN_CONV, LANE), lambda j: (0, j))
    return pl.pallas_call(
        body, name="gdn_pre_bwd", grid=(nb,),
        in_specs=[col, wcol, pl.BlockSpec((None, None, T, LANE), lambda j: (j // H, j % H, 0, 0))],
        out_specs=[col, wcol],
        out_shape=[jax.ShapeDtypeStruct((T, nb * LANE), BF16), jax.ShapeDtypeStruct((GDN_CONV, nb * LANE), F32)],
        compiler_params=_cp(("parallel",)),
    )(proj, conv_w, dout)


def _gdn_consts():
    C = GDN_CHUNK
    r = lax.broadcasted_iota(jnp.int32, (C, C), 0)
    c = lax.broadcasted_iota(jnp.int32, (C, C), 1)
    return dict(incl=r >= c, strict=r > c, eye=(r == c).astype(F32),
                low=(r >= c).astype(BF16), up=(r <= c).astype(BF16), ones=jnp.ones((C, C), BF16))


def _unit_lower_inverse_raw(a_mat, eye):
    inv = eye - a_mat
    pw = _dot3_raw(a_mat, a_mat, "nn")
    n_factors = a_mat.shape[-1].bit_length() - 2
    for f in range(n_factors):
        inv = inv + _dot3_raw(inv, pw, "nn")
        if f < n_factors - 1:
            pw = _dot3_raw(pw, pw, "nn")
    return inv


@jax.custom_vjp
def _unit_lower_inverse(a_mat, eye):
    return _unit_lower_inverse_raw(a_mat, eye)


def _unit_lower_inverse_fwd(a_mat, eye):
    inv = _unit_lower_inverse_raw(a_mat, eye)
    return inv, (inv, eye)


def _unit_lower_inverse_bwd(res, g):
    inv, eye = res
    return -_dot(_dot(inv, g, "tn"), inv, "nt"), jnp.zeros_like(eye)


_unit_lower_inverse.defvjp(_unit_lower_inverse_fwd, _unit_lower_inverse_bwd)


def _gdn_prep_chunk(q, k, v, b, a, alog, dtb, cs):
    n, C, dk = q.shape
    beta = _sigmoid(b)
    g = -jnp.exp(alog) * _softplus(a + dtb)
    g_sq = jnp.broadcast_to(g, (n, C, C))
    g_wide = jnp.broadcast_to(g, (n, C, dk))
    gc_i = _m01_left(cs["low"], cs["up"], g_sq)
    gc_j = _m01_left(cs["ones"], cs["ones"], g_sq * cs["up"].astype(F32))
    gc_wide = _m01_left(cs["low"], cs["up"], g_wide)
    gl_wide = _m01_left(cs["ones"], cs["ones"], g_wide)
    decay = jnp.where(cs["incl"], jnp.exp(jnp.where(cs["incl"], gc_i - gc_j, 0.0)), 0.0)
    egc = jnp.exp(gc_wide)
    qs = q * (dk ** -0.5)
    k_beta = k * beta
    a_mat = jnp.where(cs["strict"], _dot1_nt(k_beta, k) * decay, 0.0)
    inv = _unit_lower_inverse(a_mat, cs["eye"])
    u = _dot3(inv, v * beta)
    w = _dot3(inv, k_beta * egc)
    qk = _dot1_nt(qs, k) * decay
    q_dec = qs * egc
    k_dec = k * jnp.exp(gl_wide - gc_wide)
    g_last = jnp.exp(gl_wide)[:, 0:8, :]
    return qk, u, w, q_dec, k_dec, g_last


GDN_PREP_CHUNKS = 8
GDN_BA_BLOCK = 4 * D_MODEL // LANE


def _gdn_prep_specs(T):
    C, dk = GDN_CHUNK, GDN_HEAD_DIM
    npc = min(GDN_PREP_CHUNKS, T // C)
    tc = npc * C
    H = GDN_HEADS
    in_specs = [pl.BlockSpec((tc, dk), lambda i, h: (i, h)),
                pl.BlockSpec((tc, dk), lambda i, h: (i, H + h)),
                pl.BlockSpec((tc, dk), lambda i, h: (i, 2 * H + h)),
                pl.BlockSpec((tc, dk), lambda i, h: (i, GDN_BA_BLOCK)),
                pl.BlockSpec((8, dk), lambda i, h: (0, 0))]
    xs_specs = [pl.BlockSpec((1, tc, C), lambda i, h: (h, i, 0)),
                pl.BlockSpec((1, tc, dk), lambda i, h: (h, i, 0)),
                pl.BlockSpec((1, tc, dk), lambda i, h: (h, i, 0)),
                pl.BlockSpec((1, tc, dk), lambda i, h: (h, i, 0)),
                pl.BlockSpec((1, tc, dk), lambda i, h: (h, i, 0)),
                pl.BlockSpec((1, npc * 8, dk), lambda i, h: (h, i, 0))]
    xs_shapes = [jax.ShapeDtypeStruct((H, T, C), F32)] + [jax.ShapeDtypeStruct((H, T, dk), F32)] * 4 + [
        jax.ShapeDtypeStruct((H, 8 * T // C, dk), F32)]
    return npc, tc, in_specs, xs_specs, xs_shapes


def _lane_pick(x, lane, j):
    return jnp.sum(jnp.where(lane == j, x, 0.0), axis=1, keepdims=True)


def _gdn_head_gates(ba_ref, gates_ref, h, npc):
    lane = lax.broadcasted_iota(jnp.int32, (1, GDN_HEAD_DIM), 1)
    ba = ba_ref[...]
    b = _lane_pick(ba, lane, h).reshape(npc, GDN_CHUNK, 1)
    a = _lane_pick(ba, lane, GDN_HEADS + h).reshape(npc, GDN_CHUNK, 1)
    return b, a, _lane_pick(gates_ref[0:1, :], lane, h), _lane_pick(gates_ref[1:2, :], lane, h), lane


def _gdn_prep_fwd(qkv, proj, gates):
    T = qkv.shape[0]
    C = GDN_CHUNK
    npc, tc, in_specs, xs_specs, xs_shapes = _gdn_prep_specs(T)

    def body(q_ref, k_ref, v_ref, ba_ref, gates_ref, qk_ref, u_ref, w_ref, qd_ref, kd_ref, gl_ref):
        cs = _gdn_consts()
        b, a, alog, dtb, _ = _gdn_head_gates(ba_ref, gates_ref, pl.program_id(1), npc)

        def chunks(val):
            return val.reshape(npc, C, val.shape[-1])

        outs = _gdn_prep_chunk(chunks(q_ref[...]), chunks(k_ref[...]), chunks(v_ref[...]), b, a, alog, dtb, cs)
        for ref, val in zip((qk_ref, u_ref, w_ref, qd_ref, kd_ref), outs[:5]):
            ref[0] = val.reshape(tc, val.shape[-1])
        gl_ref[0] = outs[5].reshape(npc * 8, outs[5].shape[-1])

    return pl.pallas_call(
        body, name="gdn_prep_fwd", grid=(T // tc, GDN_HEADS),
        in_specs=in_specs, out_specs=xs_specs, out_shape=xs_shapes,
        compiler_params=_cp(("parallel", "parallel")),
    )(qkv, qkv, qkv, proj, gates)


def _gdn_prep_bwd(qkv, proj, gates, dxs):
    T = qkv.shape[0]
    C, dk, H = GDN_CHUNK, GDN_HEAD_DIM, GDN_HEADS
    npc, tc, in_specs, xs_specs, _ = _gdn_prep_specs(T)

    def body(q_ref, k_ref, v_ref, ba_ref, gates_ref, dqk_ref, du_ref, dw_ref, dqd_ref, dkd_ref, dgl_ref,
             dqkv_ref, dba_ref, dgates_ref):
        i, h = pl.program_id(0), pl.program_id(1)
        cs = _gdn_consts()
        r8 = lax.broadcasted_iota(jnp.int32, (8, dk), 0)
        c8 = lax.broadcasted_iota(jnp.int32, (8, dk), 1)
        first = (r8 == 0) & (c8 == 0)

        @pl.when((i == 0) & (h == 0))
        def _():
            dgates_ref[...] = jnp.zeros_like(dgates_ref)

        @pl.when(h == 0)
        def _():
            dba_ref[...] = jnp.zeros_like(dba_ref)

        def chunks(val):
            return val.reshape(npc, C, val.shape[-1])

        b, a, alog, dtb, lane = _gdn_head_gates(ba_ref, gates_ref, h, npc)
        prim = (chunks(q_ref[...]), chunks(k_ref[...]), chunks(v_ref[...]), b, a, alog, dtb)
        _, vjp = jax.vjp(lambda *p: _gdn_prep_chunk(*p, cs), *prim)
        dgl = jnp.where(first, dgl_ref[0].reshape(npc, 8, dk), 0.0)
        cts = tuple(chunks(r[0]) for r in (dqk_ref, du_ref, dw_ref, dqd_ref, dkd_ref)) + (dgl,)
        dq, dkk, dv, db, da, dal, ddt = vjp(cts)
        for part, val in enumerate((dq, dkk, dv)):
            dqkv_ref[part, 0] = val.reshape(tc, dk)
        dba_ref[...] += (jnp.where(lane == h, db.reshape(tc, 1), 0.0)
                         + jnp.where(lane == H + h, da.reshape(tc, 1), 0.0))
        dgates_ref[0:1, :] += jnp.where(lane == h, dal, 0.0)
        dgates_ref[1:2, :] += jnp.where(lane == h, ddt, 0.0)

    return pl.pallas_call(
        body, name="gdn_prep_bwd", grid=(T // tc, H),
        in_specs=in_specs + xs_specs,
        out_specs=[pl.BlockSpec((3, 1, tc, dk), lambda i, h: (0, h, i, 0)), pl.BlockSpec((tc, dk), lambda i, h: (i, 0)),
                   pl.BlockSpec((8, dk), lambda i, h: (0, 0))],
        out_shape=[jax.ShapeDtypeStruct((3, H, T, dk), F32), jax.ShapeDtypeStruct((T, dk), F32),
                   jax.ShapeDtypeStruct((8, dk), F32)],
        compiler_params=_cp(("arbitrary", "arbitrary")),
    )(qkv, qkv, qkv, proj, gates, *dxs)


def _gdn_scan_specs(T):
    C, dk, H = GDN_CHUNK, GDN_HEAD_DIM, GDN_HEADS
    return [pl.BlockSpec((H, C, C), lambda n: (0, n, 0))] + [pl.BlockSpec((H, C, dk), lambda n: (0, n, 0))] * 4 + [
        pl.BlockSpec((H, 8, dk), lambda n: (0, n, 0))]


def _gdn_scan_fwd(xs):
    H, T, dk = xs[1].shape
    C = GDN_CHUNK
    n = T // C

    def body(qk_ref, u_ref, w_ref, qd_ref, kd_ref, gl_ref, o_ref, s_ref, state):
        c = pl.program_id(0)

        @pl.when(c == 0)
        def _():
            state[...] = jnp.zeros_like(state)

        S = state[...]
        s_ref[0] = S
        v_new = u_ref[...] - _bdot(w_ref[...], S)
        o_ref[...] = _bdot(qd_ref[...], S) + _bdot(qk_ref[...], v_new)
        state[...] = S * jnp.tile(gl_ref[...], (1, dk // 8, 1)) + _bdot(kd_ref[...], v_new, "tn")

    return pl.pallas_call(
        body, name="gdn_scan_fwd", grid=(n,),
        in_specs=_gdn_scan_specs(T),
        out_specs=[pl.BlockSpec((H, C, dk), lambda n: (0, n, 0)), pl.BlockSpec((1, H, dk, dk), lambda n: (n, 0, 0, 0))],
        out_shape=[jax.ShapeDtypeStruct((H, T, dk), F32), jax.ShapeDtypeStruct((n, H, dk, dk), F32)],
        scratch_shapes=[pltpu.VMEM((H, dk, dk), F32)],
        compiler_params=_cp(("arbitrary",)),
    )(*xs)


def _gdn_scan_bwd(xs, states, do):
    H, T, dk = xs[1].shape
    C = GDN_CHUNK
    n = T // C

    def rev(spec_shape, f):
        return pl.BlockSpec(spec_shape, lambda i: f(n - 1 - i))

    def body(qk_ref, u_ref, w_ref, qd_ref, kd_ref, gl_ref, s_ref, do_ref,
             dqk_ref, du_ref, dw_ref, dqd_ref, dkd_ref, dgl_ref, dstate):
        i = pl.program_id(0)

        @pl.when(i == 0)
        def _():
            dstate[...] = jnp.zeros_like(dstate)

        S = s_ref[0]
        dS = dstate[...]
        do_v = do_ref[...]
        qk, w, qd, kd = qk_ref[...], w_ref[...], qd_ref[...], kd_ref[...]
        v_new = u_ref[...] - _bdot(w, S)
        dv_new = _bdot(qk, do_v, "tn") + _bdot(kd, dS)
        dqk_ref[...] = _bdot(do_v, v_new, "nt")
        dqd_ref[...] = _bdot(do_v, S, "nt")
        dkd_ref[...] = _bdot(v_new, dS, "nt")
        du_ref[...] = dv_new
        dw_ref[...] = -_bdot(dv_new, S, "nt")
        dgl = jnp.sum(jnp.sum(S * dS, axis=2, keepdims=True), axis=1, keepdims=True)
        dgl_ref[...] = jnp.broadcast_to(dgl, dgl_ref.shape)
        dstate[...] = (dS * jnp.tile(gl_ref[...], (1, dk // 8, 1)) + _bdot(qd, do_v, "tn")
                       - _bdot(w, dv_new, "tn"))

    in_specs = [rev((H, C, C), lambda m: (0, m, 0))] + [rev((H, C, dk), lambda m: (0, m, 0))] * 4 + [
        rev((H, 8, dk), lambda m: (0, m, 0)), rev((1, H, dk, dk), lambda m: (m, 0, 0, 0)),
        rev((H, C, dk), lambda m: (0, m, 0))]
    out_specs = [rev((H, C, C), lambda m: (0, m, 0))] + [rev((H, C, dk), lambda m: (0, m, 0))] * 4 + [
        rev((H, 8, dk), lambda m: (0, m, 0))]
    out_shape = [jax.ShapeDtypeStruct((H, T, C), F32)] + [jax.ShapeDtypeStruct((H, T, dk), F32)] * 4 + [
        jax.ShapeDtypeStruct((H, 8 * n, dk), F32)]
    return pl.pallas_call(
        body, name="gdn_scan_bwd", grid=(n,),
        in_specs=in_specs, out_specs=out_specs, out_shape=out_shape,
        scratch_shapes=[pltpu.VMEM((H, dk, dk), F32)],
        compiler_params=_cp(("arbitrary",)),
    )(*xs, states, do)


def _gdn_post_fwd(o, proj, norm_w):
    H, T, dk = o.shape
    tt = _tile(T, 1024)
    zoff = 3 * GDN_HEADS

    def body(o_ref, z_ref, g_ref, y_ref):
        ov = o_ref[0]
        r = lax.rsqrt(jnp.mean(ov * ov, axis=-1, keepdims=True) + EPS)
        y_ref[...] = (ov * r * g_ref[...] * _silu(z_ref[...])).astype(BF16)

    return pl.pallas_call(
        body, name="gdn_post_fwd", grid=(H, T // tt),
        in_specs=[pl.BlockSpec((1, tt, dk), lambda h, i: (h, i, 0)), pl.BlockSpec((tt, dk), lambda h, i: (i, zoff + h)),
                  pl.BlockSpec((1, dk), lambda h, i: (0, 0))],
        out_specs=pl.BlockSpec((tt, dk), lambda h, i: (i, h)),
        out_shape=jax.ShapeDtypeStruct((T, H * dk), BF16),
        compiler_params=_cp(("parallel", "parallel")),
    )(o, proj, norm_w)


def _gdn_post_bwd(o, proj, norm_w, dy):
    H, T, dk = o.shape
    tt = _tile(T, 1024)
    zoff = 3 * GDN_HEADS

    def body(o_ref, z_ref, g_ref, dy_ref, do_ref, dz_ref, dg_ref):
        i = pl.program_id(1)
        ov, z, g, dyv = o_ref[0], z_ref[...], g_ref[...], dy_ref[...]
        r = lax.rsqrt(jnp.mean(ov * ov, axis=-1, keepdims=True) + EPS)
        on = ov * r
        sz = _silu(z)
        dz_ref[...] = (dyv * on * g * _dsilu(z)).astype(BF16)
        dn = dyv * sz
        gdn = dn * g
        do_ref[0] = r * (gdn - on * jnp.mean(gdn * on, axis=-1, keepdims=True))

        @pl.when(i == 0)
        def _():
            dg_ref[...] = jnp.zeros_like(dg_ref)

        dg_ref[0] += jnp.sum(dn * on, axis=0, keepdims=True)

    return pl.pallas_call(
        body, name="gdn_post_bwd", grid=(H, T // tt),
        in_specs=[pl.BlockSpec((1, tt, dk), lambda h, i: (h, i, 0)), pl.BlockSpec((tt, dk), lambda h, i: (i, zoff + h)),
                  pl.BlockSpec((1, dk), lambda h, i: (0, 0)), pl.BlockSpec((tt, dk), lambda h, i: (i, h))],
        out_specs=[pl.BlockSpec((1, tt, dk), lambda h, i: (h, i, 0)), pl.BlockSpec((tt, dk), lambda h, i: (i, h)),
                   pl.BlockSpec((1, 1, dk), lambda h, i: (h, 0, 0))],
        out_shape=[jax.ShapeDtypeStruct((H, T, dk), F32), jax.ShapeDtypeStruct((T, H * dk), BF16),
                   jax.ShapeDtypeStruct((H, 1, dk), F32)],
        compiler_params=_cp(("parallel", "arbitrary")),
    )(o, proj, norm_w, dy)


def _ple_fwd(x, pp, gl, name):
    T, D = x.shape
    tt = _tile(T, 512)

    def body(x_ref, p_ref, g_ref, o_ref):
        o_ref[...] = x_ref[...] + p_ref[...] * _sigmoid(g_ref[...])

    row = pl.BlockSpec((tt, D), lambda i: (i, 0))
    return pl.pallas_call(
        body, name=name, grid=(T // tt,), in_specs=[row, row, row], out_specs=row,
        out_shape=jax.ShapeDtypeStruct((T, D), F32), compiler_params=_cp(("parallel",)),
    )(x, pp, gl)


def _ple_bwd(dx, pp, gl, name, after=None):
    T, D = dx.shape
    tt = _tile(T, 512)

    def body(dx_ref, p_ref, g_ref, *rest):
        dp_ref, dg_ref = rest[-2:]
        s = _sigmoid(g_ref[...])
        dxv = dx_ref[...]
        dp_ref[...] = (dxv * s).astype(BF16)
        dg_ref[...] = (dxv * p_ref[...] * s * (1.0 - s)).astype(BF16)

    row = pl.BlockSpec((tt, D), lambda i: (i, 0))
    ordered = [] if after is None else [after]
    return pl.pallas_call(
        body, name=name, grid=(T // tt,),
        in_specs=[row, row, row] + [pl.BlockSpec((8, LANE), lambda i: (0, 0)) for _ in ordered], out_specs=[row, row],
        out_shape=[jax.ShapeDtypeStruct((T, D), BF16)] * 2, compiler_params=_cp(("parallel",)),
    )(dx, pp, gl, *ordered)


UP_SHARD = 2 * FFN_DIM // N_CHIPS
DOWN_SHARD = FFN_DIM // N_CHIPS
GATE_SHARD = D_MODEL // N_CHIPS
IN_E_SHARD = 2 * D_MODEL // N_CHIPS


def _ffn_fwd(x, norm, W, conv_w, l):
    T = x.shape[0]
    hf = _rms_fwd(x, norm, f"ffn_norm{l}")
    up = _mm(hf, W["w_up"], "nn", f"ffn_up{l}", dims=(T, 2 * FFN_DIM, D_MODEL), b_view=_cols_of(1, l), tn=UP_SHARD,
             out_dtype=BF16)
    act = _ffn_act_fwd(up, conv_w, f"ffn_act{l}")
    x_out = _mm(act, W["w_down"], "nn", f"ffn_down{l}", dims=(T, D_MODEL, FFN_DIM), b_view=_layer_of(l), res=x,
                tn=1024, tk=1408)
    return x_out, (x, hf, up, act)


def _ffn_bwd(dx_out, saved, norm, W, conv_w, l):
    x, hf, up, act = saved
    T = x.shape[0]
    dact = _mm(dx_out, W["w_down"], "nt", f"ffn_dact{l}", dims=(T, FFN_DIM, D_MODEL), b_view=_layer_of(l),
               out_dtype=BF16, tn=1408)
    dw_down = _mm(act, dx_out, "tn", f"ffn_dwdown{l}", tm=1408, tn=1024, tk=512)
    dup, dcw_g, dcw_v = _ffn_act_bwd(up, conv_w, dact, f"ffn_dact_conv{l}")
    dw_up = _mm(hf, dup, "tn", f"ffn_dwup{l}", dims=(D_MODEL, 2 * FFN_DIM, T), b_view=_cols_of(FFN_DIM // UP_SHARD),
                o_view=_cols_of(1), out_shape=(N_CHIPS, D_MODEL, UP_SHARD), tm=1024, tn=UP_SHARD, tk=512)
    dhf = _mm(dup, W["w_up"], "nt", f"ffn_dhf{l}", dims=(T, D_MODEL, 2 * FFN_DIM),
              a_view=_cols_of(FFN_DIM // UP_SHARD), b_view=_cols_of(1, l), tn=1024, tk=UP_SHARD)
    dx, dnorm = _rms_bwd(x, norm, dhf, dx_out, f"ffn_dnorm{l}")
    return (dx, dnorm, dw_up, jnp.concatenate([dcw_g, dcw_v], axis=1),
            dw_down.reshape(N_CHIPS, DOWN_SHARD, D_MODEL))


def _ple_layer_fwd(x, p, norm, W, l):
    T = x.shape[0]
    hg = _rms_fwd(x, norm, f"ple_norm{l}")
    gl = _mm(hg, W["w_ple_gate"], "nn", f"ple_gate{l}", dims=(T, D_MODEL, D_MODEL), b_view=_layer_of(l), tn=1024)
    pp = _mm(p, W["w_ple"], "nn", f"ple_proj{l}", dims=(T, D_MODEL, PLE_DIM), a_view=_layer_of(l),
             b_view=_layer_of(l), tn=1024)
    return _ple_fwd(x, pp, gl, f"ple_mix{l}"), (x, hg, gl, pp)


def _ple_layer_bwd(dx_out, saved, p, norm, W, l, after=None):
    x, hg, gl, pp = saved
    T = x.shape[0]
    dpp, dgl = _ple_bwd(dx_out, pp, gl, f"ple_dmix{l}", after)
    dw_ple = _mm(p, dpp, "tn", f"ple_dwple{l}", dims=(PLE_DIM, D_MODEL, T), a_view=_layer_of(l), o_view=_cols_of(1),
                 out_shape=(N_CHIPS, PLE_DIM, PLE_DIM), tm=PLE_DIM, tn=PLE_DIM, tk=512)
    dw_gate = _mm(hg, dgl, "tn", f"ple_dwgate{l}", tm=1024, tn=1024, tk=512)
    dhg = _mm(dgl, W["w_ple_gate"], "nt", f"ple_dhg{l}", dims=(T, D_MODEL, D_MODEL), b_view=_layer_of(l), tn=1024)
    dx, dnorm = _rms_bwd(x, norm, dhg, dx_out, f"ple_dnorm{l}")
    return dx, dnorm, dw_gate.reshape(N_CHIPS, GATE_SHARD, D_MODEL), dw_ple


def _local_step(x, p, target, W, token=None, later_weights=None, on_grads=None):
    T = x.shape[0]
    H = GDN_HEADS
    G = {}

    hn_e = _rms_fwd(x, W["mix_norm_e"], "mix_norm_e", after=token)
    proj_e = _mm(hn_e, W["w_in_e"], "nn", "in_e", dims=(T, 2 * D_MODEL, D_MODEL), b_view=_cols_of(1), tn=IN_E_SHARD)
    pool_out = _pool_fwd(proj_e, W["pool_w"], W["pool_scale"])
    attn, ltot = _sb_fwd(proj_e)
    mix_e = jnp.concatenate([pool_out, attn], axis=1).astype(BF16)
    x1 = _mm(mix_e, W["w_out_e"], "nn", "out_e", res=x, tn=1024)
    if later_weights is not None:
        W = {**W, **later_weights(x1)}
    x2, ffn0 = _ffn_fwd(x1, W["ffn_norm"][0:1], W, W["ffn_conv"][0], 0)
    x3, ple0 = _ple_layer_fwd(x2, p, W["ple_norm"][0:1], W, 0)

    hn_o = _rms_fwd(x3, W["mix_norm_o"], "mix_norm_o")
    proj_o = _mm(hn_o, W["w_in_o"], "nn", "in_o", tn=1408)
    qkv = _gdn_pre_fwd(proj_o, W["conv_qkv_o"])
    gates = jnp.pad(jnp.concatenate([W["a_log_o"], W["dt_bias_o"]], axis=0), ((0, 6), (0, LANE - H)))
    xs = _gdn_prep_fwd(qkv, proj_o, gates)
    o, states = _gdn_scan_fwd(xs)
    og = _gdn_post_fwd(o, proj_o, W["gdn_norm_o"])
    x4 = _mm(og, W["w_out_o"], "nn", "out_o", res=x3, tn=1024)
    x5, ffn1 = _ffn_fwd(x4, W["ffn_norm"][1:2], W, W["ffn_conv"][1], 1)
    x6, ple1 = _ple_layer_fwd(x5, p, W["ple_norm"][1:2], W, 1)

    sq, dx6, G["final_norm"] = _final_loss(x6, W["final_norm"], target, "final_loss")

    dx5, dpn1, dwg1, dwp1 = _ple_layer_bwd(dx6, ple1, p, W["ple_norm"][1:2], W, 1)
    dx4, dfn1, dwu1, dfc1, dwd1 = _ffn_bwd(dx5, ffn1, W["ffn_norm"][1:2], W, W["ffn_conv"][1], 1)
    dog = _mm(dx4, W["w_out_o"], "nt", "d_og", tn=1024)
    G["w_out_o"] = _mm(og, dx4, "tn", "dw_out_o", tm=1024, tn=1024, tk=512).reshape(N_CHIPS, GATE_SHARD, D_MODEL)
    do, dz, dgn = _gdn_post_bwd(o, proj_o, W["gdn_norm_o"], dog)
    G["gdn_norm_o"] = jnp.sum(dgn, axis=0)
    dxs = _gdn_scan_bwd(xs, states, do)
    dqkv_act, dba, dgates = _gdn_prep_bwd(qkv, proj_o, gates, dxs)
    G["a_log_o"] = dgates[0:1, :H]
    G["dt_bias_o"] = dgates[1:2, :H]
    dqkv, G["conv_qkv_o"] = _gdn_pre_bwd(proj_o, W["conv_qkv_o"], dqkv_act)
    dproj_o = jnp.concatenate([dqkv, dz, dba.astype(BF16)], axis=1)
    G["w_in_o"] = _mm(hn_o, dproj_o, "tn", "dw_in_o", tm=1024, tn=1408, tk=512)
    dhn_o = _mm(dproj_o, W["w_in_o"], "nt", "d_hn_o", tn=1024, tk=1408)
    dx3, G["mix_norm_o"] = _rms_bwd(x3, W["mix_norm_o"], dhn_o, dx4, "d_mix_norm_o")

    token1 = token0 = None
    if on_grads is not None:
        token1 = on_grads("second", {("w_in_o", 0): _chip_major_w_in_o(G["w_in_o"]), ("w_out_o", 0): G["w_out_o"],
                                     ("w_up", 1): dwu1, ("w_down", 1): dwd1, ("w_ple_gate", 1): dwg1,
                                     ("w_ple", 1): dwp1})
    dx2, dpn0, dwg0, dwp0 = _ple_layer_bwd(dx3, ple0, p, W["ple_norm"][0:1], W, 0, after=token1)
    dx1, dfn0, dwu0, dfc0, dwd0 = _ffn_bwd(dx2, ffn0, W["ffn_norm"][0:1], W, W["ffn_conv"][0], 0)
    if on_grads is not None:
        token0 = on_grads("first_ffn", {("w_up", 0): dwu0, ("w_down", 0): dwd0, ("w_ple_gate", 0): dwg0,
                                        ("w_ple", 0): dwp0})
    dmix = _mm(dx1, W["w_out_e"], "nt", "d_mix_e", tn=1024)
    G["w_out_e"] = _mm(mix_e, dx1, "tn", "dw_out_e", tm=1024, tn=1024, tk=512).reshape(N_CHIPS, GATE_SHARD, D_MODEL)
    du, G["pool_w"], G["pool_scale"] = _pool_bwd(proj_e, W["pool_w"], W["pool_scale"], dmix)
    dqa, dka, dva = _sb_bwd(proj_e, dmix, ltot, after=token0)
    dproj_e = jnp.concatenate([du, dqa, dka, dva], axis=1).astype(BF16)
    G["w_in_e"] = _mm(hn_e, dproj_e, "tn", "dw_in_e", dims=(D_MODEL, 2 * D_MODEL, T), o_view=_cols_of(1),
                      out_shape=(N_CHIPS, D_MODEL, IN_E_SHARD), tm=1024, tn=IN_E_SHARD, tk=512)
    dhn_e = _mm(dproj_e, W["w_in_e"], "nt", "d_hn_e", dims=(T, D_MODEL, 2 * D_MODEL), b_view=_cols_of(1),
                tn=1024, tk=IN_E_SHARD)
    grad_x, G["mix_norm_e"] = _rms_bwd(x, W["mix_norm_e"], dhn_e, dx1, "d_mix_norm_e")

    G["ffn_norm"] = jnp.concatenate([dfn0, dfn1], axis=0)
    G["ple_norm"] = jnp.concatenate([dpn0, dpn1], axis=0)
    G["ffn_conv"] = jnp.stack([dfc0, dfc1])
    G["w_up"] = [dwu0, dwu1]
    G["w_down"] = [dwd0, dwd1]
    G["w_ple_gate"] = [dwg0, dwg1]
    G["w_ple"] = [dwp0, dwp1]
    return sq[0, 0], grad_x, G


BIG = ("w_in_e", "w_out_e", "w_in_o", "w_out_o", "w_up", "w_down", "w_ple_gate", "w_ple")
SHARDED_SMALL = (("mix_norm_o", 1), ("conv_qkv_o", 2), ("ffn_conv", 2))
REPLICATED = ("mix_norm_e", "pool_w", "pool_scale", "a_log_o", "dt_bias_o", "gdn_norm_o", "ffn_norm", "ple_norm",
              "final_norm")
WEIGHT_ORDER = ("mix_norm_e", "w_in_e", "pool_w", "pool_scale", "w_out_e", "mix_norm_o", "w_in_o", "conv_qkv_o",
                "a_log_o", "dt_bias_o", "gdn_norm_o", "w_out_o", "ffn_norm", "w_up", "ffn_conv", "w_down", "ple_norm",
                "w_ple_gate", "w_ple", "final_norm")
SMALL_W = LANE
SMALL_ROWS = 16


def _size(shape):
    n = 1
    for s in shape:
        n *= s
    return n


def _pack(arrs, width, granule):
    flat = jnp.concatenate([a.reshape(-1) for a in arrs])
    rows = -(-flat.shape[0] // width)
    rows = -(-rows // granule) * granule
    return jnp.pad(flat, (0, rows * width - flat.shape[0])).reshape(rows, width)


def _unpack(flat2d, shapes):
    flat = flat2d.reshape(-1)
    out, off = [], 0
    for s in shapes:
        out.append(flat[off:off + _size(s)].reshape(s))
        off += _size(s)
    return out


MESH_ID = pl.DeviceIdType.MESH
HBM_SPEC = pl.BlockSpec(memory_space=pltpu.HBM)


def _where_am_i():
    return lax.axis_index("x"), lax.axis_index("y"), lax.axis_index("c")


def _other_chips(x, y):
    return [(1 - x, y), (x, 1 - y), (1 - x, 1 - y)]


def _remote(src, dst, send_sems, recv_sems, k, to):
    return pltpu.make_async_remote_copy(src_ref=src, dst_ref=dst, send_sem=send_sems.at[k], recv_sem=recv_sems.at[k],
                                        device_id=to, device_id_type=MESH_ID)


def _chip_allgather(pack, name):
    R, Wd = pack.shape
    Rh = R // 2

    def body(src_ref, out_ref, send_sems, recv_sems, local_sem):
        x, y, c = _where_am_i()
        me, sib = (x, y, c), (x, y, 1 - c)
        chips = _other_chips(x, y)
        mine_rows = pl.ds(pl.multiple_of(c * Rh, SMALL_ROWS), Rh)
        sib_rows = pl.ds(pl.multiple_of((1 - c) * Rh, SMALL_ROWS), Rh)
        j_me = 2 * x + y
        local = pltpu.make_async_copy(src_ref, out_ref.at[j_me], local_sem)
        local.start()
        first = [_remote(src_ref.at[mine_rows], out_ref.at[j_me, mine_rows], send_sems, recv_sems, k, (cx, cy, c))
                 for k, (cx, cy) in enumerate(chips)]
        for cp in first:
            cp.start()
        passed = []
        for k, (cx, cy) in enumerate(chips):
            blk = out_ref.at[2 * cx + cy, mine_rows]
            _remote(blk, blk, send_sems, recv_sems, k, me).wait_recv()
            fw = _remote(blk, blk, send_sems, recv_sems, 3 + k, sib)
            fw.start()
            passed.append(fw)
        for k, (cx, cy) in enumerate(chips):
            blk = out_ref.at[2 * cx + cy, sib_rows]
            _remote(blk, blk, send_sems, recv_sems, 3 + k, me).wait_recv()
        for cp in first + passed:
            cp.wait_send()
        local.wait()

    return pl.pallas_call(
        body, name=name, in_specs=[HBM_SPEC], out_specs=HBM_SPEC,
        out_shape=jax.ShapeDtypeStruct((N_CHIPS, R, Wd), pack.dtype),
        scratch_shapes=[pltpu.SemaphoreType.DMA((6,)), pltpu.SemaphoreType.DMA((6,)), pltpu.SemaphoreType.DMA],
    )(pack)


def _chip_allgather_many(blocks, name):
    n = len(blocks)

    def body(*refs):
        srcs, outs = refs[:n], refs[n:2 * n]
        send_sems, recv_sems = refs[2 * n:]
        x, y, c = _where_am_i()
        me, sib = (x, y, c), (x, y, 1 - c)
        chips = _other_chips(x, y)
        j_me = 2 * x + y
        first = [_remote(srcs[p].at[c], outs[p].at[j_me, c], send_sems, recv_sems, 6 * p + k, (cx, cy, c))
                 for p in range(n) for k, (cx, cy) in enumerate(chips)]
        for cp in first:
            cp.start()
        passed = []
        for k, (cx, cy) in enumerate(chips):
            for p in range(n):
                blk = outs[p].at[2 * cx + cy, c]
                _remote(blk, blk, send_sems, recv_sems, 6 * p + k, me).wait_recv()
                fw = _remote(blk, blk, send_sems, recv_sems, 6 * p + 3 + k, sib)
                fw.start()
                passed.append(fw)
        for k, (cx, cy) in enumerate(chips):
            for p in range(n):
                blk = outs[p].at[2 * cx + cy, 1 - c]
                _remote(blk, blk, send_sems, recv_sems, 6 * p + 3 + k, me).wait_recv()
        for cp in first + passed:
            cp.wait_send()

    return pl.pallas_call(
        body, name=name, in_specs=[HBM_SPEC] * n, out_specs=[HBM_SPEC] * n,
        out_shape=[jax.ShapeDtypeStruct((N_CHIPS,) + b.shape, b.dtype) for b in blocks],
        scratch_shapes=[pltpu.SemaphoreType.DMA((6 * n,)), pltpu.SemaphoreType.DMA((6 * n,))],
    )(*blocks)


SEM_SPEC = pl.BlockSpec(memory_space=pltpu.SEMAPHORE)
DATAFLOW_EFFECT = pltpu.SideEffectType.DATAFLOW_SIDE_EFFECTING


def _chip_allgather_start(blocks, name):
    n = len(blocks)

    def body(*refs):
        srcs, lands = refs[:n], refs[n:2 * n]
        send_sems, recv_sems, token = refs[2 * n], refs[2 * n + 1], refs[-1]
        x, y, c = _where_am_i()
        j_me = 2 * x + y
        for p in range(n):
            for k, (cx, cy) in enumerate(_other_chips(x, y)):
                _remote(srcs[p].at[c], lands[p].at[j_me, c], send_sems, recv_sems, 3 * p + k, (cx, cy, c)).start()
        token[...] = jnp.zeros_like(token)

    lands = [pltpu.with_memory_space_constraint(lax.empty((N_CHIPS,) + b.shape, b.dtype), pltpu.HBM) for b in blocks]
    blocks = [pltpu.with_memory_space_constraint(b, pltpu.HBM) for b in blocks]
    outs = pl.pallas_call(
        body, name=name,
        in_specs=[HBM_SPEC] * (2 * n),
        out_specs=[SEM_SPEC, SEM_SPEC] + [HBM_SPEC] * (2 * n) + [pl.BlockSpec(memory_space=pltpu.VMEM)],
        out_shape=[pltpu.SemaphoreType.DMA((3 * n,)), pltpu.SemaphoreType.DMA((3 * n,))]
        + [pltpu.HBM(a.shape, a.dtype) for a in blocks + lands] + [jax.ShapeDtypeStruct((8, LANE), F32)],
        input_output_aliases={i: 2 + i for i in range(2 * n)},
        compiler_params=pltpu.CompilerParams(has_side_effects=DATAFLOW_EFFECT),
    )(*blocks, *lands)
    return outs[0], outs[1], list(outs[2:2 + n]), list(outs[2 + n:2 + 2 * n]), outs[-1]


def _chip_allgather_wait(send_sems, recv_sems, blocks, lands, after, name):
    n = len(blocks)

    def body(*refs):
        srcs, zones = refs[:n], refs[n:2 * n]
        send, recv = refs[2 * n], refs[2 * n + 1]
        x, y, c = _where_am_i()
        for p in range(n):
            for k, (cx, cy) in enumerate(_other_chips(x, y)):
                cp = _remote(srcs[p].at[c], zones[p].at[2 * cx + cy, c], send, recv, 3 * p + k, (x, y, c))
                cp.wait_send()
                cp.wait_recv()

    outs = pl.pallas_call(
        body, name=name,
        in_specs=[HBM_SPEC] * (2 * n) + [SEM_SPEC, SEM_SPEC, pl.BlockSpec(memory_space=pl.ANY)],
        out_specs=[HBM_SPEC] * (2 * n),
        out_shape=[pltpu.HBM(a.shape, a.dtype) for a in list(blocks) + list(lands)],
        input_output_aliases={i: i for i in range(2 * n)},
        compiler_params=pltpu.CompilerParams(has_side_effects=DATAFLOW_EFFECT),
    )(*blocks, *lands, send_sems, recv_sems, after)
    return list(outs[n:])


def _chip_scatter_start(sums, name):
    n = len(sums)

    def body(*refs):
        srcs, lands = refs[:n], refs[n:2 * n]
        send_sems, recv_sems, token = refs[2 * n], refs[2 * n + 1], refs[-1]
        x, y, c = _where_am_i()
        for p in range(n):
            for k, (cx, cy) in enumerate(_other_chips(x, y)):
                _remote(srcs[p].at[2 * cx + cy], lands[p].at[k], send_sems, recv_sems, 3 * p + k, (cx, cy, c)).start()
        token[...] = jnp.zeros_like(token)

    lands = [pltpu.with_memory_space_constraint(lax.empty((N_CHIPS - 1,) + s.shape[1:], s.dtype), pltpu.HBM)
             for s in sums]
    sums = [pltpu.with_memory_space_constraint(s, pltpu.HBM) for s in sums]
    outs = pl.pallas_call(
        body, name=name,
        in_specs=[HBM_SPEC] * (2 * n),
        out_specs=[SEM_SPEC, SEM_SPEC] + [HBM_SPEC] * (2 * n) + [pl.BlockSpec(memory_space=pltpu.VMEM)],
        out_shape=[pltpu.SemaphoreType.DMA((3 * n,)), pltpu.SemaphoreType.DMA((3 * n,))]
        + [pltpu.HBM(a.shape, a.dtype) for a in sums + lands] + [jax.ShapeDtypeStruct((8, LANE), F32)],
        input_output_aliases={i: 2 + i for i in range(2 * n)},
        compiler_params=pltpu.CompilerParams(has_side_effects=DATAFLOW_EFFECT),
    )(*sums, *lands)
    return outs[0], outs[1], list(outs[2:2 + n]), list(outs[2 + n:2 + 2 * n]), outs[-1]


def _chip_scatter_wait(send_sems, recv_sems, sums, lands, after, name):
    n = len(sums)

    def body(*refs):
        srcs, zones = refs[:n], refs[n:2 * n]
        send, recv = refs[2 * n], refs[2 * n + 1]
        x, y, c = _where_am_i()
        for p in range(n):
            for k, (cx, cy) in enumerate(_other_chips(x, y)):
                cp = _remote(srcs[p].at[2 * cx + cy], zones[p].at[k], send, recv, 3 * p + k, (x, y, c))
                cp.wait_send()
                cp.wait_recv()

    outs = pl.pallas_call(
        body, name=name,
        in_specs=[HBM_SPEC] * (2 * n) + [SEM_SPEC, SEM_SPEC, pl.BlockSpec(memory_space=pl.ANY)],
        out_specs=[HBM_SPEC] * (2 * n),
        out_shape=[pltpu.HBM(a.shape, a.dtype) for a in list(sums) + list(lands)],
        input_output_aliases={i: i for i in range(2 * n)},
        compiler_params=pltpu.CompilerParams(has_side_effects=DATAFLOW_EFFECT),
    )(*sums, *lands, send_sems, recv_sems, after)
    return list(outs[n:])


def _chip_allgather_forward(lands, name):
    n = len(lands)

    def body(*refs):
        ins, outs = refs[:n], refs[n:2 * n]
        send_sems, recv_sems = refs[2 * n:]
        x, y, c = _where_am_i()
        me, sib = (x, y, c), (x, y, 1 - c)
        chips = _other_chips(x, y)
        passed = [_remote(ins[p].at[2 * cx + cy, c], outs[p].at[2 * cx + cy, c], send_sems, recv_sems, 3 * p + k, sib)
                  for p in range(n) for k, (cx, cy) in enumerate(chips)]
        for cp in passed:
            cp.start()
        for p in range(n):
            for k, (cx, cy) in enumerate(chips):
                blk = outs[p].at[2 * cx + cy, 1 - c]
                _remote(blk, blk, send_sems, recv_sems, 3 * p + k, me).wait_recv()
        for cp in passed:
            cp.wait_send()

    return pl.pallas_call(
        body, name=name, in_specs=[HBM_SPEC] * n, out_specs=[HBM_SPEC] * n,
        out_shape=[jax.ShapeDtypeStruct(a.shape, a.dtype) for a in lands],
        input_output_aliases={i: i for i in range(n)},
        scratch_shapes=[pltpu.SemaphoreType.DMA((3 * n,)), pltpu.SemaphoreType.DMA((3 * n,))],
    )(*lands)


def _sibling_swap_many(pieces, name):
    n = len(pieces)

    def body(*refs):
        srcs, outs = refs[:n], refs[n:2 * n]
        send_sems, recv_sems = refs[2 * n:]
        x, y, c = _where_am_i()
        cps = [_remote(srcs[p].at[:, 1 - c], outs[p], send_sems, recv_sems, p, (x, y, 1 - c)) for p in range(n)]
        for cp in cps:
            cp.start()
        for cp in cps:
            cp.wait()

    return pl.pallas_call(
        body, name=name, in_specs=[HBM_SPEC] * n, out_specs=[HBM_SPEC] * n,
        out_shape=[jax.ShapeDtypeStruct((g.shape[0],) + g.shape[2:], g.dtype) for g in pieces],
        scratch_shapes=[pltpu.SemaphoreType.DMA((n,)), pltpu.SemaphoreType.DMA((n,))],
    )(*pieces)


def _chip_scatter_many(sums, name):
    n = len(sums)

    def body(*refs):
        srcs, outs = refs[:n], refs[n:2 * n]
        send_sems, recv_sems = refs[2 * n:]
        x, y, c = _where_am_i()
        cps = [_remote(srcs[p].at[2 * cx + cy], outs[p].at[k], send_sems, recv_sems, 3 * p + k, (cx, cy, c))
               for p in range(n) for k, (cx, cy) in enumerate(_other_chips(x, y))]
        for cp in cps:
            cp.start()
        for cp in cps:
            cp.wait()

    return pl.pallas_call(
        body, name=name, in_specs=[HBM_SPEC] * n, out_specs=[HBM_SPEC] * n,
        out_shape=[jax.ShapeDtypeStruct((N_CHIPS - 1,) + s.shape[1:], s.dtype) for s in sums],
        scratch_shapes=[pltpu.SemaphoreType.DMA((3 * n,)), pltpu.SemaphoreType.DMA((3 * n,))],
    )(*sums)


def _sibling_send_many(halves, name):
    n = len(halves)

    def body(*refs):
        srcs, outs = refs[:n], refs[n:2 * n]
        send_sems, recv_sems = refs[2 * n:]
        x, y, c = _where_am_i()
        cps = [_remote(srcs[p], outs[p], send_sems, recv_sems, p, (x, y, 1 - c)) for p in range(n)]
        for cp in cps:
            cp.start()
        for cp in cps:
            cp.wait()

    return pl.pallas_call(
        body, name=name, in_specs=[HBM_SPEC] * n, out_specs=[HBM_SPEC] * n,
        out_shape=[jax.ShapeDtypeStruct(h.shape, h.dtype) for h in halves],
        scratch_shapes=[pltpu.SemaphoreType.DMA((n,)), pltpu.SemaphoreType.DMA((n,))],
    )(*halves)


def _row_tile(rows, pref=512):
    best = 8
    for t in range(8, pref + 1, 8):
        if rows % t == 0:
            best = t
    return best


def _where_ids():
    x, y, c = _where_am_i()
    return jnp.stack([c, 2 * x + y]).astype(jnp.int32)


RS_ROWS = 256


def _chip_sums_bf16(G, A, ids, name):
    n, _, hr, cols = G.shape
    tr = _row_tile(hr, RS_ROWS)

    def body(ids_ref, g_ref, a_ref, o_ref):
        o_ref[...] = (g_ref[...] + a_ref[...]).astype(BF16)

    return pl.pallas_call(
        body, name=name,
        grid_spec=pltpu.PrefetchScalarGridSpec(
            num_scalar_prefetch=1, grid=(n, hr // tr),
            in_specs=[pl.BlockSpec((None, None, tr, cols), lambda j, i, ids: (j, ids[0], i, 0)),
                      pl.BlockSpec((None, tr, cols), lambda j, i, ids: (j, i, 0))],
            out_specs=pl.BlockSpec((None, tr, cols), lambda j, i, ids: (j, i, 0))),
        out_shape=jax.ShapeDtypeStruct((n, hr, cols), BF16),
        compiler_params=_cp(("parallel", "parallel")),
    )(ids, G, A)


def _total_half(G, A, B, ids, name):
    _, _, hr, cols = G.shape
    tr = _row_tile(hr, RS_ROWS)

    def body(ids_ref, g_ref, a_ref, b_ref, o_ref):
        s = g_ref[...] + a_ref[...]
        for k in range(N_CHIPS - 1):
            s = s + b_ref[k].astype(F32)
        o_ref[...] = s

    return pl.pallas_call(
        body, name=name,
        grid_spec=pltpu.PrefetchScalarGridSpec(
            num_scalar_prefetch=1, grid=(hr // tr,),
            in_specs=[pl.BlockSpec((None, None, tr, cols), lambda i, ids: (ids[1], ids[0], i, 0)),
                      pl.BlockSpec((None, tr, cols), lambda i, ids: (ids[1], i, 0)),
                      pl.BlockSpec((N_CHIPS - 1, tr, cols), lambda i, ids: (0, i, 0))],
            out_specs=pl.BlockSpec((tr, cols), lambda i, ids: (i, 0))),
        out_shape=jax.ShapeDtypeStruct((hr, cols), F32),
        compiler_params=_cp(("parallel",)),
    )(ids, G, A, B)


def _small_allreduce(v, name):
    R, Wd = v.shape

    def body(x_ref, sum_ref, all_ref, send_sems, recv_sems, local_sem):
        x, y, c = _where_am_i()
        me, sib = (x, y, c), (x, y, 1 - c)
        chips = _other_chips(x, y)

        def slot(px, py, pc):
            return all_ref.at[4 * px + 2 * py + pc]

        local = pltpu.make_async_copy(x_ref, slot(*me), local_sem)
        local.start()
        first = [_remote(x_ref, slot(*me), send_sems, recv_sems, 0, sib)]
        first += [_remote(x_ref, slot(*me), send_sems, recv_sems, 1 + k, (cx, cy, c)) for k, (cx, cy) in enumerate(chips)]
        for cp in first:
            cp.start()
        passed = []
        for k, (cx, cy) in enumerate(chips):
            blk = slot(cx, cy, c)
            _remote(blk, blk, send_sems, recv_sems, 1 + k, me).wait_recv()
            fw = _remote(blk, blk, send_sems, recv_sems, 4 + k, sib)
            fw.start()
            passed.append(fw)
        _remote(slot(*sib), slot(*sib), send_sems, recv_sems, 0, me).wait_recv()
        for k, (cx, cy) in enumerate(chips):
            blk = slot(cx, cy, 1 - c)
            _remote(blk, blk, send_sems, recv_sems, 4 + k, me).wait_recv()
        for cp in first + passed:
            cp.wait_send()
        local.wait()
        s = all_ref[0]
        for d in range(1, N_DEV):
            s = s + all_ref[d]
        sum_ref[...] = s

    vm = pl.BlockSpec(memory_space=pltpu.VMEM)
    return pl.pallas_call(
        body, name=name, in_specs=[vm], out_specs=[vm, vm],
        out_shape=[jax.ShapeDtypeStruct((R, Wd), F32), jax.ShapeDtypeStruct((N_DEV, R, Wd), F32)],
        scratch_shapes=[pltpu.SemaphoreType.DMA((7,)), pltpu.SemaphoreType.DMA((7,)), pltpu.SemaphoreType.DMA],
    )(v)[0]


def _adamw(w, g, m, v, name):
    L, R, Wd = w.shape
    tr = _row_tile(R, RS_ROWS)
    c1 = 1.0 - ADAM_B1 ** ADAM_STEP
    c2 = 1.0 - ADAM_B2 ** ADAM_STEP

    def body(w_ref, g_ref, m_ref, v_ref, d_ref, nm_ref, nv_ref):
        gv = g_ref[...]
        nm = ADAM_B1 * m_ref[...] + (1.0 - ADAM_B1) * gv
        nv = ADAM_B2 * v_ref[...] + (1.0 - ADAM_B2) * (gv * gv)
        d_ref[...] = -ADAM_LR * ((nm / c1) / (jnp.sqrt(nv / c2) + ADAM_EPS) + ADAM_WD * w_ref[...])
        nm_ref[...] = nm
        nv_ref[...] = nv

    row = pl.BlockSpec((None, tr, Wd), lambda l, i: (l, i, 0))
    shp = jax.ShapeDtypeStruct((L, R, Wd), F32)
    return pl.pallas_call(
        body, name=name, grid=(L, R // tr), in_specs=[row] * 4, out_specs=[row] * 3, out_shape=[shp] * 3,
        compiler_params=_cp(("parallel", "parallel")),
    )(w, g, m, v)


def _adamw_halves(w, m, v, mine, theirs, ids, name):
    L, R, Wd = w.shape
    hr = R // 2
    tr = _row_tile(hr, RS_ROWS)
    c1 = 1.0 - ADAM_B1 ** ADAM_STEP
    c2 = 1.0 - ADAM_B2 ** ADAM_STEP

    def body(ids_ref, w_ref, m_ref, v_ref, *refs):
        g_refs, (g_ref, d_ref, nm_ref, nv_ref) = refs[:2 * L], refs[2 * L:]
        layer, half = pl.program_id(0), pl.program_id(1)
        own = half == ids_ref[0]
        gv = jnp.where(own, g_refs[0][...], g_refs[L][...])
        for l in range(1, L):
            gv = jnp.where(layer == l, jnp.where(own, g_refs[l][...], g_refs[L + l][...]), gv)
        nm = ADAM_B1 * m_ref[...] + (1.0 - ADAM_B1) * gv
        nv = ADAM_B2 * v_ref[...] + (1.0 - ADAM_B2) * (gv * gv)
        g_ref[...] = gv
        d_ref[...] = -ADAM_LR * ((nm / c1) / (jnp.sqrt(nv / c2) + ADAM_EPS) + ADAM_WD * w_ref[...])
        nm_ref[...] = nm
        nv_ref[...] = nv

    blk = pl.BlockSpec((None, None, tr, Wd), lambda l, h, i, ids: (l, h, i, 0))
    g_blk = pl.BlockSpec((tr, Wd), lambda l, h, i, ids: (i, 0))
    shp = jax.ShapeDtypeStruct((L, 2, hr, Wd), F32)
    outs = pl.pallas_call(
        body, name=name,
        grid_spec=pltpu.PrefetchScalarGridSpec(
            num_scalar_prefetch=1, grid=(L, 2, hr // tr),
            in_specs=[blk] * 3 + [g_blk] * (2 * L), out_specs=[blk] * 4),
        out_shape=[shp] * 4,
        compiler_params=_cp(("parallel", "parallel", "parallel")),
    )(ids, *[a.reshape(L, 2, hr, Wd) for a in (w, m, v)], *mine, *theirs)
    return tuple(o.reshape(L, R, Wd) for o in outs)


def _two_halves(a):
    cols = a.shape[-1]
    return a.reshape(2, _size(a.shape) // (2 * cols), cols)


FIRST_NEEDED = ("w_in_e", "w_out_e")
LATER_NEEDED = tuple(n for n in BIG if n not in FIRST_NEEDED)


def _gather_weights(P):
    chip = 2 * lax.axis_index("x") + lax.axis_index("y")

    def with_own(landed, own):
        return lax.dynamic_update_slice_in_dim(landed, own[None], chip, axis=0)

    mine = {n: _two_halves(P[n].astype(BF16)) for n in BIG}
    first = _chip_allgather_many([mine[n] for n in FIRST_NEEDED], "ag_first")
    gathered = {n: with_own(g, mine[n]) for n, g in zip(FIRST_NEEDED, first)}
    send_sems, recv_sems, blocks, lands, token = _chip_allgather_start([mine[n] for n in LATER_NEEDED], "ag_start")

    def later(after):
        landed = _chip_allgather_wait(send_sems, recv_sems, blocks, lands, after, "ag_wait")
        g = {n: with_own(a, mine[n]) for n, a in zip(LATER_NEEDED, _chip_allgather_forward(landed, "ag_forward"))}
        w_in_o = g["w_in_o"].reshape(N_CHIPS, D_MODEL, ODD_IN // N_CHIPS)
        return {
            "w_out_o": g["w_out_o"].reshape(D_MODEL, D_MODEL),
            "w_in_o": jnp.pad(jnp.concatenate([w_in_o[j] for j in range(N_CHIPS)], axis=1),
                              ((0, 0), (0, ODD_IN_PAD - ODD_IN))),
            "w_up": g["w_up"],
            "w_down": g["w_down"].transpose(1, 0, 2, 3).reshape(2, FFN_DIM, D_MODEL),
            "w_ple_gate": g["w_ple_gate"].transpose(1, 0, 2, 3).reshape(2, D_MODEL, D_MODEL),
            "w_ple": g["w_ple"].transpose(1, 2, 0, 3).reshape(2, PLE_DIM, D_MODEL),
        }

    small_shapes = [P[n].shape for n, _ in SHARDED_SMALL]
    small = _chip_allgather(_pack([P[n] for n, _ in SHARDED_SMALL], SMALL_W, SMALL_ROWS), "ag_small")
    parts = [_unpack(small[j], small_shapes) for j in range(N_CHIPS)]
    full = {n: jnp.concatenate([parts[j][i] for j in range(N_CHIPS)], axis=ax)
            for i, (n, ax) in enumerate(SHARDED_SMALL)}
    W = {n: P[n] for n in REPLICATED}
    W["pool_w"] = P["pool_w"][0]
    W["final_norm"] = P["final_norm"].reshape(1, D_MODEL)
    W["mix_norm_o"] = full["mix_norm_o"]
    W["conv_qkv_o"] = full["conv_qkv_o"][0]
    W["ffn_conv"] = full["ffn_conv"]
    W["w_in_e"] = gathered["w_in_e"].reshape(N_CHIPS, D_MODEL, IN_E_SHARD)
    W["w_out_e"] = gathered["w_out_e"].reshape(D_MODEL, D_MODEL)
    return W, token, later


def _chip_major_w_in_o(g):
    shard = ODD_IN // N_CHIPS
    return jnp.stack([g[:, j * shard:(j + 1) * shard] for j in range(N_CHIPS)])


def _reduce_begin(grads, ids, tag, travel_later):
    keys = list(grads)
    pieces = [g.reshape(N_CHIPS, 2, g.shape[1] // 2, g.shape[2]) for g in grads.values()]
    from_sibling = _sibling_swap_many(pieces, f"rs_sibling_swap_{tag}")
    sums = [_chip_sums_bf16(g, a, ids, f"rs_chip_sums_{tag}{i}") for i, (g, a) in enumerate(zip(pieces, from_sibling))]
    state = dict(keys=keys, pieces=pieces, from_sibling=from_sibling, ids=ids, tag=tag, token=None)
    if travel_later:
        state["flight"] = _chip_scatter_start(sums, f"rs_scatter_start_{tag}")
        state["token"] = state["flight"][-1]
    else:
        state["from_chips"] = _chip_scatter_many(sums, f"rs_chip_scatter_{tag}")
    return state


def _reduce_end(state, after=None):
    tag, ids = state["tag"], state["ids"]
    if "flight" in state:
        send_sems, recv_sems, sums, lands, _ = state["flight"]
        from_chips = _chip_scatter_wait(send_sems, recv_sems, sums, lands, after, f"rs_scatter_wait_{tag}")
    else:
        from_chips = state["from_chips"]
    halves = [_total_half(g, a, b, ids, f"rs_total_{tag}{i}")
              for i, (g, a, b) in enumerate(zip(state["pieces"], state["from_sibling"], from_chips))]
    theirs = _sibling_send_many(halves, f"rs_sibling_send_{tag}")
    return {k: (h, t) for k, h, t in zip(state["keys"], halves, theirs)}


def kernel(x, p, mix_norm_e, w_in_e, pool_w, pool_scale, w_out_e, mix_norm_o, w_in_o, conv_qkv_o, a_log_o, dt_bias_o, gdn_norm_o, w_out_o, ffn_norm, w_up, ffn_conv, w_down, ple_norm, w_ple_gate, w_ple, final_norm, loss_target, m_mix_norm_e, m_w_in_e, m_pool_w, m_pool_scale, m_w_out_e, m_mix_norm_o, m_w_in_o, m_conv_qkv_o, m_a_log_o, m_dt_bias_o, m_gdn_norm_o, m_w_out_o, m_ffn_norm, m_w_up, m_ffn_conv, m_w_down, m_ple_norm, m_w_ple_gate, m_w_ple, m_final_norm, v_mix_norm_e, v_w_in_e, v_pool_w, v_pool_scale, v_w_out_e, v_mix_norm_o, v_w_in_o, v_conv_qkv_o, v_a_log_o, v_dt_bias_o, v_gdn_norm_o, v_w_out_o, v_ffn_norm, v_w_up, v_ffn_conv, v_w_down, v_ple_norm, v_w_ple_gate, v_w_ple, v_final_norm):
    args = locals()
    P = {n: args[n] for n in WEIGHT_ORDER}
    M = {n: args["m_" + n] for n in WEIGHT_ORDER}
    V = {n: args["v_" + n] for n in WEIGHT_ORDER}

    W, token, later_weights = _gather_weights(P)
    T = x.shape[1]
    ids = _where_ids()
    early = {}

    def on_grads(stage, grads):
        early[stage] = _reduce_begin(grads, ids, stage, travel_later=True)
        return early[stage]["token"]

    sq, grad_x, G = _local_step(x.reshape(T, D_MODEL), p.reshape(2, T, PLE_DIM), loss_target.reshape(T, D_MODEL), W,
                                token, later_weights, on_grads)
    last = _reduce_begin({("w_in_e", 0): G["w_in_e"], ("w_out_e", 0): G["w_out_e"]}, ids, "first_mixer",
                         travel_later=False)
    reduced = _reduce_end(last)
    for state in early.values():
        reduced.update(_reduce_end(state, after=grad_x))
    out = {}
    for n in BIG:
        halves = [reduced[(n, l)] for l in range(P[n].shape[0])]
        out[n] = _adamw_halves(P[n], M[n], V[n], [h[0] for h in halves], [h[1] for h in halves], ids, f"adamw_{n}")

    small_full = {n: G[n] for n in REPLICATED}
    small_full["pool_w"] = G["pool_w"][None]
    small_full["final_norm"] = G["final_norm"].reshape(D_MODEL)
    small_full["mix_norm_o"] = G["mix_norm_o"]
    small_full["conv_qkv_o"] = G["conv_qkv_o"][None]
    small_full["ffn_conv"] = G["ffn_conv"]
    small_names = REPLICATED + tuple(n for n, _ in SHARDED_SMALL)
    summed = _small_allreduce(_pack([small_full[n] for n in small_names] + [sq.reshape(1)], SMALL_W, 8), "ar_small")
    *g_list, sq_total = _unpack(summed, [small_full[n].shape for n in small_names] + [(1,)])
    g_small = dict(zip(small_names, g_list))
    chip = 2 * lax.axis_index("x") + lax.axis_index("y")
    for n, ax in SHARDED_SMALL:
        width = P[n].shape[ax]
        g_small[n] = lax.dynamic_slice_in_dim(g_small[n], chip * width, width, axis=ax)

    def pack_small(D):
        return _pack([D[n] for n in small_names], SMALL_W, RS_ROWS)[None]

    g_pack = pack_small(g_small)
    upd = _adamw(pack_small(P), g_pack, pack_small(M), pack_small(V), "adamw_small")
    shapes = [P[n].shape for n in small_names]
    for n, *vals in zip(small_names, *[_unpack(a[0], shapes) for a in (g_pack,) + tuple(upd)]):
        out[n] = tuple(vals)

    loss = (0.5 / D_MODEL) * sq_total[0]
    return (loss, grad_x[None]) + tuple(out[n][i] for i in range(4) for n in WEIGHT_ORDER)
```

```python
import functools

import jax
import jax.numpy as jnp
from jax import lax
from jax.experimental import pallas as pl
from jax.experimental.pallas import tpu as pltpu

F32 = jnp.float32
BF16 = jnp.bfloat16

D_MODEL = 1024
PLE_DIM = 256
POOL_WIDTH = 512
POOL_WINDOWS = (2, 4, 8, 16)
POOL_GROUP_DIM = 128
SB_HEADS = 8
SB_HEAD_DIM = 64
GDN_HEADS = 8
GDN_HEAD_DIM = 128
GDN_CONV = 4
GDN_CHUNK = 64
FFN_DIM = 2816
FFN_CONV = 3
EPS = 1e-6
ODD_IN = 4 * D_MODEL + 2 * GDN_HEADS
ODD_IN_PAD = 33 * 128
ADAM_LR, ADAM_B1, ADAM_B2, ADAM_EPS, ADAM_WD, ADAM_STEP = 0.001, 0.9, 0.999, 1e-08, 0.01, 10

LANE = 128
VMEM_LIMIT = 56 * 1024 * 1024

N_CHIPS = 4
N_DEV = 8


def _cp(sem=None):
    return pltpu.CompilerParams(dimension_semantics=sem, vmem_limit_bytes=VMEM_LIMIT)


def _tile(n, pref):
    if n <= pref:
        return n
    best = None
    for t in range(LANE, pref + 1, LANE):
        if n % t == 0:
            best = t
    assert best is not None, (n, pref)
    return best


_DIMS = {"nn": (((1,), (0,)), ((), ())), "nt": (((1,), (1,)), ((), ())), "tn": (((0,), (0,)), ((), ()))}
_BDIMS = {"nn": (((2,), (1,)), ((0,), (0,))), "nt": (((2,), (2,)), ((0,), (0,))), "tn": (((1,), (1,)), ((0,), (0,)))}


def _dims(mode, ndim):
    return (_BDIMS if ndim == 3 else _DIMS)[mode]


def _dot(a, b, mode="nn"):
    return lax.dot_general(a.astype(BF16), b.astype(BF16), _dims(mode, a.ndim), preferred_element_type=F32)


def _bdot(a, b, mode="nn"):
    return lax.dot_general(a.astype(BF16), b.astype(BF16), _BDIMS[mode], preferred_element_type=F32)


def _split2(x):
    hi = x.astype(BF16)
    lo = (x - hi.astype(F32)).astype(BF16)
    return hi, lo


def _split3(x):
    hi = x.astype(BF16)
    r = x - hi.astype(F32)
    mid = r.astype(BF16)
    lo = (r - mid.astype(F32)).astype(BF16)
    return hi, mid, lo


def _dot_x01(x, m01, mode="nn"):
    hi, lo = _split2(x)
    return (lax.dot_general(hi, m01, _DIMS[mode], preferred_element_type=F32)
            + lax.dot_general(lo, m01, _DIMS[mode], preferred_element_type=F32))


def _dot3_raw(a, b, mode):
    ah, al = _split2(a)
    bh, bl = _split2(b)
    d = _dims(mode, a.ndim)
    return (lax.dot_general(ah, bh, d, preferred_element_type=F32)
            + lax.dot_general(ah, bl, d, preferred_element_type=F32)
            + lax.dot_general(al, bh, d, preferred_element_type=F32))


@jax.custom_vjp
def _dot3(a, b):
    return _dot3_raw(a, b, "nn")


def _dot3_fwd(a, b):
    return _dot3_raw(a, b, "nn"), (a, b)


def _dot3_bwd(res, g):
    a, b = res
    return _dot(g, b, "nt"), _dot(a, g, "tn")


_dot3.defvjp(_dot3_fwd, _dot3_bwd)


@jax.custom_vjp
def _dot1_nt(a, b):
    return _dot(a, b, "nt")


def _dot1_nt_fwd(a, b):
    return _dot(a, b, "nt"), (a, b)


def _dot1_nt_bwd(res, g):
    a, b = res
    return _dot(g, b, "nn"), _dot(g, a, "tn")


_dot1_nt.defvjp(_dot1_nt_fwd, _dot1_nt_bwd)


def _m01_left_raw(m, x):
    d = _dims("nn", x.ndim)
    if x.ndim == 3:
        m = jnp.broadcast_to(m, (x.shape[0],) + m.shape)
    p0, p1, p2 = _split3(x)
    return (lax.dot_general(m, p0, d, preferred_element_type=F32)
            + lax.dot_general(m, p1, d, preferred_element_type=F32)
            + lax.dot_general(m, p2, d, preferred_element_type=F32))


@jax.custom_vjp
def _m01_left(m, mt, x):
    return _m01_left_raw(m, x)


def _m01_left_fwd(m, mt, x):
    return _m01_left_raw(m, x), (m, mt)


def _m01_left_bwd(res, g):
    m, mt = res
    return jnp.zeros_like(m), jnp.zeros_like(mt), _m01_left_raw(mt, g)


_m01_left.defvjp(_m01_left_fwd, _m01_left_bwd)


def _softplus(x):
    return jnp.maximum(x, 0.0) + jnp.log(1.0 + jnp.exp(-jnp.abs(x)))


def _sigmoid(x):
    return 0.5 * jnp.tanh(0.5 * x) + 0.5


def _silu(x):
    return x * _sigmoid(x)


def _dsilu(x):
    s = _sigmoid(x)
    return s * (1.0 + x * (1.0 - s))


def _cols_of(n_blocks_per_part, *fixed):
    return lambda r, c: (c // n_blocks_per_part,) + fixed + (r, c % n_blocks_per_part)


def _rows_of(n_blocks_per_part, *fixed):
    return lambda r, c: (r // n_blocks_per_part,) + fixed + (r % n_blocks_per_part, c)


def _layer_of(layer):
    return lambda r, c: (layer, r, c)


def _mm(a, b, mode, name, out_dtype=F32, res=None, tm=1024, tn=512, tk=1024,
        dims=None, a_view=None, b_view=None, o_view=None, out_shape=None):
    if dims is None:
        if mode == "nn":
            (M, K), (K2, N) = a.shape, b.shape
        elif mode == "nt":
            (M, K), (N, K2) = a.shape, b.shape
        else:
            (K, M), (K2, N) = a.shape, b.shape
        assert K == K2, (name, a.shape, b.shape)
    else:
        M, N, K = dims
    tm, tn, tk = _tile(M, tm), _tile(N, tn), _tile(K, tk)
    nk = K // tk

    def spec(arr, blk, view, rc):
        view = view or (lambda r, c: (r, c))
        return pl.BlockSpec((None,) * (arr.ndim - 2) + blk, lambda i, j, k: view(*rc(i, j, k)))

    if mode == "tn":
        a_spec = spec(a, (tk, tm), a_view, lambda i, j, k: (k, i))
    else:
        a_spec = spec(a, (tm, tk), a_view, lambda i, j, k: (i, k))
    if mode == "nt":
        b_spec = spec(b, (tn, tk), b_view, lambda i, j, k: (j, k))
    else:
        b_spec = spec(b, (tk, tn), b_view, lambda i, j, k: (k, j))
    out_shape = out_shape or (M, N)
    o_spec = pl.BlockSpec((None,) * (len(out_shape) - 2) + (tm, tn),
                          lambda i, j, k: (o_view or (lambda r, c: (r, c)))(i, j))
    has_res = res is not None
    assert not (has_res and o_view), name

    def body(*refs):
        a_ref, b_ref = refs[:2]
        r_ref = refs[2] if has_res else None
        o_ref = refs[3] if has_res else refs[2]

        def finish(r):
            if has_res:
                r = r + r_ref[...]
            o_ref[...] = r.astype(out_dtype)

        if nk == 1:
            finish(_dot(a_ref[...], b_ref[...], mode))
            return
        acc = refs[-1]
        k = pl.program_id(2)

        @pl.when(k == 0)
        def _():
            acc[...] = jnp.zeros_like(acc)

        acc[...] += _dot(a_ref[...], b_ref[...], mode)

        @pl.when(k == nk - 1)
        def _():
            finish(acc[...])

    ins = [a, b] + ([res] if has_res else [])
    in_specs = [a_spec, b_spec] + ([o_spec] if has_res else [])
    return pl.pallas_call(
        body, name=name, grid=(M // tm, N // tn, nk),
        in_specs=in_specs, out_specs=o_spec,
        out_shape=jax.ShapeDtypeStruct(out_shape, out_dtype),
        scratch_shapes=[pltpu.VMEM((tm, tn), F32)] if nk > 1 else [],
        compiler_params=_cp(("parallel", "parallel", "arbitrary")),
    )(*ins)


def _rms_fwd(x, gain, name, after=None):
    T, D = x.shape
    tt = _tile(T, 512)

    def body(x_ref, g_ref, *rest):
        o_ref = rest[-1]
        xv = x_ref[...]
        r = lax.rsqrt(jnp.mean(xv * xv, axis=-1, keepdims=True) + EPS)
        o_ref[...] = (xv * r * g_ref[...]).astype(BF16)

    ordered = [] if after is None else [after]
    return pl.pallas_call(
        body, name=name, grid=(T // tt,),
        in_specs=[pl.BlockSpec((tt, D), lambda i: (i, 0)), pl.BlockSpec((1, D), lambda i: (0, 0))]
        + [pl.BlockSpec((8, LANE), lambda i: (0, 0)) for _ in ordered],
        out_specs=pl.BlockSpec((tt, D), lambda i: (i, 0)),
        out_shape=jax.ShapeDtypeStruct((T, D), BF16),
        compiler_params=_cp(("parallel",)),
    )(x, gain, *ordered)


def _rms_bwd(x, gain, dh, dres, name):
    T, D = x.shape
    tt = _tile(T, 512)

    def body(x_ref, g_ref, dh_ref, dr_ref, dx_ref, dg_ref):
        i = pl.program_id(0)
        xv = x_ref[...]
        dy = dh_ref[...].astype(F32)
        r = lax.rsqrt(jnp.mean(xv * xv, axis=-1, keepdims=True) + EPS)
        xn = xv * r
        gdy = dy * g_ref[...]
        dx = r * (gdy - xn * jnp.mean(gdy * xn, axis=-1, keepdims=True))
        dx_ref[...] = dr_ref[...] + dx

        @pl.when(i == 0)
        def _():
            dg_ref[...] = jnp.zeros_like(dg_ref)

        dg_ref[...] += jnp.sum(dy * xn, axis=0, keepdims=True)

    row = pl.BlockSpec((tt, D), lambda i: (i, 0))
    vec = pl.BlockSpec((1, D), lambda i: (0, 0))
    return pl.pallas_call(
        body, name=name, grid=(T // tt,),
        in_specs=[row, vec, row, row], out_specs=[row, vec],
        out_shape=[jax.ShapeDtypeStruct((T, D), F32), jax.ShapeDtypeStruct((1, D), F32)],
        compiler_params=_cp(("arbitrary",)),
    )(x, gain, dh, dres)


def _final_loss(x, gain, target, name):
    T, D = x.shape
    tt = _tile(T, 512)

    def body(x_ref, g_ref, t_ref, l_ref, dx_ref, dg_ref):
        i = pl.program_id(0)
        xv = x_ref[...]
        r = lax.rsqrt(jnp.mean(xv * xv, axis=-1, keepdims=True) + EPS)
        xn = xv * r
        err = xn * g_ref[...] - t_ref[...]
        dy = err * (1.0 / D)
        gdy = dy * g_ref[...]
        dx_ref[...] = r * (gdy - xn * jnp.mean(gdy * xn, axis=-1, keepdims=True))

        @pl.when(i == 0)
        def _():
            dg_ref[...] = jnp.zeros_like(dg_ref)
            l_ref[...] = jnp.zeros_like(l_ref)

        dg_ref[...] += jnp.sum(dy * xn, axis=0, keepdims=True)
        l_ref[...] += jnp.sum(jnp.sum(err * err, axis=1, keepdims=True), axis=0, keepdims=True)

    row = pl.BlockSpec((tt, D), lambda i: (i, 0))
    vec = pl.BlockSpec((1, D), lambda i: (0, 0))
    return pl.pallas_call(
        body, name=name, grid=(T // tt,),
        in_specs=[row, vec, row],
        out_specs=[pl.BlockSpec((8, LANE), lambda i: (0, 0)), row, vec],
        out_shape=[jax.ShapeDtypeStruct((8, LANE), F32), jax.ShapeDtypeStruct((T, D), F32),
                   jax.ShapeDtypeStruct((1, D), F32)],
        compiler_params=_cp(("arbitrary",)),
    )(x, gain, target)


def _shift_down(x, i, t_idx):
    if i == 0:
        return x
    return jnp.where(t_idx >= i, pltpu.roll(x, i, 0), 0.0)


def _shift_up(x, i, t_idx):
    if i == 0:
        return x
    n = x.shape[0]
    return jnp.where(t_idx < n - i, pltpu.roll(x, n - i, 0), 0.0)


def _pool_select(g, vals):
    out = vals[-1]
    for gi in range(len(vals) - 2, -1, -1):
        out = jnp.where(g == gi, vals[gi], out)
    return out


def _pool_y(u, g, t_idx):
    s1 = u + _shift_down(u, 1, t_idx)
    s2 = s1 + _shift_down(s1, 2, t_idx)
    s3 = s2 + _shift_down(s2, 4, t_idx)
    s4 = s3 + _shift_down(s3, 8, t_idx)
    ws = _pool_select(g, [s1, s2, s3, s4])
    win = _pool_select(g, [jnp.float32(w) for w in POOL_WINDOWS])
    cnt = jnp.minimum(t_idx.astype(F32) + 1.0, win)
    return ws / cnt - u, cnt


def _pool_fwd(proj, pool_w, pool_scale):
    T = proj.shape[0]
    G, C = len(POOL_WINDOWS), POOL_GROUP_DIM

    def body(u_ref, w_ref, s_ref, o_ref):
        g = pl.program_id(0)
        t_idx = lax.broadcasted_iota(jnp.int32, (T, C), 0)
        y, _ = _pool_y(u_ref[...], g, t_idx)
        o_ref[...] = _dot(y, w_ref[0]) * s_ref[...]

    return pl.pallas_call(
        body, name="pool_fwd", grid=(G,),
        in_specs=[pl.BlockSpec((T, C), lambda g: (0, g)), pl.BlockSpec((1, C, C), lambda g: (g, 0, 0)),
                  pl.BlockSpec((1, C), lambda g: (0, g))],
        out_specs=pl.BlockSpec((T, C), lambda g: (0, g)),
        out_shape=jax.ShapeDtypeStruct((T, G * C), F32),
        compiler_params=_cp(("parallel",)),
    )(proj, pool_w, pool_scale)


def _pool_bwd(proj, pool_w, pool_scale, dmix):
    T = proj.shape[0]
    G, C = len(POOL_WINDOWS), POOL_GROUP_DIM

    def body(u_ref, w_ref, s_ref, do_ref, du_ref, dw_ref, ds_ref):
        g = pl.program_id(0)
        t_idx = lax.broadcasted_iota(jnp.int32, (T, C), 0)
        y, cnt = _pool_y(u_ref[...], g, t_idx)
        w = w_ref[0]
        dout = do_ref[...]
        ds_ref[...] = jnp.sum(dout * _dot(y, w), axis=0, keepdims=True)
        dy2 = dout * s_ref[...]
        dw_ref[0] = _dot(y, dy2, "tn")
        dy = _dot(dy2, w, "nt")
        dz = dy / cnt
        r1 = dz + _shift_up(dz, 1, t_idx)
        r2 = r1 + _shift_up(r1, 2, t_idx)
        r3 = r2 + _shift_up(r2, 4, t_idx)
        r4 = r3 + _shift_up(r3, 8, t_idx)
        du_ref[...] = _pool_select(g, [r1, r2, r3, r4]) - dy

    col = pl.BlockSpec((T, C), lambda g: (0, g))
    return pl.pallas_call(
        body, name="pool_bwd", grid=(G,),
        in_specs=[col, pl.BlockSpec((1, C, C), lambda g: (g, 0, 0)), pl.BlockSpec((1, C), lambda g: (0, g)), col],
        out_specs=[col, pl.BlockSpec((1, C, C), lambda g: (g, 0, 0)), pl.BlockSpec((1, C), lambda g: (0, g))],
        out_shape=[jax.ShapeDtypeStruct((T, G * C), F32), jax.ShapeDtypeStruct((G, C, C), F32),
                   jax.ShapeDtypeStruct((1, G * C), F32)],
        compiler_params=_cp(("parallel",)),
    )(proj, pool_w, pool_scale, dmix)


SB_SCALE = SB_HEAD_DIM ** -0.5
SB_PASS_SIZES = (4, 2, 1)
SB_PASS_SIZES_BWD = (2, 1)


def _sb_tile_logits(qb, kblk, valid):
    z = _dot(qb, kblk, "nt")
    sp = _softplus(z)
    l1m = -sp
    if valid is not None:
        l1m = jnp.where(valid, l1m, 0.0)
    return z, sp, l1m


SB_PAIR = LANE // SB_HEAD_DIM
SB_Q0 = POOL_WIDTH // LANE
SB_NB = SB_HEADS // SB_PAIR


def _sb_head_masks():
    lane = lax.broadcasted_iota(jnp.int32, (1, LANE), 1)
    return [(lane // SB_HEAD_DIM == h).astype(F32) for h in range(SB_PAIR)]


def _sb_fwd(proj):
    T = proj.shape[0]
    B = _tile(T, 256)
    nq = T // B

    def body(q_ref, k_ref, v_ref, o_ref, l_ref, k_bf, v_bf):
        qi = pl.program_id(1)

        @pl.when(qi == 0)
        def _():
            k_bf[...] = k_ref[...].astype(BF16)
            v_bf[...] = v_ref[...].astype(BF16)

        masks = _sb_head_masks()
        q_all = q_ref[...]
        qbs = [(q_all * (m * SB_SCALE)).astype(BF16) for m in masks]
        row = lax.broadcasted_iota(jnp.int32, (B, B), 0)
        col = lax.broadcasted_iota(jnp.int32, (B, B), 1)
        later = (row > col).astype(BF16)

        def tiles(kbs, state, valid):
            ksl = [pl.ds(pl.multiple_of(kb * B, B), B) for kb in kbs]
            kblks = [k_bf[ks, :] for ks in ksl]
            logits = [[_sb_tile_logits(qb, kblk, valid) for kblk in kblks] for qb in qbs]
            within = [[_dot_x01(l1m, later) for _, _, l1m in lg] for lg in logits]
            sums = [[jnp.sum(l1m, axis=1, keepdims=True) for _, _, l1m in lg] for lg in logits]
            out = []
            for h, (carry, acc) in enumerate(state):
                for (z, sp, _), rc, s, ks in zip(logits[h], within[h], sums[h], ksl):
                    a = jnp.exp(z - sp + rc + carry)
                    if valid is not None:
                        a = jnp.where(valid, a, 0.0)
                    acc = acc + _dot(a, v_bf[ks, :])
                    carry = carry + s
                out.append((carry, acc))
            return tuple(out)

        state = tiles([qi], ((jnp.zeros((B, 1), F32), jnp.zeros((B, LANE), F32)),) * SB_PAIR, col < row)
        left = qi
        for size in SB_PASS_SIZES:
            n_pass = left // size
            state = lax.fori_loop(
                0, n_pass, lambda i, c, left=left, size=size: tiles([left - 1 - size * i - u for u in range(size)],
                                                                     c, None), state)
            left = left - n_pass * size
        o_ref[...] = sum(acc * m for (_, acc), m in zip(state, masks))
        for h, (carry, _) in enumerate(state):
            l_ref[h] = carry

    return pl.pallas_call(
        body, name="sb_fwd", grid=(SB_NB, nq),
        in_specs=[pl.BlockSpec((B, LANE), lambda hp, i: (i, SB_Q0 + hp)),
                  pl.BlockSpec((T, LANE), lambda hp, i: (0, SB_Q0 + SB_NB + hp)),
                  pl.BlockSpec((T, LANE), lambda hp, i: (0, SB_Q0 + 2 * SB_NB + hp))],
        out_specs=[pl.BlockSpec((B, LANE), lambda hp, i: (i, hp)),
                   pl.BlockSpec((SB_PAIR, B, 1), lambda hp, i: (hp, i, 0))],
        out_shape=[jax.ShapeDtypeStruct((T, SB_HEADS * SB_HEAD_DIM), F32), jax.ShapeDtypeStruct((SB_HEADS, T, 1), F32)],
        scratch_shapes=[pltpu.VMEM((T, LANE), BF16), pltpu.VMEM((T, LANE), BF16)],
        compiler_params=_cp(("parallel", "arbitrary")),
    )(proj, proj, proj)


def _sb_bwd(proj, dmix, ltot, after=None):
    T = proj.shape[0]
    B = _tile(T, 256)
    nq = T // B
    ordered = [] if after is None else [after]

    def body(q_ref, k_ref, v_ref, do_ref, l_ref, *rest):
        dq_ref, dk_ref, dv_ref, k_bf, v_bf = rest[len(ordered):]
        qi = pl.program_id(1)

        @pl.when(qi == 0)
        def _():
            k_bf[...] = k_ref[...].astype(BF16)
            v_bf[...] = v_ref[...].astype(BF16)
            dk_ref[...] = jnp.zeros_like(dk_ref)
            dv_ref[...] = jnp.zeros_like(dv_ref)

        masks = _sb_head_masks()
        q_all, do_all = q_ref[...], do_ref[...]
        qbs = [(q_all * (m * SB_SCALE)).astype(BF16) for m in masks]
        dobs = [(do_all * m).astype(BF16) for m in masks]
        ltots = [l_ref[h] for h in range(SB_PAIR)]
        row = lax.broadcasted_iota(jnp.int32, (B, B), 0)
        col = lax.broadcasted_iota(jnp.int32, (B, B), 1)
        upto = (row <= col).astype(BF16)
        before = (row < col).astype(BF16)

        def tiles(kbs, state, valid):
            ksl = [pl.ds(pl.multiple_of(kb * B, B), B) for kb in kbs]
            kblks = [k_bf[ks, :] for ks in ksl]
            vblks = [v_bf[ks, :] for ks in ksl]
            logits = [[_sb_tile_logits(qb, kblk, valid) for kblk in kblks] for qb in qbs]
            das = [[_dot(dob, vblk, "nt") for vblk in vblks] for dob in dobs]
            within = [[_dot_x01(l1m, upto) for _, _, l1m in lg] for lg in logits]
            avals, es, Ps = [], [], []
            for h, (P, _, _) in enumerate(state):
                a_h, e_h = [], []
                for (z, sp, l1m), pc, da in zip(logits[h], within[h], das[h]):
                    a = jnp.exp(z - sp + (ltots[h] - P - pc))
                    if valid is not None:
                        a = jnp.where(valid, a, 0.0)
                    a_h.append(a)
                    e_h.append(da * a)
                    P = P + jnp.sum(l1m, axis=1, keepdims=True)
                avals.append(a_h)
                es.append(e_h)
                Ps.append(P)
            e_within = [[_dot_x01(e, before) for e in e_h] for e_h in es]
            out = []
            for h, (_, E, dq) in enumerate(state):
                for (z, sp, _), e, ew, a, kblk, ks in zip(logits[h], es[h], e_within[h], avals[h], kblks, ksl):
                    dz = e * jnp.exp(-sp) - jnp.exp(z - sp) * (ew + E)
                    if valid is not None:
                        dz = jnp.where(valid, dz, 0.0)
                    dzb = dz.astype(BF16)
                    dq = dq + _dot(dzb, kblk)
                    dk_ref[ks, :] += _dot(dzb, qbs[h], "tn")
                    dv_ref[ks, :] += _dot(a, dobs[h], "tn")
                    E = E + jnp.sum(e, axis=1, keepdims=True)
                out.append((Ps[h], E, dq))
            return tuple(out)

        zeros1 = jnp.zeros((B, 1), F32)
        state = ((zeros1, zeros1, jnp.zeros((B, LANE), F32)),) * SB_PAIR
        done = 0
        for size in SB_PASS_SIZES_BWD:
            n_pass = (qi - done) // size
            state = lax.fori_loop(
                0, n_pass, lambda i, c, done=done, size=size: tiles([done + size * i + u for u in range(size)], c, None),
                state)
            done = done + n_pass * size
        state = tiles([qi], state, col < row)
        dq_ref[...] = sum(dq * (m * SB_SCALE) for (_, _, dq), m in zip(state, masks))

    qspec = pl.BlockSpec((B, LANE), lambda hp, i: (i, SB_Q0 + hp))
    wide = jax.ShapeDtypeStruct((T, SB_HEADS * SB_HEAD_DIM), F32)
    return pl.pallas_call(
        body, name="sb_bwd", grid=(SB_NB, nq),
        in_specs=[qspec,
                  pl.BlockSpec((T, LANE), lambda hp, i: (0, SB_Q0 + SB_NB + hp)),
                  pl.BlockSpec((T, LANE), lambda hp, i: (0, SB_Q0 + 2 * SB_NB + hp)),
                  qspec,
                  pl.BlockSpec((SB_PAIR, B, 1), lambda hp, i: (hp, i, 0))]
        + [pl.BlockSpec((8, LANE), lambda hp, i: (0, 0)) for _ in ordered],
        out_specs=[pl.BlockSpec((B, LANE), lambda hp, i: (i, hp)),
                   pl.BlockSpec((T, LANE), lambda hp, i: (0, hp)),
                   pl.BlockSpec((T, LANE), lambda hp, i: (0, hp))],
        out_shape=[wide, wide, wide],
        scratch_shapes=[pltpu.VMEM((T, LANE), BF16), pltpu.VMEM((T, LANE), BF16)],
        compiler_params=_cp(("parallel", "arbitrary")),
    )(proj, proj, proj, dmix, ltot, *ordered)


def _rows(w_ref, K):
    return [w_ref[i:i + 1, :] for i in range(K)]


def _conv(x, ws, t_idx):
    K = len(ws)
    y = ws[K - 1] * x
    for i in range(K - 1):
        y = y + ws[i] * _shift_down(x, K - 1 - i, t_idx)
    return y


def _conv_bwd(x, ws, dy, t_idx):
    K = len(ws)
    dx = ws[K - 1] * dy
    dws = []
    for i in range(K - 1):
        dx = dx + ws[i] * _shift_up(dy, K - 1 - i, t_idx)
        dws.append(jnp.sum(dy * _shift_down(x, K - 1 - i, t_idx), axis=0, keepdims=True))
    dws.append(jnp.sum(dy * x, axis=0, keepdims=True))
    return dx, dws


def _store_rows(ref, rows):
    for i, r in enumerate(rows):
        ref[i:i + 1, :] = r


CONV_ROWS = 64


def _ffn_act_fwd(up, conv_w, name):
    T = up.shape[0]
    F = FFN_DIM
    nb = F // LANE

    def body(g_ref, v_ref, wg_ref, wv_ref, o_ref):
        t_idx = lax.broadcasted_iota(jnp.int32, (T, LANE), 0)
        cg = _conv(g_ref[...].astype(F32), _rows(wg_ref, FFN_CONV), t_idx)
        cv = _conv(v_ref[...].astype(F32), _rows(wv_ref, FFN_CONV), t_idx)
        o_ref[...] = (_silu(cg) * cv).astype(BF16)

    return pl.pallas_call(
        body, name=name, grid=(nb,),
        in_specs=[pl.BlockSpec((T, LANE), lambda j: (0, j)), pl.BlockSpec((T, LANE), lambda j: (0, j + nb)),
                  pl.BlockSpec((FFN_CONV, LANE), lambda j: (0, j)),
                  pl.BlockSpec((FFN_CONV, LANE), lambda j: (0, j + nb))],
        out_specs=pl.BlockSpec((T, LANE), lambda j: (0, j)),
        out_shape=jax.ShapeDtypeStruct((T, F), BF16),
        compiler_params=_cp(("parallel",)),
    )(up, up, conv_w, conv_w)


def _ffn_act_bwd(up, conv_w, dact, name):
    T = up.shape[0]
    F = FFN_DIM
    nb = F // LANE

    K = FFN_CONV
    R = CONV_ROWS
    assert T % R == 0, T
    n_chunks = T // R
    PAD = 8

    def body(g_ref, v_ref, wg_ref, wv_ref, da_ref, dup_ref, dwg_ref, dwv_ref, xg_s, xv_s, dyg_s, dyv_s):
        zeros = jnp.zeros((PAD, LANE), F32)
        for s in (xg_s, xv_s, dyg_s, dyv_s):
            s[0:PAD, :] = zeros
            s[T + PAD:T + 2 * PAD, :] = zeros
        xg_s[PAD:T + PAD, :] = g_ref[...].astype(F32)
        xv_s[PAD:T + PAD, :] = v_ref[...].astype(F32)
        wg, wv = _rows(wg_ref, K), _rows(wv_ref, K)

        def window(ext, shift):
            if shift == 0:
                return ext[PAD:PAD + R, :]
            return pltpu.roll(ext, shift % (R + 2 * PAD), 0)[PAD:PAD + R, :]

        def forward(c, carry):
            r0 = pl.multiple_of(c * R, R)
            ge, ve = xg_s[pl.ds(r0, R + 2 * PAD), :], xv_s[pl.ds(r0, R + 2 * PAD), :]
            gw = [window(ge, K - 1 - i) for i in range(K)]
            vw = [window(ve, K - 1 - i) for i in range(K)]
            cg = sum(w * x for w, x in zip(wg, gw))
            cv = sum(w * x for w, x in zip(wv, vw))
            da = da_ref[pl.ds(r0, R), :].astype(F32)
            sg = _sigmoid(cg)
            dyg = da * cv * (sg * (1.0 + cg * (1.0 - sg)))
            dyv = da * (cg * sg)
            dyg_s[pl.ds(pl.multiple_of(r0 + PAD, PAD), R), :] = dyg
            dyv_s[pl.ds(pl.multiple_of(r0 + PAD, PAD), R), :] = dyv
            return tuple(acc + jnp.sum(dy * x, axis=0, keepdims=True)
                         for acc, (dy, x) in zip(carry, [(dyg, x) for x in gw] + [(dyv, x) for x in vw]))

        sums = lax.fori_loop(0, n_chunks, forward, (jnp.zeros((1, LANE), F32),) * (2 * K))
        _store_rows(dwg_ref, sums[:K])
        _store_rows(dwv_ref, sums[K:])

        def backward(c, carry):
            r0 = pl.multiple_of(c * R, R)
            ge, ve = dyg_s[pl.ds(r0, R + 2 * PAD), :], dyv_s[pl.ds(r0, R + 2 * PAD), :]
            dxg = sum(w * window(ge, -(K - 1 - i)) for i, w in enumerate(wg))
            dxv = sum(w * window(ve, -(K - 1 - i)) for i, w in enumerate(wv))
            dup_ref[0, pl.ds(r0, R), :] = dxg.astype(BF16)
            dup_ref[1, pl.ds(r0, R), :] = dxv.astype(BF16)
            return carry

        lax.fori_loop(0, n_chunks, backward, 0)

    col = pl.BlockSpec((T, LANE), lambda j: (0, j))
    wcol = pl.BlockSpec((FFN_CONV, LANE), lambda j: (0, j))
    return pl.pallas_call(
        body, name=name, grid=(nb,),
        in_specs=[col, pl.BlockSpec((T, LANE), lambda j: (0, j + nb)), wcol,
                  pl.BlockSpec((FFN_CONV, LANE), lambda j: (0, j + nb)), col],
        out_specs=[pl.BlockSpec((2, T, LANE), lambda j: (0, 0, j)), wcol, wcol],
        out_shape=[jax.ShapeDtypeStruct((2, T, F), BF16),
                   jax.ShapeDtypeStruct((FFN_CONV, F), F32), jax.ShapeDtypeStruct((FFN_CONV, F), F32)],
        scratch_shapes=[pltpu.VMEM((T + 2 * PAD, LANE), F32)] * 4,
        compiler_params=_cp(("parallel",)),
    )(up, up, conv_w, conv_w, dact)


N_QK_BLOCKS = 2 * GDN_HEADS


def _gdn_pre_fwd(proj, conv_w):
    T = proj.shape[0]
    nb = 3 * GDN_HEADS

    def body(x_ref, w_ref, o_ref):
        j = pl.program_id(0)
        t_idx = lax.broadcasted_iota(jnp.int32, (T, LANE), 0)
        s = _silu(_conv(x_ref[...], _rows(w_ref, GDN_CONV), t_idx))
        rn = lax.rsqrt(jnp.sum(s * s, axis=-1, keepdims=True) + EPS)
        o_ref[...] = s * jnp.where(j < N_QK_BLOCKS, rn, 1.0)

    return pl.pallas_call(
        body, name="gdn_pre_fwd", grid=(nb,),
        in_specs=[pl.BlockSpec((T, LANE), lambda j: (0, j)), pl.BlockSpec((GDN_CONV, LANE), lambda j: (0, j))],
        out_specs=pl.BlockSpec((T, LANE), lambda j: (0, j)),
        out_shape=jax.ShapeDtypeStruct((T, nb * LANE), F32),
        compiler_params=_cp(("parallel",)),
    )(proj, conv_w)


def _gdn_pre_bwd(proj, conv_w, dout):
    T = proj.shape[0]
    nb = 3 * GDN_HEADS
    H = GDN_HEADS

    def body(x_ref, w_ref, do_ref, dx_ref, dw_ref):
        j = pl.program_id(0)
        t_idx = lax.broadcasted_iota(jnp.int32, (T, LANE), 0)
        x, w = x_ref[...], _rows(w_ref, GDN_CONV)
        c = _conv(x, w, t_idx)
        s = _silu(c)
        rn = lax.rsqrt(jnp.sum(s * s, axis=-1, keepdims=True) + EPS)
        do = do_ref[...]
        y = s * rn
        ds_normed = rn * (do - y * jnp.sum(do * y, axis=-1, keepdims=True))
        ds = jnp.where(j < N_QK_BLOCKS, ds_normed, do)
        dx, dw = _conv_bwd(x, w, ds * _dsilu(c), t_idx)
        dx_ref[...] = dx.astype(BF16)
        _store_rows(dw_ref, dw)

    col = pl.BlockSpec((T, LANE), lambda j: (0, j))
    wcol = pl.BlockSpec((GDN_CONV, LANE), lambda j: (0, j))
    return pl.pallas_call(
        body, name="gdn_pre_bwd", grid=(nb,),
        in_specs=[col, wcol, pl.BlockSpec((None, None, T, LANE), lambda j: (j // H, j % H, 0, 0))],
        out_specs=[col, wcol],
        out_shape=[jax.ShapeDtypeStruct((T, nb * LANE), BF16), jax.ShapeDtypeStruct((GDN_CONV, nb * LANE), F32)],
        compiler_params=_cp(("parallel",)),
    )(proj, conv_w, dout)


def _gdn_consts():
    C = GDN_CHUNK
    r = lax.broadcasted_iota(jnp.int32, (C, C), 0)
    c = lax.broadcasted_iota(jnp.int32, (C, C), 1)
    return dict(incl=r >= c, strict=r > c, eye=(r == c).astype(F32),
                low=(r >= c).astype(BF16), up=(r <= c).astype(BF16), ones=jnp.ones((C, C), BF16))


def _unit_lower_inverse_raw(a_mat, eye):
    inv = eye - a_mat
    pw = _dot3_raw(a_mat, a_mat, "nn")
    n_factors = a_mat.shape[-1].bit_length() - 2
    for f in range(n_factors):
        inv = inv + _dot3_raw(inv, pw, "nn")
        if f < n_factors - 1:
            pw = _dot3_raw(pw, pw, "nn")
    return inv


@jax.custom_vjp
def _unit_lower_inverse(a_mat, eye):
    return _unit_lower_inverse_raw(a_mat, eye)


def _unit_lower_inverse_fwd(a_mat, eye):
    inv = _unit_lower_inverse_raw(a_mat, eye)
    return inv, (inv, eye)


def _unit_lower_inverse_bwd(res, g):
    inv, eye = res
    return -_dot(_dot(inv, g, "tn"), inv, "nt"), jnp.zeros_like(eye)


_unit_lower_inverse.defvjp(_unit_lower_inverse_fwd, _unit_lower_inverse_bwd)


def _gdn_prep_chunk(q, k, v, b, a, alog, dtb, cs):
    n, C, dk = q.shape
    beta = _sigmoid(b)
    g = -jnp.exp(alog) * _softplus(a + dtb)
    g_sq = jnp.broadcast_to(g, (n, C, C))
    g_wide = jnp.broadcast_to(g, (n, C, dk))
    gc_i = _m01_left(cs["low"], cs["up"], g_sq)
    gc_j = _m01_left(cs["ones"], cs["ones"], g_sq * cs["up"].astype(F32))
    gc_wide = _m01_left(cs["low"], cs["up"], g_wide)
    gl_wide = _m01_left(cs["ones"], cs["ones"], g_wide)
    decay = jnp.where(cs["incl"], jnp.exp(jnp.where(cs["incl"], gc_i - gc_j, 0.0)), 0.0)
    egc = jnp.exp(gc_wide)
    qs = q * (dk ** -0.5)
    k_beta = k * beta
    a_mat = jnp.where(cs["strict"], _dot1_nt(k_beta, k) * decay, 0.0)
    inv = _unit_lower_inverse(a_mat, cs["eye"])
    u = _dot3(inv, v * beta)
    w = _dot3(inv, k_beta * egc)
    qk = _dot1_nt(qs, k) * decay
    q_dec = qs * egc
    k_dec = k * jnp.exp(gl_wide - gc_wide)
    g_last = jnp.exp(gl_wide)[:, 0:8, :]
    return qk, u, w, q_dec, k_dec, g_last


GDN_PREP_CHUNKS = 16
GDN_BA_BLOCK = 4 * D_MODEL // LANE


def _gdn_prep_specs(T):
    C, dk = GDN_CHUNK, GDN_HEAD_DIM
    npc = min(GDN_PREP_CHUNKS, T // C)
    tc = npc * C
    H = GDN_HEADS
    in_specs = [pl.BlockSpec((tc, dk), lambda i, h: (i, h)),
                pl.BlockSpec((tc, dk), lambda i, h: (i, H + h)),
                pl.BlockSpec((tc, dk), lambda i, h: (i, 2 * H + h)),
                pl.BlockSpec((tc, dk), lambda i, h: (i, GDN_BA_BLOCK)),
                pl.BlockSpec((8, dk), lambda i, h: (0, 0))]
    xs_specs = [pl.BlockSpec((1, tc, C), lambda i, h: (h, i, 0)),
                pl.BlockSpec((1, tc, dk), lambda i, h: (h, i, 0)),
                pl.BlockSpec((1, tc, dk), lambda i, h: (h, i, 0)),
                pl.BlockSpec((1, tc, dk), lambda i, h: (h, i, 0)),
                pl.BlockSpec((1, tc, dk), lambda i, h: (h, i, 0)),
                pl.BlockSpec((1, npc * 8, dk), lambda i, h: (h, i, 0))]
    xs_shapes = [jax.ShapeDtypeStruct((H, T, C), F32)] + [jax.ShapeDtypeStruct((H, T, dk), F32)] * 4 + [
        jax.ShapeDtypeStruct((H, 8 * T // C, dk), F32)]
    return npc, tc, in_specs, xs_specs, xs_shapes


def _lane_pick(x, lane, j):
    return jnp.sum(jnp.where(lane == j, x, 0.0), axis=1, keepdims=True)


def _gdn_head_gates(ba_ref, gates_ref, h, npc):
    lane = lax.broadcasted_iota(jnp.int32, (1, GDN_HEAD_DIM), 1)
    ba = ba_ref[...]
    b = _lane_pick(ba, lane, h).reshape(npc, GDN_CHUNK, 1)
    a = _lane_pick(ba, lane, GDN_HEADS + h).reshape(npc, GDN_CHUNK, 1)
    return b, a, _lane_pick(gates_ref[0:1, :], lane, h), _lane_pick(gates_ref[1:2, :], lane, h), lane


def _gdn_prep_fwd(qkv, proj, gates):
    T = qkv.shape[0]
    C = GDN_CHUNK
    npc, tc, in_specs, xs_specs, xs_shapes = _gdn_prep_specs(T)

    def body(q_ref, k_ref, v_ref, ba_ref, gates_ref, qk_ref, u_ref, w_ref, qd_ref, kd_ref, gl_ref):
        cs = _gdn_consts()
        b, a, alog, dtb, _ = _gdn_head_gates(ba_ref, gates_ref, pl.program_id(1), npc)

        def chunks(val):
            return val.reshape(npc, C, val.shape[-1])

        outs = _gdn_prep_chunk(chunks(q_ref[...]), chunks(k_ref[...]), chunks(v_ref[...]), b, a, alog, dtb, cs)
        for ref, val in zip((qk_ref, u_ref, w_ref, qd_ref, kd_ref), outs[:5]):
            ref[0] = val.reshape(tc, val.shape[-1])
        gl_ref[0] = outs[5].reshape(npc * 8, outs[5].shape[-1])

    return pl.pallas_call(
        body, name="gdn_prep_fwd", grid=(T // tc, GDN_HEADS),
        in_specs=in_specs, out_specs=xs_specs, out_shape=xs_shapes,
        compiler_params=_cp(("parallel", "parallel")),
    )(qkv, qkv, qkv, proj, gates)


def _gdn_prep_bwd(qkv, proj, gates, dxs):
    T = qkv.shape[0]
    C, dk, H = GDN_CHUNK, GDN_HEAD_DIM, GDN_HEADS
    npc, tc, in_specs, xs_specs, _ = _gdn_prep_specs(T)

    def body(q_ref, k_ref, v_ref, ba_ref, gates_ref, dqk_ref, du_ref, dw_ref, dqd_ref, dkd_ref, dgl_ref,
             dqkv_ref, dba_ref, dgates_ref):
        i, h = pl.program_id(0), pl.program_id(1)
        cs = _gdn_consts()
        r8 = lax.broadcasted_iota(jnp.int32, (8, dk), 0)
        c8 = lax.broadcasted_iota(jnp.int32, (8, dk), 1)
        first = (r8 == 0) & (c8 == 0)

        @pl.when((i == 0) & (h == 0))
        def _():
            dgates_ref[...] = jnp.zeros_like(dgates_ref)

        @pl.when(h == 0)
        def _():
            dba_ref[...] = jnp.zeros_like(dba_ref)

        def chunks(val):
            return val.reshape(npc, C, val.shape[-1])

        b, a, alog, dtb, lane = _gdn_head_gates(ba_ref, gates_ref, h, npc)
        prim = (chunks(q_ref[...]), chunks(k_ref[...]), chunks(v_ref[...]), b, a, alog, dtb)
        _, vjp = jax.vjp(lambda *p: _gdn_prep_chunk(*p, cs), *prim)
        dgl = jnp.where(first, dgl_ref[0].reshape(npc, 8, dk), 0.0)
        cts = tuple(chunks(r[0]) for r in (dqk_ref, du_ref, dw_ref, dqd_ref, dkd_ref)) + (dgl,)
        dq, dkk, dv, db, da, dal, ddt = vjp(cts)
        for part, val in enumerate((dq, dkk, dv)):
            dqkv_ref[part, 0] = val.reshape(tc, dk)
        dba_ref[...] += (jnp.where(lane == h, db.reshape(tc, 1), 0.0)
                         + jnp.where(lane == H + h, da.reshape(tc, 1), 0.0))
        dgates_ref[0:1, :] += jnp.where(lane == h, dal, 0.0)
        dgates_ref[1:2, :] += jnp.where(lane == h, ddt, 0.0)

    return pl.pallas_call(
        body, name="gdn_prep_bwd", grid=(T // tc, H),
        in_specs=in_specs + xs_specs,
        out_specs=[pl.BlockSpec((3, 1, tc, dk), lambda i, h: (0, h, i, 0)), pl.BlockSpec((tc, dk), lambda i, h: (i, 0)),
                   pl.BlockSpec((8, dk), lambda i, h: (0, 0))],
        out_shape=[jax.ShapeDtypeStruct((3, H, T, dk), F32), jax.ShapeDtypeStruct((T, dk), F32),
                   jax.ShapeDtypeStruct((8, dk), F32)],
        compiler_params=_cp(("arbitrary", "arbitrary")),
    )(qkv, qkv, qkv, proj, gates, *dxs)


def _gdn_scan_specs(T):
    C, dk, H = GDN_CHUNK, GDN_HEAD_DIM, GDN_HEADS
    return [pl.BlockSpec((H, C, C), lambda n: (0, n, 0))] + [pl.BlockSpec((H, C, dk), lambda n: (0, n, 0))] * 4 + [
        pl.BlockSpec((H, 8, dk), lambda n: (0, n, 0))]


def _gdn_scan_fwd(xs):
    H, T, dk = xs[1].shape
    C = GDN_CHUNK
    n = T // C

    def body(qk_ref, u_ref, w_ref, qd_ref, kd_ref, gl_ref, o_ref, s_ref, state):
        c = pl.program_id(0)

        @pl.when(c == 0)
        def _():
            state[...] = jnp.zeros_like(state)

        S = state[...]
        s_ref[0] = S
        v_new = u_ref[...] - _bdot(w_ref[...], S)
        o_ref[...] = _bdot(qd_ref[...], S) + _bdot(qk_ref[...], v_new)
        state[...] = S * jnp.tile(gl_ref[...], (1, dk // 8, 1)) + _bdot(kd_ref[...], v_new, "tn")

    return pl.pallas_call(
        body, name="gdn_scan_fwd", grid=(n,),
        in_specs=_gdn_scan_specs(T),
        out_specs=[pl.BlockSpec((H, C, dk), lambda n: (0, n, 0)), pl.BlockSpec((1, H, dk, dk), lambda n: (n, 0, 0, 0))],
        out_shape=[jax.ShapeDtypeStruct((H, T, dk), F32), jax.ShapeDtypeStruct((n, H, dk, dk), F32)],
        scratch_shapes=[pltpu.VMEM((H, dk, dk), F32)],
        compiler_params=_cp(("arbitrary",)),
    )(*xs)


def _gdn_scan_bwd(xs, states, do):
    H, T, dk = xs[1].shape
    C = GDN_CHUNK
    n = T // C

    def rev(spec_shape, f):
        return pl.BlockSpec(spec_shape, lambda i: f(n - 1 - i))

    def body(qk_ref, u_ref, w_ref, qd_ref, kd_ref, gl_ref, s_ref, do_ref,
             dqk_ref, du_ref, dw_ref, dqd_ref, dkd_ref, dgl_ref, dstate):
        i = pl.program_id(0)

        @pl.when(i == 0)
        def _():
            dstate[...] = jnp.zeros_like(dstate)

        S = s_ref[0]
        dS = dstate[...]
        do_v = do_ref[...]
        qk, w, qd, kd = qk_ref[...], w_ref[...], qd_ref[...], kd_ref[...]
        v_new = u_ref[...] - _bdot(w, S)
        dv_new = _bdot(qk, do_v, "tn") + _bdot(kd, dS)
        dqk_ref[...] = _bdot(do_v, v_new, "nt")
        dqd_ref[...] = _bdot(do_v, S, "nt")
        dkd_ref[...] = _bdot(v_new, dS, "nt")
        du_ref[...] = dv_new
        dw_ref[...] = -_bdot(dv_new, S, "nt")
        dgl = jnp.sum(jnp.sum(S * dS, axis=2, keepdims=True), axis=1, keepdims=True)
        dgl_ref[...] = jnp.broadcast_to(dgl, dgl_ref.shape)
        dstate[...] = (dS * jnp.tile(gl_ref[...], (1, dk // 8, 1)) + _bdot(qd, do_v, "tn")
                       - _bdot(w, dv_new, "tn"))

    in_specs = [rev((H, C, C), lambda m: (0, m, 0))] + [rev((H, C, dk), lambda m: (0, m, 0))] * 4 + [
        rev((H, 8, dk), lambda m: (0, m, 0)), rev((1, H, dk, dk), lambda m: (m, 0, 0, 0)),
        rev((H, C, dk), lambda m: (0, m, 0))]
    out_specs = [rev((H, C, C), lambda m: (0, m, 0))] + [rev((H, C, dk), lambda m: (0, m, 0))] * 4 + [
        rev((H, 8, dk), lambda m: (0, m, 0))]
    out_shape = [jax.ShapeDtypeStruct((H, T, C), F32)] + [jax.ShapeDtypeStruct((H, T, dk), F32)] * 4 + [
        jax.ShapeDtypeStruct((H, 8 * n, dk), F32)]
    return pl.pallas_call(
        body, name="gdn_scan_bwd", grid=(n,),
        in_specs=in_specs, out_specs=out_specs, out_shape=out_shape,
        scratch_shapes=[pltpu.VMEM((H, dk, dk), F32)],
        compiler_params=_cp(("arbitrary",)),
    )(*xs, states, do)


def _gdn_post_fwd(o, proj, norm_w):
    H, T, dk = o.shape
    tt = _tile(T, 1024)
    zoff = 3 * GDN_HEADS

    def body(o_ref, z_ref, g_ref, y_ref):
        ov = o_ref[0]
        r = lax.rsqrt(jnp.mean(ov * ov, axis=-1, keepdims=True) + EPS)
        y_ref[...] = (ov * r * g_ref[...] * _silu(z_ref[...])).astype(BF16)

    return pl.pallas_call(
        body, name="gdn_post_fwd", grid=(H, T // tt),
        in_specs=[pl.BlockSpec((1, tt, dk), lambda h, i: (h, i, 0)), pl.BlockSpec((tt, dk), lambda h, i: (i, zoff + h)),
                  pl.BlockSpec((1, dk), lambda h, i: (0, 0))],
        out_specs=pl.BlockSpec((tt, dk), lambda h, i: (i, h)),
        out_shape=jax.ShapeDtypeStruct((T, H * dk), BF16),
        compiler_params=_cp(("parallel", "parallel")),
    )(o, proj, norm_w)


def _gdn_post_bwd(o, proj, norm_w, dy):
    H, T, dk = o.shape
    tt = _tile(T, 1024)
    zoff = 3 * GDN_HEADS

    def body(o_ref, z_ref, g_ref, dy_ref, do_ref, dz_ref, dg_ref):
        i = pl.program_id(1)
        ov, z, g, dyv = o_ref[0], z_ref[...], g_ref[...], dy_ref[...]
        r = lax.rsqrt(jnp.mean(ov * ov, axis=-1, keepdims=True) + EPS)
        on = ov * r
        sz = _silu(z)
        dz_ref[...] = (dyv * on * g * _dsilu(z)).astype(BF16)
        dn = dyv * sz
        gdn = dn * g
        do_ref[0] = r * (gdn - on * jnp.mean(gdn * on, axis=-1, keepdims=True))

        @pl.when(i == 0)
        def _():
            dg_ref[...] = jnp.zeros_like(dg_ref)

        dg_ref[0] += jnp.sum(dn * on, axis=0, keepdims=True)

    return pl.pallas_call(
        body, name="gdn_post_bwd", grid=(H, T // tt),
        in_specs=[pl.BlockSpec((1, tt, dk), lambda h, i: (h, i, 0)), pl.BlockSpec((tt, dk), lambda h, i: (i, zoff + h)),
                  pl.BlockSpec((1, dk), lambda h, i: (0, 0)), pl.BlockSpec((tt, dk), lambda h, i: (i, h))],
        out_specs=[pl.BlockSpec((1, tt, dk), lambda h, i: (h, i, 0)), pl.BlockSpec((tt, dk), lambda h, i: (i, h)),
                   pl.BlockSpec((1, 1, dk), lambda h, i: (h, 0, 0))],
        out_shape=[jax.ShapeDtypeStruct((H, T, dk), F32), jax.ShapeDtypeStruct((T, H * dk), BF16),
                   jax.ShapeDtypeStruct((H, 1, dk), F32)],
        compiler_params=_cp(("parallel", "arbitrary")),
    )(o, proj, norm_w, dy)


def _ple_fwd(x, pp, gl, name):
    T, D = x.shape
    tt = _tile(T, 512)

    def body(x_ref, p_ref, g_ref, o_ref):
        o_ref[...] = x_ref[...] + p_ref[...] * _sigmoid(g_ref[...])

    row = pl.BlockSpec((tt, D), lambda i: (i, 0))
    return pl.pallas_call(
        body, name=name, grid=(T // tt,), in_specs=[row, row, row], out_specs=row,
        out_shape=jax.ShapeDtypeStruct((T, D), F32), compiler_params=_cp(("parallel",)),
    )(x, pp, gl)


def _ple_bwd(dx, pp, gl, name, after=None):
    T, D = dx.shape
    tt = _tile(T, 512)

    def body(dx_ref, p_ref, g_ref, *rest):
        dp_ref, dg_ref = rest[-2:]
        s = _sigmoid(g_ref[...])
        dxv = dx_ref[...]
        dp_ref[...] = (dxv * s).astype(BF16)
        dg_ref[...] = (dxv * p_ref[...] * s * (1.0 - s)).astype(BF16)

    row = pl.BlockSpec((tt, D), lambda i: (i, 0))
    ordered = [] if after is None else [after]
    return pl.pallas_call(
        body, name=name, grid=(T // tt,),
        in_specs=[row, row, row] + [pl.BlockSpec((8, LANE), lambda i: (0, 0)) for _ in ordered], out_specs=[row, row],
        out_shape=[jax.ShapeDtypeStruct((T, D), BF16)] * 2, compiler_params=_cp(("parallel",)),
    )(dx, pp, gl, *ordered)


UP_SHARD = 2 * FFN_DIM // N_CHIPS
DOWN_SHARD = FFN_DIM // N_CHIPS
GATE_SHARD = D_MODEL // N_CHIPS
IN_E_SHARD = 2 * D_MODEL // N_CHIPS


def _ffn_fwd(x, norm, W, conv_w, l):
    T = x.shape[0]
    hf = _rms_fwd(x, norm, f"ffn_norm{l}")
    up = _mm(hf, W["w_up"], "nn", f"ffn_up{l}", dims=(T, 2 * FFN_DIM, D_MODEL), b_view=_cols_of(1, l), tn=UP_SHARD,
             out_dtype=BF16)
    act = _ffn_act_fwd(up, conv_w, f"ffn_act{l}")
    x_out = _mm(act, W["w_down"], "nn", f"ffn_down{l}", dims=(T, D_MODEL, FFN_DIM), b_view=_layer_of(l), res=x,
                tn=1024, tk=1408)
    return x_out, (x, hf, up, act)


def _ffn_bwd(dx_out, saved, norm, W, conv_w, l):
    x, hf, up, act = saved
    T = x.shape[0]
    dact = _mm(dx_out, W["w_down"], "nt", f"ffn_dact{l}", dims=(T, FFN_DIM, D_MODEL), b_view=_layer_of(l),
               out_dtype=BF16, tn=1408)
    dw_down = _mm(act, dx_out, "tn", f"ffn_dwdown{l}", tm=1408, tn=1024, tk=512)
    dup, dcw_g, dcw_v = _ffn_act_bwd(up, conv_w, dact, f"ffn_dact_conv{l}")
    dw_up = _mm(hf, dup, "tn", f"ffn_dwup{l}", dims=(D_MODEL, 2 * FFN_DIM, T), b_view=_cols_of(FFN_DIM // UP_SHARD),
                o_view=_cols_of(1), out_shape=(N_CHIPS, D_MODEL, UP_SHARD), tm=1024, tn=UP_SHARD, tk=512)
    dhf = _mm(dup, W["w_up"], "nt", f"ffn_dhf{l}", dims=(T, D_MODEL, 2 * FFN_DIM),
              a_view=_cols_of(FFN_DIM // UP_SHARD), b_view=_cols_of(1, l), tn=1024, tk=UP_SHARD)
    dx, dnorm = _rms_bwd(x, norm, dhf, dx_out, f"ffn_dnorm{l}")
    return (dx, dnorm, dw_up, jnp.concatenate([dcw_g, dcw_v], axis=1),
            dw_down.reshape(N_CHIPS, DOWN_SHARD, D_MODEL))


def _ple_layer_fwd(x, p, norm, W, l):
    T = x.shape[0]
    hg = _rms_fwd(x, norm, f"ple_norm{l}")
    gl = _mm(hg, W["w_ple_gate"], "nn", f"ple_gate{l}", dims=(T, D_MODEL, D_MODEL), b_view=_layer_of(l), tn=1024)
    pp = _mm(p, W["w_ple"], "nn", f"ple_proj{l}", dims=(T, D_MODEL, PLE_DIM), a_view=_layer_of(l),
             b_view=_layer_of(l), tn=1024)
    return _ple_fwd(x, pp, gl, f"ple_mix{l}"), (x, hg, gl, pp)


def _ple_layer_bwd(dx_out, saved, p, norm, W, l, after=None):
    x, hg, gl, pp = saved
    T = x.shape[0]
    dpp, dgl = _ple_bwd(dx_out, pp, gl, f"ple_dmix{l}", after)
    dw_ple = _mm(p, dpp, "tn", f"ple_dwple{l}", dims=(PLE_DIM, D_MODEL, T), a_view=_layer_of(l),
                 tm=PLE_DIM, tn=1024, tk=1024)
    dw_ple = dw_ple.reshape(PLE_DIM, N_CHIPS, PLE_DIM).transpose(1, 0, 2)
    dw_gate = _mm(hg, dgl, "tn", f"ple_dwgate{l}", tm=1024, tn=1024, tk=512)
    dhg = _mm(dgl, W["w_ple_gate"], "nt", f"ple_dhg{l}", dims=(T, D_MODEL, D_MODEL), b_view=_layer_of(l), tn=1024)
    dx, dnorm = _rms_bwd(x, norm, dhg, dx_out, f"ple_dnorm{l}")
    return dx, dnorm, dw_gate.reshape(N_CHIPS, GATE_SHARD, D_MODEL), dw_ple


def _local_step(x, p, target, W, token=None, later_weights=None, on_grads=None):
    T = x.shape[0]
    H = GDN_HEADS
    G = {}

    hn_e = _rms_fwd(x, W["mix_norm_e"], "mix_norm_e", after=token)
    proj_e = _mm(hn_e, W["w_in_e"], "nn", "in_e", dims=(T, 2 * D_MODEL, D_MODEL), b_view=_cols_of(1), tn=IN_E_SHARD)
    pool_out = _pool_fwd(proj_e, W["pool_w"], W["pool_scale"])
    attn, ltot = _sb_fwd(proj_e)
    if later_weights is not None:
        W = {**W, **later_weights(attn)}
    mix_e = jnp.concatenate([pool_out, attn], axis=1).astype(BF16)
    x1 = _mm(mix_e, W["w_out_e"], "nn", "out_e", res=x, tn=1024)
    x2, ffn0 = _ffn_fwd(x1, W["ffn_norm"][0:1], W, W["ffn_conv"][0], 0)
    x3, ple0 = _ple_layer_fwd(x2, p, W["ple_norm"][0:1], W, 0)

    hn_o = _rms_fwd(x3, W["mix_norm_o"], "mix_norm_o")
    proj_o = _mm(hn_o, W["w_in_o"], "nn", "in_o", tn=1408)
    qkv = _gdn_pre_fwd(proj_o, W["conv_qkv_o"])
    gates = jnp.pad(jnp.concatenate([W["a_log_o"], W["dt_bias_o"]], axis=0), ((0, 6), (0, LANE - H)))
    xs = _gdn_prep_fwd(qkv, proj_o, gates)
    o, states = _gdn_scan_fwd(xs)
    og = _gdn_post_fwd(o, proj_o, W["gdn_norm_o"])
    x4 = _mm(og, W["w_out_o"], "nn", "out_o", res=x3, tn=1024)
    x5, ffn1 = _ffn_fwd(x4, W["ffn_norm"][1:2], W, W["ffn_conv"][1], 1)
    x6, ple1 = _ple_layer_fwd(x5, p, W["ple_norm"][1:2], W, 1)

    sq, dx6, G["final_norm"] = _final_loss(x6, W["final_norm"], target, "final_loss")

    dx5, dpn1, dwg1, dwp1 = _ple_layer_bwd(dx6, ple1, p, W["ple_norm"][1:2], W, 1)
    dx4, dfn1, dwu1, dfc1, dwd1 = _ffn_bwd(dx5, ffn1, W["ffn_norm"][1:2], W, W["ffn_conv"][1], 1)
    dog = _mm(dx4, W["w_out_o"], "nt", "d_og", tn=1024)
    G["w_out_o"] = _mm(og, dx4, "tn", "dw_out_o", tm=1024, tn=1024, tk=512).reshape(N_CHIPS, GATE_SHARD, D_MODEL)
    do, dz, dgn = _gdn_post_bwd(o, proj_o, W["gdn_norm_o"], dog)
    G["gdn_norm_o"] = jnp.sum(dgn, axis=0)
    dxs = _gdn_scan_bwd(xs, states, do)
    dqkv_act, dba, dgates = _gdn_prep_bwd(qkv, proj_o, gates, dxs)
    G["a_log_o"] = dgates[0:1, :H]
    G["dt_bias_o"] = dgates[1:2, :H]
    dqkv, G["conv_qkv_o"] = _gdn_pre_bwd(proj_o, W["conv_qkv_o"], dqkv_act)
    dproj_o = jnp.concatenate([dqkv, dz, dba.astype(BF16)], axis=1)
    G["w_in_o"] = _mm(hn_o, dproj_o, "tn", "dw_in_o", tm=1024, tn=1408, tk=512)
    dhn_o = _mm(dproj_o, W["w_in_o"], "nt", "d_hn_o", tn=1024, tk=1408)
    dx3, G["mix_norm_o"] = _rms_bwd(x3, W["mix_norm_o"], dhn_o, dx4, "d_mix_norm_o")

    token1 = token0 = None
    if on_grads is not None:
        token1 = on_grads("second", {("w_in_o", 0): _chip_major_w_in_o(G["w_in_o"]), ("w_out_o", 0): G["w_out_o"],
                                     ("w_up", 1): dwu1, ("w_down", 1): dwd1, ("w_ple_gate", 1): dwg1,
                                     ("w_ple", 1): dwp1})
    dx2, dpn0, dwg0, dwp0 = _ple_layer_bwd(dx3, ple0, p, W["ple_norm"][0:1], W, 0, after=token1)
    dx1, dfn0, dwu0, dfc0, dwd0 = _ffn_bwd(dx2, ffn0, W["ffn_norm"][0:1], W, W["ffn_conv"][0], 0)
    if on_grads is not None:
        token0 = on_grads("first_ffn", {("w_up", 0): dwu0, ("w_down", 0): dwd0, ("w_ple_gate", 0): dwg0,
                                        ("w_ple", 0): dwp0})
    dmix = _mm(dx1, W["w_out_e"], "nt", "d_mix_e", tn=1024)
    G["w_out_e"] = _mm(mix_e, dx1, "tn", "dw_out_e", tm=1024, tn=1024, tk=512).reshape(N_CHIPS, GATE_SHARD, D_MODEL)
    du, G["pool_w"], G["pool_scale"] = _pool_bwd(proj_e, W["pool_w"], W["pool_scale"], dmix)
    dqa, dka, dva = _sb_bwd(proj_e, dmix, ltot, after=token0)
    dproj_e = jnp.concatenate([du, dqa, dka, dva], axis=1).astype(BF16)
    G["w_in_e"] = _mm(hn_e, dproj_e, "tn", "dw_in_e", dims=(D_MODEL, 2 * D_MODEL, T), o_view=_cols_of(1),
                      out_shape=(N_CHIPS, D_MODEL, IN_E_SHARD), tm=1024, tn=IN_E_SHARD, tk=512)
    dhn_e = _mm(dproj_e, W["w_in_e"], "nt", "d_hn_e", dims=(T, D_MODEL, 2 * D_MODEL), b_view=_cols_of(1),
                tn=1024, tk=IN_E_SHARD)
    grad_x, G["mix_norm_e"] = _rms_bwd(x, W["mix_norm_e"], dhn_e, dx1, "d_mix_norm_e")

    G["ffn_norm"] = jnp.concatenate([dfn0, dfn1], axis=0)
    G["ple_norm"] = jnp.concatenate([dpn0, dpn1], axis=0)
    G["ffn_conv"] = jnp.stack([dfc0, dfc1])
    G["w_up"] = [dwu0, dwu1]
    G["w_down"] = [dwd0, dwd1]
    G["w_ple_gate"] = [dwg0, dwg1]
    G["w_ple"] = [dwp0, dwp1]
    return sq[0, 0], grad_x, G


BIG = ("w_in_e", "w_out_e", "w_in_o", "w_out_o", "w_up", "w_down", "w_ple_gate", "w_ple")
SHARDED_SMALL = (("mix_norm_o", 1), ("conv_qkv_o", 2), ("ffn_conv", 2))
REPLICATED = ("mix_norm_e", "pool_w", "pool_scale", "a_log_o", "dt_bias_o", "gdn_norm_o", "ffn_norm", "ple_norm",
              "final_norm")
WEIGHT_ORDER = ("mix_norm_e", "w_in_e", "pool_w", "pool_scale", "w_out_e", "mix_norm_o", "w_in_o", "conv_qkv_o",
                "a_log_o", "dt_bias_o", "gdn_norm_o", "w_out_o", "ffn_norm", "w_up", "ffn_conv", "w_down", "ple_norm",
                "w_ple_gate", "w_ple", "final_norm")
SMALL_W = LANE
SMALL_ROWS = 16


def _size(shape):
    n = 1
    for s in shape:
        n *= s
    return n


def _pack(arrs, width, granule):
    flat = jnp.concatenate([a.reshape(-1) for a in arrs])
    rows = -(-flat.shape[0] // width)
    rows = -(-rows // granule) * granule
    return jnp.pad(flat, (0, rows * width - flat.shape[0])).reshape(rows, width)


def _unpack(flat2d, shapes):
    flat = flat2d.reshape(-1)
    out, off = [], 0
    for s in shapes:
        out.append(flat[off:off + _size(s)].reshape(s))
        off += _size(s)
    return out


MESH_ID = pl.DeviceIdType.MESH
HBM_SPEC = pl.BlockSpec(memory_space=pltpu.HBM)


def _where_am_i():
    return lax.axis_index("x"), lax.axis_index("y"), lax.axis_index("c")


def _other_chips(x, y):
    return [(1 - x, y), (x, 1 - y), (1 - x, 1 - y)]


def _remote(src, dst, send_sems, recv_sems, k, to):
    return pltpu.make_async_remote_copy(src_ref=src, dst_ref=dst, send_sem=send_sems.at[k], recv_sem=recv_sems.at[k],
                                        device_id=to, device_id_type=MESH_ID)


def _chip_allgather(pack, name):
    R, Wd = pack.shape
    Rh = R // 2

    def body(src_ref, out_ref, send_sems, recv_sems, local_sem):
        x, y, c = _where_am_i()
        me, sib = (x, y, c), (x, y, 1 - c)
        chips = _other_chips(x, y)
        mine_rows = pl.ds(pl.multiple_of(c * Rh, SMALL_ROWS), Rh)
        sib_rows = pl.ds(pl.multiple_of((1 - c) * Rh, SMALL_ROWS), Rh)
        j_me = 2 * x + y
        local = pltpu.make_async_copy(src_ref, out_ref.at[j_me], local_sem)
        local.start()
        first = [_remote(src_ref.at[mine_rows], out_ref.at[j_me, mine_rows], send_sems, recv_sems, k, (cx, cy, c))
                 for k, (cx, cy) in enumerate(chips)]
        for cp in first:
            cp.start()
        passed = []
        for k, (cx, cy) in enumerate(chips):
            blk = out_ref.at[2 * cx + cy, mine_rows]
            _remote(blk, blk, send_sems, recv_sems, k, me).wait_recv()
            fw = _remote(blk, blk, send_sems, recv_sems, 3 + k, sib)
            fw.start()
            passed.append(fw)
        for k, (cx, cy) in enumerate(chips):
            blk = out_ref.at[2 * cx + cy, sib_rows]
            _remote(blk, blk, send_sems, recv_sems, 3 + k, me).wait_recv()
        for cp in first + passed:
            cp.wait_send()
        local.wait()

    return pl.pallas_call(
        body, name=name, in_specs=[HBM_SPEC], out_specs=HBM_SPEC,
        out_shape=jax.ShapeDtypeStruct((N_CHIPS, R, Wd), pack.dtype),
        scratch_shapes=[pltpu.SemaphoreType.DMA((6,)), pltpu.SemaphoreType.DMA((6,)), pltpu.SemaphoreType.DMA],
    )(pack)


def _chip_allgather_many(blocks, name):
    n = len(blocks)

    def body(*refs):
        srcs, outs = refs[:n], refs[n:2 * n]
        send_sems, recv_sems = refs[2 * n:]
        x, y, c = _where_am_i()
        me, sib = (x, y, c), (x, y, 1 - c)
        chips = _other_chips(x, y)
        j_me = 2 * x + y
        first = [_remote(srcs[p].at[c], outs[p].at[j_me, c], send_sems, recv_sems, 6 * p + k, (cx, cy, c))
                 for p in range(n) for k, (cx, cy) in enumerate(chips)]
        for cp in first:
            cp.start()
        passed = []
        for k, (cx, cy) in enumerate(chips):
            for p in range(n):
                blk = outs[p].at[2 * cx + cy, c]
                _remote(blk, blk, send_sems, recv_sems, 6 * p + k, me).wait_recv()
                fw = _remote(blk, blk, send_sems, recv_sems, 6 * p + 3 + k, sib)
                fw.start()
                passed.append(fw)
        for k, (cx, cy) in enumerate(chips):
            for p in range(n):
                blk = outs[p].at[2 * cx + cy, 1 - c]
                _remote(blk, blk, send_sems, recv_sems, 6 * p + 3 + k, me).wait_recv()
        for cp in first + passed:
            cp.wait_send()

    return pl.pallas_call(
        body, name=name, in_specs=[HBM_SPEC] * n, out_specs=[HBM_SPEC] * n,
        out_shape=[jax.ShapeDtypeStruct((N_CHIPS,) + b.shape, b.dtype) for b in blocks],
        scratch_shapes=[pltpu.SemaphoreType.DMA((6 * n,)), pltpu.SemaphoreType.DMA((6 * n,))],
    )(*blocks)


SEM_SPEC = pl.BlockSpec(memory_space=pltpu.SEMAPHORE)
DATAFLOW_EFFECT = pltpu.SideEffectType.DATAFLOW_SIDE_EFFECTING


def _chip_allgather_start(blocks, name):
    n = len(blocks)

    def body(*refs):
        srcs, lands = refs[:n], refs[n:2 * n]
        send_sems, recv_sems, token = refs[2 * n], refs[2 * n + 1], refs[-1]
        x, y, c = _where_am_i()
        j_me = 2 * x + y
        for p in range(n):
            for k, (cx, cy) in enumerate(_other_chips(x, y)):
                _remote(srcs[p].at[c], lands[p].at[j_me, c], send_sems, recv_sems, 3 * p + k, (cx, cy, c)).start()
        token[...] = jnp.zeros_like(token)

    lands = [pltpu.with_memory_space_constraint(lax.empty((N_CHIPS,) + b.shape, b.dtype), pltpu.HBM) for b in blocks]
    blocks = [pltpu.with_memory_space_constraint(b, pltpu.HBM) for b in blocks]
    outs = pl.pallas_call(
        body, name=name,
        in_specs=[HBM_SPEC] * (2 * n),
        out_specs=[SEM_SPEC, SEM_SPEC] + [HBM_SPEC] * (2 * n) + [pl.BlockSpec(memory_space=pltpu.VMEM)],
        out_shape=[pltpu.SemaphoreType.DMA((3 * n,)), pltpu.SemaphoreType.DMA((3 * n,))]
        + [pltpu.HBM(a.shape, a.dtype) for a in blocks + lands] + [jax.ShapeDtypeStruct((8, LANE), F32)],
        input_output_aliases={i: 2 + i for i in range(2 * n)},
        compiler_params=pltpu.CompilerParams(has_side_effects=DATAFLOW_EFFECT),
    )(*blocks, *lands)
    return outs[0], outs[1], list(outs[2:2 + n]), list(outs[2 + n:2 + 2 * n]), outs[-1]


def _chip_allgather_wait(send_sems, recv_sems, blocks, lands, after, name):
    n = len(blocks)

    def body(*refs):
        srcs, zones = refs[:n], refs[n:2 * n]
        send, recv = refs[2 * n], refs[2 * n + 1]
        x, y, c = _where_am_i()
        for p in range(n):
            for k, (cx, cy) in enumerate(_other_chips(x, y)):
                cp = _remote(srcs[p].at[c], zones[p].at[2 * cx + cy, c], send, recv, 3 * p + k, (x, y, c))
                cp.wait_send()
                cp.wait_recv()

    outs = pl.pallas_call(
        body, name=name,
        in_specs=[HBM_SPEC] * (2 * n) + [SEM_SPEC, SEM_SPEC, pl.BlockSpec(memory_space=pl.ANY)],
        out_specs=[HBM_SPEC] * (2 * n),
        out_shape=[pltpu.HBM(a.shape, a.dtype) for a in list(blocks) + list(lands)],
        input_output_aliases={i: i for i in range(2 * n)},
        compiler_params=pltpu.CompilerParams(has_side_effects=DATAFLOW_EFFECT),
    )(*blocks, *lands, send_sems, recv_sems, after)
    return list(outs[n:])


def _chip_scatter_start(sums, name):
    n = len(sums)

    def body(*refs):
        srcs, lands = refs[:n], refs[n:2 * n]
        send_sems, recv_sems, token = refs[2 * n], refs[2 * n + 1], refs[-1]
        x, y, c = _where_am_i()
        for p in range(n):
            for k, (cx, cy) in enumerate(_other_chips(x, y)):
                _remote(srcs[p].at[2 * cx + cy], lands[p].at[k], send_sems, recv_sems, 3 * p + k, (cx, cy, c)).start()
        token[...] = jnp.zeros_like(token)

    lands = [pltpu.with_memory_space_constraint(lax.empty((N_CHIPS - 1,) + s.shape[1:], s.dtype), pltpu.HBM)
             for s in sums]
    sums = [pltpu.with_memory_space_constraint(s, pltpu.HBM) for s in sums]
    outs = pl.pallas_call(
        body, name=name,
        in_specs=[HBM_SPEC] * (2 * n),
        out_specs=[SEM_SPEC, SEM_SPEC] + [HBM_SPEC] * (2 * n) + [pl.BlockSpec(memory_space=pltpu.VMEM)],
        out_shape=[pltpu.SemaphoreType.DMA((3 * n,)), pltpu.SemaphoreType.DMA((3 * n,))]
        + [pltpu.HBM(a.shape, a.dtype) for a in sums + lands] + [jax.ShapeDtypeStruct((8, LANE), F32)],
        input_output_aliases={i: 2 + i for i in range(2 * n)},
        compiler_params=pltpu.CompilerParams(has_side_effects=DATAFLOW_EFFECT),
    )(*sums, *lands)
    return outs[0], outs[1], list(outs[2:2 + n]), list(outs[2 + n:2 + 2 * n]), outs[-1]


def _chip_scatter_wait(send_sems, recv_sems, sums, lands, after, name):
    n = len(sums)

    def body(*refs):
        srcs, zones = refs[:n], refs[n:2 * n]
        send, recv = refs[2 * n], refs[2 * n + 1]
        x, y, c = _where_am_i()
        for p in range(n):
            for k, (cx, cy) in enumerate(_other_chips(x, y)):
                cp = _remote(srcs[p].at[2 * cx + cy], zones[p].at[k], send, recv, 3 * p + k, (x, y, c))
                cp.wait_send()
                cp.wait_recv()

    outs = pl.pallas_call(
        body, name=name,
        in_specs=[HBM_SPEC] * (2 * n) + [SEM_SPEC, SEM_SPEC, pl.BlockSpec(memory_space=pl.ANY)],
        out_specs=[HBM_SPEC] * (2 * n),
        out_shape=[pltpu.HBM(a.shape, a.dtype) for a in list(sums) + list(lands)],
        input_output_aliases={i: i for i in range(2 * n)},
        compiler_params=pltpu.CompilerParams(has_side_effects=DATAFLOW_EFFECT),
    )(*sums, *lands, send_sems, recv_sems, after)
    return list(outs[n:])


def _chip_allgather_forward(lands, name):
    n = len(lands)

    def body(*refs):
        ins, outs = refs[:n], refs[n:2 * n]
        send_sems, recv_sems = refs[2 * n:]
        x, y, c = _where_am_i()
        me, sib = (x, y, c), (x, y, 1 - c)
        chips = _other_chips(x, y)
        passed = [_remote(ins[p].at[2 * cx + cy, c], outs[p].at[2 * cx + cy, c], send_sems, recv_sems, 3 * p + k, sib)
                  for p in range(n) for k, (cx, cy) in enumerate(chips)]
        for cp in passed:
            cp.start()
        for p in range(n):
            for k, (cx, cy) in enumerate(chips):
                blk = outs[p].at[2 * cx + cy, 1 - c]
                _remote(blk, blk, send_sems, recv_sems, 3 * p + k, me).wait_recv()
        for cp in passed:
            cp.wait_send()

    return pl.pallas_call(
        body, name=name, in_specs=[HBM_SPEC] * n, out_specs=[HBM_SPEC] * n,
        out_shape=[jax.ShapeDtypeStruct(a.shape, a.dtype) for a in lands],
        input_output_aliases={i: i for i in range(n)},
        scratch_shapes=[pltpu.SemaphoreType.DMA((3 * n,)), pltpu.SemaphoreType.DMA((3 * n,))],
    )(*lands)


def _sibling_swap_many(pieces, name):
    n = len(pieces)

    def body(*refs):
        srcs, outs = refs[:n], refs[n:2 * n]
        send_sems, recv_sems = refs[2 * n:]
        x, y, c = _where_am_i()
        cps = [_remote(srcs[p].at[:, 1 - c], outs[p], send_sems, recv_sems, p, (x, y, 1 - c)) for p in range(n)]
        for cp in cps:
            cp.start()
        for cp in cps:
            cp.wait()

    return pl.pallas_call(
        body, name=name, in_specs=[HBM_SPEC] * n, out_specs=[HBM_SPEC] * n,
        out_shape=[jax.ShapeDtypeStruct((g.shape[0],) + g.shape[2:], g.dtype) for g in pieces],
        scratch_shapes=[pltpu.SemaphoreType.DMA((n,)), pltpu.SemaphoreType.DMA((n,))],
    )(*pieces)


def _chip_scatter_many(sums, name):
    n = len(sums)

    def body(*refs):
        srcs, outs = refs[:n], refs[n:2 * n]
        send_sems, recv_sems = refs[2 * n:]
        x, y, c = _where_am_i()
        cps = [_remote(srcs[p].at[2 * cx + cy], outs[p].at[k], send_sems, recv_sems, 3 * p + k, (cx, cy, c))
               for p in range(n) for k, (cx, cy) in enumerate(_other_chips(x, y))]
        for cp in cps:
            cp.start()
        for cp in cps:
            cp.wait()

    return pl.pallas_call(
        body, name=name, in_specs=[HBM_SPEC] * n, out_specs=[HBM_SPEC] * n,
        out_shape=[jax.ShapeDtypeStruct((N_CHIPS - 1,) + s.shape[1:], s.dtype) for s in sums],
        scratch_shapes=[pltpu.SemaphoreType.DMA((3 * n,)), pltpu.SemaphoreType.DMA((3 * n,))],
    )(*sums)


def _sibling_send_many(halves, name):
    n = len(halves)

    def body(*refs):
        srcs, outs = refs[:n], refs[n:2 * n]
        send_sems, recv_sems = refs[2 * n:]
        x, y, c = _where_am_i()
        cps = [_remote(srcs[p], outs[p], send_sems, recv_sems, p, (x, y, 1 - c)) for p in range(n)]
        for cp in cps:
            cp.start()
        for cp in cps:
            cp.wait()

    return pl.pallas_call(
        body, name=name, in_specs=[HBM_SPEC] * n, out_specs=[HBM_SPEC] * n,
        out_shape=[jax.ShapeDtypeStruct(h.shape, h.dtype) for h in halves],
        scratch_shapes=[pltpu.SemaphoreType.DMA((n,)), pltpu.SemaphoreType.DMA((n,))],
    )(*halves)


def _row_tile(rows, pref=512):
    best = 8
    for t in range(8, pref + 1, 8):
        if rows % t == 0:
            best = t
    return best


def _where_ids():
    x, y, c = _where_am_i()
    return jnp.stack([c, 2 * x + y]).astype(jnp.int32)


RS_ROWS = 256


def _chip_sums_bf16(G, A, ids, name):
    n, _, hr, cols = G.shape
    tr = _row_tile(hr, RS_ROWS)

    def body(ids_ref, g_ref, a_ref, o_ref):
        o_ref[...] = (g_ref[...] + a_ref[...]).astype(BF16)

    return pl.pallas_call(
        body, name=name,
        grid_spec=pltpu.PrefetchScalarGridSpec(
            num_scalar_prefetch=1, grid=(n, hr // tr),
            in_specs=[pl.BlockSpec((None, None, tr, cols), lambda j, i, ids: (j, ids[0], i, 0)),
                      pl.BlockSpec((None, tr, cols), lambda j, i, ids: (j, i, 0))],
            out_specs=pl.BlockSpec((None, tr, cols), lambda j, i, ids: (j, i, 0))),
        out_shape=jax.ShapeDtypeStruct((n, hr, cols), BF16),
        compiler_params=_cp(("parallel", "parallel")),
    )(ids, G, A)


def _total_half(G, A, B, ids, name):
    _, _, hr, cols = G.shape
    tr = _row_tile(hr, RS_ROWS)

    def body(ids_ref, g_ref, a_ref, b_ref, o_ref):
        s = g_ref[...] + a_ref[...]
        for k in range(N_CHIPS - 1):
            s = s + b_ref[k].astype(F32)
        o_ref[...] = s

    return pl.pallas_call(
        body, name=name,
        grid_spec=pltpu.PrefetchScalarGridSpec(
            num_scalar_prefetch=1, grid=(hr // tr,),
            in_specs=[pl.BlockSpec((None, None, tr, cols), lambda i, ids: (ids[1], ids[0], i, 0)),
                      pl.BlockSpec((None, tr, cols), lambda i, ids: (ids[1], i, 0)),
                      pl.BlockSpec((N_CHIPS - 1, tr, cols), lambda i, ids: (0, i, 0))],
            out_specs=pl.BlockSpec((tr, cols), lambda i, ids: (i, 0))),
        out_shape=jax.ShapeDtypeStruct((hr, cols), F32),
        compiler_params=_cp(("parallel",)),
    )(ids, G, A, B)


def _small_allreduce(v, name):
    R, Wd = v.shape

    def body(x_ref, sum_ref, all_ref, send_sems, recv_sems, local_sem):
        x, y, c = _where_am_i()
        me, sib = (x, y, c), (x, y, 1 - c)
        chips = _other_chips(x, y)

        def slot(px, py, pc):
            return all_ref.at[4 * px + 2 * py + pc]

        local = pltpu.make_async_copy(x_ref, slot(*me), local_sem)
        local.start()
        first = [_remote(x_ref, slot(*me), send_sems, recv_sems, 0, sib)]
        first += [_remote(x_ref, slot(*me), send_sems, recv_sems, 1 + k, (cx, cy, c)) for k, (cx, cy) in enumerate(chips)]
        for cp in first:
            cp.start()
        passed = []
        for k, (cx, cy) in enumerate(chips):
            blk = slot(cx, cy, c)
            _remote(blk, blk, send_sems, recv_sems, 1 + k, me).wait_recv()
            fw = _remote(blk, blk, send_sems, recv_sems, 4 + k, sib)
            fw.start()
            passed.append(fw)
        _remote(slot(*sib), slot(*sib), send_sems, recv_sems, 0, me).wait_recv()
        for k, (cx, cy) in enumerate(chips):
            blk = slot(cx, cy, 1 - c)
            _remote(blk, blk, send_sems, recv_sems, 4 + k, me).wait_recv()
        for cp in first + passed:
            cp.wait_send()
        local.wait()
        s = all_ref[0]
        for d in range(1, N_DEV):
            s = s + all_ref[d]
        sum_ref[...] = s

    vm = pl.BlockSpec(memory_space=pltpu.VMEM)
    return pl.pallas_call(
        body, name=name, in_specs=[vm], out_specs=[vm, vm],
        out_shape=[jax.ShapeDtypeStruct((R, Wd), F32), jax.ShapeDtypeStruct((N_DEV, R, Wd), F32)],
        scratch_shapes=[pltpu.SemaphoreType.DMA((7,)), pltpu.SemaphoreType.DMA((7,)), pltpu.SemaphoreType.DMA],
    )(v)[0]


def _adamw(w, g, m, v, name):
    L, R, Wd = w.shape
    tr = _row_tile(R, RS_ROWS)
    c1 = 1.0 - ADAM_B1 ** ADAM_STEP
    c2 = 1.0 - ADAM_B2 ** ADAM_STEP

    def body(w_ref, g_ref, m_ref, v_ref, d_ref, nm_ref, nv_ref):
        gv = g_ref[...]
        nm = ADAM_B1 * m_ref[...] + (1.0 - ADAM_B1) * gv
        nv = ADAM_B2 * v_ref[...] + (1.0 - ADAM_B2) * (gv * gv)
        d_ref[...] = -ADAM_LR * ((nm / c1) / (jnp.sqrt(nv / c2) + ADAM_EPS) + ADAM_WD * w_ref[...])
        nm_ref[...] = nm
        nv_ref[...] = nv

    row = pl.BlockSpec((None, tr, Wd), lambda l, i: (l, i, 0))
    shp = jax.ShapeDtypeStruct((L, R, Wd), F32)
    return pl.pallas_call(
        body, name=name, grid=(L, R // tr), in_specs=[row] * 4, out_specs=[row] * 3, out_shape=[shp] * 3,
        compiler_params=_cp(("parallel", "parallel")),
    )(w, g, m, v)


def _adamw_halves(w, m, v, mine, theirs, ids, name, after=None):
    L, R, Wd = w.shape
    hr = R // 2
    tr = _row_tile(hr, RS_ROWS)
    c1 = 1.0 - ADAM_B1 ** ADAM_STEP
    c2 = 1.0 - ADAM_B2 ** ADAM_STEP
    ordered = [] if after is None else [after]

    def body(ids_ref, w_ref, m_ref, v_ref, *refs):
        g_refs, (g_ref, d_ref, nm_ref, nv_ref) = refs[:2 * L], refs[-4:]
        layer, half = pl.program_id(0), pl.program_id(1)
        own = half == ids_ref[0]
        gv = jnp.where(own, g_refs[0][...], g_refs[L][...])
        for l in range(1, L):
            gv = jnp.where(layer == l, jnp.where(own, g_refs[l][...], g_refs[L + l][...]), gv)
        nm = ADAM_B1 * m_ref[...] + (1.0 - ADAM_B1) * gv
        nv = ADAM_B2 * v_ref[...] + (1.0 - ADAM_B2) * (gv * gv)
        g_ref[...] = gv
        d_ref[...] = -ADAM_LR * ((nm / c1) / (jnp.sqrt(nv / c2) + ADAM_EPS) + ADAM_WD * w_ref[...])
        nm_ref[...] = nm
        nv_ref[...] = nv

    blk = pl.BlockSpec((None, None, tr, Wd), lambda l, h, i, ids: (l, h, i, 0))
    g_blk = pl.BlockSpec((tr, Wd), lambda l, h, i, ids: (i, 0))
    shp = jax.ShapeDtypeStruct((L, 2, hr, Wd), F32)
    outs = pl.pallas_call(
        body, name=name,
        grid_spec=pltpu.PrefetchScalarGridSpec(
            num_scalar_prefetch=1, grid=(L, 2, hr // tr),
            in_specs=[blk] * 3 + [g_blk] * (2 * L)
            + [pl.BlockSpec((8, LANE), lambda l, h, i, ids: (0, 0)) for _ in ordered], out_specs=[blk] * 4),
        out_shape=[shp] * 4,
        compiler_params=_cp(("parallel", "parallel", "parallel")),
    )(ids, *[a.reshape(L, 2, hr, Wd) for a in (w, m, v)], *mine, *theirs, *ordered)
    return tuple(o.reshape(L, R, Wd) for o in outs)


def _two_halves(a):
    cols = a.shape[-1]
    return a.reshape(2, _size(a.shape) // (2 * cols), cols)


FIRST_NEEDED = ("w_in_e",)
LATER_NEEDED = tuple(n for n in BIG if n not in FIRST_NEEDED)


def _gather_weights(P):
    chip = 2 * lax.axis_index("x") + lax.axis_index("y")

    def with_own(landed, own):
        return lax.dynamic_update_slice_in_dim(landed, own[None], chip, axis=0)

    mine = {n: _two_halves(P[n].astype(BF16)) for n in BIG}
    first = _chip_allgather_many([mine[n] for n in FIRST_NEEDED], "ag_first")
    gathered = {n: with_own(g, mine[n]) for n, g in zip(FIRST_NEEDED, first)}
    send_sems, recv_sems, blocks, lands, token = _chip_allgather_start([mine[n] for n in LATER_NEEDED], "ag_start")

    def later(after):
        landed = _chip_allgather_wait(send_sems, recv_sems, blocks, lands, after, "ag_wait")
        g = {n: with_own(a, mine[n]) for n, a in zip(LATER_NEEDED, _chip_allgather_forward(landed, "ag_forward"))}
        w_in_o = g["w_in_o"].reshape(N_CHIPS, D_MODEL, ODD_IN // N_CHIPS)
        return {
            "w_out_e": g["w_out_e"].reshape(D_MODEL, D_MODEL),
            "w_out_o": g["w_out_o"].reshape(D_MODEL, D_MODEL),
            "w_in_o": jnp.pad(jnp.concatenate([w_in_o[j] for j in range(N_CHIPS)], axis=1),
                              ((0, 0), (0, ODD_IN_PAD - ODD_IN))),
            "w_up": g["w_up"],
            "w_down": g["w_down"].transpose(1, 0, 2, 3).reshape(2, FFN_DIM, D_MODEL),
            "w_ple_gate": g["w_ple_gate"].transpose(1, 0, 2, 3).reshape(2, D_MODEL, D_MODEL),
            "w_ple": g["w_ple"].transpose(1, 2, 0, 3).reshape(2, PLE_DIM, D_MODEL),
        }

    small_shapes = [P[n].shape for n, _ in SHARDED_SMALL]
    small = _chip_allgather(_pack([P[n] for n, _ in SHARDED_SMALL], SMALL_W, SMALL_ROWS), "ag_small")
    parts = [_unpack(small[j], small_shapes) for j in range(N_CHIPS)]
    full = {n: jnp.concatenate([parts[j][i] for j in range(N_CHIPS)], axis=ax)
            for i, (n, ax) in enumerate(SHARDED_SMALL)}
    W = {n: P[n] for n in REPLICATED}
    W["pool_w"] = P["pool_w"][0]
    W["final_norm"] = P["final_norm"].reshape(1, D_MODEL)
    W["mix_norm_o"] = full["mix_norm_o"]
    W["conv_qkv_o"] = full["conv_qkv_o"][0]
    W["ffn_conv"] = full["ffn_conv"]
    W["w_in_e"] = gathered["w_in_e"].reshape(N_CHIPS, D_MODEL, IN_E_SHARD)
    return W, token, later


def _chip_major_w_in_o(g):
    shard = ODD_IN // N_CHIPS
    return jnp.stack([g[:, j * shard:(j + 1) * shard] for j in range(N_CHIPS)])


def _reduce_begin(grads, ids, tag, travel_later):
    keys = list(grads)
    pieces = [g.reshape(N_CHIPS, 2, g.shape[1] // 2, g.shape[2]) for g in grads.values()]
    from_sibling = _sibling_swap_many(pieces, f"rs_sibling_swap_{tag}")
    sums = [_chip_sums_bf16(g, a, ids, f"rs_chip_sums_{tag}{i}") for i, (g, a) in enumerate(zip(pieces, from_sibling))]
    state = dict(keys=keys, pieces=pieces, from_sibling=from_sibling, ids=ids, tag=tag, token=None)
    if travel_later:
        state["flight"] = _chip_scatter_start(sums, f"rs_scatter_start_{tag}")
        state["token"] = state["flight"][-1]
    else:
        state["from_chips"] = _chip_scatter_many(sums, f"rs_chip_scatter_{tag}")
    return state


def _reduce_end(state, after=None):
    tag, ids = state["tag"], state["ids"]
    if "flight" in state:
        send_sems, recv_sems, sums, lands, _ = state["flight"]
        from_chips = _chip_scatter_wait(send_sems, recv_sems, sums, lands, after, f"rs_scatter_wait_{tag}")
    else:
        from_chips = state["from_chips"]
    halves = [_total_half(g, a, b, ids, f"rs_total_{tag}{i}")
              for i, (g, a, b) in enumerate(zip(state["pieces"], state["from_sibling"], from_chips))]
    theirs = _sibling_send_many(halves, f"rs_sibling_send_{tag}")
    return {k: (h, t) for k, h, t in zip(state["keys"], halves, theirs)}


def kernel(x, p, mix_norm_e, w_in_e, pool_w, pool_scale, w_out_e, mix_norm_o, w_in_o, conv_qkv_o, a_log_o, dt_bias_o, gdn_norm_o, w_out_o, ffn_norm, w_up, ffn_conv, w_down, ple_norm, w_ple_gate, w_ple, final_norm, loss_target, m_mix_norm_e, m_w_in_e, m_pool_w, m_pool_scale, m_w_out_e, m_mix_norm_o, m_w_in_o, m_conv_qkv_o, m_a_log_o, m_dt_bias_o, m_gdn_norm_o, m_w_out_o, m_ffn_norm, m_w_up, m_ffn_conv, m_w_down, m_ple_norm, m_w_ple_gate, m_w_ple, m_final_norm, v_mix_norm_e, v_w_in_e, v_pool_w, v_pool_scale, v_w_out_e, v_mix_norm_o, v_w_in_o, v_conv_qkv_o, v_a_log_o, v_dt_bias_o, v_gdn_norm_o, v_w_out_o, v_ffn_norm, v_w_up, v_ffn_conv, v_w_down, v_ple_norm, v_w_ple_gate, v_w_ple, v_final_norm):
    args = locals()
    P = {n: args[n] for n in WEIGHT_ORDER}
    M = {n: args["m_" + n] for n in WEIGHT_ORDER}
    V = {n: args["v_" + n] for n in WEIGHT_ORDER}

    W, token, later_weights = _gather_weights(P)
    T = x.shape[1]
    ids = _where_ids()
    early = {}

    def on_grads(stage, grads):
        early[stage] = _reduce_begin(grads, ids, stage, travel_later=True)
        return early[stage]["token"]

    sq, grad_x, G = _local_step(x.reshape(T, D_MODEL), p.reshape(2, T, PLE_DIM), loss_target.reshape(T, D_MODEL), W,
                                token, later_weights, on_grads)
    last = _reduce_begin({("w_in_e", 0): G["w_in_e"], ("w_out_e", 0): G["w_out_e"]}, ids, "first_mixer",
                         travel_later=True)
    reduced = {}
    for state in early.values():
        reduced.update(_reduce_end(state, after=grad_x))
    out = {}

    def adamw(n, after=None):
        halves = [reduced[(n, l)] for l in range(P[n].shape[0])]
        out[n] = _adamw_halves(P[n], M[n], V[n], [h[0] for h in halves], [h[1] for h in halves], ids, f"adamw_{n}",
                               after)

    last_names = ("w_in_e", "w_out_e")
    for n in BIG:
        if n not in last_names:
            adamw(n, after=last["token"])
    reduced.update(_reduce_end(last, after=out["w_up"][1]))
    for n in last_names:
        adamw(n)

    small_full = {n: G[n] for n in REPLICATED}
    small_full["pool_w"] = G["pool_w"][None]
    small_full["final_norm"] = G["final_norm"].reshape(D_MODEL)
    small_full["mix_norm_o"] = G["mix_norm_o"]
    small_full["conv_qkv_o"] = G["conv_qkv_o"][None]
    small_full["ffn_conv"] = G["ffn_conv"]
    small_names = REPLICATED + tuple(n for n, _ in SHARDED_SMALL)
    summed = _small_allreduce(_pack([small_full[n] for n in small_names] + [sq.reshape(1)], SMALL_W, 8), "ar_small")
    *g_list, sq_total = _unpack(summed, [small_full[n].shape for n in small_names] + [(1,)])
    g_small = dict(zip(small_names, g_list))
    chip = 2 * lax.axis_index("x") + lax.axis_index("y")
    for n, ax in SHARDED_SMALL:
        width = P[n].shape[ax]
        g_small[n] = lax.dynamic_slice_in_dim(g_small[n], chip * width, width, axis=ax)

    def pack_small(D):
        return _pack([D[n] for n in small_names], SMALL_W, RS_ROWS)[None]

    g_pack = pack_small(g_small)
    upd = _adamw(pack_small(P), g_pack, pack_small(M), pack_small(V), "adamw_small")
    shapes = [P[n].shape for n in small_names]
    for n, *vals in zip(small_names, *[_unpack(a[0], shapes) for a in (g_pack,) + tuple(upd)]):
        out[n] = tuple(vals)

    loss = (0.5 / D_MODEL) * sq_total[0]
    return (loss, grad_x[None]) + tuple(out[n][i] for i in range(4) for n in WEIGHT_ORDER)
```

```python
import functools

import jax
import jax.numpy as jnp
from jax import lax
from jax.experimental import pallas as pl
from jax.experimental.pallas import tpu as pltpu

F32 = jnp.float32
BF16 = jnp.bfloat16

D_MODEL = 1024
PLE_DIM = 256
POOL_WIDTH = 512
POOL_WINDOWS = (2, 4, 8, 16)
POOL_GROUP_DIM = 128
SB_HEADS = 8
SB_HEAD_DIM = 64
GDN_HEADS = 8
GDN_HEAD_DIM = 128
GDN_CONV = 4
GDN_CHUNK = 64
FFN_DIM = 2816
FFN_CONV = 3
EPS = 1e-6
ODD_IN = 4 * D_MODEL + 2 * GDN_HEADS
ODD_IN_PAD = 33 * 128
ADAM_LR, ADAM_B1, ADAM_B2, ADAM_EPS, ADAM_WD, ADAM_STEP = 0.001, 0.9, 0.999, 1e-08, 0.01, 10

LANE = 128
VMEM_LIMIT = 56 * 1024 * 1024

N_CHIPS = 4
N_DEV = 8


def _cp(sem=None):
    return pltpu.CompilerParams(dimension_semantics=sem, vmem_limit_bytes=VMEM_LIMIT)


def _tile(n, pref):
    if n <= pref:
        return n
    best = None
    for t in range(LANE, pref + 1, LANE):
        if n % t == 0:
            best = t
    assert best is not None, (n, pref)
    return best


_DIMS = {"nn": (((1,), (0,)), ((), ())), "nt": (((1,), (1,)), ((), ())), "tn": (((0,), (0,)), ((), ()))}
_BDIMS = {"nn": (((2,), (1,)), ((0,), (0,))), "nt": (((2,), (2,)), ((0,), (0,))), "tn": (((1,), (1,)), ((0,), (0,)))}


def _dims(mode, ndim):
    return (_BDIMS if ndim == 3 else _DIMS)[mode]


def _dot(a, b, mode="nn"):
    return lax.dot_general(a.astype(BF16), b.astype(BF16), _dims(mode, a.ndim), preferred_element_type=F32)


def _bdot(a, b, mode="nn"):
    return lax.dot_general(a.astype(BF16), b.astype(BF16), _BDIMS[mode], preferred_element_type=F32)


def _split2(x):
    hi = x.astype(BF16)
    lo = (x - hi.astype(F32)).astype(BF16)
    return hi, lo


def _split3(x):
    hi = x.astype(BF16)
    r = x - hi.astype(F32)
    mid = r.astype(BF16)
    lo = (r - mid.astype(F32)).astype(BF16)
    return hi, mid, lo


def _dot_x01(x, m01, mode="nn"):
    hi, lo = _split2(x)
    return (lax.dot_general(hi, m01, _DIMS[mode], preferred_element_type=F32)
            + lax.dot_general(lo, m01, _DIMS[mode], preferred_element_type=F32))


def _dot3_raw(a, b, mode):
    ah, al = _split2(a)
    bh, bl = _split2(b)
    d = _dims(mode, a.ndim)
    return (lax.dot_general(ah, bh, d, preferred_element_type=F32)
            + lax.dot_general(ah, bl, d, preferred_element_type=F32)
            + lax.dot_general(al, bh, d, preferred_element_type=F32))


@jax.custom_vjp
def _dot3(a, b):
    return _dot3_raw(a, b, "nn")


def _dot3_fwd(a, b):
    return _dot3_raw(a, b, "nn"), (a, b)


def _dot3_bwd(res, g):
    a, b = res
    return _dot(g, b, "nt"), _dot(a, g, "tn")


_dot3.defvjp(_dot3_fwd, _dot3_bwd)


@jax.custom_vjp
def _dot1_nt(a, b):
    return _dot(a, b, "nt")


def _dot1_nt_fwd(a, b):
    return _dot(a, b, "nt"), (a, b)


def _dot1_nt_bwd(res, g):
    a, b = res
    return _dot(g, b, "nn"), _dot(g, a, "tn")


_dot1_nt.defvjp(_dot1_nt_fwd, _dot1_nt_bwd)


def _m01_left_raw(m, x):
    d = _dims("nn", x.ndim)
    if x.ndim == 3:
        m = jnp.broadcast_to(m, (x.shape[0],) + m.shape)
    p0, p1, p2 = _split3(x)
    return (lax.dot_general(m, p0, d, preferred_element_type=F32)
            + lax.dot_general(m, p1, d, preferred_element_type=F32)
            + lax.dot_general(m, p2, d, preferred_element_type=F32))


@jax.custom_vjp
def _m01_left(m, mt, x):
    return _m01_left_raw(m, x)


def _m01_left_fwd(m, mt, x):
    return _m01_left_raw(m, x), (m, mt)


def _m01_left_bwd(res, g):
    m, mt = res
    return jnp.zeros_like(m), jnp.zeros_like(mt), _m01_left_raw(mt, g)


_m01_left.defvjp(_m01_left_fwd, _m01_left_bwd)


def _softplus(x):
    return jnp.maximum(x, 0.0) + jnp.log(1.0 + jnp.exp(-jnp.abs(x)))


def _sigmoid(x):
    return 0.5 * jnp.tanh(0.5 * x) + 0.5


def _silu(x):
    return x * _sigmoid(x)


def _dsilu(x):
    s = _sigmoid(x)
    return s * (1.0 + x * (1.0 - s))


def _cols_of(n_blocks_per_part, *fixed):
    return lambda r, c: (c // n_blocks_per_part,) + fixed + (r, c % n_blocks_per_part)


def _rows_of(n_blocks_per_part, *fixed):
    return lambda r, c: (r // n_blocks_per_part,) + fixed + (r % n_blocks_per_part, c)


def _layer_of(layer):
    return lambda r, c: (layer, r, c)


def _mm(a, b, mode, name, out_dtype=F32, res=None, tm=1024, tn=512, tk=1024,
        dims=None, a_view=None, b_view=None, o_view=None, out_shape=None):
    if dims is None:
        if mode == "nn":
            (M, K), (K2, N) = a.shape, b.shape
        elif mode == "nt":
            (M, K), (N, K2) = a.shape, b.shape
        else:
            (K, M), (K2, N) = a.shape, b.shape
        assert K == K2, (name, a.shape, b.shape)
    else:
        M, N, K = dims
    tm, tn, tk = _tile(M, tm), _tile(N, tn), _tile(K, tk)
    nk = K // tk

    def spec(arr, blk, view, rc):
        view = view or (lambda r, c: (r, c))
        return pl.BlockSpec((None,) * (arr.ndim - 2) + blk, lambda i, j, k: view(*rc(i, j, k)))

    if mode == "tn":
        a_spec = spec(a, (tk, tm), a_view, lambda i, j, k: (k, i))
    else:
        a_spec = spec(a, (tm, tk), a_view, lambda i, j, k: (i, k))
    if mode == "nt":
        b_spec = spec(b, (tn, tk), b_view, lambda i, j, k: (j, k))
    else:
        b_spec = spec(b, (tk, tn), b_view, lambda i, j, k: (k, j))
    out_shape = out_shape or (M, N)
    o_spec = pl.BlockSpec((None,) * (len(out_shape) - 2) + (tm, tn),
                          lambda i, j, k: (o_view or (lambda r, c: (r, c)))(i, j))
    has_res = res is not None
    assert not (has_res and o_view), name

    def body(*refs):
        a_ref, b_ref = refs[:2]
        r_ref = refs[2] if has_res else None
        o_ref = refs[3] if has_res else refs[2]

        def finish(r):
            if has_res:
                r = r + r_ref[...]
            o_ref[...] = r.astype(out_dtype)

        if nk == 1:
            finish(_dot(a_ref[...], b_ref[...], mode))
            return
        acc = refs[-1]
        k = pl.program_id(2)

        @pl.when(k == 0)
        def _():
            acc[...] = jnp.zeros_like(acc)

        acc[...] += _dot(a_ref[...], b_ref[...], mode)

        @pl.when(k == nk - 1)
        def _():
            finish(acc[...])

    ins = [a, b] + ([res] if has_res else [])
    in_specs = [a_spec, b_spec] + ([o_spec] if has_res else [])
    return pl.pallas_call(
        body, name=name, grid=(M // tm, N // tn, nk),
        in_specs=in_specs, out_specs=o_spec,
        out_shape=jax.ShapeDtypeStruct(out_shape, out_dtype),
        scratch_shapes=[pltpu.VMEM((tm, tn), F32)] if nk > 1 else [],
        compiler_params=_cp(("parallel", "parallel", "arbitrary")),
    )(*ins)


def _rms_fwd(x, gain, name, after=None):
    T, D = x.shape
    tt = _tile(T, 512)

    def body(x_ref, g_ref, *rest):
        o_ref = rest[-1]
        xv = x_ref[...]
        r = lax.rsqrt(jnp.mean(xv * xv, axis=-1, keepdims=True) + EPS)
        o_ref[...] = (xv * r * g_ref[...]).astype(BF16)

    ordered = [] if after is None else [after]
    return pl.pallas_call(
        body, name=name, grid=(T // tt,),
        in_specs=[pl.BlockSpec((tt, D), lambda i: (i, 0)), pl.BlockSpec((1, D), lambda i: (0, 0))]
        + [pl.BlockSpec((8, LANE), lambda i: (0, 0)) for _ in ordered],
        out_specs=pl.BlockSpec((tt, D), lambda i: (i, 0)),
        out_shape=jax.ShapeDtypeStruct((T, D), BF16),
        compiler_params=_cp(("parallel",)),
    )(x, gain, *ordered)


def _rms_bwd(x, gain, dh, dres, name):
    T, D = x.shape
    tt = _tile(T, 512)

    def body(x_ref, g_ref, dh_ref, dr_ref, dx_ref, dg_ref):
        i = pl.program_id(0)
        xv = x_ref[...]
        dy = dh_ref[...].astype(F32)
        r = lax.rsqrt(jnp.mean(xv * xv, axis=-1, keepdims=True) + EPS)
        xn = xv * r
        gdy = dy * g_ref[...]
        dx = r * (gdy - xn * jnp.mean(gdy * xn, axis=-1, keepdims=True))
        dx_ref[...] = dr_ref[...] + dx

        @pl.when(i == 0)
        def _():
            dg_ref[...] = jnp.zeros_like(dg_ref)

        dg_ref[...] += jnp.sum(dy * xn, axis=0, keepdims=True)

    row = pl.BlockSpec((tt, D), lambda i: (i, 0))
    vec = pl.BlockSpec((1, D), lambda i: (0, 0))
    return pl.pallas_call(
        body, name=name, grid=(T // tt,),
        in_specs=[row, vec, row, row], out_specs=[row, vec],
        out_shape=[jax.ShapeDtypeStruct((T, D), F32), jax.ShapeDtypeStruct((1, D), F32)],
        compiler_params=_cp(("arbitrary",)),
    )(x, gain, dh, dres)


def _final_loss(x, gain, target, name):
    T, D = x.shape
    tt = _tile(T, 512)

    def body(x_ref, g_ref, t_ref, l_ref, dx_ref, dg_ref):
        i = pl.program_id(0)
        xv = x_ref[...]
        r = lax.rsqrt(jnp.mean(xv * xv, axis=-1, keepdims=True) + EPS)
        xn = xv * r
        err = xn * g_ref[...] - t_ref[...]
        dy = err * (1.0 / D)
        gdy = dy * g_ref[...]
        dx_ref[...] = r * (gdy - xn * jnp.mean(gdy * xn, axis=-1, keepdims=True))

        @pl.when(i == 0)
        def _():
            dg_ref[...] = jnp.zeros_like(dg_ref)
            l_ref[...] = jnp.zeros_like(l_ref)

        dg_ref[...] += jnp.sum(dy * xn, axis=0, keepdims=True)
        l_ref[...] += jnp.sum(jnp.sum(err * err, axis=1, keepdims=True), axis=0, keepdims=True)

    row = pl.BlockSpec((tt, D), lambda i: (i, 0))
    vec = pl.BlockSpec((1, D), lambda i: (0, 0))
    return pl.pallas_call(
        body, name=name, grid=(T // tt,),
        in_specs=[row, vec, row],
        out_specs=[pl.BlockSpec((8, LANE), lambda i: (0, 0)), row, vec],
        out_shape=[jax.ShapeDtypeStruct((8, LANE), F32), jax.ShapeDtypeStruct((T, D), F32),
                   jax.ShapeDtypeStruct((1, D), F32)],
        compiler_params=_cp(("arbitrary",)),
    )(x, gain, target)


def _shift_down(x, i, t_idx):
    if i == 0:
        return x
    return jnp.where(t_idx >= i, pltpu.roll(x, i, 0), 0.0)


def _shift_up(x, i, t_idx):
    if i == 0:
        return x
    n = x.shape[0]
    return jnp.where(t_idx < n - i, pltpu.roll(x, n - i, 0), 0.0)


def _pool_select(g, vals):
    out = vals[-1]
    for gi in range(len(vals) - 2, -1, -1):
        out = jnp.where(g == gi, vals[gi], out)
    return out


def _pool_y(u, g, t_idx):
    s1 = u + _shift_down(u, 1, t_idx)
    s2 = s1 + _shift_down(s1, 2, t_idx)
    s3 = s2 + _shift_down(s2, 4, t_idx)
    s4 = s3 + _shift_down(s3, 8, t_idx)
    ws = _pool_select(g, [s1, s2, s3, s4])
    win = _pool_select(g, [jnp.float32(w) for w in POOL_WINDOWS])
    cnt = jnp.minimum(t_idx.astype(F32) + 1.0, win)
    return ws / cnt - u, cnt


def _pool_fwd(proj, pool_w, pool_scale):
    T = proj.shape[0]
    G, C = len(POOL_WINDOWS), POOL_GROUP_DIM

    def body(u_ref, w_ref, s_ref, o_ref):
        g = pl.program_id(0)
        t_idx = lax.broadcasted_iota(jnp.int32, (T, C), 0)
        y, _ = _pool_y(u_ref[...], g, t_idx)
        o_ref[...] = _dot(y, w_ref[0]) * s_ref[...]

    return pl.pallas_call(
        body, name="pool_fwd", grid=(G,),
        in_specs=[pl.BlockSpec((T, C), lambda g: (0, g)), pl.BlockSpec((1, C, C), lambda g: (g, 0, 0)),
                  pl.BlockSpec((1, C), lambda g: (0, g))],
        out_specs=pl.BlockSpec((T, C), lambda g: (0, g)),
        out_shape=jax.ShapeDtypeStruct((T, G * C), F32),
        compiler_params=_cp(("parallel",)),
    )(proj, pool_w, pool_scale)


def _pool_bwd(proj, pool_w, pool_scale, dmix):
    T = proj.shape[0]
    G, C = len(POOL_WINDOWS), POOL_GROUP_DIM

    def body(u_ref, w_ref, s_ref, do_ref, du_ref, dw_ref, ds_ref):
        g = pl.program_id(0)
        t_idx = lax.broadcasted_iota(jnp.int32, (T, C), 0)
        y, cnt = _pool_y(u_ref[...], g, t_idx)
        w = w_ref[0]
        dout = do_ref[...]
        ds_ref[...] = jnp.sum(dout * _dot(y, w), axis=0, keepdims=True)
        dy2 = dout * s_ref[...]
        dw_ref[0] = _dot(y, dy2, "tn")
        dy = _dot(dy2, w, "nt")
        dz = dy / cnt
        r1 = dz + _shift_up(dz, 1, t_idx)
        r2 = r1 + _shift_up(r1, 2, t_idx)
        r3 = r2 + _shift_up(r2, 4, t_idx)
        r4 = r3 + _shift_up(r3, 8, t_idx)
        du_ref[...] = _pool_select(g, [r1, r2, r3, r4]) - dy

    col = pl.BlockSpec((T, C), lambda g: (0, g))
    return pl.pallas_call(
        body, name="pool_bwd", grid=(G,),
        in_specs=[col, pl.BlockSpec((1, C, C), lambda g: (g, 0, 0)), pl.BlockSpec((1, C), lambda g: (0, g)), col],
        out_specs=[col, pl.BlockSpec((1, C, C), lambda g: (g, 0, 0)), pl.BlockSpec((1, C), lambda g: (0, g))],
        out_shape=[jax.ShapeDtypeStruct((T, G * C), F32), jax.ShapeDtypeStruct((G, C, C), F32),
                   jax.ShapeDtypeStruct((1, G * C), F32)],
        compiler_params=_cp(("parallel",)),
    )(proj, pool_w, pool_scale, dmix)


SB_SCALE = SB_HEAD_DIM ** -0.5
SB_PASS_SIZES = (4, 2, 1)
SB_PASS_SIZES_BWD = (2, 1)


def _sb_tile_logits(qb, kblk, valid):
    z = _dot(qb, kblk, "nt")
    sp = _softplus(z)
    l1m = -sp
    if valid is not None:
        l1m = jnp.where(valid, l1m, 0.0)
    return z, sp, l1m


SB_PAIR = LANE // SB_HEAD_DIM
SB_Q0 = POOL_WIDTH // LANE
SB_NB = SB_HEADS // SB_PAIR


def _sb_head_masks():
    lane = lax.broadcasted_iota(jnp.int32, (1, LANE), 1)
    return [(lane // SB_HEAD_DIM == h).astype(F32) for h in range(SB_PAIR)]


def _sb_fwd(proj):
    T = proj.shape[0]
    B = _tile(T, 256)
    nq = T // B

    def body(q_ref, k_ref, v_ref, o_ref, l_ref, k_bf, v_bf):
        qi = pl.program_id(1)

        @pl.when(qi == 0)
        def _():
            k_bf[...] = k_ref[...].astype(BF16)
            v_bf[...] = v_ref[...].astype(BF16)

        masks = _sb_head_masks()
        q_all = q_ref[...]
        qbs = [(q_all * (m * SB_SCALE)).astype(BF16) for m in masks]
        row = lax.broadcasted_iota(jnp.int32, (B, B), 0)
        col = lax.broadcasted_iota(jnp.int32, (B, B), 1)
        later = (row > col).astype(BF16)

        def tiles(kbs, state, valid):
            ksl = [pl.ds(pl.multiple_of(kb * B, B), B) for kb in kbs]
            kblks = [k_bf[ks, :] for ks in ksl]
            masks_of = [valid] + [None] * (len(kbs) - 1)
            logits = [[_sb_tile_logits(qb, kblk, m) for kblk, m in zip(kblks, masks_of)] for qb in qbs]
            within = [[_dot_x01(l1m, later) for _, _, l1m in lg] for lg in logits]
            sums = [[jnp.sum(l1m, axis=1, keepdims=True) for _, _, l1m in lg] for lg in logits]
            out = []
            for h, (carry, acc) in enumerate(state):
                for (z, sp, _), rc, s, ks, m in zip(logits[h], within[h], sums[h], ksl, masks_of):
                    a = jnp.exp(z - sp + rc + carry)
                    if m is not None:
                        a = jnp.where(m, a, 0.0)
                    acc = acc + _dot(a, v_bf[ks, :])
                    carry = carry + s
                out.append((carry, acc))
            return tuple(out)

        state = ((jnp.zeros((B, 1), F32), jnp.zeros((B, LANE), F32)),) * SB_PAIR
        state = lax.cond(qi >= 1, lambda c: tiles([qi, qi - 1], c, col < row), lambda c: tiles([qi], c, col < row),
                         state)
        left = jnp.maximum(qi - 1, 0)
        for size in SB_PASS_SIZES:
            n_pass = left // size
            state = lax.fori_loop(
                0, n_pass, lambda i, c, left=left, size=size: tiles([left - 1 - size * i - u for u in range(size)],
                                                                     c, None), state)
            left = left - n_pass * size
        o_ref[...] = sum(acc * m for (_, acc), m in zip(state, masks))
        for h, (carry, _) in enumerate(state):
            l_ref[h] = carry

    return pl.pallas_call(
        body, name="sb_fwd", grid=(SB_NB, nq),
        in_specs=[pl.BlockSpec((B, LANE), lambda hp, i: (i, SB_Q0 + hp)),
                  pl.BlockSpec((T, LANE), lambda hp, i: (0, SB_Q0 + SB_NB + hp)),
                  pl.BlockSpec((T, LANE), lambda hp, i: (0, SB_Q0 + 2 * SB_NB + hp))],
        out_specs=[pl.BlockSpec((B, LANE), lambda hp, i: (i, hp)),
                   pl.BlockSpec((SB_PAIR, B, 1), lambda hp, i: (hp, i, 0))],
        out_shape=[jax.ShapeDtypeStruct((T, SB_HEADS * SB_HEAD_DIM), F32), jax.ShapeDtypeStruct((SB_HEADS, T, 1), F32)],
        scratch_shapes=[pltpu.VMEM((T, LANE), BF16), pltpu.VMEM((T, LANE), BF16)],
        compiler_params=_cp(("parallel", "arbitrary")),
    )(proj, proj, proj)


def _sb_bwd(proj, dmix, ltot, after=None):
    T = proj.shape[0]
    B = _tile(T, 256)
    nq = T // B
    ordered = [] if after is None else [after]

    def body(q_ref, k_ref, v_ref, do_ref, l_ref, *rest):
        dq_ref, dk_ref, dv_ref, k_bf, v_bf = rest[len(ordered):]
        qi = pl.program_id(1)

        @pl.when(qi == 0)
        def _():
            k_bf[...] = k_ref[...].astype(BF16)
            v_bf[...] = v_ref[...].astype(BF16)
            dk_ref[...] = jnp.zeros_like(dk_ref)
            dv_ref[...] = jnp.zeros_like(dv_ref)

        masks = _sb_head_masks()
        q_all, do_all = q_ref[...], do_ref[...]
        qbs = [(q_all * (m * SB_SCALE)).astype(BF16) for m in masks]
        dobs = [(do_all * m).astype(BF16) for m in masks]
        ltots = [l_ref[h] for h in range(SB_PAIR)]
        row = lax.broadcasted_iota(jnp.int32, (B, B), 0)
        col = lax.broadcasted_iota(jnp.int32, (B, B), 1)
        upto = (row <= col).astype(BF16)
        before = (row < col).astype(BF16)

        def tiles(kbs, state, valid):
            ksl = [pl.ds(pl.multiple_of(kb * B, B), B) for kb in kbs]
            kblks = [k_bf[ks, :] for ks in ksl]
            vblks = [v_bf[ks, :] for ks in ksl]
            masks_of = [None] * (len(kbs) - 1) + [valid]
            logits = [[_sb_tile_logits(qb, kblk, m) for kblk, m in zip(kblks, masks_of)] for qb in qbs]
            das = [[_dot(dob, vblk, "nt") for vblk in vblks] for dob in dobs]
            within = [[_dot_x01(l1m, upto) for _, _, l1m in lg] for lg in logits]
            avals, es, Ps = [], [], []
            for h, (P, _, _) in enumerate(state):
                a_h, e_h = [], []
                for (z, sp, l1m), pc, da, m in zip(logits[h], within[h], das[h], masks_of):
                    a = jnp.exp(z - sp + (ltots[h] - P - pc))
                    if m is not None:
                        a = jnp.where(m, a, 0.0)
                    a_h.append(a)
                    e_h.append(da * a)
                    P = P + jnp.sum(l1m, axis=1, keepdims=True)
                avals.append(a_h)
                es.append(e_h)
                Ps.append(P)
            e_within = [[_dot_x01(e, before) for e in e_h] for e_h in es]
            out = []
            for h, (_, E, dq) in enumerate(state):
                for (z, sp, _), e, ew, a, kblk, ks, m in zip(logits[h], es[h], e_within[h], avals[h], kblks, ksl,
                                                              masks_of):
                    dz = e * jnp.exp(-sp) - jnp.exp(z - sp) * (ew + E)
                    if m is not None:
                        dz = jnp.where(m, dz, 0.0)
                    dzb = dz.astype(BF16)
                    dq = dq + _dot(dzb, kblk)
                    dk_ref[ks, :] += _dot(dzb, qbs[h], "tn")
                    dv_ref[ks, :] += _dot(a, dobs[h], "tn")
                    E = E + jnp.sum(e, axis=1, keepdims=True)
                out.append((Ps[h], E, dq))
            return tuple(out)

        zeros1 = jnp.zeros((B, 1), F32)
        state = ((zeros1, zeros1, jnp.zeros((B, LANE), F32)),) * SB_PAIR
        done = 0
        before_last = jnp.maximum(qi - 1, 0)
        for size in SB_PASS_SIZES_BWD:
            n_pass = (before_last - done) // size
            state = lax.fori_loop(
                0, n_pass, lambda i, c, done=done, size=size: tiles([done + size * i + u for u in range(size)], c, None),
                state)
            done = done + n_pass * size
        state = lax.cond(qi >= 1, lambda c: tiles([qi - 1, qi], c, col < row), lambda c: tiles([qi], c, col < row),
                         state)
        dq_ref[...] = sum(dq * (m * SB_SCALE) for (_, _, dq), m in zip(state, masks))

    qspec = pl.BlockSpec((B, LANE), lambda hp, i: (i, SB_Q0 + hp))
    wide = jax.ShapeDtypeStruct((T, SB_HEADS * SB_HEAD_DIM), F32)
    return pl.pallas_call(
        body, name="sb_bwd", grid=(SB_NB, nq),
        in_specs=[qspec,
                  pl.BlockSpec((T, LANE), lambda hp, i: (0, SB_Q0 + SB_NB + hp)),
                  pl.BlockSpec((T, LANE), lambda hp, i: (0, SB_Q0 + 2 * SB_NB + hp)),
                  qspec,
                  pl.BlockSpec((SB_PAIR, B, 1), lambda hp, i: (hp, i, 0))]
        + [pl.BlockSpec((8, LANE), lambda hp, i: (0, 0)) for _ in ordered],
        out_specs=[pl.BlockSpec((B, LANE), lambda hp, i: (i, hp)),
                   pl.BlockSpec((T, LANE), lambda hp, i: (0, hp)),
                   pl.BlockSpec((T, LANE), lambda hp, i: (0, hp))],
        out_shape=[wide, wide, wide],
        scratch_shapes=[pltpu.VMEM((T, LANE), BF16), pltpu.VMEM((T, LANE), BF16)],
        compiler_params=_cp(("parallel", "arbitrary")),
    )(proj, proj, proj, dmix, ltot, *ordered)


def _rows(w_ref, K):
    return [w_ref[i:i + 1, :] for i in range(K)]


def _conv(x, ws, t_idx):
    K = len(ws)
    y = ws[K - 1] * x
    for i in range(K - 1):
        y = y + ws[i] * _shift_down(x, K - 1 - i, t_idx)
    return y


def _conv_bwd(x, ws, dy, t_idx):
    K = len(ws)
    dx = ws[K - 1] * dy
    dws = []
    for i in range(K - 1):
        dx = dx + ws[i] * _shift_up(dy, K - 1 - i, t_idx)
        dws.append(jnp.sum(dy * _shift_down(x, K - 1 - i, t_idx), axis=0, keepdims=True))
    dws.append(jnp.sum(dy * x, axis=0, keepdims=True))
    return dx, dws


def _store_rows(ref, rows):
    for i, r in enumerate(rows):
        ref[i:i + 1, :] = r


CONV_ROWS = 64


def _ffn_act_fwd(up, conv_w, name):
    T = up.shape[0]
    F = FFN_DIM
    nb = F // LANE

    def body(g_ref, v_ref, wg_ref, wv_ref, o_ref):
        t_idx = lax.broadcasted_iota(jnp.int32, (T, LANE), 0)
        cg = _conv(g_ref[...].astype(F32), _rows(wg_ref, FFN_CONV), t_idx)
        cv = _conv(v_ref[...].astype(F32), _rows(wv_ref, FFN_CONV), t_idx)
        o_ref[...] = (_silu(cg) * cv).astype(BF16)

    return pl.pallas_call(
        body, name=name, grid=(nb,),
        in_specs=[pl.BlockSpec((T, LANE), lambda j: (0, j)), pl.BlockSpec((T, LANE), lambda j: (0, j + nb)),
                  pl.BlockSpec((FFN_CONV, LANE), lambda j: (0, j)),
                  pl.BlockSpec((FFN_CONV, LANE), lambda j: (0, j + nb))],
        out_specs=pl.BlockSpec((T, LANE), lambda j: (0, j)),
        out_shape=jax.ShapeDtypeStruct((T, F), BF16),
        compiler_params=_cp(("parallel",)),
    )(up, up, conv_w, conv_w)


def _ffn_act_bwd(up, conv_w, dact, name):
    T = up.shape[0]
    F = FFN_DIM
    nb = F // LANE

    K = FFN_CONV
    R = CONV_ROWS
    assert T % R == 0, T
    n_chunks = T // R
    PAD = 8

    def body(g_ref, v_ref, wg_ref, wv_ref, da_ref, dup_ref, dwg_ref, dwv_ref, xg_s, xv_s, dyg_s, dyv_s):
        zeros = jnp.zeros((PAD, LANE), F32)
        for s in (xg_s, xv_s, dyg_s, dyv_s):
            s[0:PAD, :] = zeros
            s[T + PAD:T + 2 * PAD, :] = zeros
        xg_s[PAD:T + PAD, :] = g_ref[...].astype(F32)
        xv_s[PAD:T + PAD, :] = v_ref[...].astype(F32)
        wg, wv = _rows(wg_ref, K), _rows(wv_ref, K)

        def window(ext, shift):
            if shift == 0:
                return ext[PAD:PAD + R, :]
            return pltpu.roll(ext, shift % (R + 2 * PAD), 0)[PAD:PAD + R, :]

        def forward(c, carry):
            r0 = pl.multiple_of(c * R, R)
            ge, ve = xg_s[pl.ds(r0, R + 2 * PAD), :], xv_s[pl.ds(r0, R + 2 * PAD), :]
            gw = [window(ge, K - 1 - i) for i in range(K)]
            vw = [window(ve, K - 1 - i) for i in range(K)]
            cg = sum(w * x for w, x in zip(wg, gw))
            cv = sum(w * x for w, x in zip(wv, vw))
            da = da_ref[pl.ds(r0, R), :].astype(F32)
            sg = _sigmoid(cg)
            dyg = da * cv * (sg * (1.0 + cg * (1.0 - sg)))
            dyv = da * (cg * sg)
            dyg_s[pl.ds(pl.multiple_of(r0 + PAD, PAD), R), :] = dyg
            dyv_s[pl.ds(pl.multiple_of(r0 + PAD, PAD), R), :] = dyv
            return tuple(acc + jnp.sum(dy * x, axis=0, keepdims=True)
                         for acc, (dy, x) in zip(carry, [(dyg, x) for x in gw] + [(dyv, x) for x in vw]))

        sums = lax.fori_loop(0, n_chunks, forward, (jnp.zeros((1, LANE), F32),) * (2 * K))
        _store_rows(dwg_ref, sums[:K])
        _store_rows(dwv_ref, sums[K:])

        def backward(c, carry):
            r0 = pl.multiple_of(c * R, R)
            ge, ve = dyg_s[pl.ds(r0, R + 2 * PAD), :], dyv_s[pl.ds(r0, R + 2 * PAD), :]
            dxg = sum(w * window(ge, -(K - 1 - i)) for i, w in enumerate(wg))
            dxv = sum(w * window(ve, -(K - 1 - i)) for i, w in enumerate(wv))
            dup_ref[0, pl.ds(r0, R), :] = dxg.astype(BF16)
            dup_ref[1, pl.ds(r0, R), :] = dxv.astype(BF16)
            return carry

        lax.fori_loop(0, n_chunks, backward, 0)

    col = pl.BlockSpec((T, LANE), lambda j: (0, j))
    wcol = pl.BlockSpec((FFN_CONV, LANE), lambda j: (0, j))
    return pl.pallas_call(
        body, name=name, grid=(nb,),
        in_specs=[col, pl.BlockSpec((T, LANE), lambda j: (0, j + nb)), wcol,
                  pl.BlockSpec((FFN_CONV, LANE), lambda j: (0, j + nb)), col],
        out_specs=[pl.BlockSpec((2, T, LANE), lambda j: (0, 0, j)), wcol, wcol],
        out_shape=[jax.ShapeDtypeStruct((2, T, F), BF16),
                   jax.ShapeDtypeStruct((FFN_CONV, F), F32), jax.ShapeDtypeStruct((FFN_CONV, F), F32)],
        scratch_shapes=[pltpu.VMEM((T + 2 * PAD, LANE), F32)] * 4,
        compiler_params=_cp(("parallel",)),
    )(up, up, conv_w, conv_w, dact)


N_QK_BLOCKS = 2 * GDN_HEADS


def _gdn_pre_fwd(proj, conv_w):
    T = proj.shape[0]
    nb = 3 * GDN_HEADS

    def body(x_ref, w_ref, o_ref):
        j = pl.program_id(0)
        t_idx = lax.broadcasted_iota(jnp.int32, (T, LANE), 0)
        s = _silu(_conv(x_ref[...], _rows(w_ref, GDN_CONV), t_idx))
        rn = lax.rsqrt(jnp.sum(s * s, axis=-1, keepdims=True) + EPS)
        o_ref[...] = s * jnp.where(j < N_QK_BLOCKS, rn, 1.0)

    return pl.pallas_call(
        body, name="gdn_pre_fwd", grid=(nb,),
        in_specs=[pl.BlockSpec((T, LANE), lambda j: (0, j)), pl.BlockSpec((GDN_CONV, LANE), lambda j: (0, j))],
        out_specs=pl.BlockSpec((T, LANE), lambda j: (0, j)),
        out_shape=jax.ShapeDtypeStruct((T, nb * LANE), F32),
        compiler_params=_cp(("parallel",)),
    )(proj, conv_w)


def _gdn_pre_bwd(proj, conv_w, dout):
    T = proj.shape[0]
    nb = 3 * GDN_HEADS
    H = GDN_HEADS

    def body(x_ref, w_ref, do_ref, dx_ref, dw_ref):
        j = pl.program_id(0)
        t_idx = lax.broadcasted_iota(jnp.int32, (T, LANE), 0)
        x, w = x_ref[...], _rows(w_ref, GDN_CONV)
        c = _conv(x, w, t_idx)
        s = _silu(c)
        rn = lax.rsqrt(jnp.sum(s * s, axis=-1, keepdims=True) + EPS)
        do = do_ref[...]
        y = s * rn
        ds_normed = rn * (do - y * jnp.sum(do * y, axis=-1, keepdims=True))
        ds = jnp.where(j < N_QK_BLOCKS, ds_normed, do)
        dx, dw = _conv_bwd(x, w, ds * _dsilu(c), t_idx)
        dx_ref[...] = dx.astype(BF16)
        _store_rows(dw_ref, dw)

    col = pl.BlockSpec((T, LANE), lambda j: (0, j))
    wcol = pl.BlockSpec((GDN_CONV, LANE), lambda j: (0, j))
    return pl.pallas_call(
        body, name="gdn_pre_bwd", grid=(nb,),
        in_specs=[col, wcol, pl.BlockSpec((None, None, T, LANE), lambda j: (j // H, j % H, 0, 0))],
        out_specs=[col, wcol],
        out_shape=[jax.ShapeDtypeStruct((T, nb * LANE), BF16), jax.ShapeDtypeStruct((GDN_CONV, nb * LANE), F32)],
        compiler_params=_cp(("parallel",)),
    )(proj, conv_w, dout)


def _gdn_consts():
    C = GDN_CHUNK
    r = lax.broadcasted_iota(jnp.int32, (C, C), 0)
    c = lax.broadcasted_iota(jnp.int32, (C, C), 1)
    return dict(incl=r >= c, strict=r > c, eye=(r == c).astype(F32),
                low=(r >= c).astype(BF16), up=(r <= c).astype(BF16), ones=jnp.ones((C, C), BF16))


def _unit_lower_inverse_raw(a_mat, eye):
    inv = eye - a_mat
    pw = _dot3_raw(a_mat, a_mat, "nn")
    n_factors = a_mat.shape[-1].bit_length() - 2
    for f in range(n_factors):
        inv = inv + _dot3_raw(inv, pw, "nn")
        if f < n_factors - 1:
            pw = _dot3_raw(pw, pw, "nn")
    return inv


@jax.custom_vjp
def _unit_lower_inverse(a_mat, eye):
    return _unit_lower_inverse_raw(a_mat, eye)


def _unit_lower_inverse_fwd(a_mat, eye):
    inv = _unit_lower_inverse_raw(a_mat, eye)
    return inv, (inv, eye)


def _unit_lower_inverse_bwd(res, g):
    inv, eye = res
    return -_dot(_dot(inv, g, "tn"), inv, "nt"), jnp.zeros_like(eye)


_unit_lower_inverse.defvjp(_unit_lower_inverse_fwd, _unit_lower_inverse_bwd)


def _gdn_prep_chunk(q, k, v, b, a, alog, dtb, cs):
    n, C, dk = q.shape
    beta = _sigmoid(b)
    g = -jnp.exp(alog) * _softplus(a + dtb)
    g_sq = jnp.broadcast_to(g, (n, C, C))
    g_wide = jnp.broadcast_to(g, (n, C, dk))
    gc_i = _m01_left(cs["low"], cs["up"], g_sq)
    gc_j = _m01_left(cs["ones"], cs["ones"], g_sq * cs["up"].astype(F32))
    gc_wide = _m01_left(cs["low"], cs["up"], g_wide)
    gl_wide = _m01_left(cs["ones"], cs["ones"], g_wide)
    decay = jnp.where(cs["incl"], jnp.exp(jnp.where(cs["incl"], gc_i - gc_j, 0.0)), 0.0)
    egc = jnp.exp(gc_wide)
    qs = q * (dk ** -0.5)
    k_beta = k * beta
    a_mat = jnp.where(cs["strict"], _dot1_nt(k_beta, k) * decay, 0.0)
    inv = _unit_lower_inverse(a_mat, cs["eye"])
    u = _dot3(inv, v * beta)
    w = _dot3(inv, k_beta * egc)
    qk = _dot1_nt(qs, k) * decay
    q_dec = qs * egc
    k_dec = k * jnp.exp(gl_wide - gc_wide)
    g_last = jnp.exp(gl_wide)[:, 0:8, :]
    return qk, u, w, q_dec, k_dec, g_last


GDN_PREP_CHUNKS = 16
GDN_BA_BLOCK = 4 * D_MODEL // LANE


def _gdn_prep_specs(T):
    C, dk = GDN_CHUNK, GDN_HEAD_DIM
    npc = min(GDN_PREP_CHUNKS, T // C)
    tc = npc * C
    assert T % tc == 0, (T, tc)
    H = GDN_HEADS
    in_specs = [pl.BlockSpec((tc, dk), lambda i, h: (i, h)),
                pl.BlockSpec((tc, dk), lambda i, h: (i, H + h)),
                pl.BlockSpec((tc, dk), lambda i, h: (i, 2 * H + h)),
                pl.BlockSpec((tc, dk), lambda i, h: (i, GDN_BA_BLOCK)),
                pl.BlockSpec((8, dk), lambda i, h: (0, 0))]
    xs_specs = [pl.BlockSpec((1, tc, C), lambda i, h: (h, i, 0)),
                pl.BlockSpec((1, tc, dk), lambda i, h: (h, i, 0)),
                pl.BlockSpec((1, tc, dk), lambda i, h: (h, i, 0)),
                pl.BlockSpec((1, tc, dk), lambda i, h: (h, i, 0)),
                pl.BlockSpec((1, tc, dk), lambda i, h: (h, i, 0)),
                pl.BlockSpec((1, npc * 8, dk), lambda i, h: (h, i, 0))]
    xs_shapes = [jax.ShapeDtypeStruct((H, T, C), F32)] + [jax.ShapeDtypeStruct((H, T, dk), F32)] * 4 + [
        jax.ShapeDtypeStruct((H, 8 * T // C, dk), F32)]
    return npc, tc, in_specs, xs_specs, xs_shapes


def _lane_pick(x, lane, j):
    return jnp.sum(jnp.where(lane == j, x, 0.0), axis=1, keepdims=True)


def _gdn_head_gates(ba_ref, gates_ref, h, npc):
    lane = lax.broadcasted_iota(jnp.int32, (1, GDN_HEAD_DIM), 1)
    ba = ba_ref[...]
    b = _lane_pick(ba, lane, h).reshape(npc, GDN_CHUNK, 1)
    a = _lane_pick(ba, lane, GDN_HEADS + h).reshape(npc, GDN_CHUNK, 1)
    return b, a, _lane_pick(gates_ref[0:1, :], lane, h), _lane_pick(gates_ref[1:2, :], lane, h), lane


def _gdn_prep_fwd(qkv, proj, gates):
    T = qkv.shape[0]
    C = GDN_CHUNK
    npc, tc, in_specs, xs_specs, xs_shapes = _gdn_prep_specs(T)

    def body(q_ref, k_ref, v_ref, ba_ref, gates_ref, qk_ref, u_ref, w_ref, qd_ref, kd_ref, gl_ref):
        cs = _gdn_consts()
        b, a, alog, dtb, _ = _gdn_head_gates(ba_ref, gates_ref, pl.program_id(1), npc)

        def chunks(val):
            return val.reshape(npc, C, val.shape[-1])

        outs = _gdn_prep_chunk(chunks(q_ref[...]), chunks(k_ref[...]), chunks(v_ref[...]), b, a, alog, dtb, cs)
        for ref, val in zip((qk_ref, u_ref, w_ref, qd_ref, kd_ref), outs[:5]):
            ref[0] = val.reshape(tc, val.shape[-1])
        gl_ref[0] = outs[5].reshape(npc * 8, outs[5].shape[-1])

    return pl.pallas_call(
        body, name="gdn_prep_fwd", grid=(T // tc, GDN_HEADS),
        in_specs=in_specs, out_specs=xs_specs, out_shape=xs_shapes,
        compiler_params=_cp(("parallel", "parallel")),
    )(qkv, qkv, qkv, proj, gates)


def _gdn_prep_bwd(qkv, proj, gates, dxs):
    T = qkv.shape[0]
    C, dk, H = GDN_CHUNK, GDN_HEAD_DIM, GDN_HEADS
    npc, tc, in_specs, xs_specs, _ = _gdn_prep_specs(T)

    def body(q_ref, k_ref, v_ref, ba_ref, gates_ref, dqk_ref, du_ref, dw_ref, dqd_ref, dkd_ref, dgl_ref,
             dqkv_ref, dba_ref, dgates_ref):
        i, h = pl.program_id(0), pl.program_id(1)
        cs = _gdn_consts()
        r8 = lax.broadcasted_iota(jnp.int32, (8, dk), 0)
        c8 = lax.broadcasted_iota(jnp.int32, (8, dk), 1)
        first = (r8 == 0) & (c8 == 0)

        @pl.when((i == 0) & (h == 0))
        def _():
            dgates_ref[...] = jnp.zeros_like(dgates_ref)

        @pl.when(h == 0)
        def _():
            dba_ref[...] = jnp.zeros_like(dba_ref)

        def chunks(val):
            return val.reshape(npc, C, val.shape[-1])

        b, a, alog, dtb, lane = _gdn_head_gates(ba_ref, gates_ref, h, npc)
        prim = (chunks(q_ref[...]), chunks(k_ref[...]), chunks(v_ref[...]), b, a, alog, dtb)
        _, vjp = jax.vjp(lambda *p: _gdn_prep_chunk(*p, cs), *prim)
        dgl = jnp.where(first, dgl_ref[0].reshape(npc, 8, dk), 0.0)
        cts = tuple(chunks(r[0]) for r in (dqk_ref, du_ref, dw_ref, dqd_ref, dkd_ref)) + (dgl,)
        dq, dkk, dv, db, da, dal, ddt = vjp(cts)
        for part, val in enumerate((dq, dkk, dv)):
            dqkv_ref[part, 0] = val.reshape(tc, dk)
        dba_ref[...] += (jnp.where(lane == h, db.reshape(tc, 1), 0.0)
                         + jnp.where(lane == H + h, da.reshape(tc, 1), 0.0))
        dgates_ref[0:1, :] += jnp.where(lane == h, dal, 0.0)
        dgates_ref[1:2, :] += jnp.where(lane == h, ddt, 0.0)

    return pl.pallas_call(
        body, name="gdn_prep_bwd", grid=(T // tc, H),
        in_specs=in_specs + xs_specs,
        out_specs=[pl.BlockSpec((3, 1, tc, dk), lambda i, h: (0, h, i, 0)), pl.BlockSpec((tc, dk), lambda i, h: (i, 0)),
                   pl.BlockSpec((8, dk), lambda i, h: (0, 0))],
        out_shape=[jax.ShapeDtypeStruct((3, H, T, dk), F32), jax.ShapeDtypeStruct((T, dk), F32),
                   jax.ShapeDtypeStruct((8, dk), F32)],
        compiler_params=_cp(("arbitrary", "arbitrary")),
    )(qkv, qkv, qkv, proj, gates, *dxs)


def _gdn_scan_specs(T):
    C, dk, H = GDN_CHUNK, GDN_HEAD_DIM, GDN_HEADS
    return [pl.BlockSpec((H, C, C), lambda n: (0, n, 0))] + [pl.BlockSpec((H, C, dk), lambda n: (0, n, 0))] * 4 + [
        pl.BlockSpec((H, 8, dk), lambda n: (0, n, 0))]


def _gdn_scan_fwd(xs):
    H, T, dk = xs[1].shape
    C = GDN_CHUNK
    n = T // C

    def body(qk_ref, u_ref, w_ref, qd_ref, kd_ref, gl_ref, o_ref, s_ref, state):
        c = pl.program_id(0)

        @pl.when(c == 0)
        def _():
            state[...] = jnp.zeros_like(state)

        S = state[...]
        s_ref[0] = S
        v_new = u_ref[...] - _bdot(w_ref[...], S)
        o_ref[...] = _bdot(qd_ref[...], S) + _bdot(qk_ref[...], v_new)
        state[...] = S * jnp.tile(gl_ref[...], (1, dk // 8, 1)) + _bdot(kd_ref[...], v_new, "tn")

    return pl.pallas_call(
        body, name="gdn_scan_fwd", grid=(n,),
        in_specs=_gdn_scan_specs(T),
        out_specs=[pl.BlockSpec((H, C, dk), lambda n: (0, n, 0)), pl.BlockSpec((1, H, dk, dk), lambda n: (n, 0, 0, 0))],
        out_shape=[jax.ShapeDtypeStruct((H, T, dk), F32), jax.ShapeDtypeStruct((n, H, dk, dk), F32)],
        scratch_shapes=[pltpu.VMEM((H, dk, dk), F32)],
        compiler_params=_cp(("arbitrary",)),
    )(*xs)


def _gdn_scan_bwd(xs, states, do):
    H, T, dk = xs[1].shape
    C = GDN_CHUNK
    n = T // C

    def rev(spec_shape, f):
        return pl.BlockSpec(spec_shape, lambda i: f(n - 1 - i))

    def body(qk_ref, u_ref, w_ref, qd_ref, kd_ref, gl_ref, s_ref, do_ref,
             dqk_ref, du_ref, dw_ref, dqd_ref, dkd_ref, dgl_ref, dstate):
        i = pl.program_id(0)

        @pl.when(i == 0)
        def _():
            dstate[...] = jnp.zeros_like(dstate)

        S = s_ref[0]
        dS = dstate[...]
        do_v = do_ref[...]
        qk, w, qd, kd = qk_ref[...], w_ref[...], qd_ref[...], kd_ref[...]
        v_new = u_ref[...] - _bdot(w, S)
        dv_new = _bdot(qk, do_v, "tn") + _bdot(kd, dS)
        dqk_ref[...] = _bdot(do_v, v_new, "nt")
        dqd_ref[...] = _bdot(do_v, S, "nt")
        dkd_ref[...] = _bdot(v_new, dS, "nt")
        du_ref[...] = dv_new
        dw_ref[...] = -_bdot(dv_new, S, "nt")
        dgl = jnp.sum(jnp.sum(S * dS, axis=2, keepdims=True), axis=1, keepdims=True)
        dgl_ref[...] = jnp.broadcast_to(dgl, dgl_ref.shape)
        dstate[...] = (dS * jnp.tile(gl_ref[...], (1, dk // 8, 1)) + _bdot(qd, do_v, "tn")
                       - _bdot(w, dv_new, "tn"))

    in_specs = [rev((H, C, C), lambda m: (0, m, 0))] + [rev((H, C, dk), lambda m: (0, m, 0))] * 4 + [
        rev((H, 8, dk), lambda m: (0, m, 0)), rev((1, H, dk, dk), lambda m: (m, 0, 0, 0)),
        rev((H, C, dk), lambda m: (0, m, 0))]
    out_specs = [rev((H, C, C), lambda m: (0, m, 0))] + [rev((H, C, dk), lambda m: (0, m, 0))] * 4 + [
        rev((H, 8, dk), lambda m: (0, m, 0))]
    out_shape = [jax.ShapeDtypeStruct((H, T, C), F32)] + [jax.ShapeDtypeStruct((H, T, dk), F32)] * 4 + [
        jax.ShapeDtypeStruct((H, 8 * n, dk), F32)]
    return pl.pallas_call(
        body, name="gdn_scan_bwd", grid=(n,),
        in_specs=in_specs, out_specs=out_specs, out_shape=out_shape,
        scratch_shapes=[pltpu.VMEM((H, dk, dk), F32)],
        compiler_params=_cp(("arbitrary",)),
    )(*xs, states, do)


def _gdn_post_fwd(o, proj, norm_w):
    H, T, dk = o.shape
    tt = _tile(T, 1024)
    zoff = 3 * GDN_HEADS

    def body(o_ref, z_ref, g_ref, y_ref):
        ov = o_ref[0]
        r = lax.rsqrt(jnp.mean(ov * ov, axis=-1, keepdims=True) + EPS)
        y_ref[...] = (ov * r * g_ref[...] * _silu(z_ref[...])).astype(BF16)

    return pl.pallas_call(
        body, name="gdn_post_fwd", grid=(H, T // tt),
        in_specs=[pl.BlockSpec((1, tt, dk), lambda h, i: (h, i, 0)), pl.BlockSpec((tt, dk), lambda h, i: (i, zoff + h)),
                  pl.BlockSpec((1, dk), lambda h, i: (0, 0))],
        out_specs=pl.BlockSpec((tt, dk), lambda h, i: (i, h)),
        out_shape=jax.ShapeDtypeStruct((T, H * dk), BF16),
        compiler_params=_cp(("parallel", "parallel")),
    )(o, proj, norm_w)


def _gdn_post_bwd(o, proj, norm_w, dy):
    H, T, dk = o.shape
    tt = _tile(T, 1024)
    zoff = 3 * GDN_HEADS

    def body(o_ref, z_ref, g_ref, dy_ref, do_ref, dz_ref, dg_ref):
        i = pl.program_id(1)
        ov, z, g, dyv = o_ref[0], z_ref[...], g_ref[...], dy_ref[...]
        r = lax.rsqrt(jnp.mean(ov * ov, axis=-1, keepdims=True) + EPS)
        on = ov * r
        sz = _silu(z)
        dz_ref[...] = (dyv * on * g * _dsilu(z)).astype(BF16)
        dn = dyv * sz
        gdn = dn * g
        do_ref[0] = r * (gdn - on * jnp.mean(gdn * on, axis=-1, keepdims=True))

        @pl.when(i == 0)
        def _():
            dg_ref[...] = jnp.zeros_like(dg_ref)

        dg_ref[0] += jnp.sum(dn * on, axis=0, keepdims=True)

    return pl.pallas_call(
        body, name="gdn_post_bwd", grid=(H, T // tt),
        in_specs=[pl.BlockSpec((1, tt, dk), lambda h, i: (h, i, 0)), pl.BlockSpec((tt, dk), lambda h, i: (i, zoff + h)),
                  pl.BlockSpec((1, dk), lambda h, i: (0, 0)), pl.BlockSpec((tt, dk), lambda h, i: (i, h))],
        out_specs=[pl.BlockSpec((1, tt, dk), lambda h, i: (h, i, 0)), pl.BlockSpec((tt, dk), lambda h, i: (i, h)),
                   pl.BlockSpec((1, 1, dk), lambda h, i: (h, 0, 0))],
        out_shape=[jax.ShapeDtypeStruct((H, T, dk), F32), jax.ShapeDtypeStruct((T, H * dk), BF16),
                   jax.ShapeDtypeStruct((H, 1, dk), F32)],
        compiler_params=_cp(("parallel", "arbitrary")),
    )(o, proj, norm_w, dy)


def _ple_fwd(x, pp, gl, name):
    T, D = x.shape
    tt = _tile(T, 512)

    def body(x_ref, p_ref, g_ref, o_ref):
        o_ref[...] = x_ref[...] + p_ref[...] * _sigmoid(g_ref[...])

    row = pl.BlockSpec((tt, D), lambda i: (i, 0))
    return pl.pallas_call(
        body, name=name, grid=(T // tt,), in_specs=[row, row, row], out_specs=row,
        out_shape=jax.ShapeDtypeStruct((T, D), F32), compiler_params=_cp(("parallel",)),
    )(x, pp, gl)


def _ple_bwd(dx, pp, gl, name, after=None):
    T, D = dx.shape
    tt = _tile(T, 512)

    def body(dx_ref, p_ref, g_ref, *rest):
        dp_ref, dg_ref = rest[-2:]
        s = _sigmoid(g_ref[...])
        dxv = dx_ref[...]
        dp_ref[...] = (dxv * s).astype(BF16)
        dg_ref[...] = (dxv * p_ref[...] * s * (1.0 - s)).astype(BF16)

    row = pl.BlockSpec((tt, D), lambda i: (i, 0))
    ordered = [] if after is None else [after]
    return pl.pallas_call(
        body, name=name, grid=(T // tt,),
        in_specs=[row, row, row] + [pl.BlockSpec((8, LANE), lambda i: (0, 0)) for _ in ordered], out_specs=[row, row],
        out_shape=[jax.ShapeDtypeStruct((T, D), BF16)] * 2, compiler_params=_cp(("parallel",)),
    )(dx, pp, gl, *ordered)


UP_SHARD = 2 * FFN_DIM // N_CHIPS
DOWN_SHARD = FFN_DIM // N_CHIPS
GATE_SHARD = D_MODEL // N_CHIPS
IN_E_SHARD = 2 * D_MODEL // N_CHIPS


def _ffn_fwd(x, norm, W, conv_w, l):
    T = x.shape[0]
    hf = _rms_fwd(x, norm, f"ffn_norm{l}")
    up = _mm(hf, W["w_up"], "nn", f"ffn_up{l}", dims=(T, 2 * FFN_DIM, D_MODEL), b_view=_cols_of(1, l), tn=UP_SHARD,
             out_dtype=BF16)
    act = _ffn_act_fwd(up, conv_w, f"ffn_act{l}")
    x_out = _mm(act, W["w_down"], "nn", f"ffn_down{l}", dims=(T, D_MODEL, FFN_DIM), b_view=_layer_of(l), res=x,
                tn=1024, tk=1408)
    return x_out, (x, hf, up, act)


def _ffn_bwd(dx_out, saved, norm, W, conv_w, l):
    x, hf, up, act = saved
    T = x.shape[0]
    dact = _mm(dx_out, W["w_down"], "nt", f"ffn_dact{l}", dims=(T, FFN_DIM, D_MODEL), b_view=_layer_of(l),
               out_dtype=BF16, tn=1408)
    dw_down = _mm(act, dx_out, "tn", f"ffn_dwdown{l}", tm=1408, tn=1024, tk=512)
    dup, dcw_g, dcw_v = _ffn_act_bwd(up, conv_w, dact, f"ffn_dact_conv{l}")
    dw_up = _mm(hf, dup, "tn", f"ffn_dwup{l}", dims=(D_MODEL, 2 * FFN_DIM, T), b_view=_cols_of(FFN_DIM // UP_SHARD),
                o_view=_cols_of(1), out_shape=(N_CHIPS, D_MODEL, UP_SHARD), tm=1024, tn=UP_SHARD, tk=512)
    dhf = _mm(dup, W["w_up"], "nt", f"ffn_dhf{l}", dims=(T, D_MODEL, 2 * FFN_DIM),
              a_view=_cols_of(FFN_DIM // UP_SHARD), b_view=_cols_of(1, l), tn=1024, tk=UP_SHARD)
    dx, dnorm = _rms_bwd(x, norm, dhf, dx_out, f"ffn_dnorm{l}")
    return (dx, dnorm, dw_up, jnp.concatenate([dcw_g, dcw_v], axis=1),
            dw_down.reshape(N_CHIPS, DOWN_SHARD, D_MODEL))


def _ple_layer_fwd(x, p, norm, W, l):
    T = x.shape[0]
    hg = _rms_fwd(x, norm, f"ple_norm{l}")
    gl = _mm(hg, W["w_ple_gate"], "nn", f"ple_gate{l}", dims=(T, D_MODEL, D_MODEL), b_view=_layer_of(l), tn=1024)
    pp = _mm(p, W["w_ple"], "nn", f"ple_proj{l}", dims=(T, D_MODEL, PLE_DIM), a_view=_layer_of(l),
             b_view=_layer_of(l), tn=1024)
    return _ple_fwd(x, pp, gl, f"ple_mix{l}"), (x, hg, gl, pp)


def _ple_layer_bwd(dx_out, saved, p, norm, W, l, after=None):
    x, hg, gl, pp = saved
    T = x.shape[0]
    dpp, dgl = _ple_bwd(dx_out, pp, gl, f"ple_dmix{l}", after)
    dw_ple = _mm(p, dpp, "tn", f"ple_dwple{l}", dims=(PLE_DIM, D_MODEL, T), a_view=_layer_of(l),
                 tm=PLE_DIM, tn=1024, tk=1024)
    dw_ple = dw_ple.reshape(PLE_DIM, N_CHIPS, PLE_DIM).transpose(1, 0, 2)
    dw_gate = _mm(hg, dgl, "tn", f"ple_dwgate{l}", tm=1024, tn=1024, tk=512)
    dhg = _mm(dgl, W["w_ple_gate"], "nt", f"ple_dhg{l}", dims=(T, D_MODEL, D_MODEL), b_view=_layer_of(l), tn=1024)
    dx, dnorm = _rms_bwd(x, norm, dhg, dx_out, f"ple_dnorm{l}")
    return dx, dnorm, dw_gate.reshape(N_CHIPS, GATE_SHARD, D_MODEL), dw_ple


def _local_step(x, p, target, W, token=None, later_weights=None, on_grads=None):
    T = x.shape[0]
    H = GDN_HEADS
    G = {}

    hn_e = _rms_fwd(x, W["mix_norm_e"], "mix_norm_e", after=token)
    proj_e = _mm(hn_e, W["w_in_e"], "nn", "in_e", dims=(T, 2 * D_MODEL, D_MODEL), b_view=_cols_of(1), tn=IN_E_SHARD)
    pool_out = _pool_fwd(proj_e, W["pool_w"], W["pool_scale"])
    attn, ltot = _sb_fwd(proj_e)
    if later_weights is not None:
        W = {**W, **later_weights(attn)}
    mix_e = jnp.concatenate([pool_out, attn], axis=1).astype(BF16)
    x1 = _mm(mix_e, W["w_out_e"], "nn", "out_e", res=x, tn=1024)
    x2, ffn0 = _ffn_fwd(x1, W["ffn_norm"][0:1], W, W["ffn_conv"][0], 0)
    x3, ple0 = _ple_layer_fwd(x2, p, W["ple_norm"][0:1], W, 0)

    hn_o = _rms_fwd(x3, W["mix_norm_o"], "mix_norm_o")
    proj_o = _mm(hn_o, W["w_in_o"], "nn", "in_o", tn=1408)
    qkv = _gdn_pre_fwd(proj_o, W["conv_qkv_o"])
    gates = jnp.pad(jnp.concatenate([W["a_log_o"], W["dt_bias_o"]], axis=0), ((0, 6), (0, LANE - H)))
    xs = _gdn_prep_fwd(qkv, proj_o, gates)
    o, states = _gdn_scan_fwd(xs)
    og = _gdn_post_fwd(o, proj_o, W["gdn_norm_o"])
    x4 = _mm(og, W["w_out_o"], "nn", "out_o", res=x3, tn=1024)
    x5, ffn1 = _ffn_fwd(x4, W["ffn_norm"][1:2], W, W["ffn_conv"][1], 1)
    x6, ple1 = _ple_layer_fwd(x5, p, W["ple_norm"][1:2], W, 1)

    sq, dx6, G["final_norm"] = _final_loss(x6, W["final_norm"], target, "final_loss")

    dx5, dpn1, dwg1, dwp1 = _ple_layer_bwd(dx6, ple1, p, W["ple_norm"][1:2], W, 1)
    dx4, dfn1, dwu1, dfc1, dwd1 = _ffn_bwd(dx5, ffn1, W["ffn_norm"][1:2], W, W["ffn_conv"][1], 1)
    dog = _mm(dx4, W["w_out_o"], "nt", "d_og", tn=1024)
    G["w_out_o"] = _mm(og, dx4, "tn", "dw_out_o", tm=1024, tn=1024, tk=512).reshape(N_CHIPS, GATE_SHARD, D_MODEL)
    do, dz, dgn = _gdn_post_bwd(o, proj_o, W["gdn_norm_o"], dog)
    G["gdn_norm_o"] = jnp.sum(dgn, axis=0)
    dxs = _gdn_scan_bwd(xs, states, do)
    dqkv_act, dba, dgates = _gdn_prep_bwd(qkv, proj_o, gates, dxs)
    G["a_log_o"] = dgates[0:1, :H]
    G["dt_bias_o"] = dgates[1:2, :H]
    dqkv, G["conv_qkv_o"] = _gdn_pre_bwd(proj_o, W["conv_qkv_o"], dqkv_act)
    dproj_o = jnp.concatenate([dqkv, dz, dba.astype(BF16)], axis=1)
    G["w_in_o"] = _mm(hn_o, dproj_o, "tn", "dw_in_o", tm=1024, tn=1408, tk=512)
    dhn_o = _mm(dproj_o, W["w_in_o"], "nt", "d_hn_o", tn=1024, tk=1408)
    dx3, G["mix_norm_o"] = _rms_bwd(x3, W["mix_norm_o"], dhn_o, dx4, "d_mix_norm_o")

    token1 = token0 = None
    if on_grads is not None:
        token1 = on_grads("second", {("w_in_o", 0): _chip_major_w_in_o(G["w_in_o"]), ("w_out_o", 0): G["w_out_o"],
                                     ("w_up", 1): dwu1, ("w_down", 1): dwd1, ("w_ple_gate", 1): dwg1,
                                     ("w_ple", 1): dwp1})
    dx2, dpn0, dwg0, dwp0 = _ple_layer_bwd(dx3, ple0, p, W["ple_norm"][0:1], W, 0, after=token1)
    dx1, dfn0, dwu0, dfc0, dwd0 = _ffn_bwd(dx2, ffn0, W["ffn_norm"][0:1], W, W["ffn_conv"][0], 0)
    if on_grads is not None:
        token0 = on_grads("first_ffn", {("w_up", 0): dwu0, ("w_down", 0): dwd0, ("w_ple_gate", 0): dwg0,
                                        ("w_ple", 0): dwp0})
    dmix = _mm(dx1, W["w_out_e"], "nt", "d_mix_e", tn=1024)
    G["w_out_e"] = _mm(mix_e, dx1, "tn", "dw_out_e", tm=1024, tn=1024, tk=512).reshape(N_CHIPS, GATE_SHARD, D_MODEL)
    du, G["pool_w"], G["pool_scale"] = _pool_bwd(proj_e, W["pool_w"], W["pool_scale"], dmix)
    dqa, dka, dva = _sb_bwd(proj_e, dmix, ltot, after=token0)
    dproj_e = jnp.concatenate([du, dqa, dka, dva], axis=1).astype(BF16)
    G["w_in_e"] = _mm(hn_e, dproj_e, "tn", "dw_in_e", dims=(D_MODEL, 2 * D_MODEL, T), o_view=_cols_of(1),
                      out_shape=(N_CHIPS, D_MODEL, IN_E_SHARD), tm=1024, tn=IN_E_SHARD, tk=512)
    dhn_e = _mm(dproj_e, W["w_in_e"], "nt", "d_hn_e", dims=(T, D_MODEL, 2 * D_MODEL), b_view=_cols_of(1),
                tn=1024, tk=IN_E_SHARD)
    grad_x, G["mix_norm_e"] = _rms_bwd(x, W["mix_norm_e"], dhn_e, dx1, "d_mix_norm_e")

    G["ffn_norm"] = jnp.concatenate([dfn0, dfn1], axis=0)
    G["ple_norm"] = jnp.concatenate([dpn0, dpn1], axis=0)
    G["ffn_conv"] = jnp.stack([dfc0, dfc1])
    G["w_up"] = [dwu0, dwu1]
    G["w_down"] = [dwd0, dwd1]
    G["w_ple_gate"] = [dwg0, dwg1]
    G["w_ple"] = [dwp0, dwp1]
    return sq[0, 0], grad_x, G


BIG = ("w_in_e", "w_out_e", "w_in_o", "w_out_o", "w_up", "w_down", "w_ple_gate", "w_ple")
SHARDED_SMALL = (("mix_norm_o", 1), ("conv_qkv_o", 2), ("ffn_conv", 2))
REPLICATED = ("mix_norm_e", "pool_w", "pool_scale", "a_log_o", "dt_bias_o", "gdn_norm_o", "ffn_norm", "ple_norm",
              "final_norm")
WEIGHT_ORDER = ("mix_norm_e", "w_in_e", "pool_w", "pool_scale", "w_out_e", "mix_norm_o", "w_in_o", "conv_qkv_o",
                "a_log_o", "dt_bias_o", "gdn_norm_o", "w_out_o", "ffn_norm", "w_up", "ffn_conv", "w_down", "ple_norm",
                "w_ple_gate", "w_ple", "final_norm")
SMALL_W = LANE
SMALL_ROWS = 16


def _size(shape):
    n = 1
    for s in shape:
        n *= s
    return n


def _pack(arrs, width, granule):
    flat = jnp.concatenate([a.reshape(-1) for a in arrs])
    rows = -(-flat.shape[0] // width)
    rows = -(-rows // granule) * granule
    return jnp.pad(flat, (0, rows * width - flat.shape[0])).reshape(rows, width)


def _unpack(flat2d, shapes):
    flat = flat2d.reshape(-1)
    out, off = [], 0
    for s in shapes:
        out.append(flat[off:off + _size(s)].reshape(s))
        off += _size(s)
    return out


MESH_ID = pl.DeviceIdType.MESH
HBM_SPEC = pl.BlockSpec(memory_space=pltpu.HBM)


def _where_am_i():
    return lax.axis_index("x"), lax.axis_index("y"), lax.axis_index("c")


def _other_chips(x, y):
    return [(1 - x, y), (x, 1 - y), (1 - x, 1 - y)]


def _remote(src, dst, send_sems, recv_sems, k, to):
    return pltpu.make_async_remote_copy(src_ref=src, dst_ref=dst, send_sem=send_sems.at[k], recv_sem=recv_sems.at[k],
                                        device_id=to, device_id_type=MESH_ID)


def _chip_allgather(pack, name):
    R, Wd = pack.shape
    Rh = R // 2

    def body(src_ref, out_ref, send_sems, recv_sems, local_sem):
        x, y, c = _where_am_i()
        me, sib = (x, y, c), (x, y, 1 - c)
        chips = _other_chips(x, y)
        mine_rows = pl.ds(pl.multiple_of(c * Rh, SMALL_ROWS), Rh)
        sib_rows = pl.ds(pl.multiple_of((1 - c) * Rh, SMALL_ROWS), Rh)
        j_me = 2 * x + y
        local = pltpu.make_async_copy(src_ref, out_ref.at[j_me], local_sem)
        local.start()
        first = [_remote(src_ref.at[mine_rows], out_ref.at[j_me, mine_rows], send_sems, recv_sems, k, (cx, cy, c))
                 for k, (cx, cy) in enumerate(chips)]
        for cp in first:
            cp.start()
        passed = []
        for k, (cx, cy) in enumerate(chips):
            blk = out_ref.at[2 * cx + cy, mine_rows]
            _remote(blk, blk, send_sems, recv_sems, k, me).wait_recv()
            fw = _remote(blk, blk, send_sems, recv_sems, 3 + k, sib)
            fw.start()
            passed.append(fw)
        for k, (cx, cy) in enumerate(chips):
            blk = out_ref.at[2 * cx + cy, sib_rows]
            _remote(blk, blk, send_sems, recv_sems, 3 + k, me).wait_recv()
        for cp in first + passed:
            cp.wait_send()
        local.wait()

    return pl.pallas_call(
        body, name=name, in_specs=[HBM_SPEC], out_specs=HBM_SPEC,
        out_shape=jax.ShapeDtypeStruct((N_CHIPS, R, Wd), pack.dtype),
        scratch_shapes=[pltpu.SemaphoreType.DMA((6,)), pltpu.SemaphoreType.DMA((6,)), pltpu.SemaphoreType.DMA],
    )(pack)


def _chip_allgather_many(blocks, name):
    n = len(blocks)

    def body(*refs):
        srcs, outs = refs[:n], refs[n:2 * n]
        send_sems, recv_sems = refs[2 * n:]
        x, y, c = _where_am_i()
        me, sib = (x, y, c), (x, y, 1 - c)
        chips = _other_chips(x, y)
        j_me = 2 * x + y
        first = [_remote(srcs[p].at[c], outs[p].at[j_me, c], send_sems, recv_sems, 6 * p + k, (cx, cy, c))
                 for p in range(n) for k, (cx, cy) in enumerate(chips)]
        for cp in first:
            cp.start()
        passed = []
        for k, (cx, cy) in enumerate(chips):
            for p in range(n):
                blk = outs[p].at[2 * cx + cy, c]
                _remote(blk, blk, send_sems, recv_sems, 6 * p + k, me).wait_recv()
                fw = _remote(blk, blk, send_sems, recv_sems, 6 * p + 3 + k, sib)
                fw.start()
                passed.append(fw)
        for k, (cx, cy) in enumerate(chips):
            for p in range(n):
                blk = outs[p].at[2 * cx + cy, 1 - c]
                _remote(blk, blk, send_sems, recv_sems, 6 * p + 3 + k, me).wait_recv()
        for cp in first + passed:
            cp.wait_send()

    return pl.pallas_call(
        body, name=name, in_specs=[HBM_SPEC] * n, out_specs=[HBM_SPEC] * n,
        out_shape=[jax.ShapeDtypeStruct((N_CHIPS,) + b.shape, b.dtype) for b in blocks],
        scratch_shapes=[pltpu.SemaphoreType.DMA((6 * n,)), pltpu.SemaphoreType.DMA((6 * n,))],
    )(*blocks)


SEM_SPEC = pl.BlockSpec(memory_space=pltpu.SEMAPHORE)
DATAFLOW_EFFECT = pltpu.SideEffectType.DATAFLOW_SIDE_EFFECTING


def _chip_allgather_start(blocks, name):
    n = len(blocks)

    def body(*refs):
        srcs, lands = refs[:n], refs[n:2 * n]
        send_sems, recv_sems, token = refs[2 * n], refs[2 * n + 1], refs[-1]
        x, y, c = _where_am_i()
        j_me = 2 * x + y
        for p in range(n):
            for k, (cx, cy) in enumerate(_other_chips(x, y)):
                _remote(srcs[p].at[c], lands[p].at[j_me, c], send_sems, recv_sems, 3 * p + k, (cx, cy, c)).start()
        token[...] = jnp.zeros_like(token)

    lands = [pltpu.with_memory_space_constraint(lax.empty((N_CHIPS,) + b.shape, b.dtype), pltpu.HBM) for b in blocks]
    blocks = [pltpu.with_memory_space_constraint(b, pltpu.HBM) for b in blocks]
    outs = pl.pallas_call(
        body, name=name,
        in_specs=[HBM_SPEC] * (2 * n),
        out_specs=[SEM_SPEC, SEM_SPEC] + [HBM_SPEC] * (2 * n) + [pl.BlockSpec(memory_space=pltpu.VMEM)],
        out_shape=[pltpu.SemaphoreType.DMA((3 * n,)), pltpu.SemaphoreType.DMA((3 * n,))]
        + [pltpu.HBM(a.shape, a.dtype) for a in blocks + lands] + [jax.ShapeDtypeStruct((8, LANE), F32)],
        input_output_aliases={i: 2 + i for i in range(2 * n)},
        compiler_params=pltpu.CompilerParams(has_side_effects=DATAFLOW_EFFECT),
    )(*blocks, *lands)
    return outs[0], outs[1], list(outs[2:2 + n]), list(outs[2 + n:2 + 2 * n]), outs[-1]


def _chip_allgather_wait(send_sems, recv_sems, blocks, lands, after, name):
    n = len(blocks)

    def body(*refs):
        srcs, zones = refs[:n], refs[n:2 * n]
        send, recv = refs[2 * n], refs[2 * n + 1]
        x, y, c = _where_am_i()
        for p in range(n):
            for k, (cx, cy) in enumerate(_other_chips(x, y)):
                cp = _remote(srcs[p].at[c], zones[p].at[2 * cx + cy, c], send, recv, 3 * p + k, (x, y, c))
                cp.wait_send()
                cp.wait_recv()

    outs = pl.pallas_call(
        body, name=name,
        in_specs=[HBM_SPEC] * (2 * n) + [SEM_SPEC, SEM_SPEC, pl.BlockSpec(memory_space=pl.ANY)],
        out_specs=[HBM_SPEC] * (2 * n),
        out_shape=[pltpu.HBM(a.shape, a.dtype) for a in list(blocks) + list(lands)],
        input_output_aliases={i: i for i in range(2 * n)},
        compiler_params=pltpu.CompilerParams(has_side_effects=DATAFLOW_EFFECT),
    )(*blocks, *lands, send_sems, recv_sems, after)
    return list(outs[n:])


def _chip_scatter_start(sums, name):
    n = len(sums)

    def body(*refs):
        srcs, lands = refs[:n], refs[n:2 * n]
        send_sems, recv_sems, token = refs[2 * n], refs[2 * n + 1], refs[-1]
        x, y, c = _where_am_i()
        for p in range(n):
            for k, (cx, cy) in enumerate(_other_chips(x, y)):
                _remote(srcs[p].at[2 * cx + cy], lands[p].at[k], send_sems, recv_sems, 3 * p + k, (cx, cy, c)).start()
        token[...] = jnp.zeros_like(token)

    lands = [pltpu.with_memory_space_constraint(lax.empty((N_CHIPS - 1,) + s.shape[1:], s.dtype), pltpu.HBM)
             for s in sums]
    sums = [pltpu.with_memory_space_constraint(s, pltpu.HBM) for s in sums]
    outs = pl.pallas_call(
        body, name=name,
        in_specs=[HBM_SPEC] * (2 * n),
        out_specs=[SEM_SPEC, SEM_SPEC] + [HBM_SPEC] * (2 * n) + [pl.BlockSpec(memory_space=pltpu.VMEM)],
        out_shape=[pltpu.SemaphoreType.DMA((3 * n,)), pltpu.SemaphoreType.DMA((3 * n,))]
        + [pltpu.HBM(a.shape, a.dtype) for a in sums + lands] + [jax.ShapeDtypeStruct((8, LANE), F32)],
        input_output_aliases={i: 2 + i for i in range(2 * n)},
        compiler_params=pltpu.CompilerParams(has_side_effects=DATAFLOW_EFFECT),
    )(*sums, *lands)
    return outs[0], outs[1], list(outs[2:2 + n]), list(outs[2 + n:2 + 2 * n]), outs[-1]


def _chip_scatter_wait(send_sems, recv_sems, sums, lands, after, name):
    n = len(sums)

    def body(*refs):
        srcs, zones = refs[:n], refs[n:2 * n]
        send, recv = refs[2 * n], refs[2 * n + 1]
        x, y, c = _where_am_i()
        for p in range(n):
            for k, (cx, cy) in enumerate(_other_chips(x, y)):
                cp = _remote(srcs[p].at[2 * cx + cy], zones[p].at[k], send, recv, 3 * p + k, (x, y, c))
                cp.wait_send()
                cp.wait_recv()

    outs = pl.pallas_call(
        body, name=name,
        in_specs=[HBM_SPEC] * (2 * n) + [SEM_SPEC, SEM_SPEC, pl.BlockSpec(memory_space=pl.ANY)],
        out_specs=[HBM_SPEC] * (2 * n),
        out_shape=[pltpu.HBM(a.shape, a.dtype) for a in list(sums) + list(lands)],
        input_output_aliases={i: i for i in range(2 * n)},
        compiler_params=pltpu.CompilerParams(has_side_effects=DATAFLOW_EFFECT),
    )(*sums, *lands, send_sems, recv_sems, after)
    return list(outs[n:])


def _chip_allgather_forward(lands, name):
    n = len(lands)

    def body(*refs):
        ins, outs = refs[:n], refs[n:2 * n]
        send_sems, recv_sems = refs[2 * n:]
        x, y, c = _where_am_i()
        me, sib = (x, y, c), (x, y, 1 - c)
        chips = _other_chips(x, y)
        passed = [_remote(ins[p].at[2 * cx + cy, c], outs[p].at[2 * cx + cy, c], send_sems, recv_sems, 3 * p + k, sib)
                  for p in range(n) for k, (cx, cy) in enumerate(chips)]
        for cp in passed:
            cp.start()
        for p in range(n):
            for k, (cx, cy) in enumerate(chips):
                blk = outs[p].at[2 * cx + cy, 1 - c]
                _remote(blk, blk, send_sems, recv_sems, 3 * p + k, me).wait_recv()
        for cp in passed:
            cp.wait_send()

    return pl.pallas_call(
        body, name=name, in_specs=[HBM_SPEC] * n, out_specs=[HBM_SPEC] * n,
        out_shape=[jax.ShapeDtypeStruct(a.shape, a.dtype) for a in lands],
        input_output_aliases={i: i for i in range(n)},
        scratch_shapes=[pltpu.SemaphoreType.DMA((3 * n,)), pltpu.SemaphoreType.DMA((3 * n,))],
    )(*lands)


def _sibling_swap_many(pieces, name):
    n = len(pieces)

    def body(*refs):
        srcs, outs = refs[:n], refs[n:2 * n]
        send_sems, recv_sems = refs[2 * n:]
        x, y, c = _where_am_i()
        cps = [_remote(srcs[p].at[:, 1 - c], outs[p], send_sems, recv_sems, p, (x, y, 1 - c)) for p in range(n)]
        for cp in cps:
            cp.start()
        for cp in cps:
            cp.wait()

    return pl.pallas_call(
        body, name=name, in_specs=[HBM_SPEC] * n, out_specs=[HBM_SPEC] * n,
        out_shape=[jax.ShapeDtypeStruct((g.shape[0],) + g.shape[2:], g.dtype) for g in pieces],
        scratch_shapes=[pltpu.SemaphoreType.DMA((n,)), pltpu.SemaphoreType.DMA((n,))],
    )(*pieces)


def _chip_scatter_many(sums, name):
    n = len(sums)

    def body(*refs):
        srcs, outs = refs[:n], refs[n:2 * n]
        send_sems, recv_sems = refs[2 * n:]
        x, y, c = _where_am_i()
        cps = [_remote(srcs[p].at[2 * cx + cy], outs[p].at[k], send_sems, recv_sems, 3 * p + k, (cx, cy, c))
               for p in range(n) for k, (cx, cy) in enumerate(_other_chips(x, y))]
        for cp in cps:
            cp.start()
        for cp in cps:
            cp.wait()

    return pl.pallas_call(
        body, name=name, in_specs=[HBM_SPEC] * n, out_specs=[HBM_SPEC] * n,
        out_shape=[jax.ShapeDtypeStruct((N_CHIPS - 1,) + s.shape[1:], s.dtype) for s in sums],
        scratch_shapes=[pltpu.SemaphoreType.DMA((3 * n,)), pltpu.SemaphoreType.DMA((3 * n,))],
    )(*sums)


def _sibling_send_many(halves, name):
    n = len(halves)

    def body(*refs):
        srcs, outs = refs[:n], refs[n:2 * n]
        send_sems, recv_sems = refs[2 * n:]
        x, y, c = _where_am_i()
        cps = [_remote(srcs[p], outs[p], send_sems, recv_sems, p, (x, y, 1 - c)) for p in range(n)]
        for cp in cps:
            cp.start()
        for cp in cps:
            cp.wait()

    return pl.pallas_call(
        body, name=name, in_specs=[HBM_SPEC] * n, out_specs=[HBM_SPEC] * n,
        out_shape=[jax.ShapeDtypeStruct(h.shape, h.dtype) for h in halves],
        scratch_shapes=[pltpu.SemaphoreType.DMA((n,)), pltpu.SemaphoreType.DMA((n,))],
    )(*halves)


def _row_tile(rows, pref=512):
    best = 8
    for t in range(8, pref + 1, 8):
        if rows % t == 0:
            best = t
    return best


def _where_ids():
    x, y, c = _where_am_i()
    return jnp.stack([c, 2 * x + y]).astype(jnp.int32)


RS_ROWS = 256


def _chip_sums_bf16(G, A, ids, name):
    n, _, hr, cols = G.shape
    tr = _row_tile(hr, RS_ROWS)

    def body(ids_ref, g_ref, a_ref, o_ref):
        o_ref[...] = (g_ref[...] + a_ref[...]).astype(BF16)

    return pl.pallas_call(
        body, name=name,
        grid_spec=pltpu.PrefetchScalarGridSpec(
            num_scalar_prefetch=1, grid=(n, hr // tr),
            in_specs=[pl.BlockSpec((None, None, tr, cols), lambda j, i, ids: (j, ids[0], i, 0)),
                      pl.BlockSpec((None, tr, cols), lambda j, i, ids: (j, i, 0))],
            out_specs=pl.BlockSpec((None, tr, cols), lambda j, i, ids: (j, i, 0))),
        out_shape=jax.ShapeDtypeStruct((n, hr, cols), BF16),
        compiler_params=_cp(("parallel", "parallel")),
    )(ids, G, A)


def _total_half(G, A, B, ids, name):
    _, _, hr, cols = G.shape
    tr = _row_tile(hr, RS_ROWS)

    def body(ids_ref, g_ref, a_ref, b_ref, o_ref):
        s = g_ref[...] + a_ref[...]
        for k in range(N_CHIPS - 1):
            s = s + b_ref[k].astype(F32)
        o_ref[...] = s

    return pl.pallas_call(
        body, name=name,
        grid_spec=pltpu.PrefetchScalarGridSpec(
            num_scalar_prefetch=1, grid=(hr // tr,),
            in_specs=[pl.BlockSpec((None, None, tr, cols), lambda i, ids: (ids[1], ids[0], i, 0)),
                      pl.BlockSpec((None, tr, cols), lambda i, ids: (ids[1], i, 0)),
                      pl.BlockSpec((N_CHIPS - 1, tr, cols), lambda i, ids: (0, i, 0))],
            out_specs=pl.BlockSpec((tr, cols), lambda i, ids: (i, 0))),
        out_shape=jax.ShapeDtypeStruct((hr, cols), F32),
        compiler_params=_cp(("parallel",)),
    )(ids, G, A, B)


def _small_allreduce(v, name):
    R, Wd = v.shape

    def body(x_ref, sum_ref, all_ref, send_sems, recv_sems, local_sem):
        x, y, c = _where_am_i()
        me, sib = (x, y, c), (x, y, 1 - c)
        chips = _other_chips(x, y)

        def slot(px, py, pc):
            return all_ref.at[4 * px + 2 * py + pc]

        local = pltpu.make_async_copy(x_ref, slot(*me), local_sem)
        local.start()
        first = [_remote(x_ref, slot(*me), send_sems, recv_sems, 0, sib)]
        first += [_remote(x_ref, slot(*me), send_sems, recv_sems, 1 + k, (cx, cy, c)) for k, (cx, cy) in enumerate(chips)]
        for cp in first:
            cp.start()
        passed = []
        for k, (cx, cy) in enumerate(chips):
            blk = slot(cx, cy, c)
            _remote(blk, blk, send_sems, recv_sems, 1 + k, me).wait_recv()
            fw = _remote(blk, blk, send_sems, recv_sems, 4 + k, sib)
            fw.start()
            passed.append(fw)
        _remote(slot(*sib), slot(*sib), send_sems, recv_sems, 0, me).wait_recv()
        for k, (cx, cy) in enumerate(chips):
            blk = slot(cx, cy, 1 - c)
            _remote(blk, blk, send_sems, recv_sems, 4 + k, me).wait_recv()
        for cp in first + passed:
            cp.wait_send()
        local.wait()
        s = all_ref[0]
        for d in range(1, N_DEV):
            s = s + all_ref[d]
        sum_ref[...] = s

    vm = pl.BlockSpec(memory_space=pltpu.VMEM)
    return pl.pallas_call(
        body, name=name, in_specs=[vm], out_specs=[vm, vm],
        out_shape=[jax.ShapeDtypeStruct((R, Wd), F32), jax.ShapeDtypeStruct((N_DEV, R, Wd), F32)],
        scratch_shapes=[pltpu.SemaphoreType.DMA((7,)), pltpu.SemaphoreType.DMA((7,)), pltpu.SemaphoreType.DMA],
    )(v)[0]


def _adamw(w, g, m, v, name):
    L, R, Wd = w.shape
    tr = _row_tile(R, RS_ROWS)
    c1 = 1.0 - ADAM_B1 ** ADAM_STEP
    c2 = 1.0 - ADAM_B2 ** ADAM_STEP

    def body(w_ref, g_ref, m_ref, v_ref, d_ref, nm_ref, nv_ref):
        gv = g_ref[...]
        nm = ADAM_B1 * m_ref[...] + (1.0 - ADAM_B1) * gv
        nv = ADAM_B2 * v_ref[...] + (1.0 - ADAM_B2) * (gv * gv)
        d_ref[...] = -ADAM_LR * ((nm / c1) / (jnp.sqrt(nv / c2) + ADAM_EPS) + ADAM_WD * w_ref[...])
        nm_ref[...] = nm
        nv_ref[...] = nv

    row = pl.BlockSpec((None, tr, Wd), lambda l, i: (l, i, 0))
    shp = jax.ShapeDtypeStruct((L, R, Wd), F32)
    return pl.pallas_call(
        body, name=name, grid=(L, R // tr), in_specs=[row] * 4, out_specs=[row] * 3, out_shape=[shp] * 3,
        compiler_params=_cp(("parallel", "parallel")),
    )(w, g, m, v)


def _adamw_halves(w, m, v, mine, theirs, ids, name, after=None):
    L, R, Wd = w.shape
    hr = R // 2
    tr = _row_tile(hr, RS_ROWS)
    c1 = 1.0 - ADAM_B1 ** ADAM_STEP
    c2 = 1.0 - ADAM_B2 ** ADAM_STEP
    ordered = [] if after is None else [after]

    def body(ids_ref, w_ref, m_ref, v_ref, *refs):
        g_refs, (g_ref, d_ref, nm_ref, nv_ref) = refs[:2 * L], refs[-4:]
        layer, half = pl.program_id(0), pl.program_id(1)
        own = half == ids_ref[0]
        gv = jnp.where(own, g_refs[0][...], g_refs[L][...])
        for l in range(1, L):
            gv = jnp.where(layer == l, jnp.where(own, g_refs[l][...], g_refs[L + l][...]), gv)
        nm = ADAM_B1 * m_ref[...] + (1.0 - ADAM_B1) * gv
        nv = ADAM_B2 * v_ref[...] + (1.0 - ADAM_B2) * (gv * gv)
        g_ref[...] = gv
        d_ref[...] = -ADAM_LR * ((nm / c1) / (jnp.sqrt(nv / c2) + ADAM_EPS) + ADAM_WD * w_ref[...])
        nm_ref[...] = nm
        nv_ref[...] = nv

    blk = pl.BlockSpec((None, None, tr, Wd), lambda l, h, i, ids: (l, h, i, 0))
    g_blk = pl.BlockSpec((tr, Wd), lambda l, h, i, ids: (i, 0))
    shp = jax.ShapeDtypeStruct((L, 2, hr, Wd), F32)
    outs = pl.pallas_call(
        body, name=name,
        grid_spec=pltpu.PrefetchScalarGridSpec(
            num_scalar_prefetch=1, grid=(L, 2, hr // tr),
            in_specs=[blk] * 3 + [g_blk] * (2 * L)
            + [pl.BlockSpec((8, LANE), lambda l, h, i, ids: (0, 0)) for _ in ordered], out_specs=[blk] * 4),
        out_shape=[shp] * 4,
        compiler_params=_cp(("parallel", "parallel", "parallel")),
    )(ids, *[a.reshape(L, 2, hr, Wd) for a in (w, m, v)], *mine, *theirs, *ordered)
    return tuple(o.reshape(L, R, Wd) for o in outs)


def _two_halves(a):
    cols = a.shape[-1]
    return a.reshape(2, _size(a.shape) // (2 * cols), cols)


FIRST_NEEDED = ("w_in_e",)
LATER_NEEDED = tuple(n for n in BIG if n not in FIRST_NEEDED)


def _gather_weights(P):
    chip = 2 * lax.axis_index("x") + lax.axis_index("y")

    def with_own(landed, own):
        return lax.dynamic_update_slice_in_dim(landed, own[None], chip, axis=0)

    mine = {n: _two_halves(P[n].astype(BF16)) for n in BIG}
    first = _chip_allgather_many([mine[n] for n in FIRST_NEEDED], "ag_first")
    gathered = {n: with_own(g, mine[n]) for n, g in zip(FIRST_NEEDED, first)}
    send_sems, recv_sems, blocks, lands, token = _chip_allgather_start([mine[n] for n in LATER_NEEDED], "ag_start")

    def later(after):
        landed = _chip_allgather_wait(send_sems, recv_sems, blocks, lands, after, "ag_wait")
        g = {n: with_own(a, mine[n]) for n, a in zip(LATER_NEEDED, _chip_allgather_forward(landed, "ag_forward"))}
        w_in_o = g["w_in_o"].reshape(N_CHIPS, D_MODEL, ODD_IN // N_CHIPS)
        return {
            "w_out_e": g["w_out_e"].reshape(D_MODEL, D_MODEL),
            "w_out_o": g["w_out_o"].reshape(D_MODEL, D_MODEL),
            "w_in_o": jnp.pad(jnp.concatenate([w_in_o[j] for j in range(N_CHIPS)], axis=1),
                              ((0, 0), (0, ODD_IN_PAD - ODD_IN))),
            "w_up": g["w_up"],
            "w_down": g["w_down"].transpose(1, 0, 2, 3).reshape(2, FFN_DIM, D_MODEL),
            "w_ple_gate": g["w_ple_gate"].transpose(1, 0, 2, 3).reshape(2, D_MODEL, D_MODEL),
            "w_ple": g["w_ple"].transpose(1, 2, 0, 3).reshape(2, PLE_DIM, D_MODEL),
        }

    small_shapes = [P[n].shape for n, _ in SHARDED_SMALL]
    small = _chip_allgather(_pack([P[n] for n, _ in SHARDED_SMALL], SMALL_W, SMALL_ROWS), "ag_small")
    parts = [_unpack(small[j], small_shapes) for j in range(N_CHIPS)]
    full = {n: jnp.concatenate([parts[j][i] for j in range(N_CHIPS)], axis=ax)
            for i, (n, ax) in enumerate(SHARDED_SMALL)}
    W = {n: P[n] for n in REPLICATED}
    W["pool_w"] = P["pool_w"][0]
    W["final_norm"] = P["final_norm"].reshape(1, D_MODEL)
    W["mix_norm_o"] = full["mix_norm_o"]
    W["conv_qkv_o"] = full["conv_qkv_o"][0]
    W["ffn_conv"] = full["ffn_conv"]
    W["w_in_e"] = gathered["w_in_e"].reshape(N_CHIPS, D_MODEL, IN_E_SHARD)
    return W, token, later


def _chip_major_w_in_o(g):
    shard = ODD_IN // N_CHIPS
    return jnp.stack([g[:, j * shard:(j + 1) * shard] for j in range(N_CHIPS)])


def _reduce_begin(grads, ids, tag, travel_later):
    keys = list(grads)
    pieces = [g.reshape(N_CHIPS, 2, g.shape[1] // 2, g.shape[2]) for g in grads.values()]
    from_sibling = _sibling_swap_many(pieces, f"rs_sibling_swap_{tag}")
    sums = [_chip_sums_bf16(g, a, ids, f"rs_chip_sums_{tag}{i}") for i, (g, a) in enumerate(zip(pieces, from_sibling))]
    state = dict(keys=keys, pieces=pieces, from_sibling=from_sibling, ids=ids, tag=tag, token=None)
    if travel_later:
        state["flight"] = _chip_scatter_start(sums, f"rs_scatter_start_{tag}")
        state["token"] = state["flight"][-1]
    else:
        state["from_chips"] = _chip_scatter_many(sums, f"rs_chip_scatter_{tag}")
    return state


def _reduce_end(state, after=None):
    tag, ids = state["tag"], state["ids"]
    if "flight" in state:
        send_sems, recv_sems, sums, lands, _ = state["flight"]
        from_chips = _chip_scatter_wait(send_sems, recv_sems, sums, lands, after, f"rs_scatter_wait_{tag}")
    else:
        from_chips = state["from_chips"]
    halves = [_total_half(g, a, b, ids, f"rs_total_{tag}{i}")
              for i, (g, a, b) in enumerate(zip(state["pieces"], state["from_sibling"], from_chips))]
    theirs = _sibling_send_many(halves, f"rs_sibling_send_{tag}")
    return {k: (h, t) for k, h, t in zip(state["keys"], halves, theirs)}


def kernel(x, p, mix_norm_e, w_in_e, pool_w, pool_scale, w_out_e, mix_norm_o, w_in_o, conv_qkv_o, a_log_o, dt_bias_o, gdn_norm_o, w_out_o, ffn_norm, w_up, ffn_conv, w_down, ple_norm, w_ple_gate, w_ple, final_norm, loss_target, m_mix_norm_e, m_w_in_e, m_pool_w, m_pool_scale, m_w_out_e, m_mix_norm_o, m_w_in_o, m_conv_qkv_o, m_a_log_o, m_dt_bias_o, m_gdn_norm_o, m_w_out_o, m_ffn_norm, m_w_up, m_ffn_conv, m_w_down, m_ple_norm, m_w_ple_gate, m_w_ple, m_final_norm, v_mix_norm_e, v_w_in_e, v_pool_w, v_pool_scale, v_w_out_e, v_mix_norm_o, v_w_in_o, v_conv_qkv_o, v_a_log_o, v_dt_bias_o, v_gdn_norm_o, v_w_out_o, v_ffn_norm, v_w_up, v_ffn_conv, v_w_down, v_ple_norm, v_w_ple_gate, v_w_ple, v_final_norm):
    args = locals()
    P = {n: args[n] for n in WEIGHT_ORDER}
    M = {n: args["m_" + n] for n in WEIGHT_ORDER}
    V = {n: args["v_" + n] for n in WEIGHT_ORDER}

    W, token, later_weights = _gather_weights(P)
    T = x.shape[1]
    ids = _where_ids()
    early = {}

    def on_grads(stage, grads):
        early[stage] = _reduce_begin(grads, ids, stage, travel_later=True)
        return early[stage]["token"]

    sq, grad_x, G = _local_step(x.reshape(T, D_MODEL), p.reshape(2, T, PLE_DIM), loss_target.reshape(T, D_MODEL), W,
                                token, later_weights, on_grads)
    last = _reduce_begin({("w_in_e", 0): G["w_in_e"], ("w_out_e", 0): G["w_out_e"]}, ids, "first_mixer",
                         travel_later=True)
    reduced = {}
    for state in early.values():
        reduced.update(_reduce_end(state, after=grad_x))
    out = {}

    def adamw(n, after=None):
        halves = [reduced[(n, l)] for l in range(P[n].shape[0])]
        out[n] = _adamw_halves(P[n], M[n], V[n], [h[0] for h in halves], [h[1] for h in halves], ids, f"adamw_{n}",
                               after)

    last_names = ("w_in_e", "w_out_e")
    for n in BIG:
        if n not in last_names:
            adamw(n, after=last["token"])
    reduced.update(_reduce_end(last, after=out["w_up"][1]))
    for n in last_names:
        adamw(n)

    small_full = {n: G[n] for n in REPLICATED}
    small_full["pool_w"] = G["pool_w"][None]
    small_full["final_norm"] = G["final_norm"].reshape(D_MODEL)
    small_full["mix_norm_o"] = G["mix_norm_o"]
    small_full["conv_qkv_o"] = G["conv_qkv_o"][None]
    small_full["ffn_conv"] = G["ffn_conv"]
    small_names = REPLICATED + tuple(n for n, _ in SHARDED_SMALL)
    summed = _small_allreduce(_pack([small_full[n] for n in small_names] + [sq.reshape(1)], SMALL_W, 8), "ar_small")
    *g_list, sq_total = _unpack(summed, [small_full[n].shape for n in small_names] + [(1,)])
    g_small = dict(zip(small_names, g_list))
    chip = 2 * lax.axis_index("x") + lax.axis_index("y")
    for n, ax in SHARDED_SMALL:
        width = P[n].shape[ax]
        g_small[n] = lax.dynamic_slice_in_dim(g_small[n], chip * width, width, axis=ax)

    def pack_small(D):
        return _pack([D[n] for n in small_names], SMALL_W, RS_ROWS)[None]

    g_pack = pack_small(g_small)
    upd = _adamw(pack_small(P), g_pack, pack_small(M), pack_small(V), "adamw_small")
    shapes = [P[n].shape for n in small_names]
    for n, *vals in zip(small_names, *[_unpack(a[0], shapes) for a in (g_pack,) + tuple(upd)]):
        out[n] = tuple(vals)

    loss = (0.5 / D_MODEL) * sq_total[0]
    return (loss, grad_x[None]) + tuple(out[n][i] for i in range(4) for n in WEIGHT_ORDER)
```

```python
import functools

import jax
import jax.numpy as jnp
from jax import lax
from jax.experimental import pallas as pl
from jax.experimental.pallas import tpu as pltpu

F32 = jnp.float32
BF16 = jnp.bfloat16

D_MODEL = 1024
PLE_DIM = 256
POOL_WIDTH = 512
POOL_WINDOWS = (2, 4, 8, 16)
POOL_GROUP_DIM = 128
SB_HEADS = 8
SB_HEAD_DIM = 64
GDN_HEADS = 8
GDN_HEAD_DIM = 128
GDN_CONV = 4
GDN_CHUNK = 64
FFN_DIM = 2816
FFN_CONV = 3
EPS = 1e-6
ODD_IN = 4 * D_MODEL + 2 * GDN_HEADS
ODD_IN_PAD = 33 * 128
ADAM_LR, ADAM_B1, ADAM_B2, ADAM_EPS, ADAM_WD, ADAM_STEP = 0.001, 0.9, 0.999, 1e-08, 0.01, 10

LANE = 128
VMEM_LIMIT = 56 * 1024 * 1024

N_CHIPS = 4
N_DEV = 8


def _cp(sem=None):
    return pltpu.CompilerParams(dimension_semantics=sem, vmem_limit_bytes=VMEM_LIMIT)


def _tile(n, pref):
    if n <= pref:
        return n
    best = None
    for t in range(LANE, pref + 1, LANE):
        if n % t == 0:
            best = t
    assert best is not None, (n, pref)
    return best


_DIMS = {"nn": (((1,), (0,)), ((), ())), "nt": (((1,), (1,)), ((), ())), "tn": (((0,), (0,)), ((), ()))}
_BDIMS = {"nn": (((2,), (1,)), ((0,), (0,))), "nt": (((2,), (2,)), ((0,), (0,))), "tn": (((1,), (1,)), ((0,), (0,)))}


def _dims(mode, ndim):
    return (_BDIMS if ndim == 3 else _DIMS)[mode]


def _dot(a, b, mode="nn"):
    return lax.dot_general(a.astype(BF16), b.astype(BF16), _dims(mode, a.ndim), preferred_element_type=F32)


def _bdot(a, b, mode="nn"):
    return lax.dot_general(a.astype(BF16), b.astype(BF16), _BDIMS[mode], preferred_element_type=F32)


def _split2(x):
    hi = x.astype(BF16)
    lo = (x - hi.astype(F32)).astype(BF16)
    return hi, lo


def _split3(x):
    hi = x.astype(BF16)
    r = x - hi.astype(F32)
    mid = r.astype(BF16)
    lo = (r - mid.astype(F32)).astype(BF16)
    return hi, mid, lo


def _dot_x01(x, m01, mode="nn"):
    hi, lo = _split2(x)
    return (lax.dot_general(hi, m01, _DIMS[mode], preferred_element_type=F32)
            + lax.dot_general(lo, m01, _DIMS[mode], preferred_element_type=F32))


def _dot3_raw(a, b, mode):
    ah, al = _split2(a)
    bh, bl = _split2(b)
    d = _dims(mode, a.ndim)
    return (lax.dot_general(ah, bh, d, preferred_element_type=F32)
            + lax.dot_general(ah, bl, d, preferred_element_type=F32)
            + lax.dot_general(al, bh, d, preferred_element_type=F32))


@jax.custom_vjp
def _dot3(a, b):
    return _dot3_raw(a, b, "nn")


def _dot3_fwd(a, b):
    return _dot3_raw(a, b, "nn"), (a, b)


def _dot3_bwd(res, g):
    a, b = res
    return _dot(g, b, "nt"), _dot(a, g, "tn")


_dot3.defvjp(_dot3_fwd, _dot3_bwd)


@jax.custom_vjp
def _dot1_nt(a, b):
    return _dot(a, b, "nt")


def _dot1_nt_fwd(a, b):
    return _dot(a, b, "nt"), (a, b)


def _dot1_nt_bwd(res, g):
    a, b = res
    return _dot(g, b, "nn"), _dot(g, a, "tn")


_dot1_nt.defvjp(_dot1_nt_fwd, _dot1_nt_bwd)


def _m01_left_raw(m, x):
    d = _dims("nn", x.ndim)
    if x.ndim == 3:
        m = jnp.broadcast_to(m, (x.shape[0],) + m.shape)
    p0, p1, p2 = _split3(x)
    return (lax.dot_general(m, p0, d, preferred_element_type=F32)
            + lax.dot_general(m, p1, d, preferred_element_type=F32)
            + lax.dot_general(m, p2, d, preferred_element_type=F32))


@jax.custom_vjp
def _m01_left(m, mt, x):
    return _m01_left_raw(m, x)


def _m01_left_fwd(m, mt, x):
    return _m01_left_raw(m, x), (m, mt)


def _m01_left_bwd(res, g):
    m, mt = res
    return jnp.zeros_like(m), jnp.zeros_like(mt), _m01_left_raw(mt, g)


_m01_left.defvjp(_m01_left_fwd, _m01_left_bwd)


def _softplus(x):
    return jnp.maximum(x, 0.0) + jnp.log(1.0 + jnp.exp(-jnp.abs(x)))


def _sigmoid(x):
    return 0.5 * jnp.tanh(0.5 * x) + 0.5


def _silu(x):
    return x * _sigmoid(x)


def _dsilu(x):
    s = _sigmoid(x)
    return s * (1.0 + x * (1.0 - s))


def _cols_of(n_blocks_per_part, *fixed):
    return lambda r, c: (c // n_blocks_per_part,) + fixed + (r, c % n_blocks_per_part)


def _rows_of(n_blocks_per_part, *fixed):
    return lambda r, c: (r // n_blocks_per_part,) + fixed + (r % n_blocks_per_part, c)


def _layer_of(layer):
    return lambda r, c: (layer, r, c)


def _mm(a, b, mode, name, out_dtype=F32, res=None, tm=1024, tn=512, tk=1024,
        dims=None, a_view=None, b_view=None, o_view=None, out_shape=None):
    if dims is None:
        if mode == "nn":
            (M, K), (K2, N) = a.shape, b.shape
        elif mode == "nt":
            (M, K), (N, K2) = a.shape, b.shape
        else:
            (K, M), (K2, N) = a.shape, b.shape
        assert K == K2, (name, a.shape, b.shape)
    else:
        M, N, K = dims
    tm, tn, tk = _tile(M, tm), _tile(N, tn), _tile(K, tk)
    nk = K // tk

    def spec(arr, blk, view, rc):
        view = view or (lambda r, c: (r, c))
        return pl.BlockSpec((None,) * (arr.ndim - 2) + blk, lambda i, j, k: view(*rc(i, j, k)))

    if mode == "tn":
        a_spec = spec(a, (tk, tm), a_view, lambda i, j, k: (k, i))
    else:
        a_spec = spec(a, (tm, tk), a_view, lambda i, j, k: (i, k))
    if mode == "nt":
        b_spec = spec(b, (tn, tk), b_view, lambda i, j, k: (j, k))
    else:
        b_spec = spec(b, (tk, tn), b_view, lambda i, j, k: (k, j))
    out_shape = out_shape or (M, N)
    o_spec = pl.BlockSpec((None,) * (len(out_shape) - 2) + (tm, tn),
                          lambda i, j, k: (o_view or (lambda r, c: (r, c)))(i, j))
    has_res = res is not None
    assert not (has_res and o_view), name

    def body(*refs):
        a_ref, b_ref = refs[:2]
        r_ref = refs[2] if has_res else None
        o_ref = refs[3] if has_res else refs[2]

        def finish(r):
            if has_res:
                r = r + r_ref[...]
            o_ref[...] = r.astype(out_dtype)

        if nk == 1:
            finish(_dot(a_ref[...], b_ref[...], mode))
            return
        acc = refs[-1]
        k = pl.program_id(2)

        @pl.when(k == 0)
        def _():
            acc[...] = jnp.zeros_like(acc)

        acc[...] += _dot(a_ref[...], b_ref[...], mode)

        @pl.when(k == nk - 1)
        def _():
            finish(acc[...])

    ins = [a, b] + ([res] if has_res else [])
    in_specs = [a_spec, b_spec] + ([o_spec] if has_res else [])
    return pl.pallas_call(
        body, name=name, grid=(M // tm, N // tn, nk),
        in_specs=in_specs, out_specs=o_spec,
        out_shape=jax.ShapeDtypeStruct(out_shape, out_dtype),
        scratch_shapes=[pltpu.VMEM((tm, tn), F32)] if nk > 1 else [],
        compiler_params=_cp(("parallel", "parallel", "arbitrary")),
    )(*ins)


def _rms_fwd(x, gain, name, after=None):
    T, D = x.shape
    tt = _tile(T, 512)

    def body(x_ref, g_ref, *rest):
        o_ref = rest[-1]
        xv = x_ref[...]
        r = lax.rsqrt(jnp.mean(xv * xv, axis=-1, keepdims=True) + EPS)
        o_ref[...] = (xv * r * g_ref[...]).astype(BF16)

    ordered = [] if after is None else [after]
    return pl.pallas_call(
        body, name=name, grid=(T // tt,),
        in_specs=[pl.BlockSpec((tt, D), lambda i: (i, 0)), pl.BlockSpec((1, D), lambda i: (0, 0))]
        + [pl.BlockSpec((8, LANE), lambda i: (0, 0)) for _ in ordered],
        out_specs=pl.BlockSpec((tt, D), lambda i: (i, 0)),
        out_shape=jax.ShapeDtypeStruct((T, D), BF16),
        compiler_params=_cp(("parallel",)),
    )(x, gain, *ordered)


def _rms_bwd(x, gain, dh, dres, name):
    T, D = x.shape
    tt = _tile(T, 512)

    def body(x_ref, g_ref, dh_ref, dr_ref, dx_ref, dg_ref):
        i = pl.program_id(0)
        xv = x_ref[...]
        dy = dh_ref[...].astype(F32)
        r = lax.rsqrt(jnp.mean(xv * xv, axis=-1, keepdims=True) + EPS)
        xn = xv * r
        gdy = dy * g_ref[...]
        dx = r * (gdy - xn * jnp.mean(gdy * xn, axis=-1, keepdims=True))
        dx_ref[...] = dr_ref[...] + dx

        @pl.when(i == 0)
        def _():
            dg_ref[...] = jnp.zeros_like(dg_ref)

        dg_ref[...] += jnp.sum(dy * xn, axis=0, keepdims=True)

    row = pl.BlockSpec((tt, D), lambda i: (i, 0))
    vec = pl.BlockSpec((1, D), lambda i: (0, 0))
    return pl.pallas_call(
        body, name=name, grid=(T // tt,),
        in_specs=[row, vec, row, row], out_specs=[row, vec],
        out_shape=[jax.ShapeDtypeStruct((T, D), F32), jax.ShapeDtypeStruct((1, D), F32)],
        compiler_params=_cp(("arbitrary",)),
    )(x, gain, dh, dres)


def _final_loss(x, gain, target, name):
    T, D = x.shape
    tt = _tile(T, 512)

    def body(x_ref, g_ref, t_ref, l_ref, dx_ref, dg_ref):
        i = pl.program_id(0)
        xv = x_ref[...]
        r = lax.rsqrt(jnp.mean(xv * xv, axis=-1, keepdims=True) + EPS)
        xn = xv * r
        err = xn * g_ref[...] - t_ref[...]
        dy = err * (1.0 / D)
        gdy = dy * g_ref[...]
        dx_ref[...] = r * (gdy - xn * jnp.mean(gdy * xn, axis=-1, keepdims=True))

        @pl.when(i == 0)
        def _():
            dg_ref[...] = jnp.zeros_like(dg_ref)
            l_ref[...] = jnp.zeros_like(l_ref)

        dg_ref[...] += jnp.sum(dy * xn, axis=0, keepdims=True)
        l_ref[...] += jnp.sum(jnp.sum(err * err, axis=1, keepdims=True), axis=0, keepdims=True)

    row = pl.BlockSpec((tt, D), lambda i: (i, 0))
    vec = pl.BlockSpec((1, D), lambda i: (0, 0))
    return pl.pallas_call(
        body, name=name, grid=(T // tt,),
        in_specs=[row, vec, row],
        out_specs=[pl.BlockSpec((8, LANE), lambda i: (0, 0)), row, vec],
        out_shape=[jax.ShapeDtypeStruct((8, LANE), F32), jax.ShapeDtypeStruct((T, D), F32),
                   jax.ShapeDtypeStruct((1, D), F32)],
        compiler_params=_cp(("arbitrary",)),
    )(x, gain, target)


def _shift_down(x, i, t_idx):
    if i == 0:
        return x
    return jnp.where(t_idx >= i, pltpu.roll(x, i, 0), 0.0)


def _shift_up(x, i, t_idx):
    if i == 0:
        return x
    n = x.shape[0]
    return jnp.where(t_idx < n - i, pltpu.roll(x, n - i, 0), 0.0)


def _pool_select(g, vals):
    out = vals[-1]
    for gi in range(len(vals) - 2, -1, -1):
        out = jnp.where(g == gi, vals[gi], out)
    return out


def _pool_y(u, g, t_idx):
    s1 = u + _shift_down(u, 1, t_idx)
    s2 = s1 + _shift_down(s1, 2, t_idx)
    s3 = s2 + _shift_down(s2, 4, t_idx)
    s4 = s3 + _shift_down(s3, 8, t_idx)
    ws = _pool_select(g, [s1, s2, s3, s4])
    win = _pool_select(g, [jnp.float32(w) for w in POOL_WINDOWS])
    cnt = jnp.minimum(t_idx.astype(F32) + 1.0, win)
    return ws / cnt - u, cnt


def _pool_fwd(proj, pool_w, pool_scale):
    T = proj.shape[0]
    G, C = len(POOL_WINDOWS), POOL_GROUP_DIM

    def body(u_ref, w_ref, s_ref, o_ref):
        g = pl.program_id(0)
        t_idx = lax.broadcasted_iota(jnp.int32, (T, C), 0)
        y, _ = _pool_y(u_ref[...], g, t_idx)
        o_ref[...] = _dot(y, w_ref[0]) * s_ref[...]

    return pl.pallas_call(
        body, name="pool_fwd", grid=(G,),
        in_specs=[pl.BlockSpec((T, C), lambda g: (0, g)), pl.BlockSpec((1, C, C), lambda g: (g, 0, 0)),
                  pl.BlockSpec((1, C), lambda g: (0, g))],
        out_specs=pl.BlockSpec((T, C), lambda g: (0, g)),
        out_shape=jax.ShapeDtypeStruct((T, G * C), F32),
        compiler_params=_cp(("parallel",)),
    )(proj, pool_w, pool_scale)


def _pool_bwd(proj, pool_w, pool_scale, dmix):
    T = proj.shape[0]
    G, C = len(POOL_WINDOWS), POOL_GROUP_DIM

    def body(u_ref, w_ref, s_ref, do_ref, du_ref, dw_ref, ds_ref):
        g = pl.program_id(0)
        t_idx = lax.broadcasted_iota(jnp.int32, (T, C), 0)
        y, cnt = _pool_y(u_ref[...], g, t_idx)
        w = w_ref[0]
        dout = do_ref[...]
        ds_ref[...] = jnp.sum(dout * _dot(y, w), axis=0, keepdims=True)
        dy2 = dout * s_ref[...]
        dw_ref[0] = _dot(y, dy2, "tn")
        dy = _dot(dy2, w, "nt")
        dz = dy / cnt
        r1 = dz + _shift_up(dz, 1, t_idx)
        r2 = r1 + _shift_up(r1, 2, t_idx)
        r3 = r2 + _shift_up(r2, 4, t_idx)
        r4 = r3 + _shift_up(r3, 8, t_idx)
        du_ref[...] = _pool_select(g, [r1, r2, r3, r4]) - dy

    col = pl.BlockSpec((T, C), lambda g: (0, g))
    return pl.pallas_call(
        body, name="pool_bwd", grid=(G,),
        in_specs=[col, pl.BlockSpec((1, C, C), lambda g: (g, 0, 0)), pl.BlockSpec((1, C), lambda g: (0, g)), col],
        out_specs=[col, pl.BlockSpec((1, C, C), lambda g: (g, 0, 0)), pl.BlockSpec((1, C), lambda g: (0, g))],
        out_shape=[jax.ShapeDtypeStruct((T, G * C), F32), jax.ShapeDtypeStruct((G, C, C), F32),
                   jax.ShapeDtypeStruct((1, G * C), F32)],
        compiler_params=_cp(("parallel",)),
    )(proj, pool_w, pool_scale, dmix)


SB_SCALE = SB_HEAD_DIM ** -0.5
SB_PASS_SIZES = (4, 2, 1)
SB_PASS_SIZES_BWD = (2, 1)


def _sb_tile_logits(qb, kblk, valid):
    z = _dot(qb, kblk, "nt")
    sp = _softplus(z)
    l1m = -sp
    if valid is not None:
        l1m = jnp.where(valid, l1m, 0.0)
    return z, sp, l1m


SB_PAIR = LANE // SB_HEAD_DIM
SB_Q0 = POOL_WIDTH // LANE
SB_NB = SB_HEADS // SB_PAIR


def _sb_head_masks():
    lane = lax.broadcasted_iota(jnp.int32, (1, LANE), 1)
    return [(lane // SB_HEAD_DIM == h).astype(F32) for h in range(SB_PAIR)]


def _sb_fwd(proj):
    T = proj.shape[0]
    B = _tile(T, 256)
    nq = T // B

    def body(q_ref, k_ref, v_ref, o_ref, l_ref, k_bf, v_bf):
        qi = pl.program_id(1)

        @pl.when(qi == 0)
        def _():
            k_bf[...] = k_ref[...].astype(BF16)
            v_bf[...] = v_ref[...].astype(BF16)

        masks = _sb_head_masks()
        q_all = q_ref[...]
        qbs = [(q_all * (m * SB_SCALE)).astype(BF16) for m in masks]
        row = lax.broadcasted_iota(jnp.int32, (B, B), 0)
        col = lax.broadcasted_iota(jnp.int32, (B, B), 1)
        later = (row > col).astype(BF16)

        def tiles(kbs, state, valid):
            ksl = [pl.ds(pl.multiple_of(kb * B, B), B) for kb in kbs]
            kblks = [k_bf[ks, :] for ks in ksl]
            masks_of = [valid] + [None] * (len(kbs) - 1)
            logits = [[_sb_tile_logits(qb, kblk, m) for kblk, m in zip(kblks, masks_of)] for qb in qbs]
            within = [[_dot_x01(l1m, later) for _, _, l1m in lg] for lg in logits]
            sums = [[jnp.sum(l1m, axis=1, keepdims=True) for _, _, l1m in lg] for lg in logits]
            out = []
            for h, (carry, acc) in enumerate(state):
                for (z, sp, _), rc, s, ks, m in zip(logits[h], within[h], sums[h], ksl, masks_of):
                    a = jnp.exp(z - sp + rc + carry)
                    if m is not None:
                        a = jnp.where(m, a, 0.0)
                    acc = acc + _dot(a, v_bf[ks, :])
                    carry = carry + s
                out.append((carry, acc))
            return tuple(out)

        state = ((jnp.zeros((B, 1), F32), jnp.zeros((B, LANE), F32)),) * SB_PAIR
        state = lax.cond(qi >= 1, lambda c: tiles([qi, qi - 1], c, col < row), lambda c: tiles([qi], c, col < row),
                         state)
        left = jnp.maximum(qi - 1, 0)
        for size in SB_PASS_SIZES:
            n_pass = left // size
            state = lax.fori_loop(
                0, n_pass, lambda i, c, left=left, size=size: tiles([left - 1 - size * i - u for u in range(size)],
                                                                     c, None), state)
            left = left - n_pass * size
        o_ref[...] = sum(acc * m for (_, acc), m in zip(state, masks))
        for h, (carry, _) in enumerate(state):
            l_ref[h] = carry

    return pl.pallas_call(
        body, name="sb_fwd", grid=(SB_NB, nq),
        in_specs=[pl.BlockSpec((B, LANE), lambda hp, i: (i, SB_Q0 + hp)),
                  pl.BlockSpec((T, LANE), lambda hp, i: (0, SB_Q0 + SB_NB + hp)),
                  pl.BlockSpec((T, LANE), lambda hp, i: (0, SB_Q0 + 2 * SB_NB + hp))],
        out_specs=[pl.BlockSpec((B, LANE), lambda hp, i: (i, hp)),
                   pl.BlockSpec((SB_PAIR, B, 1), lambda hp, i: (hp, i, 0))],
        out_shape=[jax.ShapeDtypeStruct((T, SB_HEADS * SB_HEAD_DIM), F32), jax.ShapeDtypeStruct((SB_HEADS, T, 1), F32)],
        scratch_shapes=[pltpu.VMEM((T, LANE), BF16), pltpu.VMEM((T, LANE), BF16)],
        compiler_params=_cp(("parallel", "arbitrary")),
    )(proj, proj, proj)


def _sb_bwd(proj, dmix, ltot, after=None):
    T = proj.shape[0]
    B = _tile(T, 256)
    nq = T // B
    ordered = [] if after is None else [after]

    def body(q_ref, k_ref, v_ref, do_ref, l_ref, *rest):
        dq_ref, dk_ref, dv_ref, k_bf, v_bf = rest[len(ordered):]
        qi = pl.program_id(1)

        @pl.when(qi == 0)
        def _():
            k_bf[...] = k_ref[...].astype(BF16)
            v_bf[...] = v_ref[...].astype(BF16)
            dk_ref[...] = jnp.zeros_like(dk_ref)
            dv_ref[...] = jnp.zeros_like(dv_ref)

        masks = _sb_head_masks()
        q_all, do_all = q_ref[...], do_ref[...]
        qbs = [(q_all * (m * SB_SCALE)).astype(BF16) for m in masks]
        dobs = [(do_all * m).astype(BF16) for m in masks]
        ltots = [l_ref[h] for h in range(SB_PAIR)]
        row = lax.broadcasted_iota(jnp.int32, (B, B), 0)
        col = lax.broadcasted_iota(jnp.int32, (B, B), 1)
        upto = (row <= col).astype(BF16)
        before = (row < col).astype(BF16)

        def tiles(kbs, state, valid):
            ksl = [pl.ds(pl.multiple_of(kb * B, B), B) for kb in kbs]
            kblks = [k_bf[ks, :] for ks in ksl]
            vblks = [v_bf[ks, :] for ks in ksl]
            masks_of = [None] * (len(kbs) - 1) + [valid]
            logits = [[_sb_tile_logits(qb, kblk, m) for kblk, m in zip(kblks, masks_of)] for qb in qbs]
            das = [[_dot(dob, vblk, "nt") for vblk in vblks] for dob in dobs]
            within = [[_dot_x01(l1m, upto) for _, _, l1m in lg] for lg in logits]
            avals, es, Ps = [], [], []
            for h, (P, _, _) in enumerate(state):
                a_h, e_h = [], []
                for (z, sp, l1m), pc, da, m in zip(logits[h], within[h], das[h], masks_of):
                    a = jnp.exp(z - sp + (ltots[h] - P - pc))
                    if m is not None:
                        a = jnp.where(m, a, 0.0)
                    a_h.append(a)
                    e_h.append(da * a)
                    P = P + jnp.sum(l1m, axis=1, keepdims=True)
                avals.append(a_h)
                es.append(e_h)
                Ps.append(P)
            e_within = [[_dot_x01(e, before) for e in e_h] for e_h in es]
            out = []
            for h, (_, E, dq) in enumerate(state):
                for (z, sp, _), e, ew, a, kblk, ks, m in zip(logits[h], es[h], e_within[h], avals[h], kblks, ksl,
                                                              masks_of):
                    dz = e * jnp.exp(-sp) - jnp.exp(z - sp) * (ew + E)
                    if m is not None:
                        dz = jnp.where(m, dz, 0.0)
                    dzb = dz.astype(BF16)
                    dq = dq + _dot(dzb, kblk)
                    dk_ref[ks, :] += _dot(dzb, qbs[h], "tn")
                    dv_ref[ks, :] += _dot(a, dobs[h], "tn")
                    E = E + jnp.sum(e, axis=1, keepdims=True)
                out.append((Ps[h], E, dq))
            return tuple(out)

        zeros1 = jnp.zeros((B, 1), F32)
        state = ((zeros1, zeros1, jnp.zeros((B, LANE), F32)),) * SB_PAIR
        done = 0
        before_last = jnp.maximum(qi - 1, 0)
        for size in SB_PASS_SIZES_BWD:
            n_pass = (before_last - done) // size
            state = lax.fori_loop(
                0, n_pass, lambda i, c, done=done, size=size: tiles([done + size * i + u for u in range(size)], c, None),
                state)
            done = done + n_pass * size
        state = lax.cond(qi >= 1, lambda c: tiles([qi - 1, qi], c, col < row), lambda c: tiles([qi], c, col < row),
                         state)
        dq_ref[...] = sum(dq * (m * SB_SCALE) for (_, _, dq), m in zip(state, masks))

    qspec = pl.BlockSpec((B, LANE), lambda hp, i: (i, SB_Q0 + hp))
    wide = jax.ShapeDtypeStruct((T, SB_HEADS * SB_HEAD_DIM), F32)
    return pl.pallas_call(
        body, name="sb_bwd", grid=(SB_NB, nq),
        in_specs=[qspec,
                  pl.BlockSpec((T, LANE), lambda hp, i: (0, SB_Q0 + SB_NB + hp)),
                  pl.BlockSpec((T, LANE), lambda hp, i: (0, SB_Q0 + 2 * SB_NB + hp)),
                  qspec,
                  pl.BlockSpec((SB_PAIR, B, 1), lambda hp, i: (hp, i, 0))]
        + [pl.BlockSpec((8, LANE), lambda hp, i: (0, 0)) for _ in ordered],
        out_specs=[pl.BlockSpec((B, LANE), lambda hp, i: (i, hp)),
                   pl.BlockSpec((T, LANE), lambda hp, i: (0, hp)),
                   pl.BlockSpec((T, LANE), lambda hp, i: (0, hp))],
        out_shape=[wide, wide, wide],
        scratch_shapes=[pltpu.VMEM((T, LANE), BF16), pltpu.VMEM((T, LANE), BF16)],
        compiler_params=_cp(("parallel", "arbitrary")),
    )(proj, proj, proj, dmix, ltot, *ordered)


def _rows(w_ref, K):
    return [w_ref[i:i + 1, :] for i in range(K)]


def _conv(x, ws, t_idx):
    K = len(ws)
    y = ws[K - 1] * x
    for i in range(K - 1):
        y = y + ws[i] * _shift_down(x, K - 1 - i, t_idx)
    return y


def _conv_bwd(x, ws, dy, t_idx):
    K = len(ws)
    dx = ws[K - 1] * dy
    dws = []
    for i in range(K - 1):
        dx = dx + ws[i] * _shift_up(dy, K - 1 - i, t_idx)
        dws.append(jnp.sum(dy * _shift_down(x, K - 1 - i, t_idx), axis=0, keepdims=True))
    dws.append(jnp.sum(dy * x, axis=0, keepdims=True))
    return dx, dws


def _store_rows(ref, rows):
    for i, r in enumerate(rows):
        ref[i:i + 1, :] = r


CONV_ROWS = 64


def _ffn_act_fwd(up, conv_w, name):
    T = up.shape[0]
    F = FFN_DIM
    nb = F // LANE

    def body(g_ref, v_ref, wg_ref, wv_ref, o_ref):
        t_idx = lax.broadcasted_iota(jnp.int32, (T, LANE), 0)
        cg = _conv(g_ref[...].astype(F32), _rows(wg_ref, FFN_CONV), t_idx)
        cv = _conv(v_ref[...].astype(F32), _rows(wv_ref, FFN_CONV), t_idx)
        o_ref[...] = (_silu(cg) * cv).astype(BF16)

    return pl.pallas_call(
        body, name=name, grid=(nb,),
        in_specs=[pl.BlockSpec((T, LANE), lambda j: (0, j)), pl.BlockSpec((T, LANE), lambda j: (0, j + nb)),
                  pl.BlockSpec((FFN_CONV, LANE), lambda j: (0, j)),
                  pl.BlockSpec((FFN_CONV, LANE), lambda j: (0, j + nb))],
        out_specs=pl.BlockSpec((T, LANE), lambda j: (0, j)),
        out_shape=jax.ShapeDtypeStruct((T, F), BF16),
        compiler_params=_cp(("parallel",)),
    )(up, up, conv_w, conv_w)


def _ffn_act_bwd(up, conv_w, dact, name):
    T = up.shape[0]
    F = FFN_DIM
    nb = F // LANE

    K = FFN_CONV
    R = CONV_ROWS
    assert T % R == 0, T
    n_chunks = T // R
    PAD = 8

    def body(g_ref, v_ref, wg_ref, wv_ref, da_ref, dup_ref, dwg_ref, dwv_ref, xg_s, xv_s, dyg_s, dyv_s):
        zeros = jnp.zeros((PAD, LANE), F32)
        for s in (xg_s, xv_s, dyg_s, dyv_s):
            s[0:PAD, :] = zeros
            s[T + PAD:T + 2 * PAD, :] = zeros
        xg_s[PAD:T + PAD, :] = g_ref[...].astype(F32)
        xv_s[PAD:T + PAD, :] = v_ref[...].astype(F32)
        wg, wv = _rows(wg_ref, K), _rows(wv_ref, K)

        def window(ext, shift):
            if shift == 0:
                return ext[PAD:PAD + R, :]
            return pltpu.roll(ext, shift % (R + 2 * PAD), 0)[PAD:PAD + R, :]

        def forward(c, carry):
            r0 = pl.multiple_of(c * R, R)
            ge, ve = xg_s[pl.ds(r0, R + 2 * PAD), :], xv_s[pl.ds(r0, R + 2 * PAD), :]
            gw = [window(ge, K - 1 - i) for i in range(K)]
            vw = [window(ve, K - 1 - i) for i in range(K)]
            cg = sum(w * x for w, x in zip(wg, gw))
            cv = sum(w * x for w, x in zip(wv, vw))
            da = da_ref[pl.ds(r0, R), :].astype(F32)
            sg = _sigmoid(cg)
            dyg = da * cv * (sg * (1.0 + cg * (1.0 - sg)))
            dyv = da * (cg * sg)
            dyg_s[pl.ds(pl.multiple_of(r0 + PAD, PAD), R), :] = dyg
            dyv_s[pl.ds(pl.multiple_of(r0 + PAD, PAD), R), :] = dyv
            return tuple(acc + jnp.sum(dy * x, axis=0, keepdims=True)
                         for acc, (dy, x) in zip(carry, [(dyg, x) for x in gw] + [(dyv, x) for x in vw]))

        sums = lax.fori_loop(0, n_chunks, forward, (jnp.zeros((1, LANE), F32),) * (2 * K))
        _store_rows(dwg_ref, sums[:K])
        _store_rows(dwv_ref, sums[K:])

        def backward(c, carry):
            r0 = pl.multiple_of(c * R, R)
            ge, ve = dyg_s[pl.ds(r0, R + 2 * PAD), :], dyv_s[pl.ds(r0, R + 2 * PAD), :]
            dxg = sum(w * window(ge, -(K - 1 - i)) for i, w in enumerate(wg))
            dxv = sum(w * window(ve, -(K - 1 - i)) for i, w in enumerate(wv))
            dup_ref[0, pl.ds(r0, R), :] = dxg.astype(BF16)
            dup_ref[1, pl.ds(r0, R), :] = dxv.astype(BF16)
            return carry

        lax.fori_loop(0, n_chunks, backward, 0)

    col = pl.BlockSpec((T, LANE), lambda j: (0, j))
    wcol = pl.BlockSpec((FFN_CONV, LANE), lambda j: (0, j))
    return pl.pallas_call(
        body, name=name, grid=(nb,),
        in_specs=[col, pl.BlockSpec((T, LANE), lambda j: (0, j + nb)), wcol,
                  pl.BlockSpec((FFN_CONV, LANE), lambda j: (0, j + nb)), col],
        out_specs=[pl.BlockSpec((2, T, LANE), lambda j: (0, 0, j)), wcol, wcol],
        out_shape=[jax.ShapeDtypeStruct((2, T, F), BF16),
                   jax.ShapeDtypeStruct((FFN_CONV, F), F32), jax.ShapeDtypeStruct((FFN_CONV, F), F32)],
        scratch_shapes=[pltpu.VMEM((T + 2 * PAD, LANE), F32)] * 4,
        compiler_params=_cp(("parallel",)),
    )(up, up, conv_w, conv_w, dact)


N_QK_BLOCKS = 2 * GDN_HEADS


def _gdn_pre_fwd(proj, conv_w):
    T = proj.shape[0]
    nb = 3 * GDN_HEADS

    def body(x_ref, w_ref, o_ref):
        j = pl.program_id(0)
        t_idx = lax.broadcasted_iota(jnp.int32, (T, LANE), 0)
        s = _silu(_conv(x_ref[...], _rows(w_ref, GDN_CONV), t_idx))
        rn = lax.rsqrt(jnp.sum(s * s, axis=-1, keepdims=True) + EPS)
        o_ref[...] = s * jnp.where(j < N_QK_BLOCKS, rn, 1.0)

    return pl.pallas_call(
        body, name="gdn_pre_fwd", grid=(nb,),
        in_specs=[pl.BlockSpec((T, LANE), lambda j: (0, j)), pl.BlockSpec((GDN_CONV, LANE), lambda j: (0, j))],
        out_specs=pl.BlockSpec((T, LANE), lambda j: (0, j)),
        out_shape=jax.ShapeDtypeStruct((T, nb * LANE), F32),
        compiler_params=_cp(("parallel",)),
    )(proj, conv_w)


def _gdn_pre_bwd(proj, conv_w, dout):
    T = proj.shape[0]
    nb = 3 * GDN_HEADS
    H = GDN_HEADS

    def body(x_ref, w_ref, do_ref, dx_ref, dw_ref):
        j = pl.program_id(0)
        t_idx = lax.broadcasted_iota(jnp.int32, (T, LANE), 0)
        x, w = x_ref[...], _rows(w_ref, GDN_CONV)
        c = _conv(x, w, t_idx)
        s = _silu(c)
        rn = lax.rsqrt(jnp.sum(s * s, axis=-1, keepdims=True) + EPS)
        do = do_ref[...]
        y = s * rn
        ds_normed = rn * (do - y * jnp.sum(do * y, axis=-1, keepdims=True))
        ds = jnp.where(j < N_QK_BLOCKS, ds_normed, do)
        dx, dw = _conv_bwd(x, w, ds * _dsilu(c), t_idx)
        dx_ref[...] = dx.astype(BF16)
        _store_rows(dw_ref, dw)

    col = pl.BlockSpec((T, LANE), lambda j: (0, j))
    wcol = pl.BlockSpec((GDN_CONV, LANE), lambda j: (0, j))
    return pl.pallas_call(
        body, name="gdn_pre_bwd", grid=(nb,),
        in_specs=[col, wcol, pl.BlockSpec((None, None, T, LANE), lambda j: (j // H, j % H, 0, 0))],
        out_specs=[col, wcol],
        out_shape=[jax.ShapeDtypeStruct((T, nb * LANE), BF16), jax.ShapeDtypeStruct((GDN_CONV, nb * LANE), F32)],
        compiler_params=_cp(("parallel",)),
    )(proj, conv_w, dout)


def _gdn_consts():
    C = GDN_CHUNK
    r = lax.broadcasted_iota(jnp.int32, (C, C), 0)
    c = lax.broadcasted_iota(jnp.int32, (C, C), 1)
    return dict(incl=r >= c, strict=r > c, eye=(r == c).astype(F32),
                low=(r >= c).astype(BF16), up=(r <= c).astype(BF16), ones=jnp.ones((C, C), BF16))


def _unit_lower_inverse_raw(a_mat, eye):
    inv = eye - a_mat
    pw = _dot3_raw(a_mat, a_mat, "nn")
    n_factors = a_mat.shape[-1].bit_length() - 2
    for f in range(n_factors):
        inv = inv + _dot3_raw(inv, pw, "nn")
        if f < n_factors - 1:
            pw = _dot3_raw(pw, pw, "nn")
    return inv


@jax.custom_vjp
def _unit_lower_inverse(a_mat, eye):
    return _unit_lower_inverse_raw(a_mat, eye)


def _unit_lower_inverse_fwd(a_mat, eye):
    inv = _unit_lower_inverse_raw(a_mat, eye)
    return inv, (inv, eye)


def _unit_lower_inverse_bwd(res, g):
    inv, eye = res
    return -_dot(_dot(inv, g, "tn"), inv, "nt"), jnp.zeros_like(eye)


_unit_lower_inverse.defvjp(_unit_lower_inverse_fwd, _unit_lower_inverse_bwd)


def _gdn_prep_chunk(q, k, v, b, a, alog, dtb, cs):
    n, C, dk = q.shape
    beta = _sigmoid(b)
    g = -jnp.exp(alog) * _softplus(a + dtb)
    g_sq = jnp.broadcast_to(g, (n, C, C))
    g_wide = jnp.broadcast_to(g, (n, C, dk))
    gc_i = _m01_left(cs["low"], cs["up"], g_sq)
    gc_j = _m01_left(cs["ones"], cs["ones"], g_sq * cs["up"].astype(F32))
    gc_wide = _m01_left(cs["low"], cs["up"], g_wide)
    gl_wide = _m01_left(cs["ones"], cs["ones"], g_wide)
    decay = jnp.where(cs["incl"], jnp.exp(jnp.where(cs["incl"], gc_i - gc_j, 0.0)), 0.0)
    egc = jnp.exp(gc_wide)
    qs = q * (dk ** -0.5)
    k_beta = k * beta
    a_mat = jnp.where(cs["strict"], _dot1_nt(k_beta, k) * decay, 0.0)
    inv = _unit_lower_inverse(a_mat, cs["eye"])
    u = _dot3(inv, v * beta)
    w = _dot3(inv, k_beta * egc)
    qk = _dot1_nt(qs, k) * decay
    q_dec = qs * egc
    k_dec = k * jnp.exp(gl_wide - gc_wide)
    g_last = jnp.exp(gl_wide)[:, 0:8, :]
    return qk, u, w, q_dec, k_dec, g_last


GDN_PREP_CHUNKS = 16
GDN_BA_BLOCK = 4 * D_MODEL // LANE


def _gdn_prep_specs(T):
    C, dk = GDN_CHUNK, GDN_HEAD_DIM
    npc = min(GDN_PREP_CHUNKS, T // C)
    tc = npc * C
    assert T % tc == 0, (T, tc)
    H = GDN_HEADS
    in_specs = [pl.BlockSpec((tc, dk), lambda i, h: (i, h)),
                pl.BlockSpec((tc, dk), lambda i, h: (i, H + h)),
                pl.BlockSpec((tc, dk), lambda i, h: (i, 2 * H + h)),
                pl.BlockSpec((tc, dk), lambda i, h: (i, GDN_BA_BLOCK)),
                pl.BlockSpec((8, dk), lambda i, h: (0, 0))]
    xs_specs = [pl.BlockSpec((1, tc, C), lambda i, h: (h, i, 0)),
                pl.BlockSpec((1, tc, dk), lambda i, h: (h, i, 0)),
                pl.BlockSpec((1, tc, dk), lambda i, h: (h, i, 0)),
                pl.BlockSpec((1, tc, dk), lambda i, h: (h, i, 0)),
                pl.BlockSpec((1, tc, dk), lambda i, h: (h, i, 0)),
                pl.BlockSpec((1, npc * 8, dk), lambda i, h: (h, i, 0))]
    xs_shapes = [jax.ShapeDtypeStruct((H, T, C), F32)] + [jax.ShapeDtypeStruct((H, T, dk), F32)] * 4 + [
        jax.ShapeDtypeStruct((H, 8 * T // C, dk), F32)]
    return npc, tc, in_specs, xs_specs, xs_shapes


def _lane_pick(x, lane, j):
    return jnp.sum(jnp.where(lane == j, x, 0.0), axis=1, keepdims=True)


def _gdn_head_gates(ba_ref, gates_ref, h, npc):
    lane = lax.broadcasted_iota(jnp.int32, (1, GDN_HEAD_DIM), 1)
    ba = ba_ref[...]
    b = _lane_pick(ba, lane, h).reshape(npc, GDN_CHUNK, 1)
    a = _lane_pick(ba, lane, GDN_HEADS + h).reshape(npc, GDN_CHUNK, 1)
    return b, a, _lane_pick(gates_ref[0:1, :], lane, h), _lane_pick(gates_ref[1:2, :], lane, h), lane


def _gdn_prep_fwd(qkv, proj, gates):
    T = qkv.shape[0]
    C = GDN_CHUNK
    npc, tc, in_specs, xs_specs, xs_shapes = _gdn_prep_specs(T)

    def body(q_ref, k_ref, v_ref, ba_ref, gates_ref, qk_ref, u_ref, w_ref, qd_ref, kd_ref, gl_ref):
        cs = _gdn_consts()
        b, a, alog, dtb, _ = _gdn_head_gates(ba_ref, gates_ref, pl.program_id(1), npc)

        def chunks(val):
            return val.reshape(npc, C, val.shape[-1])

        outs = _gdn_prep_chunk(chunks(q_ref[...]), chunks(k_ref[...]), chunks(v_ref[...]), b, a, alog, dtb, cs)
        for ref, val in zip((qk_ref, u_ref, w_ref, qd_ref, kd_ref), outs[:5]):
            ref[0] = val.reshape(tc, val.shape[-1])
        gl_ref[0] = outs[5].reshape(npc * 8, outs[5].shape[-1])

    return pl.pallas_call(
        body, name="gdn_prep_fwd", grid=(T // tc, GDN_HEADS),
        in_specs=in_specs, out_specs=xs_specs, out_shape=xs_shapes,
        compiler_params=_cp(("parallel", "parallel")),
    )(qkv, qkv, qkv, proj, gates)


def _gdn_prep_bwd(qkv, proj, gates, dxs):
    T = qkv.shape[0]
    C, dk, H = GDN_CHUNK, GDN_HEAD_DIM, GDN_HEADS
    npc, tc, in_specs, xs_specs, _ = _gdn_prep_specs(T)

    def body(q_ref, k_ref, v_ref, ba_ref, gates_ref, dqk_ref, du_ref, dw_ref, dqd_ref, dkd_ref, dgl_ref,
             dqkv_ref, dba_ref, dgates_ref):
        i, h = pl.program_id(0), pl.program_id(1)
        cs = _gdn_consts()
        r8 = lax.broadcasted_iota(jnp.int32, (8, dk), 0)
        c8 = lax.broadcasted_iota(jnp.int32, (8, dk), 1)
        first = (r8 == 0) & (c8 == 0)

        @pl.when((i == 0) & (h == 0))
        def _():
            dgates_ref[...] = jnp.zeros_like(dgates_ref)

        @pl.when(h == 0)
        def _():
            dba_ref[...] = jnp.zeros_like(dba_ref)

        def chunks(val):
            return val.reshape(npc, C, val.shape[-1])

        b, a, alog, dtb, lane = _gdn_head_gates(ba_ref, gates_ref, h, npc)
        prim = (chunks(q_ref[...]), chunks(k_ref[...]), chunks(v_ref[...]), b, a, alog, dtb)
        _, vjp = jax.vjp(lambda *p: _gdn_prep_chunk(*p, cs), *prim)
        dgl = jnp.where(first, dgl_ref[0].reshape(npc, 8, dk), 0.0)
        cts = tuple(chunks(r[0]) for r in (dqk_ref, du_ref, dw_ref, dqd_ref, dkd_ref)) + (dgl,)
        dq, dkk, dv, db, da, dal, ddt = vjp(cts)
        for part, val in enumerate((dq, dkk, dv)):
            dqkv_ref[part, 0] = val.reshape(tc, dk)
        dba_ref[...] += (jnp.where(lane == h, db.reshape(tc, 1), 0.0)
                         + jnp.where(lane == H + h, da.reshape(tc, 1), 0.0))
        dgates_ref[0:1, :] += jnp.where(lane == h, dal, 0.0)
        dgates_ref[1:2, :] += jnp.where(lane == h, ddt, 0.0)

    return pl.pallas_call(
        body, name="gdn_prep_bwd", grid=(T // tc, H),
        in_specs=in_specs + xs_specs,
        out_specs=[pl.BlockSpec((3, 1, tc, dk), lambda i, h: (0, h, i, 0)), pl.BlockSpec((tc, dk), lambda i, h: (i, 0)),
                   pl.BlockSpec((8, dk), lambda i, h: (0, 0))],
        out_shape=[jax.ShapeDtypeStruct((3, H, T, dk), F32), jax.ShapeDtypeStruct((T, dk), F32),
                   jax.ShapeDtypeStruct((8, dk), F32)],
        compiler_params=_cp(("arbitrary", "arbitrary")),
    )(qkv, qkv, qkv, proj, gates, *dxs)


def _gdn_scan_specs(T):
    C, dk, H = GDN_CHUNK, GDN_HEAD_DIM, GDN_HEADS
    return [pl.BlockSpec((H, C, C), lambda n: (0, n, 0))] + [pl.BlockSpec((H, C, dk), lambda n: (0, n, 0))] * 4 + [
        pl.BlockSpec((H, 8, dk), lambda n: (0, n, 0))]


def _gdn_scan_fwd(xs):
    H, T, dk = xs[1].shape
    C = GDN_CHUNK
    n = T // C

    def body(qk_ref, u_ref, w_ref, qd_ref, kd_ref, gl_ref, o_ref, s_ref, state):
        c = pl.program_id(0)

        @pl.when(c == 0)
        def _():
            state[...] = jnp.zeros_like(state)

        S = state[...]
        s_ref[0] = S
        v_new = u_ref[...] - _bdot(w_ref[...], S)
        o_ref[...] = _bdot(qd_ref[...], S) + _bdot(qk_ref[...], v_new)
        state[...] = S * jnp.tile(gl_ref[...], (1, dk // 8, 1)) + _bdot(kd_ref[...], v_new, "tn")

    return pl.pallas_call(
        body, name="gdn_scan_fwd", grid=(n,),
        in_specs=_gdn_scan_specs(T),
        out_specs=[pl.BlockSpec((H, C, dk), lambda n: (0, n, 0)), pl.BlockSpec((1, H, dk, dk), lambda n: (n, 0, 0, 0))],
        out_shape=[jax.ShapeDtypeStruct((H, T, dk), F32), jax.ShapeDtypeStruct((n, H, dk, dk), F32)],
        scratch_shapes=[pltpu.VMEM((H, dk, dk), F32)],
        compiler_params=_cp(("arbitrary",)),
    )(*xs)


def _gdn_scan_bwd(xs, states, do):
    H, T, dk = xs[1].shape
    C = GDN_CHUNK
    n = T // C

    def rev(spec_shape, f):
        return pl.BlockSpec(spec_shape, lambda i: f(n - 1 - i))

    def body(qk_ref, u_ref, w_ref, qd_ref, kd_ref, gl_ref, s_ref, do_ref,
             dqk_ref, du_ref, dw_ref, dqd_ref, dkd_ref, dgl_ref, dstate):
        i = pl.program_id(0)

        @pl.when(i == 0)
        def _():
            dstate[...] = jnp.zeros_like(dstate)

        S = s_ref[0]
        dS = dstate[...]
        do_v = do_ref[...]
        qk, w, qd, kd = qk_ref[...], w_ref[...], qd_ref[...], kd_ref[...]
        v_new = u_ref[...] - _bdot(w, S)
        dv_new = _bdot(qk, do_v, "tn") + _bdot(kd, dS)
        dqk_ref[...] = _bdot(do_v, v_new, "nt")
        dqd_ref[...] = _bdot(do_v, S, "nt")
        dkd_ref[...] = _bdot(v_new, dS, "nt")
        du_ref[...] = dv_new
        dw_ref[...] = -_bdot(dv_new, S, "nt")
        dgl = jnp.sum(jnp.sum(S * dS, axis=2, keepdims=True), axis=1, keepdims=True)
        dgl_ref[...] = jnp.broadcast_to(dgl, dgl_ref.shape)
        dstate[...] = (dS * jnp.tile(gl_ref[...], (1, dk // 8, 1)) + _bdot(qd, do_v, "tn")
                       - _bdot(w, dv_new, "tn"))

    in_specs = [rev((H, C, C), lambda m: (0, m, 0))] + [rev((H, C, dk), lambda m: (0, m, 0))] * 4 + [
        rev((H, 8, dk), lambda m: (0, m, 0)), rev((1, H, dk, dk), lambda m: (m, 0, 0, 0)),
        rev((H, C, dk), lambda m: (0, m, 0))]
    out_specs = [rev((H, C, C), lambda m: (0, m, 0))] + [rev((H, C, dk), lambda m: (0, m, 0))] * 4 + [
        rev((H, 8, dk), lambda m: (0, m, 0))]
    out_shape = [jax.ShapeDtypeStruct((H, T, C), F32)] + [jax.ShapeDtypeStruct((H, T, dk), F32)] * 4 + [
        jax.ShapeDtypeStruct((H, 8 * n, dk), F32)]
    return pl.pallas_call(
        body, name="gdn_scan_bwd", grid=(n,),
        in_specs=in_specs, out_specs=out_specs, out_shape=out_shape,
        scratch_shapes=[pltpu.VMEM((H, dk, dk), F32)],
        compiler_params=_cp(("arbitrary",)),
    )(*xs, states, do)


def _gdn_post_fwd(o, proj, norm_w):
    H, T, dk = o.shape
    tt = _tile(T, 1024)
    zoff = 3 * GDN_HEADS

    def body(o_ref, z_ref, g_ref, y_ref):
        ov = o_ref[0]
        r = lax.rsqrt(jnp.mean(ov * ov, axis=-1, keepdims=True) + EPS)
        y_ref[...] = (ov * r * g_ref[...] * _silu(z_ref[...])).astype(BF16)

    return pl.pallas_call(
        body, name="gdn_post_fwd", grid=(H, T // tt),
        in_specs=[pl.BlockSpec((1, tt, dk), lambda h, i: (h, i, 0)), pl.BlockSpec((tt, dk), lambda h, i: (i, zoff + h)),
                  pl.BlockSpec((1, dk), lambda h, i: (0, 0))],
        out_specs=pl.BlockSpec((tt, dk), lambda h, i: (i, h)),
        out_shape=jax.ShapeDtypeStruct((T, H * dk), BF16),
        compiler_params=_cp(("parallel", "parallel")),
    )(o, proj, norm_w)


def _gdn_post_bwd(o, proj, norm_w, dy):
    H, T, dk = o.shape
    tt = _tile(T, 1024)
    zoff = 3 * GDN_HEADS

    def body(o_ref, z_ref, g_ref, dy_ref, do_ref, dz_ref, dg_ref):
        i = pl.program_id(1)
        ov, z, g, dyv = o_ref[0], z_ref[...], g_ref[...], dy_ref[...]
        r = lax.rsqrt(jnp.mean(ov * ov, axis=-1, keepdims=True) + EPS)
        on = ov * r
        sz = _silu(z)
        dz_ref[...] = (dyv * on * g * _dsilu(z)).astype(BF16)
        dn = dyv * sz
        gdn = dn * g
        do_ref[0] = r * (gdn - on * jnp.mean(gdn * on, axis=-1, keepdims=True))

        @pl.when(i == 0)
        def _():
            dg_ref[...] = jnp.zeros_like(dg_ref)

        dg_ref[0] += jnp.sum(dn * on, axis=0, keepdims=True)

    return pl.pallas_call(
        body, name="gdn_post_bwd", grid=(H, T // tt),
        in_specs=[pl.BlockSpec((1, tt, dk), lambda h, i: (h, i, 0)), pl.BlockSpec((tt, dk), lambda h, i: (i, zoff + h)),
                  pl.BlockSpec((1, dk), lambda h, i: (0, 0)), pl.BlockSpec((tt, dk), lambda h, i: (i, h))],
        out_specs=[pl.BlockSpec((1, tt, dk), lambda h, i: (h, i, 0)), pl.BlockSpec((tt, dk), lambda h, i: (i, h)),
                   pl.BlockSpec((1, 1, dk), lambda h, i: (h, 0, 0))],
        out_shape=[jax.ShapeDtypeStruct((H, T, dk), F32), jax.ShapeDtypeStruct((T, H * dk), BF16),
                   jax.ShapeDtypeStruct((H, 1, dk), F32)],
        compiler_params=_cp(("parallel", "arbitrary")),
    )(o, proj, norm_w, dy)


def _ple_fwd(x, pp, gl, name):
    T, D = x.shape
    tt = _tile(T, 512)

    def body(x_ref, p_ref, g_ref, o_ref):
        o_ref[...] = x_ref[...] + p_ref[...] * _sigmoid(g_ref[...])

    row = pl.BlockSpec((tt, D), lambda i: (i, 0))
    return pl.pallas_call(
        body, name=name, grid=(T // tt,), in_specs=[row, row, row], out_specs=row,
        out_shape=jax.ShapeDtypeStruct((T, D), F32), compiler_params=_cp(("parallel",)),
    )(x, pp, gl)


def _ple_bwd(dx, pp, gl, name, after=None):
    T, D = dx.shape
    tt = _tile(T, 512)

    def body(dx_ref, p_ref, g_ref, *rest):
        dp_ref, dg_ref = rest[-2:]
        s = _sigmoid(g_ref[...])
        dxv = dx_ref[...]
        dp_ref[...] = (dxv * s).astype(BF16)
        dg_ref[...] = (dxv * p_ref[...] * s * (1.0 - s)).astype(BF16)

    row = pl.BlockSpec((tt, D), lambda i: (i, 0))
    ordered = [] if after is None else [after]
    return pl.pallas_call(
        body, name=name, grid=(T // tt,),
        in_specs=[row, row, row] + [pl.BlockSpec((8, LANE), lambda i: (0, 0)) for _ in ordered], out_specs=[row, row],
        out_shape=[jax.ShapeDtypeStruct((T, D), BF16)] * 2, compiler_params=_cp(("parallel",)),
    )(dx, pp, gl, *ordered)


UP_SHARD = 2 * FFN_DIM // N_CHIPS
DOWN_SHARD = FFN_DIM // N_CHIPS
GATE_SHARD = D_MODEL // N_CHIPS
IN_E_SHARD = 2 * D_MODEL // N_CHIPS


def _ffn_fwd(x, norm, W, conv_w, l):
    T = x.shape[0]
    hf = _rms_fwd(x, norm, f"ffn_norm{l}")
    up = _mm(hf, W["w_up"], "nn", f"ffn_up{l}", dims=(T, 2 * FFN_DIM, D_MODEL), b_view=_cols_of(1, l), tn=UP_SHARD,
             out_dtype=BF16)
    act = _ffn_act_fwd(up, conv_w, f"ffn_act{l}")
    x_out = _mm(act, W["w_down"], "nn", f"ffn_down{l}", dims=(T, D_MODEL, FFN_DIM), b_view=_layer_of(l), res=x,
                tn=1024, tk=1408)
    return x_out, (x, hf, up, act)


def _ffn_bwd(dx_out, saved, norm, W, conv_w, l):
    x, hf, up, act = saved
    T = x.shape[0]
    dact = _mm(dx_out, W["w_down"], "nt", f"ffn_dact{l}", dims=(T, FFN_DIM, D_MODEL), b_view=_layer_of(l),
               out_dtype=BF16, tn=1408)
    dw_down = _mm(act, dx_out, "tn", f"ffn_dwdown{l}", tm=1408, tn=1024, tk=1024)
    dup, dcw_g, dcw_v = _ffn_act_bwd(up, conv_w, dact, f"ffn_dact_conv{l}")
    dw_up = _mm(hf, dup, "tn", f"ffn_dwup{l}", dims=(D_MODEL, 2 * FFN_DIM, T), b_view=_cols_of(FFN_DIM // UP_SHARD),
                o_view=_cols_of(1), out_shape=(N_CHIPS, D_MODEL, UP_SHARD), tm=1024, tn=UP_SHARD, tk=1024)
    dhf = _mm(dup, W["w_up"], "nt", f"ffn_dhf{l}", dims=(T, D_MODEL, 2 * FFN_DIM),
              a_view=_cols_of(FFN_DIM // UP_SHARD), b_view=_cols_of(1, l), tn=1024, tk=UP_SHARD)
    dx, dnorm = _rms_bwd(x, norm, dhf, dx_out, f"ffn_dnorm{l}")
    return (dx, dnorm, dw_up, jnp.concatenate([dcw_g, dcw_v], axis=1),
            dw_down.reshape(N_CHIPS, DOWN_SHARD, D_MODEL))


def _ple_layer_fwd(x, p, norm, W, l):
    T = x.shape[0]
    hg = _rms_fwd(x, norm, f"ple_norm{l}")
    gl = _mm(hg, W["w_ple_gate"], "nn", f"ple_gate{l}", dims=(T, D_MODEL, D_MODEL), b_view=_layer_of(l), tn=1024)
    pp = _mm(p, W["w_ple"], "nn", f"ple_proj{l}", dims=(T, D_MODEL, PLE_DIM), a_view=_layer_of(l),
             b_view=_layer_of(l), tn=1024)
    return _ple_fwd(x, pp, gl, f"ple_mix{l}"), (x, hg, gl, pp)


def _ple_layer_bwd(dx_out, saved, p, norm, W, l, after=None):
    x, hg, gl, pp = saved
    T = x.shape[0]
    dpp, dgl = _ple_bwd(dx_out, pp, gl, f"ple_dmix{l}", after)
    dw_ple = _mm(p, dpp, "tn", f"ple_dwple{l}", dims=(PLE_DIM, D_MODEL, T), a_view=_layer_of(l),
                 tm=PLE_DIM, tn=1024, tk=1024)
    dw_ple = dw_ple.reshape(PLE_DIM, N_CHIPS, PLE_DIM).transpose(1, 0, 2)
    dw_gate = _mm(hg, dgl, "tn", f"ple_dwgate{l}", tm=1024, tn=1024, tk=1024)
    dhg = _mm(dgl, W["w_ple_gate"], "nt", f"ple_dhg{l}", dims=(T, D_MODEL, D_MODEL), b_view=_layer_of(l), tn=1024)
    dx, dnorm = _rms_bwd(x, norm, dhg, dx_out, f"ple_dnorm{l}")
    return dx, dnorm, dw_gate.reshape(N_CHIPS, GATE_SHARD, D_MODEL), dw_ple


def _local_step(x, p, target, W, token=None, later_weights=None, on_grads=None):
    T = x.shape[0]
    H = GDN_HEADS
    G = {}

    hn_e = _rms_fwd(x, W["mix_norm_e"], "mix_norm_e", after=token)
    proj_e = _mm(hn_e, W["w_in_e"], "nn", "in_e", dims=(T, 2 * D_MODEL, D_MODEL), b_view=_cols_of(1), tn=IN_E_SHARD)
    pool_out = _pool_fwd(proj_e, W["pool_w"], W["pool_scale"])
    attn, ltot = _sb_fwd(proj_e)
    if later_weights is not None:
        W = {**W, **later_weights(attn)}
    mix_e = jnp.concatenate([pool_out, attn], axis=1).astype(BF16)
    x1 = _mm(mix_e, W["w_out_e"], "nn", "out_e", res=x, tn=1024)
    x2, ffn0 = _ffn_fwd(x1, W["ffn_norm"][0:1], W, W["ffn_conv"][0], 0)
    x3, ple0 = _ple_layer_fwd(x2, p, W["ple_norm"][0:1], W, 0)

    hn_o = _rms_fwd(x3, W["mix_norm_o"], "mix_norm_o")
    proj_o = _mm(hn_o, W["w_in_o"], "nn", "in_o", tn=1408)
    qkv = _gdn_pre_fwd(proj_o, W["conv_qkv_o"])
    gates = jnp.pad(jnp.concatenate([W["a_log_o"], W["dt_bias_o"]], axis=0), ((0, 6), (0, LANE - H)))
    xs = _gdn_prep_fwd(qkv, proj_o, gates)
    o, states = _gdn_scan_fwd(xs)
    og = _gdn_post_fwd(o, proj_o, W["gdn_norm_o"])
    x4 = _mm(og, W["w_out_o"], "nn", "out_o", res=x3, tn=1024)
    x5, ffn1 = _ffn_fwd(x4, W["ffn_norm"][1:2], W, W["ffn_conv"][1], 1)
    x6, ple1 = _ple_layer_fwd(x5, p, W["ple_norm"][1:2], W, 1)

    sq, dx6, G["final_norm"] = _final_loss(x6, W["final_norm"], target, "final_loss")

    dx5, dpn1, dwg1, dwp1 = _ple_layer_bwd(dx6, ple1, p, W["ple_norm"][1:2], W, 1)
    dx4, dfn1, dwu1, dfc1, dwd1 = _ffn_bwd(dx5, ffn1, W["ffn_norm"][1:2], W, W["ffn_conv"][1], 1)
    dog = _mm(dx4, W["w_out_o"], "nt", "d_og", tn=1024)
    G["w_out_o"] = _mm(og, dx4, "tn", "dw_out_o", tm=1024, tn=1024, tk=1024).reshape(N_CHIPS, GATE_SHARD, D_MODEL)
    do, dz, dgn = _gdn_post_bwd(o, proj_o, W["gdn_norm_o"], dog)
    G["gdn_norm_o"] = jnp.sum(dgn, axis=0)
    dxs = _gdn_scan_bwd(xs, states, do)
    dqkv_act, dba, dgates = _gdn_prep_bwd(qkv, proj_o, gates, dxs)
    G["a_log_o"] = dgates[0:1, :H]
    G["dt_bias_o"] = dgates[1:2, :H]
    dqkv, G["conv_qkv_o"] = _gdn_pre_bwd(proj_o, W["conv_qkv_o"], dqkv_act)
    dproj_o = jnp.concatenate([dqkv, dz, dba.astype(BF16)], axis=1)
    G["w_in_o"] = _mm(hn_o, dproj_o, "tn", "dw_in_o", tm=1024, tn=1408, tk=1024)
    dhn_o = _mm(dproj_o, W["w_in_o"], "nt", "d_hn_o", tn=1024, tk=1408)
    dx3, G["mix_norm_o"] = _rms_bwd(x3, W["mix_norm_o"], dhn_o, dx4, "d_mix_norm_o")

    token1 = token0 = None
    if on_grads is not None:
        token1 = on_grads("second", {("w_in_o", 0): _chip_major_w_in_o(G["w_in_o"]), ("w_out_o", 0): G["w_out_o"],
                                     ("w_up", 1): dwu1, ("w_down", 1): dwd1, ("w_ple_gate", 1): dwg1,
                                     ("w_ple", 1): dwp1})
    dx2, dpn0, dwg0, dwp0 = _ple_layer_bwd(dx3, ple0, p, W["ple_norm"][0:1], W, 0, after=token1)
    dx1, dfn0, dwu0, dfc0, dwd0 = _ffn_bwd(dx2, ffn0, W["ffn_norm"][0:1], W, W["ffn_conv"][0], 0)
    if on_grads is not None:
        token0 = on_grads("first_ffn", {("w_up", 0): dwu0, ("w_down", 0): dwd0, ("w_ple_gate", 0): dwg0,
                                        ("w_ple", 0): dwp0})
    dmix = _mm(dx1, W["w_out_e"], "nt", "d_mix_e", tn=1024)
    G["w_out_e"] = _mm(mix_e, dx1, "tn", "dw_out_e", tm=1024, tn=1024, tk=1024).reshape(N_CHIPS, GATE_SHARD, D_MODEL)
    du, G["pool_w"], G["pool_scale"] = _pool_bwd(proj_e, W["pool_w"], W["pool_scale"], dmix)
    dqa, dka, dva = _sb_bwd(proj_e, dmix, ltot, after=token0)
    dproj_e = jnp.concatenate([du, dqa, dka, dva], axis=1).astype(BF16)
    G["w_in_e"] = _mm(hn_e, dproj_e, "tn", "dw_in_e", dims=(D_MODEL, 2 * D_MODEL, T), o_view=_cols_of(1),
                      out_shape=(N_CHIPS, D_MODEL, IN_E_SHARD), tm=1024, tn=IN_E_SHARD, tk=1024)
    dhn_e = _mm(dproj_e, W["w_in_e"], "nt", "d_hn_e", dims=(T, D_MODEL, 2 * D_MODEL), b_view=_cols_of(1),
                tn=1024, tk=IN_E_SHARD)
    grad_x, G["mix_norm_e"] = _rms_bwd(x, W["mix_norm_e"], dhn_e, dx1, "d_mix_norm_e")

    G["ffn_norm"] = jnp.concatenate([dfn0, dfn1], axis=0)
    G["ple_norm"] = jnp.concatenate([dpn0, dpn1], axis=0)
    G["ffn_conv"] = jnp.stack([dfc0, dfc1])
    G["w_up"] = [dwu0, dwu1]
    G["w_down"] = [dwd0, dwd1]
    G["w_ple_gate"] = [dwg0, dwg1]
    G["w_ple"] = [dwp0, dwp1]
    return sq[0, 0], grad_x, G


BIG = ("w_in_e", "w_out_e", "w_in_o", "w_out_o", "w_up", "w_down", "w_ple_gate", "w_ple")
SHARDED_SMALL = (("mix_norm_o", 1), ("conv_qkv_o", 2), ("ffn_conv", 2))
REPLICATED = ("mix_norm_e", "pool_w", "pool_scale", "a_log_o", "dt_bias_o", "gdn_norm_o", "ffn_norm", "ple_norm",
              "final_norm")
WEIGHT_ORDER = ("mix_norm_e", "w_in_e", "pool_w", "pool_scale", "w_out_e", "mix_norm_o", "w_in_o", "conv_qkv_o",
                "a_log_o", "dt_bias_o", "gdn_norm_o", "w_out_o", "ffn_norm", "w_up", "ffn_conv", "w_down", "ple_norm",
                "w_ple_gate", "w_ple", "final_norm")
SMALL_W = LANE
SMALL_ROWS = 16


def _size(shape):
    n = 1
    for s in shape:
        n *= s
    return n


def _pack(arrs, width, granule):
    flat = jnp.concatenate([a.reshape(-1) for a in arrs])
    rows = -(-flat.shape[0] // width)
    rows = -(-rows // granule) * granule
    return jnp.pad(flat, (0, rows * width - flat.shape[0])).reshape(rows, width)


def _unpack(flat2d, shapes):
    flat = flat2d.reshape(-1)
    out, off = [], 0
    for s in shapes:
        out.append(flat[off:off + _size(s)].reshape(s))
        off += _size(s)
    return out


MESH_ID = pl.DeviceIdType.MESH
HBM_SPEC = pl.BlockSpec(memory_space=pltpu.HBM)


def _where_am_i():
    return lax.axis_index("x"), lax.axis_index("y"), lax.axis_index("c")


def _other_chips(x, y):
    return [(1 - x, y), (x, 1 - y), (1 - x, 1 - y)]


def _remote(src, dst, send_sems, recv_sems, k, to):
    return pltpu.make_async_remote_copy(src_ref=src, dst_ref=dst, send_sem=send_sems.at[k], recv_sem=recv_sems.at[k],
                                        device_id=to, device_id_type=MESH_ID)


def _chip_allgather(pack, name):
    R, Wd = pack.shape
    Rh = R // 2

    def body(src_ref, out_ref, send_sems, recv_sems, local_sem):
        x, y, c = _where_am_i()
        me, sib = (x, y, c), (x, y, 1 - c)
        chips = _other_chips(x, y)
        mine_rows = pl.ds(pl.multiple_of(c * Rh, SMALL_ROWS), Rh)
        sib_rows = pl.ds(pl.multiple_of((1 - c) * Rh, SMALL_ROWS), Rh)
        j_me = 2 * x + y
        local = pltpu.make_async_copy(src_ref, out_ref.at[j_me], local_sem)
        local.start()
        first = [_remote(src_ref.at[mine_rows], out_ref.at[j_me, mine_rows], send_sems, recv_sems, k, (cx, cy, c))
                 for k, (cx, cy) in enumerate(chips)]
        for cp in first:
            cp.start()
        passed = []
        for k, (cx, cy) in enumerate(chips):
            blk = out_ref.at[2 * cx + cy, mine_rows]
            _remote(blk, blk, send_sems, recv_sems, k, me).wait_recv()
            fw = _remote(blk, blk, send_sems, recv_sems, 3 + k, sib)
            fw.start()
            passed.append(fw)
        for k, (cx, cy) in enumerate(chips):
            blk = out_ref.at[2 * cx + cy, sib_rows]
            _remote(blk, blk, send_sems, recv_sems, 3 + k, me).wait_recv()
        for cp in first + passed:
            cp.wait_send()
        local.wait()

    return pl.pallas_call(
        body, name=name, in_specs=[HBM_SPEC], out_specs=HBM_SPEC,
        out_shape=jax.ShapeDtypeStruct((N_CHIPS, R, Wd), pack.dtype),
        scratch_shapes=[pltpu.SemaphoreType.DMA((6,)), pltpu.SemaphoreType.DMA((6,)), pltpu.SemaphoreType.DMA],
    )(pack)


def _chip_allgather_many(blocks, name):
    n = len(blocks)

    def body(*refs):
        srcs, outs = refs[:n], refs[n:2 * n]
        send_sems, recv_sems = refs[2 * n:]
        x, y, c = _where_am_i()
        me, sib = (x, y, c), (x, y, 1 - c)
        chips = _other_chips(x, y)
        j_me = 2 * x + y
        first = [_remote(srcs[p].at[c], outs[p].at[j_me, c], send_sems, recv_sems, 6 * p + k, (cx, cy, c))
                 for p in range(n) for k, (cx, cy) in enumerate(chips)]
        for cp in first:
            cp.start()
        passed = []
        for k, (cx, cy) in enumerate(chips):
            for p in range(n):
                blk = outs[p].at[2 * cx + cy, c]
                _remote(blk, blk, send_sems, recv_sems, 6 * p + k, me).wait_recv()
                fw = _remote(blk, blk, send_sems, recv_sems, 6 * p + 3 + k, sib)
                fw.start()
                passed.append(fw)
        for k, (cx, cy) in enumerate(chips):
            for p in range(n):
                blk = outs[p].at[2 * cx + cy, 1 - c]
                _remote(blk, blk, send_sems, recv_sems, 6 * p + 3 + k, me).wait_recv()
        for cp in first + passed:
            cp.wait_send()

    return pl.pallas_call(
        body, name=name, in_specs=[HBM_SPEC] * n, out_specs=[HBM_SPEC] * n,
        out_shape=[jax.ShapeDtypeStruct((N_CHIPS,) + b.shape, b.dtype) for b in blocks],
        scratch_shapes=[pltpu.SemaphoreType.DMA((6 * n,)), pltpu.SemaphoreType.DMA((6 * n,))],
    )(*blocks)


SEM_SPEC = pl.BlockSpec(memory_space=pltpu.SEMAPHORE)
DATAFLOW_EFFECT = pltpu.SideEffectType.DATAFLOW_SIDE_EFFECTING


def _chip_allgather_start(blocks, name):
    n = len(blocks)

    def body(*refs):
        srcs, lands = refs[:n], refs[n:2 * n]
        send_sems, recv_sems, token = refs[2 * n], refs[2 * n + 1], refs[-1]
        x, y, c = _where_am_i()
        j_me = 2 * x + y
        for p in range(n):
            for k, (cx, cy) in enumerate(_other_chips(x, y)):
                _remote(srcs[p].at[c], lands[p].at[j_me, c], send_sems, recv_sems, 3 * p + k, (cx, cy, c)).start()
        token[...] = jnp.zeros_like(token)

    lands = [pltpu.with_memory_space_constraint(lax.empty((N_CHIPS,) + b.shape, b.dtype), pltpu.HBM) for b in blocks]
    blocks = [pltpu.with_memory_space_constraint(b, pltpu.HBM) for b in blocks]
    outs = pl.pallas_call(
        body, name=name,
        in_specs=[HBM_SPEC] * (2 * n),
        out_specs=[SEM_SPEC, SEM_SPEC] + [HBM_SPEC] * (2 * n) + [pl.BlockSpec(memory_space=pltpu.VMEM)],
        out_shape=[pltpu.SemaphoreType.DMA((3 * n,)), pltpu.SemaphoreType.DMA((3 * n,))]
        + [pltpu.HBM(a.shape, a.dtype) for a in blocks + lands] + [jax.ShapeDtypeStruct((8, LANE), F32)],
        input_output_aliases={i: 2 + i for i in range(2 * n)},
        compiler_params=pltpu.CompilerParams(has_side_effects=DATAFLOW_EFFECT),
    )(*blocks, *lands)
    return outs[0], outs[1], list(outs[2:2 + n]), list(outs[2 + n:2 + 2 * n]), outs[-1]


def _chip_allgather_wait(send_sems, recv_sems, blocks, lands, after, name):
    n = len(blocks)

    def body(*refs):
        srcs, zones = refs[:n], refs[n:2 * n]
        send, recv = refs[2 * n], refs[2 * n + 1]
        x, y, c = _where_am_i()
        for p in range(n):
            for k, (cx, cy) in enumerate(_other_chips(x, y)):
                cp = _remote(srcs[p].at[c], zones[p].at[2 * cx + cy, c], send, recv, 3 * p + k, (x, y, c))
                cp.wait_send()
                cp.wait_recv()

    outs = pl.pallas_call(
        body, name=name,
        in_specs=[HBM_SPEC] * (2 * n) + [SEM_SPEC, SEM_SPEC, pl.BlockSpec(memory_space=pl.ANY)],
        out_specs=[HBM_SPEC] * (2 * n),
        out_shape=[pltpu.HBM(a.shape, a.dtype) for a in list(blocks) + list(lands)],
        input_output_aliases={i: i for i in range(2 * n)},
        compiler_params=pltpu.CompilerParams(has_side_effects=DATAFLOW_EFFECT),
    )(*blocks, *lands, send_sems, recv_sems, after)
    return list(outs[n:])


def _chip_scatter_start(sums, name):
    n = len(sums)

    def body(*refs):
        srcs, lands = refs[:n], refs[n:2 * n]
        send_sems, recv_sems, token = refs[2 * n], refs[2 * n + 1], refs[-1]
        x, y, c = _where_am_i()
        for p in range(n):
            for k, (cx, cy) in enumerate(_other_chips(x, y)):
                _remote(srcs[p].at[2 * cx + cy], lands[p].at[k], send_sems, recv_sems, 3 * p + k, (cx, cy, c)).start()
        token[...] = jnp.zeros_like(token)

    lands = [pltpu.with_memory_space_constraint(lax.empty((N_CHIPS - 1,) + s.shape[1:], s.dtype), pltpu.HBM)
             for s in sums]
    sums = [pltpu.with_memory_space_constraint(s, pltpu.HBM) for s in sums]
    outs = pl.pallas_call(
        body, name=name,
        in_specs=[HBM_SPEC] * (2 * n),
        out_specs=[SEM_SPEC, SEM_SPEC] + [HBM_SPEC] * (2 * n) + [pl.BlockSpec(memory_space=pltpu.VMEM)],
        out_shape=[pltpu.SemaphoreType.DMA((3 * n,)), pltpu.SemaphoreType.DMA((3 * n,))]
        + [pltpu.HBM(a.shape, a.dtype) for a in sums + lands] + [jax.ShapeDtypeStruct((8, LANE), F32)],
        input_output_aliases={i: 2 + i for i in range(2 * n)},
        compiler_params=pltpu.CompilerParams(has_side_effects=DATAFLOW_EFFECT),
    )(*sums, *lands)
    return outs[0], outs[1], list(outs[2:2 + n]), list(outs[2 + n:2 + 2 * n]), outs[-1]


def _chip_scatter_wait(send_sems, recv_sems, sums, lands, after, name):
    n = len(sums)

    def body(*refs):
        srcs, zones = refs[:n], refs[n:2 * n]
        send, recv = refs[2 * n], refs[2 * n + 1]
        x, y, c = _where_am_i()
        for p in range(n):
            for k, (cx, cy) in enumerate(_other_chips(x, y)):
                cp = _remote(srcs[p].at[2 * cx + cy], zones[p].at[k], send, recv, 3 * p + k, (x, y, c))
                cp.wait_send()
                cp.wait_recv()

    outs = pl.pallas_call(
        body, name=name,
        in_specs=[HBM_SPEC] * (2 * n) + [SEM_SPEC, SEM_SPEC, pl.BlockSpec(memory_space=pl.ANY)],
        out_specs=[HBM_SPEC] * (2 * n),
        out_shape=[pltpu.HBM(a.shape, a.dtype) for a in list(sums) + list(lands)],
        input_output_aliases={i: i for i in range(2 * n)},
        compiler_params=pltpu.CompilerParams(has_side_effects=DATAFLOW_EFFECT),
    )(*sums, *lands, send_sems, recv_sems, after)
    return list(outs[n:])


def _chip_allgather_forward(lands, name):
    n = len(lands)

    def body(*refs):
        ins, outs = refs[:n], refs[n:2 * n]
        send_sems, recv_sems = refs[2 * n:]
        x, y, c = _where_am_i()
        me, sib = (x, y, c), (x, y, 1 - c)
        chips = _other_chips(x, y)
        passed = [_remote(ins[p].at[2 * cx + cy, c], outs[p].at[2 * cx + cy, c], send_sems, recv_sems, 3 * p + k, sib)
                  for p in range(n) for k, (cx, cy) in enumerate(chips)]
        for cp in passed:
            cp.start()
        for p in range(n):
            for k, (cx, cy) in enumerate(chips):
                blk = outs[p].at[2 * cx + cy, 1 - c]
                _remote(blk, blk, send_sems, recv_sems, 3 * p + k, me).wait_recv()
        for cp in passed:
            cp.wait_send()

    return pl.pallas_call(
        body, name=name, in_specs=[HBM_SPEC] * n, out_specs=[HBM_SPEC] * n,
        out_shape=[jax.ShapeDtypeStruct(a.shape, a.dtype) for a in lands],
        input_output_aliases={i: i for i in range(n)},
        scratch_shapes=[pltpu.SemaphoreType.DMA((3 * n,)), pltpu.SemaphoreType.DMA((3 * n,))],
    )(*lands)


def _sibling_swap_many(pieces, name):
    n = len(pieces)

    def body(*refs):
        srcs, outs = refs[:n], refs[n:2 * n]
        send_sems, recv_sems = refs[2 * n:]
        x, y, c = _where_am_i()
        cps = [_remote(srcs[p].at[:, 1 - c], outs[p], send_sems, recv_sems, p, (x, y, 1 - c)) for p in range(n)]
        for cp in cps:
            cp.start()
        for cp in cps:
            cp.wait()

    return pl.pallas_call(
        body, name=name, in_specs=[HBM_SPEC] * n, out_specs=[HBM_SPEC] * n,
        out_shape=[jax.ShapeDtypeStruct((g.shape[0],) + g.shape[2:], g.dtype) for g in pieces],
        scratch_shapes=[pltpu.SemaphoreType.DMA((n,)), pltpu.SemaphoreType.DMA((n,))],
    )(*pieces)


def _chip_scatter_many(sums, name):
    n = len(sums)

    def body(*refs):
        srcs, outs = refs[:n], refs[n:2 * n]
        send_sems, recv_sems = refs[2 * n:]
        x, y, c = _where_am_i()
        cps = [_remote(srcs[p].at[2 * cx + cy], outs[p].at[k], send_sems, recv_sems, 3 * p + k, (cx, cy, c))
               for p in range(n) for k, (cx, cy) in enumerate(_other_chips(x, y))]
        for cp in cps:
            cp.start()
        for cp in cps:
            cp.wait()

    return pl.pallas_call(
        body, name=name, in_specs=[HBM_SPEC] * n, out_specs=[HBM_SPEC] * n,
        out_shape=[jax.ShapeDtypeStruct((N_CHIPS - 1,) + s.shape[1:], s.dtype) for s in sums],
        scratch_shapes=[pltpu.SemaphoreType.DMA((3 * n,)), pltpu.SemaphoreType.DMA((3 * n,))],
    )(*sums)


def _sibling_send_many(halves, name):
    n = len(halves)

    def body(*refs):
        srcs, outs = refs[:n], refs[n:2 * n]
        send_sems, recv_sems = refs[2 * n:]
        x, y, c = _where_am_i()
        cps = [_remote(srcs[p], outs[p], send_sems, recv_sems, p, (x, y, 1 - c)) for p in range(n)]
        for cp in cps:
            cp.start()
        for cp in cps:
            cp.wait()

    return pl.pallas_call(
        body, name=name, in_specs=[HBM_SPEC] * n, out_specs=[HBM_SPEC] * n,
        out_shape=[jax.ShapeDtypeStruct(h.shape, h.dtype) for h in halves],
        scratch_shapes=[pltpu.SemaphoreType.DMA((n,)), pltpu.SemaphoreType.DMA((n,))],
    )(*halves)


def _row_tile(rows, pref=512):
    best = 8
    for t in range(8, pref + 1, 8):
        if rows % t == 0:
            best = t
    return best


def _where_ids():
    x, y, c = _where_am_i()
    return jnp.stack([c, 2 * x + y]).astype(jnp.int32)


RS_ROWS = 256


def _chip_sums_bf16(G, A, ids, name):
    n, _, hr, cols = G.shape
    tr = _row_tile(hr, RS_ROWS)

    def body(ids_ref, g_ref, a_ref, o_ref):
        o_ref[...] = (g_ref[...] + a_ref[...]).astype(BF16)

    return pl.pallas_call(
        body, name=name,
        grid_spec=pltpu.PrefetchScalarGridSpec(
            num_scalar_prefetch=1, grid=(n, hr // tr),
            in_specs=[pl.BlockSpec((None, None, tr, cols), lambda j, i, ids: (j, ids[0], i, 0)),
                      pl.BlockSpec((None, tr, cols), lambda j, i, ids: (j, i, 0))],
            out_specs=pl.BlockSpec((None, tr, cols), lambda j, i, ids: (j, i, 0))),
        out_shape=jax.ShapeDtypeStruct((n, hr, cols), BF16),
        compiler_params=_cp(("parallel", "parallel")),
    )(ids, G, A)


def _total_half(G, A, B, ids, name):
    _, _, hr, cols = G.shape
    tr = _row_tile(hr, RS_ROWS)

    def body(ids_ref, g_ref, a_ref, b_ref, o_ref):
        s = g_ref[...] + a_ref[...]
        for k in range(N_CHIPS - 1):
            s = s + b_ref[k].astype(F32)
        o_ref[...] = s

    return pl.pallas_call(
        body, name=name,
        grid_spec=pltpu.PrefetchScalarGridSpec(
            num_scalar_prefetch=1, grid=(hr // tr,),
            in_specs=[pl.BlockSpec((None, None, tr, cols), lambda i, ids: (ids[1], ids[0], i, 0)),
                      pl.BlockSpec((None, tr, cols), lambda i, ids: (ids[1], i, 0)),
                      pl.BlockSpec((N_CHIPS - 1, tr, cols), lambda i, ids: (0, i, 0))],
            out_specs=pl.BlockSpec((tr, cols), lambda i, ids: (i, 0))),
        out_shape=jax.ShapeDtypeStruct((hr, cols), F32),
        compiler_params=_cp(("parallel",)),
    )(ids, G, A, B)


def _small_allreduce(v, name):
    R, Wd = v.shape

    def body(x_ref, sum_ref, all_ref, send_sems, recv_sems, local_sem):
        x, y, c = _where_am_i()
        me, sib = (x, y, c), (x, y, 1 - c)
        chips = _other_chips(x, y)

        def slot(px, py, pc):
            return all_ref.at[4 * px + 2 * py + pc]

        local = pltpu.make_async_copy(x_ref, slot(*me), local_sem)
        local.start()
        first = [_remote(x_ref, slot(*me), send_sems, recv_sems, 0, sib)]
        first += [_remote(x_ref, slot(*me), send_sems, recv_sems, 1 + k, (cx, cy, c)) for k, (cx, cy) in enumerate(chips)]
        for cp in first:
            cp.start()
        passed = []
        for k, (cx, cy) in enumerate(chips):
            blk = slot(cx, cy, c)
            _remote(blk, blk, send_sems, recv_sems, 1 + k, me).wait_recv()
            fw = _remote(blk, blk, send_sems, recv_sems, 4 + k, sib)
            fw.start()
            passed.append(fw)
        _remote(slot(*sib), slot(*sib), send_sems, recv_sems, 0, me).wait_recv()
        for k, (cx, cy) in enumerate(chips):
            blk = slot(cx, cy, 1 - c)
            _remote(blk, blk, send_sems, recv_sems, 4 + k, me).wait_recv()
        for cp in first + passed:
            cp.wait_send()
        local.wait()
        s = all_ref[0]
        for d in range(1, N_DEV):
            s = s + all_ref[d]
        sum_ref[...] = s

    vm = pl.BlockSpec(memory_space=pltpu.VMEM)
    return pl.pallas_call(
        body, name=name, in_specs=[vm], out_specs=[vm, vm],
        out_shape=[jax.ShapeDtypeStruct((R, Wd), F32), jax.ShapeDtypeStruct((N_DEV, R, Wd), F32)],
        scratch_shapes=[pltpu.SemaphoreType.DMA((7,)), pltpu.SemaphoreType.DMA((7,)), pltpu.SemaphoreType.DMA],
    )(v)[0]


def _adamw(w, g, m, v, name):
    L, R, Wd = w.shape
    tr = _row_tile(R, RS_ROWS)
    c1 = 1.0 - ADAM_B1 ** ADAM_STEP
    c2 = 1.0 - ADAM_B2 ** ADAM_STEP

    def body(w_ref, g_ref, m_ref, v_ref, d_ref, nm_ref, nv_ref):
        gv = g_ref[...]
        nm = ADAM_B1 * m_ref[...] + (1.0 - ADAM_B1) * gv
        nv = ADAM_B2 * v_ref[...] + (1.0 - ADAM_B2) * (gv * gv)
        d_ref[...] = -ADAM_LR * ((nm / c1) / (jnp.sqrt(nv / c2) + ADAM_EPS) + ADAM_WD * w_ref[...])
        nm_ref[...] = nm
        nv_ref[...] = nv

    row = pl.BlockSpec((None, tr, Wd), lambda l, i: (l, i, 0))
    shp = jax.ShapeDtypeStruct((L, R, Wd), F32)
    return pl.pallas_call(
        body, name=name, grid=(L, R // tr), in_specs=[row] * 4, out_specs=[row] * 3, out_shape=[shp] * 3,
        compiler_params=_cp(("parallel", "parallel")),
    )(w, g, m, v)


def _adamw_halves(w, m, v, mine, theirs, ids, name, after=None):
    L, R, Wd = w.shape
    hr = R // 2
    tr = _row_tile(hr, RS_ROWS)
    c1 = 1.0 - ADAM_B1 ** ADAM_STEP
    c2 = 1.0 - ADAM_B2 ** ADAM_STEP
    ordered = [] if after is None else [after]

    def body(ids_ref, w_ref, m_ref, v_ref, *refs):
        g_refs, (g_ref, d_ref, nm_ref, nv_ref) = refs[:2 * L], refs[-4:]
        layer, half = pl.program_id(0), pl.program_id(1)
        own = half == ids_ref[0]
        gv = jnp.where(own, g_refs[0][...], g_refs[L][...])
        for l in range(1, L):
            gv = jnp.where(layer == l, jnp.where(own, g_refs[l][...], g_refs[L + l][...]), gv)
        nm = ADAM_B1 * m_ref[...] + (1.0 - ADAM_B1) * gv
        nv = ADAM_B2 * v_ref[...] + (1.0 - ADAM_B2) * (gv * gv)
        g_ref[...] = gv
        d_ref[...] = -ADAM_LR * ((nm / c1) / (jnp.sqrt(nv / c2) + ADAM_EPS) + ADAM_WD * w_ref[...])
        nm_ref[...] = nm
        nv_ref[...] = nv

    blk = pl.BlockSpec((None, None, tr, Wd), lambda l, h, i, ids: (l, h, i, 0))
    g_blk = pl.BlockSpec((tr, Wd), lambda l, h, i, ids: (i, 0))
    shp = jax.ShapeDtypeStruct((L, 2, hr, Wd), F32)
    outs = pl.pallas_call(
        body, name=name,
        grid_spec=pltpu.PrefetchScalarGridSpec(
            num_scalar_prefetch=1, grid=(L, 2, hr // tr),
            in_specs=[blk] * 3 + [g_blk] * (2 * L)
            + [pl.BlockSpec((8, LANE), lambda l, h, i, ids: (0, 0)) for _ in ordered], out_specs=[blk] * 4),
        out_shape=[shp] * 4,
        compiler_params=_cp(("parallel", "parallel", "parallel")),
    )(ids, *[a.reshape(L, 2, hr, Wd) for a in (w, m, v)], *mine, *theirs, *ordered)
    return tuple(o.reshape(L, R, Wd) for o in outs)


def _two_halves(a):
    cols = a.shape[-1]
    return a.reshape(2, _size(a.shape) // (2 * cols), cols)


FIRST_NEEDED = ("w_in_e",)
LATER_NEEDED = tuple(n for n in BIG if n not in FIRST_NEEDED)


def _gather_weights(P):
    chip = 2 * lax.axis_index("x") + lax.axis_index("y")

    def with_own(landed, own):
        return lax.dynamic_update_slice_in_dim(landed, own[None], chip, axis=0)

    mine = {n: _two_halves(P[n].astype(BF16)) for n in BIG}
    first = _chip_allgather_many([mine[n] for n in FIRST_NEEDED], "ag_first")
    gathered = {n: with_own(g, mine[n]) for n, g in zip(FIRST_NEEDED, first)}
    send_sems, recv_sems, blocks, lands, token = _chip_allgather_start([mine[n] for n in LATER_NEEDED], "ag_start")

    def later(after):
        landed = _chip_allgather_wait(send_sems, recv_sems, blocks, lands, after, "ag_wait")
        g = {n: with_own(a, mine[n]) for n, a in zip(LATER_NEEDED, _chip_allgather_forward(landed, "ag_forward"))}
        w_in_o = g["w_in_o"].reshape(N_CHIPS, D_MODEL, ODD_IN // N_CHIPS)
        return {
            "w_out_e": g["w_out_e"].reshape(D_MODEL, D_MODEL),
            "w_out_o": g["w_out_o"].reshape(D_MODEL, D_MODEL),
            "w_in_o": jnp.pad(jnp.concatenate([w_in_o[j] for j in range(N_CHIPS)], axis=1),
                              ((0, 0), (0, ODD_IN_PAD - ODD_IN))),
            "w_up": g["w_up"],
            "w_down": g["w_down"].transpose(1, 0, 2, 3).reshape(2, FFN_DIM, D_MODEL),
            "w_ple_gate": g["w_ple_gate"].transpose(1, 0, 2, 3).reshape(2, D_MODEL, D_MODEL),
            "w_ple": g["w_ple"].transpose(1, 2, 0, 3).reshape(2, PLE_DIM, D_MODEL),
        }

    small_shapes = [P[n].shape for n, _ in SHARDED_SMALL]
    small = _chip_allgather(_pack([P[n] for n, _ in SHARDED_SMALL], SMALL_W, SMALL_ROWS), "ag_small")
    parts = [_unpack(small[j], small_shapes) for j in range(N_CHIPS)]
    full = {n: jnp.concatenate([parts[j][i] for j in range(N_CHIPS)], axis=ax)
            for i, (n, ax) in enumerate(SHARDED_SMALL)}
    W = {n: P[n] for n in REPLICATED}
    W["pool_w"] = P["pool_w"][0]
    W["final_norm"] = P["final_norm"].reshape(1, D_MODEL)
    W["mix_norm_o"] = full["mix_norm_o"]
    W["conv_qkv_o"] = full["conv_qkv_o"][0]
    W["ffn_conv"] = full["ffn_conv"]
    W["w_in_e"] = gathered["w_in_e"].reshape(N_CHIPS, D_MODEL, IN_E_SHARD)
    return W, token, later


def _chip_major_w_in_o(g):
    shard = ODD_IN // N_CHIPS
    return jnp.stack([g[:, j * shard:(j + 1) * shard] for j in range(N_CHIPS)])


def _reduce_begin(grads, ids, tag, travel_later):
    keys = list(grads)
    pieces = [g.reshape(N_CHIPS, 2, g.shape[1] // 2, g.shape[2]) for g in grads.values()]
    from_sibling = _sibling_swap_many(pieces, f"rs_sibling_swap_{tag}")
    sums = [_chip_sums_bf16(g, a, ids, f"rs_chip_sums_{tag}{i}") for i, (g, a) in enumerate(zip(pieces, from_sibling))]
    state = dict(keys=keys, pieces=pieces, from_sibling=from_sibling, ids=ids, tag=tag, token=None)
    if travel_later:
        state["flight"] = _chip_scatter_start(sums, f"rs_scatter_start_{tag}")
        state["token"] = state["flight"][-1]
    else:
        state["from_chips"] = _chip_scatter_many(sums, f"rs_chip_scatter_{tag}")
    return state


def _reduce_end(state, after=None):
    tag, ids = state["tag"], state["ids"]
    if "flight" in state:
        send_sems, recv_sems, sums, lands, _ = state["flight"]
        from_chips = _chip_scatter_wait(send_sems, recv_sems, sums, lands, after, f"rs_scatter_wait_{tag}")
    else:
        from_chips = state["from_chips"]
    halves = [_total_half(g, a, b, ids, f"rs_total_{tag}{i}")
              for i, (g, a, b) in enumerate(zip(state["pieces"], state["from_sibling"], from_chips))]
    theirs = _sibling_send_many(halves, f"rs_sibling_send_{tag}")
    return {k: (h, t) for k, h, t in zip(state["keys"], halves, theirs)}


def kernel(x, p, mix_norm_e, w_in_e, pool_w, pool_scale, w_out_e, mix_norm_o, w_in_o, conv_qkv_o, a_log_o, dt_bias_o, gdn_norm_o, w_out_o, ffn_norm, w_up, ffn_conv, w_down, ple_norm, w_ple_gate, w_ple, final_norm, loss_target, m_mix_norm_e, m_w_in_e, m_pool_w, m_pool_scale, m_w_out_e, m_mix_norm_o, m_w_in_o, m_conv_qkv_o, m_a_log_o, m_dt_bias_o, m_gdn_norm_o, m_w_out_o, m_ffn_norm, m_w_up, m_ffn_conv, m_w_down, m_ple_norm, m_w_ple_gate, m_w_ple, m_final_norm, v_mix_norm_e, v_w_in_e, v_pool_w, v_pool_scale, v_w_out_e, v_mix_norm_o, v_w_in_o, v_conv_qkv_o, v_a_log_o, v_dt_bias_o, v_gdn_norm_o, v_w_out_o, v_ffn_norm, v_w_up, v_ffn_conv, v_w_down, v_ple_norm, v_w_ple_gate, v_w_ple, v_final_norm):
    args = locals()
    P = {n: args[n] for n in WEIGHT_ORDER}
    M = {n: args["m_" + n] for n in WEIGHT_ORDER}
    V = {n: args["v_" + n] for n in WEIGHT_ORDER}

    W, token, later_weights = _gather_weights(P)
    T = x.shape[1]
    ids = _where_ids()
    early = {}

    def on_grads(stage, grads):
        early[stage] = _reduce_begin(grads, ids, stage, travel_later=True)
        return early[stage]["token"]

    sq, grad_x, G = _local_step(x.reshape(T, D_MODEL), p.reshape(2, T, PLE_DIM), loss_target.reshape(T, D_MODEL), W,
                                token, later_weights, on_grads)
    last = _reduce_begin({("w_in_e", 0): G["w_in_e"], ("w_out_e", 0): G["w_out_e"]}, ids, "first_mixer",
                         travel_later=True)
    reduced = {}
    for state in early.values():
        reduced.update(_reduce_end(state, after=grad_x))
    out = {}

    def adamw(n, after=None):
        halves = [reduced[(n, l)] for l in range(P[n].shape[0])]
        out[n] = _adamw_halves(P[n], M[n], V[n], [h[0] for h in halves], [h[1] for h in halves], ids, f"adamw_{n}",
                               after)

    last_names = ("w_in_e", "w_out_e")
    for n in BIG:
        if n not in last_names:
            adamw(n, after=last["token"])
    reduced.update(_reduce_end(last, after=out["w_up"][1]))
    for n in last_names:
        adamw(n)

    small_full = {n: G[n] for n in REPLICATED}
    small_full["pool_w"] = G["pool_w"][None]
    small_full["final_norm"] = G["final_norm"].reshape(D_MODEL)
    small_full["mix_norm_o"] = G["mix_norm_o"]
    small_full["conv_qkv_o"] = G["conv_qkv_o"][None]
    small_full["ffn_conv"] = G["ffn_conv"]
    small_names = REPLICATED + tuple(n for n, _ in SHARDED_SMALL)
    summed = _small_allreduce(_pack([small_full[n] for n in small_names] + [sq.reshape(1)], SMALL_W, 8), "ar_small")
    *g_list, sq_total = _unpack(summed, [small_full[n].shape for n in small_names] + [(1,)])
    g_small = dict(zip(small_names, g_list))
    chip = 2 * lax.axis_index("x") + lax.axis_index("y")
    for n, ax in SHARDED_SMALL:
        width = P[n].shape[ax]
        g_small[n] = lax.dynamic_slice_in_dim(g_small[n], chip * width, width, axis=ax)

    def pack_small(D):
        return _pack([D[n] for n in small_names], SMALL_W, RS_ROWS)[None]

    g_pack = pack_small(g_small)
    upd = _adamw(pack_small(P), g_pack, pack_small(M), pack_small(V), "adamw_small")
    shapes = [P[n].shape for n in small_names]
    for n, *vals in zip(small_names, *[_unpack(a[0], shapes) for a in (g_pack,) + tuple(upd)]):
        out[n] = tuple(vals)

    loss = (0.5 / D_MODEL) * sq_total[0]
    return (loss, grad_x[None]) + tuple(out[n][i] for i in range(4) for n in WEIGHT_ORDER)
```

```python
import functools

import jax
import jax.numpy as jnp
from jax import lax
from jax.experimental import pallas as pl
from jax.experimental.pallas import tpu as pltpu

F32 = jnp.float32
BF16 = jnp.bfloat16

D_MODEL = 1024
PLE_DIM = 256
POOL_WIDTH = 512
POOL_WINDOWS = (2, 4, 8, 16)
POOL_GROUP_DIM = 128
SB_HEADS = 8
SB_HEAD_DIM = 64
GDN_HEADS = 8
GDN_HEAD_DIM = 128
GDN_CONV = 4
GDN_CHUNK = 64
FFN_DIM = 2816
FFN_CONV = 3
EPS = 1e-6
ODD_IN = 4 * D_MODEL + 2 * GDN_HEADS
ODD_IN_PAD = 33 * 128
ADAM_LR, ADAM_B1, ADAM_B2, ADAM_EPS, ADAM_WD, ADAM_STEP = 0.001, 0.9, 0.999, 1e-08, 0.01, 10

LANE = 128
VMEM_LIMIT = 56 * 1024 * 1024

N_CHIPS = 4
N_DEV = 8


def _cp(sem=None):
    return pltpu.CompilerParams(dimension_semantics=sem, vmem_limit_bytes=VMEM_LIMIT)


def _tile(n, pref):
    if n <= pref:
        return n
    best = None
    for t in range(LANE, pref + 1, LANE):
        if n % t == 0:
            best = t
    assert best is not None, (n, pref)
    return best


_DIMS = {"nn": (((1,), (0,)), ((), ())), "nt": (((1,), (1,)), ((), ())), "tn": (((0,), (0,)), ((), ()))}
_BDIMS = {"nn": (((2,), (1,)), ((0,), (0,))), "nt": (((2,), (2,)), ((0,), (0,))), "tn": (((1,), (1,)), ((0,), (0,)))}


def _dims(mode, ndim):
    return (_BDIMS if ndim == 3 else _DIMS)[mode]


def _dot(a, b, mode="nn"):
    return lax.dot_general(a.astype(BF16), b.astype(BF16), _dims(mode, a.ndim), preferred_element_type=F32)


def _bdot(a, b, mode="nn"):
    return lax.dot_general(a.astype(BF16), b.astype(BF16), _BDIMS[mode], preferred_element_type=F32)


def _split2(x):
    hi = x.astype(BF16)
    lo = (x - hi.astype(F32)).astype(BF16)
    return hi, lo


def _split3(x):
    hi = x.astype(BF16)
    r = x - hi.astype(F32)
    mid = r.astype(BF16)
    lo = (r - mid.astype(F32)).astype(BF16)
    return hi, mid, lo


def _dot_x01(x, m01, mode="nn"):
    hi, lo = _split2(x)
    return (lax.dot_general(hi, m01, _DIMS[mode], preferred_element_type=F32)
            + lax.dot_general(lo, m01, _DIMS[mode], preferred_element_type=F32))


def _dot3_raw(a, b, mode):
    ah, al = _split2(a)
    bh, bl = _split2(b)
    d = _dims(mode, a.ndim)
    return (lax.dot_general(ah, bh, d, preferred_element_type=F32)
            + lax.dot_general(ah, bl, d, preferred_element_type=F32)
            + lax.dot_general(al, bh, d, preferred_element_type=F32))


@jax.custom_vjp
def _dot3(a, b):
    return _dot3_raw(a, b, "nn")


def _dot3_fwd(a, b):
    return _dot3_raw(a, b, "nn"), (a, b)


def _dot3_bwd(res, g):
    a, b = res
    return _dot(g, b, "nt"), _dot(a, g, "tn")


_dot3.defvjp(_dot3_fwd, _dot3_bwd)


@jax.custom_vjp
def _dot1_nt(a, b):
    return _dot(a, b, "nt")


def _dot1_nt_fwd(a, b):
    return _dot(a, b, "nt"), (a, b)


def _dot1_nt_bwd(res, g):
    a, b = res
    return _dot(g, b, "nn"), _dot(g, a, "tn")


_dot1_nt.defvjp(_dot1_nt_fwd, _dot1_nt_bwd)


def _m01_left_raw(m, x):
    d = _dims("nn", x.ndim)
    if x.ndim == 3:
        m = jnp.broadcast_to(m, (x.shape[0],) + m.shape)
    p0, p1, p2 = _split3(x)
    return (lax.dot_general(m, p0, d, preferred_element_type=F32)
            + lax.dot_general(m, p1, d, preferred_element_type=F32)
            + lax.dot_general(m, p2, d, preferred_element_type=F32))


@jax.custom_vjp
def _m01_left(m, mt, x):
    return _m01_left_raw(m, x)


def _m01_left_fwd(m, mt, x):
    return _m01_left_raw(m, x), (m, mt)


def _m01_left_bwd(res, g):
    m, mt = res
    return jnp.zeros_like(m), jnp.zeros_like(mt), _m01_left_raw(mt, g)


_m01_left.defvjp(_m01_left_fwd, _m01_left_bwd)


def _softplus(x):
    return jnp.maximum(x, 0.0) + jnp.log(1.0 + jnp.exp(-jnp.abs(x)))


def _sigmoid(x):
    return 0.5 * jnp.tanh(0.5 * x) + 0.5


def _silu(x):
    return x * _sigmoid(x)


def _dsilu(x):
    s = _sigmoid(x)
    return s * (1.0 + x * (1.0 - s))


def _cols_of(n_blocks_per_part, *fixed):
    return lambda r, c: (c // n_blocks_per_part,) + fixed + (r, c % n_blocks_per_part)


def _rows_of(n_blocks_per_part, *fixed):
    return lambda r, c: (r // n_blocks_per_part,) + fixed + (r % n_blocks_per_part, c)


def _layer_of(layer):
    return lambda r, c: (layer, r, c)


def _mm(a, b, mode, name, out_dtype=F32, res=None, tm=1024, tn=512, tk=1024,
        dims=None, a_view=None, b_view=None, o_view=None, out_shape=None):
    if dims is None:
        if mode == "nn":
            (M, K), (K2, N) = a.shape, b.shape
        elif mode == "nt":
            (M, K), (N, K2) = a.shape, b.shape
        else:
            (K, M), (K2, N) = a.shape, b.shape
        assert K == K2, (name, a.shape, b.shape)
    else:
        M, N, K = dims
    tm, tn, tk = _tile(M, tm), _tile(N, tn), _tile(K, tk)
    nk = K // tk

    def spec(arr, blk, view, rc):
        view = view or (lambda r, c: (r, c))
        return pl.BlockSpec((None,) * (arr.ndim - 2) + blk, lambda i, j, k: view(*rc(i, j, k)))

    if mode == "tn":
        a_spec = spec(a, (tk, tm), a_view, lambda i, j, k: (k, i))
    else:
        a_spec = spec(a, (tm, tk), a_view, lambda i, j, k: (i, k))
    if mode == "nt":
        b_spec = spec(b, (tn, tk), b_view, lambda i, j, k: (j, k))
    else:
        b_spec = spec(b, (tk, tn), b_view, lambda i, j, k: (k, j))
    out_shape = out_shape or (M, N)
    o_spec = pl.BlockSpec((None,) * (len(out_shape) - 2) + (tm, tn),
                          lambda i, j, k: (o_view or (lambda r, c: (r, c)))(i, j))
    has_res = res is not None
    assert not (has_res and o_view), name

    def body(*refs):
        a_ref, b_ref = refs[:2]
        r_ref = refs[2] if has_res else None
        o_ref = refs[3] if has_res else refs[2]

        def finish(r):
            if has_res:
                r = r + r_ref[...]
            o_ref[...] = r.astype(out_dtype)

        if nk == 1:
            finish(_dot(a_ref[...], b_ref[...], mode))
            return
        acc = refs[-1]
        k = pl.program_id(2)

        @pl.when(k == 0)
        def _():
            acc[...] = jnp.zeros_like(acc)

        acc[...] += _dot(a_ref[...], b_ref[...], mode)

        @pl.when(k == nk - 1)
        def _():
            finish(acc[...])

    ins = [a, b] + ([res] if has_res else [])
    in_specs = [a_spec, b_spec] + ([o_spec] if has_res else [])
    return pl.pallas_call(
        body, name=name, grid=(M // tm, N // tn, nk),
        in_specs=in_specs, out_specs=o_spec,
        out_shape=jax.ShapeDtypeStruct(out_shape, out_dtype),
        scratch_shapes=[pltpu.VMEM((tm, tn), F32)] if nk > 1 else [],
        compiler_params=_cp(("parallel", "parallel", "arbitrary")),
    )(*ins)


def _rms_fwd(x, gain, name, after=None):
    T, D = x.shape
    tt = _tile(T, 512)

    def body(x_ref, g_ref, *rest):
        o_ref = rest[-1]
        xv = x_ref[...]
        r = lax.rsqrt(jnp.mean(xv * xv, axis=-1, keepdims=True) + EPS)
        o_ref[...] = (xv * r * g_ref[...]).astype(BF16)

    ordered = [] if after is None else [after]
    return pl.pallas_call(
        body, name=name, grid=(T // tt,),
        in_specs=[pl.BlockSpec((tt, D), lambda i: (i, 0)), pl.BlockSpec((1, D), lambda i: (0, 0))]
        + [pl.BlockSpec((8, LANE), lambda i: (0, 0)) for _ in ordered],
        out_specs=pl.BlockSpec((tt, D), lambda i: (i, 0)),
        out_shape=jax.ShapeDtypeStruct((T, D), BF16),
        compiler_params=_cp(("parallel",)),
    )(x, gain, *ordered)


def _rms_bwd(x, gain, dh, dres, name):
    T, D = x.shape
    tt = _tile(T, 512)

    def body(x_ref, g_ref, dh_ref, dr_ref, dx_ref, dg_ref):
        i = pl.program_id(0)
        xv = x_ref[...]
        dy = dh_ref[...].astype(F32)
        r = lax.rsqrt(jnp.mean(xv * xv, axis=-1, keepdims=True) + EPS)
        xn = xv * r
        gdy = dy * g_ref[...]
        dx = r * (gdy - xn * jnp.mean(gdy * xn, axis=-1, keepdims=True))
        dx_ref[...] = dr_ref[...] + dx

        @pl.when(i == 0)
        def _():
            dg_ref[...] = jnp.zeros_like(dg_ref)

        dg_ref[...] += jnp.sum(dy * xn, axis=0, keepdims=True)

    row = pl.BlockSpec((tt, D), lambda i: (i, 0))
    vec = pl.BlockSpec((1, D), lambda i: (0, 0))
    return pl.pallas_call(
        body, name=name, grid=(T // tt,),
        in_specs=[row, vec, row, row], out_specs=[row, vec],
        out_shape=[jax.ShapeDtypeStruct((T, D), F32), jax.ShapeDtypeStruct((1, D), F32)],
        compiler_params=_cp(("arbitrary",)),
    )(x, gain, dh, dres)


def _final_loss(x, gain, target, name):
    T, D = x.shape
    tt = _tile(T, 512)

    def body(x_ref, g_ref, t_ref, l_ref, dx_ref, dg_ref):
        i = pl.program_id(0)
        xv = x_ref[...]
        r = lax.rsqrt(jnp.mean(xv * xv, axis=-1, keepdims=True) + EPS)
        xn = xv * r
        err = xn * g_ref[...] - t_ref[...]
        dy = err * (1.0 / D)
        gdy = dy * g_ref[...]
        dx_ref[...] = r * (gdy - xn * jnp.mean(gdy * xn, axis=-1, keepdims=True))

        @pl.when(i == 0)
        def _():
            dg_ref[...] = jnp.zeros_like(dg_ref)
            l_ref[...] = jnp.zeros_like(l_ref)

        dg_ref[...] += jnp.sum(dy * xn, axis=0, keepdims=True)
        l_ref[...] += jnp.sum(jnp.sum(err * err, axis=1, keepdims=True), axis=0, keepdims=True)

    row = pl.BlockSpec((tt, D), lambda i: (i, 0))
    vec = pl.BlockSpec((1, D), lambda i: (0, 0))
    return pl.pallas_call(
        body, name=name, grid=(T // tt,),
        in_specs=[row, vec, row],
        out_specs=[pl.BlockSpec((8, LANE), lambda i: (0, 0)), row, vec],
        out_shape=[jax.ShapeDtypeStruct((8, LANE), F32), jax.ShapeDtypeStruct((T, D), F32),
                   jax.ShapeDtypeStruct((1, D), F32)],
        compiler_params=_cp(("arbitrary",)),
    )(x, gain, target)


def _shift_down(x, i, t_idx):
    if i == 0:
        return x
    return jnp.where(t_idx >= i, pltpu.roll(x, i, 0), 0.0)


def _shift_up(x, i, t_idx):
    if i == 0:
        return x
    n = x.shape[0]
    return jnp.where(t_idx < n - i, pltpu.roll(x, n - i, 0), 0.0)


def _pool_select(g, vals):
    out = vals[-1]
    for gi in range(len(vals) - 2, -1, -1):
        out = jnp.where(g == gi, vals[gi], out)
    return out


def _pool_y(u, g, t_idx):
    s1 = u + _shift_down(u, 1, t_idx)
    s2 = s1 + _shift_down(s1, 2, t_idx)
    s3 = s2 + _shift_down(s2, 4, t_idx)
    s4 = s3 + _shift_down(s3, 8, t_idx)
    ws = _pool_select(g, [s1, s2, s3, s4])
    win = _pool_select(g, [jnp.float32(w) for w in POOL_WINDOWS])
    cnt = jnp.minimum(t_idx.astype(F32) + 1.0, win)
    return ws / cnt - u, cnt


def _pool_fwd(proj, pool_w, pool_scale):
    T = proj.shape[0]
    G, C = len(POOL_WINDOWS), POOL_GROUP_DIM

    def body(u_ref, w_ref, s_ref, o_ref):
        g = pl.program_id(0)
        t_idx = lax.broadcasted_iota(jnp.int32, (T, C), 0)
        y, _ = _pool_y(u_ref[...], g, t_idx)
        o_ref[...] = _dot(y, w_ref[0]) * s_ref[...]

    return pl.pallas_call(
        body, name="pool_fwd", grid=(G,),
        in_specs=[pl.BlockSpec((T, C), lambda g: (0, g)), pl.BlockSpec((1, C, C), lambda g: (g, 0, 0)),
                  pl.BlockSpec((1, C), lambda g: (0, g))],
        out_specs=pl.BlockSpec((T, C), lambda g: (0, g)),
        out_shape=jax.ShapeDtypeStruct((T, G * C), F32),
        compiler_params=_cp(("parallel",)),
    )(proj, pool_w, pool_scale)


def _pool_bwd(proj, pool_w, pool_scale, dmix):
    T = proj.shape[0]
    G, C = len(POOL_WINDOWS), POOL_GROUP_DIM

    def body(u_ref, w_ref, s_ref, do_ref, du_ref, dw_ref, ds_ref):
        g = pl.program_id(0)
        t_idx = lax.broadcasted_iota(jnp.int32, (T, C), 0)
        y, cnt = _pool_y(u_ref[...], g, t_idx)
        w = w_ref[0]
        dout = do_ref[...]
        ds_ref[...] = jnp.sum(dout * _dot(y, w), axis=0, keepdims=True)
        dy2 = dout * s_ref[...]
        dw_ref[0] = _dot(y, dy2, "tn")
        dy = _dot(dy2, w, "nt")
        dz = dy / cnt
        r1 = dz + _shift_up(dz, 1, t_idx)
        r2 = r1 + _shift_up(r1, 2, t_idx)
        r3 = r2 + _shift_up(r2, 4, t_idx)
        r4 = r3 + _shift_up(r3, 8, t_idx)
        du_ref[...] = _pool_select(g, [r1, r2, r3, r4]) - dy

    col = pl.BlockSpec((T, C), lambda g: (0, g))
    return pl.pallas_call(
        body, name="pool_bwd", grid=(G,),
        in_specs=[col, pl.BlockSpec((1, C, C), lambda g: (g, 0, 0)), pl.BlockSpec((1, C), lambda g: (0, g)), col],
        out_specs=[col, pl.BlockSpec((1, C, C), lambda g: (g, 0, 0)), pl.BlockSpec((1, C), lambda g: (0, g))],
        out_shape=[jax.ShapeDtypeStruct((T, G * C), F32), jax.ShapeDtypeStruct((G, C, C), F32),
                   jax.ShapeDtypeStruct((1, G * C), F32)],
        compiler_params=_cp(("parallel",)),
    )(proj, pool_w, pool_scale, dmix)


SB_SCALE = SB_HEAD_DIM ** -0.5
SB_PASS_SIZES = (4, 2, 1)
SB_PASS_SIZES_BWD = (2, 1)


def _sb_tile_logits(qb, kblk, valid):
    z = _dot(qb, kblk, "nt")
    sp = _softplus(z)
    l1m = -sp
    if valid is not None:
        l1m = jnp.where(valid, l1m, 0.0)
    return z, sp, l1m


SB_PAIR = LANE // SB_HEAD_DIM
SB_Q0 = POOL_WIDTH // LANE
SB_NB = SB_HEADS // SB_PAIR


def _sb_head_masks():
    lane = lax.broadcasted_iota(jnp.int32, (1, LANE), 1)
    return [(lane // SB_HEAD_DIM == h).astype(F32) for h in range(SB_PAIR)]


def _sb_fwd(proj):
    T = proj.shape[0]
    B = _tile(T, 256)
    nq = T // B

    def body(q_ref, k_ref, v_ref, o_ref, l_ref, k_bf, v_bf):
        qi = pl.program_id(1)

        @pl.when(qi == 0)
        def _():
            k_bf[...] = k_ref[...].astype(BF16)
            v_bf[...] = v_ref[...].astype(BF16)

        masks = _sb_head_masks()
        q_all = q_ref[...]
        qbs = [(q_all * (m * SB_SCALE)).astype(BF16) for m in masks]
        row = lax.broadcasted_iota(jnp.int32, (B, B), 0)
        col = lax.broadcasted_iota(jnp.int32, (B, B), 1)
        later = (row > col).astype(BF16)

        def tiles(kbs, state, valid):
            ksl = [pl.ds(pl.multiple_of(kb * B, B), B) for kb in kbs]
            kblks = [k_bf[ks, :] for ks in ksl]
            masks_of = [valid] + [None] * (len(kbs) - 1)
            logits = [[_sb_tile_logits(qb, kblk, m) for kblk, m in zip(kblks, masks_of)] for qb in qbs]
            within = [[_dot_x01(l1m, later) for _, _, l1m in lg] for lg in logits]
            sums = [[jnp.sum(l1m, axis=1, keepdims=True) for _, _, l1m in lg] for lg in logits]
            out = []
            for h, (carry, acc) in enumerate(state):
                for (z, sp, _), rc, s, ks, m in zip(logits[h], within[h], sums[h], ksl, masks_of):
                    a = jnp.exp(z - sp + rc + carry)
                    if m is not None:
                        a = jnp.where(m, a, 0.0)
                    acc = acc + _dot(a, v_bf[ks, :])
                    carry = carry + s
                out.append((carry, acc))
            return tuple(out)

        state = ((jnp.zeros((B, 1), F32), jnp.zeros((B, LANE), F32)),) * SB_PAIR
        state = lax.cond(qi >= 1, lambda c: tiles([qi, qi - 1], c, col < row), lambda c: tiles([qi], c, col < row),
                         state)
        left = jnp.maximum(qi - 1, 0)
        for size in SB_PASS_SIZES:
            n_pass = left // size
            state = lax.fori_loop(
                0, n_pass, lambda i, c, left=left, size=size: tiles([left - 1 - size * i - u for u in range(size)],
                                                                     c, None), state)
            left = left - n_pass * size
        o_ref[...] = sum(acc * m for (_, acc), m in zip(state, masks))
        for h, (carry, _) in enumerate(state):
            l_ref[h] = carry

    return pl.pallas_call(
        body, name="sb_fwd", grid=(SB_NB, nq),
        in_specs=[pl.BlockSpec((B, LANE), lambda hp, i: (i, SB_Q0 + hp)),
                  pl.BlockSpec((T, LANE), lambda hp, i: (0, SB_Q0 + SB_NB + hp)),
                  pl.BlockSpec((T, LANE), lambda hp, i: (0, SB_Q0 + 2 * SB_NB + hp))],
        out_specs=[pl.BlockSpec((B, LANE), lambda hp, i: (i, hp)),
                   pl.BlockSpec((SB_PAIR, B, 1), lambda hp, i: (hp, i, 0))],
        out_shape=[jax.ShapeDtypeStruct((T, SB_HEADS * SB_HEAD_DIM), F32), jax.ShapeDtypeStruct((SB_HEADS, T, 1), F32)],
        scratch_shapes=[pltpu.VMEM((T, LANE), BF16), pltpu.VMEM((T, LANE), BF16)],
        compiler_params=_cp(("parallel", "arbitrary")),
    )(proj, proj, proj)


def _sb_bwd(proj, dmix, ltot, after=None):
    T = proj.shape[0]
    B = _tile(T, 256)
    nq = T // B
    ordered = [] if after is None else [after]

    def body(q_ref, k_ref, v_ref, do_ref, l_ref, *rest):
        dq_ref, dk_ref, dv_ref, k_bf, v_bf = rest[len(ordered):]
        qi = pl.program_id(1)

        @pl.when(qi == 0)
        def _():
            k_bf[...] = k_ref[...].astype(BF16)
            v_bf[...] = v_ref[...].astype(BF16)
            dk_ref[...] = jnp.zeros_like(dk_ref)
            dv_ref[...] = jnp.zeros_like(dv_ref)

        masks = _sb_head_masks()
        q_all, do_all = q_ref[...], do_ref[...]
        qbs = [(q_all * (m * SB_SCALE)).astype(BF16) for m in masks]
        dobs = [(do_all * m).astype(BF16) for m in masks]
        ltots = [l_ref[h] for h in range(SB_PAIR)]
        row = lax.broadcasted_iota(jnp.int32, (B, B), 0)
        col = lax.broadcasted_iota(jnp.int32, (B, B), 1)
        upto = (row <= col).astype(BF16)
        before = (row < col).astype(BF16)

        def tiles(kbs, state, valid):
            ksl = [pl.ds(pl.multiple_of(kb * B, B), B) for kb in kbs]
            kblks = [k_bf[ks, :] for ks in ksl]
            vblks = [v_bf[ks, :] for ks in ksl]
            masks_of = [None] * (len(kbs) - 1) + [valid]
            logits = [[_sb_tile_logits(qb, kblk, m) for kblk, m in zip(kblks, masks_of)] for qb in qbs]
            das = [[_dot(dob, vblk, "nt") for vblk in vblks] for dob in dobs]
            within = [[_dot_x01(l1m, upto) for _, _, l1m in lg] for lg in logits]
            avals, es, Ps = [], [], []
            for h, (P, _, _) in enumerate(state):
                a_h, e_h = [], []
                for (z, sp, l1m), pc, da, m in zip(logits[h], within[h], das[h], masks_of):
                    a = jnp.exp(z - sp + (ltots[h] - P - pc))
                    if m is not None:
                        a = jnp.where(m, a, 0.0)
                    a_h.append(a)
                    e_h.append(da * a)
                    P = P + jnp.sum(l1m, axis=1, keepdims=True)
                avals.append(a_h)
                es.append(e_h)
                Ps.append(P)
            e_within = [[_dot_x01(e, before) for e in e_h] for e_h in es]
            out = []
            for h, (_, E, dq) in enumerate(state):
                for (z, sp, _), e, ew, a, kblk, ks, m in zip(logits[h], es[h], e_within[h], avals[h], kblks, ksl,
                                                              masks_of):
                    dz = e * jnp.exp(-sp) - jnp.exp(z - sp) * (ew + E)
                    if m is not None:
                        dz = jnp.where(m, dz, 0.0)
                    dzb = dz.astype(BF16)
                    dq = dq + _dot(dzb, kblk)
                    dk_ref[ks, :] += _dot(dzb, qbs[h], "tn")
                    dv_ref[ks, :] += _dot(a, dobs[h], "tn")
                    E = E + jnp.sum(e, axis=1, keepdims=True)
                out.append((Ps[h], E, dq))
            return tuple(out)

        zeros1 = jnp.zeros((B, 1), F32)
        state = ((zeros1, zeros1, jnp.zeros((B, LANE), F32)),) * SB_PAIR
        done = 0
        before_last = jnp.maximum(qi - 1, 0)
        for size in SB_PASS_SIZES_BWD:
            n_pass = (before_last - done) // size
            state = lax.fori_loop(
                0, n_pass, lambda i, c, done=done, size=size: tiles([done + size * i + u for u in range(size)], c, None),
                state)
            done = done + n_pass * size
        state = lax.cond(qi >= 1, lambda c: tiles([qi - 1, qi], c, col < row), lambda c: tiles([qi], c, col < row),
                         state)
        dq_ref[...] = sum(dq * (m * SB_SCALE) for (_, _, dq), m in zip(state, masks))

    qspec = pl.BlockSpec((B, LANE), lambda hp, i: (i, SB_Q0 + hp))
    wide = jax.ShapeDtypeStruct((T, SB_HEADS * SB_HEAD_DIM), F32)
    return pl.pallas_call(
        body, name="sb_bwd", grid=(SB_NB, nq),
        in_specs=[qspec,
                  pl.BlockSpec((T, LANE), lambda hp, i: (0, SB_Q0 + SB_NB + hp)),
                  pl.BlockSpec((T, LANE), lambda hp, i: (0, SB_Q0 + 2 * SB_NB + hp)),
                  qspec,
                  pl.BlockSpec((SB_PAIR, B, 1), lambda hp, i: (hp, i, 0))]
        + [pl.BlockSpec((8, LANE), lambda hp, i: (0, 0)) for _ in ordered],
        out_specs=[pl.BlockSpec((B, LANE), lambda hp, i: (i, hp)),
                   pl.BlockSpec((T, LANE), lambda hp, i: (0, hp)),
                   pl.BlockSpec((T, LANE), lambda hp, i: (0, hp))],
        out_shape=[wide, wide, wide],
        scratch_shapes=[pltpu.VMEM((T, LANE), BF16), pltpu.VMEM((T, LANE), BF16)],
        compiler_params=_cp(("parallel", "arbitrary")),
    )(proj, proj, proj, dmix, ltot, *ordered)


def _rows(w_ref, K):
    return [w_ref[i:i + 1, :] for i in range(K)]


def _conv(x, ws, t_idx):
    K = len(ws)
    y = ws[K - 1] * x
    for i in range(K - 1):
        y = y + ws[i] * _shift_down(x, K - 1 - i, t_idx)
    return y


def _conv_bwd(x, ws, dy, t_idx):
    K = len(ws)
    dx = ws[K - 1] * dy
    dws = []
    for i in range(K - 1):
        dx = dx + ws[i] * _shift_up(dy, K - 1 - i, t_idx)
        dws.append(jnp.sum(dy * _shift_down(x, K - 1 - i, t_idx), axis=0, keepdims=True))
    dws.append(jnp.sum(dy * x, axis=0, keepdims=True))
    return dx, dws


def _store_rows(ref, rows):
    for i, r in enumerate(rows):
        ref[i:i + 1, :] = r


CONV_ROWS = 64


def _ffn_act_fwd(up, conv_w, name):
    T = up.shape[0]
    F = FFN_DIM
    nb = F // LANE

    def body(g_ref, v_ref, wg_ref, wv_ref, o_ref):
        t_idx = lax.broadcasted_iota(jnp.int32, (T, LANE), 0)
        cg = _conv(g_ref[...].astype(F32), _rows(wg_ref, FFN_CONV), t_idx)
        cv = _conv(v_ref[...].astype(F32), _rows(wv_ref, FFN_CONV), t_idx)
        o_ref[...] = (_silu(cg) * cv).astype(BF16)

    return pl.pallas_call(
        body, name=name, grid=(nb,),
        in_specs=[pl.BlockSpec((T, LANE), lambda j: (0, j)), pl.BlockSpec((T, LANE), lambda j: (0, j + nb)),
                  pl.BlockSpec((FFN_CONV, LANE), lambda j: (0, j)),
                  pl.BlockSpec((FFN_CONV, LANE), lambda j: (0, j + nb))],
        out_specs=pl.BlockSpec((T, LANE), lambda j: (0, j)),
        out_shape=jax.ShapeDtypeStruct((T, F), BF16),
        compiler_params=_cp(("parallel",)),
    )(up, up, conv_w, conv_w)


def _ffn_act_bwd(up, conv_w, dact, name):
    T = up.shape[0]
    F = FFN_DIM
    nb = F // LANE

    K = FFN_CONV
    R = CONV_ROWS
    assert T % R == 0, T
    n_chunks = T // R
    PAD = 8

    def body(g_ref, v_ref, wg_ref, wv_ref, da_ref, dup_ref, dwg_ref, dwv_ref, xg_s, xv_s, dyg_s, dyv_s):
        zeros = jnp.zeros((PAD, LANE), F32)
        for s in (xg_s, xv_s, dyg_s, dyv_s):
            s[0:PAD, :] = zeros
            s[T + PAD:T + 2 * PAD, :] = zeros
        xg_s[PAD:T + PAD, :] = g_ref[...].astype(F32)
        xv_s[PAD:T + PAD, :] = v_ref[...].astype(F32)
        wg, wv = _rows(wg_ref, K), _rows(wv_ref, K)

        def window(ext, shift):
            if shift == 0:
                return ext[PAD:PAD + R, :]
            return pltpu.roll(ext, shift % (R + 2 * PAD), 0)[PAD:PAD + R, :]

        def forward(c, carry):
            r0 = pl.multiple_of(c * R, R)
            ge, ve = xg_s[pl.ds(r0, R + 2 * PAD), :], xv_s[pl.ds(r0, R + 2 * PAD), :]
            gw = [window(ge, K - 1 - i) for i in range(K)]
            vw = [window(ve, K - 1 - i) for i in range(K)]
            cg = sum(w * x for w, x in zip(wg, gw))
            cv = sum(w * x for w, x in zip(wv, vw))
            da = da_ref[pl.ds(r0, R), :].astype(F32)
            sg = _sigmoid(cg)
            dyg = da * cv * (sg * (1.0 + cg * (1.0 - sg)))
            dyv = da * (cg * sg)
            dyg_s[pl.ds(pl.multiple_of(r0 + PAD, PAD), R), :] = dyg
            dyv_s[pl.ds(pl.multiple_of(r0 + PAD, PAD), R), :] = dyv
            return tuple(acc + jnp.sum(dy * x, axis=0, keepdims=True)
                         for acc, (dy, x) in zip(carry, [(dyg, x) for x in gw] + [(dyv, x) for x in vw]))

        sums = lax.fori_loop(0, n_chunks, forward, (jnp.zeros((1, LANE), F32),) * (2 * K))
        _store_rows(dwg_ref, sums[:K])
        _store_rows(dwv_ref, sums[K:])

        def backward(c, carry):
            r0 = pl.multiple_of(c * R, R)
            ge, ve = dyg_s[pl.ds(r0, R + 2 * PAD), :], dyv_s[pl.ds(r0, R + 2 * PAD), :]
            dxg = sum(w * window(ge, -(K - 1 - i)) for i, w in enumerate(wg))
            dxv = sum(w * window(ve, -(K - 1 - i)) for i, w in enumerate(wv))
            dup_ref[0, pl.ds(r0, R), :] = dxg.astype(BF16)
            dup_ref[1, pl.ds(r0, R), :] = dxv.astype(BF16)
            return carry

        lax.fori_loop(0, n_chunks, backward, 0)

    col = pl.BlockSpec((T, LANE), lambda j: (0, j))
    wcol = pl.BlockSpec((FFN_CONV, LANE), lambda j: (0, j))
    return pl.pallas_call(
        body, name=name, grid=(nb,),
        in_specs=[col, pl.BlockSpec((T, LANE), lambda j: (0, j + nb)), wcol,
                  pl.BlockSpec((FFN_CONV, LANE), lambda j: (0, j + nb)), col],
        out_specs=[pl.BlockSpec((2, T, LANE), lambda j: (0, 0, j)), wcol, wcol],
        out_shape=[jax.ShapeDtypeStruct((2, T, F), BF16),
                   jax.ShapeDtypeStruct((FFN_CONV, F), F32), jax.ShapeDtypeStruct((FFN_CONV, F), F32)],
        scratch_shapes=[pltpu.VMEM((T + 2 * PAD, LANE), F32)] * 4,
        compiler_params=_cp(("parallel",)),
    )(up, up, conv_w, conv_w, dact)


N_QK_BLOCKS = 2 * GDN_HEADS


def _gdn_pre_fwd(proj, conv_w):
    T = proj.shape[0]
    nb = 3 * GDN_HEADS

    def body(x_ref, w_ref, o_ref):
        j = pl.program_id(0)
        t_idx = lax.broadcasted_iota(jnp.int32, (T, LANE), 0)
        s = _silu(_conv(x_ref[...], _rows(w_ref, GDN_CONV), t_idx))
        rn = lax.rsqrt(jnp.sum(s * s, axis=-1, keepdims=True) + EPS)
        o_ref[...] = s * jnp.where(j < N_QK_BLOCKS, rn, 1.0)

    return pl.pallas_call(
        body, name="gdn_pre_fwd", grid=(nb,),
        in_specs=[pl.BlockSpec((T, LANE), lambda j: (0, j)), pl.BlockSpec((GDN_CONV, LANE), lambda j: (0, j))],
        out_specs=pl.BlockSpec((T, LANE), lambda j: (0, j)),
        out_shape=jax.ShapeDtypeStruct((T, nb * LANE), F32),
        compiler_params=_cp(("parallel",)),
    )(proj, conv_w)


def _gdn_pre_bwd(proj, conv_w, dout):
    T = proj.shape[0]
    nb = 3 * GDN_HEADS
    H = GDN_HEADS

    def body(x_ref, w_ref, do_ref, dx_ref, dw_ref):
        j = pl.program_id(0)
        t_idx = lax.broadcasted_iota(jnp.int32, (T, LANE), 0)
        x, w = x_ref[...], _rows(w_ref, GDN_CONV)
        c = _conv(x, w, t_idx)
        s = _silu(c)
        rn = lax.rsqrt(jnp.sum(s * s, axis=-1, keepdims=True) + EPS)
        do = do_ref[...]
        y = s * rn
        ds_normed = rn * (do - y * jnp.sum(do * y, axis=-1, keepdims=True))
        ds = jnp.where(j < N_QK_BLOCKS, ds_normed, do)
        dx, dw = _conv_bwd(x, w, ds * _dsilu(c), t_idx)
        dx_ref[...] = dx.astype(BF16)
        _store_rows(dw_ref, dw)

    col = pl.BlockSpec((T, LANE), lambda j: (0, j))
    wcol = pl.BlockSpec((GDN_CONV, LANE), lambda j: (0, j))
    return pl.pallas_call(
        body, name="gdn_pre_bwd", grid=(nb,),
        in_specs=[col, wcol, pl.BlockSpec((None, None, T, LANE), lambda j: (j // H, j % H, 0, 0))],
        out_specs=[col, wcol],
        out_shape=[jax.ShapeDtypeStruct((T, nb * LANE), BF16), jax.ShapeDtypeStruct((GDN_CONV, nb * LANE), F32)],
        compiler_params=_cp(("parallel",)),
    )(proj, conv_w, dout)


def _gdn_consts():
    C = GDN_CHUNK
    r = lax.broadcasted_iota(jnp.int32, (C, C), 0)
    c = lax.broadcasted_iota(jnp.int32, (C, C), 1)
    return dict(incl=r >= c, strict=r > c, eye=(r == c).astype(F32),
                low=(r >= c).astype(BF16), up=(r <= c).astype(BF16), ones=jnp.ones((C, C), BF16))


def _unit_lower_inverse_raw(a_mat, eye):
    inv = eye - a_mat
    pw = _dot3_raw(a_mat, a_mat, "nn")
    n_factors = a_mat.shape[-1].bit_length() - 2
    for f in range(n_factors):
        inv = inv + _dot3_raw(inv, pw, "nn")
        if f < n_factors - 1:
            pw = _dot3_raw(pw, pw, "nn")
    return inv


@jax.custom_vjp
def _unit_lower_inverse(a_mat, eye):
    return _unit_lower_inverse_raw(a_mat, eye)


def _unit_lower_inverse_fwd(a_mat, eye):
    inv = _unit_lower_inverse_raw(a_mat, eye)
    return inv, (inv, eye)


def _unit_lower_inverse_bwd(res, g):
    inv, eye = res
    return -_dot(_dot(inv, g, "tn"), inv, "nt"), jnp.zeros_like(eye)


_unit_lower_inverse.defvjp(_unit_lower_inverse_fwd, _unit_lower_inverse_bwd)


def _gdn_prep_chunk(q, k, v, b, a, alog, dtb, cs):
    n, C, dk = q.shape
    beta = _sigmoid(b)
    g = -jnp.exp(alog) * _softplus(a + dtb)
    g_sq = jnp.broadcast_to(g, (n, C, C))
    g_wide = jnp.broadcast_to(g, (n, C, dk))
    gc_i = _m01_left(cs["low"], cs["up"], g_sq)
    gc_j = _m01_left(cs["ones"], cs["ones"], g_sq * cs["up"].astype(F32))
    gc_wide = _m01_left(cs["low"], cs["up"], g_wide)
    gl_wide = _m01_left(cs["ones"], cs["ones"], g_wide)
    decay = jnp.where(cs["incl"], jnp.exp(jnp.where(cs["incl"], gc_i - gc_j, 0.0)), 0.0)
    egc = jnp.exp(gc_wide)
    qs = q * (dk ** -0.5)
    k_beta = k * beta
    a_mat = jnp.where(cs["strict"], _dot1_nt(k_beta, k) * decay, 0.0)
    inv = _unit_lower_inverse(a_mat, cs["eye"])
    u = _dot3(inv, v * beta)
    w = _dot3(inv, k_beta * egc)
    qk = _dot1_nt(qs, k) * decay
    q_dec = qs * egc
    k_dec = k * jnp.exp(gl_wide - gc_wide)
    g_last = jnp.exp(gl_wide)[:, 0:8, :]
    return qk, u, w, q_dec, k_dec, g_last


GDN_PREP_CHUNKS = 16
GDN_BA_BLOCK = 4 * D_MODEL // LANE


def _gdn_prep_specs(T):
    C, dk = GDN_CHUNK, GDN_HEAD_DIM
    npc = min(GDN_PREP_CHUNKS, T // C)
    tc = npc * C
    assert T % tc == 0, (T, tc)
    H = GDN_HEADS
    in_specs = [pl.BlockSpec((tc, dk), lambda i, h: (i, h)),
                pl.BlockSpec((tc, dk), lambda i, h: (i, H + h)),
                pl.BlockSpec((tc, dk), lambda i, h: (i, 2 * H + h)),
                pl.BlockSpec((tc, dk), lambda i, h: (i, GDN_BA_BLOCK)),
                pl.BlockSpec((8, dk), lambda i, h: (0, 0))]
    xs_specs = [pl.BlockSpec((1, tc, C), lambda i, h: (h, i, 0)),
                pl.BlockSpec((1, tc, dk), lambda i, h: (h, i, 0)),
                pl.BlockSpec((1, tc, dk), lambda i, h: (h, i, 0)),
                pl.BlockSpec((1, tc, dk), lambda i, h: (h, i, 0)),
                pl.BlockSpec((1, tc, dk), lambda i, h: (h, i, 0)),
                pl.BlockSpec((1, npc * 8, dk), lambda i, h: (h, i, 0))]
    xs_shapes = [jax.ShapeDtypeStruct((H, T, C), F32)] + [jax.ShapeDtypeStruct((H, T, dk), F32)] * 4 + [
        jax.ShapeDtypeStruct((H, 8 * T // C, dk), F32)]
    return npc, tc, in_specs, xs_specs, xs_shapes


def _lane_pick(x, lane, j):
    return jnp.sum(jnp.where(lane == j, x, 0.0), axis=1, keepdims=True)


def _gdn_head_gates(ba_ref, gates_ref, h, npc):
    lane = lax.broadcasted_iota(jnp.int32, (1, GDN_HEAD_DIM), 1)
    ba = ba_ref[...]
    b = _lane_pick(ba, lane, h).reshape(npc, GDN_CHUNK, 1)
    a = _lane_pick(ba, lane, GDN_HEADS + h).reshape(npc, GDN_CHUNK, 1)
    return b, a, _lane_pick(gates_ref[0:1, :], lane, h), _lane_pick(gates_ref[1:2, :], lane, h), lane


def _gdn_prep_fwd(qkv, proj, gates):
    T = qkv.shape[0]
    C = GDN_CHUNK
    npc, tc, in_specs, xs_specs, xs_shapes = _gdn_prep_specs(T)

    def body(q_ref, k_ref, v_ref, ba_ref, gates_ref, qk_ref, u_ref, w_ref, qd_ref, kd_ref, gl_ref):
        cs = _gdn_consts()
        b, a, alog, dtb, _ = _gdn_head_gates(ba_ref, gates_ref, pl.program_id(1), npc)

        def chunks(val):
            return val.reshape(npc, C, val.shape[-1])

        outs = _gdn_prep_chunk(chunks(q_ref[...]), chunks(k_ref[...]), chunks(v_ref[...]), b, a, alog, dtb, cs)
        for ref, val in zip((qk_ref, u_ref, w_ref, qd_ref, kd_ref), outs[:5]):
            ref[0] = val.reshape(tc, val.shape[-1])
        gl_ref[0] = outs[5].reshape(npc * 8, outs[5].shape[-1])

    return pl.pallas_call(
        body, name="gdn_prep_fwd", grid=(T // tc, GDN_HEADS),
        in_specs=in_specs, out_specs=xs_specs, out_shape=xs_shapes,
        compiler_params=_cp(("parallel", "parallel")),
    )(qkv, qkv, qkv, proj, gates)


def _gdn_prep_bwd(qkv, proj, gates, dxs):
    T = qkv.shape[0]
    C, dk, H = GDN_CHUNK, GDN_HEAD_DIM, GDN_HEADS
    npc, tc, in_specs, xs_specs, _ = _gdn_prep_specs(T)

    def body(q_ref, k_ref, v_ref, ba_ref, gates_ref, dqk_ref, du_ref, dw_ref, dqd_ref, dkd_ref, dgl_ref,
             dqkv_ref, dba_ref, dgates_ref):
        i, h = pl.program_id(0), pl.program_id(1)
        cs = _gdn_consts()
        r8 = lax.broadcasted_iota(jnp.int32, (8, dk), 0)
        c8 = lax.broadcasted_iota(jnp.int32, (8, dk), 1)
        first = (r8 == 0) & (c8 == 0)

        @pl.when((i == 0) & (h == 0))
        def _():
            dgates_ref[...] = jnp.zeros_like(dgates_ref)

        @pl.when(h == 0)
        def _():
            dba_ref[...] = jnp.zeros_like(dba_ref)

        def chunks(val):
            return val.reshape(npc, C, val.shape[-1])

        b, a, alog, dtb, lane = _gdn_head_gates(ba_ref, gates_ref, h, npc)
        prim = (chunks(q_ref[...]), chunks(k_ref[...]), chunks(v_ref[...]), b, a, alog, dtb)
        _, vjp = jax.vjp(lambda *p: _gdn_prep_chunk(*p, cs), *prim)
        dgl = jnp.where(first, dgl_ref[0].reshape(npc, 8, dk), 0.0)
        cts = tuple(chunks(r[0]) for r in (dqk_ref, du_ref, dw_ref, dqd_ref, dkd_ref)) + (dgl,)
        dq, dkk, dv, db, da, dal, ddt = vjp(cts)
        for part, val in enumerate((dq, dkk, dv)):
            dqkv_ref[part, 0] = val.reshape(tc, dk)
        dba_ref[...] += (jnp.where(lane == h, db.reshape(tc, 1), 0.0)
                         + jnp.where(lane == H + h, da.reshape(tc, 1), 0.0))
        dgates_ref[0:1, :] += jnp.where(lane == h, dal, 0.0)
        dgates_ref[1:2, :] += jnp.where(lane == h, ddt, 0.0)

    return pl.pallas_call(
        body, name="gdn_prep_bwd", grid=(T // tc, H),
        in_specs=in_specs + xs_specs,
        out_specs=[pl.BlockSpec((3, 1, tc, dk), lambda i, h: (0, h, i, 0)), pl.BlockSpec((tc, dk), lambda i, h: (i, 0)),
                   pl.BlockSpec((8, dk), lambda i, h: (0, 0))],
        out_shape=[jax.ShapeDtypeStruct((3, H, T, dk), F32), jax.ShapeDtypeStruct((T, dk), F32),
                   jax.ShapeDtypeStruct((8, dk), F32)],
        compiler_params=_cp(("arbitrary", "arbitrary")),
    )(qkv, qkv, qkv, proj, gates, *dxs)


GDN_SCAN_CHUNKS = 2


def _gdn_scan_specs(T):
    C, dk, H = GDN_CHUNK, GDN_HEAD_DIM, GDN_HEADS
    return [pl.BlockSpec((H, C, C), lambda n: (0, n, 0))] + [pl.BlockSpec((H, C, dk), lambda n: (0, n, 0))] * 4 + [
        pl.BlockSpec((H, 8, dk), lambda n: (0, n, 0))]


def _gdn_scan_fwd(xs):
    H, T, dk = xs[1].shape
    C = GDN_CHUNK
    n = T // C

    def body(qk_ref, u_ref, w_ref, qd_ref, kd_ref, gl_ref, o_ref, s_ref, state):
        c = pl.program_id(0)

        @pl.when(c == 0)
        def _():
            state[...] = jnp.zeros_like(state)

        S = state[...]
        for j in range(G):
            rows = slice(j * C, (j + 1) * C)
            s_ref[j] = S
            v_new = u_ref[:, rows, :] - _bdot(w_ref[:, rows, :], S)
            o_ref[:, rows, :] = _bdot(qd_ref[:, rows, :], S) + _bdot(qk_ref[:, rows, :], v_new)
            S = (S * jnp.tile(gl_ref[:, j * 8:(j + 1) * 8, :], (1, dk // 8, 1))
                 + _bdot(kd_ref[:, rows, :], v_new, "tn"))
        state[...] = S

    G = GDN_SCAN_CHUNKS if n % GDN_SCAN_CHUNKS == 0 else 1
    return pl.pallas_call(
        body, name="gdn_scan_fwd", grid=(n // G,),
        in_specs=[pl.BlockSpec((H, G * C, C), lambda m: (0, m, 0))] + [pl.BlockSpec((H, G * C, dk), lambda m: (0, m, 0))] * 4
        + [pl.BlockSpec((H, G * 8, dk), lambda m: (0, m, 0))],
        out_specs=[pl.BlockSpec((H, G * C, dk), lambda m: (0, m, 0)),
                   pl.BlockSpec((G, H, dk, dk), lambda m: (m, 0, 0, 0))],
        out_shape=[jax.ShapeDtypeStruct((H, T, dk), F32), jax.ShapeDtypeStruct((n, H, dk, dk), F32)],
        scratch_shapes=[pltpu.VMEM((H, dk, dk), F32)],
        compiler_params=_cp(("arbitrary",)),
    )(*xs)


def _gdn_scan_bwd(xs, states, do):
    H, T, dk = xs[1].shape
    C = GDN_CHUNK
    n = T // C

    def rev(spec_shape, f):
        return pl.BlockSpec(spec_shape, lambda i: f(n - 1 - i))

    def body(qk_ref, u_ref, w_ref, qd_ref, kd_ref, gl_ref, s_ref, do_ref,
             dqk_ref, du_ref, dw_ref, dqd_ref, dkd_ref, dgl_ref, dstate):
        i = pl.program_id(0)

        @pl.when(i == 0)
        def _():
            dstate[...] = jnp.zeros_like(dstate)

        S = s_ref[0]
        dS = dstate[...]
        do_v = do_ref[...]
        qk, w, qd, kd = qk_ref[...], w_ref[...], qd_ref[...], kd_ref[...]
        v_new = u_ref[...] - _bdot(w, S)
        dv_new = _bdot(qk, do_v, "tn") + _bdot(kd, dS)
        dqk_ref[...] = _bdot(do_v, v_new, "nt")
        dqd_ref[...] = _bdot(do_v, S, "nt")
        dkd_ref[...] = _bdot(v_new, dS, "nt")
        du_ref[...] = dv_new
        dw_ref[...] = -_bdot(dv_new, S, "nt")
        dgl = jnp.sum(jnp.sum(S * dS, axis=2, keepdims=True), axis=1, keepdims=True)
        dgl_ref[...] = jnp.broadcast_to(dgl, dgl_ref.shape)
        dstate[...] = (dS * jnp.tile(gl_ref[...], (1, dk // 8, 1)) + _bdot(qd, do_v, "tn")
                       - _bdot(w, dv_new, "tn"))

    in_specs = [rev((H, C, C), lambda m: (0, m, 0))] + [rev((H, C, dk), lambda m: (0, m, 0))] * 4 + [
        rev((H, 8, dk), lambda m: (0, m, 0)), rev((1, H, dk, dk), lambda m: (m, 0, 0, 0)),
        rev((H, C, dk), lambda m: (0, m, 0))]
    out_specs = [rev((H, C, C), lambda m: (0, m, 0))] + [rev((H, C, dk), lambda m: (0, m, 0))] * 4 + [
        rev((H, 8, dk), lambda m: (0, m, 0))]
    out_shape = [jax.ShapeDtypeStruct((H, T, C), F32)] + [jax.ShapeDtypeStruct((H, T, dk), F32)] * 4 + [
        jax.ShapeDtypeStruct((H, 8 * n, dk), F32)]
    return pl.pallas_call(
        body, name="gdn_scan_bwd", grid=(n,),
        in_specs=in_specs, out_specs=out_specs, out_shape=out_shape,
        scratch_shapes=[pltpu.VMEM((H, dk, dk), F32)],
        compiler_params=_cp(("arbitrary",)),
    )(*xs, states, do)


def _gdn_post_fwd(o, proj, norm_w):
    H, T, dk = o.shape
    tt = _tile(T, 1024)
    zoff = 3 * GDN_HEADS

    def body(o_ref, z_ref, g_ref, y_ref):
        ov = o_ref[0]
        r = lax.rsqrt(jnp.mean(ov * ov, axis=-1, keepdims=True) + EPS)
        y_ref[...] = (ov * r * g_ref[...] * _silu(z_ref[...])).astype(BF16)

    return pl.pallas_call(
        body, name="gdn_post_fwd", grid=(H, T // tt),
        in_specs=[pl.BlockSpec((1, tt, dk), lambda h, i: (h, i, 0)), pl.BlockSpec((tt, dk), lambda h, i: (i, zoff + h)),
                  pl.BlockSpec((1, dk), lambda h, i: (0, 0))],
        out_specs=pl.BlockSpec((tt, dk), lambda h, i: (i, h)),
        out_shape=jax.ShapeDtypeStruct((T, H * dk), BF16),
        compiler_params=_cp(("parallel", "parallel")),
    )(o, proj, norm_w)


def _gdn_post_bwd(o, proj, norm_w, dy):
    H, T, dk = o.shape
    tt = _tile(T, 1024)
    zoff = 3 * GDN_HEADS

    def body(o_ref, z_ref, g_ref, dy_ref, do_ref, dz_ref, dg_ref):
        i = pl.program_id(1)
        ov, z, g, dyv = o_ref[0], z_ref[...], g_ref[...], dy_ref[...]
        r = lax.rsqrt(jnp.mean(ov * ov, axis=-1, keepdims=True) + EPS)
        on = ov * r
        sz = _silu(z)
        dz_ref[...] = (dyv * on * g * _dsilu(z)).astype(BF16)
        dn = dyv * sz
        gdn = dn * g
        do_ref[0] = r * (gdn - on * jnp.mean(gdn * on, axis=-1, keepdims=True))

        @pl.when(i == 0)
        def _():
            dg_ref[...] = jnp.zeros_like(dg_ref)

        dg_ref[0] += jnp.sum(dn * on, axis=0, keepdims=True)

    return pl.pallas_call(
        body, name="gdn_post_bwd", grid=(H, T // tt),
        in_specs=[pl.BlockSpec((1, tt, dk), lambda h, i: (h, i, 0)), pl.BlockSpec((tt, dk), lambda h, i: (i, zoff + h)),
                  pl.BlockSpec((1, dk), lambda h, i: (0, 0)), pl.BlockSpec((tt, dk), lambda h, i: (i, h))],
        out_specs=[pl.BlockSpec((1, tt, dk), lambda h, i: (h, i, 0)), pl.BlockSpec((tt, dk), lambda h, i: (i, h)),
                   pl.BlockSpec((1, 1, dk), lambda h, i: (h, 0, 0))],
        out_shape=[jax.ShapeDtypeStruct((H, T, dk), F32), jax.ShapeDtypeStruct((T, H * dk), BF16),
                   jax.ShapeDtypeStruct((H, 1, dk), F32)],
        compiler_params=_cp(("parallel", "arbitrary")),
    )(o, proj, norm_w, dy)


def _ple_fwd(x, pp, gl, name):
    T, D = x.shape
    tt = _tile(T, 512)

    def body(x_ref, p_ref, g_ref, o_ref):
        o_ref[...] = x_ref[...] + p_ref[...] * _sigmoid(g_ref[...])

    row = pl.BlockSpec((tt, D), lambda i: (i, 0))
    return pl.pallas_call(
        body, name=name, grid=(T // tt,), in_specs=[row, row, row], out_specs=row,
        out_shape=jax.ShapeDtypeStruct((T, D), F32), compiler_params=_cp(("parallel",)),
    )(x, pp, gl)


def _ple_bwd(dx, pp, gl, name, after=None):
    T, D = dx.shape
    tt = _tile(T, 512)

    def body(dx_ref, p_ref, g_ref, *rest):
        dp_ref, dg_ref = rest[-2:]
        s = _sigmoid(g_ref[...])
        dxv = dx_ref[...]
        dp_ref[...] = (dxv * s).astype(BF16)
        dg_ref[...] = (dxv * p_ref[...] * s * (1.0 - s)).astype(BF16)

    row = pl.BlockSpec((tt, D), lambda i: (i, 0))
    ordered = [] if after is None else [after]
    return pl.pallas_call(
        body, name=name, grid=(T // tt,),
        in_specs=[row, row, row] + [pl.BlockSpec((8, LANE), lambda i: (0, 0)) for _ in ordered], out_specs=[row, row],
        out_shape=[jax.ShapeDtypeStruct((T, D), BF16)] * 2, compiler_params=_cp(("parallel",)),
    )(dx, pp, gl, *ordered)


UP_SHARD = 2 * FFN_DIM // N_CHIPS
DOWN_SHARD = FFN_DIM // N_CHIPS
GATE_SHARD = D_MODEL // N_CHIPS
IN_E_SHARD = 2 * D_MODEL // N_CHIPS


def _ffn_fwd(x, norm, W, conv_w, l):
    T = x.shape[0]
    hf = _rms_fwd(x, norm, f"ffn_norm{l}")
    up = _mm(hf, W["w_up"], "nn", f"ffn_up{l}", dims=(T, 2 * FFN_DIM, D_MODEL), b_view=_cols_of(1, l), tn=UP_SHARD,
             out_dtype=BF16)
    act = _ffn_act_fwd(up, conv_w, f"ffn_act{l}")
    x_out = _mm(act, W["w_down"], "nn", f"ffn_down{l}", dims=(T, D_MODEL, FFN_DIM), b_view=_layer_of(l), res=x,
                tn=1024, tk=1408)
    return x_out, (x, hf, up, act)


def _ffn_bwd(dx_out, saved, norm, W, conv_w, l):
    x, hf, up, act = saved
    T = x.shape[0]
    dact = _mm(dx_out, W["w_down"], "nt", f"ffn_dact{l}", dims=(T, FFN_DIM, D_MODEL), b_view=_layer_of(l),
               out_dtype=BF16, tn=1408)
    dw_down = _mm(act, dx_out, "tn", f"ffn_dwdown{l}", tm=1408, tn=1024, tk=1024)
    dup, dcw_g, dcw_v = _ffn_act_bwd(up, conv_w, dact, f"ffn_dact_conv{l}")
    dw_up = _mm(hf, dup, "tn", f"ffn_dwup{l}", dims=(D_MODEL, 2 * FFN_DIM, T), b_view=_cols_of(FFN_DIM // UP_SHARD),
                o_view=_cols_of(1), out_shape=(N_CHIPS, D_MODEL, UP_SHARD), tm=1024, tn=UP_SHARD, tk=1024)
    dhf = _mm(dup, W["w_up"], "nt", f"ffn_dhf{l}", dims=(T, D_MODEL, 2 * FFN_DIM),
              a_view=_cols_of(FFN_DIM // UP_SHARD), b_view=_cols_of(1, l), tn=1024, tk=UP_SHARD)
    dx, dnorm = _rms_bwd(x, norm, dhf, dx_out, f"ffn_dnorm{l}")
    return (dx, dnorm, dw_up, jnp.concatenate([dcw_g, dcw_v], axis=1),
            dw_down.reshape(N_CHIPS, DOWN_SHARD, D_MODEL))


def _ple_layer_fwd(x, p, norm, W, l):
    T = x.shape[0]
    hg = _rms_fwd(x, norm, f"ple_norm{l}")
    gl = _mm(hg, W["w_ple_gate"], "nn", f"ple_gate{l}", dims=(T, D_MODEL, D_MODEL), b_view=_layer_of(l), tn=1024)
    pp = _mm(p, W["w_ple"], "nn", f"ple_proj{l}", dims=(T, D_MODEL, PLE_DIM), a_view=_layer_of(l),
             b_view=_layer_of(l), tn=1024)
    return _ple_fwd(x, pp, gl, f"ple_mix{l}"), (x, hg, gl, pp)


def _ple_layer_bwd(dx_out, saved, p, norm, W, l, after=None):
    x, hg, gl, pp = saved
    T = x.shape[0]
    dpp, dgl = _ple_bwd(dx_out, pp, gl, f"ple_dmix{l}", after)
    dw_ple = _mm(p, dpp, "tn", f"ple_dwple{l}", dims=(PLE_DIM, D_MODEL, T), a_view=_layer_of(l),
                 tm=PLE_DIM, tn=1024, tk=1024)
    dw_ple = dw_ple.reshape(PLE_DIM, N_CHIPS, PLE_DIM).transpose(1, 0, 2)
    dw_gate = _mm(hg, dgl, "tn", f"ple_dwgate{l}", tm=1024, tn=1024, tk=1024)
    dhg = _mm(dgl, W["w_ple_gate"], "nt", f"ple_dhg{l}", dims=(T, D_MODEL, D_MODEL), b_view=_layer_of(l), tn=1024)
    dx, dnorm = _rms_bwd(x, norm, dhg, dx_out, f"ple_dnorm{l}")
    return dx, dnorm, dw_gate.reshape(N_CHIPS, GATE_SHARD, D_MODEL), dw_ple


def _local_step(x, p, target, W, token=None, later_weights=None, on_grads=None):
    T = x.shape[0]
    H = GDN_HEADS
    G = {}

    hn_e = _rms_fwd(x, W["mix_norm_e"], "mix_norm_e", after=token)
    proj_e = _mm(hn_e, W["w_in_e"], "nn", "in_e", dims=(T, 2 * D_MODEL, D_MODEL), b_view=_cols_of(1), tn=IN_E_SHARD)
    pool_out = _pool_fwd(proj_e, W["pool_w"], W["pool_scale"])
    attn, ltot = _sb_fwd(proj_e)
    if later_weights is not None:
        W = {**W, **later_weights(attn)}
    mix_e = jnp.concatenate([pool_out, attn], axis=1).astype(BF16)
    x1 = _mm(mix_e, W["w_out_e"], "nn", "out_e", res=x, tn=1024)
    x2, ffn0 = _ffn_fwd(x1, W["ffn_norm"][0:1], W, W["ffn_conv"][0], 0)
    x3, ple0 = _ple_layer_fwd(x2, p, W["ple_norm"][0:1], W, 0)

    hn_o = _rms_fwd(x3, W["mix_norm_o"], "mix_norm_o")
    proj_o = _mm(hn_o, W["w_in_o"], "nn", "in_o", tn=1408)
    qkv = _gdn_pre_fwd(proj_o, W["conv_qkv_o"])
    gates = jnp.pad(jnp.concatenate([W["a_log_o"], W["dt_bias_o"]], axis=0), ((0, 6), (0, LANE - H)))
    xs = _gdn_prep_fwd(qkv, proj_o, gates)
    o, states = _gdn_scan_fwd(xs)
    og = _gdn_post_fwd(o, proj_o, W["gdn_norm_o"])
    x4 = _mm(og, W["w_out_o"], "nn", "out_o", res=x3, tn=1024)
    x5, ffn1 = _ffn_fwd(x4, W["ffn_norm"][1:2], W, W["ffn_conv"][1], 1)
    x6, ple1 = _ple_layer_fwd(x5, p, W["ple_norm"][1:2], W, 1)

    sq, dx6, G["final_norm"] = _final_loss(x6, W["final_norm"], target, "final_loss")

    dx5, dpn1, dwg1, dwp1 = _ple_layer_bwd(dx6, ple1, p, W["ple_norm"][1:2], W, 1)
    dx4, dfn1, dwu1, dfc1, dwd1 = _ffn_bwd(dx5, ffn1, W["ffn_norm"][1:2], W, W["ffn_conv"][1], 1)
    dog = _mm(dx4, W["w_out_o"], "nt", "d_og", tn=1024)
    G["w_out_o"] = _mm(og, dx4, "tn", "dw_out_o", tm=1024, tn=1024, tk=1024).reshape(N_CHIPS, GATE_SHARD, D_MODEL)
    do, dz, dgn = _gdn_post_bwd(o, proj_o, W["gdn_norm_o"], dog)
    G["gdn_norm_o"] = jnp.sum(dgn, axis=0)
    dxs = _gdn_scan_bwd(xs, states, do)
    dqkv_act, dba, dgates = _gdn_prep_bwd(qkv, proj_o, gates, dxs)
    G["a_log_o"] = dgates[0:1, :H]
    G["dt_bias_o"] = dgates[1:2, :H]
    dqkv, G["conv_qkv_o"] = _gdn_pre_bwd(proj_o, W["conv_qkv_o"], dqkv_act)
    dproj_o = jnp.concatenate([dqkv, dz, dba.astype(BF16)], axis=1)
    G["w_in_o"] = _mm(hn_o, dproj_o, "tn", "dw_in_o", tm=1024, tn=1408, tk=1024)
    dhn_o = _mm(dproj_o, W["w_in_o"], "nt", "d_hn_o", tn=1024, tk=1408)
    dx3, G["mix_norm_o"] = _rms_bwd(x3, W["mix_norm_o"], dhn_o, dx4, "d_mix_norm_o")

    token1 = token0 = None
    if on_grads is not None:
        token1 = on_grads("second", {("w_in_o", 0): _chip_major_w_in_o(G["w_in_o"]), ("w_out_o", 0): G["w_out_o"],
                                     ("w_up", 1): dwu1, ("w_down", 1): dwd1, ("w_ple_gate", 1): dwg1,
                                     ("w_ple", 1): dwp1})
    dx2, dpn0, dwg0, dwp0 = _ple_layer_bwd(dx3, ple0, p, W["ple_norm"][0:1], W, 0, after=token1)
    dx1, dfn0, dwu0, dfc0, dwd0 = _ffn_bwd(dx2, ffn0, W["ffn_norm"][0:1], W, W["ffn_conv"][0], 0)
    if on_grads is not None:
        token0 = on_grads("first_ffn", {("w_up", 0): dwu0, ("w_down", 0): dwd0, ("w_ple_gate", 0): dwg0,
                                        ("w_ple", 0): dwp0})
    dmix = _mm(dx1, W["w_out_e"], "nt", "d_mix_e", tn=1024)
    G["w_out_e"] = _mm(mix_e, dx1, "tn", "dw_out_e", tm=1024, tn=1024, tk=1024).reshape(N_CHIPS, GATE_SHARD, D_MODEL)
    du, G["pool_w"], G["pool_scale"] = _pool_bwd(proj_e, W["pool_w"], W["pool_scale"], dmix)
    dqa, dka, dva = _sb_bwd(proj_e, dmix, ltot, after=token0)
    dproj_e = jnp.concatenate([du, dqa, dka, dva], axis=1).astype(BF16)
    G["w_in_e"] = _mm(hn_e, dproj_e, "tn", "dw_in_e", dims=(D_MODEL, 2 * D_MODEL, T), o_view=_cols_of(1),
                      out_shape=(N_CHIPS, D_MODEL, IN_E_SHARD), tm=1024, tn=IN_E_SHARD, tk=1024)
    dhn_e = _mm(dproj_e, W["w_in_e"], "nt", "d_hn_e", dims=(T, D_MODEL, 2 * D_MODEL), b_view=_cols_of(1),
                tn=1024, tk=IN_E_SHARD)
    grad_x, G["mix_norm_e"] = _rms_bwd(x, W["mix_norm_e"], dhn_e, dx1, "d_mix_norm_e")

    G["ffn_norm"] = jnp.concatenate([dfn0, dfn1], axis=0)
    G["ple_norm"] = jnp.concatenate([dpn0, dpn1], axis=0)
    G["ffn_conv"] = jnp.stack([dfc0, dfc1])
    G["w_up"] = [dwu0, dwu1]
    G["w_down"] = [dwd0, dwd1]
    G["w_ple_gate"] = [dwg0, dwg1]
    G["w_ple"] = [dwp0, dwp1]
    return sq[0, 0], grad_x, G


BIG = ("w_in_e", "w_out_e", "w_in_o", "w_out_o", "w_up", "w_down", "w_ple_gate", "w_ple")
SHARDED_SMALL = (("mix_norm_o", 1), ("conv_qkv_o", 2), ("ffn_conv", 2))
REPLICATED = ("mix_norm_e", "pool_w", "pool_scale", "a_log_o", "dt_bias_o", "gdn_norm_o", "ffn_norm", "ple_norm",
              "final_norm")
WEIGHT_ORDER = ("mix_norm_e", "w_in_e", "pool_w", "pool_scale", "w_out_e", "mix_norm_o", "w_in_o", "conv_qkv_o",
                "a_log_o", "dt_bias_o", "gdn_norm_o", "w_out_o", "ffn_norm", "w_up", "ffn_conv", "w_down", "ple_norm",
                "w_ple_gate", "w_ple", "final_norm")
SMALL_W = LANE
SMALL_ROWS = 16


def _size(shape):
    n = 1
    for s in shape:
        n *= s
    return n


def _pack(arrs, width, granule):
    flat = jnp.concatenate([a.reshape(-1) for a in arrs])
    rows = -(-flat.shape[0] // width)
    rows = -(-rows // granule) * granule
    return jnp.pad(flat, (0, rows * width - flat.shape[0])).reshape(rows, width)


def _unpack(flat2d, shapes):
    flat = flat2d.reshape(-1)
    out, off = [], 0
    for s in shapes:
        out.append(flat[off:off + _size(s)].reshape(s))
        off += _size(s)
    return out


MESH_ID = pl.DeviceIdType.MESH
HBM_SPEC = pl.BlockSpec(memory_space=pltpu.HBM)


def _where_am_i():
    return lax.axis_index("x"), lax.axis_index("y"), lax.axis_index("c")


def _other_chips(x, y):
    return [(1 - x, y), (x, 1 - y), (1 - x, 1 - y)]


def _remote(src, dst, send_sems, recv_sems, k, to):
    return pltpu.make_async_remote_copy(src_ref=src, dst_ref=dst, send_sem=send_sems.at[k], recv_sem=recv_sems.at[k],
                                        device_id=to, device_id_type=MESH_ID)


def _chip_allgather(pack, name):
    R, Wd = pack.shape
    Rh = R // 2

    def body(src_ref, out_ref, send_sems, recv_sems, local_sem):
        x, y, c = _where_am_i()
        me, sib = (x, y, c), (x, y, 1 - c)
        chips = _other_chips(x, y)
        mine_rows = pl.ds(pl.multiple_of(c * Rh, SMALL_ROWS), Rh)
        sib_rows = pl.ds(pl.multiple_of((1 - c) * Rh, SMALL_ROWS), Rh)
        j_me = 2 * x + y
        local = pltpu.make_async_copy(src_ref, out_ref.at[j_me], local_sem)
        local.start()
        first = [_remote(src_ref.at[mine_rows], out_ref.at[j_me, mine_rows], send_sems, recv_sems, k, (cx, cy, c))
                 for k, (cx, cy) in enumerate(chips)]
        for cp in first:
            cp.start()
        passed = []
        for k, (cx, cy) in enumerate(chips):
            blk = out_ref.at[2 * cx + cy, mine_rows]
            _remote(blk, blk, send_sems, recv_sems, k, me).wait_recv()
            fw = _remote(blk, blk, send_sems, recv_sems, 3 + k, sib)
            fw.start()
            passed.append(fw)
        for k, (cx, cy) in enumerate(chips):
            blk = out_ref.at[2 * cx + cy, sib_rows]
            _remote(blk, blk, send_sems, recv_sems, 3 + k, me).wait_recv()
        for cp in first + passed:
            cp.wait_send()
        local.wait()

    return pl.pallas_call(
        body, name=name, in_specs=[HBM_SPEC], out_specs=HBM_SPEC,
        out_shape=jax.ShapeDtypeStruct((N_CHIPS, R, Wd), pack.dtype),
        scratch_shapes=[pltpu.SemaphoreType.DMA((6,)), pltpu.SemaphoreType.DMA((6,)), pltpu.SemaphoreType.DMA],
    )(pack)


def _chip_allgather_many(blocks, name):
    n = len(blocks)

    def body(*refs):
        srcs, outs = refs[:n], refs[n:2 * n]
        send_sems, recv_sems = refs[2 * n:]
        x, y, c = _where_am_i()
        me, sib = (x, y, c), (x, y, 1 - c)
        chips = _other_chips(x, y)
        j_me = 2 * x + y
        first = [_remote(srcs[p].at[c], outs[p].at[j_me, c], send_sems, recv_sems, 6 * p + k, (cx, cy, c))
                 for p in range(n) for k, (cx, cy) in enumerate(chips)]
        for cp in first:
            cp.start()
        passed = []
        for k, (cx, cy) in enumerate(chips):
            for p in range(n):
                blk = outs[p].at[2 * cx + cy, c]
                _remote(blk, blk, send_sems, recv_sems, 6 * p + k, me).wait_recv()
                fw = _remote(blk, blk, send_sems, recv_sems, 6 * p + 3 + k, sib)
                fw.start()
                passed.append(fw)
        for k, (cx, cy) in enumerate(chips):
            for p in range(n):
                blk = outs[p].at[2 * cx + cy, 1 - c]
                _remote(blk, blk, send_sems, recv_sems, 6 * p + 3 + k, me).wait_recv()
        for cp in first + passed:
            cp.wait_send()

    return pl.pallas_call(
        body, name=name, in_specs=[HBM_SPEC] * n, out_specs=[HBM_SPEC] * n,
        out_shape=[jax.ShapeDtypeStruct((N_CHIPS,) + b.shape, b.dtype) for b in blocks],
        scratch_shapes=[pltpu.SemaphoreType.DMA((6 * n,)), pltpu.SemaphoreType.DMA((6 * n,))],
    )(*blocks)


SEM_SPEC = pl.BlockSpec(memory_space=pltpu.SEMAPHORE)
DATAFLOW_EFFECT = pltpu.SideEffectType.DATAFLOW_SIDE_EFFECTING


def _chip_allgather_start(blocks, name):
    n = len(blocks)

    def body(*refs):
        srcs, lands = refs[:n], refs[n:2 * n]
        send_sems, recv_sems, token = refs[2 * n], refs[2 * n + 1], refs[-1]
        x, y, c = _where_am_i()
        j_me = 2 * x + y
        for p in range(n):
            for k, (cx, cy) in enumerate(_other_chips(x, y)):
                _remote(srcs[p].at[c], lands[p].at[j_me, c], send_sems, recv_sems, 3 * p + k, (cx, cy, c)).start()
        token[...] = jnp.zeros_like(token)

    lands = [pltpu.with_memory_space_constraint(lax.empty((N_CHIPS,) + b.shape, b.dtype), pltpu.HBM) for b in blocks]
    blocks = [pltpu.with_memory_space_constraint(b, pltpu.HBM) for b in blocks]
    outs = pl.pallas_call(
        body, name=name,
        in_specs=[HBM_SPEC] * (2 * n),
        out_specs=[SEM_SPEC, SEM_SPEC] + [HBM_SPEC] * (2 * n) + [pl.BlockSpec(memory_space=pltpu.VMEM)],
        out_shape=[pltpu.SemaphoreType.DMA((3 * n,)), pltpu.SemaphoreType.DMA((3 * n,))]
        + [pltpu.HBM(a.shape, a.dtype) for a in blocks + lands] + [jax.ShapeDtypeStruct((8, LANE), F32)],
        input_output_aliases={i: 2 + i for i in range(2 * n)},
        compiler_params=pltpu.CompilerParams(has_side_effects=DATAFLOW_EFFECT),
    )(*blocks, *lands)
    return outs[0], outs[1], list(outs[2:2 + n]), list(outs[2 + n:2 + 2 * n]), outs[-1]


def _chip_allgather_wait(send_sems, recv_sems, blocks, lands, after, name):
    n = len(blocks)

    def body(*refs):
        srcs, zones = refs[:n], refs[n:2 * n]
        send, recv = refs[2 * n], refs[2 * n + 1]
        x, y, c = _where_am_i()
        for p in range(n):
            for k, (cx, cy) in enumerate(_other_chips(x, y)):
                cp = _remote(srcs[p].at[c], zones[p].at[2 * cx + cy, c], send, recv, 3 * p + k, (x, y, c))
                cp.wait_send()
                cp.wait_recv()

    outs = pl.pallas_call(
        body, name=name,
        in_specs=[HBM_SPEC] * (2 * n) + [SEM_SPEC, SEM_SPEC, pl.BlockSpec(memory_space=pl.ANY)],
        out_specs=[HBM_SPEC] * (2 * n),
        out_shape=[pltpu.HBM(a.shape, a.dtype) for a in list(blocks) + list(lands)],
        input_output_aliases={i: i for i in range(2 * n)},
        compiler_params=pltpu.CompilerParams(has_side_effects=DATAFLOW_EFFECT),
    )(*blocks, *lands, send_sems, recv_sems, after)
    return list(outs[n:])


def _chip_scatter_start(sums, name):
    n = len(sums)

    def body(*refs):
        srcs, lands = refs[:n], refs[n:2 * n]
        send_sems, recv_sems, token = refs[2 * n], refs[2 * n + 1], refs[-1]
        x, y, c = _where_am_i()
        for p in range(n):
            for k, (cx, cy) in enumerate(_other_chips(x, y)):
                _remote(srcs[p].at[2 * cx + cy], lands[p].at[k], send_sems, recv_sems, 3 * p + k, (cx, cy, c)).start()
        token[...] = jnp.zeros_like(token)

    lands = [pltpu.with_memory_space_constraint(lax.empty((N_CHIPS - 1,) + s.shape[1:], s.dtype), pltpu.HBM)
             for s in sums]
    sums = [pltpu.with_memory_space_constraint(s, pltpu.HBM) for s in sums]
    outs = pl.pallas_call(
        body, name=name,
        in_specs=[HBM_SPEC] * (2 * n),
        out_specs=[SEM_SPEC, SEM_SPEC] + [HBM_SPEC] * (2 * n) + [pl.BlockSpec(memory_space=pltpu.VMEM)],
        out_shape=[pltpu.SemaphoreType.DMA((3 * n,)), pltpu.SemaphoreType.DMA((3 * n,))]
        + [pltpu.HBM(a.shape, a.dtype) for a in sums + lands] + [jax.ShapeDtypeStruct((8, LANE), F32)],
        input_output_aliases={i: 2 + i for i in range(2 * n)},
        compiler_params=pltpu.CompilerParams(has_side_effects=DATAFLOW_EFFECT),
    )(*sums, *lands)
    return outs[0], outs[1], list(outs[2:2 + n]), list(outs[2 + n:2 + 2 * n]), outs[-1]


def _chip_scatter_wait(send_sems, recv_sems, sums, lands, after, name):
    n = len(sums)

    def body(*refs):
        srcs, zones = refs[:n], refs[n:2 * n]
        send, recv = refs[2 * n], refs[2 * n + 1]
        x, y, c = _where_am_i()
        for p in range(n):
            for k, (cx, cy) in enumerate(_other_chips(x, y)):
                cp = _remote(srcs[p].at[2 * cx + cy], zones[p].at[k], send, recv, 3 * p + k, (x, y, c))
                cp.wait_send()
                cp.wait_recv()

    outs = pl.pallas_call(
        body, name=name,
        in_specs=[HBM_SPEC] * (2 * n) + [SEM_SPEC, SEM_SPEC, pl.BlockSpec(memory_space=pl.ANY)],
        out_specs=[HBM_SPEC] * (2 * n),
        out_shape=[pltpu.HBM(a.shape, a.dtype) for a in list(sums) + list(lands)],
        input_output_aliases={i: i for i in range(2 * n)},
        compiler_params=pltpu.CompilerParams(has_side_effects=DATAFLOW_EFFECT),
    )(*sums, *lands, send_sems, recv_sems, after)
    return list(outs[n:])


def _chip_allgather_forward(lands, name):
    n = len(lands)

    def body(*refs):
        ins, outs = refs[:n], refs[n:2 * n]
        send_sems, recv_sems = refs[2 * n:]
        x, y, c = _where_am_i()
        me, sib = (x, y, c), (x, y, 1 - c)
        chips = _other_chips(x, y)
        passed = [_remote(ins[p].at[2 * cx + cy, c], outs[p].at[2 * cx + cy, c], send_sems, recv_sems, 3 * p + k, sib)
                  for p in range(n) for k, (cx, cy) in enumerate(chips)]
        for cp in passed:
            cp.start()
        for p in range(n):
            for k, (cx, cy) in enumerate(chips):
                blk = outs[p].at[2 * cx + cy, 1 - c]
                _remote(blk, blk, send_sems, recv_sems, 3 * p + k, me).wait_recv()
        for cp in passed:
            cp.wait_send()

    return pl.pallas_call(
        body, name=name, in_specs=[HBM_SPEC] * n, out_specs=[HBM_SPEC] * n,
        out_shape=[jax.ShapeDtypeStruct(a.shape, a.dtype) for a in lands],
        input_output_aliases={i: i for i in range(n)},
        scratch_shapes=[pltpu.SemaphoreType.DMA((3 * n,)), pltpu.SemaphoreType.DMA((3 * n,))],
    )(*lands)


def _sibling_swap_many(pieces, name):
    n = len(pieces)

    def body(*refs):
        srcs, outs = refs[:n], refs[n:2 * n]
        send_sems, recv_sems = refs[2 * n:]
        x, y, c = _where_am_i()
        cps = [_remote(srcs[p].at[:, 1 - c], outs[p], send_sems, recv_sems, p, (x, y, 1 - c)) for p in range(n)]
        for cp in cps:
            cp.start()
        for cp in cps:
            cp.wait()

    return pl.pallas_call(
        body, name=name, in_specs=[HBM_SPEC] * n, out_specs=[HBM_SPEC] * n,
        out_shape=[jax.ShapeDtypeStruct((g.shape[0],) + g.shape[2:], g.dtype) for g in pieces],
        scratch_shapes=[pltpu.SemaphoreType.DMA((n,)), pltpu.SemaphoreType.DMA((n,))],
    )(*pieces)


def _chip_scatter_many(sums, name):
    n = len(sums)

    def body(*refs):
        srcs, outs = refs[:n], refs[n:2 * n]
        send_sems, recv_sems = refs[2 * n:]
        x, y, c = _where_am_i()
        cps = [_remote(srcs[p].at[2 * cx + cy], outs[p].at[k], send_sems, recv_sems, 3 * p + k, (cx, cy, c))
               for p in range(n) for k, (cx, cy) in enumerate(_other_chips(x, y))]
        for cp in cps:
            cp.start()
        for cp in cps:
            cp.wait()

    return pl.pallas_call(
        body, name=name, in_specs=[HBM_SPEC] * n, out_specs=[HBM_SPEC] * n,
        out_shape=[jax.ShapeDtypeStruct((N_CHIPS - 1,) + s.shape[1:], s.dtype) for s in sums],
        scratch_shapes=[pltpu.SemaphoreType.DMA((3 * n,)), pltpu.SemaphoreType.DMA((3 * n,))],
    )(*sums)


def _sibling_send_many(halves, name):
    n = len(halves)

    def body(*refs):
        srcs, outs = refs[:n], refs[n:2 * n]
        send_sems, recv_sems = refs[2 * n:]
        x, y, c = _where_am_i()
        cps = [_remote(srcs[p], outs[p], send_sems, recv_sems, p, (x, y, 1 - c)) for p in range(n)]
        for cp in cps:
            cp.start()
        for cp in cps:
            cp.wait()

    return pl.pallas_call(
        body, name=name, in_specs=[HBM_SPEC] * n, out_specs=[HBM_SPEC] * n,
        out_shape=[jax.ShapeDtypeStruct(h.shape, h.dtype) for h in halves],
        scratch_shapes=[pltpu.SemaphoreType.DMA((n,)), pltpu.SemaphoreType.DMA((n,))],
    )(*halves)


def _row_tile(rows, pref=512):
    best = 8
    for t in range(8, pref + 1, 8):
        if rows % t == 0:
            best = t
    return best


def _where_ids():
    x, y, c = _where_am_i()
    return jnp.stack([c, 2 * x + y]).astype(jnp.int32)


RS_ROWS = 256


def _chip_sums_bf16(G, A, ids, name):
    n, _, hr, cols = G.shape
    tr = _row_tile(hr, RS_ROWS)

    def body(ids_ref, g_ref, a_ref, o_ref):
        o_ref[...] = (g_ref[...] + a_ref[...]).astype(BF16)

    return pl.pallas_call(
        body, name=name,
        grid_spec=pltpu.PrefetchScalarGridSpec(
            num_scalar_prefetch=1, grid=(n, hr // tr),
            in_specs=[pl.BlockSpec((None, None, tr, cols), lambda j, i, ids: (j, ids[0], i, 0)),
                      pl.BlockSpec((None, tr, cols), lambda j, i, ids: (j, i, 0))],
            out_specs=pl.BlockSpec((None, tr, cols), lambda j, i, ids: (j, i, 0))),
        out_shape=jax.ShapeDtypeStruct((n, hr, cols), BF16),
        compiler_params=_cp(("parallel", "parallel")),
    )(ids, G, A)


def _total_half(G, A, B, ids, name):
    _, _, hr, cols = G.shape
    tr = _row_tile(hr, RS_ROWS)

    def body(ids_ref, g_ref, a_ref, b_ref, o_ref):
        s = g_ref[...] + a_ref[...]
        for k in range(N_CHIPS - 1):
            s = s + b_ref[k].astype(F32)
        o_ref[...] = s

    return pl.pallas_call(
        body, name=name,
        grid_spec=pltpu.PrefetchScalarGridSpec(
            num_scalar_prefetch=1, grid=(hr // tr,),
            in_specs=[pl.BlockSpec((None, None, tr, cols), lambda i, ids: (ids[1], ids[0], i, 0)),
                      pl.BlockSpec((None, tr, cols), lambda i, ids: (ids[1], i, 0)),
                      pl.BlockSpec((N_CHIPS - 1, tr, cols), lambda i, ids: (0, i, 0))],
            out_specs=pl.BlockSpec((tr, cols), lambda i, ids: (i, 0))),
        out_shape=jax.ShapeDtypeStruct((hr, cols), F32),
        compiler_params=_cp(("parallel",)),
    )(ids, G, A, B)


def _small_allreduce(v, name):
    R, Wd = v.shape

    def body(x_ref, sum_ref, all_ref, send_sems, recv_sems, local_sem):
        x, y, c = _where_am_i()
        me, sib = (x, y, c), (x, y, 1 - c)
        chips = _other_chips(x, y)

        def slot(px, py, pc):
            return all_ref.at[4 * px + 2 * py + pc]

        local = pltpu.make_async_copy(x_ref, slot(*me), local_sem)
        local.start()
        first = [_remote(x_ref, slot(*me), send_sems, recv_sems, 0, sib)]
        first += [_remote(x_ref, slot(*me), send_sems, recv_sems, 1 + k, (cx, cy, c)) for k, (cx, cy) in enumerate(chips)]
        for cp in first:
            cp.start()
        passed = []
        for k, (cx, cy) in enumerate(chips):
            blk = slot(cx, cy, c)
            _remote(blk, blk, send_sems, recv_sems, 1 + k, me).wait_recv()
            fw = _remote(blk, blk, send_sems, recv_sems, 4 + k, sib)
            fw.start()
            passed.append(fw)
        _remote(slot(*sib), slot(*sib), send_sems, recv_sems, 0, me).wait_recv()
        for k, (cx, cy) in enumerate(chips):
            blk = slot(cx, cy, 1 - c)
            _remote(blk, blk, send_sems, recv_sems, 4 + k, me).wait_recv()
        for cp in first + passed:
            cp.wait_send()
        local.wait()
        s = all_ref[0]
        for d in range(1, N_DEV):
            s = s + all_ref[d]
        sum_ref[...] = s

    vm = pl.BlockSpec(memory_space=pltpu.VMEM)
    return pl.pallas_call(
        body, name=name, in_specs=[vm], out_specs=[vm, vm],
        out_shape=[jax.ShapeDtypeStruct((R, Wd), F32), jax.ShapeDtypeStruct((N_DEV, R, Wd), F32)],
        scratch_shapes=[pltpu.SemaphoreType.DMA((7,)), pltpu.SemaphoreType.DMA((7,)), pltpu.SemaphoreType.DMA],
    )(v)[0]


def _adamw(w, g, m, v, name):
    L, R, Wd = w.shape
    tr = _row_tile(R, RS_ROWS)
    c1 = 1.0 - ADAM_B1 ** ADAM_STEP
    c2 = 1.0 - ADAM_B2 ** ADAM_STEP

    def body(w_ref, g_ref, m_ref, v_ref, d_ref, nm_ref, nv_ref):
        gv = g_ref[...]
        nm = ADAM_B1 * m_ref[...] + (1.0 - ADAM_B1) * gv
        nv = ADAM_B2 * v_ref[...] + (1.0 - ADAM_B2) * (gv * gv)
        d_ref[...] = -ADAM_LR * ((nm / c1) / (jnp.sqrt(nv / c2) + ADAM_EPS) + ADAM_WD * w_ref[...])
        nm_ref[...] = nm
        nv_ref[...] = nv

    row = pl.BlockSpec((None, tr, Wd), lambda l, i: (l, i, 0))
    shp = jax.ShapeDtypeStruct((L, R, Wd), F32)
    return pl.pallas_call(
        body, name=name, grid=(L, R // tr), in_specs=[row] * 4, out_specs=[row] * 3, out_shape=[shp] * 3,
        compiler_params=_cp(("parallel", "parallel")),
    )(w, g, m, v)


def _adamw_halves(w, m, v, mine, theirs, ids, name, after=None):
    L, R, Wd = w.shape
    hr = R // 2
    tr = _row_tile(hr, RS_ROWS)
    c1 = 1.0 - ADAM_B1 ** ADAM_STEP
    c2 = 1.0 - ADAM_B2 ** ADAM_STEP
    ordered = [] if after is None else [after]

    def body(ids_ref, w_ref, m_ref, v_ref, *refs):
        g_refs, (g_ref, d_ref, nm_ref, nv_ref) = refs[:2 * L], refs[-4:]
        layer, half = pl.program_id(0), pl.program_id(1)
        own = half == ids_ref[0]
        gv = jnp.where(own, g_refs[0][...], g_refs[L][...])
        for l in range(1, L):
            gv = jnp.where(layer == l, jnp.where(own, g_refs[l][...], g_refs[L + l][...]), gv)
        nm = ADAM_B1 * m_ref[...] + (1.0 - ADAM_B1) * gv
        nv = ADAM_B2 * v_ref[...] + (1.0 - ADAM_B2) * (gv * gv)
        g_ref[...] = gv
        d_ref[...] = -ADAM_LR * ((nm / c1) / (jnp.sqrt(nv / c2) + ADAM_EPS) + ADAM_WD * w_ref[...])
        nm_ref[...] = nm
        nv_ref[...] = nv

    blk = pl.BlockSpec((None, None, tr, Wd), lambda l, h, i, ids: (l, h, i, 0))
    g_blk = pl.BlockSpec((tr, Wd), lambda l, h, i, ids: (i, 0))
    shp = jax.ShapeDtypeStruct((L, 2, hr, Wd), F32)
    outs = pl.pallas_call(
        body, name=name,
        grid_spec=pltpu.PrefetchScalarGridSpec(
            num_scalar_prefetch=1, grid=(L, 2, hr // tr),
            in_specs=[blk] * 3 + [g_blk] * (2 * L)
            + [pl.BlockSpec((8, LANE), lambda l, h, i, ids: (0, 0)) for _ in ordered], out_specs=[blk] * 4),
        out_shape=[shp] * 4,
        compiler_params=_cp(("parallel", "parallel", "parallel")),
    )(ids, *[a.reshape(L, 2, hr, Wd) for a in (w, m, v)], *mine, *theirs, *ordered)
    return tuple(o.reshape(L, R, Wd) for o in outs)


def _two_halves(a):
    cols = a.shape[-1]
    return a.reshape(2, _size(a.shape) // (2 * cols), cols)


FIRST_NEEDED = ("w_in_e",)
LATER_NEEDED = tuple(n for n in BIG if n not in FIRST_NEEDED)


def _gather_weights(P):
    chip = 2 * lax.axis_index("x") + lax.axis_index("y")

    def with_own(landed, own):
        return lax.dynamic_update_slice_in_dim(landed, own[None], chip, axis=0)

    mine = {n: _two_halves(P[n].astype(BF16)) for n in BIG}
    first = _chip_allgather_many([mine[n] for n in FIRST_NEEDED], "ag_first")
    gathered = {n: with_own(g, mine[n]) for n, g in zip(FIRST_NEEDED, first)}
    send_sems, recv_sems, blocks, lands, token = _chip_allgather_start([mine[n] for n in LATER_NEEDED], "ag_start")

    def later(after):
        landed = _chip_allgather_wait(send_sems, recv_sems, blocks, lands, after, "ag_wait")
        g = {n: with_own(a, mine[n]) for n, a in zip(LATER_NEEDED, _chip_allgather_forward(landed, "ag_forward"))}
        w_in_o = g["w_in_o"].reshape(N_CHIPS, D_MODEL, ODD_IN // N_CHIPS)
        return {
            "w_out_e": g["w_out_e"].reshape(D_MODEL, D_MODEL),
            "w_out_o": g["w_out_o"].reshape(D_MODEL, D_MODEL),
            "w_in_o": jnp.pad(jnp.concatenate([w_in_o[j] for j in range(N_CHIPS)], axis=1),
                              ((0, 0), (0, ODD_IN_PAD - ODD_IN))),
            "w_up": g["w_up"],
            "w_down": g["w_down"].transpose(1, 0, 2, 3).reshape(2, FFN_DIM, D_MODEL),
            "w_ple_gate": g["w_ple_gate"].transpose(1, 0, 2, 3).reshape(2, D_MODEL, D_MODEL),
            "w_ple": g["w_ple"].transpose(1, 2, 0, 3).reshape(2, PLE_DIM, D_MODEL),
        }

    small_shapes = [P[n].shape for n, _ in SHARDED_SMALL]
    small = _chip_allgather(_pack([P[n] for n, _ in SHARDED_SMALL], SMALL_W, SMALL_ROWS), "ag_small")
    parts = [_unpack(small[j], small_shapes) for j in range(N_CHIPS)]
    full = {n: jnp.concatenate([parts[j][i] for j in range(N_CHIPS)], axis=ax)
            for i, (n, ax) in enumerate(SHARDED_SMALL)}
    W = {n: P[n] for n in REPLICATED}
    W["pool_w"] = P["pool_w"][0]
    W["final_norm"] = P["final_norm"].reshape(1, D_MODEL)
    W["mix_norm_o"] = full["mix_norm_o"]
    W["conv_qkv_o"] = full["conv_qkv_o"][0]
    W["ffn_conv"] = full["ffn_conv"]
    W["w_in_e"] = gathered["w_in_e"].reshape(N_CHIPS, D_MODEL, IN_E_SHARD)
    return W, token, later


def _chip_major_w_in_o(g):
    shard = ODD_IN // N_CHIPS
    return jnp.stack([g[:, j * shard:(j + 1) * shard] for j in range(N_CHIPS)])


def _reduce_begin(grads, ids, tag, travel_later):
    keys = list(grads)
    pieces = [g.reshape(N_CHIPS, 2, g.shape[1] // 2, g.shape[2]) for g in grads.values()]
    from_sibling = _sibling_swap_many(pieces, f"rs_sibling_swap_{tag}")
    sums = [_chip_sums_bf16(g, a, ids, f"rs_chip_sums_{tag}{i}") for i, (g, a) in enumerate(zip(pieces, from_sibling))]
    state = dict(keys=keys, pieces=pieces, from_sibling=from_sibling, ids=ids, tag=tag, token=None)
    if travel_later:
        state["flight"] = _chip_scatter_start(sums, f"rs_scatter_start_{tag}")
        state["token"] = state["flight"][-1]
    else:
        state["from_chips"] = _chip_scatter_many(sums, f"rs_chip_scatter_{tag}")
    return state


def _reduce_end(state, after=None):
    tag, ids = state["tag"], state["ids"]
    if "flight" in state:
        send_sems, recv_sems, sums, lands, _ = state["flight"]
        from_chips = _chip_scatter_wait(send_sems, recv_sems, sums, lands, after, f"rs_scatter_wait_{tag}")
    else:
        from_chips = state["from_chips"]
    halves = [_total_half(g, a, b, ids, f"rs_total_{tag}{i}")
              for i, (g, a, b) in enumerate(zip(state["pieces"], state["from_sibling"], from_chips))]
    theirs = _sibling_send_many(halves, f"rs_sibling_send_{tag}")
    return {k: (h, t) for k, h, t in zip(state["keys"], halves, theirs)}


def kernel(x, p, mix_norm_e, w_in_e, pool_w, pool_scale, w_out_e, mix_norm_o, w_in_o, conv_qkv_o, a_log_o, dt_bias_o, gdn_norm_o, w_out_o, ffn_norm, w_up, ffn_conv, w_down, ple_norm, w_ple_gate, w_ple, final_norm, loss_target, m_mix_norm_e, m_w_in_e, m_pool_w, m_pool_scale, m_w_out_e, m_mix_norm_o, m_w_in_o, m_conv_qkv_o, m_a_log_o, m_dt_bias_o, m_gdn_norm_o, m_w_out_o, m_ffn_norm, m_w_up, m_ffn_conv, m_w_down, m_ple_norm, m_w_ple_gate, m_w_ple, m_final_norm, v_mix_norm_e, v_w_in_e, v_pool_w, v_pool_scale, v_w_out_e, v_mix_norm_o, v_w_in_o, v_conv_qkv_o, v_a_log_o, v_dt_bias_o, v_gdn_norm_o, v_w_out_o, v_ffn_norm, v_w_up, v_ffn_conv, v_w_down, v_ple_norm, v_w_ple_gate, v_w_ple, v_final_norm):
    args = locals()
    P = {n: args[n] for n in WEIGHT_ORDER}
    M = {n: args["m_" + n] for n in WEIGHT_ORDER}
    V = {n: args["v_" + n] for n in WEIGHT_ORDER}

    W, token, later_weights = _gather_weights(P)
    T = x.shape[1]
    ids = _where_ids()
    early = {}

    def on_grads(stage, grads):
        early[stage] = _reduce_begin(grads, ids, stage, travel_later=True)
        return early[stage]["token"]

    sq, grad_x, G = _local_step(x.reshape(T, D_MODEL), p.reshape(2, T, PLE_DIM), loss_target.reshape(T, D_MODEL), W,
                                token, later_weights, on_grads)
    last = _reduce_begin({("w_in_e", 0): G["w_in_e"], ("w_out_e", 0): G["w_out_e"]}, ids, "first_mixer",
                         travel_later=True)
    reduced = {}
    for state in early.values():
        reduced.update(_reduce_end(state, after=grad_x))
    out = {}

    def adamw(n, after=None):
        halves = [reduced[(n, l)] for l in range(P[n].shape[0])]
        out[n] = _adamw_halves(P[n], M[n], V[n], [h[0] for h in halves], [h[1] for h in halves], ids, f"adamw_{n}",
                               after)

    last_names = ("w_in_e", "w_out_e")
    for n in BIG:
        if n not in last_names:
            adamw(n, after=last["token"])
    reduced.update(_reduce_end(last, after=out["w_up"][1]))
    for n in last_names:
        adamw(n)

    small_full = {n: G[n] for n in REPLICATED}
    small_full["pool_w"] = G["pool_w"][None]
    small_full["final_norm"] = G["final_norm"].reshape(D_MODEL)
    small_full["mix_norm_o"] = G["mix_norm_o"]
    small_full["conv_qkv_o"] = G["conv_qkv_o"][None]
    small_full["ffn_conv"] = G["ffn_conv"]
    small_names = REPLICATED + tuple(n for n, _ in SHARDED_SMALL)
    summed = _small_allreduce(_pack([small_full[n] for n in small_names] + [sq.reshape(1)], SMALL_W, 8), "ar_small")
    *g_list, sq_total = _unpack(summed, [small_full[n].shape for n in small_names] + [(1,)])
    g_small = dict(zip(small_names, g_list))
    chip = 2 * lax.axis_index("x") + lax.axis_index("y")
    for n, ax in SHARDED_SMALL:
        width = P[n].shape[ax]
        g_small[n] = lax.dynamic_slice_in_dim(g_small[n], chip * width, width, axis=ax)

    def pack_small(D):
        return _pack([D[n] for n in small_names], SMALL_W, RS_ROWS)[None]

    g_pack = pack_small(g_small)
    upd = _adamw(pack_small(P), g_pack, pack_small(M), pack_small(V), "adamw_small")
    shapes = [P[n].shape for n in small_names]
    for n, *vals in zip(small_names, *[_unpack(a[0], shapes) for a in (g_pack,) + tuple(upd)]):
        out[n] = tuple(vals)

    loss = (0.5 / D_MODEL) * sq_total[0]
    return (loss, grad_x[None]) + tuple(out[n][i] for i in range(4) for n in WEIGHT_ORDER)
```
